```python
import math, functools
import jax, jax.numpy as jnp
from jax import lax
import numpy as np

D_MODEL = 1024
BATCH = 4
SEQ = 4096
DEPTH = 1

CTX_LEN = 256
GRID_W = 64
D_MIX = D_MODEL
MLSTM_HEADS = 4
MLSTM_HEAD_DIM = D_MODEL // 8
D_MLSTM = MLSTM_HEADS * MLSTM_HEAD_DIM
FOURIER_GROUPS = 8
FOURIER_GROUP_DIM = (D_MIX - D_MLSTM) // FOURIER_GROUPS
D_FOURIER = FOURIER_GROUPS * FOURIER_GROUP_DIM
N_DIRS = 2
N_GATE_COLS = N_DIRS * 2 * MLSTM_HEADS
D_IN_PROJ = 4 * D_MLSTM + N_GATE_COLS + D_FOURIER
SPLITS = (2 * D_MLSTM, 3 * D_MLSTM, 4 * D_MLSTM, 4 * D_MLSTM + N_GATE_COLS)
CHUNK = 128
CONV_K = 3
N_EXPERTS = 64
TOP_K = 8
D_EXPERT = D_MODEL // 4
ROUTED_SCALE = 2.5
EXPERT_BLOCK = 128
DEEPNORM_ALPHA = (2.0 * DEPTH) ** 0.25
DEEPNORM_BETA = (8.0 * DEPTH) ** -0.25
LN_EPS = 1e-5

kernel_name = "hybrid_mlstm_fourier_moe_diffusion_block"


def layer_norm(x):
    xf = x.astype(jnp.float32)
    mu = xf.mean(-1, keepdims=True)
    var = jnp.mean(jnp.square(xf - mu), -1, keepdims=True)
    return ((xf - mu) * lax.rsqrt(var + LN_EPS)).astype(x.dtype)


def layer_norm_affine(x, gain, bias):
    return (layer_norm(x) * gain + bias).astype(x.dtype)


def modulate(u, shift, scale):
    return u * (1.0 + scale) + shift


def conv_grid(u, w, b, rows):
    bsz, n, ch = u.shape
    img = u.reshape(bsz, rows, GRID_W, ch)
    out = lax.conv_general_dilated(img, w[:, :, None, :], window_strides=(1, 1), padding="SAME",
                                   dimension_numbers=("NHWC", "HWIO", "NHWC"), feature_group_count=ch)
    return out.reshape(bsz, n, ch) + b


def conv_seq(u, w, b):
    ch = u.shape[-1]
    out = lax.conv_general_dilated(u, w[CONV_K // 2][:, None, :], window_strides=(1,), padding="SAME",
                                   dimension_numbers=("NWC", "WIO", "NWC"), feature_group_count=ch)
    return out + b


def split_heads(u):
    bsz, n, _ = u.shape
    return u.astype(jnp.float32).reshape(bsz, n, MLSTM_HEADS, MLSTM_HEAD_DIM).transpose(0, 2, 1, 3)


def zero_states(bsz):
    c0 = jnp.zeros((bsz, MLSTM_HEADS, MLSTM_HEAD_DIM, MLSTM_HEAD_DIM), jnp.float32)
    n0 = jnp.zeros((bsz, MLSTM_HEADS, MLSTM_HEAD_DIM), jnp.float32)
    m0 = jnp.zeros((bsz, MLSTM_HEADS), jnp.float32)
    return ((c0, n0, m0), (c0, n0, m0))


def mlstm_chunkwise(q, k, v, log_i, log_f, state):
    bsz, nh, n, dh = q.shape
    nc = n // CHUNK

    def to_chunks(t):
        return jnp.moveaxis(t.reshape(bsz, nh, nc, CHUNK, *t.shape[3:]), 2, 0)

    tri = jnp.tril(jnp.ones((CHUNK, CHUNK), dtype=bool))

    def step(carry, inp):
        c_prev, n_prev, m_prev = carry
        qc, kc, vc, li, lf = inp
        b = jnp.cumsum(lf, axis=-1)
        log_d = jnp.where(tri, b[..., :, None] - b[..., None, :] + li[..., None, :], -jnp.inf)
        m_t = jnp.maximum(b + m_prev[..., None], log_d.max(-1))
        d = jnp.exp(log_d - m_t[..., None])
        inter = jnp.exp(b + m_prev[..., None] - m_t)
        s = jnp.einsum("bhtd,bhsd->bhts", qc, kc) * d
        num = jnp.einsum("bhts,bhsd->bhtd", s, vc) + inter[..., None] * jnp.einsum("bhvd,bhtd->bhtv", c_prev, qc)
        den = s.sum(-1) + inter * jnp.einsum("bhd,bhtd->bht", n_prev, qc)
        h = num / jnp.maximum(jnp.abs(den), jnp.exp(-m_t))[..., None]
        b_last = b[..., -1]
        log_w = b_last[..., None] - b + li
        m_new = jnp.maximum(b_last + m_prev, log_w.max(-1))
        w = jnp.exp(log_w - m_new[..., None])
        decay = jnp.exp(b_last + m_prev - m_new)
        c_new = decay[..., None, None] * c_prev + jnp.einsum("bhs,bhsv,bhsd->bhvd", w, vc, kc)
        n_new = decay[..., None] * n_prev + jnp.einsum("bhs,bhsd->bhd", w, kc)
        return (c_new, n_new, m_new), h

    state, h = lax.scan(step, state, (to_chunks(q), to_chunks(k), to_chunks(v), to_chunks(log_i), to_chunks(log_f)))
    return jnp.moveaxis(h, 0, 2).reshape(bsz, nh, n, dh), state


def mixer_heads(stream, shift, scale, w_in, conv_fn, conv_w, conv_b, gate_b, states):
    proj = modulate(layer_norm(stream), shift, scale) @ w_in
    qk_pre, v, o_pre, g, f_in = jnp.split(proj, SPLITS, axis=-1)
    qk = jax.nn.silu(conv_fn(qk_pre, conv_w, conv_b))
    q, k = jnp.split(qk, 2, axis=-1)
    q = split_heads(q)
    k = split_heads(k) * MLSTM_HEAD_DIM ** -0.5
    v = split_heads(v)
    bsz, n, _ = g.shape
    pre = (g.reshape(bsz, n, N_DIRS, 2, MLSTM_HEADS) + gate_b).astype(jnp.float32)
    pre = jnp.transpose(pre, (2, 3, 0, 4, 1))
    log_i = pre[:, 0]
    log_f = jax.nn.log_sigmoid(pre[:, 1])
    h_fwd, st_fwd = mlstm_chunkwise(q, k, v, log_i[0], log_f[0], states[0])
    rev = lambda t: jnp.flip(t, axis=2)
    h_bwd, st_bwd = mlstm_chunkwise(rev(q), rev(k), rev(v), rev(log_i[1]), rev(log_f[1]), states[1])
    return h_fwd + rev(h_bwd), o_pre, f_in, (st_fwd, st_bwd)


def mlstm_readout(h, o_pre, head_g):
    bsz, nh, n, dh = h.shape
    hn = jnp.transpose(layer_norm(h), (0, 2, 1, 3)).reshape(bsz, n, nh * dh)
    return (hn * head_g * jax.nn.sigmoid(o_pre.astype(jnp.float32))).astype(o_pre.dtype)


def fourier_mix(u):
    bsz, n, _ = u.shape
    grp = u.astype(jnp.float32).reshape(bsz, n, FOURIER_GROUPS, FOURIER_GROUP_DIM)
    spec = jnp.fft.fft2(grp, axes=(1, 3), norm="ortho")
    return jnp.real(spec).reshape(bsz, n, D_FOURIER).astype(u.dtype)


def mixer_residual(stream, h_m, o_pre, f_in, gate, head_g, w_out, ln_g, ln_b):
    mix = jnp.concatenate([mlstm_readout(h_m, o_pre, head_g), fourier_mix(f_in)], axis=-1) @ w_out
    return layer_norm_affine(DEEPNORM_ALPHA * stream + gate * mix, ln_g, ln_b)


def moe_ffn(h, w_router, router_bias, w_expert_gu, w_expert_down, w_shared_gu, w_shared_down):
    bsz, n, d = h.shape
    t = h.reshape(bsz * n, d)
    n_tok = t.shape[0]
    n_assign = n_tok * TOP_K
    scores = jax.nn.sigmoid((t @ w_router).astype(jnp.float32))
    _, idx = lax.top_k(scores + router_bias.astype(jnp.float32), TOP_K)
    sel = jnp.take_along_axis(scores, idx, axis=-1)
    gate_w = ROUTED_SCALE * sel / sel.sum(-1, keepdims=True)
    flat_e = idx.reshape(-1)
    order = jnp.argsort(flat_e)
    sorted_e = flat_e[order]
    sizes = jnp.bincount(flat_e, length=N_EXPERTS)
    padded = (sizes + EXPERT_BLOCK - 1) // EXPERT_BLOCK * EXPERT_BLOCK
    start = jnp.cumsum(sizes) - sizes
    pad_end = jnp.cumsum(padded)
    pad_start = pad_end - padded
    dest = pad_start[sorted_e] + jnp.arange(n_assign) - start[sorted_e]
    n_rows = (n_assign + EXPERT_BLOCK - 1) // EXPERT_BLOCK * EXPERT_BLOCK + N_EXPERTS * EXPERT_BLOCK
    n_blocks = n_rows // EXPERT_BLOCK
    row_tok = jnp.zeros((n_rows,), jnp.int32).at[dest].set((order // TOP_K).astype(jnp.int32))
    row_w = jnp.zeros((n_rows,), jnp.float32).at[dest].set(gate_w.reshape(-1)[order])
    block_e = jnp.minimum(jnp.searchsorted(pad_end, jnp.arange(n_blocks) * EXPERT_BLOCK, side="right"), N_EXPERTS - 1)
    xb = t[row_tok].reshape(n_blocks, EXPERT_BLOCK, d)

    def expert_block(args):
        xblk, e = args
        g, u = jnp.split(xblk @ w_expert_gu[e], 2, axis=-1)
        return (jax.nn.silu(g) * u) @ w_expert_down[e]

    yb = lax.map(expert_block, (xb, block_e)).reshape(n_rows, d)
    routed = jax.ops.segment_sum(yb * row_w[:, None].astype(yb.dtype), row_tok, num_segments=n_tok)
    sg, su = jnp.split(t @ w_shared_gu, 2, axis=-1)
    shared = (jax.nn.silu(sg) * su) @ w_shared_down
    return (routed + shared).reshape(bsz, n, d)


def channel_residual(stream, shift, scale, gate, w_router, router_bias, w_expert_gu, w_expert_down,
                     w_shared_gu, w_shared_down, ln_g, ln_b):
    y = moe_ffn(modulate(layer_norm(stream), shift, scale), w_router, router_bias,
                w_expert_gu, w_expert_down, w_shared_gu, w_shared_down)
    return layer_norm_affine(DEEPNORM_ALPHA * stream + gate * y, ln_g, ln_b)


def setup_inputs(seed: int = 0) -> dict:
    key = jax.random.key(seed)
    ks = jax.random.split(key, 24)

    def nrm(k, shape, scale):
        return scale * jax.random.normal(k, shape, jnp.float32)

    f_bias = jnp.linspace(3.0, 6.0, MLSTM_HEADS, dtype=jnp.float32)
    gate_b = jnp.stack([nrm(ks[9], (DEPTH, N_DIRS, MLSTM_HEADS), 0.1),
                        f_bias + nrm(ks[10], (DEPTH, N_DIRS, MLSTM_HEADS), 0.1)], axis=2)
    return {
        "x": nrm(ks[0], (BATCH, SEQ, D_MODEL), 1.0),
        "c": nrm(ks[1], (BATCH, D_MODEL), 1.0),
        "ctx": nrm(ks[2], (BATCH, CTX_LEN, D_MODEL), 1.0),
        "c_ctx": nrm(ks[3], (D_MODEL,), 1.0),
        "w_ada": nrm(ks[4], (DEPTH, D_MODEL, 6 * D_MODEL), 0.5 * D_MODEL ** -0.5),
        "b_ada": nrm(ks[5], (DEPTH, 6 * D_MODEL), 0.02),
        "w_in": nrm(ks[6], (DEPTH, D_MODEL, D_IN_PROJ), D_MODEL ** -0.5),
        "conv_w": nrm(ks[7], (DEPTH, CONV_K, CONV_K, 2 * D_MLSTM), 1.0 / CONV_K),
        "conv_b": nrm(ks[8], (DEPTH, 2 * D_MLSTM), 0.02),
        "gate_b": gate_b,
        "head_g": 1.0 + nrm(ks[11], (DEPTH, D_MLSTM), 0.02),
        "w_out": nrm(ks[12], (DEPTH, D_MIX, D_MODEL), DEEPNORM_BETA * D_MIX ** -0.5),
        "ln1_g": 1.0 + nrm(ks[13], (DEPTH, D_MODEL), 0.02),
        "ln1_b": nrm(ks[14], (DEPTH, D_MODEL), 0.02),
        "w_router": nrm(ks[15], (DEPTH, D_MODEL, N_EXPERTS), D_MODEL ** -0.5),
        "router_bias": nrm(ks[16], (DEPTH, N_EXPERTS), 0.01),
        "w_expert_gu": nrm(ks[17], (DEPTH, N_EXPERTS, D_MODEL, 2 * D_EXPERT), D_MODEL ** -0.5),
        "w_expert_down": nrm(ks[18], (DEPTH, N_EXPERTS, D_EXPERT, D_MODEL), DEEPNORM_BETA * D_EXPERT ** -0.5),
        "w_shared_gu": nrm(ks[19], (DEPTH, D_MODEL, 2 * D_EXPERT), D_MODEL ** -0.5),
        "w_shared_down": nrm(ks[20], (DEPTH, D_EXPERT, D_MODEL), DEEPNORM_BETA * D_EXPERT ** -0.5),
        "ln2_g": 1.0 + nrm(ks[21], (DEPTH, D_MODEL), 0.02),
        "ln2_b": nrm(ks[22], (DEPTH, D_MODEL), 0.02),
    }


def reference(x, c, ctx, c_ctx, w_ada, b_ada, w_in, conv_w, conv_b, gate_b, head_g, w_out, ln1_g, ln1_b,
              w_router, router_bias, w_expert_gu, w_expert_down, w_shared_gu, w_shared_down, ln2_g, ln2_b):
    rows = x.shape[1] // GRID_W
    latent_conv = functools.partial(conv_grid, rows=rows)
    for l in range(DEPTH):
        mod_x = [m[:, None, :] for m in jnp.split(jax.nn.silu(c) @ w_ada[l] + b_ada[l], 6, axis=-1)]
        mod_c = jnp.split(jax.nn.silu(c_ctx) @ w_ada[l] + b_ada[l], 6, axis=-1)
        h_c, o_c, f_c, ctx_states = mixer_heads(ctx, mod_c[0], mod_c[1], w_in[l], conv_seq, conv_w[l], conv_b[l],
                                                gate_b[l], zero_states(ctx.shape[0]))
        h_x, o_x, f_x, _ = mixer_heads(x, mod_x[0], mod_x[1], w_in[l], latent_conv, conv_w[l], conv_b[l],
                                       gate_b[l], ctx_states)
        x = mixer_residual(x, h_x, o_x, f_x, mod_x[2], head_g[l], w_out[l], ln1_g[l], ln1_b[l])
        x = channel_residual(x, mod_x[3], mod_x[4], mod_x[5], w_router[l], router_bias[l], w_expert_gu[l],
                             w_expert_down[l], w_shared_gu[l], w_shared_down[l], ln2_g[l], ln2_b[l])
        if l + 1 < DEPTH:
            ctx = mixer_residual(ctx, h_c, o_c, f_c, mod_c[2], head_g[l], w_out[l], ln1_g[l], ln1_b[l])
            ctx = channel_residual(ctx, mod_c[3], mod_c[4], mod_c[5], w_router[l], router_bias[l], w_expert_gu[l],
                                   w_expert_down[l], w_shared_gu[l], w_shared_down[l], ln2_g[l], ln2_b[l])
    return x
```

```python
import functools
import math

import numpy as np
import jax
import jax.numpy as jnp
from jax import lax
from jax.experimental import pallas as pl
from jax.experimental.pallas import tpu as pltpu

F32 = jnp.float32
BF16 = jnp.bfloat16
HIGHEST = lax.Precision.HIGHEST

D_MODEL = 1024
GRID_W = 64
N_HEADS = 4
D_HEAD = 128
D_MLSTM = N_HEADS * D_HEAD
D_FOURIER = 512
FOURIER_GROUP = 64
N_GATE_COLS = 16
CHUNK = 128
N_EXPERTS = 64
TOP_K = 8
D_EXPERT = 256
ROUTED_SCALE = 2.5
DEEPNORM_ALPHA = 2.0 ** 0.25
LN_EPS = 1e-5

LANES = 128
SUBLANES = 8
VMEM_LIMIT = 48 * 1024 * 1024

COL_QK = 0
COL_V = 2 * D_MLSTM
COL_O = 3 * D_MLSTM
COL_F = 4 * D_MLSTM
COL_G = COL_F + D_FOURIER
PROJ_W = COL_G + LANES

TM = 512
ROW_BLK = 256
TM_MOVE = 128
CONV_PAD = 72
CONV_RB = 256
DFT_R = 64


def _cparams(sem, vmem=VMEM_LIMIT):
    return pltpu.CompilerParams(dimension_semantics=sem, vmem_limit_bytes=vmem)


def _layer_norm(x):
    mu = jnp.mean(x, axis=-1, keepdims=True)
    xc = x - mu
    var = jnp.mean(xc * xc, axis=-1, keepdims=True)
    return xc * lax.rsqrt(var + LN_EPS)


def _silu(x):
    return x * jax.nn.sigmoid(x)


def _ada_kernel(c_ref, w_ref, b_ref, o_ref):
    s = _silu(c_ref[...])
    o_ref[...] = jnp.dot(s, w_ref[...], precision=HIGHEST, preferred_element_type=F32) + b_ref[...]


def _ada(cvec, w_ada, b_ada):
    rows, d = cvec.shape
    n_out = w_ada.shape[1]
    tn = 1024
    return pl.pallas_call(
        _ada_kernel,
        out_shape=jax.ShapeDtypeStruct((rows, n_out), F32),
        grid=(n_out // tn,),
        in_specs=[pl.BlockSpec((rows, d), lambda j: (0, 0)),
                  pl.BlockSpec((d, tn), lambda j: (0, j)),
                  pl.BlockSpec((1, tn), lambda j: (0, j))],
        out_specs=pl.BlockSpec((rows, tn), lambda j: (0, j)),
        compiler_params=_cparams(("arbitrary",)),
        name="ada",
    )(cvec, w_ada, b_ada)


def _inproj_kernel(x_ref, sh_ref, sc_ref, w_ref, y_ref):
    u = _layer_norm(x_ref[...]) * (1.0 + sc_ref[0]) + sh_ref[0]
    y_ref[...] = jnp.dot(u.astype(BF16), w_ref[...], preferred_element_type=F32)


def _inproj(x2d, shift, scale, w, n_per_batch):
    t, d = x2d.shape
    tm = min(TM, n_per_batch)
    per = n_per_batch // tm
    return pl.pallas_call(
        _inproj_kernel,
        out_shape=jax.ShapeDtypeStruct((t, PROJ_W), F32),
        grid=(t // tm,),
        in_specs=[pl.BlockSpec((tm, d), lambda i: (i, 0)),
                  pl.BlockSpec((1, 1, d), lambda i: (i // per, 0, 0)),
                  pl.BlockSpec((1, 1, d), lambda i: (i // per, 0, 0)),
                  pl.BlockSpec((d, PROJ_W), lambda i: (0, 0))],
        out_specs=pl.BlockSpec((tm, PROJ_W), lambda i: (i, 0)),
        compiler_params=_cparams(("arbitrary",)),
        name="inproj",
    )(x2d, shift, scale, w)


def _conv_kernel(y_ref, w_ref, b_ref, o_ref, pad_ref, *, n, grid2d):
    zeros = jnp.zeros((CONV_PAD, LANES), F32)
    pad_ref[0:CONV_PAD, :] = zeros
    pad_ref[CONV_PAD + n:CONV_PAD + n + CONV_PAD, :] = zeros
    pad_ref[CONV_PAD:CONV_PAD + n, :] = y_ref[0]
    k_scale = jnp.where(pl.program_id(1) >= N_HEADS, D_HEAD ** -0.5, 1.0).astype(F32)
    bias = b_ref[...]
    rb = min(CONV_RB, n)
    col = lax.broadcasted_iota(jnp.int32, (rb, LANES), 0) % GRID_W
    not_first = col >= 1
    not_last = col <= GRID_W - 2
    row_taps = (0, 1, 2) if grid2d else (1,)
    for blk in range(n // rb):
        r0 = CONV_PAD + blk * rb
        acc = jnp.zeros((rb, LANES), F32)
        for dj in range(3):
            part = jnp.zeros((rb, LANES), F32)
            for di in row_taps:
                off = (di - 1) * GRID_W + (dj - 1)
                tap = w_ref[di * 3 + dj:di * 3 + dj + 1, :]
                part = part + tap * pad_ref[r0 + off:r0 + off + rb, :]
            if grid2d and dj == 0:
                part = jnp.where(not_first, part, 0.0)
            if grid2d and dj == 2:
                part = jnp.where(not_last, part, 0.0)
            acc = acc + part
        val = _silu(acc + bias) * k_scale
        o_ref[0, blk * rb:(blk + 1) * rb, :] = val.astype(o_ref.dtype)


def _conv(y3d, conv_w9, conv_b, grid2d):
    b, n, _ = y3d.shape
    n_ct = 2 * D_MLSTM // LANES
    return pl.pallas_call(
        functools.partial(_conv_kernel, n=n, grid2d=grid2d),
        out_shape=jax.ShapeDtypeStruct((b, n, 2 * D_MLSTM), BF16),
        grid=(b, n_ct),
        in_specs=[pl.BlockSpec((1, n, LANES), lambda i, c: (i, 0, c)),
                  pl.BlockSpec((9, LANES), lambda i, c: (0, c)),
                  pl.BlockSpec((1, LANES), lambda i, c: (0, c))],
        out_specs=pl.BlockSpec((1, n, LANES), lambda i, c: (i, 0, c)),
        scratch_shapes=[pltpu.VMEM((n + 2 * CONV_PAD, LANES), F32)],
        compiler_params=_cparams(("arbitrary", "arbitrary")),
        name="conv2d" if grid2d else "conv1d",
    )(y3d, conv_w9, conv_b)


def _gates_kernel(g_ref, gb_ref, cv_ref, rv_ref, *, nc):
    row = lax.broadcasted_iota(jnp.int32, (CHUNK, CHUNK), 0)
    col = lax.broadcasted_iota(jnp.int32, (CHUNK, CHUNK), 1)
    lower = col <= row
    upper = col >= row
    tri_l = lower.astype(F32)
    tri_u = upper.astype(F32)
    sub = lax.broadcasted_iota(jnp.int32, (SUBLANES, LANES), 0)
    neg_inf = jnp.float32(-jnp.inf)

    def body(c, carry):
        r0 = pl.multiple_of(c * CHUNK, CHUNK)
        pre = g_ref[0, pl.ds(r0, CHUNK), :] + gb_ref[...]
        lf = -(jnp.maximum(-pre, 0.0) + jnp.log1p(jnp.exp(-jnp.abs(pre))))
        b_f = jnp.dot(tri_l, lf, precision=HIGHEST, preferred_element_type=F32)
        b_b = jnp.dot(tri_u, lf, precision=HIGHEST, preferred_element_type=F32)
        pre_t = pre.T
        b_f_t = b_f.T
        b_b_t = b_b.T
        for h in range(N_HEADS):
            i_f, f_f, i_b, f_b = h, N_HEADS + h, 2 * N_HEADS + h, 3 * N_HEADS + h
            bf = b_f[:, f_f:f_f + 1]
            bb = b_b[:, f_b:f_b + 1]
            af = pre[:, i_f:i_f + 1] - bf
            ab = pre[:, i_b:i_b + 1] - bb
            af_row = pre_t[i_f:i_f + 1, :] - b_f_t[f_f:f_f + 1, :]
            ab_row = pre_t[i_b:i_b + 1, :] - b_b_t[f_b:f_b + 1, :]
            cmf = jnp.max(jnp.where(lower, af_row, neg_inf), axis=-1, keepdims=True)
            cmb = jnp.max(jnp.where(upper, ab_row, neg_inf), axis=-1, keepdims=True)
            cv = jnp.where(col == 0, af, 0.0)
            cv = jnp.where(col == 1, cmf, cv)
            cv = jnp.where(col == 2, bf, cv)
            cv = jnp.where(col == 3, ab, cv)
            cv = jnp.where(col == 4, cmb, cv)
            cv = jnp.where(col == 5, bb, cv)
            cv_ref[0, h, pl.ds(r0, CHUNK), :] = cv
            rv = jnp.where(sub == 0, af_row, jnp.where(sub == 1, ab_row, 0.0))
            rv_ref[0, h, pl.ds(c, 1), :, :] = rv[None]
        return carry

    lax.fori_loop(0, nc, body, 0)


def _gates(y3d, gate_b_row):
    b, n, _ = y3d.shape
    nc = n // CHUNK
    return pl.pallas_call(
        functools.partial(_gates_kernel, nc=nc),
        out_shape=(jax.ShapeDtypeStruct((b, N_HEADS, n, LANES), F32),
                   jax.ShapeDtypeStruct((b, N_HEADS, nc, SUBLANES, LANES), F32)),
        grid=(b,),
        in_specs=[pl.BlockSpec((1, n, LANES), lambda i: (i, 0, COL_G // LANES)),
                  pl.BlockSpec((1, LANES), lambda i: (0, 0))],
        out_specs=(pl.BlockSpec((1, N_HEADS, n, LANES), lambda i: (i, 0, 0, 0)),
                   pl.BlockSpec((1, N_HEADS, nc, SUBLANES, LANES), lambda i: (i, 0, 0, 0, 0))),
        compiler_params=_cparams(("arbitrary",)),
        name="gates",
    )(y3d, gate_b_row)


def _mlstm_kernel(q_ref, k_ref, v_ref, o_ref, cv_ref, rv_ref, c0_ref, m0_ref, hg_ref,
                  out_ref, cfin_ref, mfin_ref, hf_ref, hb_ref, cst_ref, mst_ref, *, nc):
    cst_ref[...] = c0_ref[0, 0]
    mst_ref[...] = m0_ref[0, 0]
    row = lax.broadcasted_iota(jnp.int32, (CHUNK, CHUNK), 0)
    col = lax.broadcasted_iota(jnp.int32, (CHUNK, CHUNK), 1)
    ones_col = (col == 0).astype(F32)

    def one_chunk(c, d, mask, last, h_ref):
        r0 = pl.multiple_of(c * CHUNK, CHUNK)
        q = q_ref[0, pl.ds(r0, CHUNK), :]
        k = k_ref[0, pl.ds(r0, CHUNK), :]
        v_aug = jnp.concatenate([v_ref[0, pl.ds(r0, CHUNK), :], ones_col], axis=1)
        cv = cv_ref[0, 0, pl.ds(r0, CHUNK), :]
        a_col = cv[:, 3 * d:3 * d + 1]
        cm_col = cv[:, 3 * d + 1:3 * d + 2]
        b_col = cv[:, 3 * d + 2:3 * d + 3]
        a_row = rv_ref[0, 0, pl.ds(c, 1), :, :][0, d:d + 1, :]
        m_prev = mst_ref[d, 0:1, 0:1]
        m_col = jnp.maximum(m_prev, cm_col)
        dmat = jnp.where(mask, jnp.exp(a_row - m_col), 0.0)
        s = lax.dot_general(q, k, (((1,), (1,)), ((), ())), preferred_element_type=F32) * dmat
        ct = cst_ref[d]
        h_aug = (jnp.dot(s.astype(BF16), v_aug.astype(BF16), preferred_element_type=F32)
                 + jnp.exp(m_prev - m_col)
                 * jnp.dot(q, ct.astype(BF16), preferred_element_type=F32))
        den = h_aug[:, D_HEAD:D_HEAD + 1]
        floor = jnp.exp(-b_col - m_col)
        h_ref[pl.ds(r0, CHUNK), :] = h_aug[:, :D_HEAD] / jnp.maximum(jnp.abs(den), floor)
        m_last = m_col[last:last + 1, :]
        m_new = b_col[last:last + 1, :] + m_last
        wv = (jnp.exp(a_col - m_last) * v_aug).astype(BF16)
        cst_ref[d] = (jnp.exp(m_prev - m_last) * ct
                      + lax.dot_general(k, wv, (((0,), (0,)), ((), ())), preferred_element_type=F32))
        mst_ref[d] = jnp.broadcast_to(m_new, (SUBLANES, LANES))

    def body(c, carry):
        one_chunk(c, 0, col <= row, CHUNK - 1, hf_ref)
        one_chunk(nc - 1 - c, 1, col >= row, 0, hb_ref)
        return carry

    lax.fori_loop(0, nc, body, 0)
    cfin_ref[0, 0] = cst_ref[...]
    mfin_ref[0, 0] = mst_ref[...]
    hn = _layer_norm(hf_ref[...] + hb_ref[...])
    out_ref[0] = (hn * hg_ref[...] * jax.nn.sigmoid(o_ref[0])).astype(out_ref.dtype)


def _mlstm(qk, y3d, colv, rowv, c0, m0, head_g):
    b, n, _ = qk.shape
    nc = n // CHUNK
    v_blk = COL_V // LANES
    o_blk = COL_O // LANES
    tok = lambda off: pl.BlockSpec((1, n, LANES), lambda i, h: (i, 0, off + h))
    return pl.pallas_call(
        functools.partial(_mlstm_kernel, nc=nc),
        out_shape=(jax.ShapeDtypeStruct((b, n, D_MLSTM), BF16),
                   jax.ShapeDtypeStruct((b, N_HEADS, 2, D_HEAD, 2 * D_HEAD), F32),
                   jax.ShapeDtypeStruct((b, N_HEADS, 2, SUBLANES, LANES), F32)),
        grid=(b, N_HEADS),
        in_specs=[tok(0), tok(N_HEADS), tok(v_blk), tok(o_blk),
                  pl.BlockSpec((1, 1, n, LANES), lambda i, h: (i, h, 0, 0)),
                  pl.BlockSpec((1, 1, nc, SUBLANES, LANES), lambda i, h: (i, h, 0, 0, 0)),
                  pl.BlockSpec((1, 1, 2, D_HEAD, 2 * D_HEAD), lambda i, h: (i, h, 0, 0, 0)),
                  pl.BlockSpec((1, 1, 2, SUBLANES, LANES), lambda i, h: (i, h, 0, 0, 0)),
                  pl.BlockSpec((1, LANES), lambda i, h: (0, h))],
        out_specs=(pl.BlockSpec((1, n, LANES), lambda i, h: (i, 0, h)),
                   pl.BlockSpec((1, 1, 2, D_HEAD, 2 * D_HEAD), lambda i, h: (i, h, 0, 0, 0)),
                   pl.BlockSpec((1, 1, 2, SUBLANES, LANES), lambda i, h: (i, h, 0, 0, 0))),
        scratch_shapes=[pltpu.VMEM((n, LANES), F32), pltpu.VMEM((n, LANES), F32),
                        pltpu.VMEM((2, D_HEAD, 2 * D_HEAD), F32),
                        pltpu.VMEM((2, SUBLANES, LANES), F32)],
        compiler_params=_cparams(("arbitrary", "arbitrary")),
        name="mlstm",
    )(qk, qk, y3d, y3d, colv, rowv, c0, m0, head_g)


def _dft_tables():
    r = DFT_R
    idx = np.arange(r)
    cg = np.cos(2 * np.pi * np.outer(idx, idx) / FOURIER_GROUP)
    sg = np.sin(2 * np.pi * np.outer(idx, idx) / FOURIER_GROUP)
    n_grp = D_FOURIER // FOURIER_GROUP
    bdc = np.kron(np.eye(n_grp), cg) / 8.0
    bds = np.kron(np.eye(n_grp), sg) / 8.0
    k1 = idx[:, None]
    a = idx[None, :]
    m1 = np.zeros((r, 2 * r, 2 * r))
    for b2 in range(r):
        th = 2 * np.pi * k1 * (r * a + b2) / (r * r)
        ec, es = np.cos(th), np.sin(th)
        m1[b2] = np.block([[ec, -es], [es, ec]]) / 8.0
    th2 = 2 * np.pi * np.outer(idx, idx) / r
    m2 = np.concatenate([np.cos(th2), -np.sin(th2)], axis=1) / 8.0
    as_bf16 = lambda t: jnp.asarray(t, dtype=F32).astype(BF16)
    return as_bf16(bdc), as_bf16(bds), as_bf16(m1), as_bf16(m2)


FOUR_BB = 8


def _four1_kernel(f_ref, bdc_ref, bds_ref, m1_ref, z_ref):
    for j in range(FOUR_BB):
        fb = f_ref[0, j].astype(BF16)
        p = jnp.dot(fb, bdc_ref[...], preferred_element_type=F32)
        q = jnp.dot(fb, bds_ref[...], preferred_element_type=F32)
        x2 = jnp.concatenate([p, q], axis=0).astype(BF16)
        z_ref[0, j] = jnp.dot(m1_ref[j], x2, preferred_element_type=F32).astype(z_ref.dtype)


def _four2_kernel(z_ref, m2_ref, o_ref):
    for j in range(FOUR_BB):
        o_ref[0, j] = jnp.dot(m2_ref[...], z_ref[0, j], preferred_element_type=F32)


def _fourier(f_t, tables):
    bdc, bds, m1, m2 = tables
    b = f_t.shape[0]
    r = DFT_R
    nb = r // FOUR_BB
    z = pl.pallas_call(
        _four1_kernel,
        out_shape=jax.ShapeDtypeStruct((b, r, 2 * r, D_FOURIER), BF16),
        grid=(b, nb),
        in_specs=[pl.BlockSpec((1, FOUR_BB, r, D_FOURIER), lambda i, j: (i, j, 0, 0)),
                  pl.BlockSpec((D_FOURIER, D_FOURIER), lambda i, j: (0, 0)),
                  pl.BlockSpec((D_FOURIER, D_FOURIER), lambda i, j: (0, 0)),
                  pl.BlockSpec((FOUR_BB, 2 * r, 2 * r), lambda i, j: (j, 0, 0))],
        out_specs=pl.BlockSpec((1, FOUR_BB, 2 * r, D_FOURIER), lambda i, j: (i, j, 0, 0)),
        compiler_params=_cparams(("arbitrary", "arbitrary")),
        name="four1",
    )(f_t, bdc, bds, m1)
    z_t = z.reshape(b, r, 2, r, D_FOURIER).transpose(0, 3, 2, 1, 4).reshape(b, r, 2 * r, D_FOURIER)
    return pl.pallas_call(
        _four2_kernel,
        out_shape=jax.ShapeDtypeStruct((b, r, r, D_FOURIER), F32),
        grid=(b, nb),
        in_specs=[pl.BlockSpec((1, FOUR_BB, 2 * r, D_FOURIER), lambda i, j: (i, j, 0, 0)),
                  pl.BlockSpec((r, 2 * r), lambda i, j: (0, 0))],
        out_specs=pl.BlockSpec((1, FOUR_BB, r, D_FOURIER), lambda i, j: (i, j, 0, 0)),
        compiler_params=_cparams(("arbitrary", "arbitrary")),
        name="four2",
    )(z_t, m2)


def _outproj_kernel(x_ref, mr_ref, fo_ref, wm_ref, wf_ref, g1_ref, lg_ref, lb_ref,
                    sh_ref, sc_ref, wr_ref, x1_ref, xm_ref, s_ref):
    mix = (jnp.dot(mr_ref[...], wm_ref[...], preferred_element_type=F32)
           + jnp.dot(fo_ref[...].astype(BF16), wf_ref[...], preferred_element_type=F32))
    x1 = _layer_norm(DEEPNORM_ALPHA * x_ref[...] + g1_ref[0] * mix) * lg_ref[...] + lb_ref[...]
    x1_ref[...] = x1
    xm = _layer_norm(x1) * (1.0 + sc_ref[0]) + sh_ref[0]
    xm_ref[...] = xm
    logits = jnp.dot(xm, wr_ref[...], precision=HIGHEST, preferred_element_type=F32)
    s_ref[...] = jax.nn.sigmoid(logits)


def _outproj(x2d, mread, four, w_m, w_f, gate1, ln_g, ln_b, shift2, scale2, w_router, n_per_batch):
    t, d = x2d.shape
    per = n_per_batch // TM
    tokd = lambda w: pl.BlockSpec((TM, w), lambda i: (i, 0))
    full = lambda r, c: pl.BlockSpec((r, c), lambda i: (0, 0))
    mod = pl.BlockSpec((1, 1, d), lambda i: (i // per, 0, 0))
    return pl.pallas_call(
        _outproj_kernel,
        out_shape=(jax.ShapeDtypeStruct((t, d), F32),
                   jax.ShapeDtypeStruct((t, d), F32),
                   jax.ShapeDtypeStruct((t, LANES), F32)),
        grid=(t // TM,),
        in_specs=[tokd(d), tokd(D_MLSTM), tokd(D_FOURIER), full(D_MLSTM, d), full(D_FOURIER, d),
                  mod, full(1, d), full(1, d), mod, mod, full(d, LANES)],
        out_specs=(tokd(d), tokd(d), tokd(LANES)),
        compiler_params=_cparams(("arbitrary",)),
        name="outproj",
    )(x2d, mread, four, w_m, w_f, gate1, ln_g, ln_b, shift2, scale2, w_router)


def _route_kernel(s_ref, b_ref, idx_ref, gw_ref, rank_ref, cnt_ref, carry_ref):
    @pl.when(pl.program_id(0) == 0)
    def _():
        carry_ref[...] = jnp.zeros_like(carry_ref)

    tm = s_ref.shape[0]
    s = s_ref[...]
    lane_i = lax.broadcasted_iota(jnp.int32, (tm, LANES), 1)
    lane = lane_i.astype(F32)
    neg_inf = jnp.float32(-jnp.inf)
    sb = jnp.where(lane_i < N_EXPERTS, s + b_ref[...], neg_inf)
    picks, sels = [], []
    taken = jnp.zeros((tm, LANES), F32)
    for _ in range(TOP_K):
        best = jnp.max(sb, axis=-1, keepdims=True)
        pick = jnp.min(jnp.where(sb == best, lane, float(LANES)), axis=-1, keepdims=True)
        onehot = lane == pick
        sels.append(jnp.sum(jnp.where(onehot, s, 0.0), axis=-1, keepdims=True))
        picks.append(pick)
        sb = jnp.where(onehot, neg_inf, sb)
        taken = taken + onehot.astype(F32)
    r_i = lax.broadcasted_iota(jnp.int32, (tm, tm), 0)
    c_i = lax.broadcasted_iota(jnp.int32, (tm, tm), 1)
    before = (c_i < r_i).astype(BF16)
    rank_all = carry_ref[0:1, :] + jnp.dot(before, taken.astype(BF16), preferred_element_type=F32)
    carry_ref[...] = carry_ref[...] + jnp.sum(taken, axis=0, keepdims=True)
    cnt_ref[...] = carry_ref[...]
    total = sels[0]
    for v in sels[1:]:
        total = total + v
    k_i = lax.broadcasted_iota(jnp.int32, (tm, TOP_K), 1)
    idx_out = jnp.zeros((tm, TOP_K), F32)
    gw_out = jnp.zeros((tm, TOP_K), F32)
    rank_out = jnp.zeros((tm, TOP_K), F32)
    for j in range(TOP_K):
        rank_j = jnp.sum(jnp.where(lane == picks[j], rank_all, 0.0), axis=-1, keepdims=True)
        idx_out = jnp.where(k_i == j, picks[j], idx_out)
        gw_out = jnp.where(k_i == j, ROUTED_SCALE * sels[j] / total, gw_out)
        rank_out = jnp.where(k_i == j, rank_j, rank_out)
    idx_ref[...] = idx_out.astype(jnp.int32)
    gw_ref[...] = gw_out
    rank_ref[...] = rank_out.astype(jnp.int32)


def _route(scores, bias_row):
    t = scores.shape[0]
    tok = lambda w: pl.BlockSpec((TM, w), lambda i: (i, 0))
    return pl.pallas_call(
        _route_kernel,
        out_shape=(jax.ShapeDtypeStruct((t, TOP_K), jnp.int32),
                   jax.ShapeDtypeStruct((t, TOP_K), F32),
                   jax.ShapeDtypeStruct((t, TOP_K), jnp.int32),
                   jax.ShapeDtypeStruct((SUBLANES, LANES), F32)),
        grid=(t // TM,),
        in_specs=[tok(LANES), pl.BlockSpec((1, LANES), lambda i: (0, 0))],
        out_specs=(tok(TOP_K), tok(TOP_K), tok(TOP_K),
                   pl.BlockSpec((SUBLANES, LANES), lambda i: (0, 0))),
        scratch_shapes=[pltpu.VMEM((SUBLANES, LANES), F32)],
        compiler_params=_cparams(("arbitrary",)),
        name="route",
    )(scores, bias_row)


ASSIGN_PER_TILE = TM_MOVE * TOP_K
SMEM_ROWS = ASSIGN_PER_TILE // LANES


def _dest_row(ps_ref, idx_ref, rank_ref, flat):
    hi = flat // LANES
    lo = flat % LANES
    return ps_ref[idx_ref[hi, lo]] + rank_ref[hi, lo]


def _dispatch_kernel(ps_ref, idx_ref, rank_ref, xm_ref, xs_in_ref, xs_ref, sem):
    del xs_in_ref

    def row_copy(r, dest):
        return pltpu.make_async_copy(xm_ref.at[pl.ds(r, 1), :], xs_ref.at[pl.ds(dest, 1), :], sem)

    def issue(r, carry):
        for j in range(TOP_K):
            row_copy(r, _dest_row(ps_ref, idx_ref, rank_ref, r * TOP_K + j)).start()
        return carry

    lax.fori_loop(0, TM_MOVE, issue, 0)

    def drain(r, carry):
        for j in range(TOP_K):
            row_copy(r, 0).wait()
        return carry

    lax.fori_loop(0, TM_MOVE, drain, 0)


def _dispatch(pad_start, idx_flat, rank_flat, xm, n_rows):
    t, d = xm.shape
    smem = lambda: pl.BlockSpec((SMEM_ROWS, LANES), lambda i, ps: (i, 0), memory_space=pltpu.SMEM)
    grid_spec = pltpu.PrefetchScalarGridSpec(
        num_scalar_prefetch=1,
        grid=(t // TM_MOVE,),
        in_specs=[smem(), smem(),
                  pl.BlockSpec((TM_MOVE, d), lambda i, ps: (i, 0)),
                  pl.BlockSpec(memory_space=pl.ANY)],
        out_specs=pl.BlockSpec(memory_space=pl.ANY),
        scratch_shapes=[pltpu.SemaphoreType.DMA(())],
    )
    return pl.pallas_call(
        _dispatch_kernel,
        out_shape=jax.ShapeDtypeStruct((n_rows, d), F32),
        grid_spec=grid_spec,
        input_output_aliases={4: 0},
        compiler_params=_cparams(("arbitrary",)),
        name="dispatch",
    )(pad_start, idx_flat, rank_flat, xm, jnp.zeros((n_rows, d), F32))


def _expert_kernel(be_ref, xs_ref, wgu_ref, wd_ref, y_ref):
    del be_ref
    gu = jnp.dot(xs_ref[...].astype(BF16), wgu_ref[0], preferred_element_type=F32)
    act = _silu(gu[:, :D_EXPERT]) * gu[:, D_EXPERT:]
    y_ref[...] = jnp.dot(act.astype(BF16), wd_ref[0], preferred_element_type=F32)


def _experts(block_e, xs, w_gu, w_down):
    n_rows, d = xs.shape
    grid_spec = pltpu.PrefetchScalarGridSpec(
        num_scalar_prefetch=1,
        grid=(n_rows // ROW_BLK,),
        in_specs=[pl.BlockSpec((ROW_BLK, d), lambda i, be: (i, 0)),
                  pl.BlockSpec((1, d, 2 * D_EXPERT), lambda i, be: (be[i], 0, 0)),
                  pl.BlockSpec((1, D_EXPERT, d), lambda i, be: (be[i], 0, 0))],
        out_specs=pl.BlockSpec((ROW_BLK, d), lambda i, be: (i, 0)),
    )
    return pl.pallas_call(
        _expert_kernel,
        out_shape=jax.ShapeDtypeStruct((n_rows, d), F32),
        grid_spec=grid_spec,
        compiler_params=_cparams(("arbitrary",)),
        name="experts",
    )(block_e, xs, w_gu, w_down)


def _combine_kernel(ps_ref, idx_ref, rank_ref, gw_ref, x1_ref, xm_ref, g2_ref, wsgu_ref, wsd_ref,
                    lg_ref, lb_ref, y_ref, out_ref, ybuf_ref, sem):
    def row_copy(r, j, src):
        return pltpu.make_async_copy(y_ref.at[pl.ds(src, 1), :], ybuf_ref.at[j, pl.ds(r, 1), :], sem)

    def issue(r, carry):
        for j in range(TOP_K):
            row_copy(r, j, _dest_row(ps_ref, idx_ref, rank_ref, r * TOP_K + j)).start()
        return carry

    lax.fori_loop(0, TM_MOVE, issue, 0)
    gu = jnp.dot(xm_ref[...].astype(BF16), wsgu_ref[...], preferred_element_type=F32)
    act = _silu(gu[:, :D_EXPERT]) * gu[:, D_EXPERT:]
    acc = jnp.dot(act.astype(BF16), wsd_ref[...], preferred_element_type=F32)

    def drain(r, carry):
        for j in range(TOP_K):
            row_copy(r, j, 0).wait()
        return carry

    lax.fori_loop(0, TM_MOVE, drain, 0)
    gw = gw_ref[...]
    for j in range(TOP_K):
        acc = acc + gw[:, j:j + 1] * ybuf_ref[j]
    z = DEEPNORM_ALPHA * x1_ref[...] + g2_ref[0] * acc
    out_ref[...] = _layer_norm(z) * lg_ref[...] + lb_ref[...]


def _combine(pad_start, idx_flat, rank_flat, gw, x1, xm, gate2, w_sgu, w_sd, ln_g, ln_b, y, n_per_batch):
    t, d = x1.shape
    per = n_per_batch // TM_MOVE
    smem = lambda: pl.BlockSpec((SMEM_ROWS, LANES), lambda i, ps: (i, 0), memory_space=pltpu.SMEM)
    tok = lambda w: pl.BlockSpec((TM_MOVE, w), lambda i, ps: (i, 0))
    full = lambda r, c: pl.BlockSpec((r, c), lambda i, ps: (0, 0))
    grid_spec = pltpu.PrefetchScalarGridSpec(
        num_scalar_prefetch=1,
        grid=(t // TM_MOVE,),
        in_specs=[smem(), smem(), tok(TOP_K), tok(d), tok(d),
                  pl.BlockSpec((1, 1, d), lambda i, ps: (i // per, 0, 0)),
                  full(d, 2 * D_EXPERT), full(D_EXPERT, d), full(1, d), full(1, d),
                  pl.BlockSpec(memory_space=pl.ANY)],
        out_specs=tok(d),
        scratch_shapes=[pltpu.VMEM((TOP_K, TM_MOVE, d), F32), pltpu.SemaphoreType.DMA(())],
    )
    return pl.pallas_call(
        _combine_kernel,
        out_shape=jax.ShapeDtypeStruct((t, d), F32),
        grid_spec=grid_spec,
        compiler_params=_cparams(("arbitrary",)),
        name="combine",
    )(pad_start, idx_flat, rank_flat, gw, x1, xm, gate2, w_sgu, w_sd, ln_g, ln_b, y)


def _mixer_heads(stream, shift, scale, w_in_k, conv_w9, conv_b, gate_b_row, head_g, c0, m0, grid2d):
    b, n, d = stream.shape
    y = _inproj(stream.reshape(b * n, d), shift, scale, w_in_k, n).reshape(b, n, PROJ_W)
    qk = _conv(y, conv_w9, conv_b, grid2d)
    colv, rowv = _gates(y, gate_b_row)
    mread, c_fin, m_fin = _mlstm(qk, y, colv, rowv, c0, m0, head_g)
    return y, mread, c_fin, m_fin


def kernel(x, c, ctx, c_ctx, w_ada, b_ada, w_in, conv_w, conv_b, gate_b, head_g, w_out, ln1_g, ln1_b,
           w_router, router_bias, w_expert_gu, w_expert_down, w_shared_gu, w_shared_down, ln2_g, ln2_b):
    bsz, n, d = x.shape
    n_ctx = ctx.shape[1]
    depth = w_ada.shape[0]
    t = bsz * n
    tables = _dft_tables()
    n_assign = t * TOP_K
    n_rows = n_assign + N_EXPERTS * ROW_BLK
    for l in range(depth):
        cvec = jnp.zeros((SUBLANES, d), F32).at[:bsz].set(c).at[bsz].set(c_ctx)
        mod = _ada(cvec, w_ada[l], b_ada[l][None, :])
        mods = mod.reshape(SUBLANES, 6, 1, d)
        mod_x = [mods[:bsz, j] for j in range(6)]
        mod_c = [jnp.broadcast_to(mods[bsz:bsz + 1, j], (bsz, 1, d)) for j in range(6)]

        wl = w_in[l]
        g0 = 4 * D_MLSTM
        w_in_k = jnp.concatenate(
            [wl[:, :g0], wl[:, g0 + N_GATE_COLS:], wl[:, g0:g0 + N_GATE_COLS],
             jnp.zeros((d, LANES - N_GATE_COLS), F32)], axis=1).astype(BF16)
        conv_w9 = conv_w[l].reshape(9, 2 * D_MLSTM)
        conv_b_row = conv_b[l][None, :]
        gate_b_row = jnp.zeros((1, LANES), F32).at[0, :N_GATE_COLS].set(gate_b[l].reshape(-1))
        head_g_row = head_g[l][None, :]

        c0 = jnp.zeros((bsz, N_HEADS, 2, D_HEAD, 2 * D_HEAD), F32)
        m0 = jnp.zeros((bsz, N_HEADS, 2, SUBLANES, LANES), F32)
        _, _, c_ctx_fin, m_ctx_fin = _mixer_heads(ctx, mod_c[0], mod_c[1], w_in_k, conv_w9, conv_b_row,
                                                  gate_b_row, head_g_row, c0, m0, grid2d=False)
        y, mread, _, _ = _mixer_heads(x, mod_x[0], mod_x[1], w_in_k, conv_w9, conv_b_row,
                                      gate_b_row, head_g_row, c_ctx_fin, m_ctx_fin, grid2d=True)

        r = DFT_R
        f_t = y[:, :, COL_F:COL_F + D_FOURIER].reshape(bsz, r, r, D_FOURIER).transpose(0, 2, 1, 3)
        four = _fourier(f_t, tables).transpose(0, 2, 1, 3).reshape(t, D_FOURIER)

        w_o = w_out[l].astype(BF16)
        w_r = jnp.zeros((d, LANES), F32).at[:, :N_EXPERTS].set(w_router[l])
        x2d = x.reshape(t, d)
        x1, xm, scores = _outproj(x2d, mread.reshape(t, D_MLSTM), four, w_o[:D_MLSTM], w_o[D_MLSTM:],
                                  mod_x[2], ln1_g[l][None, :], ln1_b[l][None, :], mod_x[3], mod_x[4],
                                  w_r, n)

        bias_row = jnp.zeros((1, LANES), F32).at[0, :N_EXPERTS].set(router_bias[l])
        idx, gw, rank, counts = _route(scores, bias_row)
        sizes = counts[0, :N_EXPERTS].astype(jnp.int32)
        padded = (sizes + ROW_BLK - 1) // ROW_BLK * ROW_BLK
        pad_end = jnp.cumsum(padded)
        pad_start = (pad_end - padded).astype(jnp.int32)
        blk_first = jnp.arange(n_rows // ROW_BLK, dtype=jnp.int32) * ROW_BLK
        block_e = jnp.minimum(jnp.sum(blk_first[:, None] >= pad_end[None, :], axis=1),
                              N_EXPERTS - 1).astype(jnp.int32)
        idx_flat = idx.reshape(n_assign // LANES, LANES)
        rank_flat = rank.reshape(n_assign // LANES, LANES)

        xs = _dispatch(pad_start, idx_flat, rank_flat, xm, n_rows)
        yb = _experts(block_e, xs, w_expert_gu[l].astype(BF16), w_expert_down[l].astype(BF16))
        x = _combine(pad_start, idx_flat, rank_flat, gw, x1, xm, mod_x[5],
                     w_shared_gu[l].astype(BF16), w_shared_down[l].astype(BF16),
                     ln2_g[l][None, :], ln2_b[l][None, :], yb, n).reshape(bsz, n, d)
        assert depth == 1
    return x
```

```python
import functools
import math

import numpy as np
import jax
import jax.numpy as jnp
from jax import lax
from jax.experimental import pallas as pl
from jax.experimental.pallas import tpu as pltpu
from jax.experimental.pallas import tpu_sc as plsc

F32 = jnp.float32
BF16 = jnp.bfloat16
HIGHEST = lax.Precision.HIGHEST

D_MODEL = 1024
GRID_W = 64
N_HEADS = 4
D_HEAD = 128
D_MLSTM = N_HEADS * D_HEAD
D_FOURIER = 512
FOURIER_GROUP = 64
N_GATE_COLS = 16
CHUNK = 128
N_EXPERTS = 64
TOP_K = 8
D_EXPERT = 256
ROUTED_SCALE = 2.5
DEEPNORM_ALPHA = 2.0 ** 0.25
LN_EPS = 1e-5

LANES = 128
SUBLANES = 8
VMEM_LIMIT = 48 * 1024 * 1024

COL_QK = 0
COL_V = 2 * D_MLSTM
COL_O = 3 * D_MLSTM
COL_F = 4 * D_MLSTM
COL_G = COL_F + D_FOURIER
PROJ_W = COL_G + LANES

TM = 512
ROW_BLK = 256
PACK_W = D_MODEL // 2
SC_ROWS = 128
CONV_PAD = 72
CONV_RB = 256
DFT_R = 64


def _cparams(sem, vmem=VMEM_LIMIT):
    return pltpu.CompilerParams(dimension_semantics=sem, vmem_limit_bytes=vmem)


def _layer_norm(x):
    mu = jnp.mean(x, axis=-1, keepdims=True)
    xc = x - mu
    var = jnp.mean(xc * xc, axis=-1, keepdims=True)
    return xc * lax.rsqrt(var + LN_EPS)


def _silu(x):
    return x * jax.nn.sigmoid(x)


def _pack_rows(v):
    as_bits = lambda u: lax.bitcast_convert_type(u.astype(BF16).astype(F32), jnp.uint32)
    lo = lax.shift_right_logical(as_bits(v[:, :PACK_W]), jnp.uint32(16))
    return lax.bitcast_convert_type(as_bits(v[:, PACK_W:]) | lo, jnp.int32)


def _unpack_rows(w):
    u = lax.bitcast_convert_type(w, jnp.uint32)
    lo = lax.bitcast_convert_type(lax.shift_left(u, jnp.uint32(16)), F32)
    hi = lax.bitcast_convert_type(u & jnp.uint32(0xFFFF0000), F32)
    return jnp.concatenate([lo, hi], axis=1)


def _ada_kernel(c_ref, w_ref, b_ref, o_ref):
    s = _silu(c_ref[...])
    o_ref[...] = jnp.dot(s, w_ref[...], precision=HIGHEST, preferred_element_type=F32) + b_ref[...]


def _ada(cvec, w_ada, b_ada):
    rows, d = cvec.shape
    n_out = w_ada.shape[1]
    tn = 1024
    return pl.pallas_call(
        _ada_kernel,
        out_shape=jax.ShapeDtypeStruct((rows, n_out), F32),
        grid=(n_out // tn,),
        in_specs=[pl.BlockSpec((rows, d), lambda j: (0, 0)),
                  pl.BlockSpec((d, tn), lambda j: (0, j)),
                  pl.BlockSpec((1, tn), lambda j: (0, j))],
        out_specs=pl.BlockSpec((rows, tn), lambda j: (0, j)),
        compiler_params=_cparams(("arbitrary",)),
        name="ada",
    )(cvec, w_ada, b_ada)


def _inproj_kernel(x_ref, sh_ref, sc_ref, w_ref, y_ref):
    u = _layer_norm(x_ref[...]) * (1.0 + sc_ref[0]) + sh_ref[0]
    y_ref[...] = jnp.dot(u.astype(BF16), w_ref[...], preferred_element_type=F32)


def _inproj(x2d, shift, scale, w, n_per_batch):
    t, d = x2d.shape
    tm = min(TM, n_per_batch)
    per = n_per_batch // tm
    return pl.pallas_call(
        _inproj_kernel,
        out_shape=jax.ShapeDtypeStruct((t, PROJ_W), F32),
        grid=(t // tm,),
        in_specs=[pl.BlockSpec((tm, d), lambda i: (i, 0)),
                  pl.BlockSpec((1, 1, d), lambda i: (i // per, 0, 0)),
                  pl.BlockSpec((1, 1, d), lambda i: (i // per, 0, 0)),
                  pl.BlockSpec((d, PROJ_W), lambda i: (0, 0))],
        out_specs=pl.BlockSpec((tm, PROJ_W), lambda i: (i, 0)),
        compiler_params=_cparams(("arbitrary",)),
        name="inproj",
    )(x2d, shift, scale, w)


def _conv_kernel(y_ref, w_ref, b_ref, o_ref, pad_ref, *, n, grid2d):
    zeros = jnp.zeros((CONV_PAD, LANES), F32)
    pad_ref[0:CONV_PAD, :] = zeros
    pad_ref[CONV_PAD + n:CONV_PAD + n + CONV_PAD, :] = zeros
    pad_ref[CONV_PAD:CONV_PAD + n, :] = y_ref[0]
    k_scale = jnp.where(pl.program_id(1) >= N_HEADS, D_HEAD ** -0.5, 1.0).astype(F32)
    bias = b_ref[...]
    rb = min(CONV_RB, n)
    col = lax.broadcasted_iota(jnp.int32, (rb, LANES), 0) % GRID_W
    not_first = col >= 1
    not_last = col <= GRID_W - 2
    row_taps = (0, 1, 2) if grid2d else (1,)
    for blk in range(n // rb):
        r0 = CONV_PAD + blk * rb
        acc = jnp.zeros((rb, LANES), F32)
        for dj in range(3):
            part = jnp.zeros((rb, LANES), F32)
            for di in row_taps:
                off = (di - 1) * GRID_W + (dj - 1)
                tap = w_ref[di * 3 + dj:di * 3 + dj + 1, :]
                part = part + tap * pad_ref[r0 + off:r0 + off + rb, :]
            if grid2d and dj == 0:
                part = jnp.where(not_first, part, 0.0)
            if grid2d and dj == 2:
                part = jnp.where(not_last, part, 0.0)
            acc = acc + part
        val = _silu(acc + bias) * k_scale
        o_ref[0, blk * rb:(blk + 1) * rb, :] = val.astype(o_ref.dtype)


def _conv(y3d, conv_w9, conv_b, grid2d):
    b, n, _ = y3d.shape
    n_ct = 2 * D_MLSTM // LANES
    return pl.pallas_call(
        functools.partial(_conv_kernel, n=n, grid2d=grid2d),
        out_shape=jax.ShapeDtypeStruct((b, n, 2 * D_MLSTM), BF16),
        grid=(b, n_ct),
        in_specs=[pl.BlockSpec((1, n, LANES), lambda i, c: (i, 0, c)),
                  pl.BlockSpec((9, LANES), lambda i, c: (0, c)),
                  pl.BlockSpec((1, LANES), lambda i, c: (0, c))],
        out_specs=pl.BlockSpec((1, n, LANES), lambda i, c: (i, 0, c)),
        scratch_shapes=[pltpu.VMEM((n + 2 * CONV_PAD, LANES), F32)],
        compiler_params=_cparams(("arbitrary", "arbitrary")),
        name="conv2d" if grid2d else "conv1d",
    )(y3d, conv_w9, conv_b)


def _gates_kernel(g_ref, gb_ref, cv_ref, rv_ref, *, nc):
    row = lax.broadcasted_iota(jnp.int32, (CHUNK, CHUNK), 0)
    col = lax.broadcasted_iota(jnp.int32, (CHUNK, CHUNK), 1)
    lower = col <= row
    upper = col >= row
    tri_l = lower.astype(F32)
    tri_u = upper.astype(F32)
    sub = lax.broadcasted_iota(jnp.int32, (SUBLANES, LANES), 0)
    neg_inf = jnp.float32(-jnp.inf)

    def body(c, carry):
        r0 = pl.multiple_of(c * CHUNK, CHUNK)
        pre = g_ref[0, pl.ds(r0, CHUNK), :] + gb_ref[...]
        lf = -(jnp.maximum(-pre, 0.0) + jnp.log1p(jnp.exp(-jnp.abs(pre))))
        b_f = jnp.dot(tri_l, lf, precision=HIGHEST, preferred_element_type=F32)
        b_b = jnp.dot(tri_u, lf, precision=HIGHEST, preferred_element_type=F32)
        pre_t = pre.T
        b_f_t = b_f.T
        b_b_t = b_b.T
        for h in range(N_HEADS):
            i_f, f_f, i_b, f_b = h, N_HEADS + h, 2 * N_HEADS + h, 3 * N_HEADS + h
            bf = b_f[:, f_f:f_f + 1]
            bb = b_b[:, f_b:f_b + 1]
            af = pre[:, i_f:i_f + 1] - bf
            ab = pre[:, i_b:i_b + 1] - bb
            af_row = pre_t[i_f:i_f + 1, :] - b_f_t[f_f:f_f + 1, :]
            ab_row = pre_t[i_b:i_b + 1, :] - b_b_t[f_b:f_b + 1, :]
            cmf = jnp.max(jnp.where(lower, af_row, neg_inf), axis=-1, keepdims=True)
            cmb = jnp.max(jnp.where(upper, ab_row, neg_inf), axis=-1, keepdims=True)
            cv = jnp.where(col == 0, af, 0.0)
            cv = jnp.where(col == 1, cmf, cv)
            cv = jnp.where(col == 2, bf, cv)
            cv = jnp.where(col == 3, ab, cv)
            cv = jnp.where(col == 4, cmb, cv)
            cv = jnp.where(col == 5, bb, cv)
            cv_ref[0, h, pl.ds(r0, CHUNK), :] = cv
            rv = jnp.where(sub == 0, af_row, jnp.where(sub == 1, ab_row, 0.0))
            rv_ref[0, h, pl.ds(c, 1), :, :] = rv[None]
        return carry

    lax.fori_loop(0, nc, body, 0)


def _gates(y3d, gate_b_row):
    b, n, _ = y3d.shape
    nc = n // CHUNK
    return pl.pallas_call(
        functools.partial(_gates_kernel, nc=nc),
        out_shape=(jax.ShapeDtypeStruct((b, N_HEADS, n, LANES), F32),
                   jax.ShapeDtypeStruct((b, N_HEADS, nc, SUBLANES, LANES), F32)),
        grid=(b,),
        in_specs=[pl.BlockSpec((1, n, LANES), lambda i: (i, 0, COL_G // LANES)),
                  pl.BlockSpec((1, LANES), lambda i: (0, 0))],
        out_specs=(pl.BlockSpec((1, N_HEADS, n, LANES), lambda i: (i, 0, 0, 0)),
                   pl.BlockSpec((1, N_HEADS, nc, SUBLANES, LANES), lambda i: (i, 0, 0, 0, 0))),
        compiler_params=_cparams(("arbitrary",)),
        name="gates",
    )(y3d, gate_b_row)


def _mlstm_kernel(q_ref, k_ref, v_ref, o_ref, cv_ref, rv_ref, c0_ref, m0_ref, hg_ref,
                  out_ref, cfin_ref, mfin_ref, hf_ref, hb_ref, cst_ref, mst_ref, *, nc):
    cst_ref[...] = c0_ref[0, 0]
    mst_ref[...] = m0_ref[0, 0]
    row = lax.broadcasted_iota(jnp.int32, (CHUNK, CHUNK), 0)
    col = lax.broadcasted_iota(jnp.int32, (CHUNK, CHUNK), 1)
    ones_col = (col == 0).astype(F32)

    def one_chunk(c, d, mask, last, h_ref):
        r0 = pl.multiple_of(c * CHUNK, CHUNK)
        q = q_ref[0, pl.ds(r0, CHUNK), :]
        k = k_ref[0, pl.ds(r0, CHUNK), :]
        v_aug = jnp.concatenate([v_ref[0, pl.ds(r0, CHUNK), :], ones_col], axis=1)
        cv = cv_ref[0, 0, pl.ds(r0, CHUNK), :]
        a_col = cv[:, 3 * d:3 * d + 1]
        cm_col = cv[:, 3 * d + 1:3 * d + 2]
        b_col = cv[:, 3 * d + 2:3 * d + 3]
        a_row = rv_ref[0, 0, pl.ds(c, 1), :, :][0, d:d + 1, :]
        m_prev = mst_ref[d, 0:1, 0:1]
        m_col = jnp.maximum(m_prev, cm_col)
        dmat = jnp.where(mask, jnp.exp(a_row - m_col), 0.0)
        s = lax.dot_general(q, k, (((1,), (1,)), ((), ())), preferred_element_type=F32) * dmat
        ct = cst_ref[d]
        h_aug = (jnp.dot(s.astype(BF16), v_aug.astype(BF16), preferred_element_type=F32)
                 + jnp.exp(m_prev - m_col)
                 * jnp.dot(q, ct.astype(BF16), preferred_element_type=F32))
        den = h_aug[:, D_HEAD:D_HEAD + 1]
        floor = jnp.exp(-b_col - m_col)
        h_ref[pl.ds(r0, CHUNK), :] = h_aug[:, :D_HEAD] / jnp.maximum(jnp.abs(den), floor)
        m_last = m_col[last:last + 1, :]
        m_new = b_col[last:last + 1, :] + m_last
        wv = (jnp.exp(a_col - m_last) * v_aug).astype(BF16)
        cst_ref[d] = (jnp.exp(m_prev - m_last) * ct
                      + lax.dot_general(k, wv, (((0,), (0,)), ((), ())), preferred_element_type=F32))
        mst_ref[d] = jnp.broadcast_to(m_new, (SUBLANES, LANES))

    def body(c, carry):
        one_chunk(c, 0, col <= row, CHUNK - 1, hf_ref)
        one_chunk(nc - 1 - c, 1, col >= row, 0, hb_ref)
        return carry

    lax.fori_loop(0, nc, body, 0)
    cfin_ref[0, 0] = cst_ref[...]
    mfin_ref[0, 0] = mst_ref[...]
    hn = _layer_norm(hf_ref[...] + hb_ref[...])
    out_ref[0] = (hn * hg_ref[...] * jax.nn.sigmoid(o_ref[0])).astype(out_ref.dtype)


def _mlstm(qk, y3d, colv, rowv, c0, m0, head_g):
    b, n, _ = qk.shape
    nc = n // CHUNK
    v_blk = COL_V // LANES
    o_blk = COL_O // LANES
    tok = lambda off: pl.BlockSpec((1, n, LANES), lambda i, h: (i, 0, off + h))
    return pl.pallas_call(
        functools.partial(_mlstm_kernel, nc=nc),
        out_shape=(jax.ShapeDtypeStruct((b, n, D_MLSTM), BF16),
                   jax.ShapeDtypeStruct((b, N_HEADS, 2, D_HEAD, 2 * D_HEAD), F32),
                   jax.ShapeDtypeStruct((b, N_HEADS, 2, SUBLANES, LANES), F32)),
        grid=(b, N_HEADS),
        in_specs=[tok(0), tok(N_HEADS), tok(v_blk), tok(o_blk),
                  pl.BlockSpec((1, 1, n, LANES), lambda i, h: (i, h, 0, 0)),
                  pl.BlockSpec((1, 1, nc, SUBLANES, LANES), lambda i, h: (i, h, 0, 0, 0)),
                  pl.BlockSpec((1, 1, 2, D_HEAD, 2 * D_HEAD), lambda i, h: (i, h, 0, 0, 0)),
                  pl.BlockSpec((1, 1, 2, SUBLANES, LANES), lambda i, h: (i, h, 0, 0, 0)),
                  pl.BlockSpec((1, LANES), lambda i, h: (0, h))],
        out_specs=(pl.BlockSpec((1, n, LANES), lambda i, h: (i, 0, h)),
                   pl.BlockSpec((1, 1, 2, D_HEAD, 2 * D_HEAD), lambda i, h: (i, h, 0, 0, 0)),
                   pl.BlockSpec((1, 1, 2, SUBLANES, LANES), lambda i, h: (i, h, 0, 0, 0))),
        scratch_shapes=[pltpu.VMEM((n, LANES), F32), pltpu.VMEM((n, LANES), F32),
                        pltpu.VMEM((2, D_HEAD, 2 * D_HEAD), F32),
                        pltpu.VMEM((2, SUBLANES, LANES), F32)],
        compiler_params=_cparams(("arbitrary", "arbitrary")),
        name="mlstm",
    )(qk, qk, y3d, y3d, colv, rowv, c0, m0, head_g)


def _dft_tables():
    r = DFT_R
    idx = np.arange(r)
    cg = np.cos(2 * np.pi * np.outer(idx, idx) / FOURIER_GROUP)
    sg = np.sin(2 * np.pi * np.outer(idx, idx) / FOURIER_GROUP)
    n_grp = D_FOURIER // FOURIER_GROUP
    bdc = np.kron(np.eye(n_grp), cg) / 8.0
    bds = np.kron(np.eye(n_grp), sg) / 8.0
    k1 = idx[:, None]
    a = idx[None, :]
    m1 = np.zeros((r, 2 * r, 2 * r))
    for b2 in range(r):
        th = 2 * np.pi * k1 * (r * a + b2) / (r * r)
        ec, es = np.cos(th), np.sin(th)
        m1[b2] = np.block([[ec, -es], [es, ec]]) / 8.0
    th2 = 2 * np.pi * np.outer(idx, idx) / r
    m2 = np.concatenate([np.cos(th2), -np.sin(th2)], axis=1) / 8.0
    as_bf16 = lambda t: jnp.asarray(t, dtype=F32).astype(BF16)
    return as_bf16(bdc), as_bf16(bds), as_bf16(m1), as_bf16(m2)


FOUR_BB = 8


def _four1_kernel(f_ref, bdc_ref, bds_ref, m1_ref, z_ref):
    for j in range(FOUR_BB):
        fb = f_ref[0, j].astype(BF16)
        p = jnp.dot(fb, bdc_ref[...], preferred_element_type=F32)
        q = jnp.dot(fb, bds_ref[...], preferred_element_type=F32)
        x2 = jnp.concatenate([p, q], axis=0).astype(BF16)
        z_ref[0, j] = jnp.dot(m1_ref[j], x2, preferred_element_type=F32).astype(z_ref.dtype)


def _four2_kernel(z_ref, m2_ref, o_ref):
    for j in range(FOUR_BB):
        o_ref[0, j] = jnp.dot(m2_ref[...], z_ref[0, j], preferred_element_type=F32)


def _fourier(f_t, tables):
    bdc, bds, m1, m2 = tables
    b = f_t.shape[0]
    r = DFT_R
    nb = r // FOUR_BB
    z = pl.pallas_call(
        _four1_kernel,
        out_shape=jax.ShapeDtypeStruct((b, r, 2 * r, D_FOURIER), BF16),
        grid=(b, nb),
        in_specs=[pl.BlockSpec((1, FOUR_BB, r, D_FOURIER), lambda i, j: (i, j, 0, 0)),
                  pl.BlockSpec((D_FOURIER, D_FOURIER), lambda i, j: (0, 0)),
                  pl.BlockSpec((D_FOURIER, D_FOURIER), lambda i, j: (0, 0)),
                  pl.BlockSpec((FOUR_BB, 2 * r, 2 * r), lambda i, j: (j, 0, 0))],
        out_specs=pl.BlockSpec((1, FOUR_BB, 2 * r, D_FOURIER), lambda i, j: (i, j, 0, 0)),
        compiler_params=_cparams(("arbitrary", "arbitrary")),
        name="four1",
    )(f_t, bdc, bds, m1)
    z_t = z.reshape(b, r, 2, r, D_FOURIER).transpose(0, 3, 2, 1, 4).reshape(b, r, 2 * r, D_FOURIER)
    return pl.pallas_call(
        _four2_kernel,
        out_shape=jax.ShapeDtypeStruct((b, r, r, D_FOURIER), F32),
        grid=(b, nb),
        in_specs=[pl.BlockSpec((1, FOUR_BB, 2 * r, D_FOURIER), lambda i, j: (i, j, 0, 0)),
                  pl.BlockSpec((r, 2 * r), lambda i, j: (0, 0))],
        out_specs=pl.BlockSpec((1, FOUR_BB, r, D_FOURIER), lambda i, j: (i, j, 0, 0)),
        compiler_params=_cparams(("arbitrary", "arbitrary")),
        name="four2",
    )(z_t, m2)


def _outproj_kernel(x_ref, mr_ref, fo_ref, wm_ref, wf_ref, g1_ref, lg_ref, lb_ref,
                    sh_ref, sc_ref, wr_ref, x1_ref, xm_ref, s_ref):
    mix = (jnp.dot(mr_ref[...], wm_ref[...], preferred_element_type=F32)
           + jnp.dot(fo_ref[...].astype(BF16), wf_ref[...], preferred_element_type=F32))
    x1 = _layer_norm(DEEPNORM_ALPHA * x_ref[...] + g1_ref[0] * mix) * lg_ref[...] + lb_ref[...]
    x1_ref[...] = x1
    xm = _layer_norm(x1) * (1.0 + sc_ref[0]) + sh_ref[0]
    xm_ref[...] = _pack_rows(xm)
    logits = jnp.dot(xm, wr_ref[...], precision=HIGHEST, preferred_element_type=F32)
    s_ref[...] = jax.nn.sigmoid(logits)


def _outproj(x2d, mread, four, w_m, w_f, gate1, ln_g, ln_b, shift2, scale2, w_router, n_per_batch):
    t, d = x2d.shape
    per = n_per_batch // TM
    tokd = lambda w: pl.BlockSpec((TM, w), lambda i: (i, 0))
    full = lambda r, c: pl.BlockSpec((r, c), lambda i: (0, 0))
    mod = pl.BlockSpec((1, 1, d), lambda i: (i // per, 0, 0))
    return pl.pallas_call(
        _outproj_kernel,
        out_shape=(jax.ShapeDtypeStruct((t, d), F32),
                   jax.ShapeDtypeStruct((t, PACK_W), jnp.int32),
                   jax.ShapeDtypeStruct((t, LANES), F32)),
        grid=(t // TM,),
        in_specs=[tokd(d), tokd(D_MLSTM), tokd(D_FOURIER), full(D_MLSTM, d), full(D_FOURIER, d),
                  mod, full(1, d), full(1, d), mod, mod, full(d, LANES)],
        out_specs=(tokd(d), tokd(PACK_W), tokd(LANES)),
        compiler_params=_cparams(("arbitrary",)),
        name="outproj",
    )(x2d, mread, four, w_m, w_f, gate1, ln_g, ln_b, shift2, scale2, w_router)


def _route_kernel(s_ref, b_ref, idx_ref, gw_ref, rank_ref, cnt_ref, carry_ref):
    @pl.when(pl.program_id(0) == 0)
    def _():
        carry_ref[...] = jnp.zeros_like(carry_ref)

    tm = s_ref.shape[0]
    s = s_ref[...]
    lane_i = lax.broadcasted_iota(jnp.int32, (tm, LANES), 1)
    lane = lane_i.astype(F32)
    neg_inf = jnp.float32(-jnp.inf)
    sb = jnp.where(lane_i < N_EXPERTS, s + b_ref[...], neg_inf)
    picks, sels = [], []
    taken = jnp.zeros((tm, LANES), F32)
    for _ in range(TOP_K):
        best = jnp.max(sb, axis=-1, keepdims=True)
        pick = jnp.min(jnp.where(sb == best, lane, float(LANES)), axis=-1, keepdims=True)
        onehot = lane == pick
        sels.append(jnp.sum(jnp.where(onehot, s, 0.0), axis=-1, keepdims=True))
        picks.append(pick)
        sb = jnp.where(onehot, neg_inf, sb)
        taken = taken + onehot.astype(F32)
    r_i = lax.broadcasted_iota(jnp.int32, (tm, tm), 0)
    c_i = lax.broadcasted_iota(jnp.int32, (tm, tm), 1)
    before = (c_i < r_i).astype(BF16)
    rank_all = carry_ref[0:1, :] + jnp.dot(before, taken.astype(BF16), preferred_element_type=F32)
    carry_ref[...] = carry_ref[...] + jnp.sum(taken, axis=0, keepdims=True)
    cnt_ref[...] = carry_ref[...]
    total = sels[0]
    for v in sels[1:]:
        total = total + v
    k_i = lax.broadcasted_iota(jnp.int32, (tm, TOP_K), 1)
    idx_out = jnp.zeros((tm, TOP_K), F32)
    gw_out = jnp.zeros((tm, TOP_K), F32)
    rank_out = jnp.zeros((tm, TOP_K), F32)
    for j in range(TOP_K):
        rank_j = jnp.sum(jnp.where(lane == picks[j], rank_all, 0.0), axis=-1, keepdims=True)
        idx_out = jnp.where(k_i == j, picks[j], idx_out)
        gw_out = jnp.where(k_i == j, ROUTED_SCALE * sels[j] / total, gw_out)
        rank_out = jnp.where(k_i == j, rank_j, rank_out)
    idx_ref[...] = idx_out.astype(jnp.int32)
    gw_ref[...] = gw_out
    rank_ref[...] = rank_out.astype(jnp.int32)


def _route(scores, bias_row):
    t = scores.shape[0]
    tok = lambda w: pl.BlockSpec((TM, w), lambda i: (i, 0))
    return pl.pallas_call(
        _route_kernel,
        out_shape=(jax.ShapeDtypeStruct((t, TOP_K), jnp.int32),
                   jax.ShapeDtypeStruct((t, TOP_K), F32),
                   jax.ShapeDtypeStruct((t, TOP_K), jnp.int32),
                   jax.ShapeDtypeStruct((SUBLANES, LANES), F32)),
        grid=(t // TM,),
        in_specs=[tok(LANES), pl.BlockSpec((1, LANES), lambda i: (0, 0))],
        out_specs=(tok(TOP_K), tok(TOP_K), tok(TOP_K),
                   pl.BlockSpec((SUBLANES, LANES), lambda i: (0, 0))),
        scratch_shapes=[pltpu.VMEM((SUBLANES, LANES), F32)],
        compiler_params=_cparams(("arbitrary",)),
        name="route",
    )(scores, bias_row)


def _dest_kernel(idx_ref, rank_ref, ps_ref, dest_ref):
    tm = idx_ref.shape[0]
    lane = lax.broadcasted_iota(jnp.int32, (tm, LANES), 1)
    k_i = lax.broadcasted_iota(jnp.int32, (tm, TOP_K), 1)
    idx = idx_ref[...]
    start = jnp.zeros((tm, TOP_K), F32)
    for j in range(TOP_K):
        s_j = jnp.sum(jnp.where(lane == idx[:, j:j + 1], ps_ref[...], 0.0), axis=-1, keepdims=True)
        start = jnp.where(k_i == j, s_j, start)
    dest_ref[...] = start.astype(jnp.int32) + rank_ref[...]


def _dest(idx, rank, pad_start_row):
    t = idx.shape[0]
    tok = pl.BlockSpec((TM, TOP_K), lambda i: (i, 0))
    return pl.pallas_call(
        _dest_kernel,
        out_shape=jax.ShapeDtypeStruct((t, TOP_K), jnp.int32),
        grid=(t // TM,),
        in_specs=[tok, tok, pl.BlockSpec((1, LANES), lambda i: (0, 0))],
        out_specs=tok,
        compiler_params=_cparams(("arbitrary",)),
        name="dest",
    )(idx, rank, pad_start_row)


def _sc_worker_id(info):
    return lax.axis_index("s") * info.num_cores + lax.axis_index("c")


def _sc_dispatch(xm, dest3, n_rows):
    t, w = xm.shape
    info = plsc.get_sparse_core_info()
    n_workers = info.num_cores * info.num_subcores
    per_worker = t // SC_ROWS // n_workers
    mesh = plsc.VectorSubcoreMesh(core_axis_name="c", subcore_axis_name="s")

    @functools.partial(
        pl.kernel, mesh=mesh,
        out_type=jax.ShapeDtypeStruct((n_rows, w), xm.dtype),
        scratch_types=[pltpu.VMEM((TOP_K, SC_ROWS), jnp.int32), pltpu.VMEM((SC_ROWS, w), xm.dtype)],
        name="sc_dispatch",
    )
    def run(xm_hbm, dest_hbm, xs_hbm, idx_v, rows_v):
        first = _sc_worker_id(info) * per_worker

        @pl.loop(0, per_worker)
        def _(ci):
            chunk = first + ci
            pltpu.sync_copy(dest_hbm.at[chunk], idx_v)
            pltpu.sync_copy(xm_hbm.at[pl.ds(chunk * SC_ROWS, SC_ROWS)], rows_v)
            for j in range(TOP_K):
                pltpu.sync_copy(rows_v, xs_hbm.at[idx_v.at[j]])

    return run(xm, dest3)


def _sc_gather(table, idx2):
    n_chunks = idx2.shape[0]
    w = table.shape[1]
    info = plsc.get_sparse_core_info()
    n_workers = info.num_cores * info.num_subcores
    per_worker = n_chunks // n_workers
    mesh = plsc.VectorSubcoreMesh(core_axis_name="c", subcore_axis_name="s")

    @functools.partial(
        pl.kernel, mesh=mesh,
        out_type=jax.ShapeDtypeStruct((n_chunks * SC_ROWS, w), table.dtype),
        scratch_types=[pltpu.VMEM((SC_ROWS,), jnp.int32), pltpu.VMEM((SC_ROWS, w), table.dtype)],
        name="sc_gather",
    )
    def run(table_hbm, idx_hbm, out_hbm, idx_v, rows_v):
        first = _sc_worker_id(info) * per_worker

        @pl.loop(0, per_worker)
        def _(ci):
            chunk = first + ci
            pltpu.sync_copy(idx_hbm.at[chunk], idx_v)
            pltpu.sync_copy(table_hbm.at[idx_v], rows_v)
            pltpu.sync_copy(rows_v, out_hbm.at[pl.ds(chunk * SC_ROWS, SC_ROWS)])

    return run(table, idx2)


def _expert_kernel(be_ref, bv_ref, xs_ref, wgu_ref, wd_ref, y_ref):
    del be_ref
    n_valid = bv_ref[pl.program_id(0)]

    @pl.when(n_valid > 0)
    def _():
        row = lax.broadcasted_iota(jnp.int32, (ROW_BLK, D_MODEL), 0)
        x = jnp.where(row < n_valid, _unpack_rows(xs_ref[...]), 0.0).astype(BF16)
        gu = jnp.dot(x, wgu_ref[0], preferred_element_type=F32)
        act = _silu(gu[:, :D_EXPERT]) * gu[:, D_EXPERT:]
        y_ref[...] = _pack_rows(jnp.dot(act.astype(BF16), wd_ref[0], preferred_element_type=F32))


def _experts(block_e, block_valid, xs, w_gu, w_down):
    n_rows, w = xs.shape
    d = D_MODEL
    grid_spec = pltpu.PrefetchScalarGridSpec(
        num_scalar_prefetch=2,
        grid=(n_rows // ROW_BLK,),
        in_specs=[pl.BlockSpec((ROW_BLK, w), lambda i, be, bv: (i, 0)),
                  pl.BlockSpec((1, d, 2 * D_EXPERT), lambda i, be, bv: (be[i], 0, 0)),
                  pl.BlockSpec((1, D_EXPERT, d), lambda i, be, bv: (be[i], 0, 0))],
        out_specs=pl.BlockSpec((ROW_BLK, w), lambda i, be, bv: (i, 0)),
    )
    return pl.pallas_call(
        _expert_kernel,
        out_shape=jax.ShapeDtypeStruct((n_rows, w), jnp.int32),
        grid_spec=grid_spec,
        compiler_params=_cparams(("arbitrary",)),
        name="experts",
    )(block_e, block_valid, xs, w_gu, w_down)


TM_COMBINE = 256


def _combine_kernel(*refs):
    yg_refs = refs[:TOP_K]
    gw_ref, x1_ref, xm_ref, g2_ref, wsgu_ref, wsd_ref, lg_ref, lb_ref, out_ref = refs[TOP_K:]
    xm = _unpack_rows(xm_ref[...]).astype(BF16)
    gu = jnp.dot(xm, wsgu_ref[...], preferred_element_type=F32)
    act = _silu(gu[:, :D_EXPERT]) * gu[:, D_EXPERT:]
    acc = jnp.dot(act.astype(BF16), wsd_ref[...], preferred_element_type=F32)
    gw = gw_ref[...]
    for j in range(TOP_K):
        acc = acc + gw[:, j:j + 1] * _unpack_rows(yg_refs[j][...])
    z = DEEPNORM_ALPHA * x1_ref[...] + g2_ref[0] * acc
    out_ref[...] = _layer_norm(z) * lg_ref[...] + lb_ref[...]


def _combine(yg, gw, x1, xm, gate2, w_sgu, w_sd, ln_g, ln_b, n_per_batch):
    t, d = x1.shape
    tm = TM_COMBINE
    per = n_per_batch // tm
    steps = t // tm
    tok = lambda w: pl.BlockSpec((tm, w), lambda i: (i, 0))
    full = lambda r, c: pl.BlockSpec((r, c), lambda i: (0, 0))
    slot = lambda j: pl.BlockSpec((tm, PACK_W), lambda i: (j * steps + i, 0))
    return pl.pallas_call(
        _combine_kernel,
        out_shape=jax.ShapeDtypeStruct((t, d), F32),
        grid=(steps,),
        in_specs=[slot(j) for j in range(TOP_K)]
        + [tok(TOP_K), tok(d), tok(PACK_W), pl.BlockSpec((1, 1, d), lambda i: (i // per, 0, 0)),
           full(d, 2 * D_EXPERT), full(D_EXPERT, d), full(1, d), full(1, d)],
        out_specs=tok(d),
        compiler_params=_cparams(("arbitrary",)),
        name="combine",
    )(*([yg] * TOP_K), gw, x1, xm, gate2, w_sgu, w_sd, ln_g, ln_b)


def _mixer_heads(stream, shift, scale, w_in_k, conv_w9, conv_b, gate_b_row, head_g, c0, m0, grid2d):
    b, n, d = stream.shape
    y = _inproj(stream.reshape(b * n, d), shift, scale, w_in_k, n).reshape(b, n, PROJ_W)
    qk = _conv(y, conv_w9, conv_b, grid2d)
    colv, rowv = _gates(y, gate_b_row)
    mread, c_fin, m_fin = _mlstm(qk, y, colv, rowv, c0, m0, head_g)
    return y, mread, c_fin, m_fin


def kernel(x, c, ctx, c_ctx, w_ada, b_ada, w_in, conv_w, conv_b, gate_b, head_g, w_out, ln1_g, ln1_b,
           w_router, router_bias, w_expert_gu, w_expert_down, w_shared_gu, w_shared_down, ln2_g, ln2_b):
    bsz, n, d = x.shape
    n_ctx = ctx.shape[1]
    depth = w_ada.shape[0]
    t = bsz * n
    tables = _dft_tables()
    n_assign = t * TOP_K
    n_rows = n_assign + N_EXPERTS * ROW_BLK
    for l in range(depth):
        cvec = jnp.zeros((SUBLANES, d), F32).at[:bsz].set(c).at[bsz].set(c_ctx)
        mod = _ada(cvec, w_ada[l], b_ada[l][None, :])
        mods = mod.reshape(SUBLANES, 6, 1, d)
        mod_x = [mods[:bsz, j] for j in range(6)]
        mod_c = [jnp.broadcast_to(mods[bsz:bsz + 1, j], (bsz, 1, d)) for j in range(6)]

        wl = w_in[l]
        g0 = 4 * D_MLSTM
        w_in_k = jnp.concatenate(
            [wl[:, :g0], wl[:, g0 + N_GATE_COLS:], wl[:, g0:g0 + N_GATE_COLS],
             jnp.zeros((d, LANES - N_GATE_COLS), F32)], axis=1).astype(BF16)
        conv_w9 = conv_w[l].reshape(9, 2 * D_MLSTM)
        conv_b_row = conv_b[l][None, :]
        gate_b_row = jnp.zeros((1, LANES), F32).at[0, :N_GATE_COLS].set(gate_b[l].reshape(-1))
        head_g_row = head_g[l][None, :]

        c0 = jnp.zeros((bsz, N_HEADS, 2, D_HEAD, 2 * D_HEAD), F32)
        m0 = jnp.zeros((bsz, N_HEADS, 2, SUBLANES, LANES), F32)
        _, _, c_ctx_fin, m_ctx_fin = _mixer_heads(ctx, mod_c[0], mod_c[1], w_in_k, conv_w9, conv_b_row,
                                                  gate_b_row, head_g_row, c0, m0, grid2d=False)
        y, mread, _, _ = _mixer_heads(x, mod_x[0], mod_x[1], w_in_k, conv_w9, conv_b_row,
                                      gate_b_row, head_g_row, c_ctx_fin, m_ctx_fin, grid2d=True)

        r = DFT_R
        f_t = y[:, :, COL_F:COL_F + D_FOURIER].reshape(bsz, r, r, D_FOURIER).transpose(0, 2, 1, 3)
        four = _fourier(f_t, tables).transpose(0, 2, 1, 3).reshape(t, D_FOURIER)

        w_o = w_out[l].astype(BF16)
        w_r = jnp.zeros((d, LANES), F32).at[:, :N_EXPERTS].set(w_router[l])
        x2d = x.reshape(t, d)
        x1, xm, scores = _outproj(x2d, mread.reshape(t, D_MLSTM), four, w_o[:D_MLSTM], w_o[D_MLSTM:],
                                  mod_x[2], ln1_g[l][None, :], ln1_b[l][None, :], mod_x[3], mod_x[4],
                                  w_r, n)

        bias_row = jnp.zeros((1, LANES), F32).at[0, :N_EXPERTS].set(router_bias[l])
        idx, gw, rank, counts = _route(scores, bias_row)
        sizes = counts[0, :N_EXPERTS].astype(jnp.int32)
        padded = (sizes + ROW_BLK - 1) // ROW_BLK * ROW_BLK
        pad_end = jnp.cumsum(padded)
        pad_start = (pad_end - padded).astype(jnp.int32)
        blk_first = jnp.arange(n_rows // ROW_BLK, dtype=jnp.int32) * ROW_BLK
        block_e = jnp.minimum(jnp.sum(blk_first[:, None] >= pad_end[None, :], axis=1),
                              N_EXPERTS - 1).astype(jnp.int32)
        block_valid = jnp.clip(pad_start[block_e] + sizes[block_e] - blk_first, 0, ROW_BLK).astype(jnp.int32)
        block_valid = jnp.where(blk_first < pad_end[-1], block_valid, 0)
        ps_row = jnp.zeros((1, LANES), F32).at[0, :N_EXPERTS].set(pad_start.astype(F32))
        dest = _dest(idx, rank, ps_row)
        dest_t = dest.T
        dest3 = dest_t.reshape(TOP_K, t // SC_ROWS, SC_ROWS).transpose(1, 0, 2)

        xs = _sc_dispatch(xm, dest3, n_rows)
        yb = _experts(block_e, block_valid, xs, w_expert_gu[l].astype(BF16), w_expert_down[l].astype(BF16))
        yg = _sc_gather(yb, dest_t.reshape(n_assign // SC_ROWS, SC_ROWS))
        x = _combine(yg, gw, x1, xm, mod_x[5],
                     w_shared_gu[l].astype(BF16), w_shared_down[l].astype(BF16),
                     ln2_g[l][None, :], ln2_b[l][None, :], n).reshape(bsz, n, d)
        assert depth == 1
    return x
```

```python
import functools
import math

import numpy as np
import jax
import jax.numpy as jnp
from jax import lax
from jax.experimental import pallas as pl
from jax.experimental.pallas import tpu as pltpu
from jax.experimental.pallas import tpu_sc as plsc

F32 = jnp.float32
BF16 = jnp.bfloat16
HIGHEST = lax.Precision.HIGHEST

D_MODEL = 1024
GRID_W = 64
N_HEADS = 4
D_HEAD = 128
D_MLSTM = N_HEADS * D_HEAD
D_FOURIER = 512
FOURIER_GROUP = 64
N_GATE_COLS = 16
CHUNK = 128
N_EXPERTS = 64
TOP_K = 8
D_EXPERT = 256
ROUTED_SCALE = 2.5
DEEPNORM_ALPHA = 2.0 ** 0.25
LN_EPS = 1e-5

LANES = 128
SUBLANES = 8
VMEM_LIMIT = 48 * 1024 * 1024

COL_QK = 0
COL_V = 2 * D_MLSTM
COL_O = 3 * D_MLSTM
COL_F = 4 * D_MLSTM
COL_G = COL_F + D_FOURIER
PROJ_W = COL_G + LANES

TM = 512
ROW_BLK = 512
ROW_HALF = 256
PACK_W = D_MODEL // 2
SC_ROWS = 128
SC_GATHER_ROWS = 64
CONV_PAD = 72
CONV_RB = 256
DFT_R = 64


def _cparams(sem, vmem=VMEM_LIMIT):
    return pltpu.CompilerParams(dimension_semantics=sem, vmem_limit_bytes=vmem)


def _layer_norm(x):
    mu = jnp.mean(x, axis=-1, keepdims=True)
    xc = x - mu
    var = jnp.mean(xc * xc, axis=-1, keepdims=True)
    return xc * lax.rsqrt(var + LN_EPS)


def _silu(x):
    return x * jax.nn.sigmoid(x)


def _pack_rows(v):
    as_bits = lambda u: lax.bitcast_convert_type(u.astype(BF16).astype(F32), jnp.uint32)
    lo = lax.shift_right_logical(as_bits(v[:, :PACK_W]), jnp.uint32(16))
    return lax.bitcast_convert_type(as_bits(v[:, PACK_W:]) | lo, jnp.int32)


def _unpack_bf16(w):
    lo = lax.bitcast_convert_type(w.astype(jnp.int16), BF16)
    hi = lax.bitcast_convert_type(lax.shift_right_logical(w, jnp.int32(16)).astype(jnp.int16), BF16)
    return jnp.concatenate([lo, hi], axis=-1)


def _unpack_rows(w):
    u = lax.bitcast_convert_type(w, jnp.uint32)
    lo = lax.bitcast_convert_type(lax.shift_left(u, jnp.uint32(16)), F32)
    hi = lax.bitcast_convert_type(u & jnp.uint32(0xFFFF0000), F32)
    return jnp.concatenate([lo, hi], axis=1)


def _ada_kernel(c_ref, w_ref, b_ref, o_ref):
    s = _silu(c_ref[...])
    o_ref[...] = jnp.dot(s, w_ref[...], precision=HIGHEST, preferred_element_type=F32) + b_ref[...]


def _ada(cvec, w_ada, b_ada):
    rows, d = cvec.shape
    n_out = w_ada.shape[1]
    tn = 1024
    return pl.pallas_call(
        _ada_kernel,
        out_shape=jax.ShapeDtypeStruct((rows, n_out), F32),
        grid=(n_out // tn,),
        in_specs=[pl.BlockSpec((rows, d), lambda j: (0, 0)),
                  pl.BlockSpec((d, tn), lambda j: (0, j)),
                  pl.BlockSpec((1, tn), lambda j: (0, j))],
        out_specs=pl.BlockSpec((rows, tn), lambda j: (0, j)),
        compiler_params=_cparams(("arbitrary",)),
        name="ada",
    )(cvec, w_ada, b_ada)


def _inproj_kernel(x_ref, sh_ref, sc_ref, w_ref, y_ref):
    u = _layer_norm(x_ref[...]) * (1.0 + sc_ref[0]) + sh_ref[0]
    y_ref[...] = jnp.dot(u.astype(BF16), w_ref[...], preferred_element_type=F32)


def _inproj(x2d, shift, scale, w, n_per_batch):
    t, d = x2d.shape
    tm = min(TM, n_per_batch)
    per = n_per_batch // tm
    return pl.pallas_call(
        _inproj_kernel,
        out_shape=jax.ShapeDtypeStruct((t, PROJ_W), F32),
        grid=(t // tm,),
        in_specs=[pl.BlockSpec((tm, d), lambda i: (i, 0)),
                  pl.BlockSpec((1, 1, d), lambda i: (i // per, 0, 0)),
                  pl.BlockSpec((1, 1, d), lambda i: (i // per, 0, 0)),
                  pl.BlockSpec((d, PROJ_W), lambda i: (0, 0))],
        out_specs=pl.BlockSpec((tm, PROJ_W), lambda i: (i, 0)),
        compiler_params=_cparams(("arbitrary",)),
        name="inproj",
    )(x2d, shift, scale, w)


def _conv_kernel(y_ref, w_ref, b_ref, o_ref, pad_ref, *, n, grid2d):
    zeros = jnp.zeros((CONV_PAD, LANES), F32)
    pad_ref[0:CONV_PAD, :] = zeros
    pad_ref[CONV_PAD + n:CONV_PAD + n + CONV_PAD, :] = zeros
    pad_ref[CONV_PAD:CONV_PAD + n, :] = y_ref[0]
    k_scale = jnp.where(pl.program_id(1) >= N_HEADS, D_HEAD ** -0.5, 1.0).astype(F32)
    bias = b_ref[...]
    rb = min(CONV_RB, n)
    col = lax.broadcasted_iota(jnp.int32, (rb, LANES), 0) % GRID_W
    not_first = col >= 1
    not_last = col <= GRID_W - 2
    row_taps = (0, 1, 2) if grid2d else (1,)
    for blk in range(n // rb):
        r0 = CONV_PAD + blk * rb
        acc = jnp.zeros((rb, LANES), F32)
        for dj in range(3):
            part = jnp.zeros((rb, LANES), F32)
            for di in row_taps:
                off = (di - 1) * GRID_W + (dj - 1)
                tap = w_ref[di * 3 + dj:di * 3 + dj + 1, :]
                part = part + tap * pad_ref[r0 + off:r0 + off + rb, :]
            if grid2d and dj == 0:
                part = jnp.where(not_first, part, 0.0)
            if grid2d and dj == 2:
                part = jnp.where(not_last, part, 0.0)
            acc = acc + part
        val = _silu(acc + bias) * k_scale
        o_ref[0, blk * rb:(blk + 1) * rb, :] = val.astype(o_ref.dtype)


def _conv(y3d, conv_w9, conv_b, grid2d):
    b, n, _ = y3d.shape
    n_ct = 2 * D_MLSTM // LANES
    return pl.pallas_call(
        functools.partial(_conv_kernel, n=n, grid2d=grid2d),
        out_shape=jax.ShapeDtypeStruct((b, n, 2 * D_MLSTM), BF16),
        grid=(b, n_ct),
        in_specs=[pl.BlockSpec((1, n, LANES), lambda i, c: (i, 0, c)),
                  pl.BlockSpec((9, LANES), lambda i, c: (0, c)),
                  pl.BlockSpec((1, LANES), lambda i, c: (0, c))],
        out_specs=pl.BlockSpec((1, n, LANES), lambda i, c: (i, 0, c)),
        scratch_shapes=[pltpu.VMEM((n + 2 * CONV_PAD, LANES), F32)],
        compiler_params=_cparams(("arbitrary", "arbitrary")),
        name="conv2d" if grid2d else "conv1d",
    )(y3d, conv_w9, conv_b)


def _gates_kernel(g_ref, gb_ref, cv_ref, rv_ref, *, nc):
    row = lax.broadcasted_iota(jnp.int32, (CHUNK, CHUNK), 0)
    col = lax.broadcasted_iota(jnp.int32, (CHUNK, CHUNK), 1)
    lower = col <= row
    upper = col >= row
    tri_l = lower.astype(F32)
    tri_u = upper.astype(F32)
    sub = lax.broadcasted_iota(jnp.int32, (SUBLANES, LANES), 0)
    neg_inf = jnp.float32(-jnp.inf)

    def body(c, carry):
        r0 = pl.multiple_of(c * CHUNK, CHUNK)
        pre = g_ref[0, pl.ds(r0, CHUNK), :] + gb_ref[...]
        lf = -(jnp.maximum(-pre, 0.0) + jnp.log1p(jnp.exp(-jnp.abs(pre))))
        b_f = jnp.dot(tri_l, lf, precision=HIGHEST, preferred_element_type=F32)
        b_b = jnp.dot(tri_u, lf, precision=HIGHEST, preferred_element_type=F32)
        pre_t = pre.T
        b_f_t = b_f.T
        b_b_t = b_b.T
        for h in range(N_HEADS):
            i_f, f_f, i_b, f_b = h, N_HEADS + h, 2 * N_HEADS + h, 3 * N_HEADS + h
            bf = b_f[:, f_f:f_f + 1]
            bb = b_b[:, f_b:f_b + 1]
            af = pre[:, i_f:i_f + 1] - bf
            ab = pre[:, i_b:i_b + 1] - bb
            af_row = pre_t[i_f:i_f + 1, :] - b_f_t[f_f:f_f + 1, :]
            ab_row = pre_t[i_b:i_b + 1, :] - b_b_t[f_b:f_b + 1, :]
            cmf = jnp.max(jnp.where(lower, af_row, neg_inf), axis=-1, keepdims=True)
            cmb = jnp.max(jnp.where(upper, ab_row, neg_inf), axis=-1, keepdims=True)
            cv = jnp.where(col == 0, af, 0.0)
            cv = jnp.where(col == 1, cmf, cv)
            cv = jnp.where(col == 2, bf, cv)
            cv = jnp.where(col == 3, ab, cv)
            cv = jnp.where(col == 4, cmb, cv)
            cv = jnp.where(col == 5, bb, cv)
            cv_ref[0, h, pl.ds(r0, CHUNK), :] = cv
            rv = jnp.where(sub == 0, af_row, jnp.where(sub == 1, ab_row, 0.0))
            rv_ref[0, h, pl.ds(c, 1), :, :] = rv[None]
        return carry

    lax.fori_loop(0, nc, body, 0)


def _gates(y3d, gate_b_row):
    b, n, _ = y3d.shape
    nc = n // CHUNK
    return pl.pallas_call(
        functools.partial(_gates_kernel, nc=nc),
        out_shape=(jax.ShapeDtypeStruct((b, N_HEADS, n, LANES), F32),
                   jax.ShapeDtypeStruct((b, N_HEADS, nc, SUBLANES, LANES), F32)),
        grid=(b,),
        in_specs=[pl.BlockSpec((1, n, LANES), lambda i: (i, 0, COL_G // LANES)),
                  pl.BlockSpec((1, LANES), lambda i: (0, 0))],
        out_specs=(pl.BlockSpec((1, N_HEADS, n, LANES), lambda i: (i, 0, 0, 0)),
                   pl.BlockSpec((1, N_HEADS, nc, SUBLANES, LANES), lambda i: (i, 0, 0, 0, 0))),
        compiler_params=_cparams(("arbitrary",)),
        name="gates",
    )(y3d, gate_b_row)


def _mlstm_kernel(q_ref, k_ref, v_ref, o_ref, cv_ref, rv_ref, c0_ref, m0_ref, hg_ref,
                  out_ref, cfin_ref, mfin_ref, hf_ref, hb_ref, cst_ref, mst_ref, *, nc):
    cst_ref[...] = c0_ref[0, 0]
    mst_ref[...] = m0_ref[0, 0]
    row = lax.broadcasted_iota(jnp.int32, (CHUNK, CHUNK), 0)
    col = lax.broadcasted_iota(jnp.int32, (CHUNK, CHUNK), 1)
    ones_col = (col == 0).astype(F32)

    def one_chunk(c, d, mask, last, h_ref):
        r0 = pl.multiple_of(c * CHUNK, CHUNK)
        q = q_ref[0, pl.ds(r0, CHUNK), :]
        k = k_ref[0, pl.ds(r0, CHUNK), :]
        v_aug = jnp.concatenate([v_ref[0, pl.ds(r0, CHUNK), :], ones_col], axis=1)
        cv = cv_ref[0, 0, pl.ds(r0, CHUNK), :]
        a_col = cv[:, 3 * d:3 * d + 1]
        cm_col = cv[:, 3 * d + 1:3 * d + 2]
        b_col = cv[:, 3 * d + 2:3 * d + 3]
        a_row = rv_ref[0, 0, pl.ds(c, 1), :, :][0, d:d + 1, :]
        m_prev = mst_ref[d, 0:1, 0:1]
        m_col = jnp.maximum(m_prev, cm_col)
        dmat = jnp.where(mask, jnp.exp(a_row - m_col), 0.0)
        s = lax.dot_general(q, k, (((1,), (1,)), ((), ())), preferred_element_type=F32) * dmat
        ct = cst_ref[d]
        h_aug = (jnp.dot(s.astype(BF16), v_aug.astype(BF16), preferred_element_type=F32)
                 + jnp.exp(m_prev - m_col)
                 * jnp.dot(q, ct.astype(BF16), preferred_element_type=F32))
        den = h_aug[:, D_HEAD:D_HEAD + 1]
        floor = jnp.exp(-b_col - m_col)
        h_ref[pl.ds(r0, CHUNK), :] = h_aug[:, :D_HEAD] / jnp.maximum(jnp.abs(den), floor)
        m_last = m_col[last:last + 1, :]
        m_new = b_col[last:last + 1, :] + m_last
        wv = (jnp.exp(a_col - m_last) * v_aug).astype(BF16)
        cst_ref[d] = (jnp.exp(m_prev - m_last) * ct
                      + lax.dot_general(k, wv, (((0,), (0,)), ((), ())), preferred_element_type=F32))
        mst_ref[d] = jnp.broadcast_to(m_new, (SUBLANES, LANES))

    def body(c, carry):
        one_chunk(c, 0, col <= row, CHUNK - 1, hf_ref)
        one_chunk(nc - 1 - c, 1, col >= row, 0, hb_ref)
        return carry

    lax.fori_loop(0, nc, body, 0)
    cfin_ref[0, 0] = cst_ref[...]
    mfin_ref[0, 0] = mst_ref[...]
    hn = _layer_norm(hf_ref[...] + hb_ref[...])
    out_ref[0] = (hn * hg_ref[...] * jax.nn.sigmoid(o_ref[0])).astype(out_ref.dtype)


def _mlstm(qk, y3d, colv, rowv, c0, m0, head_g):
    b, n, _ = qk.shape
    nc = n // CHUNK
    v_blk = COL_V // LANES
    o_blk = COL_O // LANES
    tok = lambda off: pl.BlockSpec((1, n, LANES), lambda i, h: (i, 0, off + h))
    return pl.pallas_call(
        functools.partial(_mlstm_kernel, nc=nc),
        out_shape=(jax.ShapeDtypeStruct((b, n, D_MLSTM), BF16),
                   jax.ShapeDtypeStruct((b, N_HEADS, 2, D_HEAD, 2 * D_HEAD), F32),
                   jax.ShapeDtypeStruct((b, N_HEADS, 2, SUBLANES, LANES), F32)),
        grid=(b, N_HEADS),
        in_specs=[tok(0), tok(N_HEADS), tok(v_blk), tok(o_blk),
                  pl.BlockSpec((1, 1, n, LANES), lambda i, h: (i, h, 0, 0)),
                  pl.BlockSpec((1, 1, nc, SUBLANES, LANES), lambda i, h: (i, h, 0, 0, 0)),
                  pl.BlockSpec((1, 1, 2, D_HEAD, 2 * D_HEAD), lambda i, h: (i, h, 0, 0, 0)),
                  pl.BlockSpec((1, 1, 2, SUBLANES, LANES), lambda i, h: (i, h, 0, 0, 0)),
                  pl.BlockSpec((1, LANES), lambda i, h: (0, h))],
        out_specs=(pl.BlockSpec((1, n, LANES), lambda i, h: (i, 0, h)),
                   pl.BlockSpec((1, 1, 2, D_HEAD, 2 * D_HEAD), lambda i, h: (i, h, 0, 0, 0)),
                   pl.BlockSpec((1, 1, 2, SUBLANES, LANES), lambda i, h: (i, h, 0, 0, 0))),
        scratch_shapes=[pltpu.VMEM((n, LANES), F32), pltpu.VMEM((n, LANES), F32),
                        pltpu.VMEM((2, D_HEAD, 2 * D_HEAD), F32),
                        pltpu.VMEM((2, SUBLANES, LANES), F32)],
        compiler_params=_cparams(("arbitrary", "arbitrary")),
        name="mlstm",
    )(qk, qk, y3d, y3d, colv, rowv, c0, m0, head_g)


def _dft_tables():
    r = DFT_R
    idx = np.arange(r)
    cg = np.cos(2 * np.pi * np.outer(idx, idx) / FOURIER_GROUP)
    sg = np.sin(2 * np.pi * np.outer(idx, idx) / FOURIER_GROUP)
    n_grp = D_FOURIER // FOURIER_GROUP
    bdc = np.kron(np.eye(n_grp), cg) / 8.0
    bds = np.kron(np.eye(n_grp), sg) / 8.0
    k1 = idx[:, None]
    a = idx[None, :]
    m1 = np.zeros((r, 2 * r, 2 * r))
    for b2 in range(r):
        th = 2 * np.pi * k1 * (r * a + b2) / (r * r)
        ec, es = np.cos(th), np.sin(th)
        m1[b2] = np.block([[ec, -es], [es, ec]]) / 8.0
    th2 = 2 * np.pi * np.outer(idx, idx) / r
    m2 = np.concatenate([np.cos(th2), -np.sin(th2)], axis=1) / 8.0
    as_bf16 = lambda t: jnp.asarray(t, dtype=F32).astype(BF16)
    return as_bf16(bdc), as_bf16(bds), as_bf16(m1), as_bf16(m2)


FOUR_BB = 8


def _four1_kernel(f_ref, bdc_ref, bds_ref, m1_ref, z_ref):
    for j in range(FOUR_BB):
        fb = f_ref[0, j].astype(BF16)
        p = jnp.dot(fb, bdc_ref[...], preferred_element_type=F32)
        q = jnp.dot(fb, bds_ref[...], preferred_element_type=F32)
        x2 = jnp.concatenate([p, q], axis=0).astype(BF16)
        z_ref[0, j] = jnp.dot(m1_ref[j], x2, preferred_element_type=F32).astype(z_ref.dtype)


def _four2_kernel(z_ref, m2_ref, o_ref):
    for j in range(FOUR_BB):
        o_ref[0, j] = jnp.dot(m2_ref[...], z_ref[0, j], preferred_element_type=F32)


def _fourier(f_t, tables):
    bdc, bds, m1, m2 = tables
    b = f_t.shape[0]
    r = DFT_R
    nb = r // FOUR_BB
    z = pl.pallas_call(
        _four1_kernel,
        out_shape=jax.ShapeDtypeStruct((b, r, 2 * r, D_FOURIER), BF16),
        grid=(b, nb),
        in_specs=[pl.BlockSpec((1, FOUR_BB, r, D_FOURIER), lambda i, j: (i, j, 0, 0)),
                  pl.BlockSpec((D_FOURIER, D_FOURIER), lambda i, j: (0, 0)),
                  pl.BlockSpec((D_FOURIER, D_FOURIER), lambda i, j: (0, 0)),
                  pl.BlockSpec((FOUR_BB, 2 * r, 2 * r), lambda i, j: (j, 0, 0))],
        out_specs=pl.BlockSpec((1, FOUR_BB, 2 * r, D_FOURIER), lambda i, j: (i, j, 0, 0)),
        compiler_params=_cparams(("arbitrary", "arbitrary")),
        name="four1",
    )(f_t, bdc, bds, m1)
    z_t = z.reshape(b, r, 2, r, D_FOURIER).transpose(0, 3, 2, 1, 4).reshape(b, r, 2 * r, D_FOURIER)
    return pl.pallas_call(
        _four2_kernel,
        out_shape=jax.ShapeDtypeStruct((b, r, r, D_FOURIER), F32),
        grid=(b, nb),
        in_specs=[pl.BlockSpec((1, FOUR_BB, 2 * r, D_FOURIER), lambda i, j: (i, j, 0, 0)),
                  pl.BlockSpec((r, 2 * r), lambda i, j: (0, 0))],
        out_specs=pl.BlockSpec((1, FOUR_BB, r, D_FOURIER), lambda i, j: (i, j, 0, 0)),
        compiler_params=_cparams(("arbitrary", "arbitrary")),
        name="four2",
    )(z_t, m2)


def _outproj_kernel(x_ref, mr_ref, fo_ref, wm_ref, wf_ref, g1_ref, lg_ref, lb_ref,
                    sh_ref, sc_ref, wr_ref, x1_ref, xm_ref, s_ref):
    mix = (jnp.dot(mr_ref[...], wm_ref[...], preferred_element_type=F32)
           + jnp.dot(fo_ref[...].astype(BF16), wf_ref[...], preferred_element_type=F32))
    x1 = _layer_norm(DEEPNORM_ALPHA * x_ref[...] + g1_ref[0] * mix) * lg_ref[...] + lb_ref[...]
    x1_ref[...] = x1
    xm = _layer_norm(x1) * (1.0 + sc_ref[0]) + sh_ref[0]
    xm_ref[...] = _pack_rows(xm)
    logits = jnp.dot(xm.astype(BF16), wr_ref[...], preferred_element_type=F32)
    s_ref[...] = jax.nn.sigmoid(logits)


def _outproj(x2d, mread, four, w_m, w_f, gate1, ln_g, ln_b, shift2, scale2, w_router, n_per_batch):
    t, d = x2d.shape
    per = n_per_batch // TM
    tokd = lambda w: pl.BlockSpec((TM, w), lambda i: (i, 0))
    full = lambda r, c: pl.BlockSpec((r, c), lambda i: (0, 0))
    mod = pl.BlockSpec((1, 1, d), lambda i: (i // per, 0, 0))
    return pl.pallas_call(
        _outproj_kernel,
        out_shape=(jax.ShapeDtypeStruct((t, d), F32),
                   jax.ShapeDtypeStruct((t, PACK_W), jnp.int32),
                   jax.ShapeDtypeStruct((t, LANES), F32)),
        grid=(t // TM,),
        in_specs=[tokd(d), tokd(D_MLSTM), tokd(D_FOURIER), full(D_MLSTM, d), full(D_FOURIER, d),
                  mod, full(1, d), full(1, d), mod, mod, full(d, LANES)],
        out_specs=(tokd(d), tokd(PACK_W), tokd(LANES)),
        compiler_params=_cparams(("arbitrary",)),
        name="outproj",
    )(x2d, mread, four, w_m, w_f, gate1, ln_g, ln_b, shift2, scale2, w_router)


def _route_kernel(s_ref, b_ref, idx_ref, gw_ref, rank_ref, cnt_ref, carry_ref):
    @pl.when(pl.program_id(0) == 0)
    def _():
        carry_ref[...] = jnp.zeros_like(carry_ref)

    tm = s_ref.shape[0]
    s = s_ref[...]
    lane_i = lax.broadcasted_iota(jnp.int32, (tm, LANES), 1)
    lane = lane_i.astype(F32)
    neg_inf = jnp.float32(-jnp.inf)
    sb = jnp.where(lane_i < N_EXPERTS, s + b_ref[...], neg_inf)
    picks, sels = [], []
    taken = jnp.zeros((tm, LANES), F32)
    for _ in range(TOP_K):
        best = jnp.max(sb, axis=-1, keepdims=True)
        pick = jnp.min(jnp.where(sb == best, lane, float(LANES)), axis=-1, keepdims=True)
        onehot = lane == pick
        sels.append(jnp.sum(jnp.where(onehot, s, 0.0), axis=-1, keepdims=True))
        picks.append(pick)
        sb = jnp.where(onehot, neg_inf, sb)
        taken = taken + onehot.astype(F32)
    r_i = lax.broadcasted_iota(jnp.int32, (tm, tm), 0)
    c_i = lax.broadcasted_iota(jnp.int32, (tm, tm), 1)
    before = (c_i < r_i).astype(BF16)
    rank_all = carry_ref[0:1, :] + jnp.dot(before, taken.astype(BF16), preferred_element_type=F32)
    carry_ref[...] = carry_ref[...] + jnp.sum(taken, axis=0, keepdims=True)
    cnt_ref[...] = carry_ref[...]
    total = sels[0]
    for v in sels[1:]:
        total = total + v
    k_i = lax.broadcasted_iota(jnp.int32, (tm, TOP_K), 1)
    gw_out = jnp.zeros((tm, TOP_K), F32)
    idx_cols = jnp.zeros((tm, LANES), F32)
    rank_cols = jnp.zeros((tm, LANES), F32)
    for j in range(TOP_K):
        rank_j = jnp.sum(jnp.where(lane == picks[j], rank_all, 0.0), axis=-1, keepdims=True)
        gw_out = jnp.where(k_i == j, ROUTED_SCALE * sels[j] / total, gw_out)
        idx_cols = jnp.where(lane_i == j, picks[j], idx_cols)
        rank_cols = jnp.where(lane_i == j, rank_j, rank_cols)
    gw_ref[...] = gw_out
    idx_ref[...] = idx_cols.T[:TOP_K, :].astype(jnp.int32)
    rank_ref[...] = rank_cols.T[:TOP_K, :].astype(jnp.int32)


def _route(scores, bias_row):
    t = scores.shape[0]
    tok = lambda w: pl.BlockSpec((TM, w), lambda i: (i, 0))
    slot_major = pl.BlockSpec((TOP_K, TM), lambda i: (0, i))
    return pl.pallas_call(
        _route_kernel,
        out_shape=(jax.ShapeDtypeStruct((TOP_K, t), jnp.int32),
                   jax.ShapeDtypeStruct((t, TOP_K), F32),
                   jax.ShapeDtypeStruct((TOP_K, t), jnp.int32),
                   jax.ShapeDtypeStruct((SUBLANES, LANES), F32)),
        grid=(t // TM,),
        in_specs=[tok(LANES), pl.BlockSpec((1, LANES), lambda i: (0, 0))],
        out_specs=(slot_major, tok(TOP_K), slot_major,
                   pl.BlockSpec((SUBLANES, LANES), lambda i: (0, 0))),
        scratch_shapes=[pltpu.VMEM((SUBLANES, LANES), F32)],
        compiler_params=_cparams(("arbitrary",)),
        name="route",
    )(scores, bias_row)


TN_DEST = 2048


def _dest_kernel(ps_ref, idx_ref, rank_ref, dest_ref):
    idx = idx_ref[...]
    start = jnp.zeros(idx.shape, jnp.int32)
    for e in range(N_EXPERTS):
        start = jnp.where(idx == e, ps_ref[e], start)
    dest_ref[...] = start + rank_ref[...]


def _dest(pad_start, idx_t, rank_t):
    t = idx_t.shape[1]
    blk = lambda: pl.BlockSpec((TOP_K, TN_DEST), lambda i, ps: (0, i))
    grid_spec = pltpu.PrefetchScalarGridSpec(
        num_scalar_prefetch=1, grid=(t // TN_DEST,), in_specs=[blk(), blk()], out_specs=blk())
    return pl.pallas_call(
        _dest_kernel,
        out_shape=jax.ShapeDtypeStruct((TOP_K, t), jnp.int32),
        grid_spec=grid_spec,
        compiler_params=_cparams(("arbitrary",)),
        name="dest",
    )(pad_start, idx_t, rank_t)


def _sc_worker_id(info):
    return lax.axis_index("s") * info.num_cores + lax.axis_index("c")


def _sc_dispatch(xm, dest3, n_rows):
    t, w = xm.shape
    info = plsc.get_sparse_core_info()
    n_workers = info.num_cores * info.num_subcores
    per_worker = t // SC_ROWS // n_workers
    mesh = plsc.VectorSubcoreMesh(core_axis_name="c", subcore_axis_name="s")

    @functools.partial(
        pl.kernel, mesh=mesh,
        out_type=jax.ShapeDtypeStruct((n_rows, w), xm.dtype),
        scratch_types=[pltpu.VMEM((TOP_K, SC_ROWS), jnp.int32), pltpu.VMEM((SC_ROWS, w), xm.dtype)],
        name="sc_dispatch",
    )
    def run(xm_hbm, dest_hbm, xs_hbm, idx_v, rows_v):
        first = _sc_worker_id(info) * per_worker

        @pl.loop(0, per_worker)
        def _(ci):
            chunk = first + ci
            pltpu.sync_copy(dest_hbm.at[chunk], idx_v)
            pltpu.sync_copy(xm_hbm.at[pl.ds(chunk * SC_ROWS, SC_ROWS)], rows_v)
            for j in range(TOP_K):
                pltpu.sync_copy(rows_v, xs_hbm.at[idx_v.at[j]])

    return run(xm, dest3)


def _sc_gather(table, idx2):
    n_chunks, rows = idx2.shape
    w = table.shape[1]
    info = plsc.get_sparse_core_info()
    n_workers = info.num_cores * info.num_subcores
    per_worker = n_chunks // n_workers
    assert per_worker % 2 == 0
    mesh = plsc.VectorSubcoreMesh(core_axis_name="c", subcore_axis_name="s")

    @functools.partial(
        pl.kernel, mesh=mesh,
        out_type=jax.ShapeDtypeStruct((n_chunks * rows, w), table.dtype),
        scratch_types=[pltpu.VMEM((2, rows), jnp.int32), pltpu.VMEM((2, rows, w), table.dtype),
                       pltpu.SemaphoreType.DMA, pltpu.SemaphoreType.DMA],
        name="sc_gather",
    )
    def run(table_hbm, idx_hbm, out_hbm, idx_v, rows_v, sem0, sem1):
        first = _sc_worker_id(info) * per_worker
        sems = (sem0, sem1)

        def gather(b):
            return pltpu.make_async_copy(table_hbm.at[idx_v.at[b]], rows_v.at[b], sems[b])

        def start(chunk, b):
            pltpu.sync_copy(idx_hbm.at[chunk], idx_v.at[b])
            gather(b).start()

        def finish(chunk, b):
            gather(b).wait()
            pltpu.sync_copy(rows_v.at[b], out_hbm.at[pl.ds(chunk * rows, rows)])

        start(first, 0)

        @pl.loop(0, per_worker, step=2)
        def _(ci):
            chunk = first + ci
            start(chunk + 1, 1)
            finish(chunk, 0)

            @pl.when(ci + 2 < per_worker)
            def _():
                start(chunk + 2, 0)

            finish(chunk + 1, 1)

    return run(table, idx2)


def _expert_kernel(be_ref, bv_ref, xs_ref, wgu_ref, wd_ref, y_ref, wgu_bf, wd_bf):
    i = pl.program_id(0)
    n_valid = bv_ref[i]

    @pl.when((i == 0) | (be_ref[i] != be_ref[jnp.maximum(i - 1, 0)]))
    def _():
        wgu_bf[...] = wgu_ref[0].astype(BF16)
        wd_bf[...] = wd_ref[0].astype(BF16)

    @pl.when(n_valid > 0)
    def _():
        for h in range(ROW_BLK // ROW_HALF):
            r0 = h * ROW_HALF
            row = lax.broadcasted_iota(jnp.int32, (ROW_HALF, PACK_W), 0) + r0
            x = _unpack_bf16(jnp.where(row < n_valid, xs_ref[r0:r0 + ROW_HALF, :], 0))
            gu = jnp.dot(x, wgu_bf[...], preferred_element_type=F32)
            act = _silu(gu[:, :D_EXPERT]) * gu[:, D_EXPERT:]
            y_ref[r0:r0 + ROW_HALF, :] = _pack_rows(
                jnp.dot(act.astype(BF16), wd_bf[...], preferred_element_type=F32))


def _experts(block_e, block_valid, xs, w_gu, w_down):
    n_rows, w = xs.shape
    d = D_MODEL
    grid_spec = pltpu.PrefetchScalarGridSpec(
        num_scalar_prefetch=2,
        grid=(n_rows // ROW_BLK,),
        in_specs=[pl.BlockSpec((ROW_BLK, w), lambda i, be, bv: (i, 0)),
                  pl.BlockSpec((1, d, 2 * D_EXPERT), lambda i, be, bv: (be[i], 0, 0)),
                  pl.BlockSpec((1, D_EXPERT, d), lambda i, be, bv: (be[i], 0, 0))],
        out_specs=pl.BlockSpec((ROW_BLK, w), lambda i, be, bv: (i, 0)),
        scratch_shapes=[pltpu.VMEM((d, 2 * D_EXPERT), BF16), pltpu.VMEM((D_EXPERT, d), BF16)],
    )
    return pl.pallas_call(
        _expert_kernel,
        out_shape=jax.ShapeDtypeStruct((n_rows, w), jnp.int32),
        grid_spec=grid_spec,
        compiler_params=_cparams(("arbitrary",)),
        name="experts",
    )(block_e, block_valid, xs, w_gu, w_down)


TM_COMBINE = 256


def _combine_kernel(*refs):
    yg_refs = refs[:TOP_K]
    gw_ref, x1_ref, xm_ref, g2_ref, wsgu_ref, wsd_ref, lg_ref, lb_ref, out_ref = refs[TOP_K:]
    gu = jnp.dot(_unpack_bf16(xm_ref[...]), wsgu_ref[...], preferred_element_type=F32)
    act = _silu(gu[:, :D_EXPERT]) * gu[:, D_EXPERT:]
    acc = jnp.dot(act.astype(BF16), wsd_ref[...], preferred_element_type=F32)
    gw = gw_ref[...]
    for j in range(TOP_K):
        acc = acc + gw[:, j:j + 1] * _unpack_rows(yg_refs[j][...])
    z = DEEPNORM_ALPHA * x1_ref[...] + g2_ref[0] * acc
    out_ref[...] = _layer_norm(z) * lg_ref[...] + lb_ref[...]


def _combine(yg, gw, x1, xm, gate2, w_sgu, w_sd, ln_g, ln_b, n_per_batch):
    t, d = x1.shape
    tm = TM_COMBINE
    per = n_per_batch // tm
    steps = t // tm
    tok = lambda w: pl.BlockSpec((tm, w), lambda i: (i, 0))
    full = lambda r, c: pl.BlockSpec((r, c), lambda i: (0, 0))
    slot = lambda j: pl.BlockSpec((tm, PACK_W), lambda i: (j * steps + i, 0))
    return pl.pallas_call(
        _combine_kernel,
        out_shape=jax.ShapeDtypeStruct((t, d), F32),
        grid=(steps,),
        in_specs=[slot(j) for j in range(TOP_K)]
        + [tok(TOP_K), tok(d), tok(PACK_W), pl.BlockSpec((1, 1, d), lambda i: (i // per, 0, 0)),
           full(d, 2 * D_EXPERT), full(D_EXPERT, d), full(1, d), full(1, d)],
        out_specs=tok(d),
        compiler_params=_cparams(("arbitrary",)),
        name="combine",
    )(*([yg] * TOP_K), gw, x1, xm, gate2, w_sgu, w_sd, ln_g, ln_b)


def _mixer_heads(stream, shift, scale, w_in_k, conv_w9, conv_b, gate_b_row, head_g, c0, m0, grid2d):
    b, n, d = stream.shape
    y = _inproj(stream.reshape(b * n, d), shift, scale, w_in_k, n).reshape(b, n, PROJ_W)
    qk = _conv(y, conv_w9, conv_b, grid2d)
    colv, rowv = _gates(y, gate_b_row)
    mread, c_fin, m_fin = _mlstm(qk, y, colv, rowv, c0, m0, head_g)
    return y, mread, c_fin, m_fin


def kernel(x, c, ctx, c_ctx, w_ada, b_ada, w_in, conv_w, conv_b, gate_b, head_g, w_out, ln1_g, ln1_b,
           w_router, router_bias, w_expert_gu, w_expert_down, w_shared_gu, w_shared_down, ln2_g, ln2_b):
    bsz, n, d = x.shape
    n_ctx = ctx.shape[1]
    depth = w_ada.shape[0]
    t = bsz * n
    tables = _dft_tables()
    n_assign = t * TOP_K
    n_rows = n_assign + N_EXPERTS * ROW_BLK
    for l in range(depth):
        cvec = jnp.zeros((SUBLANES, d), F32).at[:bsz].set(c).at[bsz].set(c_ctx)
        mod = _ada(cvec, w_ada[l], b_ada[l][None, :])
        mods = mod.reshape(SUBLANES, 6, 1, d)
        mod_x = [mods[:bsz, j] for j in range(6)]
        mod_c = [jnp.broadcast_to(mods[bsz:bsz + 1, j], (bsz, 1, d)) for j in range(6)]

        wl = w_in[l]
        g0 = 4 * D_MLSTM
        w_in_k = jnp.concatenate(
            [wl[:, :g0], wl[:, g0 + N_GATE_COLS:], wl[:, g0:g0 + N_GATE_COLS],
             jnp.zeros((d, LANES - N_GATE_COLS), F32)], axis=1).astype(BF16)
        conv_w9 = conv_w[l].reshape(9, 2 * D_MLSTM)
        conv_b_row = conv_b[l][None, :]
        gate_b_row = jnp.zeros((1, LANES), F32).at[0, :N_GATE_COLS].set(gate_b[l].reshape(-1))
        head_g_row = head_g[l][None, :]

        c0 = jnp.zeros((bsz, N_HEADS, 2, D_HEAD, 2 * D_HEAD), F32)
        m0 = jnp.zeros((bsz, N_HEADS, 2, SUBLANES, LANES), F32)
        _, _, c_ctx_fin, m_ctx_fin = _mixer_heads(ctx, mod_c[0], mod_c[1], w_in_k, conv_w9, conv_b_row,
                                                  gate_b_row, head_g_row, c0, m0, grid2d=False)
        y, mread, _, _ = _mixer_heads(x, mod_x[0], mod_x[1], w_in_k, conv_w9, conv_b_row,
                                      gate_b_row, head_g_row, c_ctx_fin, m_ctx_fin, grid2d=True)

        r = DFT_R
        f_t = y[:, :, COL_F:COL_F + D_FOURIER].reshape(bsz, r, r, D_FOURIER).transpose(0, 2, 1, 3)
        four = _fourier(f_t, tables).transpose(0, 2, 1, 3).reshape(t, D_FOURIER)

        w_o = w_out[l].astype(BF16)
        w_r = jnp.zeros((d, LANES), F32).at[:, :N_EXPERTS].set(w_router[l]).astype(BF16)
        x2d = x.reshape(t, d)
        x1, xm, scores = _outproj(x2d, mread.reshape(t, D_MLSTM), four, w_o[:D_MLSTM], w_o[D_MLSTM:],
                                  mod_x[2], ln1_g[l][None, :], ln1_b[l][None, :], mod_x[3], mod_x[4],
                                  w_r, n)

        bias_row = jnp.zeros((1, LANES), F32).at[0, :N_EXPERTS].set(router_bias[l])
        idx_t, gw, rank_t, counts = _route(scores, bias_row)
        sizes = counts[0, :N_EXPERTS].astype(jnp.int32)
        padded = (sizes + ROW_BLK - 1) // ROW_BLK * ROW_BLK
        pad_end = jnp.cumsum(padded)
        pad_start = (pad_end - padded).astype(jnp.int32)
        blk_first = jnp.arange(n_rows // ROW_BLK, dtype=jnp.int32) * ROW_BLK
        block_e = jnp.minimum(jnp.sum(blk_first[:, None] >= pad_end[None, :], axis=1),
                              N_EXPERTS - 1).astype(jnp.int32)
        is_e = block_e[:, None] == jnp.arange(N_EXPERTS, dtype=jnp.int32)[None, :]
        valid_end = jnp.sum(jnp.where(is_e, (pad_start + sizes)[None, :], 0), axis=1)
        block_valid = jnp.clip(valid_end - blk_first, 0, ROW_BLK)
        block_valid = jnp.where(blk_first < pad_end[-1], block_valid, 0).astype(jnp.int32)
        dest_t = _dest(pad_start, idx_t, rank_t)
        dest3 = dest_t.reshape(TOP_K, t // SC_ROWS, SC_ROWS).transpose(1, 0, 2)

        xs = _sc_dispatch(xm, dest3, n_rows)
        yb = _experts(block_e, block_valid, xs, w_expert_gu[l], w_expert_down[l])
        yg = _sc_gather(yb, dest_t.reshape(n_assign // SC_GATHER_ROWS, SC_GATHER_ROWS))
        x = _combine(yg, gw, x1, xm, mod_x[5],
                     w_shared_gu[l].astype(BF16), w_shared_down[l].astype(BF16),
                     ln2_g[l][None, :], ln2_b[l][None, :], n).reshape(bsz, n, d)
        assert depth == 1
    return x
```

```python
import functools
import math

import numpy as np
import jax
import jax.numpy as jnp
from jax import lax
from jax.experimental import pallas as pl
from jax.experimental.pallas import tpu as pltpu
from jax.experimental.pallas import tpu_sc as plsc

F32 = jnp.float32
BF16 = jnp.bfloat16
HIGHEST = lax.Precision.HIGHEST

D_MODEL = 1024
GRID_W = 64
N_HEADS = 4
D_HEAD = 128
D_MLSTM = N_HEADS * D_HEAD
D_FOURIER = 512
FOURIER_GROUP = 64
N_GATE_COLS = 16
CHUNK = 128
N_EXPERTS = 64
TOP_K = 8
D_EXPERT = 256
ROUTED_SCALE = 2.5
DEEPNORM_ALPHA = 2.0 ** 0.25
LN_EPS = 1e-5

LANES = 128
SUBLANES = 8
VMEM_LIMIT = 48 * 1024 * 1024

COL_QK = 0
COL_V = 2 * D_MLSTM
COL_O = 3 * D_MLSTM
COL_F = 4 * D_MLSTM
COL_G = COL_F + D_FOURIER
PROJ_W = COL_G + LANES

TM = 512
ROW_BLK = 512
ROW_HALF = 256
PACK_W = D_MODEL // 2
SC_ROWS = 128
SC_GATHER_ROWS = 64
CONV_PAD = 72
CONV_RB = 256
DFT_R = 64


def _cparams(sem, vmem=VMEM_LIMIT):
    return pltpu.CompilerParams(dimension_semantics=sem, vmem_limit_bytes=vmem)


def _layer_norm(x):
    mu = jnp.mean(x, axis=-1, keepdims=True)
    xc = x - mu
    var = jnp.mean(xc * xc, axis=-1, keepdims=True)
    return xc * lax.rsqrt(var + LN_EPS)


def _silu(x):
    return x * jax.nn.sigmoid(x)


def _pack_rows(v):
    as_bits = lambda u: lax.bitcast_convert_type(u.astype(BF16).astype(F32), jnp.uint32)
    lo = lax.shift_right_logical(as_bits(v[:, :PACK_W]), jnp.uint32(16))
    return lax.bitcast_convert_type(as_bits(v[:, PACK_W:]) | lo, jnp.int32)


def _unpack_bf16(w):
    lo = lax.bitcast_convert_type(w.astype(jnp.int16), BF16)
    hi = lax.bitcast_convert_type(lax.shift_right_logical(w, jnp.int32(16)).astype(jnp.int16), BF16)
    return jnp.concatenate([lo, hi], axis=-1)


def _unpack_rows(w):
    u = lax.bitcast_convert_type(w, jnp.uint32)
    lo = lax.bitcast_convert_type(lax.shift_left(u, jnp.uint32(16)), F32)
    hi = lax.bitcast_convert_type(u & jnp.uint32(0xFFFF0000), F32)
    return jnp.concatenate([lo, hi], axis=1)


def _ada_kernel(c_ref, w_ref, b_ref, o_ref):
    s = _silu(c_ref[...])
    o_ref[...] = jnp.dot(s, w_ref[...], precision=HIGHEST, preferred_element_type=F32) + b_ref[...]


def _ada(cvec, w_ada, b_ada):
    rows, d = cvec.shape
    n_out = w_ada.shape[1]
    tn = 1024
    return pl.pallas_call(
        _ada_kernel,
        out_shape=jax.ShapeDtypeStruct((rows, n_out), F32),
        grid=(n_out // tn,),
        in_specs=[pl.BlockSpec((rows, d), lambda j: (0, 0)),
                  pl.BlockSpec((d, tn), lambda j: (0, j)),
                  pl.BlockSpec((1, tn), lambda j: (0, j))],
        out_specs=pl.BlockSpec((rows, tn), lambda j: (0, j)),
        compiler_params=_cparams(("arbitrary",)),
        name="ada",
    )(cvec, w_ada, b_ada)


def _inproj_kernel(x_ref, sh_ref, sc_ref, w_ref, y_ref):
    u = _layer_norm(x_ref[...]) * (1.0 + sc_ref[0]) + sh_ref[0]
    y_ref[...] = jnp.dot(u.astype(BF16), w_ref[...], preferred_element_type=F32)


def _inproj(x2d, shift, scale, w, n_per_batch):
    t, d = x2d.shape
    tm = min(TM, n_per_batch)
    per = n_per_batch // tm
    return pl.pallas_call(
        _inproj_kernel,
        out_shape=jax.ShapeDtypeStruct((t, PROJ_W), F32),
        grid=(t // tm,),
        in_specs=[pl.BlockSpec((tm, d), lambda i: (i, 0)),
                  pl.BlockSpec((1, 1, d), lambda i: (i // per, 0, 0)),
                  pl.BlockSpec((1, 1, d), lambda i: (i // per, 0, 0)),
                  pl.BlockSpec((d, PROJ_W), lambda i: (0, 0))],
        out_specs=pl.BlockSpec((tm, PROJ_W), lambda i: (i, 0)),
        compiler_params=_cparams(("arbitrary",)),
        name="inproj",
    )(x2d, shift, scale, w)


def _conv_kernel(y_ref, w_ref, b_ref, o_ref, pad_ref, *, n, grid2d):
    zeros = jnp.zeros((CONV_PAD, LANES), F32)
    pad_ref[0:CONV_PAD, :] = zeros
    pad_ref[CONV_PAD + n:CONV_PAD + n + CONV_PAD, :] = zeros
    pad_ref[CONV_PAD:CONV_PAD + n, :] = y_ref[0]
    k_scale = jnp.where(pl.program_id(1) >= N_HEADS, D_HEAD ** -0.5, 1.0).astype(F32)
    bias = b_ref[...]
    rb = min(CONV_RB, n)
    col = lax.broadcasted_iota(jnp.int32, (rb, LANES), 0) % GRID_W
    not_first = col >= 1
    not_last = col <= GRID_W - 2
    row_taps = (0, 1, 2) if grid2d else (1,)
    for blk in range(n // rb):
        r0 = CONV_PAD + blk * rb
        acc = jnp.zeros((rb, LANES), F32)
        for dj in range(3):
            part = jnp.zeros((rb, LANES), F32)
            for di in row_taps:
                off = (di - 1) * GRID_W + (dj - 1)
                tap = w_ref[di * 3 + dj:di * 3 + dj + 1, :]
                part = part + tap * pad_ref[r0 + off:r0 + off + rb, :]
            if grid2d and dj == 0:
                part = jnp.where(not_first, part, 0.0)
            if grid2d and dj == 2:
                part = jnp.where(not_last, part, 0.0)
            acc = acc + part
        val = _silu(acc + bias) * k_scale
        o_ref[0, blk * rb:(blk + 1) * rb, :] = val.astype(o_ref.dtype)


def _conv(y3d, conv_w9, conv_b, grid2d):
    b, n, _ = y3d.shape
    n_ct = 2 * D_MLSTM // LANES
    return pl.pallas_call(
        functools.partial(_conv_kernel, n=n, grid2d=grid2d),
        out_shape=jax.ShapeDtypeStruct((b, n, 2 * D_MLSTM), BF16),
        grid=(b, n_ct),
        in_specs=[pl.BlockSpec((1, n, LANES), lambda i, c: (i, 0, c)),
                  pl.BlockSpec((9, LANES), lambda i, c: (0, c)),
                  pl.BlockSpec((1, LANES), lambda i, c: (0, c))],
        out_specs=pl.BlockSpec((1, n, LANES), lambda i, c: (i, 0, c)),
        scratch_shapes=[pltpu.VMEM((n + 2 * CONV_PAD, LANES), F32)],
        compiler_params=_cparams(("arbitrary", "arbitrary")),
        name="conv2d" if grid2d else "conv1d",
    )(y3d, conv_w9, conv_b)


GATE_A, GATE_CM, GATE_B = 0, 1, 2
GATE_GROUP = 8


def _gates_kernel(g_ref, gb_ref, rv_ref, *, nc):
    u_i = lax.broadcasted_iota(jnp.int32, (CHUNK, CHUNK), 0)
    t_i = lax.broadcasted_iota(jnp.int32, (CHUNK, CHUNK), 1)
    prefix = (u_i <= t_i).astype(F32)
    suffix = (u_i >= t_i).astype(F32)
    grp = min(GATE_GROUP, nc)
    rows = grp * SUBLANES
    lane = lax.broadcasted_iota(jnp.int32, (rows, LANES), 1)
    is_fwd = lax.broadcasted_iota(jnp.int32, (rows, LANES), 0) % SUBLANES < N_HEADS
    h = N_HEADS

    def running_max(a):
        fwd, bwd = a, a
        sh = 1
        while sh < CHUNK:
            fwd = jnp.where(lane >= sh, jnp.maximum(fwd, pltpu.roll(fwd, sh, axis=1)), fwd)
            bwd = jnp.where(lane < CHUNK - sh, jnp.maximum(bwd, pltpu.roll(bwd, CHUNK - sh, axis=1)), bwd)
            sh *= 2
        return jnp.where(is_fwd, fwd, bwd)

    def body(i, carry):
        c0 = i * grp
        r0 = pl.multiple_of(c0 * CHUNK, CHUNK)
        pre = g_ref[0, pl.ds(r0, grp * CHUNK), :] + gb_ref[...]
        pre_t = jnp.concatenate(
            [pre[j * CHUNK:(j + 1) * CHUNK].T[:N_GATE_COLS, :] for j in range(grp)], axis=0)
        lf_t = -(jnp.maximum(-pre_t, 0.0) + jnp.log1p(jnp.exp(-jnp.abs(pre_t))))
        cum_f = jnp.dot(lf_t, prefix, precision=HIGHEST, preferred_element_type=F32)
        cum_b = jnp.dot(lf_t, suffix, precision=HIGHEST, preferred_element_type=F32)
        pick = lambda t, j, q: t[j * N_GATE_COLS + q * h:j * N_GATE_COLS + (q + 1) * h]
        b8 = jnp.concatenate([x for j in range(grp) for x in (pick(cum_f, j, 1), pick(cum_b, j, 3))], axis=0)
        li8 = jnp.concatenate([x for j in range(grp) for x in (pick(pre_t, j, 0), pick(pre_t, j, 2))], axis=0)
        a8 = li8 - b8
        as_chunks = lambda t: t.reshape(grp, SUBLANES, LANES)
        rv_ref[0, pl.ds(c0, grp), GATE_A] = as_chunks(a8)
        rv_ref[0, pl.ds(c0, grp), GATE_CM] = as_chunks(running_max(a8))
        rv_ref[0, pl.ds(c0, grp), GATE_B] = as_chunks(b8)
        return carry

    lax.fori_loop(0, nc // grp, body, 0)


def _gates(y3d, gate_b_row):
    b, n, _ = y3d.shape
    nc = n // CHUNK
    return pl.pallas_call(
        functools.partial(_gates_kernel, nc=nc),
        out_shape=jax.ShapeDtypeStruct((b, nc, 3, SUBLANES, LANES), F32),
        grid=(b,),
        in_specs=[pl.BlockSpec((1, n, LANES), lambda i: (i, 0, COL_G // LANES)),
                  pl.BlockSpec((1, LANES), lambda i: (0, 0))],
        out_specs=pl.BlockSpec((1, nc, 3, SUBLANES, LANES), lambda i: (i, 0, 0, 0, 0)),
        compiler_params=_cparams(("arbitrary",)),
        name="gates",
    )(y3d, gate_b_row)


def _mlstm_kernel(q_ref, k_ref, v_ref, o_ref, rv_ref, c0_ref, m0_ref, hg_ref,
                  out_ref, cfin_ref, mfin_ref, hf_ref, hb_ref, cst_ref, mst_ref, *, nc):
    head = pl.program_id(1)
    cst_ref[...] = c0_ref[0, 0]
    mst_ref[...] = m0_ref[0, 0]
    s_i = lax.broadcasted_iota(jnp.int32, (CHUNK, CHUNK), 0)
    t_i = lax.broadcasted_iota(jnp.int32, (CHUNK, CHUNK), 1)
    ones_row = (s_i == 0).astype(F32)

    def one_chunk(c, d, mask_t, last, h_ref):
        r0 = pl.multiple_of(c * CHUNK, CHUNK)
        g = d * N_HEADS + head
        gate_row = lambda tbl: rv_ref[0, pl.ds(c, 1), tbl, pl.ds(g, 1), :].reshape(1, LANES)
        a_row, cm_row, b_row = gate_row(GATE_A), gate_row(GATE_CM), gate_row(GATE_B)
        q_t = q_ref[0, pl.ds(r0, CHUNK), :].T
        k = k_ref[0, pl.ds(r0, CHUNK), :]
        v_t_aug = jnp.concatenate([v_ref[0, pl.ds(r0, CHUNK), :].T, ones_row], axis=0)
        m_prev = mst_ref[d, 0:1, :]
        m_row = jnp.maximum(m_prev, cm_row)
        a_bc = jnp.broadcast_to(a_row, (CHUNK, CHUNK)).T
        d_t = jnp.where(mask_t, jnp.exp(a_bc - m_row), 0.0)
        s_t = jnp.dot(k, q_t, preferred_element_type=F32) * d_t
        cs = cst_ref[d]
        rhs = jnp.concatenate(
            [s_t.astype(BF16), (q_t.astype(F32) * jnp.exp(m_prev - m_row)).astype(BF16)], axis=0)
        lhs = jnp.concatenate([v_t_aug.astype(BF16), cs.astype(BF16)], axis=1)
        h_t = jnp.dot(lhs, rhs, preferred_element_type=F32)
        den = h_t[D_HEAD:D_HEAD + 1, :]
        floor = jnp.exp(-b_row - m_row)
        h_ref[pl.ds(c, 1)] = (h_t[:D_HEAD] / jnp.maximum(jnp.abs(den), floor))[None]
        m_last = m_row[:, last:last + 1]
        wv_t = (v_t_aug * jnp.exp(a_row - m_last)).astype(BF16)
        cst_ref[d] = jnp.exp(m_prev - m_last) * cs + jnp.dot(wv_t, k, preferred_element_type=F32)
        mst_ref[d] = jnp.broadcast_to(b_row[:, last:last + 1] + m_last, (SUBLANES, LANES))

    def body(c, carry):
        one_chunk(c, 0, s_i <= t_i, CHUNK - 1, hf_ref)
        one_chunk(nc - 1 - c, 1, s_i >= t_i, 0, hb_ref)
        return carry

    lax.fori_loop(0, nc, body, 0, unroll=min(8, nc))
    cfin_ref[0, 0] = cst_ref[...]
    mfin_ref[0, 0] = mst_ref[...]

    def readout(c, carry):
        r0 = pl.multiple_of(c * CHUNK, CHUNK)
        h_t = hf_ref[pl.ds(c, 1)][0] + hb_ref[pl.ds(c, 1)][0]
        hc = h_t - jnp.mean(h_t, axis=0, keepdims=True)
        hn = (hc * lax.rsqrt(jnp.mean(hc * hc, axis=0, keepdims=True) + LN_EPS)).T
        gate = jax.nn.sigmoid(o_ref[0, pl.ds(r0, CHUNK), :])
        out_ref[0, pl.ds(r0, CHUNK), :] = (hn * hg_ref[...] * gate).astype(out_ref.dtype)
        return carry

    lax.fori_loop(0, nc, readout, 0, unroll=min(4, nc))


def _mlstm(qk, y3d, rowv, c0, m0, head_g):
    b, n, _ = qk.shape
    nc = n // CHUNK
    v_blk = COL_V // LANES
    o_blk = COL_O // LANES
    tok = lambda off: pl.BlockSpec((1, n, LANES), lambda i, h: (i, 0, off + h))
    state_c = pl.BlockSpec((1, 1, 2, 2 * D_HEAD, D_HEAD), lambda i, h: (i, h, 0, 0, 0))
    state_m = pl.BlockSpec((1, 1, 2, SUBLANES, LANES), lambda i, h: (i, h, 0, 0, 0))
    return pl.pallas_call(
        functools.partial(_mlstm_kernel, nc=nc),
        out_shape=(jax.ShapeDtypeStruct((b, n, D_MLSTM), BF16),
                   jax.ShapeDtypeStruct((b, N_HEADS, 2, 2 * D_HEAD, D_HEAD), F32),
                   jax.ShapeDtypeStruct((b, N_HEADS, 2, SUBLANES, LANES), F32)),
        grid=(b, N_HEADS),
        in_specs=[tok(0), tok(N_HEADS), tok(v_blk), tok(o_blk),
                  pl.BlockSpec((1, nc, 3, SUBLANES, LANES), lambda i, h: (i, 0, 0, 0, 0)),
                  state_c, state_m,
                  pl.BlockSpec((1, LANES), lambda i, h: (0, h))],
        out_specs=(pl.BlockSpec((1, n, LANES), lambda i, h: (i, 0, h)), state_c, state_m),
        scratch_shapes=[pltpu.VMEM((nc, D_HEAD, CHUNK), F32), pltpu.VMEM((nc, D_HEAD, CHUNK), F32),
                        pltpu.VMEM((2, 2 * D_HEAD, D_HEAD), F32),
                        pltpu.VMEM((2, SUBLANES, LANES), F32)],
        compiler_params=_cparams(("arbitrary", "arbitrary")),
        name="mlstm",
    )(qk, qk, y3d, y3d, rowv, c0, m0, head_g)


def _dft_tables():
    r = DFT_R
    idx = np.arange(r)
    cg = np.cos(2 * np.pi * np.outer(idx, idx) / FOURIER_GROUP)
    sg = np.sin(2 * np.pi * np.outer(idx, idx) / FOURIER_GROUP)
    n_grp = D_FOURIER // FOURIER_GROUP
    bdc = np.kron(np.eye(n_grp), cg) / 8.0
    bds = np.kron(np.eye(n_grp), sg) / 8.0
    k1 = idx[:, None]
    a = idx[None, :]
    m1 = np.zeros((r, 2 * r, 2 * r))
    for b2 in range(r):
        th = 2 * np.pi * k1 * (r * a + b2) / (r * r)
        ec, es = np.cos(th), np.sin(th)
        m1[b2] = np.block([[ec, -es], [es, ec]]) / 8.0
    th2 = 2 * np.pi * np.outer(idx, idx) / r
    m2 = np.concatenate([np.cos(th2), -np.sin(th2)], axis=1) / 8.0
    as_bf16 = lambda t: jnp.asarray(t, dtype=F32).astype(BF16)
    return as_bf16(bdc), as_bf16(bds), as_bf16(m1), as_bf16(m2)


FOUR_BB = 8


def _four1_kernel(f_ref, bdc_ref, bds_ref, m1_ref, z_ref):
    r = DFT_R
    fb = f_ref[0].reshape(FOUR_BB * r, D_FOURIER).astype(BF16)
    p = jnp.dot(fb, bdc_ref[...], preferred_element_type=F32).astype(BF16)
    q = jnp.dot(fb, bds_ref[...], preferred_element_type=F32).astype(BF16)
    for j in range(FOUR_BB):
        x2 = jnp.concatenate([p[j * r:(j + 1) * r], q[j * r:(j + 1) * r]], axis=0)
        z_ref[0, j] = jnp.dot(m1_ref[j], x2, preferred_element_type=F32).astype(z_ref.dtype)


def _four2_kernel(z_ref, m2_ref, o_ref):
    for j in range(FOUR_BB):
        o_ref[0, j] = jnp.dot(m2_ref[...], z_ref[0, j], preferred_element_type=F32)


def _fourier(f_t, tables):
    bdc, bds, m1, m2 = tables
    b = f_t.shape[0]
    r = DFT_R
    nb = r // FOUR_BB
    z = pl.pallas_call(
        _four1_kernel,
        out_shape=jax.ShapeDtypeStruct((b, r, 2 * r, D_FOURIER), BF16),
        grid=(b, nb),
        in_specs=[pl.BlockSpec((1, FOUR_BB, r, D_FOURIER), lambda i, j: (i, j, 0, 0)),
                  pl.BlockSpec((D_FOURIER, D_FOURIER), lambda i, j: (0, 0)),
                  pl.BlockSpec((D_FOURIER, D_FOURIER), lambda i, j: (0, 0)),
                  pl.BlockSpec((FOUR_BB, 2 * r, 2 * r), lambda i, j: (j, 0, 0))],
        out_specs=pl.BlockSpec((1, FOUR_BB, 2 * r, D_FOURIER), lambda i, j: (i, j, 0, 0)),
        compiler_params=_cparams(("arbitrary", "arbitrary")),
        name="four1",
    )(f_t, bdc, bds, m1)
    z_t = z.reshape(b, r, 2, r, D_FOURIER).transpose(0, 3, 2, 1, 4).reshape(b, r, 2 * r, D_FOURIER)
    return pl.pallas_call(
        _four2_kernel,
        out_shape=jax.ShapeDtypeStruct((b, r, r, D_FOURIER), F32),
        grid=(b, nb),
        in_specs=[pl.BlockSpec((1, FOUR_BB, 2 * r, D_FOURIER), lambda i, j: (i, j, 0, 0)),
                  pl.BlockSpec((r, 2 * r), lambda i, j: (0, 0))],
        out_specs=pl.BlockSpec((1, FOUR_BB, r, D_FOURIER), lambda i, j: (i, j, 0, 0)),
        compiler_params=_cparams(("arbitrary", "arbitrary")),
        name="four2",
    )(z_t, m2)


def _outproj_kernel(x_ref, mr_ref, fo_ref, wm_ref, wf_ref, g1_ref, lg_ref, lb_ref,
                    sh_ref, sc_ref, wr_ref, x1_ref, xm_ref, s_ref):
    mix = (jnp.dot(mr_ref[...], wm_ref[...], preferred_element_type=F32)
           + jnp.dot(fo_ref[...].astype(BF16), wf_ref[...], preferred_element_type=F32))
    x1 = _layer_norm(DEEPNORM_ALPHA * x_ref[...] + g1_ref[0] * mix) * lg_ref[...] + lb_ref[...]
    x1_ref[...] = x1
    xm = _layer_norm(x1) * (1.0 + sc_ref[0]) + sh_ref[0]
    xm_ref[...] = _pack_rows(xm)
    logits = jnp.dot(xm.astype(BF16), wr_ref[...], preferred_element_type=F32)
    s_ref[...] = jax.nn.sigmoid(logits)


def _outproj(x2d, mread, four, w_m, w_f, gate1, ln_g, ln_b, shift2, scale2, w_router, n_per_batch):
    t, d = x2d.shape
    per = n_per_batch // TM
    tokd = lambda w: pl.BlockSpec((TM, w), lambda i: (i, 0))
    full = lambda r, c: pl.BlockSpec((r, c), lambda i: (0, 0))
    mod = pl.BlockSpec((1, 1, d), lambda i: (i // per, 0, 0))
    return pl.pallas_call(
        _outproj_kernel,
        out_shape=(jax.ShapeDtypeStruct((t, d), F32),
                   jax.ShapeDtypeStruct((t, PACK_W), jnp.int32),
                   jax.ShapeDtypeStruct((t, LANES), F32)),
        grid=(t // TM,),
        in_specs=[tokd(d), tokd(D_MLSTM), tokd(D_FOURIER), full(D_MLSTM, d), full(D_FOURIER, d),
                  mod, full(1, d), full(1, d), mod, mod, full(d, LANES)],
        out_specs=(tokd(d), tokd(PACK_W), tokd(LANES)),
        compiler_params=_cparams(("arbitrary",)),
        name="outproj",
    )(x2d, mread, four, w_m, w_f, gate1, ln_g, ln_b, shift2, scale2, w_router)


def _route_kernel(s_ref, b_ref, idx_ref, gw_ref, rank_ref, cnt_ref, carry_ref):
    @pl.when(pl.program_id(0) == 0)
    def _():
        carry_ref[...] = jnp.zeros_like(carry_ref)

    tm = s_ref.shape[0]
    s = s_ref[...]
    lane_i = lax.broadcasted_iota(jnp.int32, (tm, LANES), 1)
    lane = lane_i.astype(F32)
    neg_inf = jnp.float32(-jnp.inf)
    sb = jnp.where(lane_i < N_EXPERTS, s + b_ref[...], neg_inf)
    picks, sels = [], []
    taken = jnp.zeros((tm, LANES), F32)
    for _ in range(TOP_K):
        best = jnp.max(sb, axis=-1, keepdims=True)
        pick = jnp.min(jnp.where(sb == best, lane, float(LANES)), axis=-1, keepdims=True)
        onehot = lane == pick
        sels.append(jnp.sum(jnp.where(onehot, s, 0.0), axis=-1, keepdims=True))
        picks.append(pick)
        sb = jnp.where(onehot, neg_inf, sb)
        taken = taken + onehot.astype(F32)
    r_i = lax.broadcasted_iota(jnp.int32, (tm, tm), 0)
    c_i = lax.broadcasted_iota(jnp.int32, (tm, tm), 1)
    before = (c_i < r_i).astype(BF16)
    rank_all = carry_ref[0:1, :] + jnp.dot(before, taken.astype(BF16), preferred_element_type=F32)
    carry_ref[...] = carry_ref[...] + jnp.sum(taken, axis=0, keepdims=True)
    cnt_ref[...] = carry_ref[...]
    total = sels[0]
    for v in sels[1:]:
        total = total + v
    k_i = lax.broadcasted_iota(jnp.int32, (tm, TOP_K), 1)
    gw_out = jnp.zeros((tm, TOP_K), F32)
    idx_cols = jnp.zeros((tm, LANES), F32)
    rank_cols = jnp.zeros((tm, LANES), F32)
    for j in range(TOP_K):
        rank_j = jnp.sum(jnp.where(lane == picks[j], rank_all, 0.0), axis=-1, keepdims=True)
        gw_out = jnp.where(k_i == j, ROUTED_SCALE * sels[j] / total, gw_out)
        idx_cols = jnp.where(lane_i == j, picks[j], idx_cols)
        rank_cols = jnp.where(lane_i == j, rank_j, rank_cols)
    gw_ref[...] = gw_out
    idx_ref[...] = idx_cols.T[:TOP_K, :].astype(jnp.int32)
    rank_ref[...] = rank_cols.T[:TOP_K, :].astype(jnp.int32)


def _route(scores, bias_row):
    t = scores.shape[0]
    tok = lambda w: pl.BlockSpec((TM, w), lambda i: (i, 0))
    slot_major = pl.BlockSpec((TOP_K, TM), lambda i: (0, i))
    return pl.pallas_call(
        _route_kernel,
        out_shape=(jax.ShapeDtypeStruct((TOP_K, t), jnp.int32),
                   jax.ShapeDtypeStruct((t, TOP_K), F32),
                   jax.ShapeDtypeStruct((TOP_K, t), jnp.int32),
                   jax.ShapeDtypeStruct((SUBLANES, LANES), F32)),
        grid=(t // TM,),
        in_specs=[tok(LANES), pl.BlockSpec((1, LANES), lambda i: (0, 0))],
        out_specs=(slot_major, tok(TOP_K), slot_major,
                   pl.BlockSpec((SUBLANES, LANES), lambda i: (0, 0))),
        scratch_shapes=[pltpu.VMEM((SUBLANES, LANES), F32)],
        compiler_params=_cparams(("arbitrary",)),
        name="route",
    )(scores, bias_row)


TN_DEST = 2048


def _dest_kernel(ps_ref, idx_ref, rank_ref, dest_ref):
    idx = idx_ref[...]
    start = jnp.zeros(idx.shape, jnp.int32)
    for e in range(N_EXPERTS):
        start = jnp.where(idx == e, ps_ref[e], start)
    dest_ref[...] = start + rank_ref[...]


def _dest(pad_start, idx_t, rank_t):
    t = idx_t.shape[1]
    blk = lambda: pl.BlockSpec((TOP_K, TN_DEST), lambda i, ps: (0, i))
    grid_spec = pltpu.PrefetchScalarGridSpec(
        num_scalar_prefetch=1, grid=(t // TN_DEST,), in_specs=[blk(), blk()], out_specs=blk())
    return pl.pallas_call(
        _dest_kernel,
        out_shape=jax.ShapeDtypeStruct((TOP_K, t), jnp.int32),
        grid_spec=grid_spec,
        compiler_params=_cparams(("arbitrary",)),
        name="dest",
    )(pad_start, idx_t, rank_t)


def _sc_worker_id(info):
    return lax.axis_index("s") * info.num_cores + lax.axis_index("c")


def _sc_dispatch(xm, dest3, n_rows):
    t, w = xm.shape
    info = plsc.get_sparse_core_info()
    n_workers = info.num_cores * info.num_subcores
    per_worker = t // SC_ROWS // n_workers
    mesh = plsc.VectorSubcoreMesh(core_axis_name="c", subcore_axis_name="s")

    @functools.partial(
        pl.kernel, mesh=mesh,
        out_type=jax.ShapeDtypeStruct((n_rows, w), xm.dtype),
        scratch_types=[pltpu.VMEM((TOP_K, SC_ROWS), jnp.int32), pltpu.VMEM((SC_ROWS, w), xm.dtype)],
        name="sc_dispatch",
    )
    def run(xm_hbm, dest_hbm, xs_hbm, idx_v, rows_v):
        first = _sc_worker_id(info) * per_worker

        @pl.loop(0, per_worker)
        def _(ci):
            chunk = first + ci
            pltpu.sync_copy(dest_hbm.at[chunk], idx_v)
            pltpu.sync_copy(xm_hbm.at[pl.ds(chunk * SC_ROWS, SC_ROWS)], rows_v)
            for j in range(TOP_K):
                pltpu.sync_copy(rows_v, xs_hbm.at[idx_v.at[j]])

    return run(xm, dest3)


def _sc_gather(table, idx2):
    n_chunks, rows = idx2.shape
    w = table.shape[1]
    info = plsc.get_sparse_core_info()
    n_workers = info.num_cores * info.num_subcores
    per_worker = n_chunks // n_workers
    assert per_worker % 2 == 0
    mesh = plsc.VectorSubcoreMesh(core_axis_name="c", subcore_axis_name="s")

    @functools.partial(
        pl.kernel, mesh=mesh,
        out_type=jax.ShapeDtypeStruct((n_chunks * rows, w), table.dtype),
        scratch_types=[pltpu.VMEM((2, rows), jnp.int32), pltpu.VMEM((2, rows, w), table.dtype),
                       pltpu.SemaphoreType.DMA, pltpu.SemaphoreType.DMA],
        name="sc_gather",
    )
    def run(table_hbm, idx_hbm, out_hbm, idx_v, rows_v, sem0, sem1):
        first = _sc_worker_id(info) * per_worker
        sems = (sem0, sem1)

        def gather(b):
            return pltpu.make_async_copy(table_hbm.at[idx_v.at[b]], rows_v.at[b], sems[b])

        def start(chunk, b):
            pltpu.sync_copy(idx_hbm.at[chunk], idx_v.at[b])
            gather(b).start()

        def finish(chunk, b):
            gather(b).wait()
            pltpu.sync_copy(rows_v.at[b], out_hbm.at[pl.ds(chunk * rows, rows)])

        start(first, 0)

        @pl.loop(0, per_worker, step=2)
        def _(ci):
            chunk = first + ci
            start(chunk + 1, 1)
            finish(chunk, 0)

            @pl.when(ci + 2 < per_worker)
            def _():
                start(chunk + 2, 0)

            finish(chunk + 1, 1)

    return run(table, idx2)


def _expert_kernel(be_ref, bv_ref, xs_ref, wgu_ref, wd_ref, y_ref, wgu_bf, wd_bf):
    i = pl.program_id(0)
    n_valid = bv_ref[i]

    @pl.when((i == 0) | (be_ref[i] != be_ref[jnp.maximum(i - 1, 0)]))
    def _():
        wgu_bf[...] = wgu_ref[0].astype(BF16)
        wd_bf[...] = wd_ref[0].astype(BF16)

    @pl.when(n_valid > 0)
    def _():
        for h in range(ROW_BLK // ROW_HALF):
            r0 = h * ROW_HALF
            row = lax.broadcasted_iota(jnp.int32, (ROW_HALF, PACK_W), 0) + r0
            x = _unpack_bf16(jnp.where(row < n_valid, xs_ref[r0:r0 + ROW_HALF, :], 0))
            gu = jnp.dot(x, wgu_bf[...], preferred_element_type=F32)
            act = _silu(gu[:, :D_EXPERT]) * gu[:, D_EXPERT:]
            y_ref[r0:r0 + ROW_HALF, :] = _pack_rows(
                jnp.dot(act.astype(BF16), wd_bf[...], preferred_element_type=F32))


def _experts(block_e, block_valid, xs, w_gu, w_down):
    n_rows, w = xs.shape
    d = D_MODEL
    grid_spec = pltpu.PrefetchScalarGridSpec(
        num_scalar_prefetch=2,
        grid=(n_rows // ROW_BLK,),
        in_specs=[pl.BlockSpec((ROW_BLK, w), lambda i, be, bv: (i, 0)),
                  pl.BlockSpec((1, d, 2 * D_EXPERT), lambda i, be, bv: (be[i], 0, 0)),
                  pl.BlockSpec((1, D_EXPERT, d), lambda i, be, bv: (be[i], 0, 0))],
        out_specs=pl.BlockSpec((ROW_BLK, w), lambda i, be, bv: (i, 0)),
        scratch_shapes=[pltpu.VMEM((d, 2 * D_EXPERT), BF16), pltpu.VMEM((D_EXPERT, d), BF16)],
    )
    return pl.pallas_call(
        _expert_kernel,
        out_shape=jax.ShapeDtypeStruct((n_rows, w), jnp.int32),
        grid_spec=grid_spec,
        compiler_params=_cparams(("arbitrary",)),
        name="experts",
    )(block_e, block_valid, xs, w_gu, w_down)


TM_COMBINE = 256


def _combine_kernel(*refs):
    yg_refs = refs[:TOP_K]
    gw_ref, x1_ref, xm_ref, g2_ref, wsgu_ref, wsd_ref, lg_ref, lb_ref, out_ref = refs[TOP_K:]
    gu = jnp.dot(_unpack_bf16(xm_ref[...]), wsgu_ref[...], preferred_element_type=F32)
    act = _silu(gu[:, :D_EXPERT]) * gu[:, D_EXPERT:]
    acc = jnp.dot(act.astype(BF16), wsd_ref[...], preferred_element_type=F32)
    gw = gw_ref[...]
    for j in range(TOP_K):
        acc = acc + gw[:, j:j + 1] * _unpack_rows(yg_refs[j][...])
    z = DEEPNORM_ALPHA * x1_ref[...] + g2_ref[0] * acc
    out_ref[...] = _layer_norm(z) * lg_ref[...] + lb_ref[...]


def _combine(yg, gw, x1, xm, gate2, w_sgu, w_sd, ln_g, ln_b, n_per_batch):
    t, d = x1.shape
    tm = TM_COMBINE
    per = n_per_batch // tm
    steps = t // tm
    tok = lambda w: pl.BlockSpec((tm, w), lambda i: (i, 0))
    full = lambda r, c: pl.BlockSpec((r, c), lambda i: (0, 0))
    slot = lambda j: pl.BlockSpec((tm, PACK_W), lambda i: (j * steps + i, 0))
    return pl.pallas_call(
        _combine_kernel,
        out_shape=jax.ShapeDtypeStruct((t, d), F32),
        grid=(steps,),
        in_specs=[slot(j) for j in range(TOP_K)]
        + [tok(TOP_K), tok(d), tok(PACK_W), pl.BlockSpec((1, 1, d), lambda i: (i // per, 0, 0)),
           full(d, 2 * D_EXPERT), full(D_EXPERT, d), full(1, d), full(1, d)],
        out_specs=tok(d),
        compiler_params=_cparams(("arbitrary",)),
        name="combine",
    )(*([yg] * TOP_K), gw, x1, xm, gate2, w_sgu, w_sd, ln_g, ln_b)


def _mixer_heads(stream, shift, scale, w_in_k, conv_w9, conv_b, gate_b_row, head_g, c0, m0, grid2d):
    b, n, d = stream.shape
    y = _inproj(stream.reshape(b * n, d), shift, scale, w_in_k, n).reshape(b, n, PROJ_W)
    qk = _conv(y, conv_w9, conv_b, grid2d)
    mread, c_fin, m_fin = _mlstm(qk, y, _gates(y, gate_b_row), c0, m0, head_g)
    return y, mread, c_fin, m_fin


MOE_GROUPS = 2


def _moe_group(xm, x1, scores, bias_row, gate2, w_gu, w_down, w_sgu, w_sd, ln_g, ln_b, n_per_batch):
    t = xm.shape[0]
    n_assign = t * TOP_K
    n_rows = n_assign + N_EXPERTS * ROW_BLK
    idx_t, gw, rank_t, counts = _route(scores, bias_row)
    sizes = counts[0, :N_EXPERTS].astype(jnp.int32)
    padded = (sizes + ROW_BLK - 1) // ROW_BLK * ROW_BLK
    pad_end = jnp.cumsum(padded)
    pad_start = (pad_end - padded).astype(jnp.int32)
    blk_first = jnp.arange(n_rows // ROW_BLK, dtype=jnp.int32) * ROW_BLK
    block_e = jnp.minimum(jnp.sum(blk_first[:, None] >= pad_end[None, :], axis=1),
                          N_EXPERTS - 1).astype(jnp.int32)
    is_e = block_e[:, None] == jnp.arange(N_EXPERTS, dtype=jnp.int32)[None, :]
    valid_end = jnp.sum(jnp.where(is_e, (pad_start + sizes)[None, :], 0), axis=1)
    block_valid = jnp.clip(valid_end - blk_first, 0, ROW_BLK)
    block_valid = jnp.where(blk_first < pad_end[-1], block_valid, 0).astype(jnp.int32)
    dest_t = _dest(pad_start, idx_t, rank_t)
    dest3 = dest_t.reshape(TOP_K, t // SC_ROWS, SC_ROWS).transpose(1, 0, 2)

    xs = _sc_dispatch(xm, dest3, n_rows)
    yb = _experts(block_e, block_valid, xs, w_gu, w_down)
    yg = _sc_gather(yb, dest_t.reshape(n_assign // SC_GATHER_ROWS, SC_GATHER_ROWS))
    return _combine(yg, gw, x1, xm, gate2, w_sgu, w_sd, ln_g, ln_b, n_per_batch)


def kernel(x, c, ctx, c_ctx, w_ada, b_ada, w_in, conv_w, conv_b, gate_b, head_g, w_out, ln1_g, ln1_b,
           w_router, router_bias, w_expert_gu, w_expert_down, w_shared_gu, w_shared_down, ln2_g, ln2_b):
    bsz, n, d = x.shape
    n_ctx = ctx.shape[1]
    depth = w_ada.shape[0]
    t = bsz * n
    tables = _dft_tables()
    for l in range(depth):
        cvec = jnp.zeros((SUBLANES, d), F32).at[:bsz].set(c).at[bsz].set(c_ctx)
        mod = _ada(cvec, w_ada[l], b_ada[l][None, :])
        mods = mod.reshape(SUBLANES, 6, 1, d)
        mod_x = [mods[:bsz, j] for j in range(6)]
        mod_c = [jnp.broadcast_to(mods[bsz:bsz + 1, j], (bsz, 1, d)) for j in range(6)]

        wl = w_in[l]
        g0 = 4 * D_MLSTM
        w_in_k = jnp.concatenate(
            [wl[:, :g0], wl[:, g0 + N_GATE_COLS:], wl[:, g0:g0 + N_GATE_COLS],
             jnp.zeros((d, LANES - N_GATE_COLS), F32)], axis=1).astype(BF16)
        conv_w9 = conv_w[l].reshape(9, 2 * D_MLSTM)
        conv_b_row = conv_b[l][None, :]
        gate_b_row = jnp.zeros((1, LANES), F32).at[0, :N_GATE_COLS].set(gate_b[l].reshape(-1))
        head_g_row = head_g[l][None, :]

        c0 = jnp.zeros((bsz, N_HEADS, 2, 2 * D_HEAD, D_HEAD), F32)
        m0 = jnp.zeros((bsz, N_HEADS, 2, SUBLANES, LANES), F32)
        _, _, c_ctx_fin, m_ctx_fin = _mixer_heads(ctx, mod_c[0], mod_c[1], w_in_k, conv_w9, conv_b_row,
                                                  gate_b_row, head_g_row, c0, m0, grid2d=False)
        y, mread, _, _ = _mixer_heads(x, mod_x[0], mod_x[1], w_in_k, conv_w9, conv_b_row,
                                      gate_b_row, head_g_row, c_ctx_fin, m_ctx_fin, grid2d=True)

        r = DFT_R
        f_t = y[:, :, COL_F:COL_F + D_FOURIER].reshape(bsz, r, r, D_FOURIER).transpose(0, 2, 1, 3)
        four = _fourier(f_t, tables).transpose(0, 2, 1, 3).reshape(t, D_FOURIER)

        w_o = w_out[l].astype(BF16)
        w_r = jnp.zeros((d, LANES), F32).at[:, :N_EXPERTS].set(w_router[l]).astype(BF16)
        x2d = x.reshape(t, d)
        x1, xm, scores = _outproj(x2d, mread.reshape(t, D_MLSTM), four, w_o[:D_MLSTM], w_o[D_MLSTM:],
                                  mod_x[2], ln1_g[l][None, :], ln1_b[l][None, :], mod_x[3], mod_x[4],
                                  w_r, n)

        bias_row = jnp.zeros((1, LANES), F32).at[0, :N_EXPERTS].set(router_bias[l])
        w_sgu = w_shared_gu[l].astype(BF16)
        w_sd = w_shared_down[l].astype(BF16)
        tg = t // MOE_GROUPS
        bg = bsz // MOE_GROUPS
        outs = []
        for grp in range(MOE_GROUPS):
            tok = slice(grp * tg, (grp + 1) * tg)
            outs.append(_moe_group(xm[tok], x1[tok], scores[tok], bias_row, mod_x[5][grp * bg:(grp + 1) * bg],
                                   w_expert_gu[l], w_expert_down[l], w_sgu, w_sd,
                                   ln2_g[l][None, :], ln2_b[l][None, :], n))
        x = jnp.concatenate(outs, axis=0).reshape(bsz, n, d)
        assert depth == 1
    return x
```

```python
import functools
import math

import numpy as np
import jax
import jax.numpy as jnp
from jax import lax
from jax.experimental import pallas as pl
from jax.experimental.pallas import tpu as pltpu
from jax.experimental.pallas import tpu_sc as plsc

F32 = jnp.float32
BF16 = jnp.bfloat16
HIGHEST = lax.Precision.HIGHEST

D_MODEL = 1024
GRID_W = 64
N_HEADS = 4
D_HEAD = 128
D_MLSTM = N_HEADS * D_HEAD
D_FOURIER = 512
FOURIER_GROUP = 64
N_GATE_COLS = 16
CHUNK = 128
N_EXPERTS = 64
TOP_K = 8
D_EXPERT = 256
ROUTED_SCALE = 2.5
DEEPNORM_ALPHA = 2.0 ** 0.25
LN_EPS = 1e-5

LANES = 128
SUBLANES = 8
VMEM_LIMIT = 48 * 1024 * 1024

COL_QK = 0
COL_V = 2 * D_MLSTM
COL_O = 3 * D_MLSTM
COL_F = 4 * D_MLSTM
COL_G = COL_F + D_FOURIER
PROJ_W = COL_G + LANES

TM = 512
ROW_BLK = 512
ROW_HALF = 256
PACK_W = D_MODEL // 2
SC_ROWS = 128
SC_GATHER_ROWS = 64
CONV_PAD = 72
CONV_RB = 256
DFT_R = 64


def _cparams(sem, vmem=VMEM_LIMIT):
    return pltpu.CompilerParams(dimension_semantics=sem, vmem_limit_bytes=vmem)


def _layer_norm(x):
    mu = jnp.mean(x, axis=-1, keepdims=True)
    xc = x - mu
    var = jnp.mean(xc * xc, axis=-1, keepdims=True)
    return xc * lax.rsqrt(var + LN_EPS)


def _silu(x):
    return x * jax.nn.sigmoid(x)


def _pack_pair(lo, hi):
    as_bits = lambda u: lax.bitcast_convert_type(u.astype(BF16).astype(F32), jnp.uint32)
    word = as_bits(hi) | lax.shift_right_logical(as_bits(lo), jnp.uint32(16))
    return lax.bitcast_convert_type(word, jnp.int32)


def _unpack_pair_bf16(w):
    lo = lax.bitcast_convert_type(w.astype(jnp.int16), BF16)
    hi = lax.bitcast_convert_type(lax.shift_right_logical(w, jnp.int32(16)).astype(jnp.int16), BF16)
    return lo, hi


def _pack_rows(v):
    return _pack_pair(v[:, :PACK_W], v[:, PACK_W:])


def _unpack_bf16(w):
    return jnp.concatenate(_unpack_pair_bf16(w), axis=-1)


def _unpack_rows(w):
    u = lax.bitcast_convert_type(w, jnp.uint32)
    lo = lax.bitcast_convert_type(lax.shift_left(u, jnp.uint32(16)), F32)
    hi = lax.bitcast_convert_type(u & jnp.uint32(0xFFFF0000), F32)
    return jnp.concatenate([lo, hi], axis=1)


def _ada_kernel(c_ref, w_ref, b_ref, o_ref):
    s = _silu(c_ref[...])
    o_ref[...] = jnp.dot(s, w_ref[...], precision=HIGHEST, preferred_element_type=F32) + b_ref[...]


def _ada(cvec, w_ada, b_ada):
    rows, d = cvec.shape
    n_out = w_ada.shape[1]
    tn = 1024
    return pl.pallas_call(
        _ada_kernel,
        out_shape=jax.ShapeDtypeStruct((rows, n_out), F32),
        grid=(n_out // tn,),
        in_specs=[pl.BlockSpec((rows, d), lambda j: (0, 0)),
                  pl.BlockSpec((d, tn), lambda j: (0, j)),
                  pl.BlockSpec((1, tn), lambda j: (0, j))],
        out_specs=pl.BlockSpec((rows, tn), lambda j: (0, j)),
        compiler_params=_cparams(("arbitrary",)),
        name="ada",
    )(cvec, w_ada, b_ada)


def _inproj_kernel(x_ref, sh_ref, sc_ref, w_ref, y_ref):
    u = _layer_norm(x_ref[...]) * (1.0 + sc_ref[0]) + sh_ref[0]
    y_ref[...] = jnp.dot(u.astype(BF16), w_ref[...], preferred_element_type=F32)


def _inproj(x2d, shift, scale, w, n_per_batch):
    t, d = x2d.shape
    tm = min(TM, n_per_batch)
    per = n_per_batch // tm
    return pl.pallas_call(
        _inproj_kernel,
        out_shape=jax.ShapeDtypeStruct((t, PROJ_W), F32),
        grid=(t // tm,),
        in_specs=[pl.BlockSpec((tm, d), lambda i: (i, 0)),
                  pl.BlockSpec((1, 1, d), lambda i: (i // per, 0, 0)),
                  pl.BlockSpec((1, 1, d), lambda i: (i // per, 0, 0)),
                  pl.BlockSpec((d, PROJ_W), lambda i: (0, 0))],
        out_specs=pl.BlockSpec((tm, PROJ_W), lambda i: (i, 0)),
        compiler_params=_cparams(("arbitrary",)),
        name="inproj",
    )(x2d, shift, scale, w)


def _conv_kernel(y_ref, w_ref, b_ref, o_ref, pad_ref, *, n, grid2d):
    zeros = jnp.zeros((CONV_PAD, LANES), F32)
    pad_ref[0:CONV_PAD, :] = zeros
    pad_ref[CONV_PAD + n:CONV_PAD + n + CONV_PAD, :] = zeros
    pad_ref[CONV_PAD:CONV_PAD + n, :] = y_ref[0]
    k_scale = jnp.where(pl.program_id(1) >= N_HEADS, D_HEAD ** -0.5, 1.0).astype(F32)
    bias = b_ref[...]
    rb = min(CONV_RB, n)
    col = lax.broadcasted_iota(jnp.int32, (rb, LANES), 0) % GRID_W
    not_first = col >= 1
    not_last = col <= GRID_W - 2
    row_taps = (0, 1, 2) if grid2d else (1,)
    for blk in range(n // rb):
        r0 = CONV_PAD + blk * rb
        acc = jnp.zeros((rb, LANES), F32)
        for dj in range(3):
            part = jnp.zeros((rb, LANES), F32)
            for di in row_taps:
                off = (di - 1) * GRID_W + (dj - 1)
                tap = w_ref[di * 3 + dj:di * 3 + dj + 1, :]
                part = part + tap * pad_ref[r0 + off:r0 + off + rb, :]
            if grid2d and dj == 0:
                part = jnp.where(not_first, part, 0.0)
            if grid2d and dj == 2:
                part = jnp.where(not_last, part, 0.0)
            acc = acc + part
        val = _silu(acc + bias) * k_scale
        o_ref[0, blk * rb:(blk + 1) * rb, :] = val.astype(o_ref.dtype)


def _conv(y3d, conv_w9, conv_b, grid2d):
    b, n, _ = y3d.shape
    n_ct = 2 * D_MLSTM // LANES
    return pl.pallas_call(
        functools.partial(_conv_kernel, n=n, grid2d=grid2d),
        out_shape=jax.ShapeDtypeStruct((b, n, 2 * D_MLSTM), BF16),
        grid=(b, n_ct),
        in_specs=[pl.BlockSpec((1, n, LANES), lambda i, c: (i, 0, c)),
                  pl.BlockSpec((9, LANES), lambda i, c: (0, c)),
                  pl.BlockSpec((1, LANES), lambda i, c: (0, c))],
        out_specs=pl.BlockSpec((1, n, LANES), lambda i, c: (i, 0, c)),
        scratch_shapes=[pltpu.VMEM((n + 2 * CONV_PAD, LANES), F32)],
        compiler_params=_cparams(("arbitrary", "arbitrary")),
        name="conv2d" if grid2d else "conv1d",
    )(y3d, conv_w9, conv_b)


GATE_A, GATE_CM, GATE_B = 0, 1, 2
GATE_GROUP = 8


def _gates_kernel(g_ref, gb_ref, rv_ref, *, nc):
    u_i = lax.broadcasted_iota(jnp.int32, (CHUNK, CHUNK), 0)
    t_i = lax.broadcasted_iota(jnp.int32, (CHUNK, CHUNK), 1)
    prefix = (u_i <= t_i).astype(F32)
    suffix = (u_i >= t_i).astype(F32)
    grp = min(GATE_GROUP, nc)
    rows = grp * SUBLANES
    lane = lax.broadcasted_iota(jnp.int32, (rows, LANES), 1)
    is_fwd = lax.broadcasted_iota(jnp.int32, (rows, LANES), 0) % SUBLANES < N_HEADS
    h = N_HEADS

    def running_max(a):
        fwd, bwd = a, a
        sh = 1
        while sh < CHUNK:
            fwd = jnp.where(lane >= sh, jnp.maximum(fwd, pltpu.roll(fwd, sh, axis=1)), fwd)
            bwd = jnp.where(lane < CHUNK - sh, jnp.maximum(bwd, pltpu.roll(bwd, CHUNK - sh, axis=1)), bwd)
            sh *= 2
        return jnp.where(is_fwd, fwd, bwd)

    def body(i, carry):
        c0 = i * grp
        r0 = pl.multiple_of(c0 * CHUNK, CHUNK)
        pre = g_ref[0, pl.ds(r0, grp * CHUNK), :] + gb_ref[...]
        pre_t = jnp.concatenate(
            [pre[j * CHUNK:(j + 1) * CHUNK].T[:N_GATE_COLS, :] for j in range(grp)], axis=0)
        lf_t = -(jnp.maximum(-pre_t, 0.0) + jnp.log1p(jnp.exp(-jnp.abs(pre_t))))
        cum_f = jnp.dot(lf_t, prefix, precision=HIGHEST, preferred_element_type=F32)
        cum_b = jnp.dot(lf_t, suffix, precision=HIGHEST, preferred_element_type=F32)
        pick = lambda t, j, q: t[j * N_GATE_COLS + q * h:j * N_GATE_COLS + (q + 1) * h]
        b8 = jnp.concatenate([x for j in range(grp) for x in (pick(cum_f, j, 1), pick(cum_b, j, 3))], axis=0)
        li8 = jnp.concatenate([x for j in range(grp) for x in (pick(pre_t, j, 0), pick(pre_t, j, 2))], axis=0)
        a8 = li8 - b8
        as_chunks = lambda t: t.reshape(grp, SUBLANES, LANES)
        rv_ref[0, pl.ds(c0, grp), GATE_A] = as_chunks(a8)
        rv_ref[0, pl.ds(c0, grp), GATE_CM] = as_chunks(running_max(a8))
        rv_ref[0, pl.ds(c0, grp), GATE_B] = as_chunks(b8)
        return carry

    lax.fori_loop(0, nc // grp, body, 0)


def _gates(y3d, gate_b_row):
    b, n, _ = y3d.shape
    nc = n // CHUNK
    return pl.pallas_call(
        functools.partial(_gates_kernel, nc=nc),
        out_shape=jax.ShapeDtypeStruct((b, nc, 3, SUBLANES, LANES), F32),
        grid=(b,),
        in_specs=[pl.BlockSpec((1, n, LANES), lambda i: (i, 0, COL_G // LANES)),
                  pl.BlockSpec((1, LANES), lambda i: (0, 0))],
        out_specs=pl.BlockSpec((1, nc, 3, SUBLANES, LANES), lambda i: (i, 0, 0, 0, 0)),
        compiler_params=_cparams(("arbitrary",)),
        name="gates",
    )(y3d, gate_b_row)


def _mlstm_kernel(q_ref, k_ref, v_ref, o_ref, rv_ref, c0_ref, m0_ref, hg_ref,
                  out_ref, cfin_ref, mfin_ref, hf_ref, hb_ref, cst_ref, mst_ref, *, nc):
    head = pl.program_id(1)
    cst_ref[...] = c0_ref[0, 0]
    mst_ref[...] = m0_ref[0, 0]
    s_i = lax.broadcasted_iota(jnp.int32, (CHUNK, CHUNK), 0)
    t_i = lax.broadcasted_iota(jnp.int32, (CHUNK, CHUNK), 1)
    ones_row = (s_i == 0).astype(F32)

    def one_chunk(c, d, mask_t, last, h_ref):
        r0 = pl.multiple_of(c * CHUNK, CHUNK)
        g = d * N_HEADS + head
        gate_row = lambda tbl: rv_ref[0, pl.ds(c, 1), tbl, pl.ds(g, 1), :].reshape(1, LANES)
        a_row, cm_row, b_row = gate_row(GATE_A), gate_row(GATE_CM), gate_row(GATE_B)
        q_t = q_ref[0, pl.ds(r0, CHUNK), :].T
        k = k_ref[0, pl.ds(r0, CHUNK), :]
        v_t_aug = jnp.concatenate([v_ref[0, pl.ds(r0, CHUNK), :].T, ones_row], axis=0)
        m_prev = mst_ref[d, 0:1, :]
        m_row = jnp.maximum(m_prev, cm_row)
        a_bc = jnp.broadcast_to(a_row, (CHUNK, CHUNK)).T
        d_t = jnp.where(mask_t, jnp.exp(a_bc - m_row), 0.0)
        s_t = jnp.dot(k, q_t, preferred_element_type=F32) * d_t
        cs = cst_ref[d]
        rhs = jnp.concatenate(
            [s_t.astype(BF16), (q_t.astype(F32) * jnp.exp(m_prev - m_row)).astype(BF16)], axis=0)
        lhs = jnp.concatenate([v_t_aug.astype(BF16), cs.astype(BF16)], axis=1)
        h_t = jnp.dot(lhs, rhs, preferred_element_type=F32)
        den = h_t[D_HEAD:D_HEAD + 1, :]
        floor = jnp.exp(-b_row - m_row)
        h_ref[pl.ds(c, 1)] = (h_t[:D_HEAD] / jnp.maximum(jnp.abs(den), floor))[None]
        m_last = m_row[:, last:last + 1]
        wv_t = (v_t_aug * jnp.exp(a_row - m_last)).astype(BF16)
        cst_ref[d] = jnp.exp(m_prev - m_last) * cs + jnp.dot(wv_t, k, preferred_element_type=F32)
        mst_ref[d] = jnp.broadcast_to(b_row[:, last:last + 1] + m_last, (SUBLANES, LANES))

    def body(c, carry):
        one_chunk(c, 0, s_i <= t_i, CHUNK - 1, hf_ref)
        one_chunk(nc - 1 - c, 1, s_i >= t_i, 0, hb_ref)
        return carry

    lax.fori_loop(0, nc, body, 0, unroll=min(8, nc))
    cfin_ref[0, 0] = cst_ref[...]
    mfin_ref[0, 0] = mst_ref[...]

    def readout(c, carry):
        r0 = pl.multiple_of(c * CHUNK, CHUNK)
        h_t = hf_ref[pl.ds(c, 1)][0] + hb_ref[pl.ds(c, 1)][0]
        hc = h_t - jnp.mean(h_t, axis=0, keepdims=True)
        hn = (hc * lax.rsqrt(jnp.mean(hc * hc, axis=0, keepdims=True) + LN_EPS)).T
        gate = jax.nn.sigmoid(o_ref[0, pl.ds(r0, CHUNK), :])
        out_ref[0, pl.ds(r0, CHUNK), :] = (hn * hg_ref[...] * gate).astype(out_ref.dtype)
        return carry

    lax.fori_loop(0, nc, readout, 0, unroll=min(4, nc))


def _mlstm(qk, y3d, rowv, c0, m0, head_g):
    b, n, _ = qk.shape
    nc = n // CHUNK
    v_blk = COL_V // LANES
    o_blk = COL_O // LANES
    tok = lambda off: pl.BlockSpec((1, n, LANES), lambda i, h: (i, 0, off + h))
    state_c = pl.BlockSpec((1, 1, 2, 2 * D_HEAD, D_HEAD), lambda i, h: (i, h, 0, 0, 0))
    state_m = pl.BlockSpec((1, 1, 2, SUBLANES, LANES), lambda i, h: (i, h, 0, 0, 0))
    return pl.pallas_call(
        functools.partial(_mlstm_kernel, nc=nc),
        out_shape=(jax.ShapeDtypeStruct((b, n, D_MLSTM), BF16),
                   jax.ShapeDtypeStruct((b, N_HEADS, 2, 2 * D_HEAD, D_HEAD), F32),
                   jax.ShapeDtypeStruct((b, N_HEADS, 2, SUBLANES, LANES), F32)),
        grid=(b, N_HEADS),
        in_specs=[tok(0), tok(N_HEADS), tok(v_blk), tok(o_blk),
                  pl.BlockSpec((1, nc, 3, SUBLANES, LANES), lambda i, h: (i, 0, 0, 0, 0)),
                  state_c, state_m,
                  pl.BlockSpec((1, LANES), lambda i, h: (0, h))],
        out_specs=(pl.BlockSpec((1, n, LANES), lambda i, h: (i, 0, h)), state_c, state_m),
        scratch_shapes=[pltpu.VMEM((nc, D_HEAD, CHUNK), F32), pltpu.VMEM((nc, D_HEAD, CHUNK), F32),
                        pltpu.VMEM((2, 2 * D_HEAD, D_HEAD), F32),
                        pltpu.VMEM((2, SUBLANES, LANES), F32)],
        compiler_params=_cparams(("arbitrary", "arbitrary")),
        name="mlstm",
    )(qk, qk, y3d, y3d, rowv, c0, m0, head_g)


def _dft_tables():
    r = DFT_R
    idx = np.arange(r)
    cg = np.cos(2 * np.pi * np.outer(idx, idx) / FOURIER_GROUP)
    sg = np.sin(2 * np.pi * np.outer(idx, idx) / FOURIER_GROUP)
    n_grp = D_FOURIER // FOURIER_GROUP
    bdc = np.kron(np.eye(n_grp), cg) / 8.0
    bds = np.kron(np.eye(n_grp), sg) / 8.0
    k1 = idx[:, None]
    a = idx[None, :]
    m1 = np.zeros((r, 2 * r, 2 * r))
    for b2 in range(r):
        th = 2 * np.pi * k1 * (r * a + b2) / (r * r)
        ec, es = np.cos(th), np.sin(th)
        m1[b2] = np.block([[ec, -es], [es, ec]]) / 8.0
    th2 = 2 * np.pi * np.outer(idx, idx) / r
    m2 = np.concatenate([np.cos(th2), -np.sin(th2)], axis=1) / 8.0
    as_bf16 = lambda t: jnp.asarray(t, dtype=F32).astype(BF16)
    return as_bf16(bdc), as_bf16(bds), as_bf16(m1), as_bf16(m2)


FOUR_BB = 8


N_SLABS = D_FOURIER // LANES


def _to_slabs(slab_ref, val):
    for s in range(N_SLABS):
        slab_ref[s] = val[:, s * LANES:(s + 1) * LANES]


def _strided_rows(slab_ref, start):
    return jnp.concatenate(
        [slab_ref[s, pl.ds(start, DFT_R, stride=FOUR_BB), :] for s in range(N_SLABS)], axis=1)


def _four1_kernel(f_ref, bdc_ref, bds_ref, m1_ref, z_ref, ps_ref, qs_ref):
    r = DFT_R
    fb = f_ref[0].reshape(r * FOUR_BB, D_FOURIER).astype(BF16)
    _to_slabs(ps_ref, jnp.dot(fb, bdc_ref[...], preferred_element_type=F32))
    _to_slabs(qs_ref, jnp.dot(fb, bds_ref[...], preferred_element_type=F32))
    for j in range(FOUR_BB):
        x2 = jnp.concatenate([_strided_rows(ps_ref, j), _strided_rows(qs_ref, j)], axis=0).astype(BF16)
        zz = jnp.dot(m1_ref[j], x2, preferred_element_type=F32)
        z_ref[0, j] = _pack_pair(zz[:r], zz[r:])


def _four2_kernel(z_ref, m2_ref, o_ref, zs_ref, os_ref):
    r = DFT_R
    _to_slabs(zs_ref, z_ref[0].reshape(r * FOUR_BB, D_FOURIER))
    for j in range(FOUR_BB):
        zc, zs = _unpack_pair_bf16(_strided_rows(zs_ref, j))
        o = jnp.dot(m2_ref[...], jnp.concatenate([zc, zs], axis=0), preferred_element_type=F32)
        for s in range(N_SLABS):
            os_ref[s, pl.ds(j, r, stride=FOUR_BB), :] = o[:, s * LANES:(s + 1) * LANES]
    for s in range(N_SLABS):
        o_ref[0, :, :, s * LANES:(s + 1) * LANES] = os_ref[s].reshape(r, FOUR_BB, LANES)


def _fourier(y4, tables):
    bdc, bds, m1, m2 = tables
    b = y4.shape[0]
    r = DFT_R
    nb = r // FOUR_BB
    slab = lambda dt: pltpu.VMEM((N_SLABS, r * FOUR_BB, LANES), dt)
    z = pl.pallas_call(
        _four1_kernel,
        out_shape=jax.ShapeDtypeStruct((b, r, r, D_FOURIER), jnp.int32),
        grid=(b, nb),
        in_specs=[pl.BlockSpec((1, r, FOUR_BB, D_FOURIER), lambda i, j: (i, 0, j, COL_F // D_FOURIER)),
                  pl.BlockSpec((D_FOURIER, D_FOURIER), lambda i, j: (0, 0)),
                  pl.BlockSpec((D_FOURIER, D_FOURIER), lambda i, j: (0, 0)),
                  pl.BlockSpec((FOUR_BB, 2 * r, 2 * r), lambda i, j: (j, 0, 0))],
        out_specs=pl.BlockSpec((1, FOUR_BB, r, D_FOURIER), lambda i, j: (i, j, 0, 0)),
        scratch_shapes=[slab(F32), slab(F32)],
        compiler_params=_cparams(("arbitrary", "arbitrary")),
        name="four1",
    )(y4, bdc, bds, m1)
    return pl.pallas_call(
        _four2_kernel,
        out_shape=jax.ShapeDtypeStruct((b, r, r, D_FOURIER), F32),
        grid=(b, nb),
        in_specs=[pl.BlockSpec((1, r, FOUR_BB, D_FOURIER), lambda i, j: (i, 0, j, 0)),
                  pl.BlockSpec((r, 2 * r), lambda i, j: (0, 0))],
        out_specs=pl.BlockSpec((1, r, FOUR_BB, D_FOURIER), lambda i, j: (i, 0, j, 0)),
        scratch_shapes=[slab(jnp.int32), slab(F32)],
        compiler_params=_cparams(("arbitrary", "arbitrary")),
        name="four2",
    )(z, m2)


def _outproj_kernel(x_ref, mr_ref, fo_ref, wm_ref, wf_ref, g1_ref, lg_ref, lb_ref,
                    sh_ref, sc_ref, wr_ref, x1_ref, xm_ref, s_ref):
    mix = (jnp.dot(mr_ref[...], wm_ref[...], preferred_element_type=F32)
           + jnp.dot(fo_ref[...].astype(BF16), wf_ref[...], preferred_element_type=F32))
    x1 = _layer_norm(DEEPNORM_ALPHA * x_ref[...] + g1_ref[0] * mix) * lg_ref[...] + lb_ref[...]
    x1_ref[...] = x1
    xm = _layer_norm(x1) * (1.0 + sc_ref[0]) + sh_ref[0]
    xm_ref[...] = _pack_rows(xm)
    logits = jnp.dot(xm.astype(BF16), wr_ref[...], preferred_element_type=F32)
    s_ref[...] = jax.nn.sigmoid(logits)


def _outproj(x2d, mread, four, w_m, w_f, gate1, ln_g, ln_b, shift2, scale2, w_router, n_per_batch):
    t, d = x2d.shape
    per = n_per_batch // TM
    tokd = lambda w: pl.BlockSpec((TM, w), lambda i: (i, 0))
    full = lambda r, c: pl.BlockSpec((r, c), lambda i: (0, 0))
    mod = pl.BlockSpec((1, 1, d), lambda i: (i // per, 0, 0))
    return pl.pallas_call(
        _outproj_kernel,
        out_shape=(jax.ShapeDtypeStruct((t, d), F32),
                   jax.ShapeDtypeStruct((t, PACK_W), jnp.int32),
                   jax.ShapeDtypeStruct((t, LANES), F32)),
        grid=(t // TM,),
        in_specs=[tokd(d), tokd(D_MLSTM), tokd(D_FOURIER), full(D_MLSTM, d), full(D_FOURIER, d),
                  mod, full(1, d), full(1, d), mod, mod, full(d, LANES)],
        out_specs=(tokd(d), tokd(PACK_W), tokd(LANES)),
        compiler_params=_cparams(("arbitrary",)),
        name="outproj",
    )(x2d, mread, four, w_m, w_f, gate1, ln_g, ln_b, shift2, scale2, w_router)


def _route_kernel(s_ref, b_ref, idx_ref, gw_ref, rank_ref, cnt_ref, carry_ref):
    @pl.when(pl.program_id(0) == 0)
    def _():
        carry_ref[...] = jnp.zeros_like(carry_ref)

    tm = s_ref.shape[0]
    s = s_ref[...]
    lane_i = lax.broadcasted_iota(jnp.int32, (tm, LANES), 1)
    lane = lane_i.astype(F32)
    neg_inf = jnp.float32(-jnp.inf)
    sb = jnp.where(lane_i < N_EXPERTS, s + b_ref[...], neg_inf)
    picks, sels = [], []
    taken = jnp.zeros((tm, LANES), F32)
    for _ in range(TOP_K):
        best = jnp.max(sb, axis=-1, keepdims=True)
        pick = jnp.min(jnp.where(sb == best, lane, float(LANES)), axis=-1, keepdims=True)
        onehot = lane == pick
        sels.append(jnp.sum(jnp.where(onehot, s, 0.0), axis=-1, keepdims=True))
        picks.append(pick)
        sb = jnp.where(onehot, neg_inf, sb)
        taken = taken + onehot.astype(F32)
    r_i = lax.broadcasted_iota(jnp.int32, (tm, tm), 0)
    c_i = lax.broadcasted_iota(jnp.int32, (tm, tm), 1)
    before = (c_i < r_i).astype(BF16)
    rank_all = carry_ref[0:1, :] + jnp.dot(before, taken.astype(BF16), preferred_element_type=F32)
    carry_ref[...] = carry_ref[...] + jnp.sum(taken, axis=0, keepdims=True)
    cnt_ref[...] = carry_ref[...]
    total = sels[0]
    for v in sels[1:]:
        total = total + v
    k_i = lax.broadcasted_iota(jnp.int32, (tm, TOP_K), 1)
    gw_out = jnp.zeros((tm, TOP_K), F32)
    idx_cols = jnp.zeros((tm, LANES), F32)
    rank_cols = jnp.zeros((tm, LANES), F32)
    for j in range(TOP_K):
        rank_j = jnp.sum(jnp.where(lane == picks[j], rank_all, 0.0), axis=-1, keepdims=True)
        gw_out = jnp.where(k_i == j, ROUTED_SCALE * sels[j] / total, gw_out)
        idx_cols = jnp.where(lane_i == j, picks[j], idx_cols)
        rank_cols = jnp.where(lane_i == j, rank_j, rank_cols)
    gw_ref[...] = gw_out
    idx_ref[...] = idx_cols.T[:TOP_K, :].astype(jnp.int32)
    rank_ref[...] = rank_cols.T[:TOP_K, :].astype(jnp.int32)


def _route(scores, bias_row):
    t = scores.shape[0]
    tok = lambda w: pl.BlockSpec((TM, w), lambda i: (i, 0))
    slot_major = pl.BlockSpec((TOP_K, TM), lambda i: (0, i))
    return pl.pallas_call(
        _route_kernel,
        out_shape=(jax.ShapeDtypeStruct((TOP_K, t), jnp.int32),
                   jax.ShapeDtypeStruct((t, TOP_K), F32),
                   jax.ShapeDtypeStruct((TOP_K, t), jnp.int32),
                   jax.ShapeDtypeStruct((SUBLANES, LANES), F32)),
        grid=(t // TM,),
        in_specs=[tok(LANES), pl.BlockSpec((1, LANES), lambda i: (0, 0))],
        out_specs=(slot_major, tok(TOP_K), slot_major,
                   pl.BlockSpec((SUBLANES, LANES), lambda i: (0, 0))),
        scratch_shapes=[pltpu.VMEM((SUBLANES, LANES), F32)],
        compiler_params=_cparams(("arbitrary",)),
        name="route",
    )(scores, bias_row)


TN_DEST = 2048


def _dest_kernel(ps_ref, idx_ref, rank_ref, dest_ref):
    idx = idx_ref[...]
    start = jnp.zeros(idx.shape, jnp.int32)
    for e in range(N_EXPERTS):
        start = jnp.where(idx == e, ps_ref[e], start)
    dest_ref[...] = start + rank_ref[...]


def _dest(pad_start, idx_t, rank_t):
    t = idx_t.shape[1]
    blk = lambda: pl.BlockSpec((TOP_K, TN_DEST), lambda i, ps: (0, i))
    grid_spec = pltpu.PrefetchScalarGridSpec(
        num_scalar_prefetch=1, grid=(t // TN_DEST,), in_specs=[blk(), blk()], out_specs=blk())
    return pl.pallas_call(
        _dest_kernel,
        out_shape=jax.ShapeDtypeStruct((TOP_K, t), jnp.int32),
        grid_spec=grid_spec,
        compiler_params=_cparams(("arbitrary",)),
        name="dest",
    )(pad_start, idx_t, rank_t)


def _sc_worker_id(info):
    return lax.axis_index("s") * info.num_cores + lax.axis_index("c")


def _sc_dispatch(xm, dest3, n_rows):
    t, w = xm.shape
    info = plsc.get_sparse_core_info()
    n_workers = info.num_cores * info.num_subcores
    per_worker = t // SC_ROWS // n_workers
    mesh = plsc.VectorSubcoreMesh(core_axis_name="c", subcore_axis_name="s")

    @functools.partial(
        pl.kernel, mesh=mesh,
        out_type=jax.ShapeDtypeStruct((n_rows, w), xm.dtype),
        scratch_types=[pltpu.VMEM((TOP_K, SC_ROWS), jnp.int32), pltpu.VMEM((SC_ROWS, w), xm.dtype)],
        name="sc_dispatch",
    )
    def run(xm_hbm, dest_hbm, xs_hbm, idx_v, rows_v):
        first = _sc_worker_id(info) * per_worker

        @pl.loop(0, per_worker)
        def _(ci):
            chunk = first + ci
            pltpu.sync_copy(dest_hbm.at[chunk], idx_v)
            pltpu.sync_copy(xm_hbm.at[pl.ds(chunk * SC_ROWS, SC_ROWS)], rows_v)
            for j in range(TOP_K):
                pltpu.sync_copy(rows_v, xs_hbm.at[idx_v.at[j]])

    return run(xm, dest3)


def _sc_gather(table, idx2):
    n_chunks, rows = idx2.shape
    w = table.shape[1]
    info = plsc.get_sparse_core_info()
    n_workers = info.num_cores * info.num_subcores
    per_worker = n_chunks // n_workers
    assert per_worker % 2 == 0
    mesh = plsc.VectorSubcoreMesh(core_axis_name="c", subcore_axis_name="s")

    @functools.partial(
        pl.kernel, mesh=mesh,
        out_type=jax.ShapeDtypeStruct((n_chunks * rows, w), table.dtype),
        scratch_types=[pltpu.VMEM((2, rows), jnp.int32), pltpu.VMEM((2, rows, w), table.dtype),
                       pltpu.SemaphoreType.DMA, pltpu.SemaphoreType.DMA],
        name="sc_gather",
    )
    def run(table_hbm, idx_hbm, out_hbm, idx_v, rows_v, sem0, sem1):
        first = _sc_worker_id(info) * per_worker
        sems = (sem0, sem1)

        def gather(b):
            return pltpu.make_async_copy(table_hbm.at[idx_v.at[b]], rows_v.at[b], sems[b])

        def start(chunk, b):
            pltpu.sync_copy(idx_hbm.at[chunk], idx_v.at[b])
            gather(b).start()

        def finish(chunk, b):
            gather(b).wait()
            pltpu.sync_copy(rows_v.at[b], out_hbm.at[pl.ds(chunk * rows, rows)])

        start(first, 0)

        @pl.loop(0, per_worker, step=2)
        def _(ci):
            chunk = first + ci
            start(chunk + 1, 1)
            finish(chunk, 0)

            @pl.when(ci + 2 < per_worker)
            def _():
                start(chunk + 2, 0)

            finish(chunk + 1, 1)

    return run(table, idx2)


def _expert_kernel(be_ref, bv_ref, bs_ref, xs_ref, wgu_ref, wd_ref, y_ref, wgu_bf, wd_bf):
    del bs_ref
    i = pl.program_id(0)
    n_valid = bv_ref[i]

    @pl.when((i == 0) | (be_ref[i] != be_ref[jnp.maximum(i - 1, 0)]))
    def _():
        wgu_bf[...] = wgu_ref[0].astype(BF16)
        wd_bf[...] = wd_ref[0].astype(BF16)

    @pl.when(n_valid > 0)
    def _():
        for h in range(ROW_BLK // ROW_HALF):
            r0 = h * ROW_HALF
            row = lax.broadcasted_iota(jnp.int32, (ROW_HALF, PACK_W), 0) + r0
            x = _unpack_bf16(jnp.where(row < n_valid, xs_ref[r0:r0 + ROW_HALF, :], 0))
            gu = jnp.dot(x, wgu_bf[...], preferred_element_type=F32)
            act = _silu(gu[:, :D_EXPERT]) * gu[:, D_EXPERT:]
            yv = jnp.dot(act.astype(BF16), wd_bf[...], preferred_element_type=F32)
            words = pltpu.pack_elementwise([yv[:, :PACK_W], yv[:, PACK_W:]], packed_dtype=BF16)
            y_ref[r0:r0 + ROW_HALF, :] = lax.bitcast_convert_type(words, jnp.int32)


def _experts(block_e, block_valid, block_src, xs, w_gu, w_down):
    n_rows, w = xs.shape
    d = D_MODEL
    grid_spec = pltpu.PrefetchScalarGridSpec(
        num_scalar_prefetch=3,
        grid=(n_rows // ROW_BLK,),
        in_specs=[pl.BlockSpec((ROW_BLK, w), lambda i, be, bv, bs: (bs[i], 0)),
                  pl.BlockSpec((1, d, 2 * D_EXPERT), lambda i, be, bv, bs: (be[i], 0, 0)),
                  pl.BlockSpec((1, D_EXPERT, d), lambda i, be, bv, bs: (be[i], 0, 0))],
        out_specs=pl.BlockSpec((ROW_BLK, w), lambda i, be, bv, bs: (bs[i], 0)),
        scratch_shapes=[pltpu.VMEM((d, 2 * D_EXPERT), BF16), pltpu.VMEM((D_EXPERT, d), BF16)],
    )
    return pl.pallas_call(
        _expert_kernel,
        out_shape=jax.ShapeDtypeStruct((n_rows, w), jnp.int32),
        grid_spec=grid_spec,
        compiler_params=_cparams(("arbitrary",)),
        name="experts",
    )(block_e, block_valid, block_src, xs, w_gu, w_down)


TM_COMBINE = 256


def _combine_kernel(*refs):
    yg_refs = refs[:TOP_K]
    gw_ref, x1_ref, xm_ref, g2_ref, wsgu_ref, wsd_ref, lg_ref, lb_ref, out_ref = refs[TOP_K:]
    gu = jnp.dot(_unpack_bf16(xm_ref[...]), wsgu_ref[...], preferred_element_type=F32)
    act = _silu(gu[:, :D_EXPERT]) * gu[:, D_EXPERT:]
    acc = jnp.dot(act.astype(BF16), wsd_ref[...], preferred_element_type=F32)
    gw = gw_ref[...]
    for j in range(TOP_K):
        acc = acc + gw[:, j:j + 1] * _unpack_rows(yg_refs[j][...])
    z = DEEPNORM_ALPHA * x1_ref[...] + g2_ref[0] * acc
    out_ref[...] = _layer_norm(z) * lg_ref[...] + lb_ref[...]


def _combine(yg, gw, x1, xm, gate2, w_sgu, w_sd, ln_g, ln_b, n_per_batch):
    t, d = x1.shape
    tm = TM_COMBINE
    per = n_per_batch // tm
    steps = t // tm
    tok = lambda w: pl.BlockSpec((tm, w), lambda i: (i, 0))
    full = lambda r, c: pl.BlockSpec((r, c), lambda i: (0, 0))
    slot = lambda j: pl.BlockSpec((tm, PACK_W), lambda i: (j * steps + i, 0))
    return pl.pallas_call(
        _combine_kernel,
        out_shape=jax.ShapeDtypeStruct((t, d), F32),
        grid=(steps,),
        in_specs=[slot(j) for j in range(TOP_K)]
        + [tok(TOP_K), tok(d), tok(PACK_W), pl.BlockSpec((1, 1, d), lambda i: (i // per, 0, 0)),
           full(d, 2 * D_EXPERT), full(D_EXPERT, d), full(1, d), full(1, d)],
        out_specs=tok(d),
        compiler_params=_cparams(("arbitrary",)),
        name="combine",
    )(*([yg] * TOP_K), gw, x1, xm, gate2, w_sgu, w_sd, ln_g, ln_b)


def _mixer_heads(stream, shift, scale, w_in_k, conv_w9, conv_b, gate_b_row, head_g, c0, m0, grid2d):
    b, n, d = stream.shape
    y = _inproj(stream.reshape(b * n, d), shift, scale, w_in_k, n).reshape(b, n, PROJ_W)
    qk = _conv(y, conv_w9, conv_b, grid2d)
    mread, c_fin, m_fin = _mlstm(qk, y, _gates(y, gate_b_row), c0, m0, head_g)
    return y, mread, c_fin, m_fin


MOE_GROUPS = 1


def _moe_group(xm, x1, scores, bias_row, gate2, w_gu, w_down, w_sgu, w_sd, ln_g, ln_b, n_per_batch):
    t = xm.shape[0]
    n_assign = t * TOP_K
    n_rows = n_assign + N_EXPERTS * ROW_BLK
    idx_t, gw, rank_t, counts = _route(scores, bias_row)
    sizes = counts[0, :N_EXPERTS].astype(jnp.int32)
    padded = (sizes + ROW_BLK - 1) // ROW_BLK * ROW_BLK
    pad_end = jnp.cumsum(padded)
    pad_start = (pad_end - padded).astype(jnp.int32)
    blk_first = jnp.arange(n_rows // ROW_BLK, dtype=jnp.int32) * ROW_BLK
    block_e = jnp.minimum(jnp.sum(blk_first[:, None] >= pad_end[None, :], axis=1),
                          N_EXPERTS - 1).astype(jnp.int32)
    is_e = block_e[:, None] == jnp.arange(N_EXPERTS, dtype=jnp.int32)[None, :]
    valid_end = jnp.sum(jnp.where(is_e, (pad_start + sizes)[None, :], 0), axis=1)
    block_valid = jnp.clip(valid_end - blk_first, 0, ROW_BLK)
    block_valid = jnp.where(blk_first < pad_end[-1], block_valid, 0).astype(jnp.int32)
    block_src = jnp.minimum(blk_first, pad_end[-1] - ROW_BLK) // ROW_BLK
    dest_t = _dest(pad_start, idx_t, rank_t)
    dest3 = dest_t.reshape(TOP_K, t // SC_ROWS, SC_ROWS).transpose(1, 0, 2)

    xs = _sc_dispatch(xm, dest3, n_rows)
    yb = _experts(block_e, block_valid, block_src.astype(jnp.int32), xs, w_gu, w_down)
    yg = _sc_gather(yb, dest_t.reshape(n_assign // SC_GATHER_ROWS, SC_GATHER_ROWS))
    return _combine(yg, gw, x1, xm, gate2, w_sgu, w_sd, ln_g, ln_b, n_per_batch)


def kernel(x, c, ctx, c_ctx, w_ada, b_ada, w_in, conv_w, conv_b, gate_b, head_g, w_out, ln1_g, ln1_b,
           w_router, router_bias, w_expert_gu, w_expert_down, w_shared_gu, w_shared_down, ln2_g, ln2_b):
    bsz, n, d = x.shape
    n_ctx = ctx.shape[1]
    depth = w_ada.shape[0]
    t = bsz * n
    tables = _dft_tables()
    for l in range(depth):
        cvec = jnp.zeros((SUBLANES, d), F32).at[:bsz].set(c).at[bsz].set(c_ctx)
        mod = _ada(cvec, w_ada[l], b_ada[l][None, :])
        mods = mod.reshape(SUBLANES, 6, 1, d)
        mod_x = [mods[:bsz, j] for j in range(6)]
        mod_c = [jnp.broadcast_to(mods[bsz:bsz + 1, j], (bsz, 1, d)) for j in range(6)]

        wl = w_in[l]
        g0 = 4 * D_MLSTM
        w_in_k = jnp.concatenate(
            [wl[:, :g0], wl[:, g0 + N_GATE_COLS:], wl[:, g0:g0 + N_GATE_COLS],
             jnp.zeros((d, LANES - N_GATE_COLS), F32)], axis=1).astype(BF16)
        conv_w9 = conv_w[l].reshape(9, 2 * D_MLSTM)
        conv_b_row = conv_b[l][None, :]
        gate_b_row = jnp.zeros((1, LANES), F32).at[0, :N_GATE_COLS].set(gate_b[l].reshape(-1))
        head_g_row = head_g[l][None, :]

        c0 = jnp.zeros((bsz, N_HEADS, 2, 2 * D_HEAD, D_HEAD), F32)
        m0 = jnp.zeros((bsz, N_HEADS, 2, SUBLANES, LANES), F32)
        _, _, c_ctx_fin, m_ctx_fin = _mixer_heads(ctx, mod_c[0], mod_c[1], w_in_k, conv_w9, conv_b_row,
                                                  gate_b_row, head_g_row, c0, m0, grid2d=False)
        y, mread, _, _ = _mixer_heads(x, mod_x[0], mod_x[1], w_in_k, conv_w9, conv_b_row,
                                      gate_b_row, head_g_row, c_ctx_fin, m_ctx_fin, grid2d=True)

        four = _fourier(y.reshape(bsz, DFT_R, DFT_R, PROJ_W), tables).reshape(t, D_FOURIER)

        w_o = w_out[l].astype(BF16)
        w_r = jnp.zeros((d, LANES), F32).at[:, :N_EXPERTS].set(w_router[l]).astype(BF16)
        x2d = x.reshape(t, d)
        x1, xm, scores = _outproj(x2d, mread.reshape(t, D_MLSTM), four, w_o[:D_MLSTM], w_o[D_MLSTM:],
                                  mod_x[2], ln1_g[l][None, :], ln1_b[l][None, :], mod_x[3], mod_x[4],
                                  w_r, n)

        bias_row = jnp.zeros((1, LANES), F32).at[0, :N_EXPERTS].set(router_bias[l])
        w_sgu = w_shared_gu[l].astype(BF16)
        w_sd = w_shared_down[l].astype(BF16)
        tg = t // MOE_GROUPS
        bg = bsz // MOE_GROUPS
        outs = []
        for grp in range(MOE_GROUPS):
            tok = slice(grp * tg, (grp + 1) * tg)
            outs.append(_moe_group(xm[tok], x1[tok], scores[tok], bias_row, mod_x[5][grp * bg:(grp + 1) * bg],
                                   w_expert_gu[l], w_expert_down[l], w_sgu, w_sd,
                                   ln2_g[l][None, :], ln2_b[l][None, :], n))
        x = jnp.concatenate(outs, axis=0).reshape(bsz, n, d)
        assert depth == 1
    return x
```

```python
import functools
import math

import numpy as np
import jax
import jax.numpy as jnp
from jax import lax
from jax.experimental import pallas as pl
from jax.experimental.pallas import tpu as pltpu
from jax.experimental.pallas import tpu_sc as plsc

F32 = jnp.float32
BF16 = jnp.bfloat16
HIGHEST = lax.Precision.HIGHEST

D_MODEL = 1024
GRID_W = 64
N_HEADS = 4
D_HEAD = 128
D_MLSTM = N_HEADS * D_HEAD
D_FOURIER = 512
FOURIER_GROUP = 64
N_GATE_COLS = 16
CHUNK = 128
N_EXPERTS = 64
TOP_K = 8
D_EXPERT = 256
ROUTED_SCALE = 2.5
DEEPNORM_ALPHA = 2.0 ** 0.25
LN_EPS = 1e-5

LANES = 128
SUBLANES = 8
VMEM_LIMIT = 48 * 1024 * 1024

COL_QK = 0
COL_V = 2 * D_MLSTM
COL_O = 3 * D_MLSTM
COL_F = 4 * D_MLSTM
COL_G = COL_F + D_FOURIER
PROJ_W = COL_G + LANES

TM = 512
ROW_BLK = 512
ROW_HALF = 256
PACK_W = D_MODEL // 2
SC_ROWS = 128
SC_GATHER_ROWS = 64
CONV_PAD = 72
CONV_RB = 256
DFT_R = 64


def _cparams(sem, vmem=VMEM_LIMIT):
    return pltpu.CompilerParams(dimension_semantics=sem, vmem_limit_bytes=vmem)


def _layer_norm(x):
    mu = jnp.mean(x, axis=-1, keepdims=True)
    xc = x - mu
    var = jnp.mean(xc * xc, axis=-1, keepdims=True)
    return xc * lax.rsqrt(var + LN_EPS)


def _silu(x):
    return x * jax.nn.sigmoid(x)


def _pack_pair(lo, hi):
    as_bits = lambda u: lax.bitcast_convert_type(u.astype(BF16).astype(F32), jnp.uint32)
    word = as_bits(hi) | lax.shift_right_logical(as_bits(lo), jnp.uint32(16))
    return lax.bitcast_convert_type(word, jnp.int32)


def _unpack_pair_bf16(w):
    lo = lax.bitcast_convert_type(w.astype(jnp.int16), BF16)
    hi = lax.bitcast_convert_type(lax.shift_right_logical(w, jnp.int32(16)).astype(jnp.int16), BF16)
    return lo, hi


def _pack_rows(v):
    return _pack_pair(v[:, :PACK_W], v[:, PACK_W:])


def _unpack_bf16(w):
    return jnp.concatenate(_unpack_pair_bf16(w), axis=-1)


def _unpack_rows(w):
    u = lax.bitcast_convert_type(w, jnp.uint32)
    lo = lax.bitcast_convert_type(lax.shift_left(u, jnp.uint32(16)), F32)
    hi = lax.bitcast_convert_type(u & jnp.uint32(0xFFFF0000), F32)
    return jnp.concatenate([lo, hi], axis=1)


def _ada_kernel(c_ref, w_ref, b_ref, o_ref):
    s = _silu(c_ref[...])
    o_ref[...] = jnp.dot(s, w_ref[...], precision=HIGHEST, preferred_element_type=F32) + b_ref[...]


def _ada(cvec, w_ada, b_ada):
    rows, d = cvec.shape
    n_out = w_ada.shape[1]
    tn = 1024
    return pl.pallas_call(
        _ada_kernel,
        out_shape=jax.ShapeDtypeStruct((rows, n_out), F32),
        grid=(n_out // tn,),
        in_specs=[pl.BlockSpec((rows, d), lambda j: (0, 0)),
                  pl.BlockSpec((d, tn), lambda j: (0, j)),
                  pl.BlockSpec((1, tn), lambda j: (0, j))],
        out_specs=pl.BlockSpec((rows, tn), lambda j: (0, j)),
        compiler_params=_cparams(("arbitrary",)),
        name="ada",
    )(cvec, w_ada, b_ada)


def _inproj_kernel(x_ref, sh_ref, sc_ref, w_ref, y_ref):
    u = _layer_norm(x_ref[...]) * (1.0 + sc_ref[0]) + sh_ref[0]
    y_ref[...] = jnp.dot(u.astype(BF16), w_ref[...], preferred_element_type=F32)


def _inproj(x2d, shift, scale, w, n_per_batch):
    t, d = x2d.shape
    tm = min(TM, n_per_batch)
    per = n_per_batch // tm
    return pl.pallas_call(
        _inproj_kernel,
        out_shape=jax.ShapeDtypeStruct((t, PROJ_W), F32),
        grid=(t // tm,),
        in_specs=[pl.BlockSpec((tm, d), lambda i: (i, 0)),
                  pl.BlockSpec((1, 1, d), lambda i: (i // per, 0, 0)),
                  pl.BlockSpec((1, 1, d), lambda i: (i // per, 0, 0)),
                  pl.BlockSpec((d, PROJ_W), lambda i: (0, 0))],
        out_specs=pl.BlockSpec((tm, PROJ_W), lambda i: (i, 0)),
        compiler_params=_cparams(("arbitrary",)),
        name="inproj",
    )(x2d, shift, scale, w)


def _conv_kernel(y_ref, w_ref, b_ref, o_ref, pad_ref, *, n, grid2d):
    zeros = jnp.zeros((CONV_PAD, LANES), F32)
    pad_ref[0:CONV_PAD, :] = zeros
    pad_ref[CONV_PAD + n:CONV_PAD + n + CONV_PAD, :] = zeros
    pad_ref[CONV_PAD:CONV_PAD + n, :] = y_ref[0]
    k_scale = jnp.where(pl.program_id(1) >= N_HEADS, D_HEAD ** -0.5, 1.0).astype(F32)
    bias = b_ref[...]
    rb = min(CONV_RB, n)
    col = lax.broadcasted_iota(jnp.int32, (rb, LANES), 0) % GRID_W
    not_first = col >= 1
    not_last = col <= GRID_W - 2
    row_taps = (0, 1, 2) if grid2d else (1,)
    for blk in range(n // rb):
        r0 = CONV_PAD + blk * rb
        acc = jnp.zeros((rb, LANES), F32)
        for dj in range(3):
            part = jnp.zeros((rb, LANES), F32)
            for di in row_taps:
                off = (di - 1) * GRID_W + (dj - 1)
                tap = w_ref[di * 3 + dj:di * 3 + dj + 1, :]
                part = part + tap * pad_ref[r0 + off:r0 + off + rb, :]
            if grid2d and dj == 0:
                part = jnp.where(not_first, part, 0.0)
            if grid2d and dj == 2:
                part = jnp.where(not_last, part, 0.0)
            acc = acc + part
        val = _silu(acc + bias) * k_scale
        o_ref[0, blk * rb:(blk + 1) * rb, :] = val.astype(o_ref.dtype)


def _conv(y3d, conv_w9, conv_b, grid2d):
    b, n, _ = y3d.shape
    n_ct = 2 * D_MLSTM // LANES
    return pl.pallas_call(
        functools.partial(_conv_kernel, n=n, grid2d=grid2d),
        out_shape=jax.ShapeDtypeStruct((b, n, 2 * D_MLSTM), BF16),
        grid=(b, n_ct),
        in_specs=[pl.BlockSpec((1, n, LANES), lambda i, c: (i, 0, c)),
                  pl.BlockSpec((9, LANES), lambda i, c: (0, c)),
                  pl.BlockSpec((1, LANES), lambda i, c: (0, c))],
        out_specs=pl.BlockSpec((1, n, LANES), lambda i, c: (i, 0, c)),
        scratch_shapes=[pltpu.VMEM((n + 2 * CONV_PAD, LANES), F32)],
        compiler_params=_cparams(("arbitrary", "arbitrary")),
        name="conv2d" if grid2d else "conv1d",
    )(y3d, conv_w9, conv_b)


GATE_A, GATE_CM, GATE_B = 0, 1, 2
GATE_GROUP = 8


def _gates_kernel(g_ref, gb_ref, rv_ref, *, nc):
    u_i = lax.broadcasted_iota(jnp.int32, (CHUNK, CHUNK), 0)
    t_i = lax.broadcasted_iota(jnp.int32, (CHUNK, CHUNK), 1)
    prefix = (u_i <= t_i).astype(F32)
    suffix = (u_i >= t_i).astype(F32)
    grp = min(GATE_GROUP, nc)
    rows = grp * SUBLANES
    lane = lax.broadcasted_iota(jnp.int32, (rows, LANES), 1)
    is_fwd = lax.broadcasted_iota(jnp.int32, (rows, LANES), 0) % SUBLANES < N_HEADS
    h = N_HEADS

    def running_max(a):
        fwd, bwd = a, a
        sh = 1
        while sh < CHUNK:
            fwd = jnp.where(lane >= sh, jnp.maximum(fwd, pltpu.roll(fwd, sh, axis=1)), fwd)
            bwd = jnp.where(lane < CHUNK - sh, jnp.maximum(bwd, pltpu.roll(bwd, CHUNK - sh, axis=1)), bwd)
            sh *= 2
        return jnp.where(is_fwd, fwd, bwd)

    def body(i, carry):
        c0 = i * grp
        r0 = pl.multiple_of(c0 * CHUNK, CHUNK)
        pre = g_ref[0, pl.ds(r0, grp * CHUNK), :] + gb_ref[...]
        pre_t = jnp.concatenate(
            [pre[j * CHUNK:(j + 1) * CHUNK].T[:N_GATE_COLS, :] for j in range(grp)], axis=0)
        lf_t = -(jnp.maximum(-pre_t, 0.0) + jnp.log1p(jnp.exp(-jnp.abs(pre_t))))
        cum_f = jnp.dot(lf_t, prefix, precision=HIGHEST, preferred_element_type=F32)
        cum_b = jnp.dot(lf_t, suffix, precision=HIGHEST, preferred_element_type=F32)
        pick = lambda t, j, q: t[j * N_GATE_COLS + q * h:j * N_GATE_COLS + (q + 1) * h]
        b8 = jnp.concatenate([x for j in range(grp) for x in (pick(cum_f, j, 1), pick(cum_b, j, 3))], axis=0)
        li8 = jnp.concatenate([x for j in range(grp) for x in (pick(pre_t, j, 0), pick(pre_t, j, 2))], axis=0)
        a8 = li8 - b8
        as_chunks = lambda t: t.reshape(grp, SUBLANES, LANES)
        rv_ref[0, pl.ds(c0, grp), GATE_A] = as_chunks(a8)
        rv_ref[0, pl.ds(c0, grp), GATE_CM] = as_chunks(running_max(a8))
        rv_ref[0, pl.ds(c0, grp), GATE_B] = as_chunks(b8)
        return carry

    lax.fori_loop(0, nc // grp, body, 0)


def _gates(y3d, gate_b_row):
    b, n, _ = y3d.shape
    nc = n // CHUNK
    return pl.pallas_call(
        functools.partial(_gates_kernel, nc=nc),
        out_shape=jax.ShapeDtypeStruct((b, nc, 3, SUBLANES, LANES), F32),
        grid=(b,),
        in_specs=[pl.BlockSpec((1, n, LANES), lambda i: (i, 0, COL_G // LANES)),
                  pl.BlockSpec((1, LANES), lambda i: (0, 0))],
        out_specs=pl.BlockSpec((1, nc, 3, SUBLANES, LANES), lambda i: (i, 0, 0, 0, 0)),
        compiler_params=_cparams(("arbitrary",)),
        name="gates",
    )(y3d, gate_b_row)


def _mlstm_kernel(q_ref, k_ref, v_ref, o_ref, rv_ref, c0_ref, m0_ref, hg_ref,
                  out_ref, cfin_ref, mfin_ref, hf_ref, hb_ref, cst_ref, mst_ref, *, nc):
    head = pl.program_id(1)
    cst_ref[...] = c0_ref[0, 0]
    mst_ref[...] = m0_ref[0, 0]
    s_i = lax.broadcasted_iota(jnp.int32, (CHUNK, CHUNK), 0)
    t_i = lax.broadcasted_iota(jnp.int32, (CHUNK, CHUNK), 1)
    ones_row = (s_i == 0).astype(F32)

    def one_chunk(c, d, mask_t, last, h_ref):
        r0 = pl.multiple_of(c * CHUNK, CHUNK)
        g = d * N_HEADS + head
        gate_row = lambda tbl: rv_ref[0, pl.ds(c, 1), tbl, pl.ds(g, 1), :].reshape(1, LANES)
        a_row, cm_row, b_row = gate_row(GATE_A), gate_row(GATE_CM), gate_row(GATE_B)
        q_t = q_ref[0, pl.ds(r0, CHUNK), :].T
        k = k_ref[0, pl.ds(r0, CHUNK), :]
        v_t_aug = jnp.concatenate([v_ref[0, pl.ds(r0, CHUNK), :].T, ones_row], axis=0)
        m_prev = mst_ref[d, 0:1, :]
        m_row = jnp.maximum(m_prev, cm_row)
        a_bc = jnp.broadcast_to(a_row, (CHUNK, CHUNK)).T
        d_t = jnp.where(mask_t, jnp.exp(a_bc - m_row), 0.0)
        s_t = jnp.dot(k, q_t, preferred_element_type=F32) * d_t
        cs = cst_ref[d]
        rhs = jnp.concatenate(
            [s_t.astype(BF16), (q_t.astype(F32) * jnp.exp(m_prev - m_row)).astype(BF16)], axis=0)
        lhs = jnp.concatenate([v_t_aug.astype(BF16), cs.astype(BF16)], axis=1)
        h_t = jnp.dot(lhs, rhs, preferred_element_type=F32)
        den = h_t[D_HEAD:D_HEAD + 1, :]
        floor = jnp.exp(-b_row - m_row)
        h_ref[pl.ds(c, 1)] = (h_t[:D_HEAD] / jnp.maximum(jnp.abs(den), floor))[None]
        m_last = m_row[:, last:last + 1]
        wv_t = (v_t_aug * jnp.exp(a_row - m_last)).astype(BF16)
        cst_ref[d] = jnp.exp(m_prev - m_last) * cs + jnp.dot(wv_t, k, preferred_element_type=F32)
        mst_ref[d] = jnp.broadcast_to(b_row[:, last:last + 1] + m_last, (SUBLANES, LANES))

    def body(c, carry):
        one_chunk(c, 0, s_i <= t_i, CHUNK - 1, hf_ref)
        one_chunk(nc - 1 - c, 1, s_i >= t_i, 0, hb_ref)
        return carry

    lax.fori_loop(0, nc, body, 0, unroll=min(8, nc))
    cfin_ref[0, 0] = cst_ref[...]
    mfin_ref[0, 0] = mst_ref[...]

    def readout(c, carry):
        r0 = pl.multiple_of(c * CHUNK, CHUNK)
        h_t = hf_ref[pl.ds(c, 1)][0] + hb_ref[pl.ds(c, 1)][0]
        hc = h_t - jnp.mean(h_t, axis=0, keepdims=True)
        hn = (hc * lax.rsqrt(jnp.mean(hc * hc, axis=0, keepdims=True) + LN_EPS)).T
        gate = jax.nn.sigmoid(o_ref[0, pl.ds(r0, CHUNK), :])
        out_ref[0, pl.ds(r0, CHUNK), :] = (hn * hg_ref[...] * gate).astype(out_ref.dtype)
        return carry

    lax.fori_loop(0, nc, readout, 0, unroll=min(4, nc))


def _mlstm(qk, y3d, rowv, c0, m0, head_g):
    b, n, _ = qk.shape
    nc = n // CHUNK
    v_blk = COL_V // LANES
    o_blk = COL_O // LANES
    tok = lambda off: pl.BlockSpec((1, n, LANES), lambda i, h: (i, 0, off + h))
    state_c = pl.BlockSpec((1, 1, 2, 2 * D_HEAD, D_HEAD), lambda i, h: (i, h, 0, 0, 0))
    state_m = pl.BlockSpec((1, 1, 2, SUBLANES, LANES), lambda i, h: (i, h, 0, 0, 0))
    return pl.pallas_call(
        functools.partial(_mlstm_kernel, nc=nc),
        out_shape=(jax.ShapeDtypeStruct((b, n, D_MLSTM), BF16),
                   jax.ShapeDtypeStruct((b, N_HEADS, 2, 2 * D_HEAD, D_HEAD), F32),
                   jax.ShapeDtypeStruct((b, N_HEADS, 2, SUBLANES, LANES), F32)),
        grid=(b, N_HEADS),
        in_specs=[tok(0), tok(N_HEADS), tok(v_blk), tok(o_blk),
                  pl.BlockSpec((1, nc, 3, SUBLANES, LANES), lambda i, h: (i, 0, 0, 0, 0)),
                  state_c, state_m,
                  pl.BlockSpec((1, LANES), lambda i, h: (0, h))],
        out_specs=(pl.BlockSpec((1, n, LANES), lambda i, h: (i, 0, h)), state_c, state_m),
        scratch_shapes=[pltpu.VMEM((nc, D_HEAD, CHUNK), F32), pltpu.VMEM((nc, D_HEAD, CHUNK), F32),
                        pltpu.VMEM((2, 2 * D_HEAD, D_HEAD), F32),
                        pltpu.VMEM((2, SUBLANES, LANES), F32)],
        compiler_params=_cparams(("arbitrary", "arbitrary")),
        name="mlstm",
    )(qk, qk, y3d, y3d, rowv, c0, m0, head_g)


def _dft_tables():
    r = DFT_R
    idx = np.arange(r)
    cg = np.cos(2 * np.pi * np.outer(idx, idx) / FOURIER_GROUP)
    sg = np.sin(2 * np.pi * np.outer(idx, idx) / FOURIER_GROUP)
    n_grp = D_FOURIER // FOURIER_GROUP
    bdc = np.kron(np.eye(n_grp), cg) / 8.0
    bds = np.kron(np.eye(n_grp), sg) / 8.0
    k1 = idx[:, None]
    a = idx[None, :]
    m1 = np.zeros((r, 2 * r, 2 * r))
    for b2 in range(r):
        th = 2 * np.pi * k1 * (r * a + b2) / (r * r)
        ec, es = np.cos(th), np.sin(th)
        m1[b2] = np.block([[ec, -es], [es, ec]]) / 8.0
    th2 = 2 * np.pi * np.outer(idx, idx) / r
    m2 = np.concatenate([np.cos(th2), -np.sin(th2)], axis=1) / 8.0
    as_bf16 = lambda t: jnp.asarray(t, dtype=F32).astype(BF16)
    return as_bf16(bdc), as_bf16(bds), as_bf16(m1), as_bf16(m2)


FOUR_BB = 8


N_SLABS = D_FOURIER // LANES


def _to_slabs(slab_ref, val):
    for s in range(N_SLABS):
        slab_ref[s] = val[:, s * LANES:(s + 1) * LANES]


def _strided_rows(slab_ref, start):
    return jnp.concatenate(
        [slab_ref[s, pl.ds(start, DFT_R, stride=FOUR_BB), :] for s in range(N_SLABS)], axis=1)


def _four1_kernel(f_ref, bdc_ref, bds_ref, m1_ref, z_ref, ps_ref, qs_ref):
    r = DFT_R
    fb = f_ref[0].reshape(r * FOUR_BB, D_FOURIER).astype(BF16)
    _to_slabs(ps_ref, jnp.dot(fb, bdc_ref[...], preferred_element_type=F32))
    _to_slabs(qs_ref, jnp.dot(fb, bds_ref[...], preferred_element_type=F32))
    for j in range(FOUR_BB):
        x2 = jnp.concatenate([_strided_rows(ps_ref, j), _strided_rows(qs_ref, j)], axis=0).astype(BF16)
        zz = jnp.dot(m1_ref[j], x2, preferred_element_type=F32)
        z_ref[0, j] = _pack_pair(zz[:r], zz[r:])


def _four2_kernel(z_ref, m2_ref, o_ref, zs_ref, os_ref):
    r = DFT_R
    _to_slabs(zs_ref, z_ref[0].reshape(r * FOUR_BB, D_FOURIER))
    for j in range(FOUR_BB):
        zc, zs = _unpack_pair_bf16(_strided_rows(zs_ref, j))
        o = jnp.dot(m2_ref[...], jnp.concatenate([zc, zs], axis=0), preferred_element_type=F32)
        for s in range(N_SLABS):
            os_ref[s, pl.ds(j, r, stride=FOUR_BB), :] = o[:, s * LANES:(s + 1) * LANES]
    for s in range(N_SLABS):
        o_ref[0, :, :, s * LANES:(s + 1) * LANES] = os_ref[s].reshape(r, FOUR_BB, LANES)


def _fourier(y4, tables):
    bdc, bds, m1, m2 = tables
    b = y4.shape[0]
    r = DFT_R
    nb = r // FOUR_BB
    slab = lambda dt: pltpu.VMEM((N_SLABS, r * FOUR_BB, LANES), dt)
    z = pl.pallas_call(
        _four1_kernel,
        out_shape=jax.ShapeDtypeStruct((b, r, r, D_FOURIER), jnp.int32),
        grid=(b, nb),
        in_specs=[pl.BlockSpec((1, r, FOUR_BB, D_FOURIER), lambda i, j: (i, 0, j, COL_F // D_FOURIER)),
                  pl.BlockSpec((D_FOURIER, D_FOURIER), lambda i, j: (0, 0)),
                  pl.BlockSpec((D_FOURIER, D_FOURIER), lambda i, j: (0, 0)),
                  pl.BlockSpec((FOUR_BB, 2 * r, 2 * r), lambda i, j: (j, 0, 0))],
        out_specs=pl.BlockSpec((1, FOUR_BB, r, D_FOURIER), lambda i, j: (i, j, 0, 0)),
        scratch_shapes=[slab(F32), slab(F32)],
        compiler_params=_cparams(("arbitrary", "arbitrary")),
        name="four1",
    )(y4, bdc, bds, m1)
    return pl.pallas_call(
        _four2_kernel,
        out_shape=jax.ShapeDtypeStruct((b, r, r, D_FOURIER), F32),
        grid=(b, nb),
        in_specs=[pl.BlockSpec((1, r, FOUR_BB, D_FOURIER), lambda i, j: (i, 0, j, 0)),
                  pl.BlockSpec((r, 2 * r), lambda i, j: (0, 0))],
        out_specs=pl.BlockSpec((1, r, FOUR_BB, D_FOURIER), lambda i, j: (i, 0, j, 0)),
        scratch_shapes=[slab(jnp.int32), slab(F32)],
        compiler_params=_cparams(("arbitrary", "arbitrary")),
        name="four2",
    )(z, m2)


def _outproj_kernel(x_ref, mr_ref, fo_ref, wm_ref, wf_ref, g1_ref, lg_ref, lb_ref,
                    sh_ref, sc_ref, wr_ref, x1_ref, xm_ref, s_ref):
    mix = (jnp.dot(mr_ref[...], wm_ref[...], preferred_element_type=F32)
           + jnp.dot(fo_ref[...].astype(BF16), wf_ref[...], preferred_element_type=F32))
    x1 = _layer_norm(DEEPNORM_ALPHA * x_ref[...] + g1_ref[0] * mix) * lg_ref[...] + lb_ref[...]
    x1_ref[...] = x1
    xm = _layer_norm(x1) * (1.0 + sc_ref[0]) + sh_ref[0]
    xm_ref[...] = _pack_rows(xm)
    logits = jnp.dot(xm.astype(BF16), wr_ref[...], preferred_element_type=F32)
    s_ref[...] = jax.nn.sigmoid(logits)


def _outproj(x2d, mread, four, w_m, w_f, gate1, ln_g, ln_b, shift2, scale2, w_router, n_per_batch):
    t, d = x2d.shape
    per = n_per_batch // TM
    tokd = lambda w: pl.BlockSpec((TM, w), lambda i: (i, 0))
    full = lambda r, c: pl.BlockSpec((r, c), lambda i: (0, 0))
    mod = pl.BlockSpec((1, 1, d), lambda i: (i // per, 0, 0))
    return pl.pallas_call(
        _outproj_kernel,
        out_shape=(jax.ShapeDtypeStruct((t, d), F32),
                   jax.ShapeDtypeStruct((t, PACK_W), jnp.int32),
                   jax.ShapeDtypeStruct((t, LANES), F32)),
        grid=(t // TM,),
        in_specs=[tokd(d), tokd(D_MLSTM), tokd(D_FOURIER), full(D_MLSTM, d), full(D_FOURIER, d),
                  mod, full(1, d), full(1, d), mod, mod, full(d, LANES)],
        out_specs=(tokd(d), tokd(PACK_W), tokd(LANES)),
        compiler_params=_cparams(("arbitrary",)),
        name="outproj",
    )(x2d, mread, four, w_m, w_f, gate1, ln_g, ln_b, shift2, scale2, w_router)


def _route_kernel(s_ref, b_ref, idx_ref, gw_ref, rank_ref, cnt_ref, carry_ref):
    @pl.when(pl.program_id(0) == 0)
    def _():
        carry_ref[...] = jnp.zeros_like(carry_ref)

    tm = s_ref.shape[0]
    s = s_ref[...]
    lane_i = lax.broadcasted_iota(jnp.int32, (tm, LANES), 1)
    lane = lane_i.astype(F32)
    neg_inf = jnp.float32(-jnp.inf)
    sb = jnp.where(lane_i < N_EXPERTS, s + b_ref[...], neg_inf)
    picks, sels = [], []
    taken = jnp.zeros((tm, LANES), F32)
    for _ in range(TOP_K):
        best = jnp.max(sb, axis=-1, keepdims=True)
        pick = jnp.min(jnp.where(sb == best, lane, float(LANES)), axis=-1, keepdims=True)
        onehot = lane == pick
        sels.append(jnp.sum(jnp.where(onehot, s, 0.0), axis=-1, keepdims=True))
        picks.append(pick)
        sb = jnp.where(onehot, neg_inf, sb)
        taken = taken + onehot.astype(F32)
    r_i = lax.broadcasted_iota(jnp.int32, (tm, tm), 0)
    c_i = lax.broadcasted_iota(jnp.int32, (tm, tm), 1)
    before = (c_i < r_i).astype(BF16)
    rank_all = carry_ref[0:1, :] + jnp.dot(before, taken.astype(BF16), preferred_element_type=F32)
    carry_ref[...] = carry_ref[...] + jnp.sum(taken, axis=0, keepdims=True)
    cnt_ref[...] = carry_ref[...]
    total = sels[0]
    for v in sels[1:]:
        total = total + v
    k_i = lax.broadcasted_iota(jnp.int32, (tm, TOP_K), 1)
    gw_out = jnp.zeros((tm, TOP_K), F32)
    idx_cols = jnp.zeros((tm, LANES), F32)
    rank_cols = jnp.zeros((tm, LANES), F32)
    for j in range(TOP_K):
        rank_j = jnp.sum(jnp.where(lane == picks[j], rank_all, 0.0), axis=-1, keepdims=True)
        gw_out = jnp.where(k_i == j, ROUTED_SCALE * sels[j] / total, gw_out)
        idx_cols = jnp.where(lane_i == j, picks[j], idx_cols)
        rank_cols = jnp.where(lane_i == j, rank_j, rank_cols)
    gw_ref[...] = gw_out
    idx_ref[...] = idx_cols.T[:TOP_K, :].astype(jnp.int32)
    rank_ref[...] = rank_cols.T[:TOP_K, :].astype(jnp.int32)


def _route(scores, bias_row):
    t = scores.shape[0]
    tok = lambda w: pl.BlockSpec((TM, w), lambda i: (i, 0))
    slot_major = pl.BlockSpec((TOP_K, TM), lambda i: (0, i))
    return pl.pallas_call(
        _route_kernel,
        out_shape=(jax.ShapeDtypeStruct((TOP_K, t), jnp.int32),
                   jax.ShapeDtypeStruct((t, TOP_K), F32),
                   jax.ShapeDtypeStruct((TOP_K, t), jnp.int32),
                   jax.ShapeDtypeStruct((SUBLANES, LANES), F32)),
        grid=(t // TM,),
        in_specs=[tok(LANES), pl.BlockSpec((1, LANES), lambda i: (0, 0))],
        out_specs=(slot_major, tok(TOP_K), slot_major,
                   pl.BlockSpec((SUBLANES, LANES), lambda i: (0, 0))),
        scratch_shapes=[pltpu.VMEM((SUBLANES, LANES), F32)],
        compiler_params=_cparams(("arbitrary",)),
        name="route",
    )(scores, bias_row)


TN_DEST = 2048


def _dest_kernel(ps_ref, idx_ref, rank_ref, dest_ref):
    idx = idx_ref[...]
    start = jnp.zeros(idx.shape, jnp.int32)
    for e in range(N_EXPERTS):
        start = jnp.where(idx == e, ps_ref[e], start)
    dest_ref[...] = start + rank_ref[...]


def _dest(pad_start, idx_t, rank_t):
    t = idx_t.shape[1]
    blk = lambda: pl.BlockSpec((TOP_K, TN_DEST), lambda i, ps: (0, i))
    grid_spec = pltpu.PrefetchScalarGridSpec(
        num_scalar_prefetch=1, grid=(t // TN_DEST,), in_specs=[blk(), blk()], out_specs=blk())
    return pl.pallas_call(
        _dest_kernel,
        out_shape=jax.ShapeDtypeStruct((TOP_K, t), jnp.int32),
        grid_spec=grid_spec,
        compiler_params=_cparams(("arbitrary",)),
        name="dest",
    )(pad_start, idx_t, rank_t)


def _sc_worker_id(info):
    return lax.axis_index("s") * info.num_cores + lax.axis_index("c")


def _sc_dispatch(xm, dest3, n_rows):
    t, w = xm.shape
    info = plsc.get_sparse_core_info()
    n_workers = info.num_cores * info.num_subcores
    per_worker = t // SC_ROWS // n_workers
    mesh = plsc.VectorSubcoreMesh(core_axis_name="c", subcore_axis_name="s")

    @functools.partial(
        pl.kernel, mesh=mesh,
        out_type=jax.ShapeDtypeStruct((n_rows, w), xm.dtype),
        scratch_types=[pltpu.VMEM((TOP_K, SC_ROWS), jnp.int32), pltpu.VMEM((SC_ROWS, w), xm.dtype)],
        name="sc_dispatch",
    )
    def run(xm_hbm, dest_hbm, xs_hbm, idx_v, rows_v):
        first = _sc_worker_id(info) * per_worker

        @pl.loop(0, per_worker)
        def _(ci):
            chunk = first + ci
            pltpu.sync_copy(dest_hbm.at[chunk], idx_v)
            pltpu.sync_copy(xm_hbm.at[pl.ds(chunk * SC_ROWS, SC_ROWS)], rows_v)
            for j in range(TOP_K):
                pltpu.sync_copy(rows_v, xs_hbm.at[idx_v.at[j]])

    return run(xm, dest3)


def _sc_gather(table, idx2):
    n_chunks, rows = idx2.shape
    w = table.shape[1]
    info = plsc.get_sparse_core_info()
    n_workers = info.num_cores * info.num_subcores
    per_worker = n_chunks // n_workers
    assert per_worker % 2 == 0
    mesh = plsc.VectorSubcoreMesh(core_axis_name="c", subcore_axis_name="s")

    @functools.partial(
        pl.kernel, mesh=mesh,
        out_type=jax.ShapeDtypeStruct((n_chunks * rows, w), table.dtype),
        scratch_types=[pltpu.VMEM((2, rows), jnp.int32), pltpu.VMEM((2, rows, w), table.dtype),
                       pltpu.SemaphoreType.DMA, pltpu.SemaphoreType.DMA],
        name="sc_gather",
    )
    def run(table_hbm, idx_hbm, out_hbm, idx_v, rows_v, sem0, sem1):
        first = _sc_worker_id(info) * per_worker
        sems = (sem0, sem1)

        def gather(b):
            return pltpu.make_async_copy(table_hbm.at[idx_v.at[b]], rows_v.at[b], sems[b])

        def start(chunk, b):
            pltpu.sync_copy(idx_hbm.at[chunk], idx_v.at[b])
            gather(b).start()

        def finish(chunk, b):
            gather(b).wait()
            pltpu.sync_copy(rows_v.at[b], out_hbm.at[pl.ds(chunk * rows, rows)])

        start(first, 0)

        @pl.loop(0, per_worker, step=2)
        def _(ci):
            chunk = first + ci
            start(chunk + 1, 1)
            finish(chunk, 0)

            @pl.when(ci + 2 < per_worker)
            def _():
                start(chunk + 2, 0)

            finish(chunk + 1, 1)

    return run(table, idx2)


def _expert_kernel(first_ref, count_ref, size_ref, used_ref,
                   xs_hbm, wgu_ref, wd_ref, y_hbm, wgu_bf, wd_bf, xbuf, ybuf, in_sem, out_sem):
    e = pl.program_id(0)
    n_used = used_ref[0]
    first = first_ref[e]
    n_blocks = count_ref[e]
    size = size_ref[e]

    def rows(g):
        return pl.ds(pl.multiple_of(g * ROW_BLK, ROW_BLK), ROW_BLK)

    def read(g, slot):
        return pltpu.make_async_copy(xs_hbm.at[rows(g), :], xbuf.at[slot], in_sem.at[slot])

    def write(g, slot):
        return pltpu.make_async_copy(ybuf.at[slot], y_hbm.at[rows(g), :], out_sem.at[slot])

    @pl.when(e == 0)
    def _():
        read(0, 0).start()

    @pl.when(n_blocks > 0)
    def _():
        wgu_bf[...] = wgu_ref[0].astype(BF16)
        wd_bf[...] = wd_ref[0].astype(BF16)

    def block(k, carry):
        g = first + k
        slot = g % 2

        @pl.when(g + 1 < n_used)
        def _():
            read(g + 1, 1 - slot).start()

        read(g, slot).wait()

        @pl.when(g >= 2)
        def _():
            write(g - 2, slot).wait()

        n_valid = size - k * ROW_BLK
        for h in range(ROW_BLK // ROW_HALF):
            r0 = h * ROW_HALF
            row = lax.broadcasted_iota(jnp.int32, (ROW_HALF, PACK_W), 0) + r0
            x = _unpack_bf16(jnp.where(row < n_valid, xbuf[slot, r0:r0 + ROW_HALF, :], 0))
            gu = jnp.dot(x, wgu_bf[...], preferred_element_type=F32)
            act = _silu(gu[:, :D_EXPERT]) * gu[:, D_EXPERT:]
            yv = jnp.dot(act.astype(BF16), wd_bf[...], preferred_element_type=F32)
            words = pltpu.pack_elementwise([yv[:, :PACK_W], yv[:, PACK_W:]], packed_dtype=BF16)
            ybuf[slot, r0:r0 + ROW_HALF, :] = lax.bitcast_convert_type(words, jnp.int32)
        write(g, slot).start()
        return carry

    lax.fori_loop(0, n_blocks, block, 0)

    @pl.when(e == pl.num_programs(0) - 1)
    def _():
        @pl.when(n_used >= 2)
        def _():
            write(n_used - 2, n_used % 2).wait()

        write(n_used - 1, (n_used - 1) % 2).wait()


def _experts(first_block, n_blocks, sizes, n_used, xs, w_gu, w_down):
    n_rows, w = xs.shape
    d = D_MODEL
    grid_spec = pltpu.PrefetchScalarGridSpec(
        num_scalar_prefetch=4,
        grid=(N_EXPERTS,),
        in_specs=[pl.BlockSpec(memory_space=pl.ANY),
                  pl.BlockSpec((1, d, 2 * D_EXPERT), lambda e, *_: (e, 0, 0)),
                  pl.BlockSpec((1, D_EXPERT, d), lambda e, *_: (e, 0, 0))],
        out_specs=pl.BlockSpec(memory_space=pl.ANY),
        scratch_shapes=[pltpu.VMEM((d, 2 * D_EXPERT), BF16), pltpu.VMEM((D_EXPERT, d), BF16),
                        pltpu.VMEM((2, ROW_BLK, w), jnp.int32), pltpu.VMEM((2, ROW_BLK, w), jnp.int32),
                        pltpu.SemaphoreType.DMA((2,)), pltpu.SemaphoreType.DMA((2,))],
    )
    return pl.pallas_call(
        _expert_kernel,
        out_shape=jax.ShapeDtypeStruct((n_rows, w), jnp.int32),
        grid_spec=grid_spec,
        compiler_params=_cparams(("arbitrary",)),
        name="experts",
    )(first_block, n_blocks, sizes, n_used, xs, w_gu, w_down)


TM_COMBINE = 256


def _combine_kernel(*refs):
    yg_refs = refs[:TOP_K]
    gw_ref, x1_ref, xm_ref, g2_ref, wsgu_ref, wsd_ref, lg_ref, lb_ref, out_ref = refs[TOP_K:]
    gu = jnp.dot(_unpack_bf16(xm_ref[...]), wsgu_ref[...], preferred_element_type=F32)
    act = _silu(gu[:, :D_EXPERT]) * gu[:, D_EXPERT:]
    acc = jnp.dot(act.astype(BF16), wsd_ref[...], preferred_element_type=F32)
    gw = gw_ref[...]
    for j in range(TOP_K):
        acc = acc + gw[:, j:j + 1] * _unpack_rows(yg_refs[j][...])
    z = DEEPNORM_ALPHA * x1_ref[...] + g2_ref[0] * acc
    out_ref[...] = _layer_norm(z) * lg_ref[...] + lb_ref[...]


def _combine(yg, gw, x1, xm, gate2, w_sgu, w_sd, ln_g, ln_b, n_per_batch):
    t, d = x1.shape
    tm = TM_COMBINE
    per = n_per_batch // tm
    steps = t // tm
    tok = lambda w: pl.BlockSpec((tm, w), lambda i: (i, 0))
    full = lambda r, c: pl.BlockSpec((r, c), lambda i: (0, 0))
    slot = lambda j: pl.BlockSpec((tm, PACK_W), lambda i: (j * steps + i, 0))
    return pl.pallas_call(
        _combine_kernel,
        out_shape=jax.ShapeDtypeStruct((t, d), F32),
        grid=(steps,),
        in_specs=[slot(j) for j in range(TOP_K)]
        + [tok(TOP_K), tok(d), tok(PACK_W), pl.BlockSpec((1, 1, d), lambda i: (i // per, 0, 0)),
           full(d, 2 * D_EXPERT), full(D_EXPERT, d), full(1, d), full(1, d)],
        out_specs=tok(d),
        compiler_params=_cparams(("arbitrary",)),
        name="combine",
    )(*([yg] * TOP_K), gw, x1, xm, gate2, w_sgu, w_sd, ln_g, ln_b)


def _mixer_heads(stream, shift, scale, w_in_k, conv_w9, conv_b, gate_b_row, head_g, c0, m0, grid2d):
    b, n, d = stream.shape
    y = _inproj(stream.reshape(b * n, d), shift, scale, w_in_k, n).reshape(b, n, PROJ_W)
    qk = _conv(y, conv_w9, conv_b, grid2d)
    mread, c_fin, m_fin = _mlstm(qk, y, _gates(y, gate_b_row), c0, m0, head_g)
    return y, mread, c_fin, m_fin


MOE_GROUPS = 1


def _moe_group(xm, x1, scores, bias_row, gate2, w_gu, w_down, w_sgu, w_sd, ln_g, ln_b, n_per_batch):
    t = xm.shape[0]
    n_assign = t * TOP_K
    n_rows = n_assign + N_EXPERTS * ROW_BLK
    idx_t, gw, rank_t, counts = _route(scores, bias_row)
    sizes = counts[0, :N_EXPERTS].astype(jnp.int32)
    padded = (sizes + ROW_BLK - 1) // ROW_BLK * ROW_BLK
    pad_end = jnp.cumsum(padded)
    pad_start = (pad_end - padded).astype(jnp.int32)
    dest_t = _dest(pad_start, idx_t, rank_t)
    dest3 = dest_t.reshape(TOP_K, t // SC_ROWS, SC_ROWS).transpose(1, 0, 2)

    xs = _sc_dispatch(xm, dest3, n_rows)
    yb = _experts(pad_start // ROW_BLK, (padded // ROW_BLK).astype(jnp.int32), sizes,
                  (pad_end[-1:] // ROW_BLK).astype(jnp.int32), xs, w_gu, w_down)
    yg = _sc_gather(yb, dest_t.reshape(n_assign // SC_GATHER_ROWS, SC_GATHER_ROWS))
    return _combine(yg, gw, x1, xm, gate2, w_sgu, w_sd, ln_g, ln_b, n_per_batch)


def kernel(x, c, ctx, c_ctx, w_ada, b_ada, w_in, conv_w, conv_b, gate_b, head_g, w_out, ln1_g, ln1_b,
           w_router, router_bias, w_expert_gu, w_expert_down, w_shared_gu, w_shared_down, ln2_g, ln2_b):
    bsz, n, d = x.shape
    n_ctx = ctx.shape[1]
    depth = w_ada.shape[0]
    t = bsz * n
    tables = _dft_tables()
    for l in range(depth):
        cvec = jnp.zeros((SUBLANES, d), F32).at[:bsz].set(c).at[bsz].set(c_ctx)
        mod = _ada(cvec, w_ada[l], b_ada[l][None, :])
        mods = mod.reshape(SUBLANES, 6, 1, d)
        mod_x = [mods[:bsz, j] for j in range(6)]
        mod_c = [jnp.broadcast_to(mods[bsz:bsz + 1, j], (bsz, 1, d)) for j in range(6)]

        wl = w_in[l]
        g0 = 4 * D_MLSTM
        w_in_k = jnp.concatenate(
            [wl[:, :g0], wl[:, g0 + N_GATE_COLS:], wl[:, g0:g0 + N_GATE_COLS],
             jnp.zeros((d, LANES - N_GATE_COLS), F32)], axis=1).astype(BF16)
        conv_w9 = conv_w[l].reshape(9, 2 * D_MLSTM)
        conv_b_row = conv_b[l][None, :]
        gate_b_row = jnp.zeros((1, LANES), F32).at[0, :N_GATE_COLS].set(gate_b[l].reshape(-1))
        head_g_row = head_g[l][None, :]

        c0 = jnp.zeros((bsz, N_HEADS, 2, 2 * D_HEAD, D_HEAD), F32)
        m0 = jnp.zeros((bsz, N_HEADS, 2, SUBLANES, LANES), F32)
        _, _, c_ctx_fin, m_ctx_fin = _mixer_heads(ctx, mod_c[0], mod_c[1], w_in_k, conv_w9, conv_b_row,
                                                  gate_b_row, head_g_row, c0, m0, grid2d=False)
        y, mread, _, _ = _mixer_heads(x, mod_x[0], mod_x[1], w_in_k, conv_w9, conv_b_row,
                                      gate_b_row, head_g_row, c_ctx_fin, m_ctx_fin, grid2d=True)

        four = _fourier(y.reshape(bsz, DFT_R, DFT_R, PROJ_W), tables).reshape(t, D_FOURIER)

        w_o = w_out[l].astype(BF16)
        w_r = jnp.zeros((d, LANES), F32).at[:, :N_EXPERTS].set(w_router[l]).astype(BF16)
        x2d = x.reshape(t, d)
        x1, xm, scores = _outproj(x2d, mread.reshape(t, D_MLSTM), four, w_o[:D_MLSTM], w_o[D_MLSTM:],
                                  mod_x[2], ln1_g[l][None, :], ln1_b[l][None, :], mod_x[3], mod_x[4],
                                  w_r, n)

        bias_row = jnp.zeros((1, LANES), F32).at[0, :N_EXPERTS].set(router_bias[l])
        w_sgu = w_shared_gu[l].astype(BF16)
        w_sd = w_shared_down[l].astype(BF16)
        tg = t // MOE_GROUPS
        bg = bsz // MOE_GROUPS
        outs = []
        for grp in range(MOE_GROUPS):
            tok = slice(grp * tg, (grp + 1) * tg)
            outs.append(_moe_group(xm[tok], x1[tok], scores[tok], bias_row, mod_x[5][grp * bg:(grp + 1) * bg],
                                   w_expert_gu[l], w_expert_down[l], w_sgu, w_sd,
                                   ln2_g[l][None, :], ln2_b[l][None, :], n))
        x = jnp.concatenate(outs, axis=0).reshape(bsz, n, d)
        assert depth == 1
    return x
```

```python
import functools
import math

import numpy as np
import jax
import jax.numpy as jnp
from jax import lax
from jax.experimental import pallas as pl
from jax.experimental.pallas import tpu as pltpu
from jax.experimental.pallas import tpu_sc as plsc

F32 = jnp.float32
BF16 = jnp.bfloat16
HIGHEST = lax.Precision.HIGHEST

D_MODEL = 1024
GRID_W = 64
N_HEADS = 4
D_HEAD = 128
D_MLSTM = N_HEADS * D_HEAD
D_FOURIER = 512
FOURIER_GROUP = 64
N_GATE_COLS = 16
CHUNK = 128
N_EXPERTS = 64
TOP_K = 8
D_EXPERT = 256
ROUTED_SCALE = 2.5
DEEPNORM_ALPHA = 2.0 ** 0.25
LN_EPS = 1e-5

LANES = 128
SUBLANES = 8
VMEM_LIMIT = 48 * 1024 * 1024

COL_QK = 0
COL_V = 2 * D_MLSTM
COL_O = 3 * D_MLSTM
COL_F = 4 * D_MLSTM
COL_G = COL_F + D_FOURIER
PROJ_W = COL_G + LANES

TM = 512
ROW_BLK = 512
ROW_HALF = 256
EXP_AHEAD = 3
PACK_W = D_MODEL // 2
SC_ROWS = 128
SC_GATHER_ROWS = 64
CONV_PAD = 72
CONV_RB = 256
DFT_R = 64


def _cparams(sem, vmem=VMEM_LIMIT):
    return pltpu.CompilerParams(dimension_semantics=sem, vmem_limit_bytes=vmem)


def _layer_norm(x):
    mu = jnp.mean(x, axis=-1, keepdims=True)
    xc = x - mu
    var = jnp.mean(xc * xc, axis=-1, keepdims=True)
    return xc * lax.rsqrt(var + LN_EPS)


def _silu(x):
    return x * jax.nn.sigmoid(x)


def _pack_pair(lo, hi):
    as_bits = lambda u: lax.bitcast_convert_type(u.astype(BF16).astype(F32), jnp.uint32)
    word = as_bits(hi) | lax.shift_right_logical(as_bits(lo), jnp.uint32(16))
    return lax.bitcast_convert_type(word, jnp.int32)


def _unpack_pair_bf16(w):
    lo = lax.bitcast_convert_type(w.astype(jnp.int16), BF16)
    hi = lax.bitcast_convert_type(lax.shift_right_logical(w, jnp.int32(16)).astype(jnp.int16), BF16)
    return lo, hi


def _pack_rows(v):
    return _pack_pair(v[:, :PACK_W], v[:, PACK_W:])


def _unpack_bf16(w):
    return jnp.concatenate(_unpack_pair_bf16(w), axis=-1)


def _unpack_rows(w):
    u = lax.bitcast_convert_type(w, jnp.uint32)
    lo = lax.bitcast_convert_type(lax.shift_left(u, jnp.uint32(16)), F32)
    hi = lax.bitcast_convert_type(u & jnp.uint32(0xFFFF0000), F32)
    return jnp.concatenate([lo, hi], axis=1)


def _ada_kernel(c_ref, w_ref, b_ref, o_ref):
    s = _silu(c_ref[...])
    o_ref[...] = jnp.dot(s, w_ref[...], precision=HIGHEST, preferred_element_type=F32) + b_ref[...]


def _ada(cvec, w_ada, b_ada):
    rows, d = cvec.shape
    n_out = w_ada.shape[1]
    tn = 1024
    return pl.pallas_call(
        _ada_kernel,
        out_shape=jax.ShapeDtypeStruct((rows, n_out), F32),
        grid=(n_out // tn,),
        in_specs=[pl.BlockSpec((rows, d), lambda j: (0, 0)),
                  pl.BlockSpec((d, tn), lambda j: (0, j)),
                  pl.BlockSpec((1, tn), lambda j: (0, j))],
        out_specs=pl.BlockSpec((rows, tn), lambda j: (0, j)),
        compiler_params=_cparams(("arbitrary",)),
        name="ada",
    )(cvec, w_ada, b_ada)


def _inproj_kernel(x_ref, sh_ref, sc_ref, w_ref, y_ref):
    u = _layer_norm(x_ref[...]) * (1.0 + sc_ref[0]) + sh_ref[0]
    y_ref[...] = jnp.dot(u.astype(BF16), w_ref[...], preferred_element_type=F32)


def _inproj(x2d, shift, scale, w, n_per_batch):
    t, d = x2d.shape
    tm = min(TM, n_per_batch)
    per = n_per_batch // tm
    return pl.pallas_call(
        _inproj_kernel,
        out_shape=jax.ShapeDtypeStruct((t, PROJ_W), F32),
        grid=(t // tm,),
        in_specs=[pl.BlockSpec((tm, d), lambda i: (i, 0)),
                  pl.BlockSpec((1, 1, d), lambda i: (i // per, 0, 0)),
                  pl.BlockSpec((1, 1, d), lambda i: (i // per, 0, 0)),
                  pl.BlockSpec((d, PROJ_W), lambda i: (0, 0))],
        out_specs=pl.BlockSpec((tm, PROJ_W), lambda i: (i, 0)),
        compiler_params=_cparams(("arbitrary",)),
        name="inproj",
    )(x2d, shift, scale, w)


def _conv_kernel(y_ref, w_ref, b_ref, o_ref, pad_ref, *, n, grid2d):
    zeros = jnp.zeros((CONV_PAD, LANES), F32)
    pad_ref[0:CONV_PAD, :] = zeros
    pad_ref[CONV_PAD + n:CONV_PAD + n + CONV_PAD, :] = zeros
    pad_ref[CONV_PAD:CONV_PAD + n, :] = y_ref[0]
    k_scale = jnp.where(pl.program_id(1) >= N_HEADS, D_HEAD ** -0.5, 1.0).astype(F32)
    bias = b_ref[...]
    rb = min(CONV_RB, n)
    col = lax.broadcasted_iota(jnp.int32, (rb, LANES), 0) % GRID_W
    not_first = col >= 1
    not_last = col <= GRID_W - 2
    row_taps = (0, 1, 2) if grid2d else (1,)
    for blk in range(n // rb):
        r0 = CONV_PAD + blk * rb
        acc = jnp.zeros((rb, LANES), F32)
        for dj in range(3):
            part = jnp.zeros((rb, LANES), F32)
            for di in row_taps:
                off = (di - 1) * GRID_W + (dj - 1)
                tap = w_ref[di * 3 + dj:di * 3 + dj + 1, :]
                part = part + tap * pad_ref[r0 + off:r0 + off + rb, :]
            if grid2d and dj == 0:
                part = jnp.where(not_first, part, 0.0)
            if grid2d and dj == 2:
                part = jnp.where(not_last, part, 0.0)
            acc = acc + part
        val = _silu(acc + bias) * k_scale
        o_ref[0, blk * rb:(blk + 1) * rb, :] = val.astype(o_ref.dtype)


def _conv(y3d, conv_w9, conv_b, grid2d):
    b, n, _ = y3d.shape
    n_ct = 2 * D_MLSTM // LANES
    return pl.pallas_call(
        functools.partial(_conv_kernel, n=n, grid2d=grid2d),
        out_shape=jax.ShapeDtypeStruct((b, n, 2 * D_MLSTM), BF16),
        grid=(b, n_ct),
        in_specs=[pl.BlockSpec((1, n, LANES), lambda i, c: (i, 0, c)),
                  pl.BlockSpec((9, LANES), lambda i, c: (0, c)),
                  pl.BlockSpec((1, LANES), lambda i, c: (0, c))],
        out_specs=pl.BlockSpec((1, n, LANES), lambda i, c: (i, 0, c)),
        scratch_shapes=[pltpu.VMEM((n + 2 * CONV_PAD, LANES), F32)],
        compiler_params=_cparams(("arbitrary", "arbitrary")),
        name="conv2d" if grid2d else "conv1d",
    )(y3d, conv_w9, conv_b)


GATE_A, GATE_CM, GATE_B = 0, 1, 2
GATE_GROUP = 8


def _gates_kernel(g_ref, gb_ref, rv_ref, *, nc):
    u_i = lax.broadcasted_iota(jnp.int32, (CHUNK, CHUNK), 0)
    t_i = lax.broadcasted_iota(jnp.int32, (CHUNK, CHUNK), 1)
    prefix = (u_i <= t_i).astype(F32)
    suffix = (u_i >= t_i).astype(F32)
    grp = min(GATE_GROUP, nc)
    rows = grp * SUBLANES
    lane = lax.broadcasted_iota(jnp.int32, (rows, LANES), 1)
    is_fwd = lax.broadcasted_iota(jnp.int32, (rows, LANES), 0) % SUBLANES < N_HEADS
    h = N_HEADS

    def running_max(a):
        fwd, bwd = a, a
        sh = 1
        while sh < CHUNK:
            fwd = jnp.where(lane >= sh, jnp.maximum(fwd, pltpu.roll(fwd, sh, axis=1)), fwd)
            bwd = jnp.where(lane < CHUNK - sh, jnp.maximum(bwd, pltpu.roll(bwd, CHUNK - sh, axis=1)), bwd)
            sh *= 2
        return jnp.where(is_fwd, fwd, bwd)

    def body(i, carry):
        c0 = i * grp
        r0 = pl.multiple_of(c0 * CHUNK, CHUNK)
        pre = g_ref[0, pl.ds(r0, grp * CHUNK), :] + gb_ref[...]
        pre_t = jnp.concatenate(
            [pre[j * CHUNK:(j + 1) * CHUNK].T[:N_GATE_COLS, :] for j in range(grp)], axis=0)
        lf_t = -(jnp.maximum(-pre_t, 0.0) + jnp.log1p(jnp.exp(-jnp.abs(pre_t))))
        cum_f = jnp.dot(lf_t, prefix, precision=HIGHEST, preferred_element_type=F32)
        cum_b = jnp.dot(lf_t, suffix, precision=HIGHEST, preferred_element_type=F32)
        pick = lambda t, j, q: t[j * N_GATE_COLS + q * h:j * N_GATE_COLS + (q + 1) * h]
        b8 = jnp.concatenate([x for j in range(grp) for x in (pick(cum_f, j, 1), pick(cum_b, j, 3))], axis=0)
        li8 = jnp.concatenate([x for j in range(grp) for x in (pick(pre_t, j, 0), pick(pre_t, j, 2))], axis=0)
        a8 = li8 - b8
        as_chunks = lambda t: t.reshape(grp, SUBLANES, LANES)
        rv_ref[0, pl.ds(c0, grp), GATE_A] = as_chunks(a8)
        rv_ref[0, pl.ds(c0, grp), GATE_CM] = as_chunks(running_max(a8))
        rv_ref[0, pl.ds(c0, grp), GATE_B] = as_chunks(b8)
        return carry

    lax.fori_loop(0, nc // grp, body, 0)


def _gates(y3d, gate_b_row):
    b, n, _ = y3d.shape
    nc = n // CHUNK
    return pl.pallas_call(
        functools.partial(_gates_kernel, nc=nc),
        out_shape=jax.ShapeDtypeStruct((b, nc, 3, SUBLANES, LANES), F32),
        grid=(b,),
        in_specs=[pl.BlockSpec((1, n, LANES), lambda i: (i, 0, COL_G // LANES)),
                  pl.BlockSpec((1, LANES), lambda i: (0, 0))],
        out_specs=pl.BlockSpec((1, nc, 3, SUBLANES, LANES), lambda i: (i, 0, 0, 0, 0)),
        compiler_params=_cparams(("arbitrary",)),
        name="gates",
    )(y3d, gate_b_row)


def _mlstm_kernel(q_ref, k_ref, v_ref, o_ref, rv_ref, c0_ref, m0_ref, hg_ref,
                  out_ref, cfin_ref, mfin_ref, hf_ref, hb_ref, cst_ref, mst_ref, *, nc):
    head = pl.program_id(1)
    cst_ref[...] = c0_ref[0, 0]
    mst_ref[...] = m0_ref[0, 0]
    s_i = lax.broadcasted_iota(jnp.int32, (CHUNK, CHUNK), 0)
    t_i = lax.broadcasted_iota(jnp.int32, (CHUNK, CHUNK), 1)
    ones_row = (s_i == 0).astype(F32)

    def one_chunk(c, d, mask_t, last, h_ref):
        r0 = pl.multiple_of(c * CHUNK, CHUNK)
        g = d * N_HEADS + head
        gate_row = lambda tbl: rv_ref[0, pl.ds(c, 1), tbl, pl.ds(g, 1), :].reshape(1, LANES)
        a_row, cm_row, b_row = gate_row(GATE_A), gate_row(GATE_CM), gate_row(GATE_B)
        q_t = q_ref[0, pl.ds(r0, CHUNK), :].T
        k = k_ref[0, pl.ds(r0, CHUNK), :]
        v_t_aug = jnp.concatenate([v_ref[0, pl.ds(r0, CHUNK), :].T, ones_row], axis=0)
        m_prev = mst_ref[d, 0:1, :]
        m_row = jnp.maximum(m_prev, cm_row)
        a_bc = jnp.broadcast_to(a_row, (CHUNK, CHUNK)).T
        d_t = jnp.where(mask_t, jnp.exp(a_bc - m_row), 0.0)
        s_t = jnp.dot(k, q_t, preferred_element_type=F32) * d_t
        cs = cst_ref[d]
        rhs = jnp.concatenate(
            [s_t.astype(BF16), (q_t.astype(F32) * jnp.exp(m_prev - m_row)).astype(BF16)], axis=0)
        lhs = jnp.concatenate([v_t_aug.astype(BF16), cs.astype(BF16)], axis=1)
        h_t = jnp.dot(lhs, rhs, preferred_element_type=F32)
        den = h_t[D_HEAD:D_HEAD + 1, :]
        floor = jnp.exp(-b_row - m_row)
        h_ref[pl.ds(c, 1)] = (h_t[:D_HEAD] / jnp.maximum(jnp.abs(den), floor))[None]
        m_last = m_row[:, last:last + 1]
        wv_t = (v_t_aug * jnp.exp(a_row - m_last)).astype(BF16)
        cst_ref[d] = jnp.exp(m_prev - m_last) * cs + jnp.dot(wv_t, k, preferred_element_type=F32)
        mst_ref[d] = jnp.broadcast_to(b_row[:, last:last + 1] + m_last, (SUBLANES, LANES))

    def body(c, carry):
        one_chunk(c, 0, s_i <= t_i, CHUNK - 1, hf_ref)
        one_chunk(nc - 1 - c, 1, s_i >= t_i, 0, hb_ref)
        return carry

    lax.fori_loop(0, nc, body, 0, unroll=min(8, nc))
    cfin_ref[0, 0] = cst_ref[...]
    mfin_ref[0, 0] = mst_ref[...]

    def readout(c, carry):
        r0 = pl.multiple_of(c * CHUNK, CHUNK)
        h_t = hf_ref[pl.ds(c, 1)][0] + hb_ref[pl.ds(c, 1)][0]
        hc = h_t - jnp.mean(h_t, axis=0, keepdims=True)
        hn = (hc * lax.rsqrt(jnp.mean(hc * hc, axis=0, keepdims=True) + LN_EPS)).T
        gate = jax.nn.sigmoid(o_ref[0, pl.ds(r0, CHUNK), :])
        out_ref[0, pl.ds(r0, CHUNK), :] = (hn * hg_ref[...] * gate).astype(out_ref.dtype)
        return carry

    lax.fori_loop(0, nc, readout, 0, unroll=min(4, nc))


def _mlstm(qk, y3d, rowv, c0, m0, head_g):
    b, n, _ = qk.shape
    nc = n // CHUNK
    v_blk = COL_V // LANES
    o_blk = COL_O // LANES
    tok = lambda off: pl.BlockSpec((1, n, LANES), lambda i, h: (i, 0, off + h))
    state_c = pl.BlockSpec((1, 1, 2, 2 * D_HEAD, D_HEAD), lambda i, h: (i, h, 0, 0, 0))
    state_m = pl.BlockSpec((1, 1, 2, SUBLANES, LANES), lambda i, h: (i, h, 0, 0, 0))
    return pl.pallas_call(
        functools.partial(_mlstm_kernel, nc=nc),
        out_shape=(jax.ShapeDtypeStruct((b, n, D_MLSTM), BF16),
                   jax.ShapeDtypeStruct((b, N_HEADS, 2, 2 * D_HEAD, D_HEAD), F32),
                   jax.ShapeDtypeStruct((b, N_HEADS, 2, SUBLANES, LANES), F32)),
        grid=(b, N_HEADS),
        in_specs=[tok(0), tok(N_HEADS), tok(v_blk), tok(o_blk),
                  pl.BlockSpec((1, nc, 3, SUBLANES, LANES), lambda i, h: (i, 0, 0, 0, 0)),
                  state_c, state_m,
                  pl.BlockSpec((1, LANES), lambda i, h: (0, h))],
        out_specs=(pl.BlockSpec((1, n, LANES), lambda i, h: (i, 0, h)), state_c, state_m),
        scratch_shapes=[pltpu.VMEM((nc, D_HEAD, CHUNK), F32), pltpu.VMEM((nc, D_HEAD, CHUNK), F32),
                        pltpu.VMEM((2, 2 * D_HEAD, D_HEAD), F32),
                        pltpu.VMEM((2, SUBLANES, LANES), F32)],
        compiler_params=_cparams(("arbitrary", "arbitrary")),
        name="mlstm",
    )(qk, qk, y3d, y3d, rowv, c0, m0, head_g)


def _dft_tables():
    r = DFT_R
    idx = np.arange(r)
    cg = np.cos(2 * np.pi * np.outer(idx, idx) / FOURIER_GROUP)
    sg = np.sin(2 * np.pi * np.outer(idx, idx) / FOURIER_GROUP)
    n_grp = D_FOURIER // FOURIER_GROUP
    bdc = np.kron(np.eye(n_grp), cg) / 8.0
    bds = np.kron(np.eye(n_grp), sg) / 8.0
    k1 = idx[:, None]
    a = idx[None, :]
    m1 = np.zeros((r, 2 * r, 2 * r))
    for b2 in range(r):
        th = 2 * np.pi * k1 * (r * a + b2) / (r * r)
        ec, es = np.cos(th), np.sin(th)
        m1[b2] = np.block([[ec, -es], [es, ec]]) / 8.0
    th2 = 2 * np.pi * np.outer(idx, idx) / r
    m2 = np.concatenate([np.cos(th2), -np.sin(th2)], axis=1) / 8.0
    as_bf16 = lambda t: jnp.asarray(t, dtype=F32).astype(BF16)
    return as_bf16(bdc), as_bf16(bds), as_bf16(m1), as_bf16(m2)


FOUR_BB = 8


N_SLABS = D_FOURIER // LANES


def _to_slabs(slab_ref, val):
    for s in range(N_SLABS):
        slab_ref[s] = val[:, s * LANES:(s + 1) * LANES]


def _strided_rows(slab_ref, start):
    return jnp.concatenate(
        [slab_ref[s, pl.ds(start, DFT_R, stride=FOUR_BB), :] for s in range(N_SLABS)], axis=1)


def _four1_kernel(f_ref, bdc_ref, bds_ref, m1_ref, z_ref, ps_ref, qs_ref):
    r = DFT_R
    fb = f_ref[0].reshape(r * FOUR_BB, D_FOURIER).astype(BF16)
    _to_slabs(ps_ref, jnp.dot(fb, bdc_ref[...], preferred_element_type=F32))
    _to_slabs(qs_ref, jnp.dot(fb, bds_ref[...], preferred_element_type=F32))
    for j in range(FOUR_BB):
        x2 = jnp.concatenate([_strided_rows(ps_ref, j), _strided_rows(qs_ref, j)], axis=0).astype(BF16)
        zz = jnp.dot(m1_ref[j], x2, preferred_element_type=F32)
        z_ref[0, j] = _pack_pair(zz[:r], zz[r:])


def _four2_kernel(z_ref, m2_ref, o_ref, zs_ref, os_ref):
    r = DFT_R
    _to_slabs(zs_ref, z_ref[0].reshape(r * FOUR_BB, D_FOURIER))
    for j in range(FOUR_BB):
        zc, zs = _unpack_pair_bf16(_strided_rows(zs_ref, j))
        o = jnp.dot(m2_ref[...], jnp.concatenate([zc, zs], axis=0), preferred_element_type=F32)
        for s in range(N_SLABS):
            os_ref[s, pl.ds(j, r, stride=FOUR_BB), :] = o[:, s * LANES:(s + 1) * LANES]
    for s in range(N_SLABS):
        o_ref[0, :, :, s * LANES:(s + 1) * LANES] = os_ref[s].reshape(r, FOUR_BB, LANES)


def _fourier(y4, tables):
    bdc, bds, m1, m2 = tables
    b = y4.shape[0]
    r = DFT_R
    nb = r // FOUR_BB
    slab = lambda dt: pltpu.VMEM((N_SLABS, r * FOUR_BB, LANES), dt)
    z = pl.pallas_call(
        _four1_kernel,
        out_shape=jax.ShapeDtypeStruct((b, r, r, D_FOURIER), jnp.int32),
        grid=(b, nb),
        in_specs=[pl.BlockSpec((1, r, FOUR_BB, D_FOURIER), lambda i, j: (i, 0, j, COL_F // D_FOURIER)),
                  pl.BlockSpec((D_FOURIER, D_FOURIER), lambda i, j: (0, 0)),
                  pl.BlockSpec((D_FOURIER, D_FOURIER), lambda i, j: (0, 0)),
                  pl.BlockSpec((FOUR_BB, 2 * r, 2 * r), lambda i, j: (j, 0, 0))],
        out_specs=pl.BlockSpec((1, FOUR_BB, r, D_FOURIER), lambda i, j: (i, j, 0, 0)),
        scratch_shapes=[slab(F32), slab(F32)],
        compiler_params=_cparams(("arbitrary", "arbitrary")),
        name="four1",
    )(y4, bdc, bds, m1)
    return pl.pallas_call(
        _four2_kernel,
        out_shape=jax.ShapeDtypeStruct((b, r, r, D_FOURIER), F32),
        grid=(b, nb),
        in_specs=[pl.BlockSpec((1, r, FOUR_BB, D_FOURIER), lambda i, j: (i, 0, j, 0)),
                  pl.BlockSpec((r, 2 * r), lambda i, j: (0, 0))],
        out_specs=pl.BlockSpec((1, r, FOUR_BB, D_FOURIER), lambda i, j: (i, 0, j, 0)),
        scratch_shapes=[slab(jnp.int32), slab(F32)],
        compiler_params=_cparams(("arbitrary", "arbitrary")),
        name="four2",
    )(z, m2)


def _outproj_kernel(x_ref, mr_ref, fo_ref, wm_ref, wf_ref, g1_ref, lg_ref, lb_ref,
                    sh_ref, sc_ref, wr_ref, x1_ref, xm_ref, s_ref):
    mix = (jnp.dot(mr_ref[...], wm_ref[...], preferred_element_type=F32)
           + jnp.dot(fo_ref[...].astype(BF16), wf_ref[...], preferred_element_type=F32))
    x1 = _layer_norm(DEEPNORM_ALPHA * x_ref[...] + g1_ref[0] * mix) * lg_ref[...] + lb_ref[...]
    x1_ref[...] = x1
    xm = _layer_norm(x1) * (1.0 + sc_ref[0]) + sh_ref[0]
    xm_ref[...] = _pack_rows(xm)
    logits = jnp.dot(xm.astype(BF16), wr_ref[...], preferred_element_type=F32)
    s_ref[...] = jax.nn.sigmoid(logits)


def _outproj(x2d, mread, four, w_m, w_f, gate1, ln_g, ln_b, shift2, scale2, w_router, n_per_batch):
    t, d = x2d.shape
    per = n_per_batch // TM
    tokd = lambda w: pl.BlockSpec((TM, w), lambda i: (i, 0))
    full = lambda r, c: pl.BlockSpec((r, c), lambda i: (0, 0))
    mod = pl.BlockSpec((1, 1, d), lambda i: (i // per, 0, 0))
    return pl.pallas_call(
        _outproj_kernel,
        out_shape=(jax.ShapeDtypeStruct((t, d), F32),
                   jax.ShapeDtypeStruct((t, PACK_W), jnp.int32),
                   jax.ShapeDtypeStruct((t, LANES), F32)),
        grid=(t // TM,),
        in_specs=[tokd(d), tokd(D_MLSTM), tokd(D_FOURIER), full(D_MLSTM, d), full(D_FOURIER, d),
                  mod, full(1, d), full(1, d), mod, mod, full(d, LANES)],
        out_specs=(tokd(d), tokd(PACK_W), tokd(LANES)),
        compiler_params=_cparams(("arbitrary",)),
        name="outproj",
    )(x2d, mread, four, w_m, w_f, gate1, ln_g, ln_b, shift2, scale2, w_router)


def _route_kernel(s_ref, b_ref, idx_ref, gw_ref, rank_ref, cnt_ref, carry_ref):
    @pl.when(pl.program_id(0) == 0)
    def _():
        carry_ref[...] = jnp.zeros_like(carry_ref)

    tm = s_ref.shape[0]
    reps = tm // LANES
    s_t = s_ref[...].T[:N_EXPERTS, :]
    e_i = lax.broadcasted_iota(jnp.int32, (N_EXPERTS, tm), 0).astype(F32)
    neg_inf = jnp.float32(-jnp.inf)
    sb = s_t + jnp.tile(b_ref[...], (1, reps))
    picks, sels = [], []
    taken = jnp.zeros((N_EXPERTS, tm), F32)
    for _ in range(TOP_K):
        best = jnp.max(sb, axis=0, keepdims=True)
        pick = jnp.min(jnp.where(sb == best, e_i, float(N_EXPERTS)), axis=0, keepdims=True)
        onehot = e_i == pick
        sels.append(jnp.sum(jnp.where(onehot, s_t, 0.0), axis=0, keepdims=True))
        picks.append(pick)
        sb = jnp.where(onehot, neg_inf, sb)
        taken = taken + onehot.astype(F32)
    u_i = lax.broadcasted_iota(jnp.int32, (tm, tm), 0)
    t_i = lax.broadcasted_iota(jnp.int32, (tm, tm), 1)
    before = (u_i < t_i).astype(BF16)
    taken_bf = taken.astype(BF16)
    rank_all = jnp.tile(carry_ref[...], (1, reps)) + jnp.dot(taken_bf, before, preferred_element_type=F32)
    carry_ref[...] = carry_ref[...] + jnp.dot(taken_bf, jnp.ones((tm, LANES), BF16),
                                              preferred_element_type=F32)
    cnt_ref[...] = carry_ref[...]
    total = sels[0]
    for v in sels[1:]:
        total = total + v
    ranks = [jnp.sum(jnp.where(e_i == picks[j], rank_all, 0.0), axis=0, keepdims=True) for j in range(TOP_K)]
    idx_ref[...] = jnp.concatenate(picks, axis=0).astype(jnp.int32)
    rank_ref[...] = jnp.concatenate(ranks, axis=0).astype(jnp.int32)
    gw_t = jnp.concatenate([ROUTED_SCALE * v / total for v in sels]
                           + [jnp.zeros((LANES - TOP_K, tm), F32)], axis=0)
    gw_ref[...] = gw_t.T[:, :TOP_K]


def _route(scores, bias_rep):
    t = scores.shape[0]
    tok = lambda w: pl.BlockSpec((TM, w), lambda i: (i, 0))
    slot_major = pl.BlockSpec((TOP_K, TM), lambda i: (0, i))
    per_expert = pl.BlockSpec((N_EXPERTS, LANES), lambda i: (0, 0))
    return pl.pallas_call(
        _route_kernel,
        out_shape=(jax.ShapeDtypeStruct((TOP_K, t), jnp.int32),
                   jax.ShapeDtypeStruct((t, TOP_K), F32),
                   jax.ShapeDtypeStruct((TOP_K, t), jnp.int32),
                   jax.ShapeDtypeStruct((N_EXPERTS, LANES), F32)),
        grid=(t // TM,),
        in_specs=[tok(LANES), per_expert],
        out_specs=(slot_major, tok(TOP_K), slot_major, per_expert),
        scratch_shapes=[pltpu.VMEM((N_EXPERTS, LANES), F32)],
        compiler_params=_cparams(("arbitrary",)),
        name="route",
    )(scores, bias_rep)


TN_DEST = 2048


def _dest_kernel(ps_ref, idx_ref, rank_ref, dest_ref):
    idx = idx_ref[...]
    start = jnp.zeros(idx.shape, jnp.int32)
    for e in range(N_EXPERTS):
        start = jnp.where(idx == e, ps_ref[e], start)
    dest_ref[...] = start + rank_ref[...]


def _dest(pad_start, idx_t, rank_t):
    t = idx_t.shape[1]
    blk = lambda: pl.BlockSpec((TOP_K, TN_DEST), lambda i, ps: (0, i))
    grid_spec = pltpu.PrefetchScalarGridSpec(
        num_scalar_prefetch=1, grid=(t // TN_DEST,), in_specs=[blk(), blk()], out_specs=blk())
    return pl.pallas_call(
        _dest_kernel,
        out_shape=jax.ShapeDtypeStruct((TOP_K, t), jnp.int32),
        grid_spec=grid_spec,
        compiler_params=_cparams(("arbitrary",)),
        name="dest",
    )(pad_start, idx_t, rank_t)


def _sc_worker_id(info):
    return lax.axis_index("s") * info.num_cores + lax.axis_index("c")


def _sc_dispatch(xm, dest3, n_rows):
    t, w = xm.shape
    info = plsc.get_sparse_core_info()
    n_workers = info.num_cores * info.num_subcores
    per_worker = t // SC_ROWS // n_workers
    mesh = plsc.VectorSubcoreMesh(core_axis_name="c", subcore_axis_name="s")

    @functools.partial(
        pl.kernel, mesh=mesh,
        out_type=jax.ShapeDtypeStruct((n_rows, w), xm.dtype),
        scratch_types=[pltpu.VMEM((TOP_K, SC_ROWS), jnp.int32), pltpu.VMEM((SC_ROWS, w), xm.dtype)],
        name="sc_dispatch",
    )
    def run(xm_hbm, dest_hbm, xs_hbm, idx_v, rows_v):
        first = _sc_worker_id(info) * per_worker

        @pl.loop(0, per_worker)
        def _(ci):
            chunk = first + ci
            pltpu.sync_copy(dest_hbm.at[chunk], idx_v)
            pltpu.sync_copy(xm_hbm.at[pl.ds(chunk * SC_ROWS, SC_ROWS)], rows_v)
            for j in range(TOP_K):
                pltpu.sync_copy(rows_v, xs_hbm.at[idx_v.at[j]])

    return run(xm, dest3)


def _sc_gather(table, idx2):
    n_chunks, rows = idx2.shape
    w = table.shape[1]
    info = plsc.get_sparse_core_info()
    n_workers = info.num_cores * info.num_subcores
    per_worker = n_chunks // n_workers
    assert per_worker % 2 == 0
    mesh = plsc.VectorSubcoreMesh(core_axis_name="c", subcore_axis_name="s")

    @functools.partial(
        pl.kernel, mesh=mesh,
        out_type=jax.ShapeDtypeStruct((n_chunks * rows, w), table.dtype),
        scratch_types=[pltpu.VMEM((2, rows), jnp.int32), pltpu.VMEM((2, rows, w), table.dtype),
                       pltpu.SemaphoreType.DMA, pltpu.SemaphoreType.DMA],
        name="sc_gather",
    )
    def run(table_hbm, idx_hbm, out_hbm, idx_v, rows_v, sem0, sem1):
        first = _sc_worker_id(info) * per_worker
        sems = (sem0, sem1)

        def gather(b):
            return pltpu.make_async_copy(table_hbm.at[idx_v.at[b]], rows_v.at[b], sems[b])

        def start(chunk, b):
            pltpu.sync_copy(idx_hbm.at[chunk], idx_v.at[b])
            gather(b).start()

        def finish(chunk, b):
            gather(b).wait()
            pltpu.sync_copy(rows_v.at[b], out_hbm.at[pl.ds(chunk * rows, rows)])

        start(first, 0)

        @pl.loop(0, per_worker, step=2)
        def _(ci):
            chunk = first + ci
            start(chunk + 1, 1)
            finish(chunk, 0)

            @pl.when(ci + 2 < per_worker)
            def _():
                start(chunk + 2, 0)

            finish(chunk + 1, 1)

    return run(table, idx2)


def _expert_kernel(first_ref, count_ref, size_ref, used_ref,
                   xs_hbm, wgu_ref, wd_ref, y_hbm, wgu_bf, wd_bf, xbuf, ybuf, in_sem, out_sem):
    e = pl.program_id(0)
    n_used = used_ref[0]
    first = first_ref[e]
    n_blocks = count_ref[e]
    size = size_ref[e]
    n_in = EXP_AHEAD + 1

    def rows(g):
        return pl.ds(pl.multiple_of(g * ROW_BLK, ROW_BLK), ROW_BLK)

    def read(g):
        slot = g % n_in
        return pltpu.make_async_copy(xs_hbm.at[rows(g), :], xbuf.at[slot], in_sem.at[slot])

    def write(g):
        slot = g % 2
        return pltpu.make_async_copy(ybuf.at[slot], y_hbm.at[rows(g), :], out_sem.at[slot])

    @pl.when(e == 0)
    def _():
        for g0 in range(EXP_AHEAD):
            @pl.when(g0 < n_used)
            def _():
                read(g0).start()

    @pl.when(n_blocks > 0)
    def _():
        wgu_bf[...] = wgu_ref[0].astype(BF16)
        wd_bf[...] = wd_ref[0].astype(BF16)

    def block(k, carry):
        g = first + k
        slot = g % 2
        in_slot = g % n_in

        @pl.when(g + EXP_AHEAD < n_used)
        def _():
            read(g + EXP_AHEAD).start()

        read(g).wait()

        @pl.when(g >= 2)
        def _():
            write(g - 2).wait()

        n_valid = size - k * ROW_BLK
        for h in range(ROW_BLK // ROW_HALF):
            r0 = h * ROW_HALF
            row = lax.broadcasted_iota(jnp.int32, (ROW_HALF, PACK_W), 0) + r0
            x = _unpack_bf16(jnp.where(row < n_valid, xbuf[in_slot, r0:r0 + ROW_HALF, :], 0))
            gu = jnp.dot(x, wgu_bf[...], preferred_element_type=F32)
            act = _silu(gu[:, :D_EXPERT]) * gu[:, D_EXPERT:]
            yv = jnp.dot(act.astype(BF16), wd_bf[...], preferred_element_type=F32)
            words = pltpu.pack_elementwise([yv[:, :PACK_W], yv[:, PACK_W:]], packed_dtype=BF16)
            ybuf[slot, r0:r0 + ROW_HALF, :] = lax.bitcast_convert_type(words, jnp.int32)
        write(g).start()
        return carry

    lax.fori_loop(0, n_blocks, block, 0)

    @pl.when(e == pl.num_programs(0) - 1)
    def _():
        @pl.when(n_used >= 2)
        def _():
            write(n_used - 2).wait()

        write(n_used - 1).wait()


def _experts(first_block, n_blocks, sizes, n_used, xs, w_gu, w_down):
    n_rows, w = xs.shape
    d = D_MODEL
    grid_spec = pltpu.PrefetchScalarGridSpec(
        num_scalar_prefetch=4,
        grid=(N_EXPERTS,),
        in_specs=[pl.BlockSpec(memory_space=pl.ANY),
                  pl.BlockSpec((1, d, 2 * D_EXPERT), lambda e, *_: (e, 0, 0)),
                  pl.BlockSpec((1, D_EXPERT, d), lambda e, *_: (e, 0, 0))],
        out_specs=pl.BlockSpec(memory_space=pl.ANY),
        scratch_shapes=[pltpu.VMEM((d, 2 * D_EXPERT), BF16), pltpu.VMEM((D_EXPERT, d), BF16),
                        pltpu.VMEM((EXP_AHEAD + 1, ROW_BLK, w), jnp.int32),
                        pltpu.VMEM((2, ROW_BLK, w), jnp.int32),
                        pltpu.SemaphoreType.DMA((EXP_AHEAD + 1,)), pltpu.SemaphoreType.DMA((2,))],
    )
    return pl.pallas_call(
        _expert_kernel,
        out_shape=jax.ShapeDtypeStruct((n_rows, w), jnp.int32),
        grid_spec=grid_spec,
        compiler_params=_cparams(("arbitrary",)),
        name="experts",
    )(first_block, n_blocks, sizes, n_used, xs, w_gu, w_down)


TM_COMBINE = 256


def _combine_kernel(*refs):
    yg_refs = refs[:TOP_K]
    gw_ref, x1_ref, xm_ref, g2_ref, wsgu_ref, wsd_ref, lg_ref, lb_ref, out_ref = refs[TOP_K:]
    gu = jnp.dot(_unpack_bf16(xm_ref[...]), wsgu_ref[...], preferred_element_type=F32)
    act = _silu(gu[:, :D_EXPERT]) * gu[:, D_EXPERT:]
    acc = jnp.dot(act.astype(BF16), wsd_ref[...], preferred_element_type=F32)
    gw = gw_ref[...]
    for j in range(TOP_K):
        acc = acc + gw[:, j:j + 1] * _unpack_rows(yg_refs[j][...])
    z = DEEPNORM_ALPHA * x1_ref[...] + g2_ref[0] * acc
    out_ref[...] = _layer_norm(z) * lg_ref[...] + lb_ref[...]


def _combine(yg, gw, x1, xm, gate2, w_sgu, w_sd, ln_g, ln_b, n_per_batch):
    t, d = x1.shape
    tm = TM_COMBINE
    per = n_per_batch // tm
    steps = t // tm
    tok = lambda w: pl.BlockSpec((tm, w), lambda i: (i, 0))
    full = lambda r, c: pl.BlockSpec((r, c), lambda i: (0, 0))
    slot = lambda j: pl.BlockSpec((tm, PACK_W), lambda i: (j * steps + i, 0))
    return pl.pallas_call(
        _combine_kernel,
        out_shape=jax.ShapeDtypeStruct((t, d), F32),
        grid=(steps,),
        in_specs=[slot(j) for j in range(TOP_K)]
        + [tok(TOP_K), tok(d), tok(PACK_W), pl.BlockSpec((1, 1, d), lambda i: (i // per, 0, 0)),
           full(d, 2 * D_EXPERT), full(D_EXPERT, d), full(1, d), full(1, d)],
        out_specs=tok(d),
        compiler_params=_cparams(("arbitrary",)),
        name="combine",
    )(*([yg] * TOP_K), gw, x1, xm, gate2, w_sgu, w_sd, ln_g, ln_b)


def _mixer_heads(stream, shift, scale, w_in_k, conv_w9, conv_b, gate_b_row, head_g, c0, m0, grid2d):
    b, n, d = stream.shape
    y = _inproj(stream.reshape(b * n, d), shift, scale, w_in_k, n).reshape(b, n, PROJ_W)
    qk = _conv(y, conv_w9, conv_b, grid2d)
    mread, c_fin, m_fin = _mlstm(qk, y, _gates(y, gate_b_row), c0, m0, head_g)
    return y, mread, c_fin, m_fin


MOE_GROUPS = 1


def _moe_group(xm, x1, scores, bias_row, gate2, w_gu, w_down, w_sgu, w_sd, ln_g, ln_b, n_per_batch):
    t = xm.shape[0]
    n_assign = t * TOP_K
    n_rows = n_assign + N_EXPERTS * ROW_BLK
    idx_t, gw, rank_t, counts = _route(scores, bias_row)
    sizes = counts[:, 0].astype(jnp.int32)
    padded = (sizes + ROW_BLK - 1) // ROW_BLK * ROW_BLK
    pad_end = jnp.cumsum(padded)
    pad_start = (pad_end - padded).astype(jnp.int32)
    dest_t = _dest(pad_start, idx_t, rank_t)
    dest3 = dest_t.reshape(TOP_K, t // SC_ROWS, SC_ROWS).transpose(1, 0, 2)

    xs = _sc_dispatch(xm, dest3, n_rows)
    yb = _experts(pad_start // ROW_BLK, (padded // ROW_BLK).astype(jnp.int32), sizes,
                  (pad_end[-1:] // ROW_BLK).astype(jnp.int32), xs, w_gu, w_down)
    yg = _sc_gather(yb, dest_t.reshape(n_assign // SC_GATHER_ROWS, SC_GATHER_ROWS))
    return _combine(yg, gw, x1, xm, gate2, w_sgu, w_sd, ln_g, ln_b, n_per_batch)


def kernel(x, c, ctx, c_ctx, w_ada, b_ada, w_in, conv_w, conv_b, gate_b, head_g, w_out, ln1_g, ln1_b,
           w_router, router_bias, w_expert_gu, w_expert_down, w_shared_gu, w_shared_down, ln2_g, ln2_b):
    bsz, n, d = x.shape
    n_ctx = ctx.shape[1]
    depth = w_ada.shape[0]
    t = bsz * n
    tables = _dft_tables()
    for l in range(depth):
        cvec = jnp.zeros((SUBLANES, d), F32).at[:bsz].set(c).at[bsz].set(c_ctx)
        mod = _ada(cvec, w_ada[l], b_ada[l][None, :])
        mods = mod.reshape(SUBLANES, 6, 1, d)
        mod_x = [mods[:bsz, j] for j in range(6)]
        mod_c = [jnp.broadcast_to(mods[bsz:bsz + 1, j], (bsz, 1, d)) for j in range(6)]

        wl = w_in[l]
        g0 = 4 * D_MLSTM
        w_in_k = jnp.concatenate(
            [wl[:, :g0], wl[:, g0 + N_GATE_COLS:], wl[:, g0:g0 + N_GATE_COLS],
             jnp.zeros((d, LANES - N_GATE_COLS), F32)], axis=1).astype(BF16)
        conv_w9 = conv_w[l].reshape(9, 2 * D_MLSTM)
        conv_b_row = conv_b[l][None, :]
        gate_b_row = jnp.zeros((1, LANES), F32).at[0, :N_GATE_COLS].set(gate_b[l].reshape(-1))
        head_g_row = head_g[l][None, :]

        c0 = jnp.zeros((bsz, N_HEADS, 2, 2 * D_HEAD, D_HEAD), F32)
        m0 = jnp.zeros((bsz, N_HEADS, 2, SUBLANES, LANES), F32)
        _, _, c_ctx_fin, m_ctx_fin = _mixer_heads(ctx, mod_c[0], mod_c[1], w_in_k, conv_w9, conv_b_row,
                                                  gate_b_row, head_g_row, c0, m0, grid2d=False)
        y, mread, _, _ = _mixer_heads(x, mod_x[0], mod_x[1], w_in_k, conv_w9, conv_b_row,
                                      gate_b_row, head_g_row, c_ctx_fin, m_ctx_fin, grid2d=True)

        four = _fourier(y.reshape(bsz, DFT_R, DFT_R, PROJ_W), tables).reshape(t, D_FOURIER)

        w_o = w_out[l].astype(BF16)
        w_r = jnp.zeros((d, LANES), F32).at[:, :N_EXPERTS].set(w_router[l]).astype(BF16)
        x2d = x.reshape(t, d)
        x1, xm, scores = _outproj(x2d, mread.reshape(t, D_MLSTM), four, w_o[:D_MLSTM], w_o[D_MLSTM:],
                                  mod_x[2], ln1_g[l][None, :], ln1_b[l][None, :], mod_x[3], mod_x[4],
                                  w_r, n)

        bias_row = jnp.broadcast_to(router_bias[l][:, None], (N_EXPERTS, LANES))
        w_sgu = w_shared_gu[l].astype(BF16)
        w_sd = w_shared_down[l].astype(BF16)
        tg = t // MOE_GROUPS
        bg = bsz // MOE_GROUPS
        outs = []
        for grp in range(MOE_GROUPS):
            tok = slice(grp * tg, (grp + 1) * tg)
            outs.append(_moe_group(xm[tok], x1[tok], scores[tok], bias_row, mod_x[5][grp * bg:(grp + 1) * bg],
                                   w_expert_gu[l], w_expert_down[l], w_sgu, w_sd,
                                   ln2_g[l][None, :], ln2_b[l][None, :], n))
        x = jnp.concatenate(outs, axis=0).reshape(bsz, n, d)
        assert depth == 1
    return x
```

```python
import functools
import math

import numpy as np
import jax
import jax.numpy as jnp
from jax import lax
from jax.experimental import pallas as pl
from jax.experimental.pallas import tpu as pltpu
from jax.experimental.pallas import tpu_sc as plsc

F32 = jnp.float32
BF16 = jnp.bfloat16
HIGHEST = lax.Precision.HIGHEST

D_MODEL = 1024
GRID_W = 64
N_HEADS = 4
D_HEAD = 128
D_MLSTM = N_HEADS * D_HEAD
D_FOURIER = 512
FOURIER_GROUP = 64
N_GATE_COLS = 16
CHUNK = 128
N_EXPERTS = 64
TOP_K = 8
D_EXPERT = 256
ROUTED_SCALE = 2.5
DEEPNORM_ALPHA = 2.0 ** 0.25
LN_EPS = 1e-5

LANES = 128
SUBLANES = 8
VMEM_LIMIT = 48 * 1024 * 1024

COL_QK = 0
COL_V = 2 * D_MLSTM
COL_O = 3 * D_MLSTM
COL_F = 4 * D_MLSTM
COL_G = COL_F + D_FOURIER
PROJ_W = COL_G + LANES

TM = 512
ROW_BLK = 512
ROW_HALF = 256
EXP_AHEAD = 3
EXP_IN_SLOTS = 8
EXP_OUT_SLOTS = 4
PACK_W = D_MODEL // 2
SC_ROWS = 128
SC_GATHER_ROWS = 64
CONV_PAD = 72
CONV_RB = 256
DFT_R = 64


def _cparams(sem, vmem=VMEM_LIMIT):
    return pltpu.CompilerParams(dimension_semantics=sem, vmem_limit_bytes=vmem)


def _layer_norm(x):
    mu = jnp.mean(x, axis=-1, keepdims=True)
    xc = x - mu
    var = jnp.mean(xc * xc, axis=-1, keepdims=True)
    return xc * lax.rsqrt(var + LN_EPS)


def _silu(x):
    return x * jax.nn.sigmoid(x)


def _pack_pair(lo, hi):
    as_bits = lambda u: lax.bitcast_convert_type(u.astype(BF16).astype(F32), jnp.uint32)
    word = as_bits(hi) | lax.shift_right_logical(as_bits(lo), jnp.uint32(16))
    return lax.bitcast_convert_type(word, jnp.int32)


def _unpack_pair_bf16(w):
    lo = lax.bitcast_convert_type(w.astype(jnp.int16), BF16)
    hi = lax.bitcast_convert_type(lax.shift_right_logical(w, jnp.int32(16)).astype(jnp.int16), BF16)
    return lo, hi


def _pack_rows(v):
    return _pack_pair(v[:, :PACK_W], v[:, PACK_W:])


def _unpack_bf16(w):
    return jnp.concatenate(_unpack_pair_bf16(w), axis=-1)


def _unpack_rows(w):
    u = lax.bitcast_convert_type(w, jnp.uint32)
    lo = lax.bitcast_convert_type(lax.shift_left(u, jnp.uint32(16)), F32)
    hi = lax.bitcast_convert_type(u & jnp.uint32(0xFFFF0000), F32)
    return jnp.concatenate([lo, hi], axis=1)


def _ada_kernel(c_ref, w_ref, b_ref, o_ref):
    s = _silu(c_ref[...])
    o_ref[...] = jnp.dot(s, w_ref[...], precision=HIGHEST, preferred_element_type=F32) + b_ref[...]


def _ada(cvec, w_ada, b_ada):
    rows, d = cvec.shape
    n_out = w_ada.shape[1]
    tn = 1024
    return pl.pallas_call(
        _ada_kernel,
        out_shape=jax.ShapeDtypeStruct((rows, n_out), F32),
        grid=(n_out // tn,),
        in_specs=[pl.BlockSpec((rows, d), lambda j: (0, 0)),
                  pl.BlockSpec((d, tn), lambda j: (0, j)),
                  pl.BlockSpec((1, tn), lambda j: (0, j))],
        out_specs=pl.BlockSpec((rows, tn), lambda j: (0, j)),
        compiler_params=_cparams(("arbitrary",)),
        name="ada",
    )(cvec, w_ada, b_ada)


def _inproj_kernel(x_ref, sh_ref, sc_ref, w_ref, y_ref):
    u = _layer_norm(x_ref[...]) * (1.0 + sc_ref[0]) + sh_ref[0]
    y_ref[...] = jnp.dot(u.astype(BF16), w_ref[...], preferred_element_type=F32)


def _inproj(x2d, shift, scale, w, n_per_batch):
    t, d = x2d.shape
    tm = min(TM, n_per_batch)
    per = n_per_batch // tm
    return pl.pallas_call(
        _inproj_kernel,
        out_shape=jax.ShapeDtypeStruct((t, PROJ_W), F32),
        grid=(t // tm,),
        in_specs=[pl.BlockSpec((tm, d), lambda i: (i, 0)),
                  pl.BlockSpec((1, 1, d), lambda i: (i // per, 0, 0)),
                  pl.BlockSpec((1, 1, d), lambda i: (i // per, 0, 0)),
                  pl.BlockSpec((d, PROJ_W), lambda i: (0, 0))],
        out_specs=pl.BlockSpec((tm, PROJ_W), lambda i: (i, 0)),
        compiler_params=_cparams(("arbitrary",)),
        name="inproj",
    )(x2d, shift, scale, w)


def _conv_kernel(y_ref, w_ref, b_ref, o_ref, pad_ref, *, n, grid2d):
    zeros = jnp.zeros((CONV_PAD, LANES), F32)
    pad_ref[0:CONV_PAD, :] = zeros
    pad_ref[CONV_PAD + n:CONV_PAD + n + CONV_PAD, :] = zeros
    pad_ref[CONV_PAD:CONV_PAD + n, :] = y_ref[0]
    k_scale = jnp.where(pl.program_id(1) >= N_HEADS, D_HEAD ** -0.5, 1.0).astype(F32)
    bias = b_ref[...]
    rb = min(CONV_RB, n)
    col = lax.broadcasted_iota(jnp.int32, (rb, LANES), 0) % GRID_W
    not_first = col >= 1
    not_last = col <= GRID_W - 2
    row_taps = (0, 1, 2) if grid2d else (1,)
    for blk in range(n // rb):
        r0 = CONV_PAD + blk * rb
        acc = jnp.zeros((rb, LANES), F32)
        for dj in range(3):
            part = jnp.zeros((rb, LANES), F32)
            for di in row_taps:
                off = (di - 1) * GRID_W + (dj - 1)
                tap = w_ref[di * 3 + dj:di * 3 + dj + 1, :]
                part = part + tap * pad_ref[r0 + off:r0 + off + rb, :]
            if grid2d and dj == 0:
                part = jnp.where(not_first, part, 0.0)
            if grid2d and dj == 2:
                part = jnp.where(not_last, part, 0.0)
            acc = acc + part
        val = _silu(acc + bias) * k_scale
        o_ref[0, blk * rb:(blk + 1) * rb, :] = val.astype(o_ref.dtype)


def _conv(y3d, conv_w9, conv_b, grid2d):
    b, n, _ = y3d.shape
    n_ct = 2 * D_MLSTM // LANES
    return pl.pallas_call(
        functools.partial(_conv_kernel, n=n, grid2d=grid2d),
        out_shape=jax.ShapeDtypeStruct((b, n, 2 * D_MLSTM), BF16),
        grid=(b, n_ct),
        in_specs=[pl.BlockSpec((1, n, LANES), lambda i, c: (i, 0, c)),
                  pl.BlockSpec((9, LANES), lambda i, c: (0, c)),
                  pl.BlockSpec((1, LANES), lambda i, c: (0, c))],
        out_specs=pl.BlockSpec((1, n, LANES), lambda i, c: (i, 0, c)),
        scratch_shapes=[pltpu.VMEM((n + 2 * CONV_PAD, LANES), F32)],
        compiler_params=_cparams(("arbitrary", "arbitrary")),
        name="conv2d" if grid2d else "conv1d",
    )(y3d, conv_w9, conv_b)


GATE_A, GATE_CM, GATE_B = 0, 1, 2
GATE_GROUP = 8


def _gates_kernel(g_ref, gb_ref, rv_ref, *, nc):
    u_i = lax.broadcasted_iota(jnp.int32, (CHUNK, CHUNK), 0)
    t_i = lax.broadcasted_iota(jnp.int32, (CHUNK, CHUNK), 1)
    prefix = (u_i <= t_i).astype(F32)
    suffix = (u_i >= t_i).astype(F32)
    grp = min(GATE_GROUP, nc)
    rows = grp * SUBLANES
    lane = lax.broadcasted_iota(jnp.int32, (rows, LANES), 1)
    is_fwd = lax.broadcasted_iota(jnp.int32, (rows, LANES), 0) % SUBLANES < N_HEADS
    h = N_HEADS

    def running_max(a):
        fwd, bwd = a, a
        sh = 1
        while sh < CHUNK:
            fwd = jnp.where(lane >= sh, jnp.maximum(fwd, pltpu.roll(fwd, sh, axis=1)), fwd)
            bwd = jnp.where(lane < CHUNK - sh, jnp.maximum(bwd, pltpu.roll(bwd, CHUNK - sh, axis=1)), bwd)
            sh *= 2
        return jnp.where(is_fwd, fwd, bwd)

    def body(i, carry):
        c0 = i * grp
        r0 = pl.multiple_of(c0 * CHUNK, CHUNK)
        pre = g_ref[0, pl.ds(r0, grp * CHUNK), :] + gb_ref[...]
        pre_t = jnp.concatenate(
            [pre[j * CHUNK:(j + 1) * CHUNK].T[:N_GATE_COLS, :] for j in range(grp)], axis=0)
        lf_t = -(jnp.maximum(-pre_t, 0.0) + jnp.log1p(jnp.exp(-jnp.abs(pre_t))))
        cum_f = jnp.dot(lf_t, prefix, precision=HIGHEST, preferred_element_type=F32)
        cum_b = jnp.dot(lf_t, suffix, precision=HIGHEST, preferred_element_type=F32)
        pick = lambda t, j, q: t[j * N_GATE_COLS + q * h:j * N_GATE_COLS + (q + 1) * h]
        b8 = jnp.concatenate([x for j in range(grp) for x in (pick(cum_f, j, 1), pick(cum_b, j, 3))], axis=0)
        li8 = jnp.concatenate([x for j in range(grp) for x in (pick(pre_t, j, 0), pick(pre_t, j, 2))], axis=0)
        a8 = li8 - b8
        as_chunks = lambda t: t.reshape(grp, SUBLANES, LANES)
        rv_ref[0, pl.ds(c0, grp), GATE_A] = as_chunks(a8)
        rv_ref[0, pl.ds(c0, grp), GATE_CM] = as_chunks(running_max(a8))
        rv_ref[0, pl.ds(c0, grp), GATE_B] = as_chunks(b8)
        return carry

    lax.fori_loop(0, nc // grp, body, 0)


def _gates(y3d, gate_b_row):
    b, n, _ = y3d.shape
    nc = n // CHUNK
    return pl.pallas_call(
        functools.partial(_gates_kernel, nc=nc),
        out_shape=jax.ShapeDtypeStruct((b, nc, 3, SUBLANES, LANES), F32),
        grid=(b,),
        in_specs=[pl.BlockSpec((1, n, LANES), lambda i: (i, 0, COL_G // LANES)),
                  pl.BlockSpec((1, LANES), lambda i: (0, 0))],
        out_specs=pl.BlockSpec((1, nc, 3, SUBLANES, LANES), lambda i: (i, 0, 0, 0, 0)),
        compiler_params=_cparams(("arbitrary",)),
        name="gates",
    )(y3d, gate_b_row)


def _mlstm_kernel(q_ref, k_ref, v_ref, o_ref, rv_ref, c0_ref, m0_ref, hg_ref,
                  out_ref, cfin_ref, mfin_ref, hf_ref, hb_ref, cst_ref, mst_ref, *, nc):
    head = pl.program_id(1)
    cst_ref[...] = c0_ref[0, 0]
    mst_ref[...] = m0_ref[0, 0]
    s_i = lax.broadcasted_iota(jnp.int32, (CHUNK, CHUNK), 0)
    t_i = lax.broadcasted_iota(jnp.int32, (CHUNK, CHUNK), 1)
    ones_row = (s_i == 0).astype(F32)

    def one_chunk(c, d, mask_t, last, h_ref):
        r0 = pl.multiple_of(c * CHUNK, CHUNK)
        g = d * N_HEADS + head
        gate_row = lambda tbl: rv_ref[0, pl.ds(c, 1), tbl, pl.ds(g, 1), :].reshape(1, LANES)
        a_row, cm_row, b_row = gate_row(GATE_A), gate_row(GATE_CM), gate_row(GATE_B)
        q_t = q_ref[0, pl.ds(r0, CHUNK), :].T
        k = k_ref[0, pl.ds(r0, CHUNK), :]
        v_t_aug = jnp.concatenate([v_ref[0, pl.ds(r0, CHUNK), :].T, ones_row], axis=0)
        m_prev = mst_ref[d, 0:1, :]
        m_row = jnp.maximum(m_prev, cm_row)
        a_bc = jnp.broadcast_to(a_row, (CHUNK, CHUNK)).T
        d_t = jnp.where(mask_t, jnp.exp(a_bc - m_row), 0.0)
        s_t = jnp.dot(k, q_t, preferred_element_type=F32) * d_t
        cs = cst_ref[d]
        rhs = jnp.concatenate(
            [s_t.astype(BF16), (q_t.astype(F32) * jnp.exp(m_prev - m_row)).astype(BF16)], axis=0)
        lhs = jnp.concatenate([v_t_aug.astype(BF16), cs.astype(BF16)], axis=1)
        h_t = jnp.dot(lhs, rhs, preferred_element_type=F32)
        den = h_t[D_HEAD:D_HEAD + 1, :]
        floor = jnp.exp(-b_row - m_row)
        h_ref[pl.ds(c, 1)] = (h_t[:D_HEAD] / jnp.maximum(jnp.abs(den), floor))[None]
        m_last = m_row[:, last:last + 1]
        wv_t = (v_t_aug * jnp.exp(a_row - m_last)).astype(BF16)
        cst_ref[d] = jnp.exp(m_prev - m_last) * cs + jnp.dot(wv_t, k, preferred_element_type=F32)
        mst_ref[d] = jnp.broadcast_to(b_row[:, last:last + 1] + m_last, (SUBLANES, LANES))

    def body(c, carry):
        one_chunk(c, 0, s_i <= t_i, CHUNK - 1, hf_ref)
        one_chunk(nc - 1 - c, 1, s_i >= t_i, 0, hb_ref)
        return carry

    lax.fori_loop(0, nc, body, 0, unroll=min(8, nc))
    cfin_ref[0, 0] = cst_ref[...]
    mfin_ref[0, 0] = mst_ref[...]

    def readout(c, carry):
        r0 = pl.multiple_of(c * CHUNK, CHUNK)
        h_t = hf_ref[pl.ds(c, 1)][0] + hb_ref[pl.ds(c, 1)][0]
        hc = h_t - jnp.mean(h_t, axis=0, keepdims=True)
        hn = (hc * lax.rsqrt(jnp.mean(hc * hc, axis=0, keepdims=True) + LN_EPS)).T
        gate = jax.nn.sigmoid(o_ref[0, pl.ds(r0, CHUNK), :])
        out_ref[0, pl.ds(r0, CHUNK), :] = (hn * hg_ref[...] * gate).astype(out_ref.dtype)
        return carry

    lax.fori_loop(0, nc, readout, 0, unroll=min(4, nc))


def _mlstm(qk, y3d, rowv, c0, m0, head_g):
    b, n, _ = qk.shape
    nc = n // CHUNK
    v_blk = COL_V // LANES
    o_blk = COL_O // LANES
    tok = lambda off: pl.BlockSpec((1, n, LANES), lambda i, h: (i, 0, off + h))
    state_c = pl.BlockSpec((1, 1, 2, 2 * D_HEAD, D_HEAD), lambda i, h: (i, h, 0, 0, 0))
    state_m = pl.BlockSpec((1, 1, 2, SUBLANES, LANES), lambda i, h: (i, h, 0, 0, 0))
    return pl.pallas_call(
        functools.partial(_mlstm_kernel, nc=nc),
        out_shape=(jax.ShapeDtypeStruct((b, n, D_MLSTM), BF16),
                   jax.ShapeDtypeStruct((b, N_HEADS, 2, 2 * D_HEAD, D_HEAD), F32),
                   jax.ShapeDtypeStruct((b, N_HEADS, 2, SUBLANES, LANES), F32)),
        grid=(b, N_HEADS),
        in_specs=[tok(0), tok(N_HEADS), tok(v_blk), tok(o_blk),
                  pl.BlockSpec((1, nc, 3, SUBLANES, LANES), lambda i, h: (i, 0, 0, 0, 0)),
                  state_c, state_m,
                  pl.BlockSpec((1, LANES), lambda i, h: (0, h))],
        out_specs=(pl.BlockSpec((1, n, LANES), lambda i, h: (i, 0, h)), state_c, state_m),
        scratch_shapes=[pltpu.VMEM((nc, D_HEAD, CHUNK), F32), pltpu.VMEM((nc, D_HEAD, CHUNK), F32),
                        pltpu.VMEM((2, 2 * D_HEAD, D_HEAD), F32),
                        pltpu.VMEM((2, SUBLANES, LANES), F32)],
        compiler_params=_cparams(("arbitrary", "arbitrary")),
        name="mlstm",
    )(qk, qk, y3d, y3d, rowv, c0, m0, head_g)


def _dft_tables():
    r = DFT_R
    idx = np.arange(r)
    cg = np.cos(2 * np.pi * np.outer(idx, idx) / FOURIER_GROUP)
    sg = np.sin(2 * np.pi * np.outer(idx, idx) / FOURIER_GROUP)
    n_grp = D_FOURIER // FOURIER_GROUP
    bdc = np.kron(np.eye(n_grp), cg) / 8.0
    bds = np.kron(np.eye(n_grp), sg) / 8.0
    k1 = idx[:, None]
    a = idx[None, :]
    m1 = np.zeros((r, 2 * r, 2 * r))
    for b2 in range(r):
        th = 2 * np.pi * k1 * (r * a + b2) / (r * r)
        ec, es = np.cos(th), np.sin(th)
        m1[b2] = np.block([[ec, -es], [es, ec]]) / 8.0
    th2 = 2 * np.pi * np.outer(idx, idx) / r
    m2 = np.concatenate([np.cos(th2), -np.sin(th2)], axis=1) / 8.0
    as_bf16 = lambda t: jnp.asarray(t, dtype=F32).astype(BF16)
    return as_bf16(bdc), as_bf16(bds), as_bf16(m1), as_bf16(m2)


FOUR_BB = 8


N_SLABS = D_FOURIER // LANES


def _to_slabs(slab_ref, val):
    for s in range(N_SLABS):
        slab_ref[s] = val[:, s * LANES:(s + 1) * LANES]


def _strided_rows(slab_ref, start):
    return jnp.concatenate(
        [slab_ref[s, pl.ds(start, DFT_R, stride=FOUR_BB), :] for s in range(N_SLABS)], axis=1)


def _four1_kernel(f_ref, bdc_ref, bds_ref, m1_ref, z_ref, ps_ref, qs_ref):
    r = DFT_R
    fb = f_ref[0].reshape(r * FOUR_BB, D_FOURIER).astype(BF16)
    _to_slabs(ps_ref, jnp.dot(fb, bdc_ref[...], preferred_element_type=F32))
    _to_slabs(qs_ref, jnp.dot(fb, bds_ref[...], preferred_element_type=F32))
    for j in range(FOUR_BB):
        x2 = jnp.concatenate([_strided_rows(ps_ref, j), _strided_rows(qs_ref, j)], axis=0).astype(BF16)
        zz = jnp.dot(m1_ref[j], x2, preferred_element_type=F32)
        z_ref[0, j] = _pack_pair(zz[:r], zz[r:])


def _four2_kernel(z_ref, m2_ref, o_ref, zs_ref, os_ref):
    r = DFT_R
    _to_slabs(zs_ref, z_ref[0].reshape(r * FOUR_BB, D_FOURIER))
    for j in range(FOUR_BB):
        zc, zs = _unpack_pair_bf16(_strided_rows(zs_ref, j))
        o = jnp.dot(m2_ref[...], jnp.concatenate([zc, zs], axis=0), preferred_element_type=F32)
        for s in range(N_SLABS):
            os_ref[s, pl.ds(j, r, stride=FOUR_BB), :] = o[:, s * LANES:(s + 1) * LANES]
    for s in range(N_SLABS):
        o_ref[0, :, :, s * LANES:(s + 1) * LANES] = os_ref[s].reshape(r, FOUR_BB, LANES)


def _fourier(y4, tables):
    bdc, bds, m1, m2 = tables
    b = y4.shape[0]
    r = DFT_R
    nb = r // FOUR_BB
    slab = lambda dt: pltpu.VMEM((N_SLABS, r * FOUR_BB, LANES), dt)
    z = pl.pallas_call(
        _four1_kernel,
        out_shape=jax.ShapeDtypeStruct((b, r, r, D_FOURIER), jnp.int32),
        grid=(b, nb),
        in_specs=[pl.BlockSpec((1, r, FOUR_BB, D_FOURIER), lambda i, j: (i, 0, j, COL_F // D_FOURIER)),
                  pl.BlockSpec((D_FOURIER, D_FOURIER), lambda i, j: (0, 0)),
                  pl.BlockSpec((D_FOURIER, D_FOURIER), lambda i, j: (0, 0)),
                  pl.BlockSpec((FOUR_BB, 2 * r, 2 * r), lambda i, j: (j, 0, 0))],
        out_specs=pl.BlockSpec((1, FOUR_BB, r, D_FOURIER), lambda i, j: (i, j, 0, 0)),
        scratch_shapes=[slab(F32), slab(F32)],
        compiler_params=_cparams(("arbitrary", "arbitrary")),
        name="four1",
    )(y4, bdc, bds, m1)
    return pl.pallas_call(
        _four2_kernel,
        out_shape=jax.ShapeDtypeStruct((b, r, r, D_FOURIER), F32),
        grid=(b, nb),
        in_specs=[pl.BlockSpec((1, r, FOUR_BB, D_FOURIER), lambda i, j: (i, 0, j, 0)),
                  pl.BlockSpec((r, 2 * r), lambda i, j: (0, 0))],
        out_specs=pl.BlockSpec((1, r, FOUR_BB, D_FOURIER), lambda i, j: (i, 0, j, 0)),
        scratch_shapes=[slab(jnp.int32), slab(F32)],
        compiler_params=_cparams(("arbitrary", "arbitrary")),
        name="four2",
    )(z, m2)


def _outproj_kernel(x_ref, mr_ref, fo_ref, wm_ref, wf_ref, g1_ref, lg_ref, lb_ref,
                    sh_ref, sc_ref, wr_ref, x1_ref, xm_ref, s_ref):
    mix = (jnp.dot(mr_ref[...], wm_ref[...], preferred_element_type=F32)
           + jnp.dot(fo_ref[...].astype(BF16), wf_ref[...], preferred_element_type=F32))
    x1 = _layer_norm(DEEPNORM_ALPHA * x_ref[...] + g1_ref[0] * mix) * lg_ref[...] + lb_ref[...]
    x1_ref[...] = x1
    xm = _layer_norm(x1) * (1.0 + sc_ref[0]) + sh_ref[0]
    xm_ref[...] = _pack_rows(xm)
    logits = jnp.dot(xm.astype(BF16), wr_ref[...], preferred_element_type=F32)
    s_ref[...] = jax.nn.sigmoid(logits)


def _outproj(x2d, mread, four, w_m, w_f, gate1, ln_g, ln_b, shift2, scale2, w_router, n_per_batch):
    t, d = x2d.shape
    per = n_per_batch // TM
    tokd = lambda w: pl.BlockSpec((TM, w), lambda i: (i, 0))
    full = lambda r, c: pl.BlockSpec((r, c), lambda i: (0, 0))
    mod = pl.BlockSpec((1, 1, d), lambda i: (i // per, 0, 0))
    return pl.pallas_call(
        _outproj_kernel,
        out_shape=(jax.ShapeDtypeStruct((t, d), F32),
                   jax.ShapeDtypeStruct((t, PACK_W), jnp.int32),
                   jax.ShapeDtypeStruct((t, LANES), F32)),
        grid=(t // TM,),
        in_specs=[tokd(d), tokd(D_MLSTM), tokd(D_FOURIER), full(D_MLSTM, d), full(D_FOURIER, d),
                  mod, full(1, d), full(1, d), mod, mod, full(d, LANES)],
        out_specs=(tokd(d), tokd(PACK_W), tokd(LANES)),
        compiler_params=_cparams(("arbitrary",)),
        name="outproj",
    )(x2d, mread, four, w_m, w_f, gate1, ln_g, ln_b, shift2, scale2, w_router)


def _route_kernel(s_ref, b_ref, idx_ref, gw_ref, rank_ref, cnt_ref, carry_ref):
    @pl.when(pl.program_id(0) == 0)
    def _():
        carry_ref[...] = jnp.zeros_like(carry_ref)

    tm = s_ref.shape[0]
    reps = tm // LANES
    s_t = s_ref[...].T[:N_EXPERTS, :]
    e_i = lax.broadcasted_iota(jnp.int32, (N_EXPERTS, tm), 0).astype(F32)
    neg_inf = jnp.float32(-jnp.inf)
    sb = s_t + jnp.tile(b_ref[...], (1, reps))
    picks, sels = [], []
    taken = jnp.zeros((N_EXPERTS, tm), F32)
    for _ in range(TOP_K):
        best = jnp.max(sb, axis=0, keepdims=True)
        pick = jnp.min(jnp.where(sb == best, e_i, float(N_EXPERTS)), axis=0, keepdims=True)
        onehot = e_i == pick
        sels.append(jnp.sum(jnp.where(onehot, s_t, 0.0), axis=0, keepdims=True))
        picks.append(pick)
        sb = jnp.where(onehot, neg_inf, sb)
        taken = taken + onehot.astype(F32)
    u_i = lax.broadcasted_iota(jnp.int32, (tm, tm), 0)
    t_i = lax.broadcasted_iota(jnp.int32, (tm, tm), 1)
    before = (u_i < t_i).astype(BF16)
    taken_bf = taken.astype(BF16)
    rank_all = jnp.tile(carry_ref[...], (1, reps)) + jnp.dot(taken_bf, before, preferred_element_type=F32)
    carry_ref[...] = carry_ref[...] + jnp.dot(taken_bf, jnp.ones((tm, LANES), BF16),
                                              preferred_element_type=F32)
    cnt_ref[...] = carry_ref[...]
    total = sels[0]
    for v in sels[1:]:
        total = total + v
    ranks = [jnp.sum(jnp.where(e_i == picks[j], rank_all, 0.0), axis=0, keepdims=True) for j in range(TOP_K)]
    idx_ref[...] = jnp.concatenate(picks, axis=0).astype(jnp.int32)
    rank_ref[...] = jnp.concatenate(ranks, axis=0).astype(jnp.int32)
    gw_t = jnp.concatenate([ROUTED_SCALE * v / total for v in sels]
                           + [jnp.zeros((LANES - TOP_K, tm), F32)], axis=0)
    gw_ref[...] = gw_t.T[:, :TOP_K]


def _route(scores, bias_rep):
    t = scores.shape[0]
    tok = lambda w: pl.BlockSpec((TM, w), lambda i: (i, 0))
    slot_major = pl.BlockSpec((TOP_K, TM), lambda i: (0, i))
    per_expert = pl.BlockSpec((N_EXPERTS, LANES), lambda i: (0, 0))
    return pl.pallas_call(
        _route_kernel,
        out_shape=(jax.ShapeDtypeStruct((TOP_K, t), jnp.int32),
                   jax.ShapeDtypeStruct((t, TOP_K), F32),
                   jax.ShapeDtypeStruct((TOP_K, t), jnp.int32),
                   jax.ShapeDtypeStruct((N_EXPERTS, LANES), F32)),
        grid=(t // TM,),
        in_specs=[tok(LANES), per_expert],
        out_specs=(slot_major, tok(TOP_K), slot_major, per_expert),
        scratch_shapes=[pltpu.VMEM((N_EXPERTS, LANES), F32)],
        compiler_params=_cparams(("arbitrary",)),
        name="route",
    )(scores, bias_rep)


TN_DEST = 2048


def _dest_kernel(ps_ref, idx_ref, rank_ref, dest_ref):
    idx = idx_ref[...]
    start = jnp.zeros(idx.shape, jnp.int32)
    for e in range(N_EXPERTS):
        start = jnp.where(idx == e, ps_ref[e], start)
    dest_ref[...] = start + rank_ref[...]


def _dest(pad_start, idx_t, rank_t):
    t = idx_t.shape[1]
    blk = lambda: pl.BlockSpec((TOP_K, TN_DEST), lambda i, ps: (0, i))
    grid_spec = pltpu.PrefetchScalarGridSpec(
        num_scalar_prefetch=1, grid=(t // TN_DEST,), in_specs=[blk(), blk()], out_specs=blk())
    return pl.pallas_call(
        _dest_kernel,
        out_shape=jax.ShapeDtypeStruct((TOP_K, t), jnp.int32),
        grid_spec=grid_spec,
        compiler_params=_cparams(("arbitrary",)),
        name="dest",
    )(pad_start, idx_t, rank_t)


def _sc_worker_id(info):
    return lax.axis_index("s") * info.num_cores + lax.axis_index("c")


def _sc_dispatch(xm, dest3, n_rows):
    t, w = xm.shape
    info = plsc.get_sparse_core_info()
    n_workers = info.num_cores * info.num_subcores
    per_worker = t // SC_ROWS // n_workers
    mesh = plsc.VectorSubcoreMesh(core_axis_name="c", subcore_axis_name="s")

    @functools.partial(
        pl.kernel, mesh=mesh,
        out_type=jax.ShapeDtypeStruct((n_rows, w), xm.dtype),
        scratch_types=[pltpu.VMEM((TOP_K, SC_ROWS), jnp.int32), pltpu.VMEM((SC_ROWS, w), xm.dtype)],
        name="sc_dispatch",
    )
    def run(xm_hbm, dest_hbm, xs_hbm, idx_v, rows_v):
        first = _sc_worker_id(info) * per_worker

        @pl.loop(0, per_worker)
        def _(ci):
            chunk = first + ci
            pltpu.sync_copy(dest_hbm.at[chunk], idx_v)
            pltpu.sync_copy(xm_hbm.at[pl.ds(chunk * SC_ROWS, SC_ROWS)], rows_v)
            for j in range(TOP_K):
                pltpu.sync_copy(rows_v, xs_hbm.at[idx_v.at[j]])

    return run(xm, dest3)


def _sc_gather(table, idx2):
    n_chunks, rows = idx2.shape
    w = table.shape[1]
    info = plsc.get_sparse_core_info()
    n_workers = info.num_cores * info.num_subcores
    per_worker = n_chunks // n_workers
    assert per_worker % 2 == 0
    mesh = plsc.VectorSubcoreMesh(core_axis_name="c", subcore_axis_name="s")

    @functools.partial(
        pl.kernel, mesh=mesh,
        out_type=jax.ShapeDtypeStruct((n_chunks * rows, w), table.dtype),
        scratch_types=[pltpu.VMEM((2, rows), jnp.int32), pltpu.VMEM((2, rows, w), table.dtype),
                       pltpu.SemaphoreType.DMA, pltpu.SemaphoreType.DMA],
        name="sc_gather",
    )
    def run(table_hbm, idx_hbm, out_hbm, idx_v, rows_v, sem0, sem1):
        first = _sc_worker_id(info) * per_worker
        sems = (sem0, sem1)

        def gather(b):
            return pltpu.make_async_copy(table_hbm.at[idx_v.at[b]], rows_v.at[b], sems[b])

        def start(chunk, b):
            pltpu.sync_copy(idx_hbm.at[chunk], idx_v.at[b])
            gather(b).start()

        def finish(chunk, b):
            gather(b).wait()
            pltpu.sync_copy(rows_v.at[b], out_hbm.at[pl.ds(chunk * rows, rows)])

        start(first, 0)

        @pl.loop(0, per_worker, step=2)
        def _(ci):
            chunk = first + ci
            start(chunk + 1, 1)
            finish(chunk, 0)

            @pl.when(ci + 2 < per_worker)
            def _():
                start(chunk + 2, 0)

            finish(chunk + 1, 1)

    return run(table, idx2)


def _expert_kernel(first_ref, count_ref, used_ref,
                   xs_hbm, wgu_ref, wd_ref, y_hbm, wgu_bf, wd_bf, xbuf, ybuf, in_sem, out_sem):
    e = pl.program_id(0)
    n_used = used_ref[0]
    first = first_ref[e]
    n_blocks = count_ref[e]

    def rows(g):
        return pl.ds(pl.multiple_of(g * ROW_BLK, ROW_BLK), ROW_BLK)

    def read(g):
        slot = g % EXP_IN_SLOTS
        return pltpu.make_async_copy(xs_hbm.at[rows(g), :], xbuf.at[slot], in_sem.at[slot])

    def write(g):
        slot = g % EXP_OUT_SLOTS
        return pltpu.make_async_copy(ybuf.at[slot], y_hbm.at[rows(g), :], out_sem.at[slot])

    @pl.when(e == 0)
    def _():
        for g0 in range(EXP_AHEAD):
            @pl.when(g0 < n_used)
            def _():
                read(g0).start()

    @pl.when(n_blocks > 0)
    def _():
        wgu_bf[...] = wgu_ref[0].astype(BF16)
        wd_bf[...] = wd_ref[0].astype(BF16)

    def acquire(g):
        @pl.when(g + EXP_AHEAD < n_used)
        def _():
            read(g + EXP_AHEAD).start()

        read(g).wait()

        @pl.when(g >= EXP_OUT_SLOTS)
        def _():
            write(g - EXP_OUT_SLOTS).wait()

    def compute(g):
        in_slot = g % EXP_IN_SLOTS
        out_slot = g % EXP_OUT_SLOTS
        for h in range(ROW_BLK // ROW_HALF):
            r0 = h * ROW_HALF
            x = _unpack_bf16(xbuf[in_slot, r0:r0 + ROW_HALF, :])
            gu = jnp.dot(x, wgu_bf[...], preferred_element_type=F32)
            act = _silu(gu[:, :D_EXPERT]) * gu[:, D_EXPERT:]
            yv = jnp.dot(act.astype(BF16), wd_bf[...], preferred_element_type=F32)
            words = pltpu.pack_elementwise([yv[:, :PACK_W], yv[:, PACK_W:]], packed_dtype=BF16)
            ybuf[out_slot, r0:r0 + ROW_HALF, :] = lax.bitcast_convert_type(words, jnp.int32)

    def pair(k, carry):
        g = first + 2 * k
        acquire(g)
        acquire(g + 1)
        compute(g)
        compute(g + 1)
        write(g).start()
        write(g + 1).start()
        return carry

    lax.fori_loop(0, n_blocks // 2, pair, 0)

    @pl.when(n_blocks % 2 == 1)
    def _():
        g = first + n_blocks - 1
        acquire(g)
        compute(g)
        write(g).start()

    @pl.when(e == pl.num_programs(0) - 1)
    def _():
        for back in range(EXP_OUT_SLOTS, 0, -1):
            @pl.when(n_used >= back)
            def _():
                write(n_used - back).wait()


def _experts(first_block, n_blocks, n_used, xs, w_gu, w_down):
    n_rows, w = xs.shape
    d = D_MODEL
    grid_spec = pltpu.PrefetchScalarGridSpec(
        num_scalar_prefetch=3,
        grid=(N_EXPERTS,),
        in_specs=[pl.BlockSpec(memory_space=pl.ANY),
                  pl.BlockSpec((1, d, 2 * D_EXPERT), lambda e, *_: (e, 0, 0)),
                  pl.BlockSpec((1, D_EXPERT, d), lambda e, *_: (e, 0, 0))],
        out_specs=pl.BlockSpec(memory_space=pl.ANY),
        scratch_shapes=[pltpu.VMEM((d, 2 * D_EXPERT), BF16), pltpu.VMEM((D_EXPERT, d), BF16),
                        pltpu.VMEM((EXP_IN_SLOTS, ROW_BLK, w), jnp.int32),
                        pltpu.VMEM((EXP_OUT_SLOTS, ROW_BLK, w), jnp.int32),
                        pltpu.SemaphoreType.DMA((EXP_IN_SLOTS,)),
                        pltpu.SemaphoreType.DMA((EXP_OUT_SLOTS,))],
    )
    return pl.pallas_call(
        _expert_kernel,
        out_shape=jax.ShapeDtypeStruct((n_rows, w), jnp.int32),
        grid_spec=grid_spec,
        compiler_params=_cparams(("arbitrary",)),
        name="experts",
    )(first_block, n_blocks, n_used, xs, w_gu, w_down)


TM_COMBINE = 512


def _combine_kernel(*refs):
    yg_refs = refs[:TOP_K]
    gw_ref, x1_ref, xm_ref, g2_ref, wsgu_ref, wsd_ref, lg_ref, lb_ref, out_ref = refs[TOP_K:]
    gu = jnp.dot(_unpack_bf16(xm_ref[...]), wsgu_ref[...], preferred_element_type=F32)
    act = _silu(gu[:, :D_EXPERT]) * gu[:, D_EXPERT:]
    acc = jnp.dot(act.astype(BF16), wsd_ref[...], preferred_element_type=F32)
    gw = gw_ref[...]
    for j in range(TOP_K):
        acc = acc + gw[:, j:j + 1] * _unpack_rows(yg_refs[j][...])
    z = DEEPNORM_ALPHA * x1_ref[...] + g2_ref[0] * acc
    out_ref[...] = _layer_norm(z) * lg_ref[...] + lb_ref[...]


def _combine(yg, gw, x1, xm, gate2, w_sgu, w_sd, ln_g, ln_b, n_per_batch):
    t, d = x1.shape
    tm = TM_COMBINE
    per = n_per_batch // tm
    steps = t // tm
    tok = lambda w: pl.BlockSpec((tm, w), lambda i: (i, 0))
    full = lambda r, c: pl.BlockSpec((r, c), lambda i: (0, 0))
    slot = lambda j: pl.BlockSpec((tm, PACK_W), lambda i: (j * steps + i, 0))
    return pl.pallas_call(
        _combine_kernel,
        out_shape=jax.ShapeDtypeStruct((t, d), F32),
        grid=(steps,),
        in_specs=[slot(j) for j in range(TOP_K)]
        + [tok(TOP_K), tok(d), tok(PACK_W), pl.BlockSpec((1, 1, d), lambda i: (i // per, 0, 0)),
           full(d, 2 * D_EXPERT), full(D_EXPERT, d), full(1, d), full(1, d)],
        out_specs=tok(d),
        compiler_params=_cparams(("arbitrary",)),
        name="combine",
    )(*([yg] * TOP_K), gw, x1, xm, gate2, w_sgu, w_sd, ln_g, ln_b)


def _mixer_heads(stream, shift, scale, w_in_k, conv_w9, conv_b, gate_b_row, head_g, c0, m0, grid2d):
    b, n, d = stream.shape
    y = _inproj(stream.reshape(b * n, d), shift, scale, w_in_k, n).reshape(b, n, PROJ_W)
    qk = _conv(y, conv_w9, conv_b, grid2d)
    mread, c_fin, m_fin = _mlstm(qk, y, _gates(y, gate_b_row), c0, m0, head_g)
    return y, mread, c_fin, m_fin


MOE_GROUPS = 1


def _moe_group(xm, x1, scores, bias_row, gate2, w_gu, w_down, w_sgu, w_sd, ln_g, ln_b, n_per_batch):
    t = xm.shape[0]
    n_assign = t * TOP_K
    n_rows = n_assign + N_EXPERTS * ROW_BLK
    idx_t, gw, rank_t, counts = _route(scores, bias_row)
    sizes = counts[:, 0].astype(jnp.int32)
    padded = (sizes + ROW_BLK - 1) // ROW_BLK * ROW_BLK
    pad_end = jnp.cumsum(padded)
    pad_start = (pad_end - padded).astype(jnp.int32)
    dest_t = _dest(pad_start, idx_t, rank_t)
    dest3 = dest_t.reshape(TOP_K, t // SC_ROWS, SC_ROWS).transpose(1, 0, 2)

    xs = _sc_dispatch(xm, dest3, n_rows)
    yb = _experts(pad_start // ROW_BLK, (padded // ROW_BLK).astype(jnp.int32),
                  (pad_end[-1:] // ROW_BLK).astype(jnp.int32), xs, w_gu, w_down)
    yg = _sc_gather(yb, dest_t.reshape(n_assign // SC_GATHER_ROWS, SC_GATHER_ROWS))
    return _combine(yg, gw, x1, xm, gate2, w_sgu, w_sd, ln_g, ln_b, n_per_batch)


def kernel(x, c, ctx, c_ctx, w_ada, b_ada, w_in, conv_w, conv_b, gate_b, head_g, w_out, ln1_g, ln1_b,
           w_router, router_bias, w_expert_gu, w_expert_down, w_shared_gu, w_shared_down, ln2_g, ln2_b):
    bsz, n, d = x.shape
    n_ctx = ctx.shape[1]
    depth = w_ada.shape[0]
    t = bsz * n
    tables = _dft_tables()
    for l in range(depth):
        cvec = jnp.zeros((SUBLANES, d), F32).at[:bsz].set(c).at[bsz].set(c_ctx)
        mod = _ada(cvec, w_ada[l], b_ada[l][None, :])
        mods = mod.reshape(SUBLANES, 6, 1, d)
        mod_x = [mods[:bsz, j] for j in range(6)]
        mod_c = [jnp.broadcast_to(mods[bsz:bsz + 1, j], (bsz, 1, d)) for j in range(6)]

        wl = w_in[l]
        g0 = 4 * D_MLSTM
        w_in_k = jnp.concatenate(
            [wl[:, :g0], wl[:, g0 + N_GATE_COLS:], wl[:, g0:g0 + N_GATE_COLS],
             jnp.zeros((d, LANES - N_GATE_COLS), F32)], axis=1).astype(BF16)
        conv_w9 = conv_w[l].reshape(9, 2 * D_MLSTM)
        conv_b_row = conv_b[l][None, :]
        gate_b_row = jnp.zeros((1, LANES), F32).at[0, :N_GATE_COLS].set(gate_b[l].reshape(-1))
        head_g_row = head_g[l][None, :]

        c0 = jnp.zeros((bsz, N_HEADS, 2, 2 * D_HEAD, D_HEAD), F32)
        m0 = jnp.zeros((bsz, N_HEADS, 2, SUBLANES, LANES), F32)
        _, _, c_ctx_fin, m_ctx_fin = _mixer_heads(ctx, mod_c[0], mod_c[1], w_in_k, conv_w9, conv_b_row,
                                                  gate_b_row, head_g_row, c0, m0, grid2d=False)
        y, mread, _, _ = _mixer_heads(x, mod_x[0], mod_x[1], w_in_k, conv_w9, conv_b_row,
                                      gate_b_row, head_g_row, c_ctx_fin, m_ctx_fin, grid2d=True)

        four = _fourier(y.reshape(bsz, DFT_R, DFT_R, PROJ_W), tables).reshape(t, D_FOURIER)

        w_o = w_out[l].astype(BF16)
        w_r = jnp.zeros((d, LANES), F32).at[:, :N_EXPERTS].set(w_router[l]).astype(BF16)
        x2d = x.reshape(t, d)
        x1, xm, scores = _outproj(x2d, mread.reshape(t, D_MLSTM), four, w_o[:D_MLSTM], w_o[D_MLSTM:],
                                  mod_x[2], ln1_g[l][None, :], ln1_b[l][None, :], mod_x[3], mod_x[4],
                                  w_r, n)

        bias_row = jnp.broadcast_to(router_bias[l][:, None], (N_EXPERTS, LANES))
        w_sgu = w_shared_gu[l].astype(BF16)
        w_sd = w_shared_down[l].astype(BF16)
        tg = t // MOE_GROUPS
        bg = bsz // MOE_GROUPS
        outs = []
        for grp in range(MOE_GROUPS):
            tok = slice(grp * tg, (grp + 1) * tg)
            outs.append(_moe_group(xm[tok], x1[tok], scores[tok], bias_row, mod_x[5][grp * bg:(grp + 1) * bg],
                                   w_expert_gu[l], w_expert_down[l], w_sgu, w_sd,
                                   ln2_g[l][None, :], ln2_b[l][None, :], n))
        x = jnp.concatenate(outs, axis=0).reshape(bsz, n, d)
        assert depth == 1
    return x
```

```python
import functools
import math

import numpy as np
import jax
import jax.numpy as jnp
from jax import lax
from jax.experimental import pallas as pl
from jax.experimental.pallas import tpu as pltpu
from jax.experimental.pallas import tpu_sc as plsc

F32 = jnp.float32
BF16 = jnp.bfloat16
HIGHEST = lax.Precision.HIGHEST

D_MODEL = 1024
GRID_W = 64
N_HEADS = 4
D_HEAD = 128
D_MLSTM = N_HEADS * D_HEAD
D_FOURIER = 512
FOURIER_GROUP = 64
N_GATE_COLS = 16
CHUNK = 128
N_EXPERTS = 64
TOP_K = 8
D_EXPERT = 256
ROUTED_SCALE = 2.5
DEEPNORM_ALPHA = 2.0 ** 0.25
LN_EPS = 1e-5

LANES = 128
SUBLANES = 8
VMEM_LIMIT = 48 * 1024 * 1024
FOUR_VMEM_LIMIT = 56 * 1024 * 1024

COL_QK = 0
COL_V = 2 * D_MLSTM
COL_O = 3 * D_MLSTM
COL_F = 4 * D_MLSTM
COL_G = COL_F + D_FOURIER
PROJ_W = COL_G + LANES

TM = 512
ROW_BLK = 512
ROW_HALF = 256
EXP_AHEAD = 3
EXP_IN_SLOTS = 8
EXP_OUT_SLOTS = 4
PACK_W = D_MODEL // 2
SC_ROWS = 128
SC_GATHER_ROWS = 64
CONV_PAD = 72
CONV_RB = 256
DFT_R = 64


def _cparams(sem, vmem=VMEM_LIMIT):
    return pltpu.CompilerParams(dimension_semantics=sem, vmem_limit_bytes=vmem)


def _layer_norm(x):
    mu = jnp.mean(x, axis=-1, keepdims=True)
    xc = x - mu
    var = jnp.mean(xc * xc, axis=-1, keepdims=True)
    return xc * lax.rsqrt(var + LN_EPS)


def _silu(x):
    return x * jax.nn.sigmoid(x)


def _pack_pair(lo, hi):
    as_bits = lambda u: lax.bitcast_convert_type(u.astype(BF16).astype(F32), jnp.uint32)
    word = as_bits(hi) | lax.shift_right_logical(as_bits(lo), jnp.uint32(16))
    return lax.bitcast_convert_type(word, jnp.int32)


def _unpack_pair_bf16(w):
    lo = lax.bitcast_convert_type(w.astype(jnp.int16), BF16)
    hi = lax.bitcast_convert_type(lax.shift_right_logical(w, jnp.int32(16)).astype(jnp.int16), BF16)
    return lo, hi


def _pack_rows(v):
    return _pack_pair(v[:, :PACK_W], v[:, PACK_W:])


def _unpack_bf16(w):
    return jnp.concatenate(_unpack_pair_bf16(w), axis=-1)


def _unpack_rows(w):
    u = lax.bitcast_convert_type(w, jnp.uint32)
    lo = lax.bitcast_convert_type(lax.shift_left(u, jnp.uint32(16)), F32)
    hi = lax.bitcast_convert_type(u & jnp.uint32(0xFFFF0000), F32)
    return jnp.concatenate([lo, hi], axis=1)


def _ada_kernel(c_ref, w_ref, b_ref, o_ref):
    s = _silu(c_ref[...])
    o_ref[...] = jnp.dot(s, w_ref[...], precision=HIGHEST, preferred_element_type=F32) + b_ref[...]


def _ada(cvec, w_ada, b_ada):
    rows, d = cvec.shape
    n_out = w_ada.shape[1]
    tn = 1024
    return pl.pallas_call(
        _ada_kernel,
        out_shape=jax.ShapeDtypeStruct((rows, n_out), F32),
        grid=(n_out // tn,),
        in_specs=[pl.BlockSpec((rows, d), lambda j: (0, 0)),
                  pl.BlockSpec((d, tn), lambda j: (0, j)),
                  pl.BlockSpec((1, tn), lambda j: (0, j))],
        out_specs=pl.BlockSpec((rows, tn), lambda j: (0, j)),
        compiler_params=_cparams(("arbitrary",)),
        name="ada",
    )(cvec, w_ada, b_ada)


def _inproj_kernel(x_ref, sh_ref, sc_ref, w_ref, y_ref):
    u = _layer_norm(x_ref[...]) * (1.0 + sc_ref[0]) + sh_ref[0]
    y_ref[...] = jnp.dot(u.astype(BF16), w_ref[...], preferred_element_type=F32)


def _inproj(x2d, shift, scale, w, n_per_batch):
    t, d = x2d.shape
    tm = min(TM, n_per_batch)
    per = n_per_batch // tm
    return pl.pallas_call(
        _inproj_kernel,
        out_shape=jax.ShapeDtypeStruct((t, PROJ_W), F32),
        grid=(t // tm,),
        in_specs=[pl.BlockSpec((tm, d), lambda i: (i, 0)),
                  pl.BlockSpec((1, 1, d), lambda i: (i // per, 0, 0)),
                  pl.BlockSpec((1, 1, d), lambda i: (i // per, 0, 0)),
                  pl.BlockSpec((d, PROJ_W), lambda i: (0, 0))],
        out_specs=pl.BlockSpec((tm, PROJ_W), lambda i: (i, 0)),
        compiler_params=_cparams(("arbitrary",)),
        name="inproj",
    )(x2d, shift, scale, w)


def _conv_kernel(y_ref, w_ref, b_ref, o_ref, pad_ref, *, n, grid2d, tiles):
    zeros = jnp.zeros((CONV_PAD, LANES), F32)
    pad_ref[0:CONV_PAD, :] = zeros
    pad_ref[CONV_PAD + n:CONV_PAD + n + CONV_PAD, :] = zeros
    rb = min(CONV_RB, n)
    col = lax.broadcasted_iota(jnp.int32, (rb, LANES), 0) % GRID_W
    not_first = col >= 1
    not_last = col <= GRID_W - 2
    row_taps = (0, 1, 2) if grid2d else (1,)
    for ct in range(tiles):
        lanes = slice(ct * LANES, (ct + 1) * LANES)
        pad_ref[CONV_PAD:CONV_PAD + n, :] = y_ref[0, :, lanes]
        k_scale = jnp.where(pl.program_id(1) * tiles + ct >= N_HEADS, D_HEAD ** -0.5, 1.0).astype(F32)
        bias = b_ref[:, lanes]
        for blk in range(n // rb):
            r0 = CONV_PAD + blk * rb
            acc = jnp.zeros((rb, LANES), F32)
            for dj in range(3):
                part = jnp.zeros((rb, LANES), F32)
                for di in row_taps:
                    off = (di - 1) * GRID_W + (dj - 1)
                    tap = w_ref[di * 3 + dj:di * 3 + dj + 1, lanes]
                    part = part + tap * pad_ref[r0 + off:r0 + off + rb, :]
                if grid2d and dj == 0:
                    part = jnp.where(not_first, part, 0.0)
                if grid2d and dj == 2:
                    part = jnp.where(not_last, part, 0.0)
                acc = acc + part
            val = _silu(acc + bias) * k_scale
            o_ref[0, blk * rb:(blk + 1) * rb, lanes] = val.astype(o_ref.dtype)


def _conv(y3d, conv_w9, conv_b, grid2d):
    b, n, _ = y3d.shape
    n_ct = 2 * D_MLSTM // LANES
    tiles = 1 if grid2d else n_ct
    w = tiles * LANES
    return pl.pallas_call(
        functools.partial(_conv_kernel, n=n, grid2d=grid2d, tiles=tiles),
        out_shape=jax.ShapeDtypeStruct((b, n, 2 * D_MLSTM), BF16),
        grid=(b, n_ct // tiles),
        in_specs=[pl.BlockSpec((1, n, w), lambda i, c: (i, 0, c)),
                  pl.BlockSpec((9, w), lambda i, c: (0, c)),
                  pl.BlockSpec((1, w), lambda i, c: (0, c))],
        out_specs=pl.BlockSpec((1, n, w), lambda i, c: (i, 0, c)),
        scratch_shapes=[pltpu.VMEM((n + 2 * CONV_PAD, LANES), F32)],
        compiler_params=_cparams(("arbitrary", "arbitrary")),
        name="conv2d" if grid2d else "conv1d",
    )(y3d, conv_w9, conv_b)


GATE_A, GATE_CM, GATE_B = 0, 1, 2
GATE_GROUP = 8


def _gates_kernel(g_ref, gb_ref, rv_ref, *, nc):
    u_i = lax.broadcasted_iota(jnp.int32, (CHUNK, CHUNK), 0)
    t_i = lax.broadcasted_iota(jnp.int32, (CHUNK, CHUNK), 1)
    prefix = (u_i <= t_i).astype(F32)
    suffix = (u_i >= t_i).astype(F32)
    grp = min(GATE_GROUP, nc)
    rows = grp * SUBLANES
    lane = lax.broadcasted_iota(jnp.int32, (rows, LANES), 1)
    is_fwd = lax.broadcasted_iota(jnp.int32, (rows, LANES), 0) % SUBLANES < N_HEADS
    h = N_HEADS

    def running_max(a):
        fwd, bwd = a, a
        sh = 1
        while sh < CHUNK:
            fwd = jnp.where(lane >= sh, jnp.maximum(fwd, pltpu.roll(fwd, sh, axis=1)), fwd)
            bwd = jnp.where(lane < CHUNK - sh, jnp.maximum(bwd, pltpu.roll(bwd, CHUNK - sh, axis=1)), bwd)
            sh *= 2
        return jnp.where(is_fwd, fwd, bwd)

    def body(i, carry):
        c0 = i * grp
        r0 = pl.multiple_of(c0 * CHUNK, CHUNK)
        pre = g_ref[0, pl.ds(r0, grp * CHUNK), :] + gb_ref[...]
        pre_t = jnp.concatenate(
            [pre[j * CHUNK:(j + 1) * CHUNK].T[:N_GATE_COLS, :] for j in range(grp)], axis=0)
        lf_t = -(jnp.maximum(-pre_t, 0.0) + jnp.log1p(jnp.exp(-jnp.abs(pre_t))))
        cum_f = jnp.dot(lf_t, prefix, precision=HIGHEST, preferred_element_type=F32)
        cum_b = jnp.dot(lf_t, suffix, precision=HIGHEST, preferred_element_type=F32)
        pick = lambda t, j, q: t[j * N_GATE_COLS + q * h:j * N_GATE_COLS + (q + 1) * h]
        b8 = jnp.concatenate([x for j in range(grp) for x in (pick(cum_f, j, 1), pick(cum_b, j, 3))], axis=0)
        li8 = jnp.concatenate([x for j in range(grp) for x in (pick(pre_t, j, 0), pick(pre_t, j, 2))], axis=0)
        a8 = li8 - b8
        as_chunks = lambda t: t.reshape(grp, SUBLANES, LANES)
        rv_ref[0, pl.ds(c0, grp), GATE_A] = as_chunks(a8)
        rv_ref[0, pl.ds(c0, grp), GATE_CM] = as_chunks(running_max(a8))
        rv_ref[0, pl.ds(c0, grp), GATE_B] = as_chunks(b8)
        return carry

    lax.fori_loop(0, nc // grp, body, 0)


def _gates(y3d, gate_b_row):
    b, n, _ = y3d.shape
    nc = n // CHUNK
    return pl.pallas_call(
        functools.partial(_gates_kernel, nc=nc),
        out_shape=jax.ShapeDtypeStruct((b, nc, 3, SUBLANES, LANES), F32),
        grid=(b,),
        in_specs=[pl.BlockSpec((1, n, LANES), lambda i: (i, 0, COL_G // LANES)),
                  pl.BlockSpec((1, LANES), lambda i: (0, 0))],
        out_specs=pl.BlockSpec((1, nc, 3, SUBLANES, LANES), lambda i: (i, 0, 0, 0, 0)),
        compiler_params=_cparams(("arbitrary",)),
        name="gates",
    )(y3d, gate_b_row)


def _mlstm_kernel(q_ref, k_ref, v_ref, o_ref, rv_ref, c0_ref, m0_ref, hg_ref,
                  out_ref, cfin_ref, mfin_ref, hf_ref, hb_ref, cst_ref, mst_ref, *, nc, heads):
    cst_ref[...] = c0_ref[0]
    mst_ref[...] = m0_ref[0]
    s_i = lax.broadcasted_iota(jnp.int32, (CHUNK, CHUNK), 0)
    t_i = lax.broadcasted_iota(jnp.int32, (CHUNK, CHUNK), 1)
    ones_row = (s_i == 0).astype(F32)

    def one_chunk(c, d, mask_t, last, h_ref, hh):
        r0 = pl.multiple_of(c * CHUNK, CHUNK)
        lanes = slice(hh * D_HEAD, (hh + 1) * D_HEAD)
        g = d * N_HEADS + pl.program_id(1) * heads + hh
        gate_row = lambda tbl: rv_ref[0, pl.ds(c, 1), tbl, pl.ds(g, 1), :].reshape(1, LANES)
        a_row, cm_row, b_row = gate_row(GATE_A), gate_row(GATE_CM), gate_row(GATE_B)
        q_t = q_ref[0, pl.ds(r0, CHUNK), lanes].T
        k = k_ref[0, pl.ds(r0, CHUNK), lanes]
        v_t_aug = jnp.concatenate([v_ref[0, pl.ds(r0, CHUNK), lanes].T, ones_row], axis=0)
        m_prev = mst_ref[hh, d, 0:1, :]
        m_row = jnp.maximum(m_prev, cm_row)
        a_bc = jnp.broadcast_to(a_row, (CHUNK, CHUNK)).T
        d_t = jnp.where(mask_t, jnp.exp(a_bc - m_row), 0.0)
        s_t = jnp.dot(k, q_t, preferred_element_type=F32) * d_t
        cs = cst_ref[hh, d]
        rhs = jnp.concatenate(
            [s_t.astype(BF16), (q_t.astype(F32) * jnp.exp(m_prev - m_row)).astype(BF16)], axis=0)
        lhs = jnp.concatenate([v_t_aug.astype(BF16), cs.astype(BF16)], axis=1)
        h_t = jnp.dot(lhs, rhs, preferred_element_type=F32)
        den = h_t[D_HEAD:D_HEAD + 1, :]
        floor = jnp.exp(-b_row - m_row)
        h_ref[hh, pl.ds(c, 1)] = (h_t[:D_HEAD] / jnp.maximum(jnp.abs(den), floor))[None]
        m_last = m_row[:, last:last + 1]
        wv_t = (v_t_aug * jnp.exp(a_row - m_last)).astype(BF16)
        cst_ref[hh, d] = jnp.exp(m_prev - m_last) * cs + jnp.dot(wv_t, k, preferred_element_type=F32)
        mst_ref[hh, d] = jnp.broadcast_to(b_row[:, last:last + 1] + m_last, (SUBLANES, LANES))

    def body(c, carry):
        for hh in range(heads):
            one_chunk(c, 0, s_i <= t_i, CHUNK - 1, hf_ref, hh)
            one_chunk(nc - 1 - c, 1, s_i >= t_i, 0, hb_ref, hh)
        return carry

    lax.fori_loop(0, nc, body, 0, unroll=min(8, nc))
    cfin_ref[0] = cst_ref[...]
    mfin_ref[0] = mst_ref[...]

    def readout(c, carry):
        r0 = pl.multiple_of(c * CHUNK, CHUNK)
        for hh in range(heads):
            lanes = slice(hh * D_HEAD, (hh + 1) * D_HEAD)
            h_t = hf_ref[hh, pl.ds(c, 1)][0] + hb_ref[hh, pl.ds(c, 1)][0]
            hc = h_t - jnp.mean(h_t, axis=0, keepdims=True)
            hn = (hc * lax.rsqrt(jnp.mean(hc * hc, axis=0, keepdims=True) + LN_EPS)).T
            gate = jax.nn.sigmoid(o_ref[0, pl.ds(r0, CHUNK), lanes])
            out_ref[0, pl.ds(r0, CHUNK), lanes] = (hn * hg_ref[:, lanes] * gate).astype(out_ref.dtype)
        return carry

    lax.fori_loop(0, nc, readout, 0, unroll=min(4, nc))


def _mlstm(qk, y3d, rowv, c0, m0, head_g):
    b, n, _ = qk.shape
    nc = n // CHUNK
    heads = N_HEADS if nc <= 4 else 1
    w = heads * D_HEAD
    v_blk = COL_V // w
    o_blk = COL_O // w
    tok = lambda off: pl.BlockSpec((1, n, w), lambda i, h: (i, 0, off + h))
    state_c = pl.BlockSpec((1, heads, 2, 2 * D_HEAD, D_HEAD), lambda i, h: (i, h, 0, 0, 0))
    state_m = pl.BlockSpec((1, heads, 2, SUBLANES, LANES), lambda i, h: (i, h, 0, 0, 0))
    return pl.pallas_call(
        functools.partial(_mlstm_kernel, nc=nc, heads=heads),
        out_shape=(jax.ShapeDtypeStruct((b, n, D_MLSTM), BF16),
                   jax.ShapeDtypeStruct((b, N_HEADS, 2, 2 * D_HEAD, D_HEAD), F32),
                   jax.ShapeDtypeStruct((b, N_HEADS, 2, SUBLANES, LANES), F32)),
        grid=(b, N_HEADS // heads),
        in_specs=[tok(0), tok(D_MLSTM // w), tok(v_blk), tok(o_blk),
                  pl.BlockSpec((1, nc, 3, SUBLANES, LANES), lambda i, h: (i, 0, 0, 0, 0)),
                  state_c, state_m,
                  pl.BlockSpec((1, w), lambda i, h: (0, h))],
        out_specs=(pl.BlockSpec((1, n, w), lambda i, h: (i, 0, h)), state_c, state_m),
        scratch_shapes=[pltpu.VMEM((heads, nc, D_HEAD, CHUNK), F32), pltpu.VMEM((heads, nc, D_HEAD, CHUNK), F32),
                        pltpu.VMEM((heads, 2, 2 * D_HEAD, D_HEAD), F32),
                        pltpu.VMEM((heads, 2, SUBLANES, LANES), F32)],
        compiler_params=_cparams(("arbitrary", "arbitrary")),
        name="mlstm",
    )(qk, qk, y3d, y3d, rowv, c0, m0, head_g)


def _dft_tables():
    r = DFT_R
    idx = np.arange(r)
    cg = np.cos(2 * np.pi * np.outer(idx, idx) / FOURIER_GROUP)
    sg = np.sin(2 * np.pi * np.outer(idx, idx) / FOURIER_GROUP)
    n_grp = D_FOURIER // FOURIER_GROUP
    bdc = np.kron(np.eye(n_grp), cg) / 8.0
    bds = np.kron(np.eye(n_grp), sg) / 8.0
    k1 = idx[:, None]
    a = idx[None, :]
    m1 = np.zeros((r, 2 * r, 2 * r))
    for b2 in range(r):
        th = 2 * np.pi * k1 * (r * a + b2) / (r * r)
        ec, es = np.cos(th), np.sin(th)
        m1[b2] = np.block([[ec, -es], [es, ec]]) / 8.0
    th2 = 2 * np.pi * np.outer(idx, idx) / r
    m2 = np.concatenate([np.cos(th2), -np.sin(th2)], axis=1) / 8.0
    as_bf16 = lambda t: jnp.asarray(t, dtype=F32).astype(BF16)
    return as_bf16(bdc), as_bf16(bds), as_bf16(m1), as_bf16(m2)


FOUR_BB = 8


N_SLABS = D_FOURIER // LANES


def _to_slabs(slab_ref, val):
    for s in range(N_SLABS):
        slab_ref[s] = val[:, s * LANES:(s + 1) * LANES]


def _strided_rows(slab_ref, start):
    return jnp.concatenate(
        [slab_ref[s, pl.ds(start, DFT_R, stride=FOUR_BB), :] for s in range(N_SLABS)], axis=1)


def _four_kernel(f_ref, bdc_ref, bds_ref, m1_ref, m2_ref, o_ref, z_ref, ps_ref, qs_ref, zs_ref, os_ref):
    r = DFT_R

    def stage1(blk, carry):
        b0 = pl.multiple_of(blk * FOUR_BB, FOUR_BB)
        fb = f_ref[0, :, pl.ds(b0, FOUR_BB), :].reshape(r * FOUR_BB, D_FOURIER).astype(BF16)
        _to_slabs(ps_ref, jnp.dot(fb, bdc_ref[...], preferred_element_type=F32))
        _to_slabs(qs_ref, jnp.dot(fb, bds_ref[...], preferred_element_type=F32))
        for j in range(FOUR_BB):
            x2 = jnp.concatenate([_strided_rows(ps_ref, j), _strided_rows(qs_ref, j)], axis=0).astype(BF16)
            zz = jnp.dot(m1_ref[b0 + j], x2, preferred_element_type=F32)
            z_ref[b0 + j] = _pack_pair(zz[:r], zz[r:])
        return carry

    lax.fori_loop(0, r // FOUR_BB, stage1, 0)

    def stage2(blk, carry):
        k0 = pl.multiple_of(blk * FOUR_BB, FOUR_BB)
        _to_slabs(zs_ref, z_ref[:, pl.ds(k0, FOUR_BB), :].reshape(r * FOUR_BB, D_FOURIER))
        for j in range(FOUR_BB):
            zc, zs = _unpack_pair_bf16(_strided_rows(zs_ref, j))
            o = jnp.dot(m2_ref[...], jnp.concatenate([zc, zs], axis=0), preferred_element_type=F32)
            for s in range(N_SLABS):
                os_ref[s, pl.ds(j, r, stride=FOUR_BB), :] = o[:, s * LANES:(s + 1) * LANES]
        for s in range(N_SLABS):
            o_ref[0, :, pl.ds(k0, FOUR_BB), s * LANES:(s + 1) * LANES] = os_ref[s].reshape(r, FOUR_BB, LANES)
        return carry

    lax.fori_loop(0, r // FOUR_BB, stage2, 0)


def _fourier(y4, tables):
    bdc, bds, m1, m2 = tables
    b = y4.shape[0]
    r = DFT_R
    slab = lambda dt: pltpu.VMEM((N_SLABS, r * FOUR_BB, LANES), dt)
    const = lambda *shape: pl.BlockSpec(shape, lambda i: (0,) * len(shape))
    return pl.pallas_call(
        _four_kernel,
        out_shape=jax.ShapeDtypeStruct((b, r, r, D_FOURIER), F32),
        grid=(b,),
        in_specs=[pl.BlockSpec((1, r, r, D_FOURIER), lambda i: (i, 0, 0, COL_F // D_FOURIER)),
                  const(D_FOURIER, D_FOURIER), const(D_FOURIER, D_FOURIER),
                  const(r, 2 * r, 2 * r), const(r, 2 * r)],
        out_specs=pl.BlockSpec((1, r, r, D_FOURIER), lambda i: (i, 0, 0, 0)),
        scratch_shapes=[pltpu.VMEM((r, r, D_FOURIER), jnp.int32), slab(F32), slab(F32), slab(jnp.int32), slab(F32)],
        compiler_params=_cparams(("arbitrary",), vmem=FOUR_VMEM_LIMIT),
        name="fourier",
    )(y4, bdc, bds, m1, m2)


def _outproj_kernel(x_ref, mr_ref, fo_ref, wm_ref, wf_ref, g1_ref, lg_ref, lb_ref,
                    sh_ref, sc_ref, wr_ref, x1_ref, xm_ref, s_ref):
    mix = (jnp.dot(mr_ref[...], wm_ref[...], preferred_element_type=F32)
           + jnp.dot(fo_ref[...].astype(BF16), wf_ref[...], preferred_element_type=F32))
    x1 = _layer_norm(DEEPNORM_ALPHA * x_ref[...] + g1_ref[0] * mix) * lg_ref[...] + lb_ref[...]
    x1_ref[...] = x1
    xm = _layer_norm(x1) * (1.0 + sc_ref[0]) + sh_ref[0]
    xm_ref[...] = _pack_rows(xm)
    logits = jnp.dot(xm.astype(BF16), wr_ref[...], preferred_element_type=F32)
    s_ref[...] = jax.nn.sigmoid(logits)


def _outproj(x2d, mread, four, w_m, w_f, gate1, ln_g, ln_b, shift2, scale2, w_router, n_per_batch):
    t, d = x2d.shape
    per = n_per_batch // TM
    tokd = lambda w: pl.BlockSpec((TM, w), lambda i: (i, 0))
    full = lambda r, c: pl.BlockSpec((r, c), lambda i: (0, 0))
    mod = pl.BlockSpec((1, 1, d), lambda i: (i // per, 0, 0))
    return pl.pallas_call(
        _outproj_kernel,
        out_shape=(jax.ShapeDtypeStruct((t, d), F32),
                   jax.ShapeDtypeStruct((t, PACK_W), jnp.int32),
                   jax.ShapeDtypeStruct((t, LANES), F32)),
        grid=(t // TM,),
        in_specs=[tokd(d), tokd(D_MLSTM), tokd(D_FOURIER), full(D_MLSTM, d), full(D_FOURIER, d),
                  mod, full(1, d), full(1, d), mod, mod, full(d, LANES)],
        out_specs=(tokd(d), tokd(PACK_W), tokd(LANES)),
        compiler_params=_cparams(("arbitrary",)),
        name="outproj",
    )(x2d, mread, four, w_m, w_f, gate1, ln_g, ln_b, shift2, scale2, w_router)


def _route_kernel(s_ref, b_ref, idx_ref, gw_ref, rank_ref, cnt_ref, carry_ref):
    @pl.when(pl.program_id(0) == 0)
    def _():
        carry_ref[...] = jnp.zeros_like(carry_ref)

    tm = s_ref.shape[0]
    reps = tm // LANES
    s_t = s_ref[...].T[:N_EXPERTS, :]
    e_i = lax.broadcasted_iota(jnp.int32, (N_EXPERTS, tm), 0).astype(F32)
    neg_inf = jnp.float32(-jnp.inf)
    sb = s_t + jnp.tile(b_ref[...], (1, reps))
    picks, sels = [], []
    taken = jnp.zeros((N_EXPERTS, tm), F32)
    for _ in range(TOP_K):
        best = jnp.max(sb, axis=0, keepdims=True)
        pick = jnp.min(jnp.where(sb == best, e_i, float(N_EXPERTS)), axis=0, keepdims=True)
        onehot = e_i == pick
        sels.append(jnp.sum(jnp.where(onehot, s_t, 0.0), axis=0, keepdims=True))
        picks.append(pick)
        sb = jnp.where(onehot, neg_inf, sb)
        taken = taken + onehot.astype(F32)
    u_i = lax.broadcasted_iota(jnp.int32, (tm, tm), 0)
    t_i = lax.broadcasted_iota(jnp.int32, (tm, tm), 1)
    before = (u_i < t_i).astype(BF16)
    taken_bf = taken.astype(BF16)
    rank_all = jnp.tile(carry_ref[...], (1, reps)) + jnp.dot(taken_bf, before, preferred_element_type=F32)
    carry_ref[...] = carry_ref[...] + jnp.dot(taken_bf, jnp.ones((tm, LANES), BF16),
                                              preferred_element_type=F32)
    cnt_ref[...] = carry_ref[...]
    total = sels[0]
    for v in sels[1:]:
        total = total + v
    ranks = [jnp.sum(jnp.where(e_i == picks[j], rank_all, 0.0), axis=0, keepdims=True) for j in range(TOP_K)]
    idx_ref[...] = jnp.concatenate(picks, axis=0).astype(jnp.int32)
    rank_ref[...] = jnp.concatenate(ranks, axis=0).astype(jnp.int32)
    gw_t = jnp.concatenate([ROUTED_SCALE * v / total for v in sels]
                           + [jnp.zeros((LANES - TOP_K, tm), F32)], axis=0)
    gw_ref[...] = gw_t.T[:, :TOP_K]


def _route(scores, bias_rep):
    t = scores.shape[0]
    tok = lambda w: pl.BlockSpec((TM, w), lambda i: (i, 0))
    slot_major = pl.BlockSpec((TOP_K, TM), lambda i: (0, i))
    per_expert = pl.BlockSpec((N_EXPERTS, LANES), lambda i: (0, 0))
    return pl.pallas_call(
        _route_kernel,
        out_shape=(jax.ShapeDtypeStruct((TOP_K, t), jnp.int32),
                   jax.ShapeDtypeStruct((t, TOP_K), F32),
                   jax.ShapeDtypeStruct((TOP_K, t), jnp.int32),
                   jax.ShapeDtypeStruct((N_EXPERTS, LANES), F32)),
        grid=(t // TM,),
        in_specs=[tok(LANES), per_expert],
        out_specs=(slot_major, tok(TOP_K), slot_major, per_expert),
        scratch_shapes=[pltpu.VMEM((N_EXPERTS, LANES), F32)],
        compiler_params=_cparams(("arbitrary",)),
        name="route",
    )(scores, bias_rep)


TN_DEST = 2048


def _dest_kernel(ps_ref, idx_ref, rank_ref, dest_ref):
    idx = idx_ref[...]
    start = jnp.zeros(idx.shape, jnp.int32)
    for e in range(N_EXPERTS):
        start = jnp.where(idx == e, ps_ref[e], start)
    dest_ref[...] = start + rank_ref[...]


def _dest(pad_start, idx_t, rank_t):
    t = idx_t.shape[1]
    blk = lambda: pl.BlockSpec((TOP_K, TN_DEST), lambda i, ps: (0, i))
    grid_spec = pltpu.PrefetchScalarGridSpec(
        num_scalar_prefetch=1, grid=(t // TN_DEST,), in_specs=[blk(), blk()], out_specs=blk())
    return pl.pallas_call(
        _dest_kernel,
        out_shape=jax.ShapeDtypeStruct((TOP_K, t), jnp.int32),
        grid_spec=grid_spec,
        compiler_params=_cparams(("arbitrary",)),
        name="dest",
    )(pad_start, idx_t, rank_t)


def _sc_worker_id(info):
    return lax.axis_index("s") * info.num_cores + lax.axis_index("c")


def _sc_dispatch(xm, dest3, n_rows):
    t, w = xm.shape
    info = plsc.get_sparse_core_info()
    n_workers = info.num_cores * info.num_subcores
    per_worker = t // SC_ROWS // n_workers
    mesh = plsc.VectorSubcoreMesh(core_axis_name="c", subcore_axis_name="s")

    @functools.partial(
        pl.kernel, mesh=mesh,
        out_type=jax.ShapeDtypeStruct((n_rows, w), xm.dtype),
        scratch_types=[pltpu.VMEM((TOP_K, SC_ROWS), jnp.int32), pltpu.VMEM((SC_ROWS, w), xm.dtype)],
        name="sc_dispatch",
    )
    def run(xm_hbm, dest_hbm, xs_hbm, idx_v, rows_v):
        first = _sc_worker_id(info) * per_worker

        @pl.loop(0, per_worker)
        def _(ci):
            chunk = first + ci
            pltpu.sync_copy(dest_hbm.at[chunk], idx_v)
            pltpu.sync_copy(xm_hbm.at[pl.ds(chunk * SC_ROWS, SC_ROWS)], rows_v)
            for j in range(TOP_K):
                pltpu.sync_copy(rows_v, xs_hbm.at[idx_v.at[j]])

    return run(xm, dest3)


def _sc_gather(table, idx2):
    n_chunks, rows = idx2.shape
    w = table.shape[1]
    info = plsc.get_sparse_core_info()
    n_workers = info.num_cores * info.num_subcores
    per_worker = n_chunks // n_workers
    assert per_worker % 2 == 0
    mesh = plsc.VectorSubcoreMesh(core_axis_name="c", subcore_axis_name="s")

    @functools.partial(
        pl.kernel, mesh=mesh,
        out_type=jax.ShapeDtypeStruct((n_chunks * rows, w), table.dtype),
        scratch_types=[pltpu.VMEM((2, rows), jnp.int32), pltpu.VMEM((2, rows, w), table.dtype),
                       pltpu.SemaphoreType.DMA, pltpu.SemaphoreType.DMA],
        name="sc_gather",
    )
    def run(table_hbm, idx_hbm, out_hbm, idx_v, rows_v, sem0, sem1):
        first = _sc_worker_id(info) * per_worker
        sems = (sem0, sem1)

        def gather(b):
            return pltpu.make_async_copy(table_hbm.at[idx_v.at[b]], rows_v.at[b], sems[b])

        def start(chunk, b):
            pltpu.sync_copy(idx_hbm.at[chunk], idx_v.at[b])
            gather(b).start()

        def finish(chunk, b):
            gather(b).wait()
            pltpu.sync_copy(rows_v.at[b], out_hbm.at[pl.ds(chunk * rows, rows)])

        start(first, 0)

        @pl.loop(0, per_worker, step=2)
        def _(ci):
            chunk = first + ci
            start(chunk + 1, 1)
            finish(chunk, 0)

            @pl.when(ci + 2 < per_worker)
            def _():
                start(chunk + 2, 0)

            finish(chunk + 1, 1)

    return run(table, idx2)


def _expert_kernel(first_ref, count_ref, used_ref,
                   xs_hbm, wgu_ref, wd_ref, y_hbm, wgu_bf, wd_bf, xbuf, ybuf, in_sem, out_sem):
    e = pl.program_id(0)
    n_used = used_ref[0]
    first = first_ref[e]
    n_blocks = count_ref[e]

    def rows(g):
        return pl.ds(pl.multiple_of(g * ROW_BLK, ROW_BLK), ROW_BLK)

    def read(g):
        slot = g % EXP_IN_SLOTS
        return pltpu.make_async_copy(xs_hbm.at[rows(g), :], xbuf.at[slot], in_sem.at[slot])

    def write(g):
        slot = g % EXP_OUT_SLOTS
        return pltpu.make_async_copy(ybuf.at[slot], y_hbm.at[rows(g), :], out_sem.at[slot])

    @pl.when(e == 0)
    def _():
        for g0 in range(EXP_AHEAD):
            @pl.when(g0 < n_used)
            def _():
                read(g0).start()

    @pl.when(n_blocks > 0)
    def _():
        wgu_bf[...] = wgu_ref[0].astype(BF16)
        wd_bf[...] = wd_ref[0].astype(BF16)

    def acquire(g):
        @pl.when(g + EXP_AHEAD < n_used)
        def _():
            read(g + EXP_AHEAD).start()

        read(g).wait()

        @pl.when(g >= EXP_OUT_SLOTS)
        def _():
            write(g - EXP_OUT_SLOTS).wait()

    def compute(g):
        in_slot = g % EXP_IN_SLOTS
        out_slot = g % EXP_OUT_SLOTS
        for h in range(ROW_BLK // ROW_HALF):
            r0 = h * ROW_HALF
            x = _unpack_bf16(xbuf[in_slot, r0:r0 + ROW_HALF, :])
            gu = jnp.dot(x, wgu_bf[...], preferred_element_type=F32)
            act = _silu(gu[:, :D_EXPERT]) * gu[:, D_EXPERT:]
            yv = jnp.dot(act.astype(BF16), wd_bf[...], preferred_element_type=F32)
            words = pltpu.pack_elementwise([yv[:, :PACK_W], yv[:, PACK_W:]], packed_dtype=BF16)
            ybuf[out_slot, r0:r0 + ROW_HALF, :] = lax.bitcast_convert_type(words, jnp.int32)

    def pair(k, carry):
        g = first + 2 * k
        acquire(g)
        acquire(g + 1)
        compute(g)
        compute(g + 1)
        write(g).start()
        write(g + 1).start()
        return carry

    lax.fori_loop(0, n_blocks // 2, pair, 0)

    @pl.when(n_blocks % 2 == 1)
    def _():
        g = first + n_blocks - 1
        acquire(g)
        compute(g)
        write(g).start()

    @pl.when(e == pl.num_programs(0) - 1)
    def _():
        for back in range(EXP_OUT_SLOTS, 0, -1):
            @pl.when(n_used >= back)
            def _():
                write(n_used - back).wait()


def _experts(first_block, n_blocks, n_used, xs, w_gu, w_down):
    n_rows, w = xs.shape
    d = D_MODEL
    grid_spec = pltpu.PrefetchScalarGridSpec(
        num_scalar_prefetch=3,
        grid=(N_EXPERTS,),
        in_specs=[pl.BlockSpec(memory_space=pl.ANY),
                  pl.BlockSpec((1, d, 2 * D_EXPERT), lambda e, *_: (e, 0, 0)),
                  pl.BlockSpec((1, D_EXPERT, d), lambda e, *_: (e, 0, 0))],
        out_specs=pl.BlockSpec(memory_space=pl.ANY),
        scratch_shapes=[pltpu.VMEM((d, 2 * D_EXPERT), BF16), pltpu.VMEM((D_EXPERT, d), BF16),
                        pltpu.VMEM((EXP_IN_SLOTS, ROW_BLK, w), jnp.int32),
                        pltpu.VMEM((EXP_OUT_SLOTS, ROW_BLK, w), jnp.int32),
                        pltpu.SemaphoreType.DMA((EXP_IN_SLOTS,)),
                        pltpu.SemaphoreType.DMA((EXP_OUT_SLOTS,))],
    )
    return pl.pallas_call(
        _expert_kernel,
        out_shape=jax.ShapeDtypeStruct((n_rows, w), jnp.int32),
        grid_spec=grid_spec,
        compiler_params=_cparams(("arbitrary",)),
        name="experts",
    )(first_block, n_blocks, n_used, xs, w_gu, w_down)


TM_COMBINE = 512


def _combine_kernel(*refs):
    yg_refs = refs[:TOP_K]
    gw_ref, x1_ref, xm_ref, g2_ref, wsgu_ref, wsd_ref, lg_ref, lb_ref, out_ref = refs[TOP_K:]
    gu = jnp.dot(_unpack_bf16(xm_ref[...]), wsgu_ref[...], preferred_element_type=F32)
    act = _silu(gu[:, :D_EXPERT]) * gu[:, D_EXPERT:]
    acc = jnp.dot(act.astype(BF16), wsd_ref[...], preferred_element_type=F32)
    gw = gw_ref[...]
    for j in range(TOP_K):
        acc = acc + gw[:, j:j + 1] * _unpack_rows(yg_refs[j][...])
    z = DEEPNORM_ALPHA * x1_ref[...] + g2_ref[0] * acc
    out_ref[...] = _layer_norm(z) * lg_ref[...] + lb_ref[...]


def _combine(yg, gw, x1, xm, gate2, w_sgu, w_sd, ln_g, ln_b, n_per_batch):
    t, d = x1.shape
    tm = TM_COMBINE
    per = n_per_batch // tm
    steps = t // tm
    tok = lambda w: pl.BlockSpec((tm, w), lambda i: (i, 0))
    full = lambda r, c: pl.BlockSpec((r, c), lambda i: (0, 0))
    slot = lambda j: pl.BlockSpec((tm, PACK_W), lambda i: (j * steps + i, 0))
    return pl.pallas_call(
        _combine_kernel,
        out_shape=jax.ShapeDtypeStruct((t, d), F32),
        grid=(steps,),
        in_specs=[slot(j) for j in range(TOP_K)]
        + [tok(TOP_K), tok(d), tok(PACK_W), pl.BlockSpec((1, 1, d), lambda i: (i // per, 0, 0)),
           full(d, 2 * D_EXPERT), full(D_EXPERT, d), full(1, d), full(1, d)],
        out_specs=tok(d),
        compiler_params=_cparams(("arbitrary",)),
        name="combine",
    )(*([yg] * TOP_K), gw, x1, xm, gate2, w_sgu, w_sd, ln_g, ln_b)


def _mixer_heads(stream, shift, scale, w_in_k, conv_w9, conv_b, gate_b_row, head_g, c0, m0, grid2d):
    b, n, d = stream.shape
    y = _inproj(stream.reshape(b * n, d), shift, scale, w_in_k, n).reshape(b, n, PROJ_W)
    qk = _conv(y, conv_w9, conv_b, grid2d)
    mread, c_fin, m_fin = _mlstm(qk, y, _gates(y, gate_b_row), c0, m0, head_g)
    return y, mread, c_fin, m_fin


MOE_GROUPS = 1


def _moe_group(xm, x1, scores, bias_row, gate2, w_gu, w_down, w_sgu, w_sd, ln_g, ln_b, n_per_batch):
    t = xm.shape[0]
    n_assign = t * TOP_K
    n_rows = n_assign + N_EXPERTS * ROW_BLK
    idx_t, gw, rank_t, counts = _route(scores, bias_row)
    sizes = counts[:, 0].astype(jnp.int32)
    padded = (sizes + ROW_BLK - 1) // ROW_BLK * ROW_BLK
    pad_end = jnp.cumsum(padded)
    pad_start = (pad_end - padded).astype(jnp.int32)
    dest_t = _dest(pad_start, idx_t, rank_t)
    dest3 = dest_t.reshape(TOP_K, t // SC_ROWS, SC_ROWS).transpose(1, 0, 2)

    xs = _sc_dispatch(xm, dest3, n_rows)
    yb = _experts(pad_start // ROW_BLK, (padded // ROW_BLK).astype(jnp.int32),
                  (pad_end[-1:] // ROW_BLK).astype(jnp.int32), xs, w_gu, w_down)
    yg = _sc_gather(yb, dest_t.reshape(n_assign // SC_GATHER_ROWS, SC_GATHER_ROWS))
    return _combine(yg, gw, x1, xm, gate2, w_sgu, w_sd, ln_g, ln_b, n_per_batch)


def kernel(x, c, ctx, c_ctx, w_ada, b_ada, w_in, conv_w, conv_b, gate_b, head_g, w_out, ln1_g, ln1_b,
           w_router, router_bias, w_expert_gu, w_expert_down, w_shared_gu, w_shared_down, ln2_g, ln2_b):
    bsz, n, d = x.shape
    n_ctx = ctx.shape[1]
    depth = w_ada.shape[0]
    t = bsz * n
    tables = _dft_tables()
    for l in range(depth):
        cvec = jnp.zeros((SUBLANES, d), F32).at[:bsz].set(c).at[bsz].set(c_ctx)
        mod = _ada(cvec, w_ada[l], b_ada[l][None, :])
        mods = mod.reshape(SUBLANES, 6, 1, d)
        mod_x = [mods[:bsz, j] for j in range(6)]
        mod_c = [jnp.broadcast_to(mods[bsz:bsz + 1, j], (bsz, 1, d)) for j in range(6)]

        wl = w_in[l]
        g0 = 4 * D_MLSTM
        w_in_k = jnp.concatenate(
            [wl[:, :g0], wl[:, g0 + N_GATE_COLS:], wl[:, g0:g0 + N_GATE_COLS],
             jnp.zeros((d, LANES - N_GATE_COLS), F32)], axis=1).astype(BF16)
        conv_w9 = conv_w[l].reshape(9, 2 * D_MLSTM)
        conv_b_row = conv_b[l][None, :]
        gate_b_row = jnp.zeros((1, LANES), F32).at[0, :N_GATE_COLS].set(gate_b[l].reshape(-1))
        head_g_row = head_g[l][None, :]

        c0 = jnp.zeros((bsz, N_HEADS, 2, 2 * D_HEAD, D_HEAD), F32)
        m0 = jnp.zeros((bsz, N_HEADS, 2, SUBLANES, LANES), F32)
        _, _, c_ctx_fin, m_ctx_fin = _mixer_heads(ctx, mod_c[0], mod_c[1], w_in_k, conv_w9, conv_b_row,
                                                  gate_b_row, head_g_row, c0, m0, grid2d=False)
        y, mread, _, _ = _mixer_heads(x, mod_x[0], mod_x[1], w_in_k, conv_w9, conv_b_row,
                                      gate_b_row, head_g_row, c_ctx_fin, m_ctx_fin, grid2d=True)

        four = _fourier(y.reshape(bsz, DFT_R, DFT_R, PROJ_W), tables).reshape(t, D_FOURIER)

        w_o = w_out[l].astype(BF16)
        w_r = jnp.zeros((d, LANES), F32).at[:, :N_EXPERTS].set(w_router[l]).astype(BF16)
        x2d = x.reshape(t, d)
        x1, xm, scores = _outproj(x2d, mread.reshape(t, D_MLSTM), four, w_o[:D_MLSTM], w_o[D_MLSTM:],
                                  mod_x[2], ln1_g[l][None, :], ln1_b[l][None, :], mod_x[3], mod_x[4],
                                  w_r, n)

        bias_row = jnp.broadcast_to(router_bias[l][:, None], (N_EXPERTS, LANES))
        w_sgu = w_shared_gu[l].astype(BF16)
        w_sd = w_shared_down[l].astype(BF16)
        tg = t // MOE_GROUPS
        bg = bsz // MOE_GROUPS
        outs = []
        for grp in range(MOE_GROUPS):
            tok = slice(grp * tg, (grp + 1) * tg)
            outs.append(_moe_group(xm[tok], x1[tok], scores[tok], bias_row, mod_x[5][grp * bg:(grp + 1) * bg],
                                   w_expert_gu[l], w_expert_down[l], w_sgu, w_sd,
                                   ln2_g[l][None, :], ln2_b[l][None, :], n))
        x = jnp.concatenate(outs, axis=0).reshape(bsz, n, d)
        assert depth == 1
    return x
```

```python
import functools
import math

import numpy as np
import jax
import jax.numpy as jnp
from jax import lax
from jax.experimental import pallas as pl
from jax.experimental.pallas import tpu as pltpu
from jax.experimental.pallas import tpu_sc as plsc

F32 = jnp.float32
BF16 = jnp.bfloat16
HIGHEST = lax.Precision.HIGHEST

D_MODEL = 1024
GRID_W = 64
N_HEADS = 4
D_HEAD = 128
D_MLSTM = N_HEADS * D_HEAD
D_FOURIER = 512
FOURIER_GROUP = 64
N_GATE_COLS = 16
CHUNK = 128
N_EXPERTS = 64
TOP_K = 8
D_EXPERT = 256
ROUTED_SCALE = 2.5
DEEPNORM_ALPHA = 2.0 ** 0.25
LN_EPS = 1e-5

LANES = 128
SUBLANES = 8
VMEM_LIMIT = 48 * 1024 * 1024
FOUR_VMEM_LIMIT = 56 * 1024 * 1024

COL_QK = 0
COL_V = 2 * D_MLSTM
COL_O = 3 * D_MLSTM
COL_F = 4 * D_MLSTM
COL_G = COL_F + D_FOURIER
PROJ_W = COL_G + LANES

TM = 512
ROW_BLK = 512
ROW_HALF = 256
EXP_AHEAD = 3
EXP_IN_SLOTS = 8
EXP_OUT_SLOTS = 4
PACK_W = D_MODEL // 2
SC_ROWS = 128
SC_GATHER_ROWS = 64
CONV_PAD = 72
CONV_RB = 256
DFT_R = 64


def _cparams(sem, vmem=VMEM_LIMIT):
    return pltpu.CompilerParams(dimension_semantics=sem, vmem_limit_bytes=vmem)


def _layer_norm(x):
    mu = jnp.mean(x, axis=-1, keepdims=True)
    xc = x - mu
    var = jnp.mean(xc * xc, axis=-1, keepdims=True)
    return xc * lax.rsqrt(var + LN_EPS)


def _silu(x):
    return x * jax.nn.sigmoid(x)


def _pack_pair(lo, hi):
    as_bits = lambda u: lax.bitcast_convert_type(u.astype(BF16).astype(F32), jnp.uint32)
    word = as_bits(hi) | lax.shift_right_logical(as_bits(lo), jnp.uint32(16))
    return lax.bitcast_convert_type(word, jnp.int32)


def _unpack_pair_bf16(w):
    lo = lax.bitcast_convert_type(w.astype(jnp.int16), BF16)
    hi = lax.bitcast_convert_type(lax.shift_right_logical(w, jnp.int32(16)).astype(jnp.int16), BF16)
    return lo, hi


def _pack_rows(v):
    return _pack_pair(v[:, :PACK_W], v[:, PACK_W:])


def _unpack_bf16(w):
    return jnp.concatenate(_unpack_pair_bf16(w), axis=-1)


def _unpack_rows(w):
    u = lax.bitcast_convert_type(w, jnp.uint32)
    lo = lax.bitcast_convert_type(lax.shift_left(u, jnp.uint32(16)), F32)
    hi = lax.bitcast_convert_type(u & jnp.uint32(0xFFFF0000), F32)
    return jnp.concatenate([lo, hi], axis=1)


def _ada_kernel(c_ref, w_ref, b_ref, o_ref):
    s = _silu(c_ref[...])
    o_ref[...] = jnp.dot(s, w_ref[...], precision=HIGHEST, preferred_element_type=F32) + b_ref[...]


def _ada(cvec, w_ada, b_ada):
    rows, d = cvec.shape
    n_out = w_ada.shape[1]
    tn = 1024
    return pl.pallas_call(
        _ada_kernel,
        out_shape=jax.ShapeDtypeStruct((rows, n_out), F32),
        grid=(n_out // tn,),
        in_specs=[pl.BlockSpec((rows, d), lambda j: (0, 0)),
                  pl.BlockSpec((d, tn), lambda j: (0, j)),
                  pl.BlockSpec((1, tn), lambda j: (0, j))],
        out_specs=pl.BlockSpec((rows, tn), lambda j: (0, j)),
        compiler_params=_cparams(("arbitrary",)),
        name="ada",
    )(cvec, w_ada, b_ada)


def _inproj_kernel(x_ref, sh_ref, sc_ref, w_ref, y_ref):
    half = x_ref.shape[0] // 2
    for h in range(2):
        r = slice(h * half, (h + 1) * half)
        u = _layer_norm(x_ref[r, :]) * (1.0 + sc_ref[0]) + sh_ref[0]
        y_ref[r, :] = jnp.dot(u.astype(BF16), w_ref[...], preferred_element_type=F32)


def _inproj(x2d, shift, scale, w, n_per_batch):
    t, d = x2d.shape
    tm = min(TM, n_per_batch)
    per = n_per_batch // tm
    return pl.pallas_call(
        _inproj_kernel,
        out_shape=jax.ShapeDtypeStruct((t, PROJ_W), F32),
        grid=(t // tm,),
        in_specs=[pl.BlockSpec((tm, d), lambda i: (i, 0)),
                  pl.BlockSpec((1, 1, d), lambda i: (i // per, 0, 0)),
                  pl.BlockSpec((1, 1, d), lambda i: (i // per, 0, 0)),
                  pl.BlockSpec((d, PROJ_W), lambda i: (0, 0))],
        out_specs=pl.BlockSpec((tm, PROJ_W), lambda i: (i, 0)),
        compiler_params=_cparams(("arbitrary",)),
        name="inproj",
    )(x2d, shift, scale, w)


def _conv_kernel(y_ref, w_ref, b_ref, o_ref, pad_ref, *, n, grid2d, tiles):
    zeros = jnp.zeros((CONV_PAD, LANES), F32)
    pad_ref[0:CONV_PAD, :] = zeros
    pad_ref[CONV_PAD + n:CONV_PAD + n + CONV_PAD, :] = zeros
    rb = min(CONV_RB, n)
    col = lax.broadcasted_iota(jnp.int32, (rb, LANES), 0) % GRID_W
    not_first = col >= 1
    not_last = col <= GRID_W - 2
    row_taps = (0, 1, 2) if grid2d else (1,)
    for ct in range(tiles):
        lanes = slice(ct * LANES, (ct + 1) * LANES)
        pad_ref[CONV_PAD:CONV_PAD + n, :] = y_ref[0, :, lanes]
        k_scale = jnp.where(pl.program_id(1) * tiles + ct >= N_HEADS, D_HEAD ** -0.5, 1.0).astype(F32)
        bias = b_ref[:, lanes]
        for blk in range(n // rb):
            r0 = CONV_PAD + blk * rb
            acc = jnp.zeros((rb, LANES), F32)
            for dj in range(3):
                part = jnp.zeros((rb, LANES), F32)
                for di in row_taps:
                    off = (di - 1) * GRID_W + (dj - 1)
                    tap = w_ref[di * 3 + dj:di * 3 + dj + 1, lanes]
                    part = part + tap * pad_ref[r0 + off:r0 + off + rb, :]
                if grid2d and dj == 0:
                    part = jnp.where(not_first, part, 0.0)
                if grid2d and dj == 2:
                    part = jnp.where(not_last, part, 0.0)
                acc = acc + part
            val = _silu(acc + bias) * k_scale
            o_ref[0, blk * rb:(blk + 1) * rb, lanes] = val.astype(o_ref.dtype)


def _conv(y3d, conv_w9, conv_b, grid2d):
    b, n, _ = y3d.shape
    n_ct = 2 * D_MLSTM // LANES
    tiles = 1 if grid2d else n_ct
    w = tiles * LANES
    return pl.pallas_call(
        functools.partial(_conv_kernel, n=n, grid2d=grid2d, tiles=tiles),
        out_shape=jax.ShapeDtypeStruct((b, n, 2 * D_MLSTM), BF16),
        grid=(b, n_ct // tiles),
        in_specs=[pl.BlockSpec((1, n, w), lambda i, c: (i, 0, c)),
                  pl.BlockSpec((9, w), lambda i, c: (0, c)),
                  pl.BlockSpec((1, w), lambda i, c: (0, c))],
        out_specs=pl.BlockSpec((1, n, w), lambda i, c: (i, 0, c)),
        scratch_shapes=[pltpu.VMEM((n + 2 * CONV_PAD, LANES), F32)],
        compiler_params=_cparams(("arbitrary", "arbitrary")),
        name="conv2d" if grid2d else "conv1d",
    )(y3d, conv_w9, conv_b)


STATE_ROWS = 2 * D_HEAD
GATE_A, GATE_CM, GATE_B = 0, 1, 2
GATE_GROUP = 8


def _gates_kernel(g_ref, gb_ref, rv_ref, *, nc):
    u_i = lax.broadcasted_iota(jnp.int32, (CHUNK, CHUNK), 0)
    t_i = lax.broadcasted_iota(jnp.int32, (CHUNK, CHUNK), 1)
    prefix = (u_i <= t_i).astype(F32)
    suffix = (u_i >= t_i).astype(F32)
    grp = min(GATE_GROUP, nc)
    rows = grp * SUBLANES
    lane = lax.broadcasted_iota(jnp.int32, (rows, LANES), 1)
    is_fwd = lax.broadcasted_iota(jnp.int32, (rows, LANES), 0) % SUBLANES < N_HEADS
    h = N_HEADS

    def running_max(a):
        fwd, bwd = a, a
        sh = 1
        while sh < CHUNK:
            fwd = jnp.where(lane >= sh, jnp.maximum(fwd, pltpu.roll(fwd, sh, axis=1)), fwd)
            bwd = jnp.where(lane < CHUNK - sh, jnp.maximum(bwd, pltpu.roll(bwd, CHUNK - sh, axis=1)), bwd)
            sh *= 2
        return jnp.where(is_fwd, fwd, bwd)

    def body(i, carry):
        c0 = i * grp
        r0 = pl.multiple_of(c0 * CHUNK, CHUNK)
        pre = g_ref[0, pl.ds(r0, grp * CHUNK), :] + gb_ref[...]
        pre_t = jnp.concatenate(
            [pre[j * CHUNK:(j + 1) * CHUNK].T[:N_GATE_COLS, :] for j in range(grp)], axis=0)
        lf_t = -(jnp.maximum(-pre_t, 0.0) + jnp.log1p(jnp.exp(-jnp.abs(pre_t))))
        cum_f = jnp.dot(lf_t, prefix, precision=HIGHEST, preferred_element_type=F32)
        cum_b = jnp.dot(lf_t, suffix, precision=HIGHEST, preferred_element_type=F32)
        pick = lambda t, j, q: t[j * N_GATE_COLS + q * h:j * N_GATE_COLS + (q + 1) * h]
        b8 = jnp.concatenate([x for j in range(grp) for x in (pick(cum_f, j, 1), pick(cum_b, j, 3))], axis=0)
        li8 = jnp.concatenate([x for j in range(grp) for x in (pick(pre_t, j, 0), pick(pre_t, j, 2))], axis=0)
        a8 = li8 - b8
        as_chunks = lambda t: t.reshape(grp, SUBLANES, LANES)
        rv_ref[0, pl.ds(c0, grp), GATE_A] = as_chunks(a8)
        rv_ref[0, pl.ds(c0, grp), GATE_CM] = as_chunks(running_max(a8))
        rv_ref[0, pl.ds(c0, grp), GATE_B] = as_chunks(b8)
        return carry

    lax.fori_loop(0, nc // grp, body, 0)


def _gates(y3d, gate_b_row):
    b, n, _ = y3d.shape
    nc = n // CHUNK
    return pl.pallas_call(
        functools.partial(_gates_kernel, nc=nc),
        out_shape=jax.ShapeDtypeStruct((b, nc, 3, SUBLANES, LANES), F32),
        grid=(b,),
        in_specs=[pl.BlockSpec((1, n, LANES), lambda i: (i, 0, COL_G // LANES)),
                  pl.BlockSpec((1, LANES), lambda i: (0, 0))],
        out_specs=pl.BlockSpec((1, nc, 3, SUBLANES, LANES), lambda i: (i, 0, 0, 0, 0)),
        compiler_params=_cparams(("arbitrary",)),
        name="gates",
    )(y3d, gate_b_row)


def _mlstm_kernel(q_ref, k_ref, v_ref, o_ref, rv_ref, c0_ref, m0_ref, hg_ref,
                  out_ref, cfin_ref, mfin_ref, hf_ref, hb_ref, cst_ref, mst_ref, *, nc, heads):
    cst_ref[...] = c0_ref[0]
    mst_ref[...] = m0_ref[0]
    s_i = lax.broadcasted_iota(jnp.int32, (CHUNK, CHUNK), 0)
    t_i = lax.broadcasted_iota(jnp.int32, (CHUNK, CHUNK), 1)
    ones_row = (s_i[:STATE_ROWS - D_HEAD] == 0).astype(F32)

    def one_chunk(c, d, mask_t, last, h_ref, hh):
        r0 = pl.multiple_of(c * CHUNK, CHUNK)
        lanes = slice(hh * D_HEAD, (hh + 1) * D_HEAD)
        g = d * N_HEADS + pl.program_id(1) * heads + hh
        gate_row = lambda tbl: rv_ref[0, pl.ds(c, 1), tbl, pl.ds(g, 1), :].reshape(1, LANES)
        a_row, cm_row, b_row = gate_row(GATE_A), gate_row(GATE_CM), gate_row(GATE_B)
        q_t = q_ref[0, pl.ds(r0, CHUNK), lanes].T
        k = k_ref[0, pl.ds(r0, CHUNK), lanes]
        v_t_aug = jnp.concatenate([v_ref[0, pl.ds(r0, CHUNK), lanes].T, ones_row], axis=0)
        m_prev = mst_ref[hh, d, 0:1, :]
        m_row = jnp.maximum(m_prev, cm_row)
        a_bc = jnp.broadcast_to(a_row, (CHUNK, CHUNK)).T
        d_t = jnp.where(mask_t, jnp.exp(a_bc - m_row), 0.0)
        s_t = jnp.dot(k, q_t, preferred_element_type=F32) * d_t
        cs = cst_ref[hh, d]
        rhs = jnp.concatenate(
            [s_t.astype(BF16), (q_t.astype(F32) * jnp.exp(m_prev - m_row)).astype(BF16)], axis=0)
        lhs = jnp.concatenate([v_t_aug.astype(BF16), cs.astype(BF16)], axis=1)
        h_t = jnp.dot(lhs, rhs, preferred_element_type=F32)
        den = h_t[D_HEAD:D_HEAD + 1, :]
        floor = jnp.exp(-b_row - m_row)
        h_ref[hh, pl.ds(c, 1)] = (h_t[:D_HEAD] / jnp.maximum(jnp.abs(den), floor))[None]
        m_last = m_row[:, last:last + 1]
        wv_t = (v_t_aug * jnp.exp(a_row - m_last)).astype(BF16)
        cst_ref[hh, d] = jnp.exp(m_prev - m_last) * cs + jnp.dot(wv_t, k, preferred_element_type=F32)
        mst_ref[hh, d] = jnp.broadcast_to(b_row[:, last:last + 1] + m_last, (SUBLANES, LANES))

    def body(c, carry):
        for hh in range(heads):
            one_chunk(c, 0, s_i <= t_i, CHUNK - 1, hf_ref, hh)
            one_chunk(nc - 1 - c, 1, s_i >= t_i, 0, hb_ref, hh)
        return carry

    lax.fori_loop(0, nc, body, 0, unroll=min(8 // heads, nc))
    cfin_ref[0] = cst_ref[...]
    mfin_ref[0] = mst_ref[...]

    def readout(c, carry):
        r0 = pl.multiple_of(c * CHUNK, CHUNK)
        for hh in range(heads):
            lanes = slice(hh * D_HEAD, (hh + 1) * D_HEAD)
            h_t = hf_ref[hh, pl.ds(c, 1)][0] + hb_ref[hh, pl.ds(c, 1)][0]
            hc = h_t - jnp.mean(h_t, axis=0, keepdims=True)
            hn = (hc * lax.rsqrt(jnp.mean(hc * hc, axis=0, keepdims=True) + LN_EPS)).T
            gate = jax.nn.sigmoid(o_ref[0, pl.ds(r0, CHUNK), lanes])
            out_ref[0, pl.ds(r0, CHUNK), lanes] = (hn * hg_ref[:, lanes] * gate).astype(out_ref.dtype)
        return carry

    lax.fori_loop(0, nc, readout, 0, unroll=min(4, nc))


def _mlstm(qk, y3d, rowv, c0, m0, head_g):
    b, n, _ = qk.shape
    nc = n // CHUNK
    heads = N_HEADS if nc <= 4 else 1
    w = heads * D_HEAD
    v_blk = COL_V // w
    o_blk = COL_O // w
    tok = lambda off: pl.BlockSpec((1, n, w), lambda i, h: (i, 0, off + h))
    state_c = pl.BlockSpec((1, heads, 2, STATE_ROWS, D_HEAD), lambda i, h: (i, h, 0, 0, 0))
    state_m = pl.BlockSpec((1, heads, 2, SUBLANES, LANES), lambda i, h: (i, h, 0, 0, 0))
    return pl.pallas_call(
        functools.partial(_mlstm_kernel, nc=nc, heads=heads),
        out_shape=(jax.ShapeDtypeStruct((b, n, D_MLSTM), BF16),
                   jax.ShapeDtypeStruct((b, N_HEADS, 2, STATE_ROWS, D_HEAD), F32),
                   jax.ShapeDtypeStruct((b, N_HEADS, 2, SUBLANES, LANES), F32)),
        grid=(b, N_HEADS // heads),
        in_specs=[tok(0), tok(D_MLSTM // w), tok(v_blk), tok(o_blk),
                  pl.BlockSpec((1, nc, 3, SUBLANES, LANES), lambda i, h: (i, 0, 0, 0, 0)),
                  state_c, state_m,
                  pl.BlockSpec((1, w), lambda i, h: (0, h))],
        out_specs=(pl.BlockSpec((1, n, w), lambda i, h: (i, 0, h)), state_c, state_m),
        scratch_shapes=[pltpu.VMEM((heads, nc, D_HEAD, CHUNK), F32), pltpu.VMEM((heads, nc, D_HEAD, CHUNK), F32),
                        pltpu.VMEM((heads, 2, STATE_ROWS, D_HEAD), F32),
                        pltpu.VMEM((heads, 2, SUBLANES, LANES), F32)],
        compiler_params=_cparams(("arbitrary", "arbitrary")),
        name="mlstm",
    )(qk, qk, y3d, y3d, rowv, c0, m0, head_g)


def _dft_tables():
    r = DFT_R
    idx = np.arange(r)
    cg = np.cos(2 * np.pi * np.outer(idx, idx) / FOURIER_GROUP)
    sg = np.sin(2 * np.pi * np.outer(idx, idx) / FOURIER_GROUP)
    n_grp = D_FOURIER // FOURIER_GROUP
    bdc = np.kron(np.eye(n_grp), cg) / 8.0
    bds = np.kron(np.eye(n_grp), sg) / 8.0
    k1 = idx[:, None]
    a = idx[None, :]
    m1 = np.zeros((r, 2 * r, 2 * r))
    for b2 in range(r):
        th = 2 * np.pi * k1 * (r * a + b2) / (r * r)
        ec, es = np.cos(th), np.sin(th)
        m1[b2] = np.block([[ec, -es], [es, ec]]) / 8.0
    th2 = 2 * np.pi * np.outer(idx, idx) / r
    m2 = np.concatenate([np.cos(th2), -np.sin(th2)], axis=1) / 8.0
    as_bf16 = lambda t: jnp.asarray(t, dtype=F32).astype(BF16)
    return as_bf16(bdc), as_bf16(bds), as_bf16(m1), as_bf16(m2)


FOUR_BB = 8


N_SLABS = D_FOURIER // LANES


def _to_slabs(slab_ref, val):
    for s in range(N_SLABS):
        slab_ref[s] = val[:, s * LANES:(s + 1) * LANES]


def _strided_rows(slab_ref, start):
    return jnp.concatenate(
        [slab_ref[s, pl.ds(start, DFT_R, stride=FOUR_BB), :] for s in range(N_SLABS)], axis=1)


def _four_kernel(f_ref, bdc_ref, bds_ref, m1_ref, m2_ref, o_ref, z_ref, ps_ref, qs_ref, zs_ref, os_ref):
    r = DFT_R

    def stage1(blk, carry):
        b0 = pl.multiple_of(blk * FOUR_BB, FOUR_BB)
        fb = f_ref[0, :, pl.ds(b0, FOUR_BB), :].reshape(r * FOUR_BB, D_FOURIER).astype(BF16)
        _to_slabs(ps_ref, jnp.dot(fb, bdc_ref[...], preferred_element_type=F32))
        _to_slabs(qs_ref, jnp.dot(fb, bds_ref[...], preferred_element_type=F32))
        for j in range(FOUR_BB):
            x2 = jnp.concatenate([_strided_rows(ps_ref, j), _strided_rows(qs_ref, j)], axis=0).astype(BF16)
            zz = jnp.dot(m1_ref[b0 + j], x2, preferred_element_type=F32)
            z_ref[b0 + j] = _pack_pair(zz[:r], zz[r:])
        return carry

    lax.fori_loop(0, r // FOUR_BB, stage1, 0)

    def stage2(blk, carry):
        k0 = pl.multiple_of(blk * FOUR_BB, FOUR_BB)
        _to_slabs(zs_ref, z_ref[:, pl.ds(k0, FOUR_BB), :].reshape(r * FOUR_BB, D_FOURIER))
        for j in range(FOUR_BB):
            zc, zs = _unpack_pair_bf16(_strided_rows(zs_ref, j))
            o = jnp.dot(m2_ref[...], jnp.concatenate([zc, zs], axis=0), preferred_element_type=F32)
            for s in range(N_SLABS):
                os_ref[s, pl.ds(j, r, stride=FOUR_BB), :] = o[:, s * LANES:(s + 1) * LANES]
        for s in range(N_SLABS):
            o_ref[0, :, pl.ds(k0, FOUR_BB), s * LANES:(s + 1) * LANES] = os_ref[s].reshape(r, FOUR_BB, LANES)
        return carry

    lax.fori_loop(0, r // FOUR_BB, stage2, 0)


def _fourier(y4, tables):
    bdc, bds, m1, m2 = tables
    b = y4.shape[0]
    r = DFT_R
    slab = lambda dt: pltpu.VMEM((N_SLABS, r * FOUR_BB, LANES), dt)
    const = lambda *shape: pl.BlockSpec(shape, lambda i: (0,) * len(shape))
    return pl.pallas_call(
        _four_kernel,
        out_shape=jax.ShapeDtypeStruct((b, r, r, D_FOURIER), F32),
        grid=(b,),
        in_specs=[pl.BlockSpec((1, r, r, D_FOURIER), lambda i: (i, 0, 0, COL_F // D_FOURIER)),
                  const(D_FOURIER, D_FOURIER), const(D_FOURIER, D_FOURIER),
                  const(r, 2 * r, 2 * r), const(r, 2 * r)],
        out_specs=pl.BlockSpec((1, r, r, D_FOURIER), lambda i: (i, 0, 0, 0)),
        scratch_shapes=[pltpu.VMEM((r, r, D_FOURIER), jnp.int32), slab(F32), slab(F32), slab(jnp.int32), slab(F32)],
        compiler_params=_cparams(("arbitrary",), vmem=FOUR_VMEM_LIMIT),
        name="fourier",
    )(y4, bdc, bds, m1, m2)


def _outproj_kernel(x_ref, mr_ref, fo_ref, wm_ref, wf_ref, g1_ref, lg_ref, lb_ref,
                    sh_ref, sc_ref, wr_ref, x1_ref, xm_ref, s_ref):
    half = x_ref.shape[0] // 2
    for h in range(2):
        r = slice(h * half, (h + 1) * half)
        mix = (jnp.dot(mr_ref[r, :], wm_ref[...], preferred_element_type=F32)
               + jnp.dot(fo_ref[r, :].astype(BF16), wf_ref[...], preferred_element_type=F32))
        x1 = _layer_norm(DEEPNORM_ALPHA * x_ref[r, :] + g1_ref[0] * mix) * lg_ref[...] + lb_ref[...]
        x1_ref[r, :] = x1
        xm = (_layer_norm(x1) * (1.0 + sc_ref[0]) + sh_ref[0]).astype(BF16)
        xm_ref[r, :] = _pack_rows(xm)
        logits = jnp.dot(xm, wr_ref[...], preferred_element_type=F32)
        s_ref[r, :] = jax.nn.sigmoid(logits)


def _outproj(x2d, mread, four, w_m, w_f, gate1, ln_g, ln_b, shift2, scale2, w_router, n_per_batch):
    t, d = x2d.shape
    per = n_per_batch // TM
    tokd = lambda w: pl.BlockSpec((TM, w), lambda i: (i, 0))
    full = lambda r, c: pl.BlockSpec((r, c), lambda i: (0, 0))
    mod = pl.BlockSpec((1, 1, d), lambda i: (i // per, 0, 0))
    return pl.pallas_call(
        _outproj_kernel,
        out_shape=(jax.ShapeDtypeStruct((t, d), F32),
                   jax.ShapeDtypeStruct((t, PACK_W), jnp.int32),
                   jax.ShapeDtypeStruct((t, LANES), F32)),
        grid=(t // TM,),
        in_specs=[tokd(d), tokd(D_MLSTM), tokd(D_FOURIER), full(D_MLSTM, d), full(D_FOURIER, d),
                  mod, full(1, d), full(1, d), mod, mod, full(d, LANES)],
        out_specs=(tokd(d), tokd(PACK_W), tokd(LANES)),
        compiler_params=_cparams(("arbitrary",)),
        name="outproj",
    )(x2d, mread, four, w_m, w_f, gate1, ln_g, ln_b, shift2, scale2, w_router)


def _route_kernel(s_ref, b_ref, idx_ref, gw_ref, rank_ref, cnt_ref, carry_ref):
    @pl.when(pl.program_id(0) == 0)
    def _():
        carry_ref[...] = jnp.zeros_like(carry_ref)

    tm = s_ref.shape[0]
    reps = tm // LANES
    s_t = s_ref[...].T[:N_EXPERTS, :]
    e_i = lax.broadcasted_iota(jnp.int32, (N_EXPERTS, tm), 0).astype(F32)
    neg_inf = jnp.float32(-jnp.inf)
    sb = s_t + jnp.tile(b_ref[...], (1, reps))
    picks, sels = [], []
    taken = jnp.zeros((N_EXPERTS, tm), F32)
    for _ in range(TOP_K):
        best = jnp.max(sb, axis=0, keepdims=True)
        pick = jnp.min(jnp.where(sb == best, e_i, float(N_EXPERTS)), axis=0, keepdims=True)
        onehot = e_i == pick
        sels.append(jnp.sum(jnp.where(onehot, s_t, 0.0), axis=0, keepdims=True))
        picks.append(pick)
        sb = jnp.where(onehot, neg_inf, sb)
        taken = taken + onehot.astype(F32)
    u_i = lax.broadcasted_iota(jnp.int32, (tm, tm), 0)
    t_i = lax.broadcasted_iota(jnp.int32, (tm, tm), 1)
    before = (u_i < t_i).astype(BF16)
    taken_bf = taken.astype(BF16)
    rank_all = jnp.tile(carry_ref[...], (1, reps)) + jnp.dot(taken_bf, before, preferred_element_type=F32)
    carry_ref[...] = carry_ref[...] + jnp.dot(taken_bf, jnp.ones((tm, LANES), BF16),
                                              preferred_element_type=F32)
    cnt_ref[...] = carry_ref[...]
    total = sels[0]
    for v in sels[1:]:
        total = total + v
    ranks = [jnp.sum(jnp.where(e_i == picks[j], rank_all, 0.0), axis=0, keepdims=True) for j in range(TOP_K)]
    idx_ref[...] = jnp.concatenate(picks, axis=0).astype(jnp.int32)
    rank_ref[...] = jnp.concatenate(ranks, axis=0).astype(jnp.int32)
    gw_t = jnp.concatenate([ROUTED_SCALE * v / total for v in sels]
                           + [jnp.zeros((LANES - TOP_K, tm), F32)], axis=0)
    gw_ref[...] = gw_t.T[:, :TOP_K]


def _route(scores, bias_rep):
    t = scores.shape[0]
    tok = lambda w: pl.BlockSpec((TM, w), lambda i: (i, 0))
    slot_major = pl.BlockSpec((TOP_K, TM), lambda i: (0, i))
    per_expert = pl.BlockSpec((N_EXPERTS, LANES), lambda i: (0, 0))
    return pl.pallas_call(
        _route_kernel,
        out_shape=(jax.ShapeDtypeStruct((TOP_K, t), jnp.int32),
                   jax.ShapeDtypeStruct((t, TOP_K), F32),
                   jax.ShapeDtypeStruct((TOP_K, t), jnp.int32),
                   jax.ShapeDtypeStruct((N_EXPERTS, LANES), F32)),
        grid=(t // TM,),
        in_specs=[tok(LANES), per_expert],
        out_specs=(slot_major, tok(TOP_K), slot_major, per_expert),
        scratch_shapes=[pltpu.VMEM((N_EXPERTS, LANES), F32)],
        compiler_params=_cparams(("arbitrary",)),
        name="route",
    )(scores, bias_rep)


TN_DEST = 2048


def _dest_kernel(ps_ref, idx_ref, rank_ref, dest_ref):
    idx = idx_ref[...]
    start = jnp.zeros(idx.shape, jnp.int32)
    for e in range(N_EXPERTS):
        start = jnp.where(idx == e, ps_ref[e], start)
    dest_ref[...] = start + rank_ref[...]


def _dest(pad_start, idx_t, rank_t):
    t = idx_t.shape[1]
    blk = lambda: pl.BlockSpec((TOP_K, TN_DEST), lambda i, ps: (0, i))
    grid_spec = pltpu.PrefetchScalarGridSpec(
        num_scalar_prefetch=1, grid=(t // TN_DEST,), in_specs=[blk(), blk()], out_specs=blk())
    return pl.pallas_call(
        _dest_kernel,
        out_shape=jax.ShapeDtypeStruct((TOP_K, t), jnp.int32),
        grid_spec=grid_spec,
        compiler_params=_cparams(("arbitrary",)),
        name="dest",
    )(pad_start, idx_t, rank_t)


def _sc_worker_id(info):
    return lax.axis_index("s") * info.num_cores + lax.axis_index("c")


def _sc_dispatch(xm, dest3, n_rows):
    t, w = xm.shape
    info = plsc.get_sparse_core_info()
    n_workers = info.num_cores * info.num_subcores
    per_worker = t // SC_ROWS // n_workers
    mesh = plsc.VectorSubcoreMesh(core_axis_name="c", subcore_axis_name="s")

    @functools.partial(
        pl.kernel, mesh=mesh,
        out_type=jax.ShapeDtypeStruct((n_rows, w), xm.dtype),
        scratch_types=[pltpu.VMEM((TOP_K, SC_ROWS), jnp.int32), pltpu.VMEM((SC_ROWS, w), xm.dtype)],
        name="sc_dispatch",
    )
    def run(xm_hbm, dest_hbm, xs_hbm, idx_v, rows_v):
        first = _sc_worker_id(info) * per_worker

        @pl.loop(0, per_worker)
        def _(ci):
            chunk = first + ci
            pltpu.sync_copy(dest_hbm.at[chunk], idx_v)
            pltpu.sync_copy(xm_hbm.at[pl.ds(chunk * SC_ROWS, SC_ROWS)], rows_v)
            for j in range(TOP_K):
                pltpu.sync_copy(rows_v, xs_hbm.at[idx_v.at[j]])

    return run(xm, dest3)


def _sc_gather(table, idx2):
    n_chunks, rows = idx2.shape
    w = table.shape[1]
    info = plsc.get_sparse_core_info()
    n_workers = info.num_cores * info.num_subcores
    per_worker = n_chunks // n_workers
    assert per_worker % 2 == 0
    mesh = plsc.VectorSubcoreMesh(core_axis_name="c", subcore_axis_name="s")

    @functools.partial(
        pl.kernel, mesh=mesh,
        out_type=jax.ShapeDtypeStruct((n_chunks * rows, w), table.dtype),
        scratch_types=[pltpu.VMEM((2, rows), jnp.int32), pltpu.VMEM((2, rows, w), table.dtype),
                       pltpu.SemaphoreType.DMA, pltpu.SemaphoreType.DMA],
        name="sc_gather",
    )
    def run(table_hbm, idx_hbm, out_hbm, idx_v, rows_v, sem0, sem1):
        first = _sc_worker_id(info) * per_worker
        sems = (sem0, sem1)

        def gather(b):
            return pltpu.make_async_copy(table_hbm.at[idx_v.at[b]], rows_v.at[b], sems[b])

        def start(chunk, b):
            pltpu.sync_copy(idx_hbm.at[chunk], idx_v.at[b])
            gather(b).start()

        def finish(chunk, b):
            gather(b).wait()
            pltpu.sync_copy(rows_v.at[b], out_hbm.at[pl.ds(chunk * rows, rows)])

        start(first, 0)

        @pl.loop(0, per_worker, step=2)
        def _(ci):
            chunk = first + ci
            start(chunk + 1, 1)
            finish(chunk, 0)

            @pl.when(ci + 2 < per_worker)
            def _():
                start(chunk + 2, 0)

            finish(chunk + 1, 1)

    return run(table, idx2)


def _expert_kernel(first_ref, count_ref, used_ref,
                   xs_hbm, wgu_ref, wd_ref, y_hbm, wgu_bf, wd_bf, xbuf, ybuf, in_sem, out_sem):
    e = pl.program_id(0)
    n_used = used_ref[0]
    first = first_ref[e]
    n_blocks = count_ref[e]

    def rows(g):
        return pl.ds(pl.multiple_of(g * ROW_BLK, ROW_BLK), ROW_BLK)

    def read(g):
        slot = g % EXP_IN_SLOTS
        return pltpu.make_async_copy(xs_hbm.at[rows(g), :], xbuf.at[slot], in_sem.at[slot])

    def write(g):
        slot = g % EXP_OUT_SLOTS
        return pltpu.make_async_copy(ybuf.at[slot], y_hbm.at[rows(g), :], out_sem.at[slot])

    @pl.when(e == 0)
    def _():
        for g0 in range(EXP_AHEAD):
            @pl.when(g0 < n_used)
            def _():
                read(g0).start()

    @pl.when(n_blocks > 0)
    def _():
        wgu_bf[...] = wgu_ref[0].astype(BF16)
        wd_bf[...] = wd_ref[0].astype(BF16)

    def acquire(g):
        @pl.when(g + EXP_AHEAD < n_used)
        def _():
            read(g + EXP_AHEAD).start()

        read(g).wait()

        @pl.when(g >= EXP_OUT_SLOTS)
        def _():
            write(g - EXP_OUT_SLOTS).wait()

    def compute(g):
        in_slot = g % EXP_IN_SLOTS
        out_slot = g % EXP_OUT_SLOTS
        for h in range(ROW_BLK // ROW_HALF):
            r0 = h * ROW_HALF
            x = _unpack_bf16(xbuf[in_slot, r0:r0 + ROW_HALF, :])
            gu = jnp.dot(x, wgu_bf[...], preferred_element_type=F32)
            act = _silu(gu[:, :D_EXPERT]) * gu[:, D_EXPERT:]
            yv = jnp.dot(act.astype(BF16), wd_bf[...], preferred_element_type=F32)
            words = pltpu.pack_elementwise([yv[:, :PACK_W], yv[:, PACK_W:]], packed_dtype=BF16)
            ybuf[out_slot, r0:r0 + ROW_HALF, :] = lax.bitcast_convert_type(words, jnp.int32)

    def pair(k, carry):
        g = first + 2 * k
        acquire(g)
        acquire(g + 1)
        compute(g)
        compute(g + 1)
        write(g).start()
        write(g + 1).start()
        return carry

    lax.fori_loop(0, n_blocks // 2, pair, 0)

    @pl.when(n_blocks % 2 == 1)
    def _():
        g = first + n_blocks - 1
        acquire(g)
        compute(g)
        write(g).start()

    @pl.when(e == pl.num_programs(0) - 1)
    def _():
        for back in range(EXP_OUT_SLOTS, 0, -1):
            @pl.when(n_used >= back)
            def _():
                write(n_used - back).wait()


def _experts(first_block, n_blocks, n_used, xs, w_gu, w_down):
    n_rows, w = xs.shape
    d = D_MODEL
    grid_spec = pltpu.PrefetchScalarGridSpec(
        num_scalar_prefetch=3,
        grid=(N_EXPERTS,),
        in_specs=[pl.BlockSpec(memory_space=pl.ANY),
                  pl.BlockSpec((1, d, 2 * D_EXPERT), lambda e, *_: (e, 0, 0)),
                  pl.BlockSpec((1, D_EXPERT, d), lambda e, *_: (e, 0, 0))],
        out_specs=pl.BlockSpec(memory_space=pl.ANY),
        scratch_shapes=[pltpu.VMEM((d, 2 * D_EXPERT), BF16), pltpu.VMEM((D_EXPERT, d), BF16),
                        pltpu.VMEM((EXP_IN_SLOTS, ROW_BLK, w), jnp.int32),
                        pltpu.VMEM((EXP_OUT_SLOTS, ROW_BLK, w), jnp.int32),
                        pltpu.SemaphoreType.DMA((EXP_IN_SLOTS,)),
                        pltpu.SemaphoreType.DMA((EXP_OUT_SLOTS,))],
    )
    return pl.pallas_call(
        _expert_kernel,
        out_shape=jax.ShapeDtypeStruct((n_rows, w), jnp.int32),
        grid_spec=grid_spec,
        compiler_params=_cparams(("arbitrary",)),
        name="experts",
    )(first_block, n_blocks, n_used, xs, w_gu, w_down)


TM_COMBINE = 512


def _combine_kernel(*refs):
    yg_refs = refs[:TOP_K]
    gw_ref, x1_ref, xm_ref, g2_ref, wsgu_ref, wsd_ref, lg_ref, lb_ref, out_ref = refs[TOP_K:]
    gu = jnp.dot(_unpack_bf16(xm_ref[...]), wsgu_ref[...], preferred_element_type=F32)
    act = _silu(gu[:, :D_EXPERT]) * gu[:, D_EXPERT:]
    acc = jnp.dot(act.astype(BF16), wsd_ref[...], preferred_element_type=F32)
    gw = gw_ref[...]
    for j in range(TOP_K):
        acc = acc + gw[:, j:j + 1] * _unpack_rows(yg_refs[j][...])
    z = DEEPNORM_ALPHA * x1_ref[...] + g2_ref[0] * acc
    out_ref[...] = _layer_norm(z) * lg_ref[...] + lb_ref[...]


def _combine(yg, gw, x1, xm, gate2, w_sgu, w_sd, ln_g, ln_b, n_per_batch):
    t, d = x1.shape
    tm = TM_COMBINE
    per = n_per_batch // tm
    steps = t // tm
    tok = lambda w: pl.BlockSpec((tm, w), lambda i: (i, 0))
    full = lambda r, c: pl.BlockSpec((r, c), lambda i: (0, 0))
    slot = lambda j: pl.BlockSpec((tm, PACK_W), lambda i: (j * steps + i, 0))
    return pl.pallas_call(
        _combine_kernel,
        out_shape=jax.ShapeDtypeStruct((t, d), F32),
        grid=(steps,),
        in_specs=[slot(j) for j in range(TOP_K)]
        + [tok(TOP_K), tok(d), tok(PACK_W), pl.BlockSpec((1, 1, d), lambda i: (i // per, 0, 0)),
           full(d, 2 * D_EXPERT), full(D_EXPERT, d), full(1, d), full(1, d)],
        out_specs=tok(d),
        compiler_params=_cparams(("arbitrary",)),
        name="combine",
    )(*([yg] * TOP_K), gw, x1, xm, gate2, w_sgu, w_sd, ln_g, ln_b)


def _mixer_heads(stream, shift, scale, w_in_k, conv_w9, conv_b, gate_b_row, head_g, c0, m0, grid2d):
    b, n, d = stream.shape
    y = _inproj(stream.reshape(b * n, d), shift, scale, w_in_k, n).reshape(b, n, PROJ_W)
    qk = _conv(y, conv_w9, conv_b, grid2d)
    mread, c_fin, m_fin = _mlstm(qk, y, _gates(y, gate_b_row), c0, m0, head_g)
    return y, mread, c_fin, m_fin


MOE_GROUPS = 1


def _moe_group(xm, x1, scores, bias_row, gate2, w_gu, w_down, w_sgu, w_sd, ln_g, ln_b, n_per_batch):
    t = xm.shape[0]
    n_assign = t * TOP_K
    n_rows = n_assign + N_EXPERTS * ROW_BLK
    idx_t, gw, rank_t, counts = _route(scores, bias_row)
    sizes = counts[:, 0].astype(jnp.int32)
    padded = (sizes + ROW_BLK - 1) // ROW_BLK * ROW_BLK
    pad_end = jnp.cumsum(padded)
    pad_start = (pad_end - padded).astype(jnp.int32)
    dest_t = _dest(pad_start, idx_t, rank_t)
    dest3 = dest_t.reshape(TOP_K, t // SC_ROWS, SC_ROWS).transpose(1, 0, 2)

    xs = _sc_dispatch(xm, dest3, n_rows)
    yb = _experts(pad_start // ROW_BLK, (padded // ROW_BLK).astype(jnp.int32),
                  (pad_end[-1:] // ROW_BLK).astype(jnp.int32), xs, w_gu, w_down)
    yg = _sc_gather(yb, dest_t.reshape(n_assign // SC_GATHER_ROWS, SC_GATHER_ROWS))
    return _combine(yg, gw, x1, xm, gate2, w_sgu, w_sd, ln_g, ln_b, n_per_batch)


def kernel(x, c, ctx, c_ctx, w_ada, b_ada, w_in, conv_w, conv_b, gate_b, head_g, w_out, ln1_g, ln1_b,
           w_router, router_bias, w_expert_gu, w_expert_down, w_shared_gu, w_shared_down, ln2_g, ln2_b):
    bsz, n, d = x.shape
    n_ctx = ctx.shape[1]
    depth = w_ada.shape[0]
    t = bsz * n
    tables = _dft_tables()
    for l in range(depth):
        cvec = jnp.zeros((SUBLANES, d), F32).at[:bsz].set(c).at[bsz].set(c_ctx)
        mod = _ada(cvec, w_ada[l], b_ada[l][None, :])
        mods = mod.reshape(SUBLANES, 6, 1, d)
        mod_x = [mods[:bsz, j] for j in range(6)]
        mod_c = [jnp.broadcast_to(mods[bsz:bsz + 1, j], (bsz, 1, d)) for j in range(6)]

        wl = w_in[l]
        g0 = 4 * D_MLSTM
        w_in_k = jnp.concatenate(
            [wl[:, :g0], wl[:, g0 + N_GATE_COLS:], wl[:, g0:g0 + N_GATE_COLS],
             jnp.zeros((d, LANES - N_GATE_COLS), F32)], axis=1).astype(BF16)
        conv_w9 = conv_w[l].reshape(9, 2 * D_MLSTM)
        conv_b_row = conv_b[l][None, :]
        gate_b_row = jnp.zeros((1, LANES), F32).at[0, :N_GATE_COLS].set(gate_b[l].reshape(-1))
        head_g_row = head_g[l][None, :]

        c0 = jnp.zeros((bsz, N_HEADS, 2, STATE_ROWS, D_HEAD), F32)
        m0 = jnp.zeros((bsz, N_HEADS, 2, SUBLANES, LANES), F32)
        _, _, c_ctx_fin, m_ctx_fin = _mixer_heads(ctx, mod_c[0], mod_c[1], w_in_k, conv_w9, conv_b_row,
                                                  gate_b_row, head_g_row, c0, m0, grid2d=False)
        y, mread, _, _ = _mixer_heads(x, mod_x[0], mod_x[1], w_in_k, conv_w9, conv_b_row,
                                      gate_b_row, head_g_row, c_ctx_fin, m_ctx_fin, grid2d=True)

        four = _fourier(y.reshape(bsz, DFT_R, DFT_R, PROJ_W), tables).reshape(t, D_FOURIER)

        w_o = w_out[l].astype(BF16)
        w_r = jnp.zeros((d, LANES), F32).at[:, :N_EXPERTS].set(w_router[l]).astype(BF16)
        x2d = x.reshape(t, d)
        x1, xm, scores = _outproj(x2d, mread.reshape(t, D_MLSTM), four, w_o[:D_MLSTM], w_o[D_MLSTM:],
                                  mod_x[2], ln1_g[l][None, :], ln1_b[l][None, :], mod_x[3], mod_x[4],
                                  w_r, n)

        bias_row = jnp.broadcast_to(router_bias[l][:, None], (N_EXPERTS, LANES))
        w_sgu = w_shared_gu[l].astype(BF16)
        w_sd = w_shared_down[l].astype(BF16)
        tg = t // MOE_GROUPS
        bg = bsz // MOE_GROUPS
        outs = []
        for grp in range(MOE_GROUPS):
            tok = slice(grp * tg, (grp + 1) * tg)
            outs.append(_moe_group(xm[tok], x1[tok], scores[tok], bias_row, mod_x[5][grp * bg:(grp + 1) * bg],
                                   w_expert_gu[l], w_expert_down[l], w_sgu, w_sd,
                                   ln2_g[l][None, :], ln2_b[l][None, :], n))
        x = jnp.concatenate(outs, axis=0).reshape(bsz, n, d)
        assert depth == 1
    return x
```

```python
import functools
import math

import numpy as np
import jax
import jax.numpy as jnp
from jax import lax
from jax.experimental import pallas as pl
from jax.experimental.pallas import tpu as pltpu
from jax.experimental.pallas import tpu_sc as plsc

F32 = jnp.float32
BF16 = jnp.bfloat16
HIGHEST = lax.Precision.HIGHEST

D_MODEL = 1024
GRID_W = 64
N_HEADS = 4
D_HEAD = 128
D_MLSTM = N_HEADS * D_HEAD
D_FOURIER = 512
FOURIER_GROUP = 64
N_GATE_COLS = 16
CHUNK = 128
N_EXPERTS = 64
TOP_K = 8
D_EXPERT = 256
ROUTED_SCALE = 2.5
DEEPNORM_ALPHA = 2.0 ** 0.25
LN_EPS = 1e-5

LANES = 128
SUBLANES = 8
VMEM_LIMIT = 48 * 1024 * 1024
FOUR_VMEM_LIMIT = 56 * 1024 * 1024

COL_QK = 0
COL_V = 2 * D_MLSTM
COL_O = 3 * D_MLSTM
COL_F = 4 * D_MLSTM
COL_G = COL_F + D_FOURIER
PROJ_W = COL_G + LANES

TM = 512
ROW_BLK = 512
ROW_HALF = 256
EXP_AHEAD = 3
EXP_IN_SLOTS = 8
EXP_OUT_SLOTS = 4
PACK_W = D_MODEL // 2
SC_ROWS = 128
SC_GATHER_ROWS = 64
CONV_PAD = 72
CONV_RB = 256
DFT_R = 64


def _cparams(sem, vmem=VMEM_LIMIT):
    return pltpu.CompilerParams(dimension_semantics=sem, vmem_limit_bytes=vmem)


def _layer_norm(x):
    mu = jnp.mean(x, axis=-1, keepdims=True)
    xc = x - mu
    var = jnp.mean(xc * xc, axis=-1, keepdims=True)
    return xc * lax.rsqrt(var + LN_EPS)


def _silu(x):
    return x * jax.nn.sigmoid(x)


def _pack_pair(lo, hi):
    as_bits = lambda u: lax.bitcast_convert_type(u.astype(BF16).astype(F32), jnp.uint32)
    word = as_bits(hi) | lax.shift_right_logical(as_bits(lo), jnp.uint32(16))
    return lax.bitcast_convert_type(word, jnp.int32)


def _unpack_pair_bf16(w):
    lo = lax.bitcast_convert_type(w.astype(jnp.int16), BF16)
    hi = lax.bitcast_convert_type(lax.shift_right_logical(w, jnp.int32(16)).astype(jnp.int16), BF16)
    return lo, hi


def _pack_rows(v):
    return _pack_pair(v[:, :PACK_W], v[:, PACK_W:])


def _unpack_bf16(w):
    return jnp.concatenate(_unpack_pair_bf16(w), axis=-1)


def _unpack_rows(w):
    u = lax.bitcast_convert_type(w, jnp.uint32)
    lo = lax.bitcast_convert_type(lax.shift_left(u, jnp.uint32(16)), F32)
    hi = lax.bitcast_convert_type(u & jnp.uint32(0xFFFF0000), F32)
    return jnp.concatenate([lo, hi], axis=1)


def _ada_kernel(c_ref, w_ref, b_ref, o_ref):
    s = _silu(c_ref[...])
    o_ref[...] = jnp.dot(s, w_ref[...], precision=HIGHEST, preferred_element_type=F32) + b_ref[...]


def _ada(cvec, w_ada, b_ada):
    rows, d = cvec.shape
    n_out = w_ada.shape[1]
    tn = 1024
    return pl.pallas_call(
        _ada_kernel,
        out_shape=jax.ShapeDtypeStruct((rows, n_out), F32),
        grid=(n_out // tn,),
        in_specs=[pl.BlockSpec((rows, d), lambda j: (0, 0)),
                  pl.BlockSpec((d, tn), lambda j: (0, j)),
                  pl.BlockSpec((1, tn), lambda j: (0, j))],
        out_specs=pl.BlockSpec((rows, tn), lambda j: (0, j)),
        compiler_params=_cparams(("arbitrary",)),
        name="ada",
    )(cvec, w_ada, b_ada)


def _inproj_kernel(x_ref, sh_ref, sc_ref, w_ref, y_ref):
    half = x_ref.shape[0] // 2
    for h in range(2):
        r = slice(h * half, (h + 1) * half)
        u = _layer_norm(x_ref[r, :]) * (1.0 + sc_ref[0]) + sh_ref[0]
        y_ref[r, :] = jnp.dot(u.astype(BF16), w_ref[...], preferred_element_type=F32)


def _inproj(x2d, shift, scale, w, n_per_batch):
    t, d = x2d.shape
    tm = min(TM, n_per_batch)
    per = n_per_batch // tm
    return pl.pallas_call(
        _inproj_kernel,
        out_shape=jax.ShapeDtypeStruct((t, PROJ_W), F32),
        grid=(t // tm,),
        in_specs=[pl.BlockSpec((tm, d), lambda i: (i, 0)),
                  pl.BlockSpec((1, 1, d), lambda i: (i // per, 0, 0)),
                  pl.BlockSpec((1, 1, d), lambda i: (i // per, 0, 0)),
                  pl.BlockSpec((d, PROJ_W), lambda i: (0, 0))],
        out_specs=pl.BlockSpec((tm, PROJ_W), lambda i: (i, 0)),
        compiler_params=_cparams(("arbitrary",)),
        name="inproj",
    )(x2d, shift, scale, w)


def _conv_kernel(y_ref, w_ref, b_ref, o_ref, pad_ref, *, n, grid2d, tiles):
    zeros = jnp.zeros((CONV_PAD, LANES), F32)
    pad_ref[0:CONV_PAD, :] = zeros
    pad_ref[CONV_PAD + n:CONV_PAD + n + CONV_PAD, :] = zeros
    rb = min(CONV_RB, n)
    col = lax.broadcasted_iota(jnp.int32, (rb, LANES), 0) % GRID_W
    not_first = col >= 1
    not_last = col <= GRID_W - 2
    row_taps = (0, 1, 2) if grid2d else (1,)
    for ct in range(tiles):
        lanes = slice(ct * LANES, (ct + 1) * LANES)
        pad_ref[CONV_PAD:CONV_PAD + n, :] = y_ref[0, :, lanes]
        k_scale = jnp.where(pl.program_id(1) * tiles + ct >= N_HEADS, D_HEAD ** -0.5, 1.0).astype(F32)
        bias = b_ref[:, lanes]
        for blk in range(n // rb):
            r0 = CONV_PAD + blk * rb
            acc = jnp.zeros((rb, LANES), F32)
            for dj in range(3):
                part = jnp.zeros((rb, LANES), F32)
                for di in row_taps:
                    off = (di - 1) * GRID_W + (dj - 1)
                    tap = w_ref[di * 3 + dj:di * 3 + dj + 1, lanes]
                    part = part + tap * pad_ref[r0 + off:r0 + off + rb, :]
                if grid2d and dj == 0:
                    part = jnp.where(not_first, part, 0.0)
                if grid2d and dj == 2:
                    part = jnp.where(not_last, part, 0.0)
                acc = acc + part
            val = _silu(acc + bias) * k_scale
            o_ref[0, blk * rb:(blk + 1) * rb, lanes] = val.astype(o_ref.dtype)


def _conv(y3d, conv_w9, conv_b, grid2d):
    b, n, _ = y3d.shape
    n_ct = 2 * D_MLSTM // LANES
    tiles = 1 if grid2d else n_ct
    w = tiles * LANES
    return pl.pallas_call(
        functools.partial(_conv_kernel, n=n, grid2d=grid2d, tiles=tiles),
        out_shape=jax.ShapeDtypeStruct((b, n, 2 * D_MLSTM), BF16),
        grid=(b, n_ct // tiles),
        in_specs=[pl.BlockSpec((1, n, w), lambda i, c: (i, 0, c)),
                  pl.BlockSpec((9, w), lambda i, c: (0, c)),
                  pl.BlockSpec((1, w), lambda i, c: (0, c))],
        out_specs=pl.BlockSpec((1, n, w), lambda i, c: (i, 0, c)),
        scratch_shapes=[pltpu.VMEM((n + 2 * CONV_PAD, LANES), F32)],
        compiler_params=_cparams(("arbitrary", "arbitrary")),
        name="conv2d" if grid2d else "conv1d",
    )(y3d, conv_w9, conv_b)


STATE_ROWS = 2 * D_HEAD
GATE_A, GATE_CM, GATE_B = 0, 1, 2
GATE_GROUP = 8


def _gates_kernel(g_ref, gb_ref, rv_ref, *, nc):
    u_i = lax.broadcasted_iota(jnp.int32, (CHUNK, CHUNK), 0)
    t_i = lax.broadcasted_iota(jnp.int32, (CHUNK, CHUNK), 1)
    prefix = (u_i <= t_i).astype(F32)
    suffix = (u_i >= t_i).astype(F32)
    grp = min(GATE_GROUP, nc)
    rows = grp * SUBLANES
    lane = lax.broadcasted_iota(jnp.int32, (rows, LANES), 1)
    is_fwd = lax.broadcasted_iota(jnp.int32, (rows, LANES), 0) % SUBLANES < N_HEADS
    h = N_HEADS

    def running_max(a):
        fwd, bwd = a, a
        sh = 1
        while sh < CHUNK:
            fwd = jnp.where(lane >= sh, jnp.maximum(fwd, pltpu.roll(fwd, sh, axis=1)), fwd)
            bwd = jnp.where(lane < CHUNK - sh, jnp.maximum(bwd, pltpu.roll(bwd, CHUNK - sh, axis=1)), bwd)
            sh *= 2
        return jnp.where(is_fwd, fwd, bwd)

    def body(i, carry):
        c0 = i * grp
        r0 = pl.multiple_of(c0 * CHUNK, CHUNK)
        pre = g_ref[0, pl.ds(r0, grp * CHUNK), :] + gb_ref[...]
        pre_t = jnp.concatenate(
            [pre[j * CHUNK:(j + 1) * CHUNK].T[:N_GATE_COLS, :] for j in range(grp)], axis=0)
        lf_t = -(jnp.maximum(-pre_t, 0.0) + jnp.log1p(jnp.exp(-jnp.abs(pre_t))))
        cum_f = jnp.dot(lf_t, prefix, precision=HIGHEST, preferred_element_type=F32)
        cum_b = jnp.dot(lf_t, suffix, precision=HIGHEST, preferred_element_type=F32)
        pick = lambda t, j, q: t[j * N_GATE_COLS + q * h:j * N_GATE_COLS + (q + 1) * h]
        b8 = jnp.concatenate([x for j in range(grp) for x in (pick(cum_f, j, 1), pick(cum_b, j, 3))], axis=0)
        li8 = jnp.concatenate([x for j in range(grp) for x in (pick(pre_t, j, 0), pick(pre_t, j, 2))], axis=0)
        a8 = li8 - b8
        as_chunks = lambda t: t.reshape(grp, SUBLANES, LANES)
        rv_ref[0, pl.ds(c0, grp), GATE_A] = as_chunks(a8)
        rv_ref[0, pl.ds(c0, grp), GATE_CM] = as_chunks(running_max(a8))
        rv_ref[0, pl.ds(c0, grp), GATE_B] = as_chunks(b8)
        return carry

    lax.fori_loop(0, nc // grp, body, 0)


def _gates(y3d, gate_b_row):
    b, n, _ = y3d.shape
    nc = n // CHUNK
    return pl.pallas_call(
        functools.partial(_gates_kernel, nc=nc),
        out_shape=jax.ShapeDtypeStruct((b, nc, 3, SUBLANES, LANES), F32),
        grid=(b,),
        in_specs=[pl.BlockSpec((1, n, LANES), lambda i: (i, 0, COL_G // LANES)),
                  pl.BlockSpec((1, LANES), lambda i: (0, 0))],
        out_specs=pl.BlockSpec((1, nc, 3, SUBLANES, LANES), lambda i: (i, 0, 0, 0, 0)),
        compiler_params=_cparams(("arbitrary",)),
        name="gates",
    )(y3d, gate_b_row)


def _mlstm_kernel(q_ref, k_ref, v_ref, o_ref, rv_ref, c0_ref, m0_ref, hg_ref,
                  out_ref, cfin_ref, mfin_ref, hf_ref, hb_ref, cst_ref, mst_ref, *, nc, heads):
    cst_ref[...] = c0_ref[0]
    mst_ref[...] = m0_ref[0]
    s_i = lax.broadcasted_iota(jnp.int32, (CHUNK, CHUNK), 0)
    t_i = lax.broadcasted_iota(jnp.int32, (CHUNK, CHUNK), 1)
    ones_row = (s_i[:STATE_ROWS - D_HEAD] == 0).astype(F32)

    def one_chunk(c, d, mask_t, last, h_ref, hh):
        r0 = pl.multiple_of(c * CHUNK, CHUNK)
        lanes = slice(hh * D_HEAD, (hh + 1) * D_HEAD)
        g = d * N_HEADS + pl.program_id(1) * heads + hh
        gate_row = lambda tbl: rv_ref[0, pl.ds(c, 1), tbl, pl.ds(g, 1), :].reshape(1, LANES)
        a_row, cm_row, b_row = gate_row(GATE_A), gate_row(GATE_CM), gate_row(GATE_B)
        q_t = q_ref[0, pl.ds(r0, CHUNK), lanes].T
        k = k_ref[0, pl.ds(r0, CHUNK), lanes]
        v_t_aug = jnp.concatenate([v_ref[0, pl.ds(r0, CHUNK), lanes].T, ones_row], axis=0)
        m_prev = mst_ref[hh, d, 0:1, :]
        m_row = jnp.maximum(m_prev, cm_row)
        a_bc = jnp.broadcast_to(a_row, (CHUNK, CHUNK)).T
        d_t = jnp.where(mask_t, jnp.exp(a_bc - m_row), 0.0)
        s_t = jnp.dot(k, q_t, preferred_element_type=F32) * d_t
        cs = cst_ref[hh, d]
        rhs = jnp.concatenate(
            [s_t.astype(BF16), (q_t.astype(F32) * jnp.exp(m_prev - m_row)).astype(BF16)], axis=0)
        lhs = jnp.concatenate([v_t_aug.astype(BF16), cs.astype(BF16)], axis=1)
        h_t = jnp.dot(lhs, rhs, preferred_element_type=F32)
        den = h_t[D_HEAD:D_HEAD + 1, :]
        floor = jnp.exp(-b_row - m_row)
        h_ref[hh, pl.ds(c, 1)] = (h_t[:D_HEAD] / jnp.maximum(jnp.abs(den), floor))[None]
        m_last = m_row[:, last:last + 1]
        wv_t = (v_t_aug * jnp.exp(a_row - m_last)).astype(BF16)
        cst_ref[hh, d] = jnp.exp(m_prev - m_last) * cs + jnp.dot(wv_t, k, preferred_element_type=F32)
        mst_ref[hh, d] = jnp.broadcast_to(b_row[:, last:last + 1] + m_last, (SUBLANES, LANES))

    def body(c, carry):
        for hh in range(heads):
            one_chunk(c, 0, s_i <= t_i, CHUNK - 1, hf_ref, hh)
            one_chunk(nc - 1 - c, 1, s_i >= t_i, 0, hb_ref, hh)
        return carry

    lax.fori_loop(0, nc, body, 0, unroll=min(8 // heads, nc))
    cfin_ref[0] = cst_ref[...]
    mfin_ref[0] = mst_ref[...]

    def readout(c, carry):
        r0 = pl.multiple_of(c * CHUNK, CHUNK)
        for hh in range(heads):
            lanes = slice(hh * D_HEAD, (hh + 1) * D_HEAD)
            h_t = hf_ref[hh, pl.ds(c, 1)][0] + hb_ref[hh, pl.ds(c, 1)][0]
            hc = h_t - jnp.mean(h_t, axis=0, keepdims=True)
            hn = (hc * lax.rsqrt(jnp.mean(hc * hc, axis=0, keepdims=True) + LN_EPS)).T
            gate = jax.nn.sigmoid(o_ref[0, pl.ds(r0, CHUNK), lanes])
            out_ref[0, pl.ds(r0, CHUNK), lanes] = (hn * hg_ref[:, lanes] * gate).astype(out_ref.dtype)
        return carry

    lax.fori_loop(0, nc, readout, 0, unroll=min(4, nc))


def _mlstm(qk, y3d, rowv, c0, m0, head_g):
    b, n, _ = qk.shape
    nc = n // CHUNK
    heads = N_HEADS if nc <= 4 else 1
    w = heads * D_HEAD
    v_blk = COL_V // w
    o_blk = COL_O // w
    tok = lambda off: pl.BlockSpec((1, n, w), lambda i, h: (i, 0, off + h))
    state_c = pl.BlockSpec((1, heads, 2, STATE_ROWS, D_HEAD), lambda i, h: (i, h, 0, 0, 0))
    state_m = pl.BlockSpec((1, heads, 2, SUBLANES, LANES), lambda i, h: (i, h, 0, 0, 0))
    return pl.pallas_call(
        functools.partial(_mlstm_kernel, nc=nc, heads=heads),
        out_shape=(jax.ShapeDtypeStruct((b, n, D_MLSTM), BF16),
                   jax.ShapeDtypeStruct((b, N_HEADS, 2, STATE_ROWS, D_HEAD), F32),
                   jax.ShapeDtypeStruct((b, N_HEADS, 2, SUBLANES, LANES), F32)),
        grid=(b, N_HEADS // heads),
        in_specs=[tok(0), tok(D_MLSTM // w), tok(v_blk), tok(o_blk),
                  pl.BlockSpec((1, nc, 3, SUBLANES, LANES), lambda i, h: (i, 0, 0, 0, 0)),
                  state_c, state_m,
                  pl.BlockSpec((1, w), lambda i, h: (0, h))],
        out_specs=(pl.BlockSpec((1, n, w), lambda i, h: (i, 0, h)), state_c, state_m),
        scratch_shapes=[pltpu.VMEM((heads, nc, D_HEAD, CHUNK), F32), pltpu.VMEM((heads, nc, D_HEAD, CHUNK), F32),
                        pltpu.VMEM((heads, 2, STATE_ROWS, D_HEAD), F32),
                        pltpu.VMEM((heads, 2, SUBLANES, LANES), F32)],
        compiler_params=_cparams(("arbitrary", "arbitrary")),
        name="mlstm",
    )(qk, qk, y3d, y3d, rowv, c0, m0, head_g)


def _dft_tables():
    r = DFT_R
    idx = np.arange(r)
    cg = np.cos(2 * np.pi * np.outer(idx, idx) / FOURIER_GROUP)
    sg = np.sin(2 * np.pi * np.outer(idx, idx) / FOURIER_GROUP)
    n_grp = D_FOURIER // FOURIER_GROUP
    bdc = np.kron(np.eye(n_grp), cg) / 8.0
    bds = np.kron(np.eye(n_grp), sg) / 8.0
    k1 = idx[:, None]
    a = idx[None, :]
    m1 = np.zeros((r, 2 * r, 2 * r))
    for b2 in range(r):
        th = 2 * np.pi * k1 * (r * a + b2) / (r * r)
        ec, es = np.cos(th), np.sin(th)
        m1[b2] = np.block([[ec, -es], [es, ec]]) / 8.0
    th2 = 2 * np.pi * np.outer(idx, idx) / r
    m2 = np.concatenate([np.cos(th2), -np.sin(th2)], axis=1) / 8.0
    as_bf16 = lambda t: jnp.asarray(t, dtype=F32).astype(BF16)
    return as_bf16(bdc), as_bf16(bds), as_bf16(m1), as_bf16(m2)


FOUR_BB = 8


N_SLABS = D_FOURIER // LANES


def _to_slabs(slab_ref, val):
    for s in range(N_SLABS):
        slab_ref[s] = val[:, s * LANES:(s + 1) * LANES]


def _strided_rows(slab_ref, start):
    return jnp.concatenate(
        [slab_ref[s, pl.ds(start, DFT_R, stride=FOUR_BB), :] for s in range(N_SLABS)], axis=1)


def _four_kernel(f_ref, bdc_ref, bds_ref, m1_ref, m2_ref, o_ref, z_ref, ps_ref, qs_ref, zs_ref, os_ref):
    r = DFT_R

    def stage1(blk, carry):
        b0 = pl.multiple_of(blk * FOUR_BB, FOUR_BB)
        fb = f_ref[0, :, pl.ds(b0, FOUR_BB), :].reshape(r * FOUR_BB, D_FOURIER).astype(BF16)
        _to_slabs(ps_ref, jnp.dot(fb, bdc_ref[...], preferred_element_type=F32))
        _to_slabs(qs_ref, jnp.dot(fb, bds_ref[...], preferred_element_type=F32))
        for j in range(FOUR_BB):
            x2 = jnp.concatenate([_strided_rows(ps_ref, j), _strided_rows(qs_ref, j)], axis=0).astype(BF16)
            zz = jnp.dot(m1_ref[b0 + j], x2, preferred_element_type=F32)
            z_ref[b0 + j] = _pack_pair(zz[:r], zz[r:])
        return carry

    lax.fori_loop(0, r // FOUR_BB, stage1, 0)

    def stage2(blk, carry):
        k0 = pl.multiple_of(blk * FOUR_BB, FOUR_BB)
        _to_slabs(zs_ref, z_ref[:, pl.ds(k0, FOUR_BB), :].reshape(r * FOUR_BB, D_FOURIER))
        for j in range(FOUR_BB):
            zc, zs = _unpack_pair_bf16(_strided_rows(zs_ref, j))
            o = jnp.dot(m2_ref[...], jnp.concatenate([zc, zs], axis=0), preferred_element_type=F32)
            for s in range(N_SLABS):
                os_ref[s, pl.ds(j, r, stride=FOUR_BB), :] = o[:, s * LANES:(s + 1) * LANES]
        for s in range(N_SLABS):
            o_ref[0, :, pl.ds(k0, FOUR_BB), s * LANES:(s + 1) * LANES] = os_ref[s].reshape(r, FOUR_BB, LANES)
        return carry

    lax.fori_loop(0, r // FOUR_BB, stage2, 0)


def _fourier(y4, tables):
    bdc, bds, m1, m2 = tables
    b = y4.shape[0]
    r = DFT_R
    slab = lambda dt: pltpu.VMEM((N_SLABS, r * FOUR_BB, LANES), dt)
    const = lambda *shape: pl.BlockSpec(shape, lambda i: (0,) * len(shape))
    return pl.pallas_call(
        _four_kernel,
        out_shape=jax.ShapeDtypeStruct((b, r, r, D_FOURIER), F32),
        grid=(b,),
        in_specs=[pl.BlockSpec((1, r, r, D_FOURIER), lambda i: (i, 0, 0, COL_F // D_FOURIER)),
                  const(D_FOURIER, D_FOURIER), const(D_FOURIER, D_FOURIER),
                  const(r, 2 * r, 2 * r), const(r, 2 * r)],
        out_specs=pl.BlockSpec((1, r, r, D_FOURIER), lambda i: (i, 0, 0, 0)),
        scratch_shapes=[pltpu.VMEM((r, r, D_FOURIER), jnp.int32), slab(F32), slab(F32), slab(jnp.int32), slab(F32)],
        compiler_params=_cparams(("arbitrary",), vmem=FOUR_VMEM_LIMIT),
        name="fourier",
    )(y4, bdc, bds, m1, m2)


def _outproj_kernel(x_ref, mr_ref, fo_ref, wm_ref, wf_ref, g1_ref, lg_ref, lb_ref,
                    sh_ref, sc_ref, wr_ref, x1_ref, xm_ref, s_ref):
    half = x_ref.shape[0] // 2
    for h in range(2):
        r = slice(h * half, (h + 1) * half)
        mix = (jnp.dot(mr_ref[r, :], wm_ref[...], preferred_element_type=F32)
               + jnp.dot(fo_ref[r, :].astype(BF16), wf_ref[...], preferred_element_type=F32))
        x1 = _layer_norm(DEEPNORM_ALPHA * x_ref[r, :] + g1_ref[0] * mix) * lg_ref[...] + lb_ref[...]
        x1_ref[r, :] = x1
        xm = (_layer_norm(x1) * (1.0 + sc_ref[0]) + sh_ref[0]).astype(BF16)
        xm_ref[r, :] = _pack_rows(xm)
        logits = jnp.dot(xm, wr_ref[...], preferred_element_type=F32)
        s_ref[r, :] = jax.nn.sigmoid(logits)


def _outproj(x2d, mread, four, w_m, w_f, gate1, ln_g, ln_b, shift2, scale2, w_router, n_per_batch):
    t, d = x2d.shape
    per = n_per_batch // TM
    tokd = lambda w: pl.BlockSpec((TM, w), lambda i: (i, 0))
    full = lambda r, c: pl.BlockSpec((r, c), lambda i: (0, 0))
    mod = pl.BlockSpec((1, 1, d), lambda i: (i // per, 0, 0))
    return pl.pallas_call(
        _outproj_kernel,
        out_shape=(jax.ShapeDtypeStruct((t, d), F32),
                   jax.ShapeDtypeStruct((t, PACK_W), jnp.int32),
                   jax.ShapeDtypeStruct((t, LANES), F32)),
        grid=(t // TM,),
        in_specs=[tokd(d), tokd(D_MLSTM), tokd(D_FOURIER), full(D_MLSTM, d), full(D_FOURIER, d),
                  mod, full(1, d), full(1, d), mod, mod, full(d, LANES)],
        out_specs=(tokd(d), tokd(PACK_W), tokd(LANES)),
        compiler_params=_cparams(("arbitrary",)),
        name="outproj",
    )(x2d, mread, four, w_m, w_f, gate1, ln_g, ln_b, shift2, scale2, w_router)


def _route_kernel(s_ref, b_ref, idx_ref, gw_ref, rank_ref, cnt_ref, carry_ref):
    @pl.when(pl.program_id(0) == 0)
    def _():
        carry_ref[...] = jnp.zeros_like(carry_ref)

    tm = s_ref.shape[0]
    reps = tm // LANES
    s_t = s_ref[...].T[:N_EXPERTS, :]
    e_i = lax.broadcasted_iota(jnp.int32, (N_EXPERTS, tm), 0).astype(F32)
    neg_inf = jnp.float32(-jnp.inf)
    sb = s_t + jnp.tile(b_ref[...], (1, reps))
    picks, sels = [], []
    taken = jnp.zeros((N_EXPERTS, tm), F32)
    for _ in range(TOP_K):
        best = jnp.max(sb, axis=0, keepdims=True)
        pick = jnp.min(jnp.where(sb == best, e_i, float(N_EXPERTS)), axis=0, keepdims=True)
        onehot = e_i == pick
        sels.append(jnp.sum(jnp.where(onehot, s_t, 0.0), axis=0, keepdims=True))
        picks.append(pick)
        sb = jnp.where(onehot, neg_inf, sb)
        taken = taken + onehot.astype(F32)
    u_i = lax.broadcasted_iota(jnp.int32, (tm, tm), 0)
    t_i = lax.broadcasted_iota(jnp.int32, (tm, tm), 1)
    before = (u_i < t_i).astype(BF16)
    taken_bf = taken.astype(BF16)
    rank_all = jnp.tile(carry_ref[...], (1, reps)) + jnp.dot(taken_bf, before, preferred_element_type=F32)
    carry_ref[...] = carry_ref[...] + jnp.dot(taken_bf, jnp.ones((tm, LANES), BF16),
                                              preferred_element_type=F32)
    cnt_ref[...] = carry_ref[...]
    total = sels[0]
    for v in sels[1:]:
        total = total + v
    ranks = [jnp.sum(jnp.where(e_i == picks[j], rank_all, 0.0), axis=0, keepdims=True) for j in range(TOP_K)]
    idx_ref[...] = jnp.concatenate(picks, axis=0).astype(jnp.int32)
    rank_ref[...] = jnp.concatenate(ranks, axis=0).astype(jnp.int32)
    gw_t = jnp.concatenate([ROUTED_SCALE * v / total for v in sels]
                           + [jnp.zeros((LANES - TOP_K, tm), F32)], axis=0)
    gw_ref[...] = gw_t.T[:, :TOP_K]


def _route(scores, bias_rep):
    t = scores.shape[0]
    tok = lambda w: pl.BlockSpec((TM, w), lambda i: (i, 0))
    slot_major = pl.BlockSpec((TOP_K, TM), lambda i: (0, i))
    per_expert = pl.BlockSpec((N_EXPERTS, LANES), lambda i: (0, 0))
    return pl.pallas_call(
        _route_kernel,
        out_shape=(jax.ShapeDtypeStruct((TOP_K, t), jnp.int32),
                   jax.ShapeDtypeStruct((t, TOP_K), F32),
                   jax.ShapeDtypeStruct((TOP_K, t), jnp.int32),
                   jax.ShapeDtypeStruct((N_EXPERTS, LANES), F32)),
        grid=(t // TM,),
        in_specs=[tok(LANES), per_expert],
        out_specs=(slot_major, tok(TOP_K), slot_major, per_expert),
        scratch_shapes=[pltpu.VMEM((N_EXPERTS, LANES), F32)],
        compiler_params=_cparams(("arbitrary",)),
        name="route",
    )(scores, bias_rep)


TN_DEST = 2048


def _dest_kernel(ps_ref, idx_ref, rank_ref, dest_ref):
    idx = idx_ref[...]
    start = jnp.zeros(idx.shape, jnp.int32)
    for e in range(N_EXPERTS):
        start = jnp.where(idx == e, ps_ref[e], start)
    dest_ref[...] = start + rank_ref[...]


def _dest(pad_start, idx_t, rank_t):
    t = idx_t.shape[1]
    blk = lambda: pl.BlockSpec((TOP_K, TN_DEST), lambda i, ps: (0, i))
    grid_spec = pltpu.PrefetchScalarGridSpec(
        num_scalar_prefetch=1, grid=(t // TN_DEST,), in_specs=[blk(), blk()], out_specs=blk())
    return pl.pallas_call(
        _dest_kernel,
        out_shape=jax.ShapeDtypeStruct((TOP_K, t), jnp.int32),
        grid_spec=grid_spec,
        compiler_params=_cparams(("arbitrary",)),
        name="dest",
    )(pad_start, idx_t, rank_t)


def _sc_worker_id(info):
    return lax.axis_index("s") * info.num_cores + lax.axis_index("c")


def _sc_dispatch(xm, dest3, n_rows):
    t, w = xm.shape
    info = plsc.get_sparse_core_info()
    n_workers = info.num_cores * info.num_subcores
    per_worker = t // SC_ROWS // n_workers
    mesh = plsc.VectorSubcoreMesh(core_axis_name="c", subcore_axis_name="s")

    @functools.partial(
        pl.kernel, mesh=mesh,
        out_type=jax.ShapeDtypeStruct((n_rows, w), xm.dtype),
        scratch_types=[pltpu.VMEM((TOP_K, SC_ROWS), jnp.int32), pltpu.VMEM((SC_ROWS, w), xm.dtype)],
        name="sc_dispatch",
    )
    def run(xm_hbm, dest_hbm, xs_hbm, idx_v, rows_v):
        first = _sc_worker_id(info) * per_worker

        @pl.loop(0, per_worker)
        def _(ci):
            chunk = first + ci
            pltpu.sync_copy(dest_hbm.at[chunk], idx_v)
            pltpu.sync_copy(xm_hbm.at[pl.ds(chunk * SC_ROWS, SC_ROWS)], rows_v)
            for j in range(TOP_K):
                pltpu.sync_copy(rows_v, xs_hbm.at[idx_v.at[j]])

    return run(xm, dest3)


def _sc_gather(table, idx2):
    n_chunks, rows = idx2.shape
    w = table.shape[1]
    info = plsc.get_sparse_core_info()
    n_workers = info.num_cores * info.num_subcores
    per_worker = n_chunks // n_workers
    assert per_worker % 2 == 0
    mesh = plsc.VectorSubcoreMesh(core_axis_name="c", subcore_axis_name="s")

    @functools.partial(
        pl.kernel, mesh=mesh,
        out_type=jax.ShapeDtypeStruct((n_chunks * rows, w), table.dtype),
        scratch_types=[pltpu.VMEM((2, rows), jnp.int32), pltpu.VMEM((2, rows, w), table.dtype),
                       pltpu.SemaphoreType.DMA, pltpu.SemaphoreType.DMA],
        name="sc_gather",
    )
    def run(table_hbm, idx_hbm, out_hbm, idx_v, rows_v, sem0, sem1):
        first = _sc_worker_id(info) * per_worker
        sems = (sem0, sem1)

        def gather(b):
            return pltpu.make_async_copy(table_hbm.at[idx_v.at[b]], rows_v.at[b], sems[b])

        def start(chunk, b):
            pltpu.sync_copy(idx_hbm.at[chunk], idx_v.at[b])
            gather(b).start()

        def finish(chunk, b):
            gather(b).wait()
            pltpu.sync_copy(rows_v.at[b], out_hbm.at[pl.ds(chunk * rows, rows)])

        start(first, 0)

        @pl.loop(0, per_worker, step=2)
        def _(ci):
            chunk = first + ci
            start(chunk + 1, 1)
            finish(chunk, 0)

            @pl.when(ci + 2 < per_worker)
            def _():
                start(chunk + 2, 0)

            finish(chunk + 1, 1)

    return run(table, idx2)


def _expert_kernel(first_ref, count_ref, used_ref,
                   xs_hbm, wgu_ref, wd_ref, y_hbm, wgu_bf, wd_bf, xbuf, ybuf, in_sem, out_sem):
    e = pl.program_id(0)
    n_used = used_ref[0]
    first = first_ref[e]
    n_blocks = count_ref[e]

    def rows(g):
        return pl.ds(pl.multiple_of(g * ROW_BLK, ROW_BLK), ROW_BLK)

    def read(g):
        slot = g % EXP_IN_SLOTS
        return pltpu.make_async_copy(xs_hbm.at[rows(g), :], xbuf.at[slot], in_sem.at[slot])

    def write(g):
        slot = g % EXP_OUT_SLOTS
        return pltpu.make_async_copy(ybuf.at[slot], y_hbm.at[rows(g), :], out_sem.at[slot])

    @pl.when(e == 0)
    def _():
        for g0 in range(EXP_AHEAD):
            @pl.when(g0 < n_used)
            def _():
                read(g0).start()

    @pl.when(n_blocks > 0)
    def _():
        wgu_bf[...] = wgu_ref[0].astype(BF16)
        wd_bf[...] = wd_ref[0].astype(BF16)

    def acquire(g):
        @pl.when(g + EXP_AHEAD < n_used)
        def _():
            read(g + EXP_AHEAD).start()

        read(g).wait()

        @pl.when(g >= EXP_OUT_SLOTS)
        def _():
            write(g - EXP_OUT_SLOTS).wait()

    def compute(g):
        in_slot = g % EXP_IN_SLOTS
        out_slot = g % EXP_OUT_SLOTS
        for h in range(ROW_BLK // ROW_HALF):
            r0 = h * ROW_HALF
            x = _unpack_bf16(xbuf[in_slot, r0:r0 + ROW_HALF, :])
            gu = jnp.dot(x, wgu_bf[...], preferred_element_type=F32)
            act = _silu(gu[:, :D_EXPERT]) * gu[:, D_EXPERT:]
            yv = jnp.dot(act.astype(BF16), wd_bf[...], preferred_element_type=F32)
            words = pltpu.pack_elementwise([yv[:, :PACK_W], yv[:, PACK_W:]], packed_dtype=BF16)
            ybuf[out_slot, r0:r0 + ROW_HALF, :] = lax.bitcast_convert_type(words, jnp.int32)

    def pair(k, carry):
        g = first + 2 * k
        acquire(g)
        acquire(g + 1)
        compute(g)
        compute(g + 1)
        write(g).start()
        write(g + 1).start()
        return carry

    lax.fori_loop(0, n_blocks // 2, pair, 0)

    @pl.when(n_blocks % 2 == 1)
    def _():
        g = first + n_blocks - 1
        acquire(g)
        compute(g)
        write(g).start()

    @pl.when(e == pl.num_programs(0) - 1)
    def _():
        for back in range(EXP_OUT_SLOTS, 0, -1):
            @pl.when(n_used >= back)
            def _():
                write(n_used - back).wait()


def _experts(first_block, n_blocks, n_used, xs, w_gu, w_down):
    n_rows, w = xs.shape
    d = D_MODEL
    grid_spec = pltpu.PrefetchScalarGridSpec(
        num_scalar_prefetch=3,
        grid=(N_EXPERTS,),
        in_specs=[pl.BlockSpec(memory_space=pl.ANY),
                  pl.BlockSpec((1, d, 2 * D_EXPERT), lambda e, *_: (e, 0, 0)),
                  pl.BlockSpec((1, D_EXPERT, d), lambda e, *_: (e, 0, 0))],
        out_specs=pl.BlockSpec(memory_space=pl.ANY),
        scratch_shapes=[pltpu.VMEM((d, 2 * D_EXPERT), BF16), pltpu.VMEM((D_EXPERT, d), BF16),
                        pltpu.VMEM((EXP_IN_SLOTS, ROW_BLK, w), jnp.int32),
                        pltpu.VMEM((EXP_OUT_SLOTS, ROW_BLK, w), jnp.int32),
                        pltpu.SemaphoreType.DMA((EXP_IN_SLOTS,)),
                        pltpu.SemaphoreType.DMA((EXP_OUT_SLOTS,))],
    )
    return pl.pallas_call(
        _expert_kernel,
        out_shape=jax.ShapeDtypeStruct((n_rows, w), jnp.int32),
        grid_spec=grid_spec,
        compiler_params=_cparams(("arbitrary",)),
        name="experts",
    )(first_block, n_blocks, n_used, xs, w_gu, w_down)


TM_COMBINE = 512


def _combine_kernel(*refs):
    yg_refs = refs[:TOP_K]
    gw_ref, x1_ref, xm_ref, g2_ref, wsgu_ref, wsd_ref, lg_ref, lb_ref = refs[TOP_K:TOP_K + 8]
    out_ref = refs[-1]
    gu = jnp.dot(_unpack_bf16(xm_ref[...]), wsgu_ref[...], preferred_element_type=F32)
    act = _silu(gu[:, :D_EXPERT]) * gu[:, D_EXPERT:]
    acc = jnp.dot(act.astype(BF16), wsd_ref[...], preferred_element_type=F32)
    gw = gw_ref[...]
    for j in range(TOP_K):
        acc = acc + gw[:, j:j + 1] * _unpack_rows(yg_refs[j][...])
    z = DEEPNORM_ALPHA * x1_ref[...] + g2_ref[0] * acc
    out_ref[...] = _layer_norm(z) * lg_ref[...] + lb_ref[...]


def _combine(yg, tok0, gw, x1, xm, gate2, w_sgu, w_sd, ln_g, ln_b, n_per_batch, prev=None):
    t, d = x1.shape
    tm = TM_COMBINE
    per = n_per_batch // tm
    steps = yg.shape[0] // TOP_K // tm
    off = tok0 // tm
    tok = lambda w: pl.BlockSpec((tm, w), lambda i: (off + i, 0))
    full = lambda r, c: pl.BlockSpec((r, c), lambda i: (0, 0))
    slot = lambda j: pl.BlockSpec((tm, PACK_W), lambda i: (j * steps + i, 0))
    in_specs = ([slot(j) for j in range(TOP_K)]
                + [tok(TOP_K), tok(d), tok(PACK_W), pl.BlockSpec((1, 1, d), lambda i: ((off + i) // per, 0, 0)),
                   full(d, 2 * D_EXPERT), full(D_EXPERT, d), full(1, d), full(1, d)])
    args = [yg] * TOP_K + [gw, x1, xm, gate2, w_sgu, w_sd, ln_g, ln_b]
    aliases = {}
    if prev is not None:
        in_specs.append(pl.BlockSpec(memory_space=pl.ANY))
        args.append(prev)
        aliases = {len(args) - 1: 0}
    return pl.pallas_call(
        _combine_kernel,
        out_shape=jax.ShapeDtypeStruct((t, d), F32),
        grid=(steps,),
        in_specs=in_specs,
        out_specs=tok(d),
        input_output_aliases=aliases,
        compiler_params=_cparams(("arbitrary",)),
        name="combine",
    )(*args)


def _mixer_heads(stream, shift, scale, w_in_k, conv_w9, conv_b, gate_b_row, head_g, c0, m0, grid2d):
    b, n, d = stream.shape
    y = _inproj(stream.reshape(b * n, d), shift, scale, w_in_k, n).reshape(b, n, PROJ_W)
    qk = _conv(y, conv_w9, conv_b, grid2d)
    mread, c_fin, m_fin = _mlstm(qk, y, _gates(y, gate_b_row), c0, m0, head_g)
    return y, mread, c_fin, m_fin


MOE_GROUPS = 1
COMBINE_PARTS = 2


def _moe_group(xm, x1, scores, bias_row, gate2, w_gu, w_down, w_sgu, w_sd, ln_g, ln_b, n_per_batch):
    t = xm.shape[0]
    n_assign = t * TOP_K
    n_rows = n_assign + N_EXPERTS * ROW_BLK
    idx_t, gw, rank_t, counts = _route(scores, bias_row)
    sizes = counts[:, 0].astype(jnp.int32)
    padded = (sizes + ROW_BLK - 1) // ROW_BLK * ROW_BLK
    pad_end = jnp.cumsum(padded)
    pad_start = (pad_end - padded).astype(jnp.int32)
    dest_t = _dest(pad_start, idx_t, rank_t)
    dest3 = dest_t.reshape(TOP_K, t // SC_ROWS, SC_ROWS).transpose(1, 0, 2)

    xs = _sc_dispatch(xm, dest3, n_rows)
    yb = _experts(pad_start // ROW_BLK, (padded // ROW_BLK).astype(jnp.int32),
                  (pad_end[-1:] // ROW_BLK).astype(jnp.int32), xs, w_gu, w_down)
    part = t // COMBINE_PARTS
    out = None
    for p in range(COMBINE_PARTS):
        idx = dest_t[:, p * part:(p + 1) * part].reshape(part * TOP_K // SC_GATHER_ROWS, SC_GATHER_ROWS)
        out = _combine(_sc_gather(yb, idx), p * part, gw, x1, xm, gate2, w_sgu, w_sd, ln_g, ln_b,
                       n_per_batch, prev=out)
    return out


def kernel(x, c, ctx, c_ctx, w_ada, b_ada, w_in, conv_w, conv_b, gate_b, head_g, w_out, ln1_g, ln1_b,
           w_router, router_bias, w_expert_gu, w_expert_down, w_shared_gu, w_shared_down, ln2_g, ln2_b):
    bsz, n, d = x.shape
    n_ctx = ctx.shape[1]
    depth = w_ada.shape[0]
    t = bsz * n
    tables = _dft_tables()
    for l in range(depth):
        cvec = jnp.zeros((SUBLANES, d), F32).at[:bsz].set(c).at[bsz].set(c_ctx)
        mod = _ada(cvec, w_ada[l], b_ada[l][None, :])
        mods = mod.reshape(SUBLANES, 6, 1, d)
        mod_x = [mods[:bsz, j] for j in range(6)]
        mod_c = [jnp.broadcast_to(mods[bsz:bsz + 1, j], (bsz, 1, d)) for j in range(6)]

        wl = w_in[l]
        g0 = 4 * D_MLSTM
        w_in_k = jnp.concatenate(
            [wl[:, :g0], wl[:, g0 + N_GATE_COLS:], wl[:, g0:g0 + N_GATE_COLS],
             jnp.zeros((d, LANES - N_GATE_COLS), F32)], axis=1).astype(BF16)
        conv_w9 = conv_w[l].reshape(9, 2 * D_MLSTM)
        conv_b_row = conv_b[l][None, :]
        gate_b_row = jnp.zeros((1, LANES), F32).at[0, :N_GATE_COLS].set(gate_b[l].reshape(-1))
        head_g_row = head_g[l][None, :]

        c0 = jnp.zeros((bsz, N_HEADS, 2, STATE_ROWS, D_HEAD), F32)
        m0 = jnp.zeros((bsz, N_HEADS, 2, SUBLANES, LANES), F32)
        _, _, c_ctx_fin, m_ctx_fin = _mixer_heads(ctx, mod_c[0], mod_c[1], w_in_k, conv_w9, conv_b_row,
                                                  gate_b_row, head_g_row, c0, m0, grid2d=False)
        y, mread, _, _ = _mixer_heads(x, mod_x[0], mod_x[1], w_in_k, conv_w9, conv_b_row,
                                      gate_b_row, head_g_row, c_ctx_fin, m_ctx_fin, grid2d=True)

        four = _fourier(y.reshape(bsz, DFT_R, DFT_R, PROJ_W), tables).reshape(t, D_FOURIER)

        w_o = w_out[l].astype(BF16)
        w_r = jnp.zeros((d, LANES), F32).at[:, :N_EXPERTS].set(w_router[l]).astype(BF16)
        x2d = x.reshape(t, d)
        x1, xm, scores = _outproj(x2d, mread.reshape(t, D_MLSTM), four, w_o[:D_MLSTM], w_o[D_MLSTM:],
                                  mod_x[2], ln1_g[l][None, :], ln1_b[l][None, :], mod_x[3], mod_x[4],
                                  w_r, n)

        bias_row = jnp.broadcast_to(router_bias[l][:, None], (N_EXPERTS, LANES))
        w_sgu = w_shared_gu[l].astype(BF16)
        w_sd = w_shared_down[l].astype(BF16)
        tg = t // MOE_GROUPS
        bg = bsz // MOE_GROUPS
        outs = []
        for grp in range(MOE_GROUPS):
            tok = slice(grp * tg, (grp + 1) * tg)
            outs.append(_moe_group(xm[tok], x1[tok], scores[tok], bias_row, mod_x[5][grp * bg:(grp + 1) * bg],
                                   w_expert_gu[l], w_expert_down[l], w_sgu, w_sd,
                                   ln2_g[l][None, :], ln2_b[l][None, :], n))
        x = jnp.concatenate(outs, axis=0).reshape(bsz, n, d)
        assert depth == 1
    return x
```

```python
import functools

import numpy as np
import jax
import jax.numpy as jnp
from jax import lax
from jax.experimental import pallas as pl
from jax.experimental.pallas import tpu as pltpu
from jax.experimental.pallas import tpu_sc as plsc

F32 = jnp.float32
BF16 = jnp.bfloat16
HIGHEST = lax.Precision.HIGHEST

D_MODEL = 1024
GRID_W = 64
N_HEADS = 4
D_HEAD = 128
D_MLSTM = N_HEADS * D_HEAD
D_FOURIER = 512
FOURIER_GROUP = 64
N_GATE_COLS = 16
CHUNK = 128
N_EXPERTS = 64
TOP_K = 8
D_EXPERT = 256
ROUTED_SCALE = 2.5
DEEPNORM_ALPHA = 2.0 ** 0.25
LN_EPS = 1e-5

LANES = 128
SUBLANES = 8
VMEM_LIMIT = 48 * 1024 * 1024
FOUR_VMEM_LIMIT = 56 * 1024 * 1024

COL_QK = 0
COL_V = 2 * D_MLSTM
COL_O = 3 * D_MLSTM
COL_F = 4 * D_MLSTM
COL_G = COL_F + D_FOURIER
PROJ_W = COL_G + LANES

TM = 512
ROW_BLK = 512
ROW_HALF = 256
EXP_AHEAD = 5
EXP_IN_SLOTS = 8
EXP_OUT_SLOTS = 4
PACK_W = D_MODEL // 2
SC_ROWS = 128
SC_GATHER_ROWS = 64
CONV_PAD = 72
CONV_RB = 256
DFT_R = 64


def _cparams(sem, vmem=VMEM_LIMIT):
    return pltpu.CompilerParams(dimension_semantics=sem, vmem_limit_bytes=vmem)


def _layer_norm(x):
    mu = jnp.mean(x, axis=-1, keepdims=True)
    xc = x - mu
    var = jnp.mean(xc * xc, axis=-1, keepdims=True)
    return xc * lax.rsqrt(var + LN_EPS)


def _silu(x):
    return x * jax.nn.sigmoid(x)


def _pack_pair(lo, hi):
    as_bits = lambda u: lax.bitcast_convert_type(u.astype(BF16).astype(F32), jnp.uint32)
    word = as_bits(hi) | lax.shift_right_logical(as_bits(lo), jnp.uint32(16))
    return lax.bitcast_convert_type(word, jnp.int32)


def _unpack_pair_bf16(w):
    lo = lax.bitcast_convert_type(w.astype(jnp.int16), BF16)
    hi = lax.bitcast_convert_type(lax.shift_right_logical(w, jnp.int32(16)).astype(jnp.int16), BF16)
    return lo, hi


def _pack_rows(v):
    return _pack_pair(v[:, :PACK_W], v[:, PACK_W:])


def _unpack_bf16(w):
    return jnp.concatenate(_unpack_pair_bf16(w), axis=-1)


def _unpack_rows(w):
    u = lax.bitcast_convert_type(w, jnp.uint32)
    lo = lax.bitcast_convert_type(lax.shift_left(u, jnp.uint32(16)), F32)
    hi = lax.bitcast_convert_type(u & jnp.uint32(0xFFFF0000), F32)
    return jnp.concatenate([lo, hi], axis=1)


def _ada_kernel(c_ref, w_ref, b_ref, o_ref):
    s = _silu(c_ref[...])
    o_ref[...] = jnp.dot(s, w_ref[...], precision=HIGHEST, preferred_element_type=F32) + b_ref[...]


def _ada(cvec, w_ada, b_ada):
    rows, d = cvec.shape
    n_out = w_ada.shape[1]
    tn = 2048
    return pl.pallas_call(
        _ada_kernel,
        out_shape=jax.ShapeDtypeStruct((rows, n_out), F32),
        grid=(n_out // tn,),
        in_specs=[pl.BlockSpec((rows, d), lambda j: (0, 0)),
                  pl.BlockSpec((d, tn), lambda j: (0, j)),
                  pl.BlockSpec((1, tn), lambda j: (0, j))],
        out_specs=pl.BlockSpec((rows, tn), lambda j: (0, j)),
        compiler_params=_cparams(("arbitrary",)),
        name="ada",
    )(cvec, w_ada, b_ada)


def _inproj_kernel(x_ref, sh_ref, sc_ref, w_ref, y_ref):
    half = x_ref.shape[0] // 2
    for h in range(2):
        r = slice(h * half, (h + 1) * half)
        u = _layer_norm(x_ref[r, :]) * (1.0 + sc_ref[0]) + sh_ref[0]
        y_ref[r, :] = jnp.dot(u.astype(BF16), w_ref[...], preferred_element_type=F32)


def _inproj(x2d, shift, scale, w, n_per_batch):
    t, d = x2d.shape
    tm = min(TM, n_per_batch)
    per = n_per_batch // tm
    return pl.pallas_call(
        _inproj_kernel,
        out_shape=jax.ShapeDtypeStruct((t, PROJ_W), F32),
        grid=(t // tm,),
        in_specs=[pl.BlockSpec((tm, d), lambda i: (i, 0)),
                  pl.BlockSpec((1, 1, d), lambda i: (i // per, 0, 0)),
                  pl.BlockSpec((1, 1, d), lambda i: (i // per, 0, 0)),
                  pl.BlockSpec((d, PROJ_W), lambda i: (0, 0))],
        out_specs=pl.BlockSpec((tm, PROJ_W), lambda i: (i, 0)),
        compiler_params=_cparams(("arbitrary",)),
        name="inproj",
    )(x2d, shift, scale, w)


def _conv_kernel(y_ref, w_ref, b_ref, o_ref, pad_ref, *, n, grid2d, tiles):
    zeros = jnp.zeros((CONV_PAD, LANES), F32)
    pad_ref[0:CONV_PAD, :] = zeros
    pad_ref[CONV_PAD + n:CONV_PAD + n + CONV_PAD, :] = zeros
    rb = min(CONV_RB, n)
    col = lax.broadcasted_iota(jnp.int32, (rb, LANES), 0) % GRID_W
    not_first = col >= 1
    not_last = col <= GRID_W - 2
    row_taps = (0, 1, 2) if grid2d else (1,)
    for ct in range(tiles):
        lanes = slice(ct * LANES, (ct + 1) * LANES)
        pad_ref[CONV_PAD:CONV_PAD + n, :] = y_ref[0, :, lanes]
        k_scale = jnp.where(pl.program_id(1) * tiles + ct >= N_HEADS, D_HEAD ** -0.5, 1.0).astype(F32)
        bias = b_ref[:, lanes]
        for blk in range(n // rb):
            r0 = CONV_PAD + blk * rb
            acc = jnp.zeros((rb, LANES), F32)
            for dj in range(3):
                part = jnp.zeros((rb, LANES), F32)
                for di in row_taps:
                    off = (di - 1) * GRID_W + (dj - 1)
                    tap = w_ref[di * 3 + dj:di * 3 + dj + 1, lanes]
                    part = part + tap * pad_ref[r0 + off:r0 + off + rb, :]
                if grid2d and dj == 0:
                    part = jnp.where(not_first, part, 0.0)
                if grid2d and dj == 2:
                    part = jnp.where(not_last, part, 0.0)
                acc = acc + part
            val = _silu(acc + bias) * k_scale
            o_ref[0, blk * rb:(blk + 1) * rb, lanes] = val.astype(o_ref.dtype)


def _conv(y3d, conv_w9, conv_b, grid2d):
    b, n, _ = y3d.shape
    n_ct = 2 * D_MLSTM // LANES
    tiles = 1 if grid2d else n_ct
    w = tiles * LANES
    return pl.pallas_call(
        functools.partial(_conv_kernel, n=n, grid2d=grid2d, tiles=tiles),
        out_shape=jax.ShapeDtypeStruct((b, n, 2 * D_MLSTM), BF16),
        grid=(b, n_ct // tiles),
        in_specs=[pl.BlockSpec((1, n, w), lambda i, c: (i, 0, c)),
                  pl.BlockSpec((9, w), lambda i, c: (0, c)),
                  pl.BlockSpec((1, w), lambda i, c: (0, c))],
        out_specs=pl.BlockSpec((1, n, w), lambda i, c: (i, 0, c)),
        scratch_shapes=[pltpu.VMEM((n + 2 * CONV_PAD, LANES), F32)],
        compiler_params=_cparams(("arbitrary", "arbitrary")),
        name="conv2d" if grid2d else "conv1d",
    )(y3d, conv_w9, conv_b)


STATE_ROWS = 2 * D_HEAD
GATE_A, GATE_CM, GATE_B = 0, 1, 2
GATE_GROUP = 8


def _gates_kernel(g_ref, gb_ref, rv_ref, *, nc):
    u_i = lax.broadcasted_iota(jnp.int32, (CHUNK, CHUNK), 0)
    t_i = lax.broadcasted_iota(jnp.int32, (CHUNK, CHUNK), 1)
    prefix = (u_i <= t_i).astype(F32)
    suffix = (u_i >= t_i).astype(F32)
    grp = min(GATE_GROUP, nc)
    rows = grp * SUBLANES
    lane = lax.broadcasted_iota(jnp.int32, (rows, LANES), 1)
    is_fwd = lax.broadcasted_iota(jnp.int32, (rows, LANES), 0) % SUBLANES < N_HEADS
    h = N_HEADS

    def running_max(a):
        fwd, bwd = a, a
        sh = 1
        while sh < CHUNK:
            fwd = jnp.where(lane >= sh, jnp.maximum(fwd, pltpu.roll(fwd, sh, axis=1)), fwd)
            bwd = jnp.where(lane < CHUNK - sh, jnp.maximum(bwd, pltpu.roll(bwd, CHUNK - sh, axis=1)), bwd)
            sh *= 2
        return jnp.where(is_fwd, fwd, bwd)

    def body(i, carry):
        c0 = i * grp
        r0 = pl.multiple_of(c0 * CHUNK, CHUNK)
        pre = g_ref[0, pl.ds(r0, grp * CHUNK), :] + gb_ref[...]
        pre_t = jnp.concatenate(
            [pre[j * CHUNK:(j + 1) * CHUNK].T[:N_GATE_COLS, :] for j in range(grp)], axis=0)
        lf_t = -(jnp.maximum(-pre_t, 0.0) + jnp.log1p(jnp.exp(-jnp.abs(pre_t))))
        cum_f = jnp.dot(lf_t, prefix, precision=HIGHEST, preferred_element_type=F32)
        cum_b = jnp.dot(lf_t, suffix, precision=HIGHEST, preferred_element_type=F32)
        pick = lambda t, j, q: t[j * N_GATE_COLS + q * h:j * N_GATE_COLS + (q + 1) * h]
        b8 = jnp.concatenate([x for j in range(grp) for x in (pick(cum_f, j, 1), pick(cum_b, j, 3))], axis=0)
        li8 = jnp.concatenate([x for j in range(grp) for x in (pick(pre_t, j, 0), pick(pre_t, j, 2))], axis=0)
        a8 = li8 - b8
        as_chunks = lambda t: t.reshape(grp, SUBLANES, LANES)
        rv_ref[0, pl.ds(c0, grp), GATE_A] = as_chunks(a8)
        rv_ref[0, pl.ds(c0, grp), GATE_CM] = as_chunks(running_max(a8))
        rv_ref[0, pl.ds(c0, grp), GATE_B] = as_chunks(b8)
        return carry

    lax.fori_loop(0, nc // grp, body, 0)


def _gates(y3d, gate_b_row):
    b, n, _ = y3d.shape
    nc = n // CHUNK
    return pl.pallas_call(
        functools.partial(_gates_kernel, nc=nc),
        out_shape=jax.ShapeDtypeStruct((b, nc, 3, SUBLANES, LANES), F32),
        grid=(b,),
        in_specs=[pl.BlockSpec((1, n, LANES), lambda i: (i, 0, COL_G // LANES)),
                  pl.BlockSpec((1, LANES), lambda i: (0, 0))],
        out_specs=pl.BlockSpec((1, nc, 3, SUBLANES, LANES), lambda i: (i, 0, 0, 0, 0)),
        compiler_params=_cparams(("arbitrary",)),
        name="gates",
    )(y3d, gate_b_row)


def _mlstm_kernel(q_ref, k_ref, v_ref, o_ref, rv_ref, c0_ref, m0_ref, hg_ref,
                  out_ref, cfin_ref, mfin_ref, hf_ref, hb_ref, cst_ref, mst_ref, *, nc, heads):
    cst_ref[...] = c0_ref[0]
    mst_ref[...] = m0_ref[0]
    s_i = lax.broadcasted_iota(jnp.int32, (CHUNK, CHUNK), 0)
    t_i = lax.broadcasted_iota(jnp.int32, (CHUNK, CHUNK), 1)
    ones_row = (s_i[:STATE_ROWS - D_HEAD] == 0).astype(F32)

    def one_chunk(c, d, mask_t, last, h_ref, hh):
        r0 = pl.multiple_of(c * CHUNK, CHUNK)
        lanes = slice(hh * D_HEAD, (hh + 1) * D_HEAD)
        g = d * N_HEADS + pl.program_id(1) * heads + hh
        gate_row = lambda tbl: rv_ref[0, pl.ds(c, 1), tbl, pl.ds(g, 1), :].reshape(1, LANES)
        a_row, cm_row, b_row = gate_row(GATE_A), gate_row(GATE_CM), gate_row(GATE_B)
        q_t = q_ref[0, pl.ds(r0, CHUNK), lanes].T
        k = k_ref[0, pl.ds(r0, CHUNK), lanes]
        v_t_aug = jnp.concatenate([v_ref[0, pl.ds(r0, CHUNK), lanes].T, ones_row], axis=0)
        m_prev = mst_ref[hh, d, 0:1, :]
        m_row = jnp.maximum(m_prev, cm_row)
        a_bc = jnp.broadcast_to(a_row, (CHUNK, CHUNK)).T
        d_t = jnp.where(mask_t, jnp.exp(a_bc - m_row), 0.0)
        s_t = jnp.dot(k, q_t, preferred_element_type=F32) * d_t
        cs = cst_ref[hh, d]
        rhs = jnp.concatenate(
            [s_t.astype(BF16), (q_t.astype(F32) * jnp.exp(m_prev - m_row)).astype(BF16)], axis=0)
        lhs = jnp.concatenate([v_t_aug.astype(BF16), cs.astype(BF16)], axis=1)
        h_t = jnp.dot(lhs, rhs, preferred_element_type=F32)
        den = h_t[D_HEAD:D_HEAD + 1, :]
        floor = jnp.exp(-b_row - m_row)
        h_ref[hh, pl.ds(c, 1)] = (h_t[:D_HEAD] / jnp.maximum(jnp.abs(den), floor))[None]
        m_last = m_row[:, last:last + 1]
        wv_t = (v_t_aug * jnp.exp(a_row - m_last)).astype(BF16)
        cst_ref[hh, d] = jnp.exp(m_prev - m_last) * cs + jnp.dot(wv_t, k, preferred_element_type=F32)
        mst_ref[hh, d] = jnp.broadcast_to(b_row[:, last:last + 1] + m_last, (SUBLANES, LANES))

    def body(c, carry):
        for hh in range(heads):
            one_chunk(c, 0, s_i <= t_i, CHUNK - 1, hf_ref, hh)
            one_chunk(nc - 1 - c, 1, s_i >= t_i, 0, hb_ref, hh)
        return carry

    lax.fori_loop(0, nc, body, 0, unroll=min(8 // heads, nc))
    cfin_ref[0] = cst_ref[...]
    mfin_ref[0] = mst_ref[...]

    def readout(c, carry):
        r0 = pl.multiple_of(c * CHUNK, CHUNK)
        for hh in range(heads):
            lanes = slice(hh * D_HEAD, (hh + 1) * D_HEAD)
            h_t = hf_ref[hh, pl.ds(c, 1)][0] + hb_ref[hh, pl.ds(c, 1)][0]
            hc = h_t - jnp.mean(h_t, axis=0, keepdims=True)
            hn = (hc * lax.rsqrt(jnp.mean(hc * hc, axis=0, keepdims=True) + LN_EPS)).T
            gate = jax.nn.sigmoid(o_ref[0, pl.ds(r0, CHUNK), lanes])
            out_ref[0, pl.ds(r0, CHUNK), lanes] = (hn * hg_ref[:, lanes] * gate).astype(out_ref.dtype)
        return carry

    lax.fori_loop(0, nc, readout, 0, unroll=min(4, nc))


def _mlstm(qk, y3d, rowv, c0, m0, head_g):
    b, n, _ = qk.shape
    nc = n // CHUNK
    heads = N_HEADS if nc <= 4 else 1
    w = heads * D_HEAD
    v_blk = COL_V // w
    o_blk = COL_O // w
    tok = lambda off: pl.BlockSpec((1, n, w), lambda i, h: (i, 0, off + h))
    state_c = pl.BlockSpec((1, heads, 2, STATE_ROWS, D_HEAD), lambda i, h: (i, h, 0, 0, 0))
    state_m = pl.BlockSpec((1, heads, 2, SUBLANES, LANES), lambda i, h: (i, h, 0, 0, 0))
    return pl.pallas_call(
        functools.partial(_mlstm_kernel, nc=nc, heads=heads),
        out_shape=(jax.ShapeDtypeStruct((b, n, D_MLSTM), BF16),
                   jax.ShapeDtypeStruct((b, N_HEADS, 2, STATE_ROWS, D_HEAD), F32),
                   jax.ShapeDtypeStruct((b, N_HEADS, 2, SUBLANES, LANES), F32)),
        grid=(b, N_HEADS // heads),
        in_specs=[tok(0), tok(D_MLSTM // w), tok(v_blk), tok(o_blk),
                  pl.BlockSpec((1, nc, 3, SUBLANES, LANES), lambda i, h: (i, 0, 0, 0, 0)),
                  state_c, state_m,
                  pl.BlockSpec((1, w), lambda i, h: (0, h))],
        out_specs=(pl.BlockSpec((1, n, w), lambda i, h: (i, 0, h)), state_c, state_m),
        scratch_shapes=[pltpu.VMEM((heads, nc, D_HEAD, CHUNK), F32), pltpu.VMEM((heads, nc, D_HEAD, CHUNK), F32),
                        pltpu.VMEM((heads, 2, STATE_ROWS, D_HEAD), F32),
                        pltpu.VMEM((heads, 2, SUBLANES, LANES), F32)],
        compiler_params=_cparams(("arbitrary", "arbitrary")),
        name="mlstm",
    )(qk, qk, y3d, y3d, rowv, c0, m0, head_g)


def _dft_tables():
    r = DFT_R
    idx = np.arange(r)
    cg = np.cos(2 * np.pi * np.outer(idx, idx) / FOURIER_GROUP)
    sg = np.sin(2 * np.pi * np.outer(idx, idx) / FOURIER_GROUP)
    n_grp = D_FOURIER // FOURIER_GROUP
    bdc = np.kron(np.eye(n_grp), cg) / 8.0
    bds = np.kron(np.eye(n_grp), sg) / 8.0
    k1 = idx[:, None]
    a = idx[None, :]
    m1 = np.zeros((r, 2 * r, 2 * r))
    for b2 in range(r):
        th = 2 * np.pi * k1 * (r * a + b2) / (r * r)
        ec, es = np.cos(th), np.sin(th)
        m1[b2] = np.block([[ec, -es], [es, ec]]) / 8.0
    th2 = 2 * np.pi * np.outer(idx, idx) / r
    m2 = np.concatenate([np.cos(th2), -np.sin(th2)], axis=1) / 8.0
    as_bf16 = lambda t: jnp.asarray(t, dtype=F32).astype(BF16)
    return as_bf16(bdc), as_bf16(bds), as_bf16(m1), as_bf16(m2)


FOUR_BB = 8


N_SLABS = D_FOURIER // LANES


def _to_slabs(slab_ref, val):
    for s in range(N_SLABS):
        slab_ref[s] = val[:, s * LANES:(s + 1) * LANES]


def _strided_rows(slab_ref, start):
    return jnp.concatenate(
        [slab_ref[s, pl.ds(start, DFT_R, stride=FOUR_BB), :] for s in range(N_SLABS)], axis=1)


def _four_kernel(f_ref, bdc_ref, bds_ref, m1_ref, m2_ref, o_ref, z_ref, ps_ref, qs_ref, zs_ref, os_ref):
    r = DFT_R

    def stage1(blk, carry):
        b0 = pl.multiple_of(blk * FOUR_BB, FOUR_BB)
        fb = f_ref[0, :, pl.ds(b0, FOUR_BB), :].reshape(r * FOUR_BB, D_FOURIER).astype(BF16)
        _to_slabs(ps_ref, jnp.dot(fb, bdc_ref[...], preferred_element_type=F32))
        _to_slabs(qs_ref, jnp.dot(fb, bds_ref[...], preferred_element_type=F32))
        for j in range(FOUR_BB):
            x2 = jnp.concatenate([_strided_rows(ps_ref, j), _strided_rows(qs_ref, j)], axis=0).astype(BF16)
            zz = jnp.dot(m1_ref[b0 + j], x2, preferred_element_type=F32)
            z_ref[b0 + j] = _pack_pair(zz[:r], zz[r:])
        return carry

    lax.fori_loop(0, r // FOUR_BB, stage1, 0)

    def stage2(blk, carry):
        k0 = pl.multiple_of(blk * FOUR_BB, FOUR_BB)
        _to_slabs(zs_ref, z_ref[:, pl.ds(k0, FOUR_BB), :].reshape(r * FOUR_BB, D_FOURIER))
        for j in range(FOUR_BB):
            zc, zs = _unpack_pair_bf16(_strided_rows(zs_ref, j))
            o = jnp.dot(m2_ref[...], jnp.concatenate([zc, zs], axis=0), preferred_element_type=F32)
            for s in range(N_SLABS):
                os_ref[s, pl.ds(j, r, stride=FOUR_BB), :] = o[:, s * LANES:(s + 1) * LANES]
        for s in range(N_SLABS):
            o_ref[0, :, pl.ds(k0, FOUR_BB), s * LANES:(s + 1) * LANES] = os_ref[s].reshape(r, FOUR_BB, LANES)
        return carry

    lax.fori_loop(0, r // FOUR_BB, stage2, 0)


def _fourier(y4, tables):
    bdc, bds, m1, m2 = tables
    b = y4.shape[0]
    r = DFT_R
    slab = lambda dt: pltpu.VMEM((N_SLABS, r * FOUR_BB, LANES), dt)
    const = lambda *shape: pl.BlockSpec(shape, lambda i: (0,) * len(shape))
    return pl.pallas_call(
        _four_kernel,
        out_shape=jax.ShapeDtypeStruct((b, r, r, D_FOURIER), F32),
        grid=(b,),
        in_specs=[pl.BlockSpec((1, r, r, D_FOURIER), lambda i: (i, 0, 0, COL_F // D_FOURIER)),
                  const(D_FOURIER, D_FOURIER), const(D_FOURIER, D_FOURIER),
                  const(r, 2 * r, 2 * r), const(r, 2 * r)],
        out_specs=pl.BlockSpec((1, r, r, D_FOURIER), lambda i: (i, 0, 0, 0)),
        scratch_shapes=[pltpu.VMEM((r, r, D_FOURIER), jnp.int32), slab(F32), slab(F32), slab(jnp.int32), slab(F32)],
        compiler_params=_cparams(("arbitrary",), vmem=FOUR_VMEM_LIMIT),
        name="fourier",
    )(y4, bdc, bds, m1, m2)


def _outproj_kernel(x_ref, mr_ref, fo_ref, wm_ref, wf_ref, g1_ref, lg_ref, lb_ref,
                    sh_ref, sc_ref, wr_ref, x1_ref, xm_ref, s_ref):
    half = x_ref.shape[0] // 2
    for h in range(2):
        r = slice(h * half, (h + 1) * half)
        mix = (jnp.dot(mr_ref[r, :], wm_ref[...], preferred_element_type=F32)
               + jnp.dot(fo_ref[r, :].astype(BF16), wf_ref[...], preferred_element_type=F32))
        x1 = _layer_norm(DEEPNORM_ALPHA * x_ref[r, :] + g1_ref[0] * mix) * lg_ref[...] + lb_ref[...]
        x1_ref[r, :] = x1
        xm = (_layer_norm(x1) * (1.0 + sc_ref[0]) + sh_ref[0]).astype(BF16)
        xm_ref[r, :] = _pack_rows(xm)
        logits = jnp.dot(xm, wr_ref[...], preferred_element_type=F32)
        s_ref[r, :] = jax.nn.sigmoid(logits)


def _outproj(x2d, mread, four, w_m, w_f, gate1, ln_g, ln_b, shift2, scale2, w_router, n_per_batch):
    t, d = x2d.shape
    per = n_per_batch // TM
    tokd = lambda w: pl.BlockSpec((TM, w), lambda i: (i, 0))
    full = lambda r, c: pl.BlockSpec((r, c), lambda i: (0, 0))
    mod = pl.BlockSpec((1, 1, d), lambda i: (i // per, 0, 0))
    return pl.pallas_call(
        _outproj_kernel,
        out_shape=(jax.ShapeDtypeStruct((t, d), F32),
                   jax.ShapeDtypeStruct((t, PACK_W), jnp.int32),
                   jax.ShapeDtypeStruct((t, LANES), F32)),
        grid=(t // TM,),
        in_specs=[tokd(d), tokd(D_MLSTM), tokd(D_FOURIER), full(D_MLSTM, d), full(D_FOURIER, d),
                  mod, full(1, d), full(1, d), mod, mod, full(d, LANES)],
        out_specs=(tokd(d), tokd(PACK_W), tokd(LANES)),
        compiler_params=_cparams(("arbitrary",)),
        name="outproj",
    )(x2d, mread, four, w_m, w_f, gate1, ln_g, ln_b, shift2, scale2, w_router)


def _route_kernel(s_ref, b_ref, idx_ref, gw_ref, rank_ref, cnt_ref, carry_ref):
    @pl.when(pl.program_id(0) == 0)
    def _():
        carry_ref[...] = jnp.zeros_like(carry_ref)

    tm = s_ref.shape[0]
    reps = tm // LANES
    s_t = s_ref[...].T[:N_EXPERTS, :]
    e_i = lax.broadcasted_iota(jnp.int32, (N_EXPERTS, tm), 0).astype(F32)
    neg_inf = jnp.float32(-jnp.inf)
    sb = s_t + jnp.tile(b_ref[...], (1, reps))
    picks, sels = [], []
    taken = jnp.zeros((N_EXPERTS, tm), F32)
    for _ in range(TOP_K):
        best = jnp.max(sb, axis=0, keepdims=True)
        pick = jnp.min(jnp.where(sb == best, e_i, float(N_EXPERTS)), axis=0, keepdims=True)
        onehot = e_i == pick
        sels.append(jnp.sum(jnp.where(onehot, s_t, 0.0), axis=0, keepdims=True))
        picks.append(pick)
        sb = jnp.where(onehot, neg_inf, sb)
        taken = taken + onehot.astype(F32)
    u_i = lax.broadcasted_iota(jnp.int32, (tm, tm), 0)
    t_i = lax.broadcasted_iota(jnp.int32, (tm, tm), 1)
    before = (u_i < t_i).astype(BF16)
    taken_bf = taken.astype(BF16)
    rank_all = jnp.tile(carry_ref[...], (1, reps)) + jnp.dot(taken_bf, before, preferred_element_type=F32)
    carry_ref[...] = carry_ref[...] + jnp.dot(taken_bf, jnp.ones((tm, LANES), BF16),
                                              preferred_element_type=F32)
    cnt_ref[...] = carry_ref[...]
    total = sels[0]
    for v in sels[1:]:
        total = total + v
    ranks = [jnp.sum(jnp.where(e_i == picks[j], rank_all, 0.0), axis=0, keepdims=True) for j in range(TOP_K)]
    idx_ref[...] = jnp.concatenate(picks, axis=0).astype(jnp.int32)
    rank_ref[...] = jnp.concatenate(ranks, axis=0).astype(jnp.int32)
    gw_t = jnp.concatenate([ROUTED_SCALE * v / total for v in sels]
                           + [jnp.zeros((LANES - TOP_K, tm), F32)], axis=0)
    gw_ref[...] = gw_t.T[:, :TOP_K]


def _route(scores, bias_rep):
    t = scores.shape[0]
    tok = lambda w: pl.BlockSpec((TM, w), lambda i: (i, 0))
    slot_major = pl.BlockSpec((TOP_K, TM), lambda i: (0, i))
    per_expert = pl.BlockSpec((N_EXPERTS, LANES), lambda i: (0, 0))
    return pl.pallas_call(
        _route_kernel,
        out_shape=(jax.ShapeDtypeStruct((TOP_K, t), jnp.int32),
                   jax.ShapeDtypeStruct((t, TOP_K), F32),
                   jax.ShapeDtypeStruct((TOP_K, t), jnp.int32),
                   jax.ShapeDtypeStruct((N_EXPERTS, LANES), F32)),
        grid=(t // TM,),
        in_specs=[tok(LANES), per_expert],
        out_specs=(slot_major, tok(TOP_K), slot_major, per_expert),
        scratch_shapes=[pltpu.VMEM((N_EXPERTS, LANES), F32)],
        compiler_params=_cparams(("arbitrary",)),
        name="route",
    )(scores, bias_rep)


TN_DEST = 2048


def _dest_kernel(ps_ref, idx_ref, rank_ref, dest_ref):
    idx = idx_ref[...]
    start = jnp.zeros(idx.shape, jnp.int32)
    for e in range(N_EXPERTS):
        start = jnp.where(idx == e, ps_ref[e], start)
    dest_ref[...] = start + rank_ref[...]


def _dest(pad_start, idx_t, rank_t):
    t = idx_t.shape[1]
    blk = lambda: pl.BlockSpec((TOP_K, TN_DEST), lambda i, ps: (0, i))
    grid_spec = pltpu.PrefetchScalarGridSpec(
        num_scalar_prefetch=1, grid=(t // TN_DEST,), in_specs=[blk(), blk()], out_specs=blk())
    return pl.pallas_call(
        _dest_kernel,
        out_shape=jax.ShapeDtypeStruct((TOP_K, t), jnp.int32),
        grid_spec=grid_spec,
        compiler_params=_cparams(("arbitrary",)),
        name="dest",
    )(pad_start, idx_t, rank_t)


def _sc_worker_id(info):
    return lax.axis_index("s") * info.num_cores + lax.axis_index("c")


def _sc_dispatch(xm, dest3, n_rows):
    t, w = xm.shape
    info = plsc.get_sparse_core_info()
    n_workers = info.num_cores * info.num_subcores
    per_worker = t // SC_ROWS // n_workers
    mesh = plsc.VectorSubcoreMesh(core_axis_name="c", subcore_axis_name="s")

    @functools.partial(
        pl.kernel, mesh=mesh,
        out_type=jax.ShapeDtypeStruct((n_rows, w), xm.dtype),
        scratch_types=[pltpu.VMEM((TOP_K, SC_ROWS), jnp.int32), pltpu.VMEM((SC_ROWS, w), xm.dtype)],
        name="sc_dispatch",
    )
    def run(xm_hbm, dest_hbm, xs_hbm, idx_v, rows_v):
        first = _sc_worker_id(info) * per_worker

        @pl.loop(0, per_worker)
        def _(ci):
            chunk = first + ci
            pltpu.sync_copy(dest_hbm.at[chunk], idx_v)
            pltpu.sync_copy(xm_hbm.at[pl.ds(chunk * SC_ROWS, SC_ROWS)], rows_v)
            for j in range(TOP_K):
                pltpu.sync_copy(rows_v, xs_hbm.at[idx_v.at[j]])

    return run(xm, dest3)


def _sc_gather(table, idx2):
    n_chunks, rows = idx2.shape
    w = table.shape[1]
    info = plsc.get_sparse_core_info()
    n_workers = info.num_cores * info.num_subcores
    per_worker = n_chunks // n_workers
    assert per_worker % 2 == 0
    mesh = plsc.VectorSubcoreMesh(core_axis_name="c", subcore_axis_name="s")

    @functools.partial(
        pl.kernel, mesh=mesh,
        out_type=jax.ShapeDtypeStruct((n_chunks * rows, w), table.dtype),
        scratch_types=[pltpu.VMEM((2, rows), jnp.int32), pltpu.VMEM((2, rows, w), table.dtype),
                       pltpu.SemaphoreType.DMA, pltpu.SemaphoreType.DMA],
        name="sc_gather",
    )
    def run(table_hbm, idx_hbm, out_hbm, idx_v, rows_v, sem0, sem1):
        first = _sc_worker_id(info) * per_worker
        sems = (sem0, sem1)

        def gather(b):
            return pltpu.make_async_copy(table_hbm.at[idx_v.at[b]], rows_v.at[b], sems[b])

        def start(chunk, b):
            pltpu.sync_copy(idx_hbm.at[chunk], idx_v.at[b])
            gather(b).start()

        def finish(chunk, b):
            gather(b).wait()
            pltpu.sync_copy(rows_v.at[b], out_hbm.at[pl.ds(chunk * rows, rows)])

        start(first, 0)

        @pl.loop(0, per_worker, step=2)
        def _(ci):
            chunk = first + ci
            start(chunk + 1, 1)
            finish(chunk, 0)

            @pl.when(ci + 2 < per_worker)
            def _():
                start(chunk + 2, 0)

            finish(chunk + 1, 1)

    return run(table, idx2)


def _expert_kernel(first_ref, count_ref, used_ref,
                   xs_hbm, wgu_ref, wd_ref, y_hbm, wgu_bf, wd_bf, xbuf, ybuf, in_sem, out_sem):
    e = pl.program_id(0)
    n_used = used_ref[0]
    first = first_ref[e]
    n_blocks = count_ref[e]

    def rows(g):
        return pl.ds(pl.multiple_of(g * ROW_BLK, ROW_BLK), ROW_BLK)

    def read(g):
        slot = g % EXP_IN_SLOTS
        return pltpu.make_async_copy(xs_hbm.at[rows(g), :], xbuf.at[slot], in_sem.at[slot])

    def write(g):
        slot = g % EXP_OUT_SLOTS
        return pltpu.make_async_copy(ybuf.at[slot], y_hbm.at[rows(g), :], out_sem.at[slot])

    @pl.when(e == 0)
    def _():
        for g0 in range(EXP_AHEAD):
            @pl.when(g0 < n_used)
            def _():
                read(g0).start()

    @pl.when(n_blocks > 0)
    def _():
        wgu_bf[...] = wgu_ref[0].astype(BF16)
        wd_bf[...] = wd_ref[0].astype(BF16)

    def acquire(g):
        @pl.when(g + EXP_AHEAD < n_used)
        def _():
            read(g + EXP_AHEAD).start()

        read(g).wait()

        @pl.when(g >= EXP_OUT_SLOTS)
        def _():
            write(g - EXP_OUT_SLOTS).wait()

    def compute(g):
        in_slot = g % EXP_IN_SLOTS
        out_slot = g % EXP_OUT_SLOTS
        for h in range(ROW_BLK // ROW_HALF):
            r0 = h * ROW_HALF
            x = _unpack_bf16(xbuf[in_slot, r0:r0 + ROW_HALF, :])
            gu = jnp.dot(x, wgu_bf[...], preferred_element_type=F32)
            act = _silu(gu[:, :D_EXPERT]) * gu[:, D_EXPERT:]
            yv = jnp.dot(act.astype(BF16), wd_bf[...], preferred_element_type=F32)
            words = pltpu.pack_elementwise([yv[:, :PACK_W], yv[:, PACK_W:]], packed_dtype=BF16)
            ybuf[out_slot, r0:r0 + ROW_HALF, :] = lax.bitcast_convert_type(words, jnp.int32)

    def pair(k, carry):
        g = first + 2 * k
        acquire(g)
        acquire(g + 1)
        compute(g)
        compute(g + 1)
        write(g).start()
        write(g + 1).start()
        return carry

    lax.fori_loop(0, n_blocks // 2, pair, 0)

    @pl.when(n_blocks % 2 == 1)
    def _():
        g = first + n_blocks - 1
        acquire(g)
        compute(g)
        write(g).start()

    @pl.when(e == pl.num_programs(0) - 1)
    def _():
        for back in range(EXP_OUT_SLOTS, 0, -1):
            @pl.when(n_used >= back)
            def _():
                write(n_used - back).wait()


def _experts(first_block, n_blocks, n_used, xs, w_gu, w_down):
    n_rows, w = xs.shape
    d = D_MODEL
    grid_spec = pltpu.PrefetchScalarGridSpec(
        num_scalar_prefetch=3,
        grid=(N_EXPERTS,),
        in_specs=[pl.BlockSpec(memory_space=pl.ANY),
                  pl.BlockSpec((1, d, 2 * D_EXPERT), lambda e, *_: (e, 0, 0)),
                  pl.BlockSpec((1, D_EXPERT, d), lambda e, *_: (e, 0, 0))],
        out_specs=pl.BlockSpec(memory_space=pl.ANY),
        scratch_shapes=[pltpu.VMEM((d, 2 * D_EXPERT), BF16), pltpu.VMEM((D_EXPERT, d), BF16),
                        pltpu.VMEM((EXP_IN_SLOTS, ROW_BLK, w), jnp.int32),
                        pltpu.VMEM((EXP_OUT_SLOTS, ROW_BLK, w), jnp.int32),
                        pltpu.SemaphoreType.DMA((EXP_IN_SLOTS,)),
                        pltpu.SemaphoreType.DMA((EXP_OUT_SLOTS,))],
    )
    return pl.pallas_call(
        _expert_kernel,
        out_shape=jax.ShapeDtypeStruct((n_rows, w), jnp.int32),
        grid_spec=grid_spec,
        compiler_params=_cparams(("arbitrary",)),
        name="experts",
    )(first_block, n_blocks, n_used, xs, w_gu, w_down)


TM_COMBINE = 512


def _combine_kernel(*refs):
    yg_refs = refs[:TOP_K]
    gw_ref, x1_ref, xm_ref, g2_ref, wsgu_ref, wsd_ref, lg_ref, lb_ref, out_ref = refs[TOP_K:]
    gu = jnp.dot(_unpack_bf16(xm_ref[...]), wsgu_ref[...], preferred_element_type=F32)
    act = _silu(gu[:, :D_EXPERT]) * gu[:, D_EXPERT:]
    acc = jnp.dot(act.astype(BF16), wsd_ref[...], preferred_element_type=F32)
    gw = gw_ref[...]
    for j in range(TOP_K):
        acc = acc + gw[:, j:j + 1] * _unpack_rows(yg_refs[j][...])
    z = DEEPNORM_ALPHA * x1_ref[...] + g2_ref[0] * acc
    out_ref[...] = _layer_norm(z) * lg_ref[...] + lb_ref[...]


def _combine(yg, gw, x1, xm, gate2, w_sgu, w_sd, ln_g, ln_b, n_per_batch):
    t, d = x1.shape
    tm = TM_COMBINE
    per = n_per_batch // tm
    steps = t // tm
    tok = lambda w: pl.BlockSpec((tm, w), lambda i: (i, 0))
    full = lambda r, c: pl.BlockSpec((r, c), lambda i: (0, 0))
    slot = lambda j: pl.BlockSpec((tm, PACK_W), lambda i: (j * steps + i, 0))
    return pl.pallas_call(
        _combine_kernel,
        out_shape=jax.ShapeDtypeStruct((t, d), F32),
        grid=(steps,),
        in_specs=[slot(j) for j in range(TOP_K)]
        + [tok(TOP_K), tok(d), tok(PACK_W), pl.BlockSpec((1, 1, d), lambda i: (i // per, 0, 0)),
           full(d, 2 * D_EXPERT), full(D_EXPERT, d), full(1, d), full(1, d)],
        out_specs=tok(d),
        compiler_params=_cparams(("arbitrary",)),
        name="combine",
    )(*([yg] * TOP_K), gw, x1, xm, gate2, w_sgu, w_sd, ln_g, ln_b)


def _mixer_heads(stream, shift, scale, w_in_k, conv_w9, conv_b, gate_b_row, head_g, c0, m0, grid2d):
    b, n, d = stream.shape
    y = _inproj(stream.reshape(b * n, d), shift, scale, w_in_k, n).reshape(b, n, PROJ_W)
    qk = _conv(y, conv_w9, conv_b, grid2d)
    mread, c_fin, m_fin = _mlstm(qk, y, _gates(y, gate_b_row), c0, m0, head_g)
    return y, mread, c_fin, m_fin


def _moe(xm, x1, scores, bias_row, gate2, w_gu, w_down, w_sgu, w_sd, ln_g, ln_b, n_per_batch):
    t = xm.shape[0]
    n_assign = t * TOP_K
    n_rows = n_assign + N_EXPERTS * ROW_BLK
    idx_t, gw, rank_t, counts = _route(scores, bias_row)
    sizes = counts[:, 0].astype(jnp.int32)
    padded = (sizes + ROW_BLK - 1) // ROW_BLK * ROW_BLK
    pad_end = jnp.cumsum(padded)
    pad_start = (pad_end - padded).astype(jnp.int32)
    dest_t = _dest(pad_start, idx_t, rank_t)
    dest3 = dest_t.reshape(TOP_K, t // SC_ROWS, SC_ROWS).transpose(1, 0, 2)

    xs = _sc_dispatch(xm, dest3, n_rows)
    yb = _experts(pad_start // ROW_BLK, (padded // ROW_BLK).astype(jnp.int32),
                  (pad_end[-1:] // ROW_BLK).astype(jnp.int32), xs, w_gu, w_down)
    yg = _sc_gather(yb, dest_t.reshape(n_assign // SC_GATHER_ROWS, SC_GATHER_ROWS))
    return _combine(yg, gw, x1, xm, gate2, w_sgu, w_sd, ln_g, ln_b, n_per_batch)


def kernel(x, c, ctx, c_ctx, w_ada, b_ada, w_in, conv_w, conv_b, gate_b, head_g, w_out, ln1_g, ln1_b,
           w_router, router_bias, w_expert_gu, w_expert_down, w_shared_gu, w_shared_down, ln2_g, ln2_b):
    bsz, n, d = x.shape
    depth = w_ada.shape[0]
    assert depth == 1
    t = bsz * n
    tables = _dft_tables()
    for l in range(depth):
        cvec = jnp.zeros((SUBLANES, d), F32).at[:bsz].set(c).at[bsz].set(c_ctx)
        mod = _ada(cvec, w_ada[l], b_ada[l][None, :])
        mods = mod.reshape(SUBLANES, 6, 1, d)
        mod_x = [mods[:bsz, j] for j in range(6)]
        mod_c = [jnp.broadcast_to(mods[bsz:bsz + 1, j], (bsz, 1, d)) for j in range(6)]

        wl = w_in[l]
        g0 = 4 * D_MLSTM
        w_in_k = jnp.concatenate(
            [wl[:, :g0], wl[:, g0 + N_GATE_COLS:], wl[:, g0:g0 + N_GATE_COLS],
             jnp.zeros((d, LANES - N_GATE_COLS), F32)], axis=1).astype(BF16)
        conv_w9 = conv_w[l].reshape(9, 2 * D_MLSTM)
        conv_b_row = conv_b[l][None, :]
        gate_b_row = jnp.zeros((1, LANES), F32).at[0, :N_GATE_COLS].set(gate_b[l].reshape(-1))
        head_g_row = head_g[l][None, :]

        c0 = jnp.zeros((bsz, N_HEADS, 2, STATE_ROWS, D_HEAD), F32)
        m0 = jnp.zeros((bsz, N_HEADS, 2, SUBLANES, LANES), F32)
        _, _, c_ctx_fin, m_ctx_fin = _mixer_heads(ctx, mod_c[0], mod_c[1], w_in_k, conv_w9, conv_b_row,
                                                  gate_b_row, head_g_row, c0, m0, grid2d=False)
        y, mread, _, _ = _mixer_heads(x, mod_x[0], mod_x[1], w_in_k, conv_w9, conv_b_row,
                                      gate_b_row, head_g_row, c_ctx_fin, m_ctx_fin, grid2d=True)

        four = _fourier(y.reshape(bsz, DFT_R, DFT_R, PROJ_W), tables).reshape(t, D_FOURIER)

        w_o = w_out[l].astype(BF16)
        w_r = jnp.zeros((d, LANES), F32).at[:, :N_EXPERTS].set(w_router[l]).astype(BF16)
        x2d = x.reshape(t, d)
        x1, xm, scores = _outproj(x2d, mread.reshape(t, D_MLSTM), four, w_o[:D_MLSTM], w_o[D_MLSTM:],
                                  mod_x[2], ln1_g[l][None, :], ln1_b[l][None, :], mod_x[3], mod_x[4],
                                  w_r, n)

        bias_rep = jnp.broadcast_to(router_bias[l][:, None], (N_EXPERTS, LANES))
        x = _moe(xm, x1, scores, bias_rep, mod_x[5], w_expert_gu[l], w_expert_down[l],
                 w_shared_gu[l].astype(BF16), w_shared_down[l].astype(BF16),
                 ln2_g[l][None, :], ln2_b[l][None, :], n).reshape(bsz, n, d)
    return x
```

```python
import functools

import numpy as np
import jax
import jax.numpy as jnp
from jax import lax
from jax.experimental import pallas as pl
from jax.experimental.pallas import tpu as pltpu
from jax.experimental.pallas import tpu_sc as plsc

F32 = jnp.float32
BF16 = jnp.bfloat16
HIGHEST = lax.Precision.HIGHEST

D_MODEL = 1024
GRID_W = 64
N_HEADS = 4
D_HEAD = 128
D_MLSTM = N_HEADS * D_HEAD
D_FOURIER = 512
FOURIER_GROUP = 64
N_GATE_COLS = 16
CHUNK = 128
N_EXPERTS = 64
TOP_K = 8
D_EXPERT = 256
ROUTED_SCALE = 2.5
DEEPNORM_ALPHA = 2.0 ** 0.25
LN_EPS = 1e-5

LANES = 128
SUBLANES = 8
VMEM_LIMIT = 48 * 1024 * 1024
FOUR_VMEM_LIMIT = 56 * 1024 * 1024

COL_QK = 0
COL_V = 2 * D_MLSTM
COL_O = 3 * D_MLSTM
COL_F = 4 * D_MLSTM
COL_G = COL_F + D_FOURIER
PROJ_W = COL_G + LANES

TM = 1024
TM_ROUTE = 512
ROW_BLK = 512
ROW_HALF = 256
EXP_AHEAD = 5
EXP_IN_SLOTS = 8
EXP_OUT_SLOTS = 4
PACK_W = D_MODEL // 2
SC_ROWS = 128
SC_GATHER_ROWS = 64
CONV_PAD = 72
CONV_RB = 256
DFT_R = 64


def _cparams(sem, vmem=VMEM_LIMIT):
    return pltpu.CompilerParams(dimension_semantics=sem, vmem_limit_bytes=vmem)


def _layer_norm(x):
    mu = jnp.mean(x, axis=-1, keepdims=True)
    xc = x - mu
    var = jnp.mean(xc * xc, axis=-1, keepdims=True)
    return xc * lax.rsqrt(var + LN_EPS)


def _silu(x):
    return x * jax.nn.sigmoid(x)


def _pack_pair(lo, hi):
    as_bits = lambda u: lax.bitcast_convert_type(u.astype(BF16).astype(F32), jnp.uint32)
    word = as_bits(hi) | lax.shift_right_logical(as_bits(lo), jnp.uint32(16))
    return lax.bitcast_convert_type(word, jnp.int32)


def _unpack_pair_bf16(w):
    lo = lax.bitcast_convert_type(w.astype(jnp.int16), BF16)
    hi = lax.bitcast_convert_type(lax.shift_right_logical(w, jnp.int32(16)).astype(jnp.int16), BF16)
    return lo, hi


def _pack_rows(v):
    return _pack_pair(v[:, :PACK_W], v[:, PACK_W:])


def _unpack_bf16(w):
    return jnp.concatenate(_unpack_pair_bf16(w), axis=-1)


def _unpack_rows(w):
    u = lax.bitcast_convert_type(w, jnp.uint32)
    lo = lax.bitcast_convert_type(lax.shift_left(u, jnp.uint32(16)), F32)
    hi = lax.bitcast_convert_type(u & jnp.uint32(0xFFFF0000), F32)
    return jnp.concatenate([lo, hi], axis=1)


def _ada_kernel(c_ref, w_ref, b_ref, o_ref):
    s = _silu(c_ref[...])
    o_ref[...] = jnp.dot(s, w_ref[...], precision=HIGHEST, preferred_element_type=F32) + b_ref[...]


def _ada(cvec, w_ada, b_ada):
    rows, d = cvec.shape
    n_out = w_ada.shape[1]
    tn = 2048
    return pl.pallas_call(
        _ada_kernel,
        out_shape=jax.ShapeDtypeStruct((rows, n_out), F32),
        grid=(n_out // tn,),
        in_specs=[pl.BlockSpec((rows, d), lambda j: (0, 0)),
                  pl.BlockSpec((d, tn), lambda j: (0, j)),
                  pl.BlockSpec((1, tn), lambda j: (0, j))],
        out_specs=pl.BlockSpec((rows, tn), lambda j: (0, j)),
        compiler_params=_cparams(("arbitrary",)),
        name="ada",
    )(cvec, w_ada, b_ada)


def _inproj_kernel(x_ref, sh_ref, sc_ref, w_ref, y_ref):
    half = x_ref.shape[0] // 2
    for h in range(2):
        r = slice(h * half, (h + 1) * half)
        u = _layer_norm(x_ref[r, :]) * (1.0 + sc_ref[0]) + sh_ref[0]
        y_ref[r, :] = jnp.dot(u.astype(BF16), w_ref[...], preferred_element_type=F32)


def _inproj(x2d, shift, scale, w, n_per_batch):
    t, d = x2d.shape
    tm = min(TM, n_per_batch)
    per = n_per_batch // tm
    return pl.pallas_call(
        _inproj_kernel,
        out_shape=jax.ShapeDtypeStruct((t, PROJ_W), F32),
        grid=(t // tm,),
        in_specs=[pl.BlockSpec((tm, d), lambda i: (i, 0)),
                  pl.BlockSpec((1, 1, d), lambda i: (i // per, 0, 0)),
                  pl.BlockSpec((1, 1, d), lambda i: (i // per, 0, 0)),
                  pl.BlockSpec((d, PROJ_W), lambda i: (0, 0))],
        out_specs=pl.BlockSpec((tm, PROJ_W), lambda i: (i, 0)),
        compiler_params=_cparams(("arbitrary",)),
        name="inproj",
    )(x2d, shift, scale, w)


def _conv_kernel(y_ref, w_ref, b_ref, o_ref, pad_ref, *, n, grid2d, tiles):
    zeros = jnp.zeros((CONV_PAD, LANES), F32)
    pad_ref[0:CONV_PAD, :] = zeros
    pad_ref[CONV_PAD + n:CONV_PAD + n + CONV_PAD, :] = zeros
    rb = min(CONV_RB, n)
    col = lax.broadcasted_iota(jnp.int32, (rb, LANES), 0) % GRID_W
    not_first = col >= 1
    not_last = col <= GRID_W - 2
    row_taps = (0, 1, 2) if grid2d else (1,)
    for ct in range(tiles):
        lanes = slice(ct * LANES, (ct + 1) * LANES)
        pad_ref[CONV_PAD:CONV_PAD + n, :] = y_ref[0, :, lanes]
        k_scale = jnp.where(pl.program_id(1) * tiles + ct >= N_HEADS, D_HEAD ** -0.5, 1.0).astype(F32)
        bias = b_ref[:, lanes]
        for blk in range(n // rb):
            r0 = CONV_PAD + blk * rb
            acc = jnp.zeros((rb, LANES), F32)
            for dj in range(3):
                part = jnp.zeros((rb, LANES), F32)
                for di in row_taps:
                    off = (di - 1) * GRID_W + (dj - 1)
                    tap = w_ref[di * 3 + dj:di * 3 + dj + 1, lanes]
                    part = part + tap * pad_ref[r0 + off:r0 + off + rb, :]
                if grid2d and dj == 0:
                    part = jnp.where(not_first, part, 0.0)
                if grid2d and dj == 2:
                    part = jnp.where(not_last, part, 0.0)
                acc = acc + part
            val = _silu(acc + bias) * k_scale
            o_ref[0, blk * rb:(blk + 1) * rb, lanes] = val.astype(o_ref.dtype)


def _conv(y3d, conv_w9, conv_b, grid2d):
    b, n, _ = y3d.shape
    n_ct = 2 * D_MLSTM // LANES
    tiles = 1 if grid2d else n_ct
    w = tiles * LANES
    return pl.pallas_call(
        functools.partial(_conv_kernel, n=n, grid2d=grid2d, tiles=tiles),
        out_shape=jax.ShapeDtypeStruct((b, n, 2 * D_MLSTM), BF16),
        grid=(b, n_ct // tiles),
        in_specs=[pl.BlockSpec((1, n, w), lambda i, c: (i, 0, c)),
                  pl.BlockSpec((9, w), lambda i, c: (0, c)),
                  pl.BlockSpec((1, w), lambda i, c: (0, c))],
        out_specs=pl.BlockSpec((1, n, w), lambda i, c: (i, 0, c)),
        scratch_shapes=[pltpu.VMEM((n + 2 * CONV_PAD, LANES), F32)],
        compiler_params=_cparams(("arbitrary", "arbitrary")),
        name="conv2d" if grid2d else "conv1d",
    )(y3d, conv_w9, conv_b)


STATE_ROWS = 2 * D_HEAD
GATE_A, GATE_CM, GATE_B = 0, 1, 2
GATE_GROUP = 8


def _gates_kernel(g_ref, gb_ref, rv_ref, *, nc):
    u_i = lax.broadcasted_iota(jnp.int32, (CHUNK, CHUNK), 0)
    t_i = lax.broadcasted_iota(jnp.int32, (CHUNK, CHUNK), 1)
    prefix = (u_i <= t_i).astype(F32)
    suffix = (u_i >= t_i).astype(F32)
    grp = min(GATE_GROUP, nc)
    rows = grp * SUBLANES
    lane = lax.broadcasted_iota(jnp.int32, (rows, LANES), 1)
    is_fwd = lax.broadcasted_iota(jnp.int32, (rows, LANES), 0) % SUBLANES < N_HEADS
    h = N_HEADS

    def running_max(a):
        fwd, bwd = a, a
        sh = 1
        while sh < CHUNK:
            fwd = jnp.where(lane >= sh, jnp.maximum(fwd, pltpu.roll(fwd, sh, axis=1)), fwd)
            bwd = jnp.where(lane < CHUNK - sh, jnp.maximum(bwd, pltpu.roll(bwd, CHUNK - sh, axis=1)), bwd)
            sh *= 2
        return jnp.where(is_fwd, fwd, bwd)

    def body(i, carry):
        c0 = i * grp
        r0 = pl.multiple_of(c0 * CHUNK, CHUNK)
        pre = g_ref[0, pl.ds(r0, grp * CHUNK), :] + gb_ref[...]
        pre_t = jnp.concatenate(
            [pre[j * CHUNK:(j + 1) * CHUNK].T[:N_GATE_COLS, :] for j in range(grp)], axis=0)
        lf_t = -(jnp.maximum(-pre_t, 0.0) + jnp.log1p(jnp.exp(-jnp.abs(pre_t))))
        cum_f = jnp.dot(lf_t, prefix, precision=HIGHEST, preferred_element_type=F32)
        cum_b = jnp.dot(lf_t, suffix, precision=HIGHEST, preferred_element_type=F32)
        pick = lambda t, j, q: t[j * N_GATE_COLS + q * h:j * N_GATE_COLS + (q + 1) * h]
        b8 = jnp.concatenate([x for j in range(grp) for x in (pick(cum_f, j, 1), pick(cum_b, j, 3))], axis=0)
        li8 = jnp.concatenate([x for j in range(grp) for x in (pick(pre_t, j, 0), pick(pre_t, j, 2))], axis=0)
        a8 = li8 - b8
        as_chunks = lambda t: t.reshape(grp, SUBLANES, LANES)
        rv_ref[0, pl.ds(c0, grp), GATE_A] = as_chunks(a8)
        rv_ref[0, pl.ds(c0, grp), GATE_CM] = as_chunks(running_max(a8))
        rv_ref[0, pl.ds(c0, grp), GATE_B] = as_chunks(b8)
        return carry

    lax.fori_loop(0, nc // grp, body, 0)


def _gates(y3d, gate_b_row):
    b, n, _ = y3d.shape
    nc = n // CHUNK
    return pl.pallas_call(
        functools.partial(_gates_kernel, nc=nc),
        out_shape=jax.ShapeDtypeStruct((b, nc, 3, SUBLANES, LANES), F32),
        grid=(b,),
        in_specs=[pl.BlockSpec((1, n, LANES), lambda i: (i, 0, COL_G // LANES)),
                  pl.BlockSpec((1, LANES), lambda i: (0, 0))],
        out_specs=pl.BlockSpec((1, nc, 3, SUBLANES, LANES), lambda i: (i, 0, 0, 0, 0)),
        compiler_params=_cparams(("arbitrary",)),
        name="gates",
    )(y3d, gate_b_row)


def _mlstm_kernel(q_ref, k_ref, v_ref, o_ref, rv_ref, c0_ref, m0_ref, hg_ref,
                  out_ref, cfin_ref, mfin_ref, hf_ref, hb_ref, cst_ref, mst_ref, *, nc, heads):
    cst_ref[...] = c0_ref[0]
    mst_ref[...] = m0_ref[0]
    s_i = lax.broadcasted_iota(jnp.int32, (CHUNK, CHUNK), 0)
    t_i = lax.broadcasted_iota(jnp.int32, (CHUNK, CHUNK), 1)
    ones_row = (s_i[:STATE_ROWS - D_HEAD] == 0).astype(F32)

    def one_chunk(c, d, mask_t, last, h_ref, hh):
        r0 = pl.multiple_of(c * CHUNK, CHUNK)
        lanes = slice(hh * D_HEAD, (hh + 1) * D_HEAD)
        g = d * N_HEADS + pl.program_id(1) * heads + hh
        gate_row = lambda tbl: rv_ref[0, pl.ds(c, 1), tbl, pl.ds(g, 1), :].reshape(1, LANES)
        a_row, cm_row, b_row = gate_row(GATE_A), gate_row(GATE_CM), gate_row(GATE_B)
        q_t = q_ref[0, pl.ds(r0, CHUNK), lanes].T
        k = k_ref[0, pl.ds(r0, CHUNK), lanes]
        v_t_aug = jnp.concatenate([v_ref[0, pl.ds(r0, CHUNK), lanes].T, ones_row], axis=0)
        m_prev = mst_ref[hh, d, 0:1, :]
        m_row = jnp.maximum(m_prev, cm_row)
        a_bc = jnp.broadcast_to(a_row, (CHUNK, CHUNK)).T
        d_t = jnp.where(mask_t, jnp.exp(a_bc - m_row), 0.0)
        s_t = jnp.dot(k, q_t, preferred_element_type=F32) * d_t
        cs = cst_ref[hh, d]
        rhs = jnp.concatenate(
            [s_t.astype(BF16), (q_t.astype(F32) * jnp.exp(m_prev - m_row)).astype(BF16)], axis=0)
        lhs = jnp.concatenate([v_t_aug.astype(BF16), cs.astype(BF16)], axis=1)
        h_t = jnp.dot(lhs, rhs, preferred_element_type=F32)
        den = h_t[D_HEAD:D_HEAD + 1, :]
        floor = jnp.exp(-b_row - m_row)
        h_ref[hh, pl.ds(c, 1)] = (h_t[:D_HEAD] / jnp.maximum(jnp.abs(den), floor))[None]
        m_last = m_row[:, last:last + 1]
        wv_t = (v_t_aug * jnp.exp(a_row - m_last)).astype(BF16)
        cst_ref[hh, d] = jnp.exp(m_prev - m_last) * cs + jnp.dot(wv_t, k, preferred_element_type=F32)
        mst_ref[hh, d] = jnp.broadcast_to(b_row[:, last:last + 1] + m_last, (SUBLANES, LANES))

    def body(c, carry):
        for hh in range(heads):
            one_chunk(c, 0, s_i <= t_i, CHUNK - 1, hf_ref, hh)
            one_chunk(nc - 1 - c, 1, s_i >= t_i, 0, hb_ref, hh)
        return carry

    lax.fori_loop(0, nc, body, 0, unroll=min(8 // heads, nc))
    cfin_ref[0] = cst_ref[...]
    mfin_ref[0] = mst_ref[...]

    def readout(c, carry):
        r0 = pl.multiple_of(c * CHUNK, CHUNK)
        for hh in range(heads):
            lanes = slice(hh * D_HEAD, (hh + 1) * D_HEAD)
            h_t = hf_ref[hh, pl.ds(c, 1)][0] + hb_ref[hh, pl.ds(c, 1)][0]
            hc = h_t - jnp.mean(h_t, axis=0, keepdims=True)
            hn = (hc * lax.rsqrt(jnp.mean(hc * hc, axis=0, keepdims=True) + LN_EPS)).T
            gate = jax.nn.sigmoid(o_ref[0, pl.ds(r0, CHUNK), lanes])
            out_ref[0, pl.ds(r0, CHUNK), lanes] = (hn * hg_ref[:, lanes] * gate).astype(out_ref.dtype)
        return carry

    lax.fori_loop(0, nc, readout, 0, unroll=min(4, nc))


def _mlstm(qk, y3d, rowv, c0, m0, head_g):
    b, n, _ = qk.shape
    nc = n // CHUNK
    heads = N_HEADS if nc <= 4 else 1
    w = heads * D_HEAD
    v_blk = COL_V // w
    o_blk = COL_O // w
    tok = lambda off: pl.BlockSpec((1, n, w), lambda i, h: (i, 0, off + h))
    state_c = pl.BlockSpec((1, heads, 2, STATE_ROWS, D_HEAD), lambda i, h: (i, h, 0, 0, 0))
    state_m = pl.BlockSpec((1, heads, 2, SUBLANES, LANES), lambda i, h: (i, h, 0, 0, 0))
    return pl.pallas_call(
        functools.partial(_mlstm_kernel, nc=nc, heads=heads),
        out_shape=(jax.ShapeDtypeStruct((b, n, D_MLSTM), BF16),
                   jax.ShapeDtypeStruct((b, N_HEADS, 2, STATE_ROWS, D_HEAD), F32),
                   jax.ShapeDtypeStruct((b, N_HEADS, 2, SUBLANES, LANES), F32)),
        grid=(b, N_HEADS // heads),
        in_specs=[tok(0), tok(D_MLSTM // w), tok(v_blk), tok(o_blk),
                  pl.BlockSpec((1, nc, 3, SUBLANES, LANES), lambda i, h: (i, 0, 0, 0, 0)),
                  state_c, state_m,
                  pl.BlockSpec((1, w), lambda i, h: (0, h))],
        out_specs=(pl.BlockSpec((1, n, w), lambda i, h: (i, 0, h)), state_c, state_m),
        scratch_shapes=[pltpu.VMEM((heads, nc, D_HEAD, CHUNK), F32), pltpu.VMEM((heads, nc, D_HEAD, CHUNK), F32),
                        pltpu.VMEM((heads, 2, STATE_ROWS, D_HEAD), F32),
                        pltpu.VMEM((heads, 2, SUBLANES, LANES), F32)],
        compiler_params=_cparams(("arbitrary", "arbitrary")),
        name="mlstm",
    )(qk, qk, y3d, y3d, rowv, c0, m0, head_g)


def _dft_tables():
    r = DFT_R
    idx = np.arange(r)
    cg = np.cos(2 * np.pi * np.outer(idx, idx) / FOURIER_GROUP)
    sg = np.sin(2 * np.pi * np.outer(idx, idx) / FOURIER_GROUP)
    n_grp = D_FOURIER // FOURIER_GROUP
    bdc = np.kron(np.eye(n_grp), cg) / 8.0
    bds = np.kron(np.eye(n_grp), sg) / 8.0
    k1 = idx[:, None]
    a = idx[None, :]
    m1 = np.zeros((r, 2 * r, 2 * r))
    for b2 in range(r):
        th = 2 * np.pi * k1 * (r * a + b2) / (r * r)
        ec, es = np.cos(th), np.sin(th)
        m1[b2] = np.block([[ec, -es], [es, ec]]) / 8.0
    th2 = 2 * np.pi * np.outer(idx, idx) / r
    m2 = np.concatenate([np.cos(th2), -np.sin(th2)], axis=1) / 8.0
    as_bf16 = lambda t: jnp.asarray(t, dtype=F32).astype(BF16)
    return as_bf16(bdc), as_bf16(bds), as_bf16(m1), as_bf16(m2)


FOUR_BB = 8


N_SLABS = D_FOURIER // LANES


def _to_slabs(slab_ref, val):
    for s in range(N_SLABS):
        slab_ref[s] = val[:, s * LANES:(s + 1) * LANES]


def _strided_rows(slab_ref, start):
    return jnp.concatenate(
        [slab_ref[s, pl.ds(start, DFT_R, stride=FOUR_BB), :] for s in range(N_SLABS)], axis=1)


def _four_kernel(f_ref, bdc_ref, bds_ref, m1_ref, m2_ref, o_ref, z_ref, ps_ref, qs_ref, zs_ref, os_ref):
    r = DFT_R

    def stage1(blk, carry):
        b0 = pl.multiple_of(blk * FOUR_BB, FOUR_BB)
        fb = f_ref[0, :, pl.ds(b0, FOUR_BB), :].reshape(r * FOUR_BB, D_FOURIER).astype(BF16)
        _to_slabs(ps_ref, jnp.dot(fb, bdc_ref[...], preferred_element_type=F32))
        _to_slabs(qs_ref, jnp.dot(fb, bds_ref[...], preferred_element_type=F32))
        for j in range(FOUR_BB):
            x2 = jnp.concatenate([_strided_rows(ps_ref, j), _strided_rows(qs_ref, j)], axis=0).astype(BF16)
            zz = jnp.dot(m1_ref[b0 + j], x2, preferred_element_type=F32)
            z_ref[b0 + j] = _pack_pair(zz[:r], zz[r:])
        return carry

    lax.fori_loop(0, r // FOUR_BB, stage1, 0)

    def stage2(blk, carry):
        k0 = pl.multiple_of(blk * FOUR_BB, FOUR_BB)
        _to_slabs(zs_ref, z_ref[:, pl.ds(k0, FOUR_BB), :].reshape(r * FOUR_BB, D_FOURIER))
        for j in range(FOUR_BB):
            zc, zs = _unpack_pair_bf16(_strided_rows(zs_ref, j))
            o = jnp.dot(m2_ref[...], jnp.concatenate([zc, zs], axis=0), preferred_element_type=F32)
            for s in range(N_SLABS):
                os_ref[s, pl.ds(j, r, stride=FOUR_BB), :] = o[:, s * LANES:(s + 1) * LANES]
        for s in range(N_SLABS):
            o_ref[0, :, pl.ds(k0, FOUR_BB), s * LANES:(s + 1) * LANES] = os_ref[s].reshape(r, FOUR_BB, LANES)
        return carry

    lax.fori_loop(0, r // FOUR_BB, stage2, 0)


def _fourier(y4, tables):
    bdc, bds, m1, m2 = tables
    b = y4.shape[0]
    r = DFT_R
    slab = lambda dt: pltpu.VMEM((N_SLABS, r * FOUR_BB, LANES), dt)
    const = lambda *shape: pl.BlockSpec(shape, lambda i: (0,) * len(shape))
    return pl.pallas_call(
        _four_kernel,
        out_shape=jax.ShapeDtypeStruct((b, r, r, D_FOURIER), F32),
        grid=(b,),
        in_specs=[pl.BlockSpec((1, r, r, D_FOURIER), lambda i: (i, 0, 0, COL_F // D_FOURIER)),
                  const(D_FOURIER, D_FOURIER), const(D_FOURIER, D_FOURIER),
                  const(r, 2 * r, 2 * r), const(r, 2 * r)],
        out_specs=pl.BlockSpec((1, r, r, D_FOURIER), lambda i: (i, 0, 0, 0)),
        scratch_shapes=[pltpu.VMEM((r, r, D_FOURIER), jnp.int32), slab(F32), slab(F32), slab(jnp.int32), slab(F32)],
        compiler_params=_cparams(("arbitrary",), vmem=FOUR_VMEM_LIMIT),
        name="fourier",
    )(y4, bdc, bds, m1, m2)


def _outproj_kernel(x_ref, mr_ref, fo_ref, wm_ref, wf_ref, g1_ref, lg_ref, lb_ref,
                    sh_ref, sc_ref, wr_ref, x1_ref, xm_ref, s_ref):
    half = x_ref.shape[0] // 2
    for h in range(2):
        r = slice(h * half, (h + 1) * half)
        mix = (jnp.dot(mr_ref[r, :], wm_ref[...], preferred_element_type=F32)
               + jnp.dot(fo_ref[r, :].astype(BF16), wf_ref[...], preferred_element_type=F32))
        x1 = _layer_norm(DEEPNORM_ALPHA * x_ref[r, :] + g1_ref[0] * mix) * lg_ref[...] + lb_ref[...]
        x1_ref[r, :] = x1
        xm = (_layer_norm(x1) * (1.0 + sc_ref[0]) + sh_ref[0]).astype(BF16)
        xm_ref[r, :] = _pack_rows(xm)
        logits = jnp.dot(xm, wr_ref[...], preferred_element_type=F32)
        s_ref[r, :] = jax.nn.sigmoid(logits)


def _outproj(x2d, mread, four, w_m, w_f, gate1, ln_g, ln_b, shift2, scale2, w_router, n_per_batch):
    t, d = x2d.shape
    per = n_per_batch // TM
    tokd = lambda w: pl.BlockSpec((TM, w), lambda i: (i, 0))
    full = lambda r, c: pl.BlockSpec((r, c), lambda i: (0, 0))
    mod = pl.BlockSpec((1, 1, d), lambda i: (i // per, 0, 0))
    return pl.pallas_call(
        _outproj_kernel,
        out_shape=(jax.ShapeDtypeStruct((t, d), F32),
                   jax.ShapeDtypeStruct((t, PACK_W), jnp.int32),
                   jax.ShapeDtypeStruct((t, LANES), F32)),
        grid=(t // TM,),
        in_specs=[tokd(d), tokd(D_MLSTM), tokd(D_FOURIER), full(D_MLSTM, d), full(D_FOURIER, d),
                  mod, full(1, d), full(1, d), mod, mod, full(d, LANES)],
        out_specs=(tokd(d), tokd(PACK_W), tokd(LANES)),
        compiler_params=_cparams(("arbitrary",)),
        name="outproj",
    )(x2d, mread, four, w_m, w_f, gate1, ln_g, ln_b, shift2, scale2, w_router)


def _route_kernel(s_ref, b_ref, idx_ref, gw_ref, rank_ref, cnt_ref, carry_ref):
    @pl.when(pl.program_id(0) == 0)
    def _():
        carry_ref[...] = jnp.zeros_like(carry_ref)

    tm = s_ref.shape[0]
    reps = tm // LANES
    s_t = s_ref[...].T[:N_EXPERTS, :]
    e_i = lax.broadcasted_iota(jnp.int32, (N_EXPERTS, tm), 0).astype(F32)
    neg_inf = jnp.float32(-jnp.inf)
    sb = s_t + jnp.tile(b_ref[...], (1, reps))
    picks, sels = [], []
    taken = jnp.zeros((N_EXPERTS, tm), F32)
    for _ in range(TOP_K):
        best = jnp.max(sb, axis=0, keepdims=True)
        pick = jnp.min(jnp.where(sb == best, e_i, float(N_EXPERTS)), axis=0, keepdims=True)
        onehot = e_i == pick
        sels.append(jnp.sum(jnp.where(onehot, s_t, 0.0), axis=0, keepdims=True))
        picks.append(pick)
        sb = jnp.where(onehot, neg_inf, sb)
        taken = taken + onehot.astype(F32)
    u_i = lax.broadcasted_iota(jnp.int32, (tm, tm), 0)
    t_i = lax.broadcasted_iota(jnp.int32, (tm, tm), 1)
    before = (u_i < t_i).astype(BF16)
    taken_bf = taken.astype(BF16)
    rank_all = jnp.tile(carry_ref[...], (1, reps)) + jnp.dot(taken_bf, before, preferred_element_type=F32)
    carry_ref[...] = carry_ref[...] + jnp.dot(taken_bf, jnp.ones((tm, LANES), BF16),
                                              preferred_element_type=F32)
    cnt_ref[...] = carry_ref[...]
    total = sels[0]
    for v in sels[1:]:
        total = total + v
    ranks = [jnp.sum(jnp.where(e_i == picks[j], rank_all, 0.0), axis=0, keepdims=True) for j in range(TOP_K)]
    idx_ref[...] = jnp.concatenate(picks, axis=0).astype(jnp.int32)
    rank_ref[...] = jnp.concatenate(ranks, axis=0).astype(jnp.int32)
    gw_t = jnp.concatenate([ROUTED_SCALE * v / total for v in sels]
                           + [jnp.zeros((LANES - TOP_K, tm), F32)], axis=0)
    gw_ref[...] = gw_t.T[:, :TOP_K]


def _route(scores, bias_rep):
    t = scores.shape[0]
    tok = lambda w: pl.BlockSpec((TM_ROUTE, w), lambda i: (i, 0))
    slot_major = pl.BlockSpec((TOP_K, TM_ROUTE), lambda i: (0, i))
    per_expert = pl.BlockSpec((N_EXPERTS, LANES), lambda i: (0, 0))
    return pl.pallas_call(
        _route_kernel,
        out_shape=(jax.ShapeDtypeStruct((TOP_K, t), jnp.int32),
                   jax.ShapeDtypeStruct((t, TOP_K), F32),
                   jax.ShapeDtypeStruct((TOP_K, t), jnp.int32),
                   jax.ShapeDtypeStruct((N_EXPERTS, LANES), F32)),
        grid=(t // TM_ROUTE,),
        in_specs=[tok(LANES), per_expert],
        out_specs=(slot_major, tok(TOP_K), slot_major, per_expert),
        scratch_shapes=[pltpu.VMEM((N_EXPERTS, LANES), F32)],
        compiler_params=_cparams(("arbitrary",)),
        name="route",
    )(scores, bias_rep)


TN_DEST = 2048


def _dest_kernel(ps_ref, idx_ref, rank_ref, dest_ref):
    idx = idx_ref[...]
    start = jnp.zeros(idx.shape, jnp.int32)
    for e in range(N_EXPERTS):
        start = jnp.where(idx == e, ps_ref[e], start)
    dest_ref[...] = start + rank_ref[...]


def _dest(pad_start, idx_t, rank_t):
    t = idx_t.shape[1]
    blk = lambda: pl.BlockSpec((TOP_K, TN_DEST), lambda i, ps: (0, i))
    grid_spec = pltpu.PrefetchScalarGridSpec(
        num_scalar_prefetch=1, grid=(t // TN_DEST,), in_specs=[blk(), blk()], out_specs=blk())
    return pl.pallas_call(
        _dest_kernel,
        out_shape=jax.ShapeDtypeStruct((TOP_K, t), jnp.int32),
        grid_spec=grid_spec,
        compiler_params=_cparams(("arbitrary",)),
        name="dest",
    )(pad_start, idx_t, rank_t)


def _sc_worker_id(info):
    return lax.axis_index("s") * info.num_cores + lax.axis_index("c")


def _sc_dispatch(xm, dest3, n_rows):
    t, w = xm.shape
    info = plsc.get_sparse_core_info()
    n_workers = info.num_cores * info.num_subcores
    per_worker = t // SC_ROWS // n_workers
    mesh = plsc.VectorSubcoreMesh(core_axis_name="c", subcore_axis_name="s")

    @functools.partial(
        pl.kernel, mesh=mesh,
        out_type=jax.ShapeDtypeStruct((n_rows, w), xm.dtype),
        scratch_types=[pltpu.VMEM((TOP_K, SC_ROWS), jnp.int32), pltpu.VMEM((SC_ROWS, w), xm.dtype)],
        name="sc_dispatch",
    )
    def run(xm_hbm, dest_hbm, xs_hbm, idx_v, rows_v):
        first = _sc_worker_id(info) * per_worker

        @pl.loop(0, per_worker)
        def _(ci):
            chunk = first + ci
            pltpu.sync_copy(dest_hbm.at[chunk], idx_v)
            pltpu.sync_copy(xm_hbm.at[pl.ds(chunk * SC_ROWS, SC_ROWS)], rows_v)
            for j in range(TOP_K):
                pltpu.sync_copy(rows_v, xs_hbm.at[idx_v.at[j]])

    return run(xm, dest3)


def _sc_gather(table, idx2):
    n_chunks, rows = idx2.shape
    w = table.shape[1]
    info = plsc.get_sparse_core_info()
    n_workers = info.num_cores * info.num_subcores
    per_worker = n_chunks // n_workers
    assert per_worker % 2 == 0
    mesh = plsc.VectorSubcoreMesh(core_axis_name="c", subcore_axis_name="s")

    @functools.partial(
        pl.kernel, mesh=mesh,
        out_type=jax.ShapeDtypeStruct((n_chunks * rows, w), table.dtype),
        scratch_types=[pltpu.VMEM((2, rows), jnp.int32), pltpu.VMEM((2, rows, w), table.dtype),
                       pltpu.SemaphoreType.DMA, pltpu.SemaphoreType.DMA],
        name="sc_gather",
    )
    def run(table_hbm, idx_hbm, out_hbm, idx_v, rows_v, sem0, sem1):
        first = _sc_worker_id(info) * per_worker
        sems = (sem0, sem1)

        def gather(b):
            return pltpu.make_async_copy(table_hbm.at[idx_v.at[b]], rows_v.at[b], sems[b])

        def start(chunk, b):
            pltpu.sync_copy(idx_hbm.at[chunk], idx_v.at[b])
            gather(b).start()

        def finish(chunk, b):
            gather(b).wait()
            pltpu.sync_copy(rows_v.at[b], out_hbm.at[pl.ds(chunk * rows, rows)])

        start(first, 0)

        @pl.loop(0, per_worker, step=2)
        def _(ci):
            chunk = first + ci
            start(chunk + 1, 1)
            finish(chunk, 0)

            @pl.when(ci + 2 < per_worker)
            def _():
                start(chunk + 2, 0)

            finish(chunk + 1, 1)

    return run(table, idx2)


def _expert_kernel(first_ref, count_ref, used_ref,
                   xs_hbm, wgu_ref, wd_ref, y_hbm, wgu_bf, wd_bf, xbuf, ybuf, in_sem, out_sem):
    e = pl.program_id(0)
    n_used = used_ref[0]
    first = first_ref[e]
    n_blocks = count_ref[e]

    def rows(g):
        return pl.ds(pl.multiple_of(g * ROW_BLK, ROW_BLK), ROW_BLK)

    def read(g):
        slot = g % EXP_IN_SLOTS
        return pltpu.make_async_copy(xs_hbm.at[rows(g), :], xbuf.at[slot], in_sem.at[slot])

    def write(g):
        slot = g % EXP_OUT_SLOTS
        return pltpu.make_async_copy(ybuf.at[slot], y_hbm.at[rows(g), :], out_sem.at[slot])

    @pl.when(e == 0)
    def _():
        for g0 in range(EXP_AHEAD):
            @pl.when(g0 < n_used)
            def _():
                read(g0).start()

    @pl.when(n_blocks > 0)
    def _():
        wgu_bf[...] = wgu_ref[0].astype(BF16)
        wd_bf[...] = wd_ref[0].astype(BF16)

    def acquire(g):
        @pl.when(g + EXP_AHEAD < n_used)
        def _():
            read(g + EXP_AHEAD).start()

        read(g).wait()

        @pl.when(g >= EXP_OUT_SLOTS)
        def _():
            write(g - EXP_OUT_SLOTS).wait()

    def compute(g):
        in_slot = g % EXP_IN_SLOTS
        out_slot = g % EXP_OUT_SLOTS
        for h in range(ROW_BLK // ROW_HALF):
            r0 = h * ROW_HALF
            x = _unpack_bf16(xbuf[in_slot, r0:r0 + ROW_HALF, :])
            gu = jnp.dot(x, wgu_bf[...], preferred_element_type=F32)
            act = _silu(gu[:, :D_EXPERT]) * gu[:, D_EXPERT:]
            yv = jnp.dot(act.astype(BF16), wd_bf[...], preferred_element_type=F32)
            words = pltpu.pack_elementwise([yv[:, :PACK_W], yv[:, PACK_W:]], packed_dtype=BF16)
            ybuf[out_slot, r0:r0 + ROW_HALF, :] = lax.bitcast_convert_type(words, jnp.int32)

    def pair(k, carry):
        g = first + 2 * k
        acquire(g)
        acquire(g + 1)
        compute(g)
        compute(g + 1)
        write(g).start()
        write(g + 1).start()
        return carry

    lax.fori_loop(0, n_blocks // 2, pair, 0)

    @pl.when(n_blocks % 2 == 1)
    def _():
        g = first + n_blocks - 1
        acquire(g)
        compute(g)
        write(g).start()

    @pl.when(e == pl.num_programs(0) - 1)
    def _():
        for back in range(EXP_OUT_SLOTS, 0, -1):
            @pl.when(n_used >= back)
            def _():
                write(n_used - back).wait()


def _experts(first_block, n_blocks, n_used, xs, w_gu, w_down):
    n_rows, w = xs.shape
    d = D_MODEL
    grid_spec = pltpu.PrefetchScalarGridSpec(
        num_scalar_prefetch=3,
        grid=(N_EXPERTS,),
        in_specs=[pl.BlockSpec(memory_space=pl.ANY),
                  pl.BlockSpec((1, d, 2 * D_EXPERT), lambda e, *_: (e, 0, 0)),
                  pl.BlockSpec((1, D_EXPERT, d), lambda e, *_: (e, 0, 0))],
        out_specs=pl.BlockSpec(memory_space=pl.ANY),
        scratch_shapes=[pltpu.VMEM((d, 2 * D_EXPERT), BF16), pltpu.VMEM((D_EXPERT, d), BF16),
                        pltpu.VMEM((EXP_IN_SLOTS, ROW_BLK, w), jnp.int32),
                        pltpu.VMEM((EXP_OUT_SLOTS, ROW_BLK, w), jnp.int32),
                        pltpu.SemaphoreType.DMA((EXP_IN_SLOTS,)),
                        pltpu.SemaphoreType.DMA((EXP_OUT_SLOTS,))],
    )
    return pl.pallas_call(
        _expert_kernel,
        out_shape=jax.ShapeDtypeStruct((n_rows, w), jnp.int32),
        grid_spec=grid_spec,
        compiler_params=_cparams(("arbitrary",)),
        name="experts",
    )(first_block, n_blocks, n_used, xs, w_gu, w_down)


TM_COMBINE = 512


def _combine_kernel(*refs):
    yg_refs = refs[:TOP_K]
    gw_ref, x1_ref, xm_ref, g2_ref, wsgu_ref, wsd_ref, lg_ref, lb_ref, out_ref = refs[TOP_K:]
    gu = jnp.dot(_unpack_bf16(xm_ref[...]), wsgu_ref[...], preferred_element_type=F32)
    act = _silu(gu[:, :D_EXPERT]) * gu[:, D_EXPERT:]
    acc = jnp.dot(act.astype(BF16), wsd_ref[...], preferred_element_type=F32)
    gw = gw_ref[...]
    for j in range(TOP_K):
        acc = acc + gw[:, j:j + 1] * _unpack_rows(yg_refs[j][...])
    z = DEEPNORM_ALPHA * x1_ref[...] + g2_ref[0] * acc
    out_ref[...] = _layer_norm(z) * lg_ref[...] + lb_ref[...]


def _combine(yg, gw, x1, xm, gate2, w_sgu, w_sd, ln_g, ln_b, n_per_batch):
    t, d = x1.shape
    tm = TM_COMBINE
    per = n_per_batch // tm
    steps = t // tm
    tok = lambda w: pl.BlockSpec((tm, w), lambda i: (i, 0))
    full = lambda r, c: pl.BlockSpec((r, c), lambda i: (0, 0))
    slot = lambda j: pl.BlockSpec((tm, PACK_W), lambda i: (j * steps + i, 0))
    return pl.pallas_call(
        _combine_kernel,
        out_shape=jax.ShapeDtypeStruct((t, d), F32),
        grid=(steps,),
        in_specs=[slot(j) for j in range(TOP_K)]
        + [tok(TOP_K), tok(d), tok(PACK_W), pl.BlockSpec((1, 1, d), lambda i: (i // per, 0, 0)),
           full(d, 2 * D_EXPERT), full(D_EXPERT, d), full(1, d), full(1, d)],
        out_specs=tok(d),
        compiler_params=_cparams(("arbitrary",)),
        name="combine",
    )(*([yg] * TOP_K), gw, x1, xm, gate2, w_sgu, w_sd, ln_g, ln_b)


def _mixer_heads(stream, shift, scale, w_in_k, conv_w9, conv_b, gate_b_row, head_g, c0, m0, grid2d):
    b, n, d = stream.shape
    y = _inproj(stream.reshape(b * n, d), shift, scale, w_in_k, n).reshape(b, n, PROJ_W)
    qk = _conv(y, conv_w9, conv_b, grid2d)
    mread, c_fin, m_fin = _mlstm(qk, y, _gates(y, gate_b_row), c0, m0, head_g)
    return y, mread, c_fin, m_fin


def _moe(xm, x1, scores, bias_row, gate2, w_gu, w_down, w_sgu, w_sd, ln_g, ln_b, n_per_batch):
    t = xm.shape[0]
    n_assign = t * TOP_K
    n_rows = n_assign + N_EXPERTS * ROW_BLK
    idx_t, gw, rank_t, counts = _route(scores, bias_row)
    sizes = counts[:, 0].astype(jnp.int32)
    padded = (sizes + ROW_BLK - 1) // ROW_BLK * ROW_BLK
    pad_end = jnp.cumsum(padded)
    pad_start = (pad_end - padded).astype(jnp.int32)
    dest_t = _dest(pad_start, idx_t, rank_t)
    dest3 = dest_t.reshape(TOP_K, t // SC_ROWS, SC_ROWS).transpose(1, 0, 2)

    xs = _sc_dispatch(xm, dest3, n_rows)
    yb = _experts(pad_start // ROW_BLK, (padded // ROW_BLK).astype(jnp.int32),
                  (pad_end[-1:] // ROW_BLK).astype(jnp.int32), xs, w_gu, w_down)
    yg = _sc_gather(yb, dest_t.reshape(n_assign // SC_GATHER_ROWS, SC_GATHER_ROWS))
    return _combine(yg, gw, x1, xm, gate2, w_sgu, w_sd, ln_g, ln_b, n_per_batch)


def kernel(x, c, ctx, c_ctx, w_ada, b_ada, w_in, conv_w, conv_b, gate_b, head_g, w_out, ln1_g, ln1_b,
           w_router, router_bias, w_expert_gu, w_expert_down, w_shared_gu, w_shared_down, ln2_g, ln2_b):
    bsz, n, d = x.shape
    depth = w_ada.shape[0]
    assert depth == 1
    t = bsz * n
    tables = _dft_tables()
    for l in range(depth):
        cvec = jnp.zeros((SUBLANES, d), F32).at[:bsz].set(c).at[bsz].set(c_ctx)
        mod = _ada(cvec, w_ada[l], b_ada[l][None, :])
        mods = mod.reshape(SUBLANES, 6, 1, d)
        mod_x = [mods[:bsz, j] for j in range(6)]
        mod_c = [jnp.broadcast_to(mods[bsz:bsz + 1, j], (bsz, 1, d)) for j in range(6)]

        wl = w_in[l]
        g0 = 4 * D_MLSTM
        w_in_k = jnp.concatenate(
            [wl[:, :g0], wl[:, g0 + N_GATE_COLS:], wl[:, g0:g0 + N_GATE_COLS],
             jnp.zeros((d, LANES - N_GATE_COLS), F32)], axis=1).astype(BF16)
        conv_w9 = conv_w[l].reshape(9, 2 * D_MLSTM)
        conv_b_row = conv_b[l][None, :]
        gate_b_row = jnp.zeros((1, LANES), F32).at[0, :N_GATE_COLS].set(gate_b[l].reshape(-1))
        head_g_row = head_g[l][None, :]

        c0 = jnp.zeros((bsz, N_HEADS, 2, STATE_ROWS, D_HEAD), F32)
        m0 = jnp.zeros((bsz, N_HEADS, 2, SUBLANES, LANES), F32)
        _, _, c_ctx_fin, m_ctx_fin = _mixer_heads(ctx, mod_c[0], mod_c[1], w_in_k, conv_w9, conv_b_row,
                                                  gate_b_row, head_g_row, c0, m0, grid2d=False)
        y, mread, _, _ = _mixer_heads(x, mod_x[0], mod_x[1], w_in_k, conv_w9, conv_b_row,
                                      gate_b_row, head_g_row, c_ctx_fin, m_ctx_fin, grid2d=True)

        four = _fourier(y.reshape(bsz, DFT_R, DFT_R, PROJ_W), tables).reshape(t, D_FOURIER)

        w_o = w_out[l].astype(BF16)
        w_r = jnp.zeros((d, LANES), F32).at[:, :N_EXPERTS].set(w_router[l]).astype(BF16)
        x2d = x.reshape(t, d)
        x1, xm, scores = _outproj(x2d, mread.reshape(t, D_MLSTM), four, w_o[:D_MLSTM], w_o[D_MLSTM:],
                                  mod_x[2], ln1_g[l][None, :], ln1_b[l][None, :], mod_x[3], mod_x[4],
                                  w_r, n)

        bias_rep = jnp.broadcast_to(router_bias[l][:, None], (N_EXPERTS, LANES))
        x = _moe(xm, x1, scores, bias_rep, mod_x[5], w_expert_gu[l], w_expert_down[l],
                 w_shared_gu[l].astype(BF16), w_shared_down[l].astype(BF16),
                 ln2_g[l][None, :], ln2_b[l][None, :], n).reshape(bsz, n, d)
    return x
```

```python
import functools

import numpy as np
import jax
import jax.numpy as jnp
from jax import lax
from jax.experimental import pallas as pl
from jax.experimental.pallas import tpu as pltpu
from jax.experimental.pallas import tpu_sc as plsc

F32 = jnp.float32
BF16 = jnp.bfloat16
HIGHEST = lax.Precision.HIGHEST

D_MODEL = 1024
GRID_W = 64
N_HEADS = 4
D_HEAD = 128
D_MLSTM = N_HEADS * D_HEAD
D_FOURIER = 512
FOURIER_GROUP = 64
N_GATE_COLS = 16
CHUNK = 128
N_EXPERTS = 64
TOP_K = 8
D_EXPERT = 256
ROUTED_SCALE = 2.5
DEEPNORM_ALPHA = 2.0 ** 0.25
LN_EPS = 1e-5

LANES = 128
SUBLANES = 8
VMEM_LIMIT = 48 * 1024 * 1024
FOUR_VMEM_LIMIT = 56 * 1024 * 1024

COL_QK = 0
COL_V = 2 * D_MLSTM
COL_O = 3 * D_MLSTM
COL_F = 4 * D_MLSTM
COL_G = COL_F + D_FOURIER
PROJ_W = COL_G + LANES

TM = 1024
TM_ROUTE = 512
ROW_BLK = 512
ROW_HALF = 256
EXP_AHEAD = 5
EXP_IN_SLOTS = 8
EXP_OUT_SLOTS = 4
PACK_W = D_MODEL // 2
SC_ROWS = 128
SC_GATHER_ROWS = 64
CONV_PAD = 72
CONV_RB = 256
DFT_R = 64


def _cparams(sem, vmem=VMEM_LIMIT):
    return pltpu.CompilerParams(dimension_semantics=sem, vmem_limit_bytes=vmem)


def _layer_norm(x):
    mu = jnp.mean(x, axis=-1, keepdims=True)
    xc = x - mu
    var = jnp.mean(xc * xc, axis=-1, keepdims=True)
    return xc * lax.rsqrt(var + LN_EPS)


def _silu(x):
    return x * jax.nn.sigmoid(x)


def _pack_pair(lo, hi):
    as_bits = lambda u: lax.bitcast_convert_type(u.astype(BF16).astype(F32), jnp.uint32)
    word = as_bits(hi) | lax.shift_right_logical(as_bits(lo), jnp.uint32(16))
    return lax.bitcast_convert_type(word, jnp.int32)


def _unpack_pair_bf16(w):
    lo = lax.bitcast_convert_type(w.astype(jnp.int16), BF16)
    hi = lax.bitcast_convert_type(lax.shift_right_logical(w, jnp.int32(16)).astype(jnp.int16), BF16)
    return lo, hi


def _pack_rows(v):
    return _pack_pair(v[:, :PACK_W], v[:, PACK_W:])


def _unpack_bf16(w):
    return jnp.concatenate(_unpack_pair_bf16(w), axis=-1)


def _unpack_rows(w):
    u = lax.bitcast_convert_type(w, jnp.uint32)
    lo = lax.bitcast_convert_type(lax.shift_left(u, jnp.uint32(16)), F32)
    hi = lax.bitcast_convert_type(u & jnp.uint32(0xFFFF0000), F32)
    return jnp.concatenate([lo, hi], axis=1)


def _ada_kernel(c_ref, w_ref, b_ref, o_ref):
    s = _silu(c_ref[...]).astype(BF16)
    o_ref[...] = jnp.dot(s, w_ref[...].astype(BF16), preferred_element_type=F32) + b_ref[...]


def _ada(cvec, w_ada, b_ada):
    rows, d = cvec.shape
    n_out = w_ada.shape[1]
    tn = 2048
    return pl.pallas_call(
        _ada_kernel,
        out_shape=jax.ShapeDtypeStruct((rows, n_out), F32),
        grid=(n_out // tn,),
        in_specs=[pl.BlockSpec((rows, d), lambda j: (0, 0)),
                  pl.BlockSpec((d, tn), lambda j: (0, j)),
                  pl.BlockSpec((1, tn), lambda j: (0, j))],
        out_specs=pl.BlockSpec((rows, tn), lambda j: (0, j)),
        compiler_params=_cparams(("arbitrary",)),
        name="ada",
    )(cvec, w_ada, b_ada)


def _inproj_kernel(x_ref, sh_ref, sc_ref, w_ref, y_ref):
    half = x_ref.shape[0] // 2
    for h in range(2):
        r = slice(h * half, (h + 1) * half)
        u = _layer_norm(x_ref[r, :]) * (1.0 + sc_ref[0]) + sh_ref[0]
        y_ref[r, :] = jnp.dot(u.astype(BF16), w_ref[...], preferred_element_type=F32)


def _inproj(x2d, shift, scale, w, n_per_batch):
    t, d = x2d.shape
    if shift.shape[0] == 1:
        tm = min(TM, t)
        per = t // tm
    else:
        tm = min(TM, n_per_batch)
        per = n_per_batch // tm
    return pl.pallas_call(
        _inproj_kernel,
        out_shape=jax.ShapeDtypeStruct((t, PROJ_W), F32),
        grid=(t // tm,),
        in_specs=[pl.BlockSpec((tm, d), lambda i: (i, 0)),
                  pl.BlockSpec((1, 1, d), lambda i: (i // per, 0, 0)),
                  pl.BlockSpec((1, 1, d), lambda i: (i // per, 0, 0)),
                  pl.BlockSpec((d, PROJ_W), lambda i: (0, 0))],
        out_specs=pl.BlockSpec((tm, PROJ_W), lambda i: (i, 0)),
        compiler_params=_cparams(("arbitrary",)),
        name="inproj",
    )(x2d, shift, scale, w)


def _conv_kernel(y_ref, w_ref, b_ref, o_ref, pad_ref, *, n, grid2d, tiles):
    zeros = jnp.zeros((CONV_PAD, LANES), F32)
    pad_ref[0:CONV_PAD, :] = zeros
    pad_ref[CONV_PAD + n:CONV_PAD + n + CONV_PAD, :] = zeros
    rb = min(CONV_RB, n)
    col = lax.broadcasted_iota(jnp.int32, (rb, LANES), 0) % GRID_W
    not_first = col >= 1
    not_last = col <= GRID_W - 2
    row_taps = (0, 1, 2) if grid2d else (1,)
    for ct in range(tiles):
        lanes = slice(ct * LANES, (ct + 1) * LANES)
        pad_ref[CONV_PAD:CONV_PAD + n, :] = y_ref[0, :, lanes]
        k_scale = jnp.where(pl.program_id(1) * tiles + ct >= N_HEADS, D_HEAD ** -0.5, 1.0).astype(F32)
        bias = b_ref[:, lanes]
        for blk in range(n // rb):
            r0 = CONV_PAD + blk * rb
            acc = jnp.zeros((rb, LANES), F32)
            for dj in range(3):
                part = jnp.zeros((rb, LANES), F32)
                for di in row_taps:
                    off = (di - 1) * GRID_W + (dj - 1)
                    tap = w_ref[di * 3 + dj:di * 3 + dj + 1, lanes]
                    part = part + tap * pad_ref[r0 + off:r0 + off + rb, :]
                if grid2d and dj == 0:
                    part = jnp.where(not_first, part, 0.0)
                if grid2d and dj == 2:
                    part = jnp.where(not_last, part, 0.0)
                acc = acc + part
            val = _silu(acc + bias) * k_scale
            o_ref[0, blk * rb:(blk + 1) * rb, lanes] = val.astype(o_ref.dtype)


def _conv(y3d, conv_w9, conv_b, grid2d):
    b, n, _ = y3d.shape
    n_ct = 2 * D_MLSTM // LANES
    tiles = 1 if grid2d else n_ct
    w = tiles * LANES
    return pl.pallas_call(
        functools.partial(_conv_kernel, n=n, grid2d=grid2d, tiles=tiles),
        out_shape=jax.ShapeDtypeStruct((b, n, 2 * D_MLSTM), BF16),
        grid=(b, n_ct // tiles),
        in_specs=[pl.BlockSpec((1, n, w), lambda i, c: (i, 0, c)),
                  pl.BlockSpec((9, w), lambda i, c: (0, c)),
                  pl.BlockSpec((1, w), lambda i, c: (0, c))],
        out_specs=pl.BlockSpec((1, n, w), lambda i, c: (i, 0, c)),
        scratch_shapes=[pltpu.VMEM((n + 2 * CONV_PAD, LANES), F32)],
        compiler_params=_cparams(("arbitrary", "arbitrary")),
        name="conv2d" if grid2d else "conv1d",
    )(y3d, conv_w9, conv_b)


STATE_ROWS = 2 * D_HEAD
GATE_A, GATE_CM, GATE_B = 0, 1, 2
GATE_GROUP = 8


def _gates_kernel(g_ref, gb_ref, rv_ref, *, nc):
    u_i = lax.broadcasted_iota(jnp.int32, (CHUNK, CHUNK), 0)
    t_i = lax.broadcasted_iota(jnp.int32, (CHUNK, CHUNK), 1)
    prefix = (u_i <= t_i).astype(F32)
    suffix = (u_i >= t_i).astype(F32)
    grp = min(GATE_GROUP, nc)
    rows = grp * SUBLANES
    lane = lax.broadcasted_iota(jnp.int32, (rows, LANES), 1)
    is_fwd = lax.broadcasted_iota(jnp.int32, (rows, LANES), 0) % SUBLANES < N_HEADS
    h = N_HEADS

    def running_max(a):
        fwd, bwd = a, a
        sh = 1
        while sh < CHUNK:
            fwd = jnp.where(lane >= sh, jnp.maximum(fwd, pltpu.roll(fwd, sh, axis=1)), fwd)
            bwd = jnp.where(lane < CHUNK - sh, jnp.maximum(bwd, pltpu.roll(bwd, CHUNK - sh, axis=1)), bwd)
            sh *= 2
        return jnp.where(is_fwd, fwd, bwd)

    def body(i, carry):
        c0 = i * grp
        r0 = pl.multiple_of(c0 * CHUNK, CHUNK)
        pre = g_ref[0, pl.ds(r0, grp * CHUNK), :] + gb_ref[...]
        pre_t = jnp.concatenate(
            [pre[j * CHUNK:(j + 1) * CHUNK].T[:N_GATE_COLS, :] for j in range(grp)], axis=0)
        lf_t = -(jnp.maximum(-pre_t, 0.0) + jnp.log1p(jnp.exp(-jnp.abs(pre_t))))
        cum_f = jnp.dot(lf_t, prefix, precision=HIGHEST, preferred_element_type=F32)
        cum_b = jnp.dot(lf_t, suffix, precision=HIGHEST, preferred_element_type=F32)
        pick = lambda t, j, q: t[j * N_GATE_COLS + q * h:j * N_GATE_COLS + (q + 1) * h]
        b8 = jnp.concatenate([x for j in range(grp) for x in (pick(cum_f, j, 1), pick(cum_b, j, 3))], axis=0)
        li8 = jnp.concatenate([x for j in range(grp) for x in (pick(pre_t, j, 0), pick(pre_t, j, 2))], axis=0)
        a8 = li8 - b8
        as_chunks = lambda t: t.reshape(grp, SUBLANES, LANES)
        rv_ref[0, pl.ds(c0, grp), GATE_A] = as_chunks(a8)
        rv_ref[0, pl.ds(c0, grp), GATE_CM] = as_chunks(running_max(a8))
        rv_ref[0, pl.ds(c0, grp), GATE_B] = as_chunks(b8)
        return carry

    lax.fori_loop(0, nc // grp, body, 0)


def _gates(y3d, gate_b_row):
    b, n, _ = y3d.shape
    nc = n // CHUNK
    return pl.pallas_call(
        functools.partial(_gates_kernel, nc=nc),
        out_shape=jax.ShapeDtypeStruct((b, nc, 3, SUBLANES, LANES), F32),
        grid=(b,),
        in_specs=[pl.BlockSpec((1, n, LANES), lambda i: (i, 0, COL_G // LANES)),
                  pl.BlockSpec((1, LANES), lambda i: (0, 0))],
        out_specs=pl.BlockSpec((1, nc, 3, SUBLANES, LANES), lambda i: (i, 0, 0, 0, 0)),
        compiler_params=_cparams(("arbitrary",)),
        name="gates",
    )(y3d, gate_b_row)


def _mlstm_kernel(q_ref, k_ref, v_ref, o_ref, rv_ref, c0_ref, m0_ref, hg_ref,
                  out_ref, cfin_ref, mfin_ref, hf_ref, hb_ref, cst_ref, mst_ref, *, nc, heads):
    cst_ref[...] = c0_ref[0]
    mst_ref[...] = m0_ref[0]
    s_i = lax.broadcasted_iota(jnp.int32, (CHUNK, CHUNK), 0)
    t_i = lax.broadcasted_iota(jnp.int32, (CHUNK, CHUNK), 1)
    ones_row = (s_i[:STATE_ROWS - D_HEAD] == 0).astype(F32)

    def one_chunk(c, d, mask_t, last, h_ref, hh):
        r0 = pl.multiple_of(c * CHUNK, CHUNK)
        lanes = slice(hh * D_HEAD, (hh + 1) * D_HEAD)
        g = d * N_HEADS + pl.program_id(1) * heads + hh
        gate_row = lambda tbl: rv_ref[0, pl.ds(c, 1), tbl, pl.ds(g, 1), :].reshape(1, LANES)
        a_row, cm_row, b_row = gate_row(GATE_A), gate_row(GATE_CM), gate_row(GATE_B)
        q_t = q_ref[0, pl.ds(r0, CHUNK), lanes].T
        k = k_ref[0, pl.ds(r0, CHUNK), lanes]
        v_t_aug = jnp.concatenate([v_ref[0, pl.ds(r0, CHUNK), lanes].T, ones_row], axis=0)
        m_prev = mst_ref[hh, d, 0:1, :]
        m_row = jnp.maximum(m_prev, cm_row)
        a_bc = jnp.broadcast_to(a_row, (CHUNK, CHUNK)).T
        d_t = jnp.where(mask_t, jnp.exp(a_bc - m_row), 0.0)
        s_t = jnp.dot(k, q_t, preferred_element_type=F32) * d_t
        cs = cst_ref[hh, d]
        rhs = jnp.concatenate(
            [s_t.astype(BF16), (q_t.astype(F32) * jnp.exp(m_prev - m_row)).astype(BF16)], axis=0)
        lhs = jnp.concatenate([v_t_aug.astype(BF16), cs.astype(BF16)], axis=1)
        h_t = jnp.dot(lhs, rhs, preferred_element_type=F32)
        den = h_t[D_HEAD:D_HEAD + 1, :]
        floor = jnp.exp(-b_row - m_row)
        h_ref[hh, pl.ds(c, 1)] = (h_t[:D_HEAD] / jnp.maximum(jnp.abs(den), floor))[None]
        m_last = m_row[:, last:last + 1]
        wv_t = (v_t_aug * jnp.exp(a_row - m_last)).astype(BF16)
        cst_ref[hh, d] = jnp.exp(m_prev - m_last) * cs + jnp.dot(wv_t, k, preferred_element_type=F32)
        mst_ref[hh, d] = jnp.broadcast_to(b_row[:, last:last + 1] + m_last, (SUBLANES, LANES))

    def body(c, carry):
        for hh in range(heads):
            one_chunk(c, 0, s_i <= t_i, CHUNK - 1, hf_ref, hh)
            one_chunk(nc - 1 - c, 1, s_i >= t_i, 0, hb_ref, hh)
        return carry

    lax.fori_loop(0, nc, body, 0, unroll=min(8 // heads, nc))
    cfin_ref[0] = cst_ref[...]
    mfin_ref[0] = mst_ref[...]

    def readout(c, carry):
        r0 = pl.multiple_of(c * CHUNK, CHUNK)
        for hh in range(heads):
            lanes = slice(hh * D_HEAD, (hh + 1) * D_HEAD)
            h_t = hf_ref[hh, pl.ds(c, 1)][0] + hb_ref[hh, pl.ds(c, 1)][0]
            hc = h_t - jnp.mean(h_t, axis=0, keepdims=True)
            hn = (hc * lax.rsqrt(jnp.mean(hc * hc, axis=0, keepdims=True) + LN_EPS)).T
            gate = jax.nn.sigmoid(o_ref[0, pl.ds(r0, CHUNK), lanes])
            out_ref[0, pl.ds(r0, CHUNK), lanes] = (hn * hg_ref[:, lanes] * gate).astype(out_ref.dtype)
        return carry

    lax.fori_loop(0, nc, readout, 0, unroll=min(4, nc))


def _mlstm(qk, y3d, rowv, c0, m0, head_g):
    b, n, _ = qk.shape
    nc = n // CHUNK
    heads = N_HEADS if nc <= 4 else 1
    w = heads * D_HEAD
    v_blk = COL_V // w
    o_blk = COL_O // w
    tok = lambda off: pl.BlockSpec((1, n, w), lambda i, h: (i, 0, off + h))
    state_c = pl.BlockSpec((1, heads, 2, STATE_ROWS, D_HEAD), lambda i, h: (i, h, 0, 0, 0))
    state_m = pl.BlockSpec((1, heads, 2, SUBLANES, LANES), lambda i, h: (i, h, 0, 0, 0))
    return pl.pallas_call(
        functools.partial(_mlstm_kernel, nc=nc, heads=heads),
        out_shape=(jax.ShapeDtypeStruct((b, n, D_MLSTM), BF16),
                   jax.ShapeDtypeStruct((b, N_HEADS, 2, STATE_ROWS, D_HEAD), F32),
                   jax.ShapeDtypeStruct((b, N_HEADS, 2, SUBLANES, LANES), F32)),
        grid=(b, N_HEADS // heads),
        in_specs=[tok(0), tok(D_MLSTM // w), tok(v_blk), tok(o_blk),
                  pl.BlockSpec((1, nc, 3, SUBLANES, LANES), lambda i, h: (i, 0, 0, 0, 0)),
                  state_c, state_m,
                  pl.BlockSpec((1, w), lambda i, h: (0, h))],
        out_specs=(pl.BlockSpec((1, n, w), lambda i, h: (i, 0, h)), state_c, state_m),
        scratch_shapes=[pltpu.VMEM((heads, nc, D_HEAD, CHUNK), F32), pltpu.VMEM((heads, nc, D_HEAD, CHUNK), F32),
                        pltpu.VMEM((heads, 2, STATE_ROWS, D_HEAD), F32),
                        pltpu.VMEM((heads, 2, SUBLANES, LANES), F32)],
        compiler_params=_cparams(("arbitrary", "arbitrary")),
        name="mlstm",
    )(qk, qk, y3d, y3d, rowv, c0, m0, head_g)


def _dft_tables():
    r = DFT_R
    idx = np.arange(r)
    cg = np.cos(2 * np.pi * np.outer(idx, idx) / FOURIER_GROUP)
    sg = np.sin(2 * np.pi * np.outer(idx, idx) / FOURIER_GROUP)
    n_grp = D_FOURIER // FOURIER_GROUP
    bdc = np.kron(np.eye(n_grp), cg) / 8.0
    bds = np.kron(np.eye(n_grp), sg) / 8.0
    k1 = idx[:, None]
    a = idx[None, :]
    m1 = np.zeros((r, 2 * r, 2 * r))
    for b2 in range(r):
        th = 2 * np.pi * k1 * (r * a + b2) / (r * r)
        ec, es = np.cos(th), np.sin(th)
        m1[b2] = np.block([[ec, -es], [es, ec]]) / 8.0
    th2 = 2 * np.pi * np.outer(idx, idx) / r
    m2 = np.concatenate([np.cos(th2), -np.sin(th2)], axis=1) / 8.0
    as_bf16 = lambda t: jnp.asarray(t, dtype=F32).astype(BF16)
    return as_bf16(bdc), as_bf16(bds), as_bf16(m1), as_bf16(m2)


FOUR_BB = 8


N_SLABS = D_FOURIER // LANES


def _to_slabs(slab_ref, val):
    for s in range(N_SLABS):
        slab_ref[s] = val[:, s * LANES:(s + 1) * LANES]


def _strided_rows(slab_ref, start):
    return jnp.concatenate(
        [slab_ref[s, pl.ds(start, DFT_R, stride=FOUR_BB), :] for s in range(N_SLABS)], axis=1)


def _four_kernel(f_ref, bdc_ref, bds_ref, m1_ref, m2_ref, o_ref, z_ref, ps_ref, qs_ref, zs_ref, os_ref):
    r = DFT_R

    def stage1(blk, carry):
        b0 = pl.multiple_of(blk * FOUR_BB, FOUR_BB)
        fb = f_ref[0, :, pl.ds(b0, FOUR_BB), :].reshape(r * FOUR_BB, D_FOURIER).astype(BF16)
        _to_slabs(ps_ref, jnp.dot(fb, bdc_ref[...], preferred_element_type=F32))
        _to_slabs(qs_ref, jnp.dot(fb, bds_ref[...], preferred_element_type=F32))
        for j in range(FOUR_BB):
            x2 = jnp.concatenate([_strided_rows(ps_ref, j), _strided_rows(qs_ref, j)], axis=0).astype(BF16)
            zz = jnp.dot(m1_ref[b0 + j], x2, preferred_element_type=F32)
            z_ref[b0 + j] = _pack_pair(zz[:r], zz[r:])
        return carry

    lax.fori_loop(0, r // FOUR_BB, stage1, 0)

    def stage2(blk, carry):
        k0 = pl.multiple_of(blk * FOUR_BB, FOUR_BB)
        _to_slabs(zs_ref, z_ref[:, pl.ds(k0, FOUR_BB), :].reshape(r * FOUR_BB, D_FOURIER))
        for j in range(FOUR_BB):
            zc, zs = _unpack_pair_bf16(_strided_rows(zs_ref, j))
            o = jnp.dot(m2_ref[...], jnp.concatenate([zc, zs], axis=0), preferred_element_type=F32)
            for s in range(N_SLABS):
                os_ref[s, pl.ds(j, r, stride=FOUR_BB), :] = o[:, s * LANES:(s + 1) * LANES]
        for s in range(N_SLABS):
            o_ref[0, :, pl.ds(k0, FOUR_BB), s * LANES:(s + 1) * LANES] = os_ref[s].reshape(r, FOUR_BB, LANES)
        return carry

    lax.fori_loop(0, r // FOUR_BB, stage2, 0)


def _fourier(y4, tables):
    bdc, bds, m1, m2 = tables
    b = y4.shape[0]
    r = DFT_R
    slab = lambda dt: pltpu.VMEM((N_SLABS, r * FOUR_BB, LANES), dt)
    const = lambda *shape: pl.BlockSpec(shape, lambda i: (0,) * len(shape))
    return pl.pallas_call(
        _four_kernel,
        out_shape=jax.ShapeDtypeStruct((b, r, r, D_FOURIER), F32),
        grid=(b,),
        in_specs=[pl.BlockSpec((1, r, r, D_FOURIER), lambda i: (i, 0, 0, COL_F // D_FOURIER)),
                  const(D_FOURIER, D_FOURIER), const(D_FOURIER, D_FOURIER),
                  const(r, 2 * r, 2 * r), const(r, 2 * r)],
        out_specs=pl.BlockSpec((1, r, r, D_FOURIER), lambda i: (i, 0, 0, 0)),
        scratch_shapes=[pltpu.VMEM((r, r, D_FOURIER), jnp.int32), slab(F32), slab(F32), slab(jnp.int32), slab(F32)],
        compiler_params=_cparams(("arbitrary",), vmem=FOUR_VMEM_LIMIT),
        name="fourier",
    )(y4, bdc, bds, m1, m2)


def _outproj_kernel(x_ref, mr_ref, fo_ref, wm_ref, wf_ref, g1_ref, lg_ref, lb_ref,
                    sh_ref, sc_ref, wr_ref, x1_ref, xm_ref, s_ref):
    half = x_ref.shape[0] // 2
    for h in range(2):
        r = slice(h * half, (h + 1) * half)
        mix = (jnp.dot(mr_ref[r, :], wm_ref[...], preferred_element_type=F32)
               + jnp.dot(fo_ref[r, :].astype(BF16), wf_ref[...], preferred_element_type=F32))
        x1 = _layer_norm(DEEPNORM_ALPHA * x_ref[r, :] + g1_ref[0] * mix) * lg_ref[...] + lb_ref[...]
        x1_ref[r, :] = x1
        xm = (_layer_norm(x1) * (1.0 + sc_ref[0]) + sh_ref[0]).astype(BF16)
        xm_ref[r, :] = _pack_rows(xm)
        logits = jnp.dot(xm, wr_ref[...], preferred_element_type=F32)
        s_ref[r, :] = jax.nn.sigmoid(logits)


def _outproj(x2d, mread, four, w_m, w_f, gate1, ln_g, ln_b, shift2, scale2, w_router, n_per_batch):
    t, d = x2d.shape
    per = n_per_batch // TM
    tokd = lambda w: pl.BlockSpec((TM, w), lambda i: (i, 0))
    full = lambda r, c: pl.BlockSpec((r, c), lambda i: (0, 0))
    mod = pl.BlockSpec((1, 1, d), lambda i: (i // per, 0, 0))
    return pl.pallas_call(
        _outproj_kernel,
        out_shape=(jax.ShapeDtypeStruct((t, d), F32),
                   jax.ShapeDtypeStruct((t, PACK_W), jnp.int32),
                   jax.ShapeDtypeStruct((t, LANES), F32)),
        grid=(t // TM,),
        in_specs=[tokd(d), tokd(D_MLSTM), tokd(D_FOURIER), full(D_MLSTM, d), full(D_FOURIER, d),
                  mod, full(1, d), full(1, d), mod, mod, full(d, LANES)],
        out_specs=(tokd(d), tokd(PACK_W), tokd(LANES)),
        compiler_params=_cparams(("arbitrary",)),
        name="outproj",
    )(x2d, mread, four, w_m, w_f, gate1, ln_g, ln_b, shift2, scale2, w_router)


def _route_kernel(s_ref, b_ref, idx_ref, gw_ref, rank_ref, cnt_ref, carry_ref):
    @pl.when(pl.program_id(0) == 0)
    def _():
        carry_ref[...] = jnp.zeros_like(carry_ref)

    tm = s_ref.shape[0]
    reps = tm // LANES
    s_t = s_ref[...].T[:N_EXPERTS, :]
    e_i = lax.broadcasted_iota(jnp.int32, (N_EXPERTS, tm), 0).astype(F32)
    neg_inf = jnp.float32(-jnp.inf)
    sb = s_t + jnp.tile(b_ref[...], (1, reps))
    picks, sels = [], []
    taken = jnp.zeros((N_EXPERTS, tm), F32)
    for _ in range(TOP_K):
        best = jnp.max(sb, axis=0, keepdims=True)
        pick = jnp.min(jnp.where(sb == best, e_i, float(N_EXPERTS)), axis=0, keepdims=True)
        onehot = e_i == pick
        sels.append(jnp.sum(jnp.where(onehot, s_t, 0.0), axis=0, keepdims=True))
        picks.append(pick)
        sb = jnp.where(onehot, neg_inf, sb)
        taken = taken + onehot.astype(F32)
    u_i = lax.broadcasted_iota(jnp.int32, (tm, tm), 0)
    t_i = lax.broadcasted_iota(jnp.int32, (tm, tm), 1)
    before = (u_i < t_i).astype(BF16)
    taken_bf = taken.astype(BF16)
    rank_all = jnp.tile(carry_ref[...], (1, reps)) + jnp.dot(taken_bf, before, preferred_element_type=F32)
    carry_ref[...] = carry_ref[...] + jnp.dot(taken_bf, jnp.ones((tm, LANES), BF16),
                                              preferred_element_type=F32)
    cnt_ref[...] = carry_ref[...]
    total = sels[0]
    for v in sels[1:]:
        total = total + v
    ranks = [jnp.sum(jnp.where(e_i == picks[j], rank_all, 0.0), axis=0, keepdims=True) for j in range(TOP_K)]
    idx_ref[...] = jnp.concatenate(picks, axis=0).astype(jnp.int32)
    rank_ref[...] = jnp.concatenate(ranks, axis=0).astype(jnp.int32)
    gw_t = jnp.concatenate([ROUTED_SCALE * v / total for v in sels]
                           + [jnp.zeros((LANES - TOP_K, tm), F32)], axis=0)
    gw_ref[...] = gw_t.T[:, :TOP_K]


def _route(scores, bias_rep):
    t = scores.shape[0]
    tok = lambda w: pl.BlockSpec((TM_ROUTE, w), lambda i: (i, 0))
    slot_major = pl.BlockSpec((TOP_K, TM_ROUTE), lambda i: (0, i))
    per_expert = pl.BlockSpec((N_EXPERTS, LANES), lambda i: (0, 0))
    return pl.pallas_call(
        _route_kernel,
        out_shape=(jax.ShapeDtypeStruct((TOP_K, t), jnp.int32),
                   jax.ShapeDtypeStruct((t, TOP_K), F32),
                   jax.ShapeDtypeStruct((TOP_K, t), jnp.int32),
                   jax.ShapeDtypeStruct((N_EXPERTS, LANES), F32)),
        grid=(t // TM_ROUTE,),
        in_specs=[tok(LANES), per_expert],
        out_specs=(slot_major, tok(TOP_K), slot_major, per_expert),
        scratch_shapes=[pltpu.VMEM((N_EXPERTS, LANES), F32)],
        compiler_params=_cparams(("arbitrary",)),
        name="route",
    )(scores, bias_rep)


TN_DEST = 2048


def _dest_kernel(ps_ref, idx_ref, rank_ref, dest_ref):
    idx = idx_ref[...]
    start = jnp.zeros(idx.shape, jnp.int32)
    for e in range(N_EXPERTS):
        start = jnp.where(idx == e, ps_ref[e], start)
    dest_ref[...] = start + rank_ref[...]


def _dest(pad_start, idx_t, rank_t):
    t = idx_t.shape[1]
    blk = lambda: pl.BlockSpec((TOP_K, TN_DEST), lambda i, ps: (0, i))
    grid_spec = pltpu.PrefetchScalarGridSpec(
        num_scalar_prefetch=1, grid=(t // TN_DEST,), in_specs=[blk(), blk()], out_specs=blk())
    return pl.pallas_call(
        _dest_kernel,
        out_shape=jax.ShapeDtypeStruct((TOP_K, t), jnp.int32),
        grid_spec=grid_spec,
        compiler_params=_cparams(("arbitrary",)),
        name="dest",
    )(pad_start, idx_t, rank_t)


def _sc_worker_id(info):
    return lax.axis_index("s") * info.num_cores + lax.axis_index("c")


def _sc_dispatch(xm, dest3, n_rows):
    t, w = xm.shape
    info = plsc.get_sparse_core_info()
    n_workers = info.num_cores * info.num_subcores
    per_worker = t // SC_ROWS // n_workers
    mesh = plsc.VectorSubcoreMesh(core_axis_name="c", subcore_axis_name="s")

    @functools.partial(
        pl.kernel, mesh=mesh,
        out_type=jax.ShapeDtypeStruct((n_rows, w), xm.dtype),
        scratch_types=[pltpu.VMEM((TOP_K, SC_ROWS), jnp.int32), pltpu.VMEM((SC_ROWS, w), xm.dtype)],
        name="sc_dispatch",
    )
    def run(xm_hbm, dest_hbm, xs_hbm, idx_v, rows_v):
        first = _sc_worker_id(info) * per_worker

        @pl.loop(0, per_worker)
        def _(ci):
            chunk = first + ci
            pltpu.sync_copy(dest_hbm.at[chunk], idx_v)
            pltpu.sync_copy(xm_hbm.at[pl.ds(chunk * SC_ROWS, SC_ROWS)], rows_v)
            for j in range(TOP_K):
                pltpu.sync_copy(rows_v, xs_hbm.at[idx_v.at[j]])

    return run(xm, dest3)


def _sc_gather(table, idx2):
    n_chunks, rows = idx2.shape
    w = table.shape[1]
    info = plsc.get_sparse_core_info()
    n_workers = info.num_cores * info.num_subcores
    per_worker = n_chunks // n_workers
    assert per_worker % 2 == 0
    mesh = plsc.VectorSubcoreMesh(core_axis_name="c", subcore_axis_name="s")

    @functools.partial(
        pl.kernel, mesh=mesh,
        out_type=jax.ShapeDtypeStruct((n_chunks * rows, w), table.dtype),
        scratch_types=[pltpu.VMEM((2, rows), jnp.int32), pltpu.VMEM((2, rows, w), table.dtype),
                       pltpu.SemaphoreType.DMA, pltpu.SemaphoreType.DMA],
        name="sc_gather",
    )
    def run(table_hbm, idx_hbm, out_hbm, idx_v, rows_v, sem0, sem1):
        first = _sc_worker_id(info) * per_worker
        sems = (sem0, sem1)

        def gather(b):
            return pltpu.make_async_copy(table_hbm.at[idx_v.at[b]], rows_v.at[b], sems[b])

        def start(chunk, b):
            pltpu.sync_copy(idx_hbm.at[chunk], idx_v.at[b])
            gather(b).start()

        def finish(chunk, b):
            gather(b).wait()
            pltpu.sync_copy(rows_v.at[b], out_hbm.at[pl.ds(chunk * rows, rows)])

        start(first, 0)

        @pl.loop(0, per_worker, step=2)
        def _(ci):
            chunk = first + ci
            start(chunk + 1, 1)
            finish(chunk, 0)

            @pl.when(ci + 2 < per_worker)
            def _():
                start(chunk + 2, 0)

            finish(chunk + 1, 1)

    return run(table, idx2)


def _expert_kernel(first_ref, count_ref, used_ref,
                   xs_hbm, wgu_ref, wd_ref, y_hbm, wgu_bf, wd_bf, xbuf, ybuf, in_sem, out_sem):
    e = pl.program_id(0)
    n_used = used_ref[0]
    first = first_ref[e]
    n_blocks = count_ref[e]

    def rows(g):
        return pl.ds(pl.multiple_of(g * ROW_BLK, ROW_BLK), ROW_BLK)

    def read(g):
        slot = g % EXP_IN_SLOTS
        return pltpu.make_async_copy(xs_hbm.at[rows(g), :], xbuf.at[slot], in_sem.at[slot])

    def write(g):
        slot = g % EXP_OUT_SLOTS
        return pltpu.make_async_copy(ybuf.at[slot], y_hbm.at[rows(g), :], out_sem.at[slot])

    @pl.when(e == 0)
    def _():
        for g0 in range(EXP_AHEAD):
            @pl.when(g0 < n_used)
            def _():
                read(g0).start()

    @pl.when(n_blocks > 0)
    def _():
        wgu_bf[...] = wgu_ref[0].astype(BF16)
        wd_bf[...] = wd_ref[0].astype(BF16)

    def acquire(g):
        @pl.when(g + EXP_AHEAD < n_used)
        def _():
            read(g + EXP_AHEAD).start()

        read(g).wait()

        @pl.when(g >= EXP_OUT_SLOTS)
        def _():
            write(g - EXP_OUT_SLOTS).wait()

    def compute(g):
        in_slot = g % EXP_IN_SLOTS
        out_slot = g % EXP_OUT_SLOTS
        for h in range(ROW_BLK // ROW_HALF):
            r0 = h * ROW_HALF
            x = _unpack_bf16(xbuf[in_slot, r0:r0 + ROW_HALF, :])
            gu = jnp.dot(x, wgu_bf[...], preferred_element_type=F32)
            act = _silu(gu[:, :D_EXPERT]) * gu[:, D_EXPERT:]
            yv = jnp.dot(act.astype(BF16), wd_bf[...], preferred_element_type=F32)
            words = pltpu.pack_elementwise([yv[:, :PACK_W], yv[:, PACK_W:]], packed_dtype=BF16)
            ybuf[out_slot, r0:r0 + ROW_HALF, :] = lax.bitcast_convert_type(words, jnp.int32)

    def pair(k, carry):
        g = first + 2 * k
        acquire(g)
        acquire(g + 1)
        compute(g)
        compute(g + 1)
        write(g).start()
        write(g + 1).start()
        return carry

    lax.fori_loop(0, n_blocks // 2, pair, 0)

    @pl.when(n_blocks % 2 == 1)
    def _():
        g = first + n_blocks - 1
        acquire(g)
        compute(g)
        write(g).start()

    @pl.when(e == pl.num_programs(0) - 1)
    def _():
        for back in range(EXP_OUT_SLOTS, 0, -1):
            @pl.when(n_used >= back)
            def _():
                write(n_used - back).wait()


def _experts(first_block, n_blocks, n_used, xs, w_gu, w_down):
    n_rows, w = xs.shape
    d = D_MODEL
    grid_spec = pltpu.PrefetchScalarGridSpec(
        num_scalar_prefetch=3,
        grid=(N_EXPERTS,),
        in_specs=[pl.BlockSpec(memory_space=pl.ANY),
                  pl.BlockSpec((1, d, 2 * D_EXPERT), lambda e, *_: (e, 0, 0)),
                  pl.BlockSpec((1, D_EXPERT, d), lambda e, *_: (e, 0, 0))],
        out_specs=pl.BlockSpec(memory_space=pl.ANY),
        scratch_shapes=[pltpu.VMEM((d, 2 * D_EXPERT), BF16), pltpu.VMEM((D_EXPERT, d), BF16),
                        pltpu.VMEM((EXP_IN_SLOTS, ROW_BLK, w), jnp.int32),
                        pltpu.VMEM((EXP_OUT_SLOTS, ROW_BLK, w), jnp.int32),
                        pltpu.SemaphoreType.DMA((EXP_IN_SLOTS,)),
                        pltpu.SemaphoreType.DMA((EXP_OUT_SLOTS,))],
    )
    return pl.pallas_call(
        _expert_kernel,
        out_shape=jax.ShapeDtypeStruct((n_rows, w), jnp.int32),
        grid_spec=grid_spec,
        compiler_params=_cparams(("arbitrary",)),
        name="experts",
    )(first_block, n_blocks, n_used, xs, w_gu, w_down)


TM_COMBINE = 512


def _combine_kernel(*refs):
    yg_refs = refs[:TOP_K]
    gw_ref, x1_ref, xm_ref, g2_ref, wsgu_ref, wsd_ref, lg_ref, lb_ref, out_ref = refs[TOP_K:]
    gu = jnp.dot(_unpack_bf16(xm_ref[...]), wsgu_ref[...], preferred_element_type=F32)
    act = _silu(gu[:, :D_EXPERT]) * gu[:, D_EXPERT:]
    acc = jnp.dot(act.astype(BF16), wsd_ref[...], preferred_element_type=F32)
    gw = gw_ref[...]
    for j in range(TOP_K):
        acc = acc + gw[:, j:j + 1] * _unpack_rows(yg_refs[j][...])
    z = DEEPNORM_ALPHA * x1_ref[...] + g2_ref[0] * acc
    out_ref[...] = _layer_norm(z) * lg_ref[...] + lb_ref[...]


def _combine(yg, gw, x1, xm, gate2, w_sgu, w_sd, ln_g, ln_b, n_per_batch):
    t, d = x1.shape
    tm = TM_COMBINE
    per = n_per_batch // tm
    steps = t // tm
    tok = lambda w: pl.BlockSpec((tm, w), lambda i: (i, 0))
    full = lambda r, c: pl.BlockSpec((r, c), lambda i: (0, 0))
    slot = lambda j: pl.BlockSpec((tm, PACK_W), lambda i: (j * steps + i, 0))
    return pl.pallas_call(
        _combine_kernel,
        out_shape=jax.ShapeDtypeStruct((t, d), F32),
        grid=(steps,),
        in_specs=[slot(j) for j in range(TOP_K)]
        + [tok(TOP_K), tok(d), tok(PACK_W), pl.BlockSpec((1, 1, d), lambda i: (i // per, 0, 0)),
           full(d, 2 * D_EXPERT), full(D_EXPERT, d), full(1, d), full(1, d)],
        out_specs=tok(d),
        compiler_params=_cparams(("arbitrary",)),
        name="combine",
    )(*([yg] * TOP_K), gw, x1, xm, gate2, w_sgu, w_sd, ln_g, ln_b)


def _mixer_heads(stream, shift, scale, w_in_k, conv_w9, conv_b, gate_b_row, head_g, c0, m0, grid2d):
    b, n, d = stream.shape
    y = _inproj(stream.reshape(b * n, d), shift, scale, w_in_k, n).reshape(b, n, PROJ_W)
    qk = _conv(y, conv_w9, conv_b, grid2d)
    mread, c_fin, m_fin = _mlstm(qk, y, _gates(y, gate_b_row), c0, m0, head_g)
    return y, mread, c_fin, m_fin


def _moe(xm, x1, scores, bias_row, gate2, w_gu, w_down, w_sgu, w_sd, ln_g, ln_b, n_per_batch):
    t = xm.shape[0]
    n_assign = t * TOP_K
    n_rows = n_assign + N_EXPERTS * ROW_BLK
    idx_t, gw, rank_t, counts = _route(scores, bias_row)
    sizes = counts[:, 0].astype(jnp.int32)
    padded = (sizes + ROW_BLK - 1) // ROW_BLK * ROW_BLK
    pad_end = jnp.cumsum(padded)
    pad_start = (pad_end - padded).astype(jnp.int32)
    dest_t = _dest(pad_start, idx_t, rank_t)
    dest3 = dest_t.reshape(TOP_K, t // SC_ROWS, SC_ROWS).transpose(1, 0, 2)

    xs = _sc_dispatch(xm, dest3, n_rows)
    yb = _experts(pad_start // ROW_BLK, (padded // ROW_BLK).astype(jnp.int32),
                  (pad_end[-1:] // ROW_BLK).astype(jnp.int32), xs, w_gu, w_down)
    yg = _sc_gather(yb, dest_t.reshape(n_assign // SC_GATHER_ROWS, SC_GATHER_ROWS))
    return _combine(yg, gw, x1, xm, gate2, w_sgu, w_sd, ln_g, ln_b, n_per_batch)


def kernel(x, c, ctx, c_ctx, w_ada, b_ada, w_in, conv_w, conv_b, gate_b, head_g, w_out, ln1_g, ln1_b,
           w_router, router_bias, w_expert_gu, w_expert_down, w_shared_gu, w_shared_down, ln2_g, ln2_b):
    bsz, n, d = x.shape
    depth = w_ada.shape[0]
    assert depth == 1
    t = bsz * n
    tables = _dft_tables()
    for l in range(depth):
        cvec = jnp.zeros((SUBLANES, d), F32).at[:bsz].set(c).at[bsz].set(c_ctx)
        mod = _ada(cvec, w_ada[l], b_ada[l][None, :])
        mods = mod.reshape(SUBLANES, 6, 1, d)
        mod_x = [mods[:bsz, j] for j in range(6)]
        mod_c = [mods[bsz:bsz + 1, j] for j in range(2)]

        wl = w_in[l]
        g0 = 4 * D_MLSTM
        w_in_k = jnp.concatenate(
            [wl[:, :g0], wl[:, g0 + N_GATE_COLS:], wl[:, g0:g0 + N_GATE_COLS],
             jnp.zeros((d, LANES - N_GATE_COLS), F32)], axis=1).astype(BF16)
        conv_w9 = conv_w[l].reshape(9, 2 * D_MLSTM)
        conv_b_row = conv_b[l][None, :]
        gate_b_row = jnp.zeros((1, LANES), F32).at[0, :N_GATE_COLS].set(gate_b[l].reshape(-1))
        head_g_row = head_g[l][None, :]

        c0 = jnp.zeros((bsz, N_HEADS, 2, STATE_ROWS, D_HEAD), F32)
        m0 = jnp.zeros((bsz, N_HEADS, 2, SUBLANES, LANES), F32)
        _, _, c_ctx_fin, m_ctx_fin = _mixer_heads(ctx, mod_c[0], mod_c[1], w_in_k, conv_w9, conv_b_row,
                                                  gate_b_row, head_g_row, c0, m0, grid2d=False)
        y, mread, _, _ = _mixer_heads(x, mod_x[0], mod_x[1], w_in_k, conv_w9, conv_b_row,
                                      gate_b_row, head_g_row, c_ctx_fin, m_ctx_fin, grid2d=True)

        four = _fourier(y.reshape(bsz, DFT_R, DFT_R, PROJ_W), tables).reshape(t, D_FOURIER)

        w_o = w_out[l].astype(BF16)
        w_r = jnp.zeros((d, LANES), F32).at[:, :N_EXPERTS].set(w_router[l]).astype(BF16)
        x2d = x.reshape(t, d)
        x1, xm, scores = _outproj(x2d, mread.reshape(t, D_MLSTM), four, w_o[:D_MLSTM], w_o[D_MLSTM:],
                                  mod_x[2], ln1_g[l][None, :], ln1_b[l][None, :], mod_x[3], mod_x[4],
                                  w_r, n)

        bias_rep = jnp.broadcast_to(router_bias[l][:, None], (N_EXPERTS, LANES))
        x = _moe(xm, x1, scores, bias_rep, mod_x[5], w_expert_gu[l], w_expert_down[l],
                 w_shared_gu[l].astype(BF16), w_shared_down[l].astype(BF16),
                 ln2_g[l][None, :], ln2_b[l][None, :], n).reshape(bsz, n, d)
    return x
```

```python
import functools

import numpy as np
import jax
import jax.numpy as jnp
from jax import lax
from jax.experimental import pallas as pl
from jax.experimental.pallas import tpu as pltpu
from jax.experimental.pallas import tpu_sc as plsc

F32 = jnp.float32
BF16 = jnp.bfloat16
HIGHEST = lax.Precision.HIGHEST

D_MODEL = 1024
GRID_W = 64
N_HEADS = 4
D_HEAD = 128
D_MLSTM = N_HEADS * D_HEAD
D_FOURIER = 512
FOURIER_GROUP = 64
N_GATE_COLS = 16
CHUNK = 128
N_EXPERTS = 64
TOP_K = 8
D_EXPERT = 256
ROUTED_SCALE = 2.5
DEEPNORM_ALPHA = 2.0 ** 0.25
LN_EPS = 1e-5

LANES = 128
SUBLANES = 8
VMEM_LIMIT = 48 * 1024 * 1024
FOUR_VMEM_LIMIT = 56 * 1024 * 1024

COL_QK = 0
COL_V = 2 * D_MLSTM
COL_O = 3 * D_MLSTM
COL_F = 4 * D_MLSTM
COL_G = COL_F + D_FOURIER
PROJ_W = COL_G + LANES

TM = 1024
TM_ROUTE = 512
ROW_BLK = 512
ROW_HALF = 256
EXP_AHEAD = 5
EXP_IN_SLOTS = 8
EXP_OUT_SLOTS = 4
PACK_W = D_MODEL // 2
SC_ROWS = 128
SC_GATHER_ROWS = 64
CONV_PAD = 72
CONV_RB = 256
DFT_R = 64


def _cparams(sem, vmem=VMEM_LIMIT):
    return pltpu.CompilerParams(dimension_semantics=sem, vmem_limit_bytes=vmem)


def _layer_norm(x):
    mu = jnp.mean(x, axis=-1, keepdims=True)
    xc = x - mu
    var = jnp.mean(xc * xc, axis=-1, keepdims=True)
    return xc * lax.rsqrt(var + LN_EPS)


def _silu(x):
    return x * jax.nn.sigmoid(x)


def _pack_pair(lo, hi):
    as_bits = lambda u: lax.bitcast_convert_type(u.astype(BF16).astype(F32), jnp.uint32)
    word = as_bits(hi) | lax.shift_right_logical(as_bits(lo), jnp.uint32(16))
    return lax.bitcast_convert_type(word, jnp.int32)


def _unpack_pair_bf16(w):
    lo = lax.bitcast_convert_type(w.astype(jnp.int16), BF16)
    hi = lax.bitcast_convert_type(lax.shift_right_logical(w, jnp.int32(16)).astype(jnp.int16), BF16)
    return lo, hi


def _pack_rows(v):
    return _pack_pair(v[:, :PACK_W], v[:, PACK_W:])


def _unpack_bf16(w):
    return jnp.concatenate(_unpack_pair_bf16(w), axis=-1)


def _unpack_rows(w):
    u = lax.bitcast_convert_type(w, jnp.uint32)
    lo = lax.bitcast_convert_type(lax.shift_left(u, jnp.uint32(16)), F32)
    hi = lax.bitcast_convert_type(u & jnp.uint32(0xFFFF0000), F32)
    return jnp.concatenate([lo, hi], axis=1)


def _ada_kernel(c_ref, w_ref, b_ref, o_ref):
    s = _silu(c_ref[...]).astype(BF16)
    o_ref[...] = jnp.dot(s, w_ref[...].astype(BF16), preferred_element_type=F32) + b_ref[...]


def _ada(cvec, w_ada, b_ada):
    rows, d = cvec.shape
    n_out = w_ada.shape[1]
    tn = 2048
    return pl.pallas_call(
        _ada_kernel,
        out_shape=jax.ShapeDtypeStruct((rows, n_out), F32),
        grid=(n_out // tn,),
        in_specs=[pl.BlockSpec((rows, d), lambda j: (0, 0)),
                  pl.BlockSpec((d, tn), lambda j: (0, j)),
                  pl.BlockSpec((1, tn), lambda j: (0, j))],
        out_specs=pl.BlockSpec((rows, tn), lambda j: (0, j)),
        compiler_params=_cparams(("arbitrary",)),
        name="ada",
    )(cvec, w_ada, b_ada)


def _inproj_kernel(x_ref, sh_ref, sc_ref, w_ref, y_ref):
    half = x_ref.shape[0] // 2
    for h in range(2):
        r = slice(h * half, (h + 1) * half)
        u = _layer_norm(x_ref[r, :]) * (1.0 + sc_ref[0]) + sh_ref[0]
        y_ref[r, :] = jnp.dot(u.astype(BF16), w_ref[...], preferred_element_type=F32)


def _inproj(x2d, shift, scale, w, n_per_batch):
    t, d = x2d.shape
    if shift.shape[0] == 1:
        tm = min(TM, t)
        per = t // tm
    else:
        tm = min(TM, n_per_batch)
        per = n_per_batch // tm
    return pl.pallas_call(
        _inproj_kernel,
        out_shape=jax.ShapeDtypeStruct((t, PROJ_W), F32),
        grid=(t // tm,),
        in_specs=[pl.BlockSpec((tm, d), lambda i: (i, 0)),
                  pl.BlockSpec((1, 1, d), lambda i: (i // per, 0, 0)),
                  pl.BlockSpec((1, 1, d), lambda i: (i // per, 0, 0)),
                  pl.BlockSpec((d, PROJ_W), lambda i: (0, 0))],
        out_specs=pl.BlockSpec((tm, PROJ_W), lambda i: (i, 0)),
        compiler_params=_cparams(("arbitrary",)),
        name="inproj",
    )(x2d, shift, scale, w)


def _conv_kernel(y_ref, w_ref, b_ref, o_ref, pad_ref, *, n, grid2d, tiles):
    zeros = jnp.zeros((CONV_PAD, LANES), F32)
    pad_ref[0:CONV_PAD, :] = zeros
    pad_ref[CONV_PAD + n:CONV_PAD + n + CONV_PAD, :] = zeros
    rb = min(CONV_RB, n)
    col = lax.broadcasted_iota(jnp.int32, (rb, LANES), 0) % GRID_W
    not_first = col >= 1
    not_last = col <= GRID_W - 2
    row_taps = (0, 1, 2) if grid2d else (1,)
    for ct in range(tiles):
        lanes = slice(ct * LANES, (ct + 1) * LANES)
        pad_ref[CONV_PAD:CONV_PAD + n, :] = y_ref[0, :, lanes]
        k_scale = jnp.where(pl.program_id(1) * tiles + ct >= N_HEADS, D_HEAD ** -0.5, 1.0).astype(F32)
        bias = b_ref[:, lanes]
        for blk in range(n // rb):
            r0 = CONV_PAD + blk * rb
            acc = jnp.zeros((rb, LANES), F32)
            for dj in range(3):
                part = jnp.zeros((rb, LANES), F32)
                for di in row_taps:
                    off = (di - 1) * GRID_W + (dj - 1)
                    tap = w_ref[di * 3 + dj:di * 3 + dj + 1, lanes]
                    part = part + tap * pad_ref[r0 + off:r0 + off + rb, :]
                if grid2d and dj == 0:
                    part = jnp.where(not_first, part, 0.0)
                if grid2d and dj == 2:
                    part = jnp.where(not_last, part, 0.0)
                acc = acc + part
            val = _silu(acc + bias) * k_scale
            o_ref[0, blk * rb:(blk + 1) * rb, lanes] = val.astype(o_ref.dtype)


def _conv(y3d, conv_w9, conv_b, grid2d):
    b, n, _ = y3d.shape
    n_ct = 2 * D_MLSTM // LANES
    tiles = 1 if grid2d else n_ct
    w = tiles * LANES
    return pl.pallas_call(
        functools.partial(_conv_kernel, n=n, grid2d=grid2d, tiles=tiles),
        out_shape=jax.ShapeDtypeStruct((b, n, 2 * D_MLSTM), BF16),
        grid=(b, n_ct // tiles),
        in_specs=[pl.BlockSpec((1, n, w), lambda i, c: (i, 0, c)),
                  pl.BlockSpec((9, w), lambda i, c: (0, c)),
                  pl.BlockSpec((1, w), lambda i, c: (0, c))],
        out_specs=pl.BlockSpec((1, n, w), lambda i, c: (i, 0, c)),
        scratch_shapes=[pltpu.VMEM((n + 2 * CONV_PAD, LANES), F32)],
        compiler_params=_cparams(("arbitrary", "arbitrary")),
        name="conv2d" if grid2d else "conv1d",
    )(y3d, conv_w9, conv_b)


STATE_ROWS = 2 * D_HEAD
GATE_A, GATE_CM, GATE_B = 0, 1, 2
GATE_GROUP = 16


def _gates_kernel(g_ref, gb_ref, rv_ref, *, nc):
    u_i = lax.broadcasted_iota(jnp.int32, (CHUNK, CHUNK), 0)
    t_i = lax.broadcasted_iota(jnp.int32, (CHUNK, CHUNK), 1)
    prefix = (u_i <= t_i).astype(F32)
    suffix = (u_i >= t_i).astype(F32)
    grp = min(GATE_GROUP, nc)
    rows = grp * SUBLANES
    lane = lax.broadcasted_iota(jnp.int32, (rows, LANES), 1)
    is_fwd = lax.broadcasted_iota(jnp.int32, (rows, LANES), 0) % SUBLANES < N_HEADS
    h = N_HEADS

    def running_max(a):
        fwd, bwd = a, a
        sh = 1
        while sh < CHUNK:
            fwd = jnp.where(lane >= sh, jnp.maximum(fwd, pltpu.roll(fwd, sh, axis=1)), fwd)
            bwd = jnp.where(lane < CHUNK - sh, jnp.maximum(bwd, pltpu.roll(bwd, CHUNK - sh, axis=1)), bwd)
            sh *= 2
        return jnp.where(is_fwd, fwd, bwd)

    def body(i, carry):
        c0 = i * grp
        r0 = pl.multiple_of(c0 * CHUNK, CHUNK)
        pre = g_ref[0, pl.ds(r0, grp * CHUNK), :] + gb_ref[...]
        pre_t = jnp.concatenate(
            [pre[j * CHUNK:(j + 1) * CHUNK].T[:N_GATE_COLS, :] for j in range(grp)], axis=0)
        lf_t = -(jnp.maximum(-pre_t, 0.0) + jnp.log1p(jnp.exp(-jnp.abs(pre_t))))
        cum_f = jnp.dot(lf_t, prefix, precision=HIGHEST, preferred_element_type=F32)
        cum_b = jnp.dot(lf_t, suffix, precision=HIGHEST, preferred_element_type=F32)
        pick = lambda t, j, q: t[j * N_GATE_COLS + q * h:j * N_GATE_COLS + (q + 1) * h]
        b8 = jnp.concatenate([x for j in range(grp) for x in (pick(cum_f, j, 1), pick(cum_b, j, 3))], axis=0)
        li8 = jnp.concatenate([x for j in range(grp) for x in (pick(pre_t, j, 0), pick(pre_t, j, 2))], axis=0)
        a8 = li8 - b8
        as_chunks = lambda t: t.reshape(grp, SUBLANES, LANES)
        rv_ref[0, pl.ds(c0, grp), GATE_A] = as_chunks(a8)
        rv_ref[0, pl.ds(c0, grp), GATE_CM] = as_chunks(running_max(a8))
        rv_ref[0, pl.ds(c0, grp), GATE_B] = as_chunks(b8)
        return carry

    lax.fori_loop(0, nc // grp, body, 0)


def _gates(y3d, gate_b_row):
    b, n, _ = y3d.shape
    nc = n // CHUNK
    return pl.pallas_call(
        functools.partial(_gates_kernel, nc=nc),
        out_shape=jax.ShapeDtypeStruct((b, nc, 3, SUBLANES, LANES), F32),
        grid=(b,),
        in_specs=[pl.BlockSpec((1, n, LANES), lambda i: (i, 0, COL_G // LANES)),
                  pl.BlockSpec((1, LANES), lambda i: (0, 0))],
        out_specs=pl.BlockSpec((1, nc, 3, SUBLANES, LANES), lambda i: (i, 0, 0, 0, 0)),
        compiler_params=_cparams(("arbitrary",)),
        name="gates",
    )(y3d, gate_b_row)


def _mlstm_kernel(q_ref, k_ref, v_ref, o_ref, rv_ref, c0_ref, m0_ref, hg_ref,
                  out_ref, cfin_ref, mfin_ref, hf_ref, hb_ref, cst_ref, mst_ref, *, nc, heads):
    cst_ref[...] = c0_ref[0]
    mst_ref[...] = m0_ref[0]
    s_i = lax.broadcasted_iota(jnp.int32, (CHUNK, CHUNK), 0)
    t_i = lax.broadcasted_iota(jnp.int32, (CHUNK, CHUNK), 1)
    ones_row = (s_i[:STATE_ROWS - D_HEAD] == 0).astype(F32)

    def one_chunk(c, d, mask_t, last, h_ref, hh):
        r0 = pl.multiple_of(c * CHUNK, CHUNK)
        lanes = slice(hh * D_HEAD, (hh + 1) * D_HEAD)
        g = d * N_HEADS + pl.program_id(1) * heads + hh
        gate_row = lambda tbl: rv_ref[0, pl.ds(c, 1), tbl, pl.ds(g, 1), :].reshape(1, LANES)
        a_row, cm_row, b_row = gate_row(GATE_A), gate_row(GATE_CM), gate_row(GATE_B)
        q_t = q_ref[0, pl.ds(r0, CHUNK), lanes].T
        k = k_ref[0, pl.ds(r0, CHUNK), lanes]
        v_t_aug = jnp.concatenate([v_ref[0, pl.ds(r0, CHUNK), lanes].T, ones_row], axis=0)
        m_prev = mst_ref[hh, d, 0:1, :]
        m_row = jnp.maximum(m_prev, cm_row)
        a_bc = jnp.broadcast_to(a_row, (CHUNK, CHUNK)).T
        d_t = jnp.where(mask_t, jnp.exp(a_bc - m_row), 0.0)
        s_t = jnp.dot(k, q_t, preferred_element_type=F32) * d_t
        cs = cst_ref[hh, d]
        rhs = jnp.concatenate(
            [s_t.astype(BF16), (q_t.astype(F32) * jnp.exp(m_prev - m_row)).astype(BF16)], axis=0)
        lhs = jnp.concatenate([v_t_aug.astype(BF16), cs.astype(BF16)], axis=1)
        h_t = jnp.dot(lhs, rhs, preferred_element_type=F32)
        den = h_t[D_HEAD:D_HEAD + 1, :]
        floor = jnp.exp(-b_row - m_row)
        h_ref[hh, pl.ds(c, 1)] = (h_t[:D_HEAD] / jnp.maximum(jnp.abs(den), floor))[None]
        m_last = m_row[:, last:last + 1]
        wv_t = (v_t_aug * jnp.exp(a_row - m_last)).astype(BF16)
        cst_ref[hh, d] = jnp.exp(m_prev - m_last) * cs + jnp.dot(wv_t, k, preferred_element_type=F32)
        mst_ref[hh, d] = jnp.broadcast_to(b_row[:, last:last + 1] + m_last, (SUBLANES, LANES))

    def body(c, carry):
        for hh in range(heads):
            one_chunk(c, 0, s_i <= t_i, CHUNK - 1, hf_ref, hh)
            one_chunk(nc - 1 - c, 1, s_i >= t_i, 0, hb_ref, hh)
        return carry

    lax.fori_loop(0, nc, body, 0, unroll=min(8 // heads, nc))
    cfin_ref[0] = cst_ref[...]
    mfin_ref[0] = mst_ref[...]

    def readout(c, carry):
        r0 = pl.multiple_of(c * CHUNK, CHUNK)
        for hh in range(heads):
            lanes = slice(hh * D_HEAD, (hh + 1) * D_HEAD)
            h_t = hf_ref[hh, pl.ds(c, 1)][0] + hb_ref[hh, pl.ds(c, 1)][0]
            hc = h_t - jnp.mean(h_t, axis=0, keepdims=True)
            hn = (hc * lax.rsqrt(jnp.mean(hc * hc, axis=0, keepdims=True) + LN_EPS)).T
            gate = jax.nn.sigmoid(o_ref[0, pl.ds(r0, CHUNK), lanes])
            out_ref[0, pl.ds(r0, CHUNK), lanes] = (hn * hg_ref[:, lanes] * gate).astype(out_ref.dtype)
        return carry

    lax.fori_loop(0, nc, readout, 0, unroll=min(8, nc))


def _mlstm(qk, y3d, rowv, c0, m0, head_g):
    b, n, _ = qk.shape
    nc = n // CHUNK
    heads = N_HEADS if nc <= 4 else 1
    w = heads * D_HEAD
    v_blk = COL_V // w
    o_blk = COL_O // w
    tok = lambda off: pl.BlockSpec((1, n, w), lambda i, h: (i, 0, off + h))
    state_c = pl.BlockSpec((1, heads, 2, STATE_ROWS, D_HEAD), lambda i, h: (i, h, 0, 0, 0))
    state_m = pl.BlockSpec((1, heads, 2, SUBLANES, LANES), lambda i, h: (i, h, 0, 0, 0))
    return pl.pallas_call(
        functools.partial(_mlstm_kernel, nc=nc, heads=heads),
        out_shape=(jax.ShapeDtypeStruct((b, n, D_MLSTM), BF16),
                   jax.ShapeDtypeStruct((b, N_HEADS, 2, STATE_ROWS, D_HEAD), F32),
                   jax.ShapeDtypeStruct((b, N_HEADS, 2, SUBLANES, LANES), F32)),
        grid=(b, N_HEADS // heads),
        in_specs=[tok(0), tok(D_MLSTM // w), tok(v_blk), tok(o_blk),
                  pl.BlockSpec((1, nc, 3, SUBLANES, LANES), lambda i, h: (i, 0, 0, 0, 0)),
                  state_c, state_m,
                  pl.BlockSpec((1, w), lambda i, h: (0, h))],
        out_specs=(pl.BlockSpec((1, n, w), lambda i, h: (i, 0, h)), state_c, state_m),
        scratch_shapes=[pltpu.VMEM((heads, nc, D_HEAD, CHUNK), F32), pltpu.VMEM((heads, nc, D_HEAD, CHUNK), F32),
                        pltpu.VMEM((heads, 2, STATE_ROWS, D_HEAD), F32),
                        pltpu.VMEM((heads, 2, SUBLANES, LANES), F32)],
        compiler_params=_cparams(("arbitrary", "arbitrary")),
        name="mlstm",
    )(qk, qk, y3d, y3d, rowv, c0, m0, head_g)


def _dft_tables():
    r = DFT_R
    idx = np.arange(r)
    cg = np.cos(2 * np.pi * np.outer(idx, idx) / FOURIER_GROUP)
    sg = np.sin(2 * np.pi * np.outer(idx, idx) / FOURIER_GROUP)
    n_grp = D_FOURIER // FOURIER_GROUP
    bdc = np.kron(np.eye(n_grp), cg) / 8.0
    bds = np.kron(np.eye(n_grp), sg) / 8.0
    k1 = idx[:, None]
    a = idx[None, :]
    m1 = np.zeros((r, 2 * r, 2 * r))
    for b2 in range(r):
        th = 2 * np.pi * k1 * (r * a + b2) / (r * r)
        ec, es = np.cos(th), np.sin(th)
        m1[b2] = np.block([[ec, -es], [es, ec]]) / 8.0
    th2 = 2 * np.pi * np.outer(idx, idx) / r
    m2 = np.concatenate([np.cos(th2), -np.sin(th2)], axis=1) / 8.0
    as_bf16 = lambda t: jnp.asarray(t, dtype=F32).astype(BF16)
    return as_bf16(bdc), as_bf16(bds), as_bf16(m1), as_bf16(m2)


FOUR_BB = 8


N_SLABS = D_FOURIER // LANES


def _to_slabs(slab_ref, val):
    for s in range(N_SLABS):
        slab_ref[s] = val[:, s * LANES:(s + 1) * LANES]


def _strided_rows(slab_ref, start):
    return jnp.concatenate(
        [slab_ref[s, pl.ds(start, DFT_R, stride=FOUR_BB), :] for s in range(N_SLABS)], axis=1)


def _four_kernel(f_ref, bdc_ref, bds_ref, m1_ref, m2_ref, o_ref, z_ref, ps_ref, qs_ref, zs_ref, os_ref):
    r = DFT_R

    def stage1(blk, carry):
        b0 = pl.multiple_of(blk * FOUR_BB, FOUR_BB)
        fb = f_ref[0, :, pl.ds(b0, FOUR_BB), :].reshape(r * FOUR_BB, D_FOURIER).astype(BF16)
        _to_slabs(ps_ref, jnp.dot(fb, bdc_ref[...], preferred_element_type=F32))
        _to_slabs(qs_ref, jnp.dot(fb, bds_ref[...], preferred_element_type=F32))
        for j in range(FOUR_BB):
            x2 = jnp.concatenate([_strided_rows(ps_ref, j), _strided_rows(qs_ref, j)], axis=0).astype(BF16)
            zz = jnp.dot(m1_ref[b0 + j], x2, preferred_element_type=F32)
            z_ref[b0 + j] = _pack_pair(zz[:r], zz[r:])
        return carry

    lax.fori_loop(0, r // FOUR_BB, stage1, 0)

    def stage2(blk, carry):
        k0 = pl.multiple_of(blk * FOUR_BB, FOUR_BB)
        _to_slabs(zs_ref, z_ref[:, pl.ds(k0, FOUR_BB), :].reshape(r * FOUR_BB, D_FOURIER))
        for j in range(FOUR_BB):
            zc, zs = _unpack_pair_bf16(_strided_rows(zs_ref, j))
            o = jnp.dot(m2_ref[...], jnp.concatenate([zc, zs], axis=0), preferred_element_type=F32)
            for s in range(N_SLABS):
                os_ref[s, pl.ds(j, r, stride=FOUR_BB), :] = o[:, s * LANES:(s + 1) * LANES]
        for s in range(N_SLABS):
            o_ref[0, :, pl.ds(k0, FOUR_BB), s * LANES:(s + 1) * LANES] = os_ref[s].reshape(r, FOUR_BB, LANES)
        return carry

    lax.fori_loop(0, r // FOUR_BB, stage2, 0)


def _fourier(y4, tables):
    bdc, bds, m1, m2 = tables
    b = y4.shape[0]
    r = DFT_R
    slab = lambda dt: pltpu.VMEM((N_SLABS, r * FOUR_BB, LANES), dt)
    const = lambda *shape: pl.BlockSpec(shape, lambda i: (0,) * len(shape))
    return pl.pallas_call(
        _four_kernel,
        out_shape=jax.ShapeDtypeStruct((b, r, r, D_FOURIER), F32),
        grid=(b,),
        in_specs=[pl.BlockSpec((1, r, r, D_FOURIER), lambda i: (i, 0, 0, COL_F // D_FOURIER)),
                  const(D_FOURIER, D_FOURIER), const(D_FOURIER, D_FOURIER),
                  const(r, 2 * r, 2 * r), const(r, 2 * r)],
        out_specs=pl.BlockSpec((1, r, r, D_FOURIER), lambda i: (i, 0, 0, 0)),
        scratch_shapes=[pltpu.VMEM((r, r, D_FOURIER), jnp.int32), slab(F32), slab(F32), slab(jnp.int32), slab(F32)],
        compiler_params=_cparams(("arbitrary",), vmem=FOUR_VMEM_LIMIT),
        name="fourier",
    )(y4, bdc, bds, m1, m2)


def _outproj_kernel(x_ref, mr_ref, fo_ref, wm_ref, wf_ref, g1_ref, lg_ref, lb_ref,
                    sh_ref, sc_ref, wr_ref, x1_ref, xm_ref, s_ref):
    half = x_ref.shape[0] // 2
    for h in range(2):
        r = slice(h * half, (h + 1) * half)
        mix = (jnp.dot(mr_ref[r, :], wm_ref[...], preferred_element_type=F32)
               + jnp.dot(fo_ref[r, :].astype(BF16), wf_ref[...], preferred_element_type=F32))
        x1 = _layer_norm(DEEPNORM_ALPHA * x_ref[r, :] + g1_ref[0] * mix) * lg_ref[...] + lb_ref[...]
        x1_ref[r, :] = x1
        xm = (_layer_norm(x1) * (1.0 + sc_ref[0]) + sh_ref[0]).astype(BF16)
        xm_ref[r, :] = _pack_rows(xm)
        logits = jnp.dot(xm, wr_ref[...], preferred_element_type=F32)
        s_ref[r, :] = jax.nn.sigmoid(logits)


def _outproj(x2d, mread, four, w_m, w_f, gate1, ln_g, ln_b, shift2, scale2, w_router, n_per_batch):
    t, d = x2d.shape
    per = n_per_batch // TM
    tokd = lambda w: pl.BlockSpec((TM, w), lambda i: (i, 0))
    full = lambda r, c: pl.BlockSpec((r, c), lambda i: (0, 0))
    mod = pl.BlockSpec((1, 1, d), lambda i: (i // per, 0, 0))
    return pl.pallas_call(
        _outproj_kernel,
        out_shape=(jax.ShapeDtypeStruct((t, d), F32),
                   jax.ShapeDtypeStruct((t, PACK_W), jnp.int32),
                   jax.ShapeDtypeStruct((t, LANES), F32)),
        grid=(t // TM,),
        in_specs=[tokd(d), tokd(D_MLSTM), tokd(D_FOURIER), full(D_MLSTM, d), full(D_FOURIER, d),
                  mod, full(1, d), full(1, d), mod, mod, full(d, LANES)],
        out_specs=(tokd(d), tokd(PACK_W), tokd(LANES)),
        compiler_params=_cparams(("arbitrary",)),
        name="outproj",
    )(x2d, mread, four, w_m, w_f, gate1, ln_g, ln_b, shift2, scale2, w_router)


def _route_kernel(s_ref, b_ref, idx_ref, gw_ref, rank_ref, cnt_ref, carry_ref):
    @pl.when(pl.program_id(0) == 0)
    def _():
        carry_ref[...] = jnp.zeros_like(carry_ref)

    tm = s_ref.shape[0]
    reps = tm // LANES
    s_t = s_ref[...].T[:N_EXPERTS, :]
    e_i = lax.broadcasted_iota(jnp.int32, (N_EXPERTS, tm), 0).astype(F32)
    neg_inf = jnp.float32(-jnp.inf)
    sb = s_t + jnp.tile(b_ref[...], (1, reps))
    picks, sels = [], []
    taken = jnp.zeros((N_EXPERTS, tm), F32)
    for _ in range(TOP_K):
        best = jnp.max(sb, axis=0, keepdims=True)
        pick = jnp.min(jnp.where(sb == best, e_i, float(N_EXPERTS)), axis=0, keepdims=True)
        onehot = e_i == pick
        sels.append(jnp.sum(jnp.where(onehot, s_t, 0.0), axis=0, keepdims=True))
        picks.append(pick)
        sb = jnp.where(onehot, neg_inf, sb)
        taken = taken + onehot.astype(F32)
    u_i = lax.broadcasted_iota(jnp.int32, (tm, tm), 0)
    t_i = lax.broadcasted_iota(jnp.int32, (tm, tm), 1)
    before = (u_i < t_i).astype(BF16)
    taken_bf = taken.astype(BF16)
    rank_all = jnp.tile(carry_ref[...], (1, reps)) + jnp.dot(taken_bf, before, preferred_element_type=F32)
    carry_ref[...] = carry_ref[...] + jnp.dot(taken_bf, jnp.ones((tm, LANES), BF16),
                                              preferred_element_type=F32)
    cnt_ref[...] = carry_ref[...]
    total = sels[0]
    for v in sels[1:]:
        total = total + v
    ranks = [jnp.sum(jnp.where(e_i == picks[j], rank_all, 0.0), axis=0, keepdims=True) for j in range(TOP_K)]
    idx_ref[...] = jnp.concatenate(picks, axis=0).astype(jnp.int32)
    rank_ref[...] = jnp.concatenate(ranks, axis=0).astype(jnp.int32)
    gw_t = jnp.concatenate([ROUTED_SCALE * v / total for v in sels]
                           + [jnp.zeros((LANES - TOP_K, tm), F32)], axis=0)
    gw_ref[...] = gw_t.T[:, :TOP_K]


def _route(scores, bias_rep):
    t = scores.shape[0]
    tok = lambda w: pl.BlockSpec((TM_ROUTE, w), lambda i: (i, 0))
    slot_major = pl.BlockSpec((TOP_K, TM_ROUTE), lambda i: (0, i))
    per_expert = pl.BlockSpec((N_EXPERTS, LANES), lambda i: (0, 0))
    return pl.pallas_call(
        _route_kernel,
        out_shape=(jax.ShapeDtypeStruct((TOP_K, t), jnp.int32),
                   jax.ShapeDtypeStruct((t, TOP_K), F32),
                   jax.ShapeDtypeStruct((TOP_K, t), jnp.int32),
                   jax.ShapeDtypeStruct((N_EXPERTS, LANES), F32)),
        grid=(t // TM_ROUTE,),
        in_specs=[tok(LANES), per_expert],
        out_specs=(slot_major, tok(TOP_K), slot_major, per_expert),
        scratch_shapes=[pltpu.VMEM((N_EXPERTS, LANES), F32)],
        compiler_params=_cparams(("arbitrary",)),
        name="route",
    )(scores, bias_rep)


TN_DEST = 2048


def _dest_kernel(ps_ref, idx_ref, rank_ref, dest_ref):
    idx = idx_ref[...]
    start = jnp.zeros(idx.shape, jnp.int32)
    for e in range(N_EXPERTS):
        start = jnp.where(idx == e, ps_ref[e], start)
    dest_ref[...] = start + rank_ref[...]


def _dest(pad_start, idx_t, rank_t):
    t = idx_t.shape[1]
    blk = lambda: pl.BlockSpec((TOP_K, TN_DEST), lambda i, ps: (0, i))
    grid_spec = pltpu.PrefetchScalarGridSpec(
        num_scalar_prefetch=1, grid=(t // TN_DEST,), in_specs=[blk(), blk()], out_specs=blk())
    return pl.pallas_call(
        _dest_kernel,
        out_shape=jax.ShapeDtypeStruct((TOP_K, t), jnp.int32),
        grid_spec=grid_spec,
        compiler_params=_cparams(("arbitrary",)),
        name="dest",
    )(pad_start, idx_t, rank_t)


def _sc_worker_id(info):
    return lax.axis_index("s") * info.num_cores + lax.axis_index("c")


def _sc_dispatch(xm, dest3, n_rows):
    t, w = xm.shape
    info = plsc.get_sparse_core_info()
    n_workers = info.num_cores * info.num_subcores
    per_worker = t // SC_ROWS // n_workers
    mesh = plsc.VectorSubcoreMesh(core_axis_name="c", subcore_axis_name="s")

    @functools.partial(
        pl.kernel, mesh=mesh,
        out_type=jax.ShapeDtypeStruct((n_rows, w), xm.dtype),
        scratch_types=[pltpu.VMEM((TOP_K, SC_ROWS), jnp.int32), pltpu.VMEM((SC_ROWS, w), xm.dtype)],
        name="sc_dispatch",
    )
    def run(xm_hbm, dest_hbm, xs_hbm, idx_v, rows_v):
        first = _sc_worker_id(info) * per_worker

        @pl.loop(0, per_worker)
        def _(ci):
            chunk = first + ci
            pltpu.sync_copy(dest_hbm.at[chunk], idx_v)
            pltpu.sync_copy(xm_hbm.at[pl.ds(chunk * SC_ROWS, SC_ROWS)], rows_v)
            for j in range(TOP_K):
                pltpu.sync_copy(rows_v, xs_hbm.at[idx_v.at[j]])

    return run(xm, dest3)


def _sc_gather(table, idx2):
    n_chunks, rows = idx2.shape
    w = table.shape[1]
    info = plsc.get_sparse_core_info()
    n_workers = info.num_cores * info.num_subcores
    per_worker = n_chunks // n_workers
    assert per_worker % 2 == 0
    mesh = plsc.VectorSubcoreMesh(core_axis_name="c", subcore_axis_name="s")

    @functools.partial(
        pl.kernel, mesh=mesh,
        out_type=jax.ShapeDtypeStruct((n_chunks * rows, w), table.dtype),
        scratch_types=[pltpu.VMEM((2, rows), jnp.int32), pltpu.VMEM((2, rows, w), table.dtype),
                       pltpu.SemaphoreType.DMA, pltpu.SemaphoreType.DMA],
        name="sc_gather",
    )
    def run(table_hbm, idx_hbm, out_hbm, idx_v, rows_v, sem0, sem1):
        first = _sc_worker_id(info) * per_worker
        sems = (sem0, sem1)

        def gather(b):
            return pltpu.make_async_copy(table_hbm.at[idx_v.at[b]], rows_v.at[b], sems[b])

        def start(chunk, b):
            pltpu.sync_copy(idx_hbm.at[chunk], idx_v.at[b])
            gather(b).start()

        def finish(chunk, b):
            gather(b).wait()
            pltpu.sync_copy(rows_v.at[b], out_hbm.at[pl.ds(chunk * rows, rows)])

        start(first, 0)

        @pl.loop(0, per_worker, step=2)
        def _(ci):
            chunk = first + ci
            start(chunk + 1, 1)
            finish(chunk, 0)

            @pl.when(ci + 2 < per_worker)
            def _():
                start(chunk + 2, 0)

            finish(chunk + 1, 1)

    return run(table, idx2)


def _expert_kernel(first_ref, count_ref, used_ref,
                   xs_hbm, wgu_ref, wd_ref, y_hbm, wgu_bf, wd_bf, xbuf, ybuf, in_sem, out_sem):
    e = pl.program_id(0)
    n_used = used_ref[0]
    first = first_ref[e]
    n_blocks = count_ref[e]

    def rows(g):
        return pl.ds(pl.multiple_of(g * ROW_BLK, ROW_BLK), ROW_BLK)

    def read(g):
        slot = g % EXP_IN_SLOTS
        return pltpu.make_async_copy(xs_hbm.at[rows(g), :], xbuf.at[slot], in_sem.at[slot])

    def write(g):
        slot = g % EXP_OUT_SLOTS
        return pltpu.make_async_copy(ybuf.at[slot], y_hbm.at[rows(g), :], out_sem.at[slot])

    @pl.when(e == 0)
    def _():
        for g0 in range(EXP_AHEAD):
            @pl.when(g0 < n_used)
            def _():
                read(g0).start()

    @pl.when(n_blocks > 0)
    def _():
        wgu_bf[...] = wgu_ref[0].astype(BF16)
        wd_bf[...] = wd_ref[0].astype(BF16)

    def acquire(g):
        @pl.when(g + EXP_AHEAD < n_used)
        def _():
            read(g + EXP_AHEAD).start()

        read(g).wait()

        @pl.when(g >= EXP_OUT_SLOTS)
        def _():
            write(g - EXP_OUT_SLOTS).wait()

    def compute(g):
        in_slot = g % EXP_IN_SLOTS
        out_slot = g % EXP_OUT_SLOTS
        for h in range(ROW_BLK // ROW_HALF):
            r0 = h * ROW_HALF
            x = _unpack_bf16(xbuf[in_slot, r0:r0 + ROW_HALF, :])
            gu = jnp.dot(x, wgu_bf[...], preferred_element_type=F32)
            act = _silu(gu[:, :D_EXPERT]) * gu[:, D_EXPERT:]
            yv = jnp.dot(act.astype(BF16), wd_bf[...], preferred_element_type=F32)
            words = pltpu.pack_elementwise([yv[:, :PACK_W], yv[:, PACK_W:]], packed_dtype=BF16)
            ybuf[out_slot, r0:r0 + ROW_HALF, :] = lax.bitcast_convert_type(words, jnp.int32)

    def pair(k, carry):
        g = first + 2 * k
        acquire(g)
        acquire(g + 1)
        compute(g)
        compute(g + 1)
        write(g).start()
        write(g + 1).start()
        return carry

    lax.fori_loop(0, n_blocks // 2, pair, 0)

    @pl.when(n_blocks % 2 == 1)
    def _():
        g = first + n_blocks - 1
        acquire(g)
        compute(g)
        write(g).start()

    @pl.when(e == pl.num_programs(0) - 1)
    def _():
        for back in range(EXP_OUT_SLOTS, 0, -1):
            @pl.when(n_used >= back)
            def _():
                write(n_used - back).wait()


def _experts(first_block, n_blocks, n_used, xs, w_gu, w_down):
    n_rows, w = xs.shape
    d = D_MODEL
    grid_spec = pltpu.PrefetchScalarGridSpec(
        num_scalar_prefetch=3,
        grid=(N_EXPERTS,),
        in_specs=[pl.BlockSpec(memory_space=pl.ANY),
                  pl.BlockSpec((1, d, 2 * D_EXPERT), lambda e, *_: (e, 0, 0)),
                  pl.BlockSpec((1, D_EXPERT, d), lambda e, *_: (e, 0, 0))],
        out_specs=pl.BlockSpec(memory_space=pl.ANY),
        scratch_shapes=[pltpu.VMEM((d, 2 * D_EXPERT), BF16), pltpu.VMEM((D_EXPERT, d), BF16),
                        pltpu.VMEM((EXP_IN_SLOTS, ROW_BLK, w), jnp.int32),
                        pltpu.VMEM((EXP_OUT_SLOTS, ROW_BLK, w), jnp.int32),
                        pltpu.SemaphoreType.DMA((EXP_IN_SLOTS,)),
                        pltpu.SemaphoreType.DMA((EXP_OUT_SLOTS,))],
    )
    return pl.pallas_call(
        _expert_kernel,
        out_shape=jax.ShapeDtypeStruct((n_rows, w), jnp.int32),
        grid_spec=grid_spec,
        compiler_params=_cparams(("arbitrary",)),
        name="experts",
    )(first_block, n_blocks, n_used, xs, w_gu, w_down)


TM_COMBINE = 512


def _combine_kernel(*refs):
    yg_refs = refs[:TOP_K]
    gw_ref, x1_ref, xm_ref, g2_ref, wsgu_ref, wsd_ref, lg_ref, lb_ref, out_ref = refs[TOP_K:]
    gu = jnp.dot(_unpack_bf16(xm_ref[...]), wsgu_ref[...], preferred_element_type=F32)
    act = _silu(gu[:, :D_EXPERT]) * gu[:, D_EXPERT:]
    acc = jnp.dot(act.astype(BF16), wsd_ref[...], preferred_element_type=F32)
    gw = gw_ref[...]
    for j in range(TOP_K):
        acc = acc + gw[:, j:j + 1] * _unpack_rows(yg_refs[j][...])
    z = DEEPNORM_ALPHA * x1_ref[...] + g2_ref[0] * acc
    out_ref[...] = _layer_norm(z) * lg_ref[...] + lb_ref[...]


def _combine(yg, gw, x1, xm, gate2, w_sgu, w_sd, ln_g, ln_b, n_per_batch):
    t, d = x1.shape
    tm = TM_COMBINE
    per = n_per_batch // tm
    steps = t // tm
    tok = lambda w: pl.BlockSpec((tm, w), lambda i: (i, 0))
    full = lambda r, c: pl.BlockSpec((r, c), lambda i: (0, 0))
    slot = lambda j: pl.BlockSpec((tm, PACK_W), lambda i: (j * steps + i, 0))
    return pl.pallas_call(
        _combine_kernel,
        out_shape=jax.ShapeDtypeStruct((t, d), F32),
        grid=(steps,),
        in_specs=[slot(j) for j in range(TOP_K)]
        + [tok(TOP_K), tok(d), tok(PACK_W), pl.BlockSpec((1, 1, d), lambda i: (i // per, 0, 0)),
           full(d, 2 * D_EXPERT), full(D_EXPERT, d), full(1, d), full(1, d)],
        out_specs=tok(d),
        compiler_params=_cparams(("arbitrary",)),
        name="combine",
    )(*([yg] * TOP_K), gw, x1, xm, gate2, w_sgu, w_sd, ln_g, ln_b)


def _mixer_heads(stream, shift, scale, w_in_k, conv_w9, conv_b, gate_b_row, head_g, c0, m0, grid2d):
    b, n, d = stream.shape
    y = _inproj(stream.reshape(b * n, d), shift, scale, w_in_k, n).reshape(b, n, PROJ_W)
    qk = _conv(y, conv_w9, conv_b, grid2d)
    mread, c_fin, m_fin = _mlstm(qk, y, _gates(y, gate_b_row), c0, m0, head_g)
    return y, mread, c_fin, m_fin


def _moe(xm, x1, scores, bias_row, gate2, w_gu, w_down, w_sgu, w_sd, ln_g, ln_b, n_per_batch):
    t = xm.shape[0]
    n_assign = t * TOP_K
    n_rows = n_assign + N_EXPERTS * ROW_BLK
    idx_t, gw, rank_t, counts = _route(scores, bias_row)
    sizes = counts[:, 0].astype(jnp.int32)
    padded = (sizes + ROW_BLK - 1) // ROW_BLK * ROW_BLK
    pad_end = jnp.cumsum(padded)
    pad_start = (pad_end - padded).astype(jnp.int32)
    dest_t = _dest(pad_start, idx_t, rank_t)
    dest3 = dest_t.reshape(TOP_K, t // SC_ROWS, SC_ROWS).transpose(1, 0, 2)

    xs = _sc_dispatch(xm, dest3, n_rows)
    yb = _experts(pad_start // ROW_BLK, (padded // ROW_BLK).astype(jnp.int32),
                  (pad_end[-1:] // ROW_BLK).astype(jnp.int32), xs, w_gu, w_down)
    yg = _sc_gather(yb, dest_t.reshape(n_assign // SC_GATHER_ROWS, SC_GATHER_ROWS))
    return _combine(yg, gw, x1, xm, gate2, w_sgu, w_sd, ln_g, ln_b, n_per_batch)


def kernel(x, c, ctx, c_ctx, w_ada, b_ada, w_in, conv_w, conv_b, gate_b, head_g, w_out, ln1_g, ln1_b,
           w_router, router_bias, w_expert_gu, w_expert_down, w_shared_gu, w_shared_down, ln2_g, ln2_b):
    bsz, n, d = x.shape
    depth = w_ada.shape[0]
    assert depth == 1
    t = bsz * n
    tables = _dft_tables()
    for l in range(depth):
        cvec = jnp.zeros((SUBLANES, d), F32).at[:bsz].set(c).at[bsz].set(c_ctx)
        mod = _ada(cvec, w_ada[l], b_ada[l][None, :])
        mods = mod.reshape(SUBLANES, 6, 1, d)
        mod_x = [mods[:bsz, j] for j in range(6)]
        mod_c = [mods[bsz:bsz + 1, j] for j in range(2)]

        wl = w_in[l]
        g0 = 4 * D_MLSTM
        w_in_k = jnp.concatenate(
            [wl[:, :g0], wl[:, g0 + N_GATE_COLS:], wl[:, g0:g0 + N_GATE_COLS],
             jnp.zeros((d, LANES - N_GATE_COLS), F32)], axis=1).astype(BF16)
        conv_w9 = conv_w[l].reshape(9, 2 * D_MLSTM)
        conv_b_row = conv_b[l][None, :]
        gate_b_row = jnp.zeros((1, LANES), F32).at[0, :N_GATE_COLS].set(gate_b[l].reshape(-1))
        head_g_row = head_g[l][None, :]

        c0 = jnp.zeros((bsz, N_HEADS, 2, STATE_ROWS, D_HEAD), F32)
        m0 = jnp.zeros((bsz, N_HEADS, 2, SUBLANES, LANES), F32)
        _, _, c_ctx_fin, m_ctx_fin = _mixer_heads(ctx, mod_c[0], mod_c[1], w_in_k, conv_w9, conv_b_row,
                                                  gate_b_row, head_g_row, c0, m0, grid2d=False)
        y, mread, _, _ = _mixer_heads(x, mod_x[0], mod_x[1], w_in_k, conv_w9, conv_b_row,
                                      gate_b_row, head_g_row, c_ctx_fin, m_ctx_fin, grid2d=True)

        four = _fourier(y.reshape(bsz, DFT_R, DFT_R, PROJ_W), tables).reshape(t, D_FOURIER)

        w_o = w_out[l].astype(BF16)
        w_r = jnp.zeros((d, LANES), F32).at[:, :N_EXPERTS].set(w_router[l]).astype(BF16)
        x2d = x.reshape(t, d)
        x1, xm, scores = _outproj(x2d, mread.reshape(t, D_MLSTM), four, w_o[:D_MLSTM], w_o[D_MLSTM:],
                                  mod_x[2], ln1_g[l][None, :], ln1_b[l][None, :], mod_x[3], mod_x[4],
                                  w_r, n)

        bias_rep = jnp.broadcast_to(router_bias[l][:, None], (N_EXPERTS, LANES))
        x = _moe(xm, x1, scores, bias_rep, mod_x[5], w_expert_gu[l], w_expert_down[l],
                 w_shared_gu[l].astype(BF16), w_shared_down[l].astype(BF16),
                 ln2_g[l][None, :], ln2_b[l][None, :], n).reshape(bsz, n, d)
    return x
```

```python
import functools

import numpy as np
import jax
import jax.numpy as jnp
from jax import lax
from jax.experimental import pallas as pl
from jax.experimental.pallas import tpu as pltpu
from jax.experimental.pallas import tpu_sc as plsc

F32 = jnp.float32
BF16 = jnp.bfloat16
HIGHEST = lax.Precision.HIGHEST

D_MODEL = 1024
GRID_W = 64
N_HEADS = 4
D_HEAD = 128
D_MLSTM = N_HEADS * D_HEAD
D_FOURIER = 512
FOURIER_GROUP = 64
N_GATE_COLS = 16
CHUNK = 128
N_EXPERTS = 64
TOP_K = 8
D_EXPERT = 256
ROUTED_SCALE = 2.5
DEEPNORM_ALPHA = 2.0 ** 0.25
LN_EPS = 1e-5

LANES = 128
SUBLANES = 8
VMEM_LIMIT = 48 * 1024 * 1024
FOUR_VMEM_LIMIT = 56 * 1024 * 1024

COL_QK = 0
COL_V = 2 * D_MLSTM
COL_O = 3 * D_MLSTM
COL_G = 4 * D_MLSTM
PROJ_W = COL_G + LANES

TM = 1024
TM_ROUTE = 512
ROW_BLK = 512
ROW_HALF = 256
EXP_AHEAD = 5
EXP_IN_SLOTS = 8
EXP_OUT_SLOTS = 4
PACK_W = D_MODEL // 2
SC_ROWS = 128
SC_GATHER_ROWS = 64
CONV_PAD = 72
CONV_RB = 256
DFT_R = 64


def _cparams(sem, vmem=VMEM_LIMIT):
    return pltpu.CompilerParams(dimension_semantics=sem, vmem_limit_bytes=vmem)


def _layer_norm(x):
    mu = jnp.mean(x, axis=-1, keepdims=True)
    xc = x - mu
    var = jnp.mean(xc * xc, axis=-1, keepdims=True)
    return xc * lax.rsqrt(var + LN_EPS)


def _silu(x):
    return x * jax.nn.sigmoid(x)


def _pack_pair(lo, hi):
    as_bits = lambda u: lax.bitcast_convert_type(u.astype(BF16).astype(F32), jnp.uint32)
    word = as_bits(hi) | lax.shift_right_logical(as_bits(lo), jnp.uint32(16))
    return lax.bitcast_convert_type(word, jnp.int32)


def _unpack_pair_bf16(w):
    lo = lax.bitcast_convert_type(w.astype(jnp.int16), BF16)
    hi = lax.bitcast_convert_type(lax.shift_right_logical(w, jnp.int32(16)).astype(jnp.int16), BF16)
    return lo, hi


def _pack_rows(v):
    return _pack_pair(v[:, :PACK_W], v[:, PACK_W:])


def _unpack_bf16(w):
    return jnp.concatenate(_unpack_pair_bf16(w), axis=-1)


def _unpack_rows(w):
    u = lax.bitcast_convert_type(w, jnp.uint32)
    lo = lax.bitcast_convert_type(lax.shift_left(u, jnp.uint32(16)), F32)
    hi = lax.bitcast_convert_type(u & jnp.uint32(0xFFFF0000), F32)
    return jnp.concatenate([lo, hi], axis=1)


def _ada_kernel(c_ref, w_ref, b_ref, o_ref):
    s = _silu(c_ref[...]).astype(BF16)
    o_ref[...] = jnp.dot(s, w_ref[...].astype(BF16), preferred_element_type=F32) + b_ref[...]


def _ada(cvec, w_ada, b_ada):
    rows, d = cvec.shape
    n_out = w_ada.shape[1]
    tn = 2048
    return pl.pallas_call(
        _ada_kernel,
        out_shape=jax.ShapeDtypeStruct((rows, n_out), F32),
        grid=(n_out // tn,),
        in_specs=[pl.BlockSpec((rows, d), lambda j: (0, 0)),
                  pl.BlockSpec((d, tn), lambda j: (0, j)),
                  pl.BlockSpec((1, tn), lambda j: (0, j))],
        out_specs=pl.BlockSpec((rows, tn), lambda j: (0, j)),
        compiler_params=_cparams(("arbitrary",)),
        name="ada",
    )(cvec, w_ada, b_ada)


def _inproj_kernel(x_ref, sh_ref, sc_ref, w_ref, y_ref, f_ref):
    half = x_ref.shape[0] // 2
    for h in range(2):
        r = slice(h * half, (h + 1) * half)
        u = _layer_norm(x_ref[r, :]) * (1.0 + sc_ref[0]) + sh_ref[0]
        proj = jnp.dot(u.astype(BF16), w_ref[...], preferred_element_type=F32)
        y_ref[r, :] = proj[:, :PROJ_W]
        f_ref[r, :] = proj[:, PROJ_W:]


def _inproj(x2d, shift, scale, w, n_per_batch):
    t, d = x2d.shape
    if shift.shape[0] == 1:
        tm = min(TM, t)
        per = t // tm
    else:
        tm = min(TM, n_per_batch)
        per = n_per_batch // tm
    return pl.pallas_call(
        _inproj_kernel,
        out_shape=(jax.ShapeDtypeStruct((t, PROJ_W), F32), jax.ShapeDtypeStruct((t, D_FOURIER), F32)),
        grid=(t // tm,),
        in_specs=[pl.BlockSpec((tm, d), lambda i: (i, 0)),
                  pl.BlockSpec((1, 1, d), lambda i: (i // per, 0, 0)),
                  pl.BlockSpec((1, 1, d), lambda i: (i // per, 0, 0)),
                  pl.BlockSpec((d, PROJ_W + D_FOURIER), lambda i: (0, 0))],
        out_specs=(pl.BlockSpec((tm, PROJ_W), lambda i: (i, 0)),
                   pl.BlockSpec((tm, D_FOURIER), lambda i: (i, 0))),
        compiler_params=_cparams(("arbitrary",)),
        name="inproj",
    )(x2d, shift, scale, w)


def _conv_kernel(y_ref, w_ref, b_ref, o_ref, pad_ref, *, n, grid2d, tiles):
    zeros = jnp.zeros((CONV_PAD, LANES), F32)
    pad_ref[0:CONV_PAD, :] = zeros
    pad_ref[CONV_PAD + n:CONV_PAD + n + CONV_PAD, :] = zeros
    rb = min(CONV_RB, n)
    col = lax.broadcasted_iota(jnp.int32, (rb, LANES), 0) % GRID_W
    not_first = col >= 1
    not_last = col <= GRID_W - 2
    row_taps = (0, 1, 2) if grid2d else (1,)
    for ct in range(tiles):
        lanes = slice(ct * LANES, (ct + 1) * LANES)
        pad_ref[CONV_PAD:CONV_PAD + n, :] = y_ref[0, :, lanes]
        k_scale = jnp.where(pl.program_id(1) * tiles + ct >= N_HEADS, D_HEAD ** -0.5, 1.0).astype(F32)
        bias = b_ref[:, lanes]
        for blk in range(n // rb):
            r0 = CONV_PAD + blk * rb
            acc = jnp.zeros((rb, LANES), F32)
            for dj in range(3):
                part = jnp.zeros((rb, LANES), F32)
                for di in row_taps:
                    off = (di - 1) * GRID_W + (dj - 1)
                    tap = w_ref[di * 3 + dj:di * 3 + dj + 1, lanes]
                    part = part + tap * pad_ref[r0 + off:r0 + off + rb, :]
                if grid2d and dj == 0:
                    part = jnp.where(not_first, part, 0.0)
                if grid2d and dj == 2:
                    part = jnp.where(not_last, part, 0.0)
                acc = acc + part
            val = _silu(acc + bias) * k_scale
            o_ref[0, blk * rb:(blk + 1) * rb, lanes] = val.astype(o_ref.dtype)


def _conv(y3d, conv_w9, conv_b, grid2d):
    b, n, _ = y3d.shape
    n_ct = 2 * D_MLSTM // LANES
    tiles = 1 if grid2d else n_ct
    w = tiles * LANES
    return pl.pallas_call(
        functools.partial(_conv_kernel, n=n, grid2d=grid2d, tiles=tiles),
        out_shape=jax.ShapeDtypeStruct((b, n, 2 * D_MLSTM), BF16),
        grid=(b, n_ct // tiles),
        in_specs=[pl.BlockSpec((1, n, w), lambda i, c: (i, 0, c)),
                  pl.BlockSpec((9, w), lambda i, c: (0, c)),
                  pl.BlockSpec((1, w), lambda i, c: (0, c))],
        out_specs=pl.BlockSpec((1, n, w), lambda i, c: (i, 0, c)),
        scratch_shapes=[pltpu.VMEM((n + 2 * CONV_PAD, LANES), F32)],
        compiler_params=_cparams(("arbitrary", "arbitrary")),
        name="conv2d" if grid2d else "conv1d",
    )(y3d, conv_w9, conv_b)


STATE_ROWS = 2 * D_HEAD
GATE_A, GATE_CM, GATE_B = 0, 1, 2
GATE_GROUP = 16


def _gates_kernel(g_ref, gb_ref, rv_ref, *, nc):
    u_i = lax.broadcasted_iota(jnp.int32, (CHUNK, CHUNK), 0)
    t_i = lax.broadcasted_iota(jnp.int32, (CHUNK, CHUNK), 1)
    prefix = (u_i <= t_i).astype(F32)
    suffix = (u_i >= t_i).astype(F32)
    grp = min(GATE_GROUP, nc)
    rows = grp * SUBLANES
    lane = lax.broadcasted_iota(jnp.int32, (rows, LANES), 1)
    is_fwd = lax.broadcasted_iota(jnp.int32, (rows, LANES), 0) % SUBLANES < N_HEADS
    h = N_HEADS

    def running_max(a):
        fwd, bwd = a, a
        sh = 1
        while sh < CHUNK:
            fwd = jnp.where(lane >= sh, jnp.maximum(fwd, pltpu.roll(fwd, sh, axis=1)), fwd)
            bwd = jnp.where(lane < CHUNK - sh, jnp.maximum(bwd, pltpu.roll(bwd, CHUNK - sh, axis=1)), bwd)
            sh *= 2
        return jnp.where(is_fwd, fwd, bwd)

    def body(i, carry):
        c0 = i * grp
        r0 = pl.multiple_of(c0 * CHUNK, CHUNK)
        pre = g_ref[0, pl.ds(r0, grp * CHUNK), :] + gb_ref[...]
        pre_t = jnp.concatenate(
            [pre[j * CHUNK:(j + 1) * CHUNK].T[:N_GATE_COLS, :] for j in range(grp)], axis=0)
        lf_t = -(jnp.maximum(-pre_t, 0.0) + jnp.log1p(jnp.exp(-jnp.abs(pre_t))))
        cum_f = jnp.dot(lf_t, prefix, precision=HIGHEST, preferred_element_type=F32)
        cum_b = jnp.dot(lf_t, suffix, precision=HIGHEST, preferred_element_type=F32)
        pick = lambda t, j, q: t[j * N_GATE_COLS + q * h:j * N_GATE_COLS + (q + 1) * h]
        b8 = jnp.concatenate([x for j in range(grp) for x in (pick(cum_f, j, 1), pick(cum_b, j, 3))], axis=0)
        li8 = jnp.concatenate([x for j in range(grp) for x in (pick(pre_t, j, 0), pick(pre_t, j, 2))], axis=0)
        a8 = li8 - b8
        as_chunks = lambda t: t.reshape(grp, SUBLANES, LANES)
        rv_ref[0, pl.ds(c0, grp), GATE_A] = as_chunks(a8)
        rv_ref[0, pl.ds(c0, grp), GATE_CM] = as_chunks(running_max(a8))
        rv_ref[0, pl.ds(c0, grp), GATE_B] = as_chunks(b8)
        return carry

    lax.fori_loop(0, nc // grp, body, 0)


def _gates(y3d, gate_b_row):
    b, n, _ = y3d.shape
    nc = n // CHUNK
    return pl.pallas_call(
        functools.partial(_gates_kernel, nc=nc),
        out_shape=jax.ShapeDtypeStruct((b, nc, 3, SUBLANES, LANES), F32),
        grid=(b,),
        in_specs=[pl.BlockSpec((1, n, LANES), lambda i: (i, 0, COL_G // LANES)),
                  pl.BlockSpec((1, LANES), lambda i: (0, 0))],
        out_specs=pl.BlockSpec((1, nc, 3, SUBLANES, LANES), lambda i: (i, 0, 0, 0, 0)),
        compiler_params=_cparams(("arbitrary",)),
        name="gates",
    )(y3d, gate_b_row)


def _mlstm_kernel(q_ref, k_ref, v_ref, o_ref, rv_ref, c0_ref, m0_ref, hg_ref,
                  out_ref, cfin_ref, mfin_ref, hf_ref, hb_ref, cst_ref, mst_ref, *, nc, heads):
    cst_ref[...] = c0_ref[0]
    mst_ref[...] = m0_ref[0]
    s_i = lax.broadcasted_iota(jnp.int32, (CHUNK, CHUNK), 0)
    t_i = lax.broadcasted_iota(jnp.int32, (CHUNK, CHUNK), 1)
    ones_row = (s_i[:STATE_ROWS - D_HEAD] == 0).astype(F32)

    def one_chunk(c, d, mask_t, last, h_ref, hh):
        r0 = pl.multiple_of(c * CHUNK, CHUNK)
        lanes = slice(hh * D_HEAD, (hh + 1) * D_HEAD)
        g = d * N_HEADS + pl.program_id(1) * heads + hh
        gate_row = lambda tbl: rv_ref[0, pl.ds(c, 1), tbl, pl.ds(g, 1), :].reshape(1, LANES)
        a_row, cm_row, b_row = gate_row(GATE_A), gate_row(GATE_CM), gate_row(GATE_B)
        q_t = q_ref[0, pl.ds(r0, CHUNK), lanes].T
        k = k_ref[0, pl.ds(r0, CHUNK), lanes]
        v_t_aug = jnp.concatenate([v_ref[0, pl.ds(r0, CHUNK), lanes].T, ones_row], axis=0)
        m_prev = mst_ref[hh, d, 0:1, :]
        m_row = jnp.maximum(m_prev, cm_row)
        a_bc = jnp.broadcast_to(a_row, (CHUNK, CHUNK)).T
        d_t = jnp.where(mask_t, jnp.exp(a_bc - m_row), 0.0)
        s_t = jnp.dot(k, q_t, preferred_element_type=F32) * d_t
        cs = cst_ref[hh, d]
        rhs = jnp.concatenate(
            [s_t.astype(BF16), (q_t.astype(F32) * jnp.exp(m_prev - m_row)).astype(BF16)], axis=0)
        lhs = jnp.concatenate([v_t_aug.astype(BF16), cs.astype(BF16)], axis=1)
        h_t = jnp.dot(lhs, rhs, preferred_element_type=F32)
        den = h_t[D_HEAD:D_HEAD + 1, :]
        floor = jnp.exp(-b_row - m_row)
        h_ref[hh, pl.ds(c, 1)] = (h_t[:D_HEAD] / jnp.maximum(jnp.abs(den), floor))[None]
        m_last = m_row[:, last:last + 1]
        wv_t = (v_t_aug * jnp.exp(a_row - m_last)).astype(BF16)
        cst_ref[hh, d] = jnp.exp(m_prev - m_last) * cs + jnp.dot(wv_t, k, preferred_element_type=F32)
        mst_ref[hh, d] = jnp.broadcast_to(b_row[:, last:last + 1] + m_last, (SUBLANES, LANES))

    def body(c, carry):
        for hh in range(heads):
            one_chunk(c, 0, s_i <= t_i, CHUNK - 1, hf_ref, hh)
            one_chunk(nc - 1 - c, 1, s_i >= t_i, 0, hb_ref, hh)
        return carry

    lax.fori_loop(0, nc, body, 0, unroll=min(16 // heads, nc))
    cfin_ref[0] = cst_ref[...]
    mfin_ref[0] = mst_ref[...]

    def readout(c, carry):
        r0 = pl.multiple_of(c * CHUNK, CHUNK)
        for hh in range(heads):
            lanes = slice(hh * D_HEAD, (hh + 1) * D_HEAD)
            h_t = hf_ref[hh, pl.ds(c, 1)][0] + hb_ref[hh, pl.ds(c, 1)][0]
            hc = h_t - jnp.mean(h_t, axis=0, keepdims=True)
            hn = (hc * lax.rsqrt(jnp.mean(hc * hc, axis=0, keepdims=True) + LN_EPS)).T
            gate = jax.nn.sigmoid(o_ref[0, pl.ds(r0, CHUNK), lanes])
            out_ref[0, pl.ds(r0, CHUNK), lanes] = (hn * hg_ref[:, lanes] * gate).astype(out_ref.dtype)
        return carry

    lax.fori_loop(0, nc, readout, 0, unroll=min(8, nc))


def _mlstm(qk, y3d, rowv, c0, m0, head_g):
    b, n, _ = qk.shape
    nc = n // CHUNK
    heads = N_HEADS if nc <= 4 else 1
    w = heads * D_HEAD
    v_blk = COL_V // w
    o_blk = COL_O // w
    tok = lambda off: pl.BlockSpec((1, n, w), lambda i, h: (i, 0, off + h))
    state_c = pl.BlockSpec((1, heads, 2, STATE_ROWS, D_HEAD), lambda i, h: (i, h, 0, 0, 0))
    state_m = pl.BlockSpec((1, heads, 2, SUBLANES, LANES), lambda i, h: (i, h, 0, 0, 0))
    return pl.pallas_call(
        functools.partial(_mlstm_kernel, nc=nc, heads=heads),
        out_shape=(jax.ShapeDtypeStruct((b, n, D_MLSTM), BF16),
                   jax.ShapeDtypeStruct((b, N_HEADS, 2, STATE_ROWS, D_HEAD), F32),
                   jax.ShapeDtypeStruct((b, N_HEADS, 2, SUBLANES, LANES), F32)),
        grid=(b, N_HEADS // heads),
        in_specs=[tok(0), tok(D_MLSTM // w), tok(v_blk), tok(o_blk),
                  pl.BlockSpec((1, nc, 3, SUBLANES, LANES), lambda i, h: (i, 0, 0, 0, 0)),
                  state_c, state_m,
                  pl.BlockSpec((1, w), lambda i, h: (0, h))],
        out_specs=(pl.BlockSpec((1, n, w), lambda i, h: (i, 0, h)), state_c, state_m),
        scratch_shapes=[pltpu.VMEM((heads, nc, D_HEAD, CHUNK), F32), pltpu.VMEM((heads, nc, D_HEAD, CHUNK), F32),
                        pltpu.VMEM((heads, 2, STATE_ROWS, D_HEAD), F32),
                        pltpu.VMEM((heads, 2, SUBLANES, LANES), F32)],
        compiler_params=_cparams(("arbitrary", "arbitrary")),
        name="mlstm",
    )(qk, qk, y3d, y3d, rowv, c0, m0, head_g)


def _dft_tables():
    r = DFT_R
    idx = np.arange(r)
    cg = np.cos(2 * np.pi * np.outer(idx, idx) / FOURIER_GROUP)
    sg = np.sin(2 * np.pi * np.outer(idx, idx) / FOURIER_GROUP)
    n_grp = D_FOURIER // FOURIER_GROUP
    bdc = np.kron(np.eye(n_grp), cg) / 8.0
    bds = np.kron(np.eye(n_grp), sg) / 8.0
    k1 = idx[:, None]
    a = idx[None, :]
    m1 = np.zeros((r, 2 * r, 2 * r))
    for b2 in range(r):
        th = 2 * np.pi * k1 * (r * a + b2) / (r * r)
        ec, es = np.cos(th), np.sin(th)
        m1[b2] = np.block([[ec, -es], [es, ec]]) / 8.0
    th2 = 2 * np.pi * np.outer(idx, idx) / r
    m2 = np.concatenate([np.cos(th2), -np.sin(th2)], axis=1) / 8.0
    as_bf16 = lambda t: jnp.asarray(t, dtype=F32).astype(BF16)
    return as_bf16(bdc), as_bf16(bds), as_bf16(m1), as_bf16(m2)


FOUR_BB = 8


N_SLABS = D_FOURIER // LANES


def _to_slabs(slab_ref, val):
    for s in range(N_SLABS):
        slab_ref[s] = val[:, s * LANES:(s + 1) * LANES]


def _strided_rows(slab_ref, start):
    return jnp.concatenate(
        [slab_ref[s, pl.ds(start, DFT_R, stride=FOUR_BB), :] for s in range(N_SLABS)], axis=1)


def _four_kernel(f_ref, bdc_ref, bds_ref, m1_ref, m2_ref, o_ref, z_ref, ps_ref, qs_ref, zs_ref, os_ref):
    r = DFT_R

    def stage1(blk, carry):
        b0 = pl.multiple_of(blk * FOUR_BB, FOUR_BB)
        fb = f_ref[0, :, pl.ds(b0, FOUR_BB), :].reshape(r * FOUR_BB, D_FOURIER).astype(BF16)
        _to_slabs(ps_ref, jnp.dot(fb, bdc_ref[...], preferred_element_type=F32))
        _to_slabs(qs_ref, jnp.dot(fb, bds_ref[...], preferred_element_type=F32))
        for j in range(FOUR_BB):
            x2 = jnp.concatenate([_strided_rows(ps_ref, j), _strided_rows(qs_ref, j)], axis=0).astype(BF16)
            zz = jnp.dot(m1_ref[b0 + j], x2, preferred_element_type=F32)
            z_ref[b0 + j] = _pack_pair(zz[:r], zz[r:])
        return carry

    lax.fori_loop(0, r // FOUR_BB, stage1, 0)

    def stage2(blk, carry):
        k0 = pl.multiple_of(blk * FOUR_BB, FOUR_BB)
        _to_slabs(zs_ref, z_ref[:, pl.ds(k0, FOUR_BB), :].reshape(r * FOUR_BB, D_FOURIER))
        for j in range(FOUR_BB):
            zc, zs = _unpack_pair_bf16(_strided_rows(zs_ref, j))
            o = jnp.dot(m2_ref[...], jnp.concatenate([zc, zs], axis=0), preferred_element_type=F32)
            for s in range(N_SLABS):
                os_ref[s, pl.ds(j, r, stride=FOUR_BB), :] = o[:, s * LANES:(s + 1) * LANES]
        for s in range(N_SLABS):
            o_ref[0, :, pl.ds(k0, FOUR_BB), s * LANES:(s + 1) * LANES] = os_ref[s].reshape(r, FOUR_BB, LANES)
        return carry

    lax.fori_loop(0, r // FOUR_BB, stage2, 0)


def _fourier(y4, tables):
    bdc, bds, m1, m2 = tables
    b = y4.shape[0]
    r = DFT_R
    slab = lambda dt: pltpu.VMEM((N_SLABS, r * FOUR_BB, LANES), dt)
    const = lambda *shape: pl.BlockSpec(shape, lambda i: (0,) * len(shape))
    return pl.pallas_call(
        _four_kernel,
        out_shape=jax.ShapeDtypeStruct((b, r, r, D_FOURIER), F32),
        grid=(b,),
        in_specs=[pl.BlockSpec((1, r, r, D_FOURIER), lambda i: (i, 0, 0, 0)),
                  const(D_FOURIER, D_FOURIER), const(D_FOURIER, D_FOURIER),
                  const(r, 2 * r, 2 * r), const(r, 2 * r)],
        out_specs=pl.BlockSpec((1, r, r, D_FOURIER), lambda i: (i, 0, 0, 0)),
        scratch_shapes=[pltpu.VMEM((r, r, D_FOURIER), jnp.int32), slab(F32), slab(F32), slab(jnp.int32), slab(F32)],
        compiler_params=_cparams(("arbitrary",), vmem=FOUR_VMEM_LIMIT),
        name="fourier",
    )(y4, bdc, bds, m1, m2)


def _outproj_kernel(x_ref, mr_ref, fo_ref, wm_ref, wf_ref, g1_ref, lg_ref, lb_ref,
                    sh_ref, sc_ref, wr_ref, x1_ref, xm_ref, s_ref):
    half = x_ref.shape[0] // 2
    for h in range(2):
        r = slice(h * half, (h + 1) * half)
        mix = (jnp.dot(mr_ref[r, :], wm_ref[...], preferred_element_type=F32)
               + jnp.dot(fo_ref[r, :].astype(BF16), wf_ref[...], preferred_element_type=F32))
        x1 = _layer_norm(DEEPNORM_ALPHA * x_ref[r, :] + g1_ref[0] * mix) * lg_ref[...] + lb_ref[...]
        x1_ref[r, :] = x1
        xm = (_layer_norm(x1) * (1.0 + sc_ref[0]) + sh_ref[0]).astype(BF16)
        xm_ref[r, :] = _pack_rows(xm)
        logits = jnp.dot(xm, wr_ref[...], preferred_element_type=F32)
        s_ref[r, :] = jax.nn.sigmoid(logits)


def _outproj(x2d, mread, four, w_m, w_f, gate1, ln_g, ln_b, shift2, scale2, w_router, n_per_batch):
    t, d = x2d.shape
    per = n_per_batch // TM
    tokd = lambda w: pl.BlockSpec((TM, w), lambda i: (i, 0))
    full = lambda r, c: pl.BlockSpec((r, c), lambda i: (0, 0))
    mod = pl.BlockSpec((1, 1, d), lambda i: (i // per, 0, 0))
    return pl.pallas_call(
        _outproj_kernel,
        out_shape=(jax.ShapeDtypeStruct((t, d), F32),
                   jax.ShapeDtypeStruct((t, PACK_W), jnp.int32),
                   jax.ShapeDtypeStruct((t, LANES), F32)),
        grid=(t // TM,),
        in_specs=[tokd(d), tokd(D_MLSTM), tokd(D_FOURIER), full(D_MLSTM, d), full(D_FOURIER, d),
                  mod, full(1, d), full(1, d), mod, mod, full(d, LANES)],
        out_specs=(tokd(d), tokd(PACK_W), tokd(LANES)),
        compiler_params=_cparams(("arbitrary",)),
        name="outproj",
    )(x2d, mread, four, w_m, w_f, gate1, ln_g, ln_b, shift2, scale2, w_router)


def _route_kernel(s_ref, b_ref, idx_ref, gw_ref, rank_ref, cnt_ref, carry_ref):
    @pl.when(pl.program_id(0) == 0)
    def _():
        carry_ref[...] = jnp.zeros_like(carry_ref)

    tm = s_ref.shape[0]
    reps = tm // LANES
    s_t = s_ref[...].T[:N_EXPERTS, :]
    e_i = lax.broadcasted_iota(jnp.int32, (N_EXPERTS, tm), 0).astype(F32)
    neg_inf = jnp.float32(-jnp.inf)
    sb = s_t + jnp.tile(b_ref[...], (1, reps))
    picks, sels = [], []
    taken = jnp.zeros((N_EXPERTS, tm), F32)
    for _ in range(TOP_K):
        best = jnp.max(sb, axis=0, keepdims=True)
        pick = jnp.min(jnp.where(sb == best, e_i, float(N_EXPERTS)), axis=0, keepdims=True)
        onehot = e_i == pick
        sels.append(jnp.sum(jnp.where(onehot, s_t, 0.0), axis=0, keepdims=True))
        picks.append(pick)
        sb = jnp.where(onehot, neg_inf, sb)
        taken = taken + onehot.astype(F32)
    u_i = lax.broadcasted_iota(jnp.int32, (tm, tm), 0)
    t_i = lax.broadcasted_iota(jnp.int32, (tm, tm), 1)
    before = (u_i < t_i).astype(BF16)
    taken_bf = taken.astype(BF16)
    rank_all = jnp.tile(carry_ref[...], (1, reps)) + jnp.dot(taken_bf, before, preferred_element_type=F32)
    carry_ref[...] = carry_ref[...] + jnp.dot(taken_bf, jnp.ones((tm, LANES), BF16),
                                              preferred_element_type=F32)
    cnt_ref[...] = carry_ref[...]
    total = sels[0]
    for v in sels[1:]:
        total = total + v
    ranks = [jnp.sum(jnp.where(e_i == picks[j], rank_all, 0.0), axis=0, keepdims=True) for j in range(TOP_K)]
    idx_ref[...] = jnp.concatenate(picks, axis=0).astype(jnp.int32)
    rank_ref[...] = jnp.concatenate(ranks, axis=0).astype(jnp.int32)
    gw_t = jnp.concatenate([ROUTED_SCALE * v / total for v in sels]
                           + [jnp.zeros((LANES - TOP_K, tm), F32)], axis=0)
    gw_ref[...] = gw_t.T[:, :TOP_K]


def _route(scores, bias_rep):
    t = scores.shape[0]
    tok = lambda w: pl.BlockSpec((TM_ROUTE, w), lambda i: (i, 0))
    slot_major = pl.BlockSpec((TOP_K, TM_ROUTE), lambda i: (0, i))
    per_expert = pl.BlockSpec((N_EXPERTS, LANES), lambda i: (0, 0))
    return pl.pallas_call(
        _route_kernel,
        out_shape=(jax.ShapeDtypeStruct((TOP_K, t), jnp.int32),
                   jax.ShapeDtypeStruct((t, TOP_K), F32),
                   jax.ShapeDtypeStruct((TOP_K, t), jnp.int32),
                   jax.ShapeDtypeStruct((N_EXPERTS, LANES), F32)),
        grid=(t // TM_ROUTE,),
        in_specs=[tok(LANES), per_expert],
        out_specs=(slot_major, tok(TOP_K), slot_major, per_expert),
        scratch_shapes=[pltpu.VMEM((N_EXPERTS, LANES), F32)],
        compiler_params=_cparams(("arbitrary",)),
        name="route",
    )(scores, bias_rep)


TN_DEST = 2048


def _dest_kernel(ps_ref, idx_ref, rank_ref, dest_ref):
    idx = idx_ref[...]
    start = jnp.zeros(idx.shape, jnp.int32)
    for e in range(N_EXPERTS):
        start = jnp.where(idx == e, ps_ref[e], start)
    dest_ref[...] = start + rank_ref[...]


def _dest(pad_start, idx_t, rank_t):
    t = idx_t.shape[1]
    blk = lambda: pl.BlockSpec((TOP_K, TN_DEST), lambda i, ps: (0, i))
    grid_spec = pltpu.PrefetchScalarGridSpec(
        num_scalar_prefetch=1, grid=(t // TN_DEST,), in_specs=[blk(), blk()], out_specs=blk())
    return pl.pallas_call(
        _dest_kernel,
        out_shape=jax.ShapeDtypeStruct((TOP_K, t), jnp.int32),
        grid_spec=grid_spec,
        compiler_params=_cparams(("arbitrary",)),
        name="dest",
    )(pad_start, idx_t, rank_t)


def _sc_worker_id(info):
    return lax.axis_index("s") * info.num_cores + lax.axis_index("c")


def _sc_dispatch(xm, dest3, n_rows):
    t, w = xm.shape
    info = plsc.get_sparse_core_info()
    n_workers = info.num_cores * info.num_subcores
    per_worker = t // SC_ROWS // n_workers
    mesh = plsc.VectorSubcoreMesh(core_axis_name="c", subcore_axis_name="s")

    @functools.partial(
        pl.kernel, mesh=mesh,
        out_type=jax.ShapeDtypeStruct((n_rows, w), xm.dtype),
        scratch_types=[pltpu.VMEM((TOP_K, SC_ROWS), jnp.int32), pltpu.VMEM((SC_ROWS, w), xm.dtype)],
        name="sc_dispatch",
    )
    def run(xm_hbm, dest_hbm, xs_hbm, idx_v, rows_v):
        first = _sc_worker_id(info) * per_worker

        @pl.loop(0, per_worker)
        def _(ci):
            chunk = first + ci
            pltpu.sync_copy(dest_hbm.at[chunk], idx_v)
            pltpu.sync_copy(xm_hbm.at[pl.ds(chunk * SC_ROWS, SC_ROWS)], rows_v)
            for j in range(TOP_K):
                pltpu.sync_copy(rows_v, xs_hbm.at[idx_v.at[j]])

    return run(xm, dest3)


def _sc_gather(table, idx2):
    n_chunks, rows = idx2.shape
    w = table.shape[1]
    info = plsc.get_sparse_core_info()
    n_workers = info.num_cores * info.num_subcores
    per_worker = n_chunks // n_workers
    assert per_worker % 2 == 0
    mesh = plsc.VectorSubcoreMesh(core_axis_name="c", subcore_axis_name="s")

    @functools.partial(
        pl.kernel, mesh=mesh,
        out_type=jax.ShapeDtypeStruct((n_chunks * rows, w), table.dtype),
        scratch_types=[pltpu.VMEM((2, rows), jnp.int32), pltpu.VMEM((2, rows, w), table.dtype),
                       pltpu.SemaphoreType.DMA, pltpu.SemaphoreType.DMA],
        name="sc_gather",
    )
    def run(table_hbm, idx_hbm, out_hbm, idx_v, rows_v, sem0, sem1):
        first = _sc_worker_id(info) * per_worker
        sems = (sem0, sem1)

        def gather(b):
            return pltpu.make_async_copy(table_hbm.at[idx_v.at[b]], rows_v.at[b], sems[b])

        def start(chunk, b):
            pltpu.sync_copy(idx_hbm.at[chunk], idx_v.at[b])
            gather(b).start()

        def finish(chunk, b):
            gather(b).wait()
            pltpu.sync_copy(rows_v.at[b], out_hbm.at[pl.ds(chunk * rows, rows)])

        start(first, 0)

        @pl.loop(0, per_worker, step=2)
        def _(ci):
            chunk = first + ci
            start(chunk + 1, 1)
            finish(chunk, 0)

            @pl.when(ci + 2 < per_worker)
            def _():
                start(chunk + 2, 0)

            finish(chunk + 1, 1)

    return run(table, idx2)


def _expert_kernel(first_ref, count_ref, used_ref,
                   xs_hbm, wgu_ref, wd_ref, y_hbm, wgu_bf, wd_bf, xbuf, ybuf, in_sem, out_sem):
    e = pl.program_id(0)
    n_used = used_ref[0]
    first = first_ref[e]
    n_blocks = count_ref[e]

    def rows(g):
        return pl.ds(pl.multiple_of(g * ROW_BLK, ROW_BLK), ROW_BLK)

    def read(g):
        slot = g % EXP_IN_SLOTS
        return pltpu.make_async_copy(xs_hbm.at[rows(g), :], xbuf.at[slot], in_sem.at[slot])

    def write(g):
        slot = g % EXP_OUT_SLOTS
        return pltpu.make_async_copy(ybuf.at[slot], y_hbm.at[rows(g), :], out_sem.at[slot])

    @pl.when(e == 0)
    def _():
        for g0 in range(EXP_AHEAD):
            @pl.when(g0 < n_used)
            def _():
                read(g0).start()

    @pl.when(n_blocks > 0)
    def _():
        wgu_bf[...] = wgu_ref[0].astype(BF16)
        wd_bf[...] = wd_ref[0].astype(BF16)

    def acquire(g):
        @pl.when(g + EXP_AHEAD < n_used)
        def _():
            read(g + EXP_AHEAD).start()

        read(g).wait()

        @pl.when(g >= EXP_OUT_SLOTS)
        def _():
            write(g - EXP_OUT_SLOTS).wait()

    def compute(g):
        in_slot = g % EXP_IN_SLOTS
        out_slot = g % EXP_OUT_SLOTS
        for h in range(ROW_BLK // ROW_HALF):
            r0 = h * ROW_HALF
            x = _unpack_bf16(xbuf[in_slot, r0:r0 + ROW_HALF, :])
            gu = jnp.dot(x, wgu_bf[...], preferred_element_type=F32)
            act = _silu(gu[:, :D_EXPERT]) * gu[:, D_EXPERT:]
            yv = jnp.dot(act.astype(BF16), wd_bf[...], preferred_element_type=F32)
            words = pltpu.pack_elementwise([yv[:, :PACK_W], yv[:, PACK_W:]], packed_dtype=BF16)
            ybuf[out_slot, r0:r0 + ROW_HALF, :] = lax.bitcast_convert_type(words, jnp.int32)

    def pair(k, carry):
        g = first + 2 * k
        acquire(g)
        acquire(g + 1)
        compute(g)
        compute(g + 1)
        write(g).start()
        write(g + 1).start()
        return carry

    lax.fori_loop(0, n_blocks // 2, pair, 0)

    @pl.when(n_blocks % 2 == 1)
    def _():
        g = first + n_blocks - 1
        acquire(g)
        compute(g)
        write(g).start()

    @pl.when(e == pl.num_programs(0) - 1)
    def _():
        for back in range(EXP_OUT_SLOTS, 0, -1):
            @pl.when(n_used >= back)
            def _():
                write(n_used - back).wait()


def _experts(first_block, n_blocks, n_used, xs, w_gu, w_down):
    n_rows, w = xs.shape
    d = D_MODEL
    grid_spec = pltpu.PrefetchScalarGridSpec(
        num_scalar_prefetch=3,
        grid=(N_EXPERTS,),
        in_specs=[pl.BlockSpec(memory_space=pl.ANY),
                  pl.BlockSpec((1, d, 2 * D_EXPERT), lambda e, *_: (e, 0, 0)),
                  pl.BlockSpec((1, D_EXPERT, d), lambda e, *_: (e, 0, 0))],
        out_specs=pl.BlockSpec(memory_space=pl.ANY),
        scratch_shapes=[pltpu.VMEM((d, 2 * D_EXPERT), BF16), pltpu.VMEM((D_EXPERT, d), BF16),
                        pltpu.VMEM((EXP_IN_SLOTS, ROW_BLK, w), jnp.int32),
                        pltpu.VMEM((EXP_OUT_SLOTS, ROW_BLK, w), jnp.int32),
                        pltpu.SemaphoreType.DMA((EXP_IN_SLOTS,)),
                        pltpu.SemaphoreType.DMA((EXP_OUT_SLOTS,))],
    )
    return pl.pallas_call(
        _expert_kernel,
        out_shape=jax.ShapeDtypeStruct((n_rows, w), jnp.int32),
        grid_spec=grid_spec,
        compiler_params=_cparams(("arbitrary",)),
        name="experts",
    )(first_block, n_blocks, n_used, xs, w_gu, w_down)


TM_COMBINE = 512


def _combine_kernel(*refs):
    yg_refs = refs[:TOP_K]
    gw_ref, x1_ref, xm_ref, g2_ref, wsgu_ref, wsd_ref, lg_ref, lb_ref, out_ref = refs[TOP_K:]
    gu = jnp.dot(_unpack_bf16(xm_ref[...]), wsgu_ref[...], preferred_element_type=F32)
    act = _silu(gu[:, :D_EXPERT]) * gu[:, D_EXPERT:]
    acc = jnp.dot(act.astype(BF16), wsd_ref[...], preferred_element_type=F32)
    gw = gw_ref[...]
    for j in range(TOP_K):
        acc = acc + gw[:, j:j + 1] * _unpack_rows(yg_refs[j][...])
    z = DEEPNORM_ALPHA * x1_ref[...] + g2_ref[0] * acc
    out_ref[...] = _layer_norm(z) * lg_ref[...] + lb_ref[...]


def _combine(yg, gw, x1, xm, gate2, w_sgu, w_sd, ln_g, ln_b, n_per_batch):
    t, d = x1.shape
    tm = TM_COMBINE
    per = n_per_batch // tm
    steps = t // tm
    tok = lambda w: pl.BlockSpec((tm, w), lambda i: (i, 0))
    full = lambda r, c: pl.BlockSpec((r, c), lambda i: (0, 0))
    slot = lambda j: pl.BlockSpec((tm, PACK_W), lambda i: (j * steps + i, 0))
    return pl.pallas_call(
        _combine_kernel,
        out_shape=jax.ShapeDtypeStruct((t, d), F32),
        grid=(steps,),
        in_specs=[slot(j) for j in range(TOP_K)]
        + [tok(TOP_K), tok(d), tok(PACK_W), pl.BlockSpec((1, 1, d), lambda i: (i // per, 0, 0)),
           full(d, 2 * D_EXPERT), full(D_EXPERT, d), full(1, d), full(1, d)],
        out_specs=tok(d),
        compiler_params=_cparams(("arbitrary",)),
        name="combine",
    )(*([yg] * TOP_K), gw, x1, xm, gate2, w_sgu, w_sd, ln_g, ln_b)


def _mixer_heads(stream, shift, scale, w_in_k, conv_w9, conv_b, gate_b_row, head_g, c0, m0, grid2d):
    b, n, d = stream.shape
    y, f_in = _inproj(stream.reshape(b * n, d), shift, scale, w_in_k, n)
    y = y.reshape(b, n, PROJ_W)
    qk = _conv(y, conv_w9, conv_b, grid2d)
    mread, c_fin, m_fin = _mlstm(qk, y, _gates(y, gate_b_row), c0, m0, head_g)
    return f_in, mread, c_fin, m_fin


def _moe(xm, x1, scores, bias_row, gate2, w_gu, w_down, w_sgu, w_sd, ln_g, ln_b, n_per_batch):
    t = xm.shape[0]
    n_assign = t * TOP_K
    n_rows = n_assign + N_EXPERTS * ROW_BLK
    idx_t, gw, rank_t, counts = _route(scores, bias_row)
    sizes = counts[:, 0].astype(jnp.int32)
    padded = (sizes + ROW_BLK - 1) // ROW_BLK * ROW_BLK
    pad_end = jnp.cumsum(padded)
    pad_start = (pad_end - padded).astype(jnp.int32)
    dest_t = _dest(pad_start, idx_t, rank_t)
    dest3 = dest_t.reshape(TOP_K, t // SC_ROWS, SC_ROWS).transpose(1, 0, 2)

    xs = _sc_dispatch(xm, dest3, n_rows)
    yb = _experts(pad_start // ROW_BLK, (padded // ROW_BLK).astype(jnp.int32),
                  (pad_end[-1:] // ROW_BLK).astype(jnp.int32), xs, w_gu, w_down)
    yg = _sc_gather(yb, dest_t.reshape(n_assign // SC_GATHER_ROWS, SC_GATHER_ROWS))
    return _combine(yg, gw, x1, xm, gate2, w_sgu, w_sd, ln_g, ln_b, n_per_batch)


def kernel(x, c, ctx, c_ctx, w_ada, b_ada, w_in, conv_w, conv_b, gate_b, head_g, w_out, ln1_g, ln1_b,
           w_router, router_bias, w_expert_gu, w_expert_down, w_shared_gu, w_shared_down, ln2_g, ln2_b):
    bsz, n, d = x.shape
    depth = w_ada.shape[0]
    assert depth == 1
    t = bsz * n
    tables = _dft_tables()
    for l in range(depth):
        cvec = jnp.zeros((SUBLANES, d), F32).at[:bsz].set(c).at[bsz].set(c_ctx)
        mod = _ada(cvec, w_ada[l], b_ada[l][None, :])
        mods = mod.reshape(SUBLANES, 6, 1, d)
        mod_x = [mods[:bsz, j] for j in range(6)]
        mod_c = [mods[bsz:bsz + 1, j] for j in range(2)]

        wl = w_in[l]
        g0 = 4 * D_MLSTM
        w_in_k = jnp.concatenate(
            [wl[:, :g0 + N_GATE_COLS], jnp.zeros((d, LANES - N_GATE_COLS), F32),
             wl[:, g0 + N_GATE_COLS:]], axis=1).astype(BF16)
        conv_w9 = conv_w[l].reshape(9, 2 * D_MLSTM)
        conv_b_row = conv_b[l][None, :]
        gate_b_row = jnp.zeros((1, LANES), F32).at[0, :N_GATE_COLS].set(gate_b[l].reshape(-1))
        head_g_row = head_g[l][None, :]

        c0 = jnp.zeros((bsz, N_HEADS, 2, STATE_ROWS, D_HEAD), F32)
        m0 = jnp.zeros((bsz, N_HEADS, 2, SUBLANES, LANES), F32)
        _, _, c_ctx_fin, m_ctx_fin = _mixer_heads(ctx, mod_c[0], mod_c[1], w_in_k, conv_w9, conv_b_row,
                                                  gate_b_row, head_g_row, c0, m0, grid2d=False)
        f_in, mread, _, _ = _mixer_heads(x, mod_x[0], mod_x[1], w_in_k, conv_w9, conv_b_row,
                                      gate_b_row, head_g_row, c_ctx_fin, m_ctx_fin, grid2d=True)

        four = _fourier(f_in.reshape(bsz, DFT_R, DFT_R, D_FOURIER), tables).reshape(t, D_FOURIER)

        w_o = w_out[l].astype(BF16)
        w_r = jnp.zeros((d, LANES), F32).at[:, :N_EXPERTS].set(w_router[l]).astype(BF16)
        x2d = x.reshape(t, d)
        x1, xm, scores = _outproj(x2d, mread.reshape(t, D_MLSTM), four, w_o[:D_MLSTM], w_o[D_MLSTM:],
                                  mod_x[2], ln1_g[l][None, :], ln1_b[l][None, :], mod_x[3], mod_x[4],
                                  w_r, n)

        bias_rep = jnp.broadcast_to(router_bias[l][:, None], (N_EXPERTS, LANES))
        x = _moe(xm, x1, scores, bias_rep, mod_x[5], w_expert_gu[l], w_expert_down[l],
                 w_shared_gu[l].astype(BF16), w_shared_down[l].astype(BF16),
                 ln2_g[l][None, :], ln2_b[l][None, :], n).reshape(bsz, n, d)
    return x
```

```python
import functools

import numpy as np
import jax
import jax.numpy as jnp
from jax import lax
from jax.experimental import pallas as pl
from jax.experimental.pallas import tpu as pltpu
from jax.experimental.pallas import tpu_sc as plsc

F32 = jnp.float32
BF16 = jnp.bfloat16
HIGHEST = lax.Precision.HIGHEST

D_MODEL = 1024
GRID_W = 64
N_HEADS = 4
D_HEAD = 128
D_MLSTM = N_HEADS * D_HEAD
D_FOURIER = 512
FOURIER_GROUP = 64
N_GATE_COLS = 16
CHUNK = 128
N_EXPERTS = 64
TOP_K = 8
D_EXPERT = 256
ROUTED_SCALE = 2.5
DEEPNORM_ALPHA = 2.0 ** 0.25
LN_EPS = 1e-5

LANES = 128
SUBLANES = 8
VMEM_LIMIT = 48 * 1024 * 1024
FOUR_VMEM_LIMIT = 56 * 1024 * 1024

COL_QK = 0
COL_V = 2 * D_MLSTM
COL_O = 3 * D_MLSTM
COL_G = 4 * D_MLSTM
PROJ_W = COL_G + LANES

TM = 1024
TM_ROUTE = 512
ROW_BLK = 512
ROW_HALF = 256
EXP_AHEAD = 5
EXP_IN_SLOTS = 8
EXP_OUT_SLOTS = 4
PACK_W = D_MODEL // 2
SC_ROWS = 128
SC_GATHER_ROWS = 64
CONV_PAD = 72
CONV_RB = 256
DFT_R = 64


def _cparams(sem, vmem=VMEM_LIMIT):
    return pltpu.CompilerParams(dimension_semantics=sem, vmem_limit_bytes=vmem)


def _layer_norm(x):
    mu = jnp.mean(x, axis=-1, keepdims=True)
    xc = x - mu
    var = jnp.mean(xc * xc, axis=-1, keepdims=True)
    return xc * lax.rsqrt(var + LN_EPS)


def _silu(x):
    return x * jax.nn.sigmoid(x)


def _pack_pair(lo, hi):
    as_bits = lambda u: lax.bitcast_convert_type(u.astype(BF16).astype(F32), jnp.uint32)
    word = as_bits(hi) | lax.shift_right_logical(as_bits(lo), jnp.uint32(16))
    return lax.bitcast_convert_type(word, jnp.int32)


def _unpack_pair_bf16(w):
    lo = lax.bitcast_convert_type(w.astype(jnp.int16), BF16)
    hi = lax.bitcast_convert_type(lax.shift_right_logical(w, jnp.int32(16)).astype(jnp.int16), BF16)
    return lo, hi


def _pack_rows(v):
    return _pack_pair(v[:, :PACK_W], v[:, PACK_W:])


def _unpack_bf16(w):
    return jnp.concatenate(_unpack_pair_bf16(w), axis=-1)


def _unpack_rows(w):
    u = lax.bitcast_convert_type(w, jnp.uint32)
    lo = lax.bitcast_convert_type(lax.shift_left(u, jnp.uint32(16)), F32)
    hi = lax.bitcast_convert_type(u & jnp.uint32(0xFFFF0000), F32)
    return jnp.concatenate([lo, hi], axis=1)


def _ada_kernel(c_ref, w_ref, b_ref, o_ref):
    s = _silu(c_ref[...]).astype(BF16)
    o_ref[...] = jnp.dot(s, w_ref[...].astype(BF16), preferred_element_type=F32) + b_ref[...]


def _ada(cvec, w_ada, b_ada):
    rows, d = cvec.shape
    n_out = w_ada.shape[1]
    tn = 2048
    return pl.pallas_call(
        _ada_kernel,
        out_shape=jax.ShapeDtypeStruct((rows, n_out), F32),
        grid=(n_out // tn,),
        in_specs=[pl.BlockSpec((rows, d), lambda j: (0, 0)),
                  pl.BlockSpec((d, tn), lambda j: (0, j)),
                  pl.BlockSpec((1, tn), lambda j: (0, j))],
        out_specs=pl.BlockSpec((rows, tn), lambda j: (0, j)),
        compiler_params=_cparams(("arbitrary",)),
        name="ada",
    )(cvec, w_ada, b_ada)


def _inproj_kernel(x_ref, sh_ref, sc_ref, w_ref, y_ref, f_ref):
    half = x_ref.shape[0] // 2
    for h in range(2):
        r = slice(h * half, (h + 1) * half)
        u = _layer_norm(x_ref[r, :]) * (1.0 + sc_ref[0]) + sh_ref[0]
        proj = jnp.dot(u.astype(BF16), w_ref[...], preferred_element_type=F32)
        y_ref[r, :] = proj[:, :PROJ_W]
        f_ref[r, :] = proj[:, PROJ_W:]


def _inproj(x2d, shift, scale, w, n_per_batch):
    t, d = x2d.shape
    if shift.shape[0] == 1:
        tm = min(TM, t)
        per = t // tm
    else:
        tm = min(TM, n_per_batch)
        per = n_per_batch // tm
    return pl.pallas_call(
        _inproj_kernel,
        out_shape=(jax.ShapeDtypeStruct((t, PROJ_W), F32), jax.ShapeDtypeStruct((t, D_FOURIER), F32)),
        grid=(t // tm,),
        in_specs=[pl.BlockSpec((tm, d), lambda i: (i, 0)),
                  pl.BlockSpec((1, 1, d), lambda i: (i // per, 0, 0)),
                  pl.BlockSpec((1, 1, d), lambda i: (i // per, 0, 0)),
                  pl.BlockSpec((d, PROJ_W + D_FOURIER), lambda i: (0, 0))],
        out_specs=(pl.BlockSpec((tm, PROJ_W), lambda i: (i, 0)),
                   pl.BlockSpec((tm, D_FOURIER), lambda i: (i, 0))),
        compiler_params=_cparams(("arbitrary",)),
        name="inproj",
    )(x2d, shift, scale, w)


def _conv_kernel(y_ref, w_ref, b_ref, o_ref, pad_ref, *, n, grid2d, tiles):
    zeros = jnp.zeros((CONV_PAD, LANES), F32)
    pad_ref[0:CONV_PAD, :] = zeros
    pad_ref[CONV_PAD + n:CONV_PAD + n + CONV_PAD, :] = zeros
    rb = min(CONV_RB, n)
    col = lax.broadcasted_iota(jnp.int32, (rb, LANES), 0) % GRID_W
    not_first = col >= 1
    not_last = col <= GRID_W - 2
    row_taps = (0, 1, 2) if grid2d else (1,)
    for ct in range(tiles):
        lanes = slice(ct * LANES, (ct + 1) * LANES)
        pad_ref[CONV_PAD:CONV_PAD + n, :] = y_ref[0, :, lanes]
        k_scale = jnp.where(pl.program_id(1) * tiles + ct >= N_HEADS, D_HEAD ** -0.5, 1.0).astype(F32)
        bias = b_ref[:, lanes]
        for blk in range(n // rb):
            r0 = CONV_PAD + blk * rb
            acc = jnp.zeros((rb, LANES), F32)
            for dj in range(3):
                part = jnp.zeros((rb, LANES), F32)
                for di in row_taps:
                    off = (di - 1) * GRID_W + (dj - 1)
                    tap = w_ref[di * 3 + dj:di * 3 + dj + 1, lanes]
                    part = part + tap * pad_ref[r0 + off:r0 + off + rb, :]
                if grid2d and dj == 0:
                    part = jnp.where(not_first, part, 0.0)
                if grid2d and dj == 2:
                    part = jnp.where(not_last, part, 0.0)
                acc = acc + part
            val = _silu(acc + bias) * k_scale
            o_ref[0, blk * rb:(blk + 1) * rb, lanes] = val.astype(o_ref.dtype)


def _conv(y3d, conv_w9, conv_b, grid2d):
    b, n, _ = y3d.shape
    n_ct = 2 * D_MLSTM // LANES
    tiles = 1 if grid2d else n_ct
    w = tiles * LANES
    return pl.pallas_call(
        functools.partial(_conv_kernel, n=n, grid2d=grid2d, tiles=tiles),
        out_shape=jax.ShapeDtypeStruct((b, n, 2 * D_MLSTM), BF16),
        grid=(b, n_ct // tiles),
        in_specs=[pl.BlockSpec((1, n, w), lambda i, c: (i, 0, c)),
                  pl.BlockSpec((9, w), lambda i, c: (0, c)),
                  pl.BlockSpec((1, w), lambda i, c: (0, c))],
        out_specs=pl.BlockSpec((1, n, w), lambda i, c: (i, 0, c)),
        scratch_shapes=[pltpu.VMEM((n + 2 * CONV_PAD, LANES), F32)],
        compiler_params=_cparams(("arbitrary", "arbitrary")),
        name="conv2d" if grid2d else "conv1d",
    )(y3d, conv_w9, conv_b)


STATE_ROWS = 2 * D_HEAD
GATE_A, GATE_CM, GATE_B = 0, 1, 2
GATE_GROUP = 16


def _gates_kernel(g_ref, gb_ref, rv_ref, *, nc):
    u_i = lax.broadcasted_iota(jnp.int32, (CHUNK, CHUNK), 0)
    t_i = lax.broadcasted_iota(jnp.int32, (CHUNK, CHUNK), 1)
    prefix = (u_i <= t_i).astype(F32)
    suffix = (u_i >= t_i).astype(F32)
    grp = min(GATE_GROUP, nc)
    rows = grp * SUBLANES
    lane = lax.broadcasted_iota(jnp.int32, (rows, LANES), 1)
    is_fwd = lax.broadcasted_iota(jnp.int32, (rows, LANES), 0) % SUBLANES < N_HEADS
    h = N_HEADS

    def running_max(a):
        fwd, bwd = a, a
        sh = 1
        while sh < CHUNK:
            fwd = jnp.where(lane >= sh, jnp.maximum(fwd, pltpu.roll(fwd, sh, axis=1)), fwd)
            bwd = jnp.where(lane < CHUNK - sh, jnp.maximum(bwd, pltpu.roll(bwd, CHUNK - sh, axis=1)), bwd)
            sh *= 2
        return jnp.where(is_fwd, fwd, bwd)

    def body(i, carry):
        c0 = i * grp
        r0 = pl.multiple_of(c0 * CHUNK, CHUNK)
        pre = g_ref[0, pl.ds(r0, grp * CHUNK), :] + gb_ref[...]
        pre_t = jnp.concatenate(
            [pre[j * CHUNK:(j + 1) * CHUNK].T[:N_GATE_COLS, :] for j in range(grp)], axis=0)
        lf_t = -(jnp.maximum(-pre_t, 0.0) + jnp.log1p(jnp.exp(-jnp.abs(pre_t))))
        cum_f = jnp.dot(lf_t, prefix, precision=HIGHEST, preferred_element_type=F32)
        cum_b = jnp.dot(lf_t, suffix, precision=HIGHEST, preferred_element_type=F32)
        pick = lambda t, j, q: t[j * N_GATE_COLS + q * h:j * N_GATE_COLS + (q + 1) * h]
        b8 = jnp.concatenate([x for j in range(grp) for x in (pick(cum_f, j, 1), pick(cum_b, j, 3))], axis=0)
        li8 = jnp.concatenate([x for j in range(grp) for x in (pick(pre_t, j, 0), pick(pre_t, j, 2))], axis=0)
        a8 = li8 - b8
        as_chunks = lambda t: t.reshape(grp, SUBLANES, LANES)
        rv_ref[0, pl.ds(c0, grp), GATE_A] = as_chunks(a8)
        rv_ref[0, pl.ds(c0, grp), GATE_CM] = as_chunks(running_max(a8))
        rv_ref[0, pl.ds(c0, grp), GATE_B] = as_chunks(b8)
        return carry

    lax.fori_loop(0, nc // grp, body, 0)


def _gates(y3d, gate_b_row):
    b, n, _ = y3d.shape
    nc = n // CHUNK
    return pl.pallas_call(
        functools.partial(_gates_kernel, nc=nc),
        out_shape=jax.ShapeDtypeStruct((b, nc, 3, SUBLANES, LANES), F32),
        grid=(b,),
        in_specs=[pl.BlockSpec((1, n, LANES), lambda i: (i, 0, COL_G // LANES)),
                  pl.BlockSpec((1, LANES), lambda i: (0, 0))],
        out_specs=pl.BlockSpec((1, nc, 3, SUBLANES, LANES), lambda i: (i, 0, 0, 0, 0)),
        compiler_params=_cparams(("arbitrary",)),
        name="gates",
    )(y3d, gate_b_row)


def _mlstm_kernel(q_ref, k_ref, v_ref, o_ref, rv_ref, c0_ref, m0_ref, hg_ref,
                  out_ref, cfin_ref, mfin_ref, hf_ref, hb_ref, cst_ref, mst_ref, *, nc, heads):
    cst_ref[...] = c0_ref[0]
    mst_ref[...] = m0_ref[0]
    s_i = lax.broadcasted_iota(jnp.int32, (CHUNK, CHUNK), 0)
    t_i = lax.broadcasted_iota(jnp.int32, (CHUNK, CHUNK), 1)
    ones_row = (s_i[:STATE_ROWS - D_HEAD] == 0).astype(F32)

    def one_chunk(c, d, mask_t, last, h_ref, hh):
        r0 = pl.multiple_of(c * CHUNK, CHUNK)
        lanes = slice(hh * D_HEAD, (hh + 1) * D_HEAD)
        g = d * N_HEADS + pl.program_id(1) * heads + hh
        gate_row = lambda tbl: rv_ref[0, pl.ds(c, 1), tbl, pl.ds(g, 1), :].reshape(1, LANES)
        a_row, cm_row, b_row = gate_row(GATE_A), gate_row(GATE_CM), gate_row(GATE_B)
        q_t = q_ref[0, pl.ds(r0, CHUNK), lanes].T
        k = k_ref[0, pl.ds(r0, CHUNK), lanes]
        v_t_aug = jnp.concatenate([v_ref[0, pl.ds(r0, CHUNK), lanes].T, ones_row], axis=0)
        m_prev = mst_ref[hh, d, 0:1, :]
        m_row = jnp.maximum(m_prev, cm_row)
        a_bc = jnp.broadcast_to(a_row, (CHUNK, CHUNK)).T
        d_t = jnp.where(mask_t, jnp.exp(a_bc - m_row), 0.0)
        s_t = jnp.dot(k, q_t, preferred_element_type=F32) * d_t
        cs = cst_ref[hh, d]
        rhs = jnp.concatenate(
            [s_t.astype(BF16), (q_t.astype(F32) * jnp.exp(m_prev - m_row)).astype(BF16)], axis=0)
        lhs = jnp.concatenate([v_t_aug.astype(BF16), cs.astype(BF16)], axis=1)
        h_t = jnp.dot(lhs, rhs, preferred_element_type=F32)
        den = h_t[D_HEAD:D_HEAD + 1, :]
        floor = jnp.exp(-b_row - m_row)
        h_ref[hh, pl.ds(c, 1)] = (h_t[:D_HEAD] / jnp.maximum(jnp.abs(den), floor))[None]
        m_last = m_row[:, last:last + 1]
        wv_t = (v_t_aug * jnp.exp(a_row - m_last)).astype(BF16)
        cst_ref[hh, d] = jnp.exp(m_prev - m_last) * cs + jnp.dot(wv_t, k, preferred_element_type=F32)
        mst_ref[hh, d] = jnp.broadcast_to(b_row[:, last:last + 1] + m_last, (SUBLANES, LANES))

    def body(c, carry):
        for hh in range(heads):
            one_chunk(c, 0, s_i <= t_i, CHUNK - 1, hf_ref, hh)
            one_chunk(nc - 1 - c, 1, s_i >= t_i, 0, hb_ref, hh)
        return carry

    lax.fori_loop(0, nc, body, 0, unroll=min(16 // heads, nc))
    cfin_ref[0] = cst_ref[...]
    mfin_ref[0] = mst_ref[...]

    def readout(c, carry):
        r0 = pl.multiple_of(c * CHUNK, CHUNK)
        for hh in range(heads):
            lanes = slice(hh * D_HEAD, (hh + 1) * D_HEAD)
            h_t = hf_ref[hh, pl.ds(c, 1)][0] + hb_ref[hh, pl.ds(c, 1)][0]
            hc = h_t - jnp.mean(h_t, axis=0, keepdims=True)
            hn = (hc * lax.rsqrt(jnp.mean(hc * hc, axis=0, keepdims=True) + LN_EPS)).T
            gate = jax.nn.sigmoid(o_ref[0, pl.ds(r0, CHUNK), lanes])
            out_ref[0, pl.ds(r0, CHUNK), lanes] = (hn * hg_ref[:, lanes] * gate).astype(out_ref.dtype)
        return carry

    lax.fori_loop(0, nc, readout, 0, unroll=min(8, nc))


CAST_SPLIT = 2


def _mlstm_cast_kernel(*refs, nc, heads):
    ins, (wgu_ref, wd_ref) = refs[:8], refs[8:10]
    outs, (wgu_bf_ref, wd_bf_ref) = refs[10:13], refs[13:15]
    scratch = refs[15:]

    @pl.when(pl.program_id(2) == 0)
    def _():
        _mlstm_kernel(*ins, *outs, *scratch, nc=nc, heads=heads)

    wgu_bf_ref[...] = wgu_ref[...].astype(BF16)
    wd_bf_ref[...] = wd_ref[...].astype(BF16)


def _mlstm(qk, y3d, rowv, c0, m0, head_g, cast_weights=None):
    b, n, _ = qk.shape
    nc = n // CHUNK
    heads = N_HEADS if nc <= 4 else 1
    w = heads * D_HEAD
    v_blk = COL_V // w
    o_blk = COL_O // w
    tok = lambda off: pl.BlockSpec((1, n, w), lambda i, h, *_: (i, 0, off + h))
    state_c = pl.BlockSpec((1, heads, 2, STATE_ROWS, D_HEAD), lambda i, h, *_: (i, h, 0, 0, 0))
    state_m = pl.BlockSpec((1, heads, 2, SUBLANES, LANES), lambda i, h, *_: (i, h, 0, 0, 0))
    grid = (b, N_HEADS // heads)
    in_specs = [tok(0), tok(D_MLSTM // w), tok(v_blk), tok(o_blk),
                pl.BlockSpec((1, nc, 3, SUBLANES, LANES), lambda i, h, *_: (i, 0, 0, 0, 0)),
                state_c, state_m,
                pl.BlockSpec((1, w), lambda i, h, *_: (0, h))]
    out_specs = [pl.BlockSpec((1, n, w), lambda i, h, *_: (i, 0, h)), state_c, state_m]
    out_shape = [jax.ShapeDtypeStruct((b, n, D_MLSTM), BF16),
                 jax.ShapeDtypeStruct((b, N_HEADS, 2, STATE_ROWS, D_HEAD), F32),
                 jax.ShapeDtypeStruct((b, N_HEADS, 2, SUBLANES, LANES), F32)]
    args = [qk, qk, y3d, y3d, rowv, c0, m0, head_g]
    body = _mlstm_kernel
    if cast_weights is not None:
        steps = grid[0] * grid[1] * CAST_SPLIT
        per_step = N_EXPERTS // steps
        slab = lambda i, h, j: ((i * grid[1] + h) * CAST_SPLIT + j, 0, 0)
        for wt in cast_weights:
            blk = pl.BlockSpec((per_step,) + wt.shape[1:], slab)
            in_specs.append(blk)
            out_specs.append(blk)
            out_shape.append(jax.ShapeDtypeStruct(wt.shape, BF16))
            args.append(wt)
        grid = grid + (CAST_SPLIT,)
        body = _mlstm_cast_kernel
    return pl.pallas_call(
        functools.partial(body, nc=nc, heads=heads),
        out_shape=tuple(out_shape),
        grid=grid,
        in_specs=in_specs,
        out_specs=tuple(out_specs),
        scratch_shapes=[pltpu.VMEM((heads, nc, D_HEAD, CHUNK), F32), pltpu.VMEM((heads, nc, D_HEAD, CHUNK), F32),
                        pltpu.VMEM((heads, 2, STATE_ROWS, D_HEAD), F32),
                        pltpu.VMEM((heads, 2, SUBLANES, LANES), F32)],
        compiler_params=_cparams(("arbitrary",) * len(grid)),
        name="mlstm",
    )(*args)


def _dft_tables():
    r = DFT_R
    idx = np.arange(r)
    cg = np.cos(2 * np.pi * np.outer(idx, idx) / FOURIER_GROUP)
    sg = np.sin(2 * np.pi * np.outer(idx, idx) / FOURIER_GROUP)
    n_grp = D_FOURIER // FOURIER_GROUP
    bdc = np.kron(np.eye(n_grp), cg) / 8.0
    bds = np.kron(np.eye(n_grp), sg) / 8.0
    k1 = idx[:, None]
    a = idx[None, :]
    m1 = np.zeros((r, 2 * r, 2 * r))
    for b2 in range(r):
        th = 2 * np.pi * k1 * (r * a + b2) / (r * r)
        ec, es = np.cos(th), np.sin(th)
        m1[b2] = np.block([[ec, -es], [es, ec]]) / 8.0
    th2 = 2 * np.pi * np.outer(idx, idx) / r
    m2 = np.concatenate([np.cos(th2), -np.sin(th2)], axis=1) / 8.0
    as_bf16 = lambda t: jnp.asarray(t, dtype=F32).astype(BF16)
    return as_bf16(bdc), as_bf16(bds), as_bf16(m1), as_bf16(m2)


FOUR_BB = 8


N_SLABS = D_FOURIER // LANES


def _to_slabs(slab_ref, val):
    for s in range(N_SLABS):
        slab_ref[s] = val[:, s * LANES:(s + 1) * LANES]


def _strided_rows(slab_ref, start):
    return jnp.concatenate(
        [slab_ref[s, pl.ds(start, DFT_R, stride=FOUR_BB), :] for s in range(N_SLABS)], axis=1)


def _four_kernel(f_ref, bdc_ref, bds_ref, m1_ref, m2_ref, o_ref, z_ref, ps_ref, qs_ref, zs_ref, os_ref):
    r = DFT_R

    def stage1(blk, carry):
        b0 = pl.multiple_of(blk * FOUR_BB, FOUR_BB)
        fb = f_ref[0, :, pl.ds(b0, FOUR_BB), :].reshape(r * FOUR_BB, D_FOURIER).astype(BF16)
        _to_slabs(ps_ref, jnp.dot(fb, bdc_ref[...], preferred_element_type=F32))
        _to_slabs(qs_ref, jnp.dot(fb, bds_ref[...], preferred_element_type=F32))
        for j in range(FOUR_BB):
            x2 = jnp.concatenate([_strided_rows(ps_ref, j), _strided_rows(qs_ref, j)], axis=0).astype(BF16)
            zz = jnp.dot(m1_ref[b0 + j], x2, preferred_element_type=F32)
            z_ref[b0 + j] = _pack_pair(zz[:r], zz[r:])
        return carry

    lax.fori_loop(0, r // FOUR_BB, stage1, 0)

    def stage2(blk, carry):
        k0 = pl.multiple_of(blk * FOUR_BB, FOUR_BB)
        _to_slabs(zs_ref, z_ref[:, pl.ds(k0, FOUR_BB), :].reshape(r * FOUR_BB, D_FOURIER))
        for j in range(FOUR_BB):
            zc, zs = _unpack_pair_bf16(_strided_rows(zs_ref, j))
            o = jnp.dot(m2_ref[...], jnp.concatenate([zc, zs], axis=0), preferred_element_type=F32)
            for s in range(N_SLABS):
                os_ref[s, pl.ds(j, r, stride=FOUR_BB), :] = o[:, s * LANES:(s + 1) * LANES]
        for s in range(N_SLABS):
            o_ref[0, :, pl.ds(k0, FOUR_BB), s * LANES:(s + 1) * LANES] = os_ref[s].reshape(r, FOUR_BB, LANES)
        return carry

    lax.fori_loop(0, r // FOUR_BB, stage2, 0)


def _fourier(y4, tables):
    bdc, bds, m1, m2 = tables
    b = y4.shape[0]
    r = DFT_R
    slab = lambda dt: pltpu.VMEM((N_SLABS, r * FOUR_BB, LANES), dt)
    const = lambda *shape: pl.BlockSpec(shape, lambda i: (0,) * len(shape))
    return pl.pallas_call(
        _four_kernel,
        out_shape=jax.ShapeDtypeStruct((b, r, r, D_FOURIER), F32),
        grid=(b,),
        in_specs=[pl.BlockSpec((1, r, r, D_FOURIER), lambda i: (i, 0, 0, 0)),
                  const(D_FOURIER, D_FOURIER), const(D_FOURIER, D_FOURIER),
                  const(r, 2 * r, 2 * r), const(r, 2 * r)],
        out_specs=pl.BlockSpec((1, r, r, D_FOURIER), lambda i: (i, 0, 0, 0)),
        scratch_shapes=[pltpu.VMEM((r, r, D_FOURIER), jnp.int32), slab(F32), slab(F32), slab(jnp.int32), slab(F32)],
        compiler_params=_cparams(("arbitrary",), vmem=FOUR_VMEM_LIMIT),
        name="fourier",
    )(y4, bdc, bds, m1, m2)


def _outproj_kernel(x_ref, mr_ref, fo_ref, wm_ref, wf_ref, g1_ref, lg_ref, lb_ref,
                    sh_ref, sc_ref, wr_ref, x1_ref, xm_ref, s_ref):
    half = x_ref.shape[0] // 2
    for h in range(2):
        r = slice(h * half, (h + 1) * half)
        mix = (jnp.dot(mr_ref[r, :], wm_ref[...], preferred_element_type=F32)
               + jnp.dot(fo_ref[r, :].astype(BF16), wf_ref[...], preferred_element_type=F32))
        x1 = _layer_norm(DEEPNORM_ALPHA * x_ref[r, :] + g1_ref[0] * mix) * lg_ref[...] + lb_ref[...]
        x1_ref[r, :] = x1
        xm = (_layer_norm(x1) * (1.0 + sc_ref[0]) + sh_ref[0]).astype(BF16)
        xm_ref[r, :] = _pack_rows(xm)
        logits = jnp.dot(xm, wr_ref[...], preferred_element_type=F32)
        s_ref[r, :] = jax.nn.sigmoid(logits)


def _outproj(x2d, mread, four, w_m, w_f, gate1, ln_g, ln_b, shift2, scale2, w_router, n_per_batch):
    t, d = x2d.shape
    per = n_per_batch // TM
    tokd = lambda w: pl.BlockSpec((TM, w), lambda i: (i, 0))
    full = lambda r, c: pl.BlockSpec((r, c), lambda i: (0, 0))
    mod = pl.BlockSpec((1, 1, d), lambda i: (i // per, 0, 0))
    return pl.pallas_call(
        _outproj_kernel,
        out_shape=(jax.ShapeDtypeStruct((t, d), F32),
                   jax.ShapeDtypeStruct((t, PACK_W), jnp.int32),
                   jax.ShapeDtypeStruct((t, LANES), F32)),
        grid=(t // TM,),
        in_specs=[tokd(d), tokd(D_MLSTM), tokd(D_FOURIER), full(D_MLSTM, d), full(D_FOURIER, d),
                  mod, full(1, d), full(1, d), mod, mod, full(d, LANES)],
        out_specs=(tokd(d), tokd(PACK_W), tokd(LANES)),
        compiler_params=_cparams(("arbitrary",)),
        name="outproj",
    )(x2d, mread, four, w_m, w_f, gate1, ln_g, ln_b, shift2, scale2, w_router)


def _route_kernel(s_ref, b_ref, idx_ref, gw_ref, rank_ref, cnt_ref, carry_ref):
    @pl.when(pl.program_id(0) == 0)
    def _():
        carry_ref[...] = jnp.zeros_like(carry_ref)

    tm = s_ref.shape[0]
    reps = tm // LANES
    s_t = s_ref[...].T[:N_EXPERTS, :]
    e_i = lax.broadcasted_iota(jnp.int32, (N_EXPERTS, tm), 0).astype(F32)
    neg_inf = jnp.float32(-jnp.inf)
    sb = s_t + jnp.tile(b_ref[...], (1, reps))
    picks, sels = [], []
    taken = jnp.zeros((N_EXPERTS, tm), F32)
    for _ in range(TOP_K):
        best = jnp.max(sb, axis=0, keepdims=True)
        pick = jnp.min(jnp.where(sb == best, e_i, float(N_EXPERTS)), axis=0, keepdims=True)
        onehot = e_i == pick
        sels.append(jnp.sum(jnp.where(onehot, s_t, 0.0), axis=0, keepdims=True))
        picks.append(pick)
        sb = jnp.where(onehot, neg_inf, sb)
        taken = taken + onehot.astype(F32)
    u_i = lax.broadcasted_iota(jnp.int32, (tm, tm), 0)
    t_i = lax.broadcasted_iota(jnp.int32, (tm, tm), 1)
    before = (u_i < t_i).astype(BF16)
    taken_bf = taken.astype(BF16)
    rank_all = jnp.tile(carry_ref[...], (1, reps)) + jnp.dot(taken_bf, before, preferred_element_type=F32)
    carry_ref[...] = carry_ref[...] + jnp.dot(taken_bf, jnp.ones((tm, LANES), BF16),
                                              preferred_element_type=F32)
    cnt_ref[...] = carry_ref[...]
    total = sels[0]
    for v in sels[1:]:
        total = total + v
    ranks = [jnp.sum(jnp.where(e_i == picks[j], rank_all, 0.0), axis=0, keepdims=True) for j in range(TOP_K)]
    idx_ref[...] = jnp.concatenate(picks, axis=0).astype(jnp.int32)
    rank_ref[...] = jnp.concatenate(ranks, axis=0).astype(jnp.int32)
    gw_t = jnp.concatenate([ROUTED_SCALE * v / total for v in sels]
                           + [jnp.zeros((LANES - TOP_K, tm), F32)], axis=0)
    gw_ref[...] = gw_t.T[:, :TOP_K]


def _route(scores, bias_rep):
    t = scores.shape[0]
    tok = lambda w: pl.BlockSpec((TM_ROUTE, w), lambda i: (i, 0))
    slot_major = pl.BlockSpec((TOP_K, TM_ROUTE), lambda i: (0, i))
    per_expert = pl.BlockSpec((N_EXPERTS, LANES), lambda i: (0, 0))
    return pl.pallas_call(
        _route_kernel,
        out_shape=(jax.ShapeDtypeStruct((TOP_K, t), jnp.int32),
                   jax.ShapeDtypeStruct((t, TOP_K), F32),
                   jax.ShapeDtypeStruct((TOP_K, t), jnp.int32),
                   jax.ShapeDtypeStruct((N_EXPERTS, LANES), F32)),
        grid=(t // TM_ROUTE,),
        in_specs=[tok(LANES), per_expert],
        out_specs=(slot_major, tok(TOP_K), slot_major, per_expert),
        scratch_shapes=[pltpu.VMEM((N_EXPERTS, LANES), F32)],
        compiler_params=_cparams(("arbitrary",)),
        name="route",
    )(scores, bias_rep)


TN_DEST = 2048


def _dest_kernel(ps_ref, idx_ref, rank_ref, dest_ref):
    idx = idx_ref[...]
    start = jnp.zeros(idx.shape, jnp.int32)
    for e in range(N_EXPERTS):
        start = jnp.where(idx == e, ps_ref[e], start)
    dest_ref[...] = start + rank_ref[...]


def _dest(pad_start, idx_t, rank_t):
    t = idx_t.shape[1]
    blk = lambda: pl.BlockSpec((TOP_K, TN_DEST), lambda i, ps: (0, i))
    grid_spec = pltpu.PrefetchScalarGridSpec(
        num_scalar_prefetch=1, grid=(t // TN_DEST,), in_specs=[blk(), blk()], out_specs=blk())
    return pl.pallas_call(
        _dest_kernel,
        out_shape=jax.ShapeDtypeStruct((TOP_K, t), jnp.int32),
        grid_spec=grid_spec,
        compiler_params=_cparams(("arbitrary",)),
        name="dest",
    )(pad_start, idx_t, rank_t)


def _sc_worker_id(info):
    return lax.axis_index("s") * info.num_cores + lax.axis_index("c")


def _sc_dispatch(xm, dest3, n_rows):
    t, w = xm.shape
    info = plsc.get_sparse_core_info()
    n_workers = info.num_cores * info.num_subcores
    per_worker = t // SC_ROWS // n_workers
    mesh = plsc.VectorSubcoreMesh(core_axis_name="c", subcore_axis_name="s")

    @functools.partial(
        pl.kernel, mesh=mesh,
        out_type=jax.ShapeDtypeStruct((n_rows, w), xm.dtype),
        scratch_types=[pltpu.VMEM((TOP_K, SC_ROWS), jnp.int32), pltpu.VMEM((SC_ROWS, w), xm.dtype)],
        name="sc_dispatch",
    )
    def run(xm_hbm, dest_hbm, xs_hbm, idx_v, rows_v):
        first = _sc_worker_id(info) * per_worker

        @pl.loop(0, per_worker)
        def _(ci):
            chunk = first + ci
            pltpu.sync_copy(dest_hbm.at[chunk], idx_v)
            pltpu.sync_copy(xm_hbm.at[pl.ds(chunk * SC_ROWS, SC_ROWS)], rows_v)
            for j in range(TOP_K):
                pltpu.sync_copy(rows_v, xs_hbm.at[idx_v.at[j]])

    return run(xm, dest3)


def _sc_gather(table, idx2):
    n_chunks, rows = idx2.shape
    w = table.shape[1]
    info = plsc.get_sparse_core_info()
    n_workers = info.num_cores * info.num_subcores
    per_worker = n_chunks // n_workers
    assert per_worker % 2 == 0
    mesh = plsc.VectorSubcoreMesh(core_axis_name="c", subcore_axis_name="s")

    @functools.partial(
        pl.kernel, mesh=mesh,
        out_type=jax.ShapeDtypeStruct((n_chunks * rows, w), table.dtype),
        scratch_types=[pltpu.VMEM((2, rows), jnp.int32), pltpu.VMEM((2, rows, w), table.dtype),
                       pltpu.SemaphoreType.DMA, pltpu.SemaphoreType.DMA],
        name="sc_gather",
    )
    def run(table_hbm, idx_hbm, out_hbm, idx_v, rows_v, sem0, sem1):
        first = _sc_worker_id(info) * per_worker
        sems = (sem0, sem1)

        def gather(b):
            return pltpu.make_async_copy(table_hbm.at[idx_v.at[b]], rows_v.at[b], sems[b])

        def start(chunk, b):
            pltpu.sync_copy(idx_hbm.at[chunk], idx_v.at[b])
            gather(b).start()

        def finish(chunk, b):
            gather(b).wait()
            pltpu.sync_copy(rows_v.at[b], out_hbm.at[pl.ds(chunk * rows, rows)])

        start(first, 0)

        @pl.loop(0, per_worker, step=2)
        def _(ci):
            chunk = first + ci
            start(chunk + 1, 1)
            finish(chunk, 0)

            @pl.when(ci + 2 < per_worker)
            def _():
                start(chunk + 2, 0)

            finish(chunk + 1, 1)

    return run(table, idx2)


def _expert_kernel(first_ref, count_ref, used_ref,
                   xs_hbm, wgu_ref, wd_ref, y_hbm, xbuf, ybuf, in_sem, out_sem):
    e = pl.program_id(0)
    n_used = used_ref[0]
    first = first_ref[e]
    n_blocks = count_ref[e]

    def rows(g):
        return pl.ds(pl.multiple_of(g * ROW_BLK, ROW_BLK), ROW_BLK)

    def read(g):
        slot = g % EXP_IN_SLOTS
        return pltpu.make_async_copy(xs_hbm.at[rows(g), :], xbuf.at[slot], in_sem.at[slot])

    def write(g):
        slot = g % EXP_OUT_SLOTS
        return pltpu.make_async_copy(ybuf.at[slot], y_hbm.at[rows(g), :], out_sem.at[slot])

    @pl.when(e == 0)
    def _():
        for g0 in range(EXP_AHEAD):
            @pl.when(g0 < n_used)
            def _():
                read(g0).start()

    def acquire(g):
        @pl.when(g + EXP_AHEAD < n_used)
        def _():
            read(g + EXP_AHEAD).start()

        read(g).wait()

        @pl.when(g >= EXP_OUT_SLOTS)
        def _():
            write(g - EXP_OUT_SLOTS).wait()

    def compute(g):
        in_slot = g % EXP_IN_SLOTS
        out_slot = g % EXP_OUT_SLOTS
        for h in range(ROW_BLK // ROW_HALF):
            r0 = h * ROW_HALF
            x = _unpack_bf16(xbuf[in_slot, r0:r0 + ROW_HALF, :])
            gu = jnp.dot(x, wgu_ref[0], preferred_element_type=F32)
            act = _silu(gu[:, :D_EXPERT]) * gu[:, D_EXPERT:]
            yv = jnp.dot(act.astype(BF16), wd_ref[0], preferred_element_type=F32)
            words = pltpu.pack_elementwise([yv[:, :PACK_W], yv[:, PACK_W:]], packed_dtype=BF16)
            ybuf[out_slot, r0:r0 + ROW_HALF, :] = lax.bitcast_convert_type(words, jnp.int32)

    def pair(k, carry):
        g = first + 2 * k
        acquire(g)
        acquire(g + 1)
        compute(g)
        compute(g + 1)
        write(g).start()
        write(g + 1).start()
        return carry

    lax.fori_loop(0, n_blocks // 2, pair, 0)

    @pl.when(n_blocks % 2 == 1)
    def _():
        g = first + n_blocks - 1
        acquire(g)
        compute(g)
        write(g).start()

    @pl.when(e == pl.num_programs(0) - 1)
    def _():
        for back in range(EXP_OUT_SLOTS, 0, -1):
            @pl.when(n_used >= back)
            def _():
                write(n_used - back).wait()


def _experts(first_block, n_blocks, n_used, xs, w_gu, w_down):
    n_rows, w = xs.shape
    d = D_MODEL
    grid_spec = pltpu.PrefetchScalarGridSpec(
        num_scalar_prefetch=3,
        grid=(N_EXPERTS,),
        in_specs=[pl.BlockSpec(memory_space=pl.ANY),
                  pl.BlockSpec((1, d, 2 * D_EXPERT), lambda e, *_: (e, 0, 0)),
                  pl.BlockSpec((1, D_EXPERT, d), lambda e, *_: (e, 0, 0))],
        out_specs=pl.BlockSpec(memory_space=pl.ANY),
        scratch_shapes=[pltpu.VMEM((EXP_IN_SLOTS, ROW_BLK, w), jnp.int32),
                        pltpu.VMEM((EXP_OUT_SLOTS, ROW_BLK, w), jnp.int32),
                        pltpu.SemaphoreType.DMA((EXP_IN_SLOTS,)),
                        pltpu.SemaphoreType.DMA((EXP_OUT_SLOTS,))],
    )
    return pl.pallas_call(
        _expert_kernel,
        out_shape=jax.ShapeDtypeStruct((n_rows, w), jnp.int32),
        grid_spec=grid_spec,
        compiler_params=_cparams(("arbitrary",)),
        name="experts",
    )(first_block, n_blocks, n_used, xs, w_gu, w_down)


TM_COMBINE = 512


def _combine_kernel(*refs):
    yg_refs = refs[:TOP_K]
    gw_ref, x1_ref, xm_ref, g2_ref, wsgu_ref, wsd_ref, lg_ref, lb_ref, out_ref = refs[TOP_K:]
    gu = jnp.dot(_unpack_bf16(xm_ref[...]), wsgu_ref[...], preferred_element_type=F32)
    act = _silu(gu[:, :D_EXPERT]) * gu[:, D_EXPERT:]
    acc = jnp.dot(act.astype(BF16), wsd_ref[...], preferred_element_type=F32)
    gw = gw_ref[...]
    for j in range(TOP_K):
        acc = acc + gw[:, j:j + 1] * _unpack_rows(yg_refs[j][...])
    z = DEEPNORM_ALPHA * x1_ref[...] + g2_ref[0] * acc
    out_ref[...] = _layer_norm(z) * lg_ref[...] + lb_ref[...]


def _combine(yg, gw, x1, xm, gate2, w_sgu, w_sd, ln_g, ln_b, n_per_batch):
    t, d = x1.shape
    tm = TM_COMBINE
    per = n_per_batch // tm
    steps = t // tm
    tok = lambda w: pl.BlockSpec((tm, w), lambda i: (i, 0))
    full = lambda r, c: pl.BlockSpec((r, c), lambda i: (0, 0))
    slot = lambda j: pl.BlockSpec((tm, PACK_W), lambda i: (j * steps + i, 0))
    return pl.pallas_call(
        _combine_kernel,
        out_shape=jax.ShapeDtypeStruct((t, d), F32),
        grid=(steps,),
        in_specs=[slot(j) for j in range(TOP_K)]
        + [tok(TOP_K), tok(d), tok(PACK_W), pl.BlockSpec((1, 1, d), lambda i: (i // per, 0, 0)),
           full(d, 2 * D_EXPERT), full(D_EXPERT, d), full(1, d), full(1, d)],
        out_specs=tok(d),
        compiler_params=_cparams(("arbitrary",)),
        name="combine",
    )(*([yg] * TOP_K), gw, x1, xm, gate2, w_sgu, w_sd, ln_g, ln_b)


def _mixer_heads(stream, shift, scale, w_in_k, conv_w9, conv_b, gate_b_row, head_g, c0, m0, grid2d,
                 cast_weights=None):
    b, n, d = stream.shape
    y, f_in = _inproj(stream.reshape(b * n, d), shift, scale, w_in_k, n)
    y = y.reshape(b, n, PROJ_W)
    qk = _conv(y, conv_w9, conv_b, grid2d)
    return f_in, _mlstm(qk, y, _gates(y, gate_b_row), c0, m0, head_g, cast_weights)


def _moe(xm, x1, scores, bias_row, gate2, w_gu, w_down, w_sgu, w_sd, ln_g, ln_b, n_per_batch):
    t = xm.shape[0]
    n_assign = t * TOP_K
    n_rows = n_assign + N_EXPERTS * ROW_BLK
    idx_t, gw, rank_t, counts = _route(scores, bias_row)
    sizes = counts[:, 0].astype(jnp.int32)
    padded = (sizes + ROW_BLK - 1) // ROW_BLK * ROW_BLK
    pad_end = jnp.cumsum(padded)
    pad_start = (pad_end - padded).astype(jnp.int32)
    dest_t = _dest(pad_start, idx_t, rank_t)
    dest3 = dest_t.reshape(TOP_K, t // SC_ROWS, SC_ROWS).transpose(1, 0, 2)

    xs = _sc_dispatch(xm, dest3, n_rows)
    yb = _experts(pad_start // ROW_BLK, (padded // ROW_BLK).astype(jnp.int32),
                  (pad_end[-1:] // ROW_BLK).astype(jnp.int32), xs, w_gu, w_down)
    yg = _sc_gather(yb, dest_t.reshape(n_assign // SC_GATHER_ROWS, SC_GATHER_ROWS))
    return _combine(yg, gw, x1, xm, gate2, w_sgu, w_sd, ln_g, ln_b, n_per_batch)


def kernel(x, c, ctx, c_ctx, w_ada, b_ada, w_in, conv_w, conv_b, gate_b, head_g, w_out, ln1_g, ln1_b,
           w_router, router_bias, w_expert_gu, w_expert_down, w_shared_gu, w_shared_down, ln2_g, ln2_b):
    bsz, n, d = x.shape
    depth = w_ada.shape[0]
    assert depth == 1
    t = bsz * n
    tables = _dft_tables()
    for l in range(depth):
        cvec = jnp.zeros((SUBLANES, d), F32).at[:bsz].set(c).at[bsz].set(c_ctx)
        mod = _ada(cvec, w_ada[l], b_ada[l][None, :])
        mods = mod.reshape(SUBLANES, 6, 1, d)
        mod_x = [mods[:bsz, j] for j in range(6)]
        mod_c = [mods[bsz:bsz + 1, j] for j in range(2)]

        wl = w_in[l]
        g0 = 4 * D_MLSTM
        w_in_k = jnp.concatenate(
            [wl[:, :g0 + N_GATE_COLS], jnp.zeros((d, LANES - N_GATE_COLS), F32),
             wl[:, g0 + N_GATE_COLS:]], axis=1).astype(BF16)
        conv_w9 = conv_w[l].reshape(9, 2 * D_MLSTM)
        conv_b_row = conv_b[l][None, :]
        gate_b_row = jnp.zeros((1, LANES), F32).at[0, :N_GATE_COLS].set(gate_b[l].reshape(-1))
        head_g_row = head_g[l][None, :]

        c0 = jnp.zeros((bsz, N_HEADS, 2, STATE_ROWS, D_HEAD), F32)
        m0 = jnp.zeros((bsz, N_HEADS, 2, SUBLANES, LANES), F32)
        _, (_, c_ctx_fin, m_ctx_fin) = _mixer_heads(ctx, mod_c[0], mod_c[1], w_in_k, conv_w9, conv_b_row,
                                                    gate_b_row, head_g_row, c0, m0, grid2d=False)
        f_in, (mread, _, _, w_gu_bf, w_down_bf) = _mixer_heads(
            x, mod_x[0], mod_x[1], w_in_k, conv_w9, conv_b_row, gate_b_row, head_g_row,
            c_ctx_fin, m_ctx_fin, grid2d=True, cast_weights=(w_expert_gu[l], w_expert_down[l]))

        four = _fourier(f_in.reshape(bsz, DFT_R, DFT_R, D_FOURIER), tables).reshape(t, D_FOURIER)

        w_o = w_out[l].astype(BF16)
        w_r = jnp.zeros((d, LANES), F32).at[:, :N_EXPERTS].set(w_router[l]).astype(BF16)
        x2d = x.reshape(t, d)
        x1, xm, scores = _outproj(x2d, mread.reshape(t, D_MLSTM), four, w_o[:D_MLSTM], w_o[D_MLSTM:],
                                  mod_x[2], ln1_g[l][None, :], ln1_b[l][None, :], mod_x[3], mod_x[4],
                                  w_r, n)

        bias_rep = jnp.broadcast_to(router_bias[l][:, None], (N_EXPERTS, LANES))
        x = _moe(xm, x1, scores, bias_rep, mod_x[5], w_gu_bf, w_down_bf,
                 w_shared_gu[l].astype(BF16), w_shared_down[l].astype(BF16),
                 ln2_g[l][None, :], ln2_b[l][None, :], n).reshape(bsz, n, d)
    return x
```

```python
import functools

import numpy as np
import jax
import jax.numpy as jnp
from jax import lax
from jax.experimental import pallas as pl
from jax.experimental.pallas import tpu as pltpu
from jax.experimental.pallas import tpu_sc as plsc

F32 = jnp.float32
BF16 = jnp.bfloat16
HIGHEST = lax.Precision.HIGHEST

D_MODEL = 1024
GRID_W = 64
N_HEADS = 4
D_HEAD = 128
D_MLSTM = N_HEADS * D_HEAD
D_FOURIER = 512
FOURIER_GROUP = 64
N_GATE_COLS = 16
CHUNK = 128
N_EXPERTS = 64
TOP_K = 8
D_EXPERT = 256
ROUTED_SCALE = 2.5
DEEPNORM_ALPHA = 2.0 ** 0.25
LN_EPS = 1e-5

LANES = 128
SUBLANES = 8
VMEM_LIMIT = 48 * 1024 * 1024
FOUR_VMEM_LIMIT = 56 * 1024 * 1024

COL_QK = 0
COL_V = 2 * D_MLSTM
COL_O = 3 * D_MLSTM
COL_G = 4 * D_MLSTM
PROJ_W = COL_G + LANES

TN_ADA = 2048
TM = 1024
TM_ROUTE = 1024
ROW_BLK = 512
ROW_HALF = 256
EXP_AHEAD = 5
EXP_IN_SLOTS = 8
EXP_OUT_SLOTS = 4
PACK_W = D_MODEL // 2
SC_ROWS = 128
SC_GATHER_ROWS = 64
CONV_PAD = 72
CONV_RB = 256
DFT_R = 64


def _cparams(sem, vmem=VMEM_LIMIT):
    return pltpu.CompilerParams(dimension_semantics=sem, vmem_limit_bytes=vmem)


def _layer_norm(x):
    mu = jnp.mean(x, axis=-1, keepdims=True)
    xc = x - mu
    var = jnp.mean(xc * xc, axis=-1, keepdims=True)
    return xc * lax.rsqrt(var + LN_EPS)


def _silu(x):
    return x * jax.nn.sigmoid(x)


def _pack_pair(lo, hi):
    as_bits = lambda u: lax.bitcast_convert_type(u.astype(BF16).astype(F32), jnp.uint32)
    word = as_bits(hi) | lax.shift_right_logical(as_bits(lo), jnp.uint32(16))
    return lax.bitcast_convert_type(word, jnp.int32)


def _unpack_pair_bf16(w):
    lo = lax.bitcast_convert_type(w.astype(jnp.int16), BF16)
    hi = lax.bitcast_convert_type(lax.shift_right_logical(w, jnp.int32(16)).astype(jnp.int16), BF16)
    return lo, hi


def _pack_rows(v):
    return _pack_pair(v[:, :PACK_W], v[:, PACK_W:])


def _unpack_bf16(w):
    return jnp.concatenate(_unpack_pair_bf16(w), axis=-1)


def _unpack_rows(w):
    u = lax.bitcast_convert_type(w, jnp.uint32)
    lo = lax.bitcast_convert_type(lax.shift_left(u, jnp.uint32(16)), F32)
    hi = lax.bitcast_convert_type(u & jnp.uint32(0xFFFF0000), F32)
    return jnp.concatenate([lo, hi], axis=1)


def _ada_kernel(c_ref, w_ref, b_ref, o_ref):
    s = _silu(c_ref[...]).astype(BF16)
    o_ref[...] = jnp.dot(s, w_ref[...].astype(BF16), preferred_element_type=F32) + b_ref[...]


def _ada(cvec, w_ada, b_ada):
    rows, d = cvec.shape
    n_out = w_ada.shape[1]
    tn = TN_ADA
    return pl.pallas_call(
        _ada_kernel,
        out_shape=jax.ShapeDtypeStruct((rows, n_out), F32),
        grid=(n_out // tn,),
        in_specs=[pl.BlockSpec((rows, d), lambda j: (0, 0)),
                  pl.BlockSpec((d, tn), lambda j: (0, j)),
                  pl.BlockSpec((1, tn), lambda j: (0, j))],
        out_specs=pl.BlockSpec((rows, tn), lambda j: (0, j)),
        compiler_params=_cparams(("arbitrary",)),
        name="ada",
    )(cvec, w_ada, b_ada)


def _inproj_kernel(x_ref, sh_ref, sc_ref, w_ref, y_ref, f_ref):
    half = x_ref.shape[0] // 2
    for h in range(2):
        r = slice(h * half, (h + 1) * half)
        u = _layer_norm(x_ref[r, :]) * (1.0 + sc_ref[0]) + sh_ref[0]
        proj = jnp.dot(u.astype(BF16), w_ref[...], preferred_element_type=F32)
        y_ref[r, :] = proj[:, :PROJ_W]
        f_ref[r, :] = proj[:, PROJ_W:]


def _inproj(x2d, shift, scale, w, n_per_batch):
    t, d = x2d.shape
    if shift.shape[0] == 1:
        tm = min(TM, t)
        per = t // tm
    else:
        tm = min(TM, n_per_batch)
        per = n_per_batch // tm
    return pl.pallas_call(
        _inproj_kernel,
        out_shape=(jax.ShapeDtypeStruct((t, PROJ_W), F32), jax.ShapeDtypeStruct((t, D_FOURIER), F32)),
        grid=(t // tm,),
        in_specs=[pl.BlockSpec((tm, d), lambda i: (i, 0)),
                  pl.BlockSpec((1, 1, d), lambda i: (i // per, 0, 0)),
                  pl.BlockSpec((1, 1, d), lambda i: (i // per, 0, 0)),
                  pl.BlockSpec((d, PROJ_W + D_FOURIER), lambda i: (0, 0))],
        out_specs=(pl.BlockSpec((tm, PROJ_W), lambda i: (i, 0)),
                   pl.BlockSpec((tm, D_FOURIER), lambda i: (i, 0))),
        compiler_params=_cparams(("arbitrary",)),
        name="inproj",
    )(x2d, shift, scale, w)


def _conv_kernel(y_ref, w_ref, b_ref, o_ref, pad_ref, *, n, grid2d, tiles):
    zeros = jnp.zeros((CONV_PAD, LANES), F32)
    pad_ref[0:CONV_PAD, :] = zeros
    pad_ref[CONV_PAD + n:CONV_PAD + n + CONV_PAD, :] = zeros
    rb = min(CONV_RB, n)
    col = lax.broadcasted_iota(jnp.int32, (rb, LANES), 0) % GRID_W
    not_first = col >= 1
    not_last = col <= GRID_W - 2
    row_taps = (0, 1, 2) if grid2d else (1,)
    for ct in range(tiles):
        lanes = slice(ct * LANES, (ct + 1) * LANES)
        pad_ref[CONV_PAD:CONV_PAD + n, :] = y_ref[0, :, lanes]
        k_scale = jnp.where(pl.program_id(1) * tiles + ct >= N_HEADS, D_HEAD ** -0.5, 1.0).astype(F32)
        bias = b_ref[:, lanes]
        for blk in range(n // rb):
            r0 = CONV_PAD + blk * rb
            acc = jnp.zeros((rb, LANES), F32)
            for dj in range(3):
                part = jnp.zeros((rb, LANES), F32)
                for di in row_taps:
                    off = (di - 1) * GRID_W + (dj - 1)
                    tap = w_ref[di * 3 + dj:di * 3 + dj + 1, lanes]
                    part = part + tap * pad_ref[r0 + off:r0 + off + rb, :]
                if grid2d and dj == 0:
                    part = jnp.where(not_first, part, 0.0)
                if grid2d and dj == 2:
                    part = jnp.where(not_last, part, 0.0)
                acc = acc + part
            val = _silu(acc + bias) * k_scale
            o_ref[0, blk * rb:(blk + 1) * rb, lanes] = val.astype(o_ref.dtype)


def _conv(y3d, conv_w9, conv_b, grid2d):
    b, n, _ = y3d.shape
    n_ct = 2 * D_MLSTM // LANES
    tiles = 1 if grid2d else n_ct
    w = tiles * LANES
    return pl.pallas_call(
        functools.partial(_conv_kernel, n=n, grid2d=grid2d, tiles=tiles),
        out_shape=jax.ShapeDtypeStruct((b, n, 2 * D_MLSTM), BF16),
        grid=(b, n_ct // tiles),
        in_specs=[pl.BlockSpec((1, n, w), lambda i, c: (i, 0, c)),
                  pl.BlockSpec((9, w), lambda i, c: (0, c)),
                  pl.BlockSpec((1, w), lambda i, c: (0, c))],
        out_specs=pl.BlockSpec((1, n, w), lambda i, c: (i, 0, c)),
        scratch_shapes=[pltpu.VMEM((n + 2 * CONV_PAD, LANES), F32)],
        compiler_params=_cparams(("arbitrary", "arbitrary")),
        name="conv2d" if grid2d else "conv1d",
    )(y3d, conv_w9, conv_b)


STATE_ROWS = 2 * D_HEAD
GATE_A, GATE_CM, GATE_B = 0, 1, 2
GATE_GROUP = 32


def _gates_kernel(g_ref, gb_ref, rv_ref, *, nc):
    u_i = lax.broadcasted_iota(jnp.int32, (CHUNK, CHUNK), 0)
    t_i = lax.broadcasted_iota(jnp.int32, (CHUNK, CHUNK), 1)
    prefix = (u_i <= t_i).astype(F32)
    suffix = (u_i >= t_i).astype(F32)
    grp = min(GATE_GROUP, nc)
    rows = grp * SUBLANES
    lane = lax.broadcasted_iota(jnp.int32, (rows, LANES), 1)
    is_fwd = lax.broadcasted_iota(jnp.int32, (rows, LANES), 0) % SUBLANES < N_HEADS
    h = N_HEADS

    def running_max(a):
        fwd, bwd = a, a
        sh = 1
        while sh < CHUNK:
            fwd = jnp.where(lane >= sh, jnp.maximum(fwd, pltpu.roll(fwd, sh, axis=1)), fwd)
            bwd = jnp.where(lane < CHUNK - sh, jnp.maximum(bwd, pltpu.roll(bwd, CHUNK - sh, axis=1)), bwd)
            sh *= 2
        return jnp.where(is_fwd, fwd, bwd)

    def body(i, carry):
        c0 = i * grp
        r0 = pl.multiple_of(c0 * CHUNK, CHUNK)
        pre = g_ref[0, pl.ds(r0, grp * CHUNK), :] + gb_ref[...]
        pre_t = jnp.concatenate(
            [pre[j * CHUNK:(j + 1) * CHUNK].T[:N_GATE_COLS, :] for j in range(grp)], axis=0)
        lf_t = -(jnp.maximum(-pre_t, 0.0) + jnp.log1p(jnp.exp(-jnp.abs(pre_t))))
        cum_f = jnp.dot(lf_t, prefix, precision=HIGHEST, preferred_element_type=F32)
        cum_b = jnp.dot(lf_t, suffix, precision=HIGHEST, preferred_element_type=F32)
        pick = lambda t, j, q: t[j * N_GATE_COLS + q * h:j * N_GATE_COLS + (q + 1) * h]
        b8 = jnp.concatenate([x for j in range(grp) for x in (pick(cum_f, j, 1), pick(cum_b, j, 3))], axis=0)
        li8 = jnp.concatenate([x for j in range(grp) for x in (pick(pre_t, j, 0), pick(pre_t, j, 2))], axis=0)
        a8 = li8 - b8
        as_chunks = lambda t: t.reshape(grp, SUBLANES, LANES)
        rv_ref[0, pl.ds(c0, grp), GATE_A] = as_chunks(a8)
        rv_ref[0, pl.ds(c0, grp), GATE_CM] = as_chunks(running_max(a8))
        rv_ref[0, pl.ds(c0, grp), GATE_B] = as_chunks(b8)
        return carry

    lax.fori_loop(0, nc // grp, body, 0)


def _gates(y3d, gate_b_row):
    b, n, _ = y3d.shape
    nc = n // CHUNK
    return pl.pallas_call(
        functools.partial(_gates_kernel, nc=nc),
        out_shape=jax.ShapeDtypeStruct((b, nc, 3, SUBLANES, LANES), F32),
        grid=(b,),
        in_specs=[pl.BlockSpec((1, n, LANES), lambda i: (i, 0, COL_G // LANES)),
                  pl.BlockSpec((1, LANES), lambda i: (0, 0))],
        out_specs=pl.BlockSpec((1, nc, 3, SUBLANES, LANES), lambda i: (i, 0, 0, 0, 0)),
        compiler_params=_cparams(("arbitrary",)),
        name="gates",
    )(y3d, gate_b_row)


def _mlstm_kernel(q_ref, k_ref, v_ref, o_ref, rv_ref, c0_ref, m0_ref, hg_ref,
                  out_ref, cfin_ref, mfin_ref, hf_ref, hb_ref, cst_ref, mst_ref, *, nc, heads):
    cst_ref[...] = c0_ref[0]
    mst_ref[...] = m0_ref[0]
    s_i = lax.broadcasted_iota(jnp.int32, (CHUNK, CHUNK), 0)
    t_i = lax.broadcasted_iota(jnp.int32, (CHUNK, CHUNK), 1)
    ones_row = (s_i[:STATE_ROWS - D_HEAD] == 0).astype(F32)

    def one_chunk(c, d, mask_t, last, h_ref, hh):
        r0 = pl.multiple_of(c * CHUNK, CHUNK)
        lanes = slice(hh * D_HEAD, (hh + 1) * D_HEAD)
        g = d * N_HEADS + pl.program_id(1) * heads + hh
        gate_row = lambda tbl: rv_ref[0, pl.ds(c, 1), tbl, pl.ds(g, 1), :].reshape(1, LANES)
        a_row, cm_row, b_row = gate_row(GATE_A), gate_row(GATE_CM), gate_row(GATE_B)
        q_t = q_ref[0, pl.ds(r0, CHUNK), lanes].T
        k = k_ref[0, pl.ds(r0, CHUNK), lanes]
        v_t_aug = jnp.concatenate([v_ref[0, pl.ds(r0, CHUNK), lanes].T, ones_row], axis=0)
        m_prev = mst_ref[hh, d, 0:1, :]
        m_row = jnp.maximum(m_prev, cm_row)
        a_bc = jnp.broadcast_to(a_row, (CHUNK, CHUNK)).T
        d_t = jnp.where(mask_t, jnp.exp(a_bc - m_row), 0.0)
        s_t = jnp.dot(k, q_t, preferred_element_type=F32) * d_t
        cs = cst_ref[hh, d]
        rhs = jnp.concatenate(
            [s_t.astype(BF16), (q_t.astype(F32) * jnp.exp(m_prev - m_row)).astype(BF16)], axis=0)
        lhs = jnp.concatenate([v_t_aug.astype(BF16), cs.astype(BF16)], axis=1)
        h_t = jnp.dot(lhs, rhs, preferred_element_type=F32)
        den = h_t[D_HEAD:D_HEAD + 1, :]
        floor = jnp.exp(-b_row - m_row)
        h_ref[hh, pl.ds(c, 1)] = (h_t[:D_HEAD] / jnp.maximum(jnp.abs(den), floor))[None]
        m_last = m_row[:, last:last + 1]
        wv_t = (v_t_aug * jnp.exp(a_row - m_last)).astype(BF16)
        cst_ref[hh, d] = jnp.exp(m_prev - m_last) * cs + jnp.dot(wv_t, k, preferred_element_type=F32)
        mst_ref[hh, d] = jnp.broadcast_to(b_row[:, last:last + 1] + m_last, (SUBLANES, LANES))

    def body(c, carry):
        for hh in range(heads):
            one_chunk(c, 0, s_i <= t_i, CHUNK - 1, hf_ref, hh)
            one_chunk(nc - 1 - c, 1, s_i >= t_i, 0, hb_ref, hh)
        return carry

    lax.fori_loop(0, nc, body, 0, unroll=min(16 // heads, nc))
    cfin_ref[0] = cst_ref[...]
    mfin_ref[0] = mst_ref[...]

    def readout(c, carry):
        r0 = pl.multiple_of(c * CHUNK, CHUNK)
        for hh in range(heads):
            lanes = slice(hh * D_HEAD, (hh + 1) * D_HEAD)
            h_t = hf_ref[hh, pl.ds(c, 1)][0] + hb_ref[hh, pl.ds(c, 1)][0]
            hc = h_t - jnp.mean(h_t, axis=0, keepdims=True)
            hn = (hc * lax.rsqrt(jnp.mean(hc * hc, axis=0, keepdims=True) + LN_EPS)).T
            gate = jax.nn.sigmoid(o_ref[0, pl.ds(r0, CHUNK), lanes])
            out_ref[0, pl.ds(r0, CHUNK), lanes] = (hn * hg_ref[:, lanes] * gate).astype(out_ref.dtype)
        return carry

    lax.fori_loop(0, nc, readout, 0, unroll=min(16, nc))


def _mlstm(qk, y3d, rowv, c0, m0, head_g):
    b, n, _ = qk.shape
    nc = n // CHUNK
    heads = N_HEADS if nc <= 4 else 1
    w = heads * D_HEAD
    v_blk = COL_V // w
    o_blk = COL_O // w
    tok = lambda off: pl.BlockSpec((1, n, w), lambda i, h: (i, 0, off + h))
    state_c = pl.BlockSpec((1, heads, 2, STATE_ROWS, D_HEAD), lambda i, h: (i, h, 0, 0, 0))
    state_m = pl.BlockSpec((1, heads, 2, SUBLANES, LANES), lambda i, h: (i, h, 0, 0, 0))
    return pl.pallas_call(
        functools.partial(_mlstm_kernel, nc=nc, heads=heads),
        out_shape=(jax.ShapeDtypeStruct((b, n, D_MLSTM), BF16),
                   jax.ShapeDtypeStruct((b, N_HEADS, 2, STATE_ROWS, D_HEAD), F32),
                   jax.ShapeDtypeStruct((b, N_HEADS, 2, SUBLANES, LANES), F32)),
        grid=(b, N_HEADS // heads),
        in_specs=[tok(0), tok(D_MLSTM // w), tok(v_blk), tok(o_blk),
                  pl.BlockSpec((1, nc, 3, SUBLANES, LANES), lambda i, h: (i, 0, 0, 0, 0)),
                  state_c, state_m,
                  pl.BlockSpec((1, w), lambda i, h: (0, h))],
        out_specs=(pl.BlockSpec((1, n, w), lambda i, h: (i, 0, h)), state_c, state_m),
        scratch_shapes=[pltpu.VMEM((heads, nc, D_HEAD, CHUNK), F32), pltpu.VMEM((heads, nc, D_HEAD, CHUNK), F32),
                        pltpu.VMEM((heads, 2, STATE_ROWS, D_HEAD), F32),
                        pltpu.VMEM((heads, 2, SUBLANES, LANES), F32)],
        compiler_params=_cparams(("arbitrary", "arbitrary")),
        name="mlstm",
    )(qk, qk, y3d, y3d, rowv, c0, m0, head_g)


def _dft_tables():
    r = DFT_R
    idx = np.arange(r)
    cg = np.cos(2 * np.pi * np.outer(idx, idx) / FOURIER_GROUP)
    sg = np.sin(2 * np.pi * np.outer(idx, idx) / FOURIER_GROUP)
    n_grp = D_FOURIER // FOURIER_GROUP
    bdc = np.kron(np.eye(n_grp), cg) / 8.0
    bds = np.kron(np.eye(n_grp), sg) / 8.0
    k1 = idx[:, None]
    a = idx[None, :]
    m1 = np.zeros((r, 2 * r, 2 * r))
    for b2 in range(r):
        th = 2 * np.pi * k1 * (r * a + b2) / (r * r)
        ec, es = np.cos(th), np.sin(th)
        m1[b2] = np.block([[ec, -es], [es, ec]]) / 8.0
    th2 = 2 * np.pi * np.outer(idx, idx) / r
    m2 = np.concatenate([np.cos(th2), -np.sin(th2)], axis=1) / 8.0
    as_bf16 = lambda t: jnp.asarray(t, dtype=F32).astype(BF16)
    return as_bf16(bdc), as_bf16(bds), as_bf16(m1), as_bf16(m2)


FOUR_BB = 8


N_SLABS = D_FOURIER // LANES


def _to_slabs(slab_ref, val):
    for s in range(N_SLABS):
        slab_ref[s] = val[:, s * LANES:(s + 1) * LANES]


def _strided_rows(slab_ref, start):
    return jnp.concatenate(
        [slab_ref[s, pl.ds(start, DFT_R, stride=FOUR_BB), :] for s in range(N_SLABS)], axis=1)


def _four_kernel(f_ref, bdc_ref, bds_ref, m1_ref, m2_ref, o_ref, z_ref, ps_ref, qs_ref, zs_ref, os_ref):
    r = DFT_R

    def stage1(blk, carry):
        b0 = pl.multiple_of(blk * FOUR_BB, FOUR_BB)
        fb = f_ref[0, :, pl.ds(b0, FOUR_BB), :].reshape(r * FOUR_BB, D_FOURIER).astype(BF16)
        _to_slabs(ps_ref, jnp.dot(fb, bdc_ref[...], preferred_element_type=F32))
        _to_slabs(qs_ref, jnp.dot(fb, bds_ref[...], preferred_element_type=F32))
        for j in range(FOUR_BB):
            x2 = jnp.concatenate([_strided_rows(ps_ref, j), _strided_rows(qs_ref, j)], axis=0).astype(BF16)
            zz = jnp.dot(m1_ref[b0 + j], x2, preferred_element_type=F32)
            z_ref[b0 + j] = _pack_pair(zz[:r], zz[r:])
        return carry

    lax.fori_loop(0, r // FOUR_BB, stage1, 0)

    def stage2(blk, carry):
        k0 = pl.multiple_of(blk * FOUR_BB, FOUR_BB)
        _to_slabs(zs_ref, z_ref[:, pl.ds(k0, FOUR_BB), :].reshape(r * FOUR_BB, D_FOURIER))
        for j in range(FOUR_BB):
            zc, zs = _unpack_pair_bf16(_strided_rows(zs_ref, j))
            o = jnp.dot(m2_ref[...], jnp.concatenate([zc, zs], axis=0), preferred_element_type=F32)
            for s in range(N_SLABS):
                os_ref[s, pl.ds(j, r, stride=FOUR_BB), :] = o[:, s * LANES:(s + 1) * LANES]
        for s in range(N_SLABS):
            o_ref[0, :, pl.ds(k0, FOUR_BB), s * LANES:(s + 1) * LANES] = os_ref[s].reshape(r, FOUR_BB, LANES)
        return carry

    lax.fori_loop(0, r // FOUR_BB, stage2, 0)


def _fourier(y4, tables):
    bdc, bds, m1, m2 = tables
    b = y4.shape[0]
    r = DFT_R
    slab = lambda dt: pltpu.VMEM((N_SLABS, r * FOUR_BB, LANES), dt)
    const = lambda *shape: pl.BlockSpec(shape, lambda i: (0,) * len(shape))
    return pl.pallas_call(
        _four_kernel,
        out_shape=jax.ShapeDtypeStruct((b, r, r, D_FOURIER), F32),
        grid=(b,),
        in_specs=[pl.BlockSpec((1, r, r, D_FOURIER), lambda i: (i, 0, 0, 0)),
                  const(D_FOURIER, D_FOURIER), const(D_FOURIER, D_FOURIER),
                  const(r, 2 * r, 2 * r), const(r, 2 * r)],
        out_specs=pl.BlockSpec((1, r, r, D_FOURIER), lambda i: (i, 0, 0, 0)),
        scratch_shapes=[pltpu.VMEM((r, r, D_FOURIER), jnp.int32), slab(F32), slab(F32), slab(jnp.int32), slab(F32)],
        compiler_params=_cparams(("arbitrary",), vmem=FOUR_VMEM_LIMIT),
        name="fourier",
    )(y4, bdc, bds, m1, m2)


def _outproj_kernel(x_ref, mr_ref, fo_ref, wm_ref, wf_ref, g1_ref, lg_ref, lb_ref,
                    sh_ref, sc_ref, wr_ref, x1_ref, xm_ref, s_ref):
    half = x_ref.shape[0] // 2
    for h in range(2):
        r = slice(h * half, (h + 1) * half)
        mix = (jnp.dot(mr_ref[r, :], wm_ref[...], preferred_element_type=F32)
               + jnp.dot(fo_ref[r, :].astype(BF16), wf_ref[...], preferred_element_type=F32))
        x1 = _layer_norm(DEEPNORM_ALPHA * x_ref[r, :] + g1_ref[0] * mix) * lg_ref[...] + lb_ref[...]
        x1_ref[r, :] = x1
        xm = (_layer_norm(x1) * (1.0 + sc_ref[0]) + sh_ref[0]).astype(BF16)
        xm_ref[r, :] = _pack_rows(xm)
        logits = jnp.dot(xm, wr_ref[...], preferred_element_type=F32)
        s_ref[r, :] = jax.nn.sigmoid(logits)


def _outproj(x2d, mread, four, w_m, w_f, gate1, ln_g, ln_b, shift2, scale2, w_router, n_per_batch):
    t, d = x2d.shape
    per = n_per_batch // TM
    tokd = lambda w: pl.BlockSpec((TM, w), lambda i: (i, 0))
    full = lambda r, c: pl.BlockSpec((r, c), lambda i: (0, 0))
    mod = pl.BlockSpec((1, 1, d), lambda i: (i // per, 0, 0))
    return pl.pallas_call(
        _outproj_kernel,
        out_shape=(jax.ShapeDtypeStruct((t, d), F32),
                   jax.ShapeDtypeStruct((t, PACK_W), jnp.int32),
                   jax.ShapeDtypeStruct((t, LANES), F32)),
        grid=(t // TM,),
        in_specs=[tokd(d), tokd(D_MLSTM), tokd(D_FOURIER), full(D_MLSTM, d), full(D_FOURIER, d),
                  mod, full(1, d), full(1, d), mod, mod, full(d, LANES)],
        out_specs=(tokd(d), tokd(PACK_W), tokd(LANES)),
        compiler_params=_cparams(("arbitrary",)),
        name="outproj",
    )(x2d, mread, four, w_m, w_f, gate1, ln_g, ln_b, shift2, scale2, w_router)


def _route_kernel(s_ref, b_ref, idx_ref, gw_ref, rank_ref, cnt_ref, carry_ref):
    @pl.when(pl.program_id(0) == 0)
    def _():
        carry_ref[...] = jnp.zeros_like(carry_ref)

    tm = s_ref.shape[0]
    reps = tm // LANES
    s_t = s_ref[...].T[:N_EXPERTS, :]
    e_i = lax.broadcasted_iota(jnp.int32, (N_EXPERTS, tm), 0).astype(F32)
    neg_inf = jnp.float32(-jnp.inf)
    sb = s_t + jnp.tile(b_ref[...], (1, reps))
    picks, sels = [], []
    taken = jnp.zeros((N_EXPERTS, tm), F32)
    for _ in range(TOP_K):
        best = jnp.max(sb, axis=0, keepdims=True)
        pick = jnp.min(jnp.where(sb == best, e_i, float(N_EXPERTS)), axis=0, keepdims=True)
        onehot = e_i == pick
        sels.append(jnp.sum(jnp.where(onehot, s_t, 0.0), axis=0, keepdims=True))
        picks.append(pick)
        sb = jnp.where(onehot, neg_inf, sb)
        taken = taken + onehot.astype(F32)
    u_i = lax.broadcasted_iota(jnp.int32, (tm, tm), 0)
    t_i = lax.broadcasted_iota(jnp.int32, (tm, tm), 1)
    before = (u_i < t_i).astype(BF16)
    taken_bf = taken.astype(BF16)
    rank_all = jnp.tile(carry_ref[...], (1, reps)) + jnp.dot(taken_bf, before, preferred_element_type=F32)
    carry_ref[...] = carry_ref[...] + jnp.dot(taken_bf, jnp.ones((tm, LANES), BF16),
                                              preferred_element_type=F32)
    cnt_ref[...] = carry_ref[...]
    total = sels[0]
    for v in sels[1:]:
        total = total + v
    ranks = [jnp.sum(jnp.where(e_i == picks[j], rank_all, 0.0), axis=0, keepdims=True) for j in range(TOP_K)]
    idx_ref[...] = jnp.concatenate(picks, axis=0).astype(jnp.int32)
    rank_ref[...] = jnp.concatenate(ranks, axis=0).astype(jnp.int32)
    gw_t = jnp.concatenate([ROUTED_SCALE * v / total for v in sels]
                           + [jnp.zeros((LANES - TOP_K, tm), F32)], axis=0)
    gw_ref[...] = gw_t.T[:, :TOP_K]


def _route(scores, bias_rep):
    t = scores.shape[0]
    tok = lambda w: pl.BlockSpec((TM_ROUTE, w), lambda i: (i, 0))
    slot_major = pl.BlockSpec((TOP_K, TM_ROUTE), lambda i: (0, i))
    per_expert = pl.BlockSpec((N_EXPERTS, LANES), lambda i: (0, 0))
    return pl.pallas_call(
        _route_kernel,
        out_shape=(jax.ShapeDtypeStruct((TOP_K, t), jnp.int32),
                   jax.ShapeDtypeStruct((t, TOP_K), F32),
                   jax.ShapeDtypeStruct((TOP_K, t), jnp.int32),
                   jax.ShapeDtypeStruct((N_EXPERTS, LANES), F32)),
        grid=(t // TM_ROUTE,),
        in_specs=[tok(LANES), per_expert],
        out_specs=(slot_major, tok(TOP_K), slot_major, per_expert),
        scratch_shapes=[pltpu.VMEM((N_EXPERTS, LANES), F32)],
        compiler_params=_cparams(("arbitrary",)),
        name="route",
    )(scores, bias_rep)


TN_DEST = 2048


def _dest_kernel(ps_ref, idx_ref, rank_ref, dest_ref):
    idx = idx_ref[...]
    start = jnp.zeros(idx.shape, jnp.int32)
    for e in range(N_EXPERTS):
        start = jnp.where(idx == e, ps_ref[e], start)
    dest_ref[...] = start + rank_ref[...]


def _dest(pad_start, idx_t, rank_t):
    t = idx_t.shape[1]
    blk = lambda: pl.BlockSpec((TOP_K, TN_DEST), lambda i, ps: (0, i))
    grid_spec = pltpu.PrefetchScalarGridSpec(
        num_scalar_prefetch=1, grid=(t // TN_DEST,), in_specs=[blk(), blk()], out_specs=blk())
    return pl.pallas_call(
        _dest_kernel,
        out_shape=jax.ShapeDtypeStruct((TOP_K, t), jnp.int32),
        grid_spec=grid_spec,
        compiler_params=_cparams(("arbitrary",)),
        name="dest",
    )(pad_start, idx_t, rank_t)


def _sc_worker_id(info):
    return lax.axis_index("s") * info.num_cores + lax.axis_index("c")


def _sc_dispatch(xm, dest3, n_rows):
    t, w = xm.shape
    info = plsc.get_sparse_core_info()
    n_workers = info.num_cores * info.num_subcores
    per_worker = t // SC_ROWS // n_workers
    mesh = plsc.VectorSubcoreMesh(core_axis_name="c", subcore_axis_name="s")

    @functools.partial(
        pl.kernel, mesh=mesh,
        out_type=jax.ShapeDtypeStruct((n_rows, w), xm.dtype),
        scratch_types=[pltpu.VMEM((TOP_K, SC_ROWS), jnp.int32), pltpu.VMEM((SC_ROWS, w), xm.dtype)],
        name="sc_dispatch",
    )
    def run(xm_hbm, dest_hbm, xs_hbm, idx_v, rows_v):
        first = _sc_worker_id(info) * per_worker

        @pl.loop(0, per_worker)
        def _(ci):
            chunk = first + ci
            pltpu.sync_copy(dest_hbm.at[chunk], idx_v)
            pltpu.sync_copy(xm_hbm.at[pl.ds(chunk * SC_ROWS, SC_ROWS)], rows_v)
            for j in range(TOP_K):
                pltpu.sync_copy(rows_v, xs_hbm.at[idx_v.at[j]])

    return run(xm, dest3)


def _sc_gather(table, idx2):
    n_chunks, rows = idx2.shape
    w = table.shape[1]
    info = plsc.get_sparse_core_info()
    n_workers = info.num_cores * info.num_subcores
    per_worker = n_chunks // n_workers
    assert per_worker % 2 == 0
    mesh = plsc.VectorSubcoreMesh(core_axis_name="c", subcore_axis_name="s")

    @functools.partial(
        pl.kernel, mesh=mesh,
        out_type=jax.ShapeDtypeStruct((n_chunks * rows, w), table.dtype),
        scratch_types=[pltpu.VMEM((2, rows), jnp.int32), pltpu.VMEM((2, rows, w), table.dtype),
                       pltpu.SemaphoreType.DMA, pltpu.SemaphoreType.DMA],
        name="sc_gather",
    )
    def run(table_hbm, idx_hbm, out_hbm, idx_v, rows_v, sem0, sem1):
        first = _sc_worker_id(info) * per_worker
        sems = (sem0, sem1)

        def gather(b):
            return pltpu.make_async_copy(table_hbm.at[idx_v.at[b]], rows_v.at[b], sems[b])

        def start(chunk, b):
            pltpu.sync_copy(idx_hbm.at[chunk], idx_v.at[b])
            gather(b).start()

        def finish(chunk, b):
            gather(b).wait()
            pltpu.sync_copy(rows_v.at[b], out_hbm.at[pl.ds(chunk * rows, rows)])

        start(first, 0)

        @pl.loop(0, per_worker, step=2)
        def _(ci):
            chunk = first + ci
            start(chunk + 1, 1)
            finish(chunk, 0)

            @pl.when(ci + 2 < per_worker)
            def _():
                start(chunk + 2, 0)

            finish(chunk + 1, 1)

    return run(table, idx2)


def _expert_kernel(first_ref, count_ref, used_ref,
                   xs_hbm, wgu_ref, wd_ref, y_hbm, wgu_bf, wd_bf, xbuf, ybuf, in_sem, out_sem):
    e = pl.program_id(0)
    n_used = used_ref[0]
    first = first_ref[e]
    n_blocks = count_ref[e]

    def rows(g):
        return pl.ds(pl.multiple_of(g * ROW_BLK, ROW_BLK), ROW_BLK)

    def read(g):
        slot = g % EXP_IN_SLOTS
        return pltpu.make_async_copy(xs_hbm.at[rows(g), :], xbuf.at[slot], in_sem.at[slot])

    def write(g):
        slot = g % EXP_OUT_SLOTS
        return pltpu.make_async_copy(ybuf.at[slot], y_hbm.at[rows(g), :], out_sem.at[slot])

    @pl.when(e == 0)
    def _():
        for g0 in range(EXP_AHEAD):
            @pl.when(g0 < n_used)
            def _():
                read(g0).start()

    @pl.when(n_blocks > 0)
    def _():
        wgu_bf[...] = wgu_ref[0].astype(BF16)
        wd_bf[...] = wd_ref[0].astype(BF16)

    def acquire(g):
        @pl.when(g + EXP_AHEAD < n_used)
        def _():
            read(g + EXP_AHEAD).start()

        read(g).wait()

        @pl.when(g >= EXP_OUT_SLOTS)
        def _():
            write(g - EXP_OUT_SLOTS).wait()

    def compute(g):
        in_slot = g % EXP_IN_SLOTS
        out_slot = g % EXP_OUT_SLOTS
        for h in range(ROW_BLK // ROW_HALF):
            r0 = h * ROW_HALF
            x = _unpack_bf16(xbuf[in_slot, r0:r0 + ROW_HALF, :])
            gu = jnp.dot(x, wgu_bf[...], preferred_element_type=F32)
            act = _silu(gu[:, :D_EXPERT]) * gu[:, D_EXPERT:]
            yv = jnp.dot(act.astype(BF16), wd_bf[...], preferred_element_type=F32)
            words = pltpu.pack_elementwise([yv[:, :PACK_W], yv[:, PACK_W:]], packed_dtype=BF16)
            ybuf[out_slot, r0:r0 + ROW_HALF, :] = lax.bitcast_convert_type(words, jnp.int32)

    def pair(k, carry):
        g = first + 2 * k
        acquire(g)
        acquire(g + 1)
        compute(g)
        compute(g + 1)
        write(g).start()
        write(g + 1).start()
        return carry

    lax.fori_loop(0, n_blocks // 2, pair, 0)

    @pl.when(n_blocks % 2 == 1)
    def _():
        g = first + n_blocks - 1
        acquire(g)
        compute(g)
        write(g).start()

    @pl.when(e == pl.num_programs(0) - 1)
    def _():
        for back in range(EXP_OUT_SLOTS, 0, -1):
            @pl.when(n_used >= back)
            def _():
                write(n_used - back).wait()


def _experts(first_block, n_blocks, n_used, xs, w_gu, w_down):
    n_rows, w = xs.shape
    d = D_MODEL
    grid_spec = pltpu.PrefetchScalarGridSpec(
        num_scalar_prefetch=3,
        grid=(N_EXPERTS,),
        in_specs=[pl.BlockSpec(memory_space=pl.ANY),
                  pl.BlockSpec((1, d, 2 * D_EXPERT), lambda e, *_: (e, 0, 0)),
                  pl.BlockSpec((1, D_EXPERT, d), lambda e, *_: (e, 0, 0))],
        out_specs=pl.BlockSpec(memory_space=pl.ANY),
        scratch_shapes=[pltpu.VMEM((d, 2 * D_EXPERT), BF16), pltpu.VMEM((D_EXPERT, d), BF16),
                        pltpu.VMEM((EXP_IN_SLOTS, ROW_BLK, w), jnp.int32),
                        pltpu.VMEM((EXP_OUT_SLOTS, ROW_BLK, w), jnp.int32),
                        pltpu.SemaphoreType.DMA((EXP_IN_SLOTS,)),
                        pltpu.SemaphoreType.DMA((EXP_OUT_SLOTS,))],
    )
    return pl.pallas_call(
        _expert_kernel,
        out_shape=jax.ShapeDtypeStruct((n_rows, w), jnp.int32),
        grid_spec=grid_spec,
        compiler_params=_cparams(("arbitrary",)),
        name="experts",
    )(first_block, n_blocks, n_used, xs, w_gu, w_down)


TM_COMBINE = 512


def _combine_kernel(*refs):
    yg_refs = refs[:TOP_K]
    gw_ref, x1_ref, xm_ref, g2_ref, wsgu_ref, wsd_ref, lg_ref, lb_ref, out_ref = refs[TOP_K:]
    gu = jnp.dot(_unpack_bf16(xm_ref[...]), wsgu_ref[...], preferred_element_type=F32)
    act = _silu(gu[:, :D_EXPERT]) * gu[:, D_EXPERT:]
    acc = jnp.dot(act.astype(BF16), wsd_ref[...], preferred_element_type=F32)
    gw = gw_ref[...]
    for j in range(TOP_K):
        acc = acc + gw[:, j:j + 1] * _unpack_rows(yg_refs[j][...])
    z = DEEPNORM_ALPHA * x1_ref[...] + g2_ref[0] * acc
    out_ref[...] = _layer_norm(z) * lg_ref[...] + lb_ref[...]


def _combine(yg, gw, x1, xm, gate2, w_sgu, w_sd, ln_g, ln_b, n_per_batch):
    t, d = x1.shape
    tm = TM_COMBINE
    per = n_per_batch // tm
    steps = t // tm
    tok = lambda w: pl.BlockSpec((tm, w), lambda i: (i, 0))
    full = lambda r, c: pl.BlockSpec((r, c), lambda i: (0, 0))
    slot = lambda j: pl.BlockSpec((tm, PACK_W), lambda i: (j * steps + i, 0))
    return pl.pallas_call(
        _combine_kernel,
        out_shape=jax.ShapeDtypeStruct((t, d), F32),
        grid=(steps,),
        in_specs=[slot(j) for j in range(TOP_K)]
        + [tok(TOP_K), tok(d), tok(PACK_W), pl.BlockSpec((1, 1, d), lambda i: (i // per, 0, 0)),
           full(d, 2 * D_EXPERT), full(D_EXPERT, d), full(1, d), full(1, d)],
        out_specs=tok(d),
        compiler_params=_cparams(("arbitrary",)),
        name="combine",
    )(*([yg] * TOP_K), gw, x1, xm, gate2, w_sgu, w_sd, ln_g, ln_b)


def _mixer_heads(stream, shift, scale, w_in_k, conv_w9, conv_b, gate_b_row, head_g, c0, m0, grid2d):
    b, n, d = stream.shape
    y, f_in = _inproj(stream.reshape(b * n, d), shift, scale, w_in_k, n)
    y = y.reshape(b, n, PROJ_W)
    qk = _conv(y, conv_w9, conv_b, grid2d)
    mread, c_fin, m_fin = _mlstm(qk, y, _gates(y, gate_b_row), c0, m0, head_g)
    return f_in, mread, c_fin, m_fin


def _moe(xm, x1, scores, bias_row, gate2, w_gu, w_down, w_sgu, w_sd, ln_g, ln_b, n_per_batch):
    t = xm.shape[0]
    n_assign = t * TOP_K
    n_rows = n_assign + N_EXPERTS * ROW_BLK
    idx_t, gw, rank_t, counts = _route(scores, bias_row)
    sizes = counts[:, 0].astype(jnp.int32)
    padded = (sizes + ROW_BLK - 1) // ROW_BLK * ROW_BLK
    pad_end = jnp.cumsum(padded)
    pad_start = (pad_end - padded).astype(jnp.int32)
    dest_t = _dest(pad_start, idx_t, rank_t)
    dest3 = dest_t.reshape(TOP_K, t // SC_ROWS, SC_ROWS).transpose(1, 0, 2)

    xs = _sc_dispatch(xm, dest3, n_rows)
    yb = _experts(pad_start // ROW_BLK, (padded // ROW_BLK).astype(jnp.int32),
                  (pad_end[-1:] // ROW_BLK).astype(jnp.int32), xs, w_gu, w_down)
    yg = _sc_gather(yb, dest_t.reshape(n_assign // SC_GATHER_ROWS, SC_GATHER_ROWS))
    return _combine(yg, gw, x1, xm, gate2, w_sgu, w_sd, ln_g, ln_b, n_per_batch)


def kernel(x, c, ctx, c_ctx, w_ada, b_ada, w_in, conv_w, conv_b, gate_b, head_g, w_out, ln1_g, ln1_b,
           w_router, router_bias, w_expert_gu, w_expert_down, w_shared_gu, w_shared_down, ln2_g, ln2_b):
    bsz, n, d = x.shape
    depth = w_ada.shape[0]
    assert depth == 1
    t = bsz * n
    tables = _dft_tables()
    for l in range(depth):
        cvec = jnp.zeros((SUBLANES, d), F32).at[:bsz].set(c).at[bsz].set(c_ctx)
        mod = _ada(cvec, w_ada[l], b_ada[l][None, :])
        mods = mod.reshape(SUBLANES, 6, 1, d)
        mod_x = [mods[:bsz, j] for j in range(6)]
        mod_c = [mods[bsz:bsz + 1, j] for j in range(2)]

        wl = w_in[l]
        g0 = 4 * D_MLSTM
        w_in_k = jnp.concatenate(
            [wl[:, :g0 + N_GATE_COLS], jnp.zeros((d, LANES - N_GATE_COLS), F32),
             wl[:, g0 + N_GATE_COLS:]], axis=1).astype(BF16)
        conv_w9 = conv_w[l].reshape(9, 2 * D_MLSTM)
        conv_b_row = conv_b[l][None, :]
        gate_b_row = jnp.zeros((1, LANES), F32).at[0, :N_GATE_COLS].set(gate_b[l].reshape(-1))
        head_g_row = head_g[l][None, :]

        c0 = jnp.zeros((bsz, N_HEADS, 2, STATE_ROWS, D_HEAD), F32)
        m0 = jnp.zeros((bsz, N_HEADS, 2, SUBLANES, LANES), F32)
        _, _, c_ctx_fin, m_ctx_fin = _mixer_heads(ctx, mod_c[0], mod_c[1], w_in_k, conv_w9, conv_b_row,
                                                  gate_b_row, head_g_row, c0, m0, grid2d=False)
        f_in, mread, _, _ = _mixer_heads(x, mod_x[0], mod_x[1], w_in_k, conv_w9, conv_b_row,
                                      gate_b_row, head_g_row, c_ctx_fin, m_ctx_fin, grid2d=True)

        four = _fourier(f_in.reshape(bsz, DFT_R, DFT_R, D_FOURIER), tables).reshape(t, D_FOURIER)

        w_o = w_out[l].astype(BF16)
        w_r = jnp.zeros((d, LANES), F32).at[:, :N_EXPERTS].set(w_router[l]).astype(BF16)
        x2d = x.reshape(t, d)
        x1, xm, scores = _outproj(x2d, mread.reshape(t, D_MLSTM), four, w_o[:D_MLSTM], w_o[D_MLSTM:],
                                  mod_x[2], ln1_g[l][None, :], ln1_b[l][None, :], mod_x[3], mod_x[4],
                                  w_r, n)

        bias_rep = jnp.broadcast_to(router_bias[l][:, None], (N_EXPERTS, LANES))
        x = _moe(xm, x1, scores, bias_rep, mod_x[5], w_expert_gu[l], w_expert_down[l],
                 w_shared_gu[l].astype(BF16), w_shared_down[l].astype(BF16),
                 ln2_g[l][None, :], ln2_b[l][None, :], n).reshape(bsz, n, d)
    return x
```

```python
import functools

import numpy as np
import jax
import jax.numpy as jnp
from jax import lax
from jax.experimental import pallas as pl
from jax.experimental.pallas import tpu as pltpu
from jax.experimental.pallas import tpu_sc as plsc

F32 = jnp.float32
BF16 = jnp.bfloat16
HIGHEST = lax.Precision.HIGHEST

D_MODEL = 1024
GRID_W = 64
N_HEADS = 4
D_HEAD = 128
D_MLSTM = N_HEADS * D_HEAD
D_FOURIER = 512
FOURIER_GROUP = 64
N_GATE_COLS = 16
CHUNK = 128
N_EXPERTS = 64
TOP_K = 8
D_EXPERT = 256
ROUTED_SCALE = 2.5
DEEPNORM_ALPHA = 2.0 ** 0.25
LN_EPS = 1e-5

LANES = 128
SUBLANES = 8
VMEM_LIMIT = 48 * 1024 * 1024
FOUR_VMEM_LIMIT = 56 * 1024 * 1024

COL_QK = 0
COL_V = 2 * D_MLSTM
COL_O = 3 * D_MLSTM
COL_G = 4 * D_MLSTM
PROJ_W = COL_G + LANES

TN_ADA = 2048
TM = 1024
TM_ROUTE = 1024
ROW_BLK = 512
ROW_HALF = 256
EXP_AHEAD = 5
EXP_IN_SLOTS = 8
EXP_OUT_SLOTS = 4
PACK_W = D_MODEL // 2
SC_ROWS = 128
SC_GATHER_ROWS = 64
CONV_PAD = 72
CONV_RB = 256
DFT_R = 64


def _cparams(sem, vmem=VMEM_LIMIT):
    return pltpu.CompilerParams(dimension_semantics=sem, vmem_limit_bytes=vmem)


def _layer_norm(x):
    mu = jnp.mean(x, axis=-1, keepdims=True)
    xc = x - mu
    var = jnp.mean(xc * xc, axis=-1, keepdims=True)
    return xc * lax.rsqrt(var + LN_EPS)


def _silu(x):
    return x * jax.nn.sigmoid(x)


def _pack_pair(lo, hi):
    as_bits = lambda u: lax.bitcast_convert_type(u.astype(BF16).astype(F32), jnp.uint32)
    word = as_bits(hi) | lax.shift_right_logical(as_bits(lo), jnp.uint32(16))
    return lax.bitcast_convert_type(word, jnp.int32)


def _unpack_pair_bf16(w):
    lo = lax.bitcast_convert_type(w.astype(jnp.int16), BF16)
    hi = lax.bitcast_convert_type(lax.shift_right_logical(w, jnp.int32(16)).astype(jnp.int16), BF16)
    return lo, hi


def _pack_rows(v):
    return _pack_pair(v[:, :PACK_W], v[:, PACK_W:])


def _unpack_bf16(w):
    return jnp.concatenate(_unpack_pair_bf16(w), axis=-1)


def _unpack_rows(w):
    u = lax.bitcast_convert_type(w, jnp.uint32)
    lo = lax.bitcast_convert_type(lax.shift_left(u, jnp.uint32(16)), F32)
    hi = lax.bitcast_convert_type(u & jnp.uint32(0xFFFF0000), F32)
    return jnp.concatenate([lo, hi], axis=1)


def _ada_kernel(c_ref, w_ref, b_ref, o_ref):
    s = _silu(c_ref[...]).astype(BF16)
    o_ref[...] = jnp.dot(s, w_ref[...].astype(BF16), preferred_element_type=F32) + b_ref[...]


def _ada(cvec, w_ada, b_ada):
    rows, d = cvec.shape
    n_out = w_ada.shape[1]
    tn = TN_ADA
    return pl.pallas_call(
        _ada_kernel,
        out_shape=jax.ShapeDtypeStruct((rows, n_out), F32),
        grid=(n_out // tn,),
        in_specs=[pl.BlockSpec((rows, d), lambda j: (0, 0)),
                  pl.BlockSpec((d, tn), lambda j: (0, j)),
                  pl.BlockSpec((1, tn), lambda j: (0, j))],
        out_specs=pl.BlockSpec((rows, tn), lambda j: (0, j)),
        compiler_params=_cparams(("arbitrary",)),
        name="ada",
    )(cvec, w_ada, b_ada)


def _inproj_kernel(x_ref, sh_ref, sc_ref, w_ref, y_ref, f_ref):
    half = x_ref.shape[0] // 2
    for h in range(2):
        r = slice(h * half, (h + 1) * half)
        u = _layer_norm(x_ref[r, :]) * (1.0 + sc_ref[0]) + sh_ref[0]
        proj = jnp.dot(u.astype(BF16), w_ref[...], preferred_element_type=F32)
        y_ref[r, :] = proj[:, :PROJ_W]
        f_ref[r, :] = proj[:, PROJ_W:]


def _inproj(x2d, shift, scale, w, n_per_batch):
    t, d = x2d.shape
    if shift.shape[0] == 1:
        tm = min(TM, t)
        per = t // tm
    else:
        tm = min(TM, n_per_batch)
        per = n_per_batch // tm
    return pl.pallas_call(
        _inproj_kernel,
        out_shape=(jax.ShapeDtypeStruct((t, PROJ_W), F32), jax.ShapeDtypeStruct((t, D_FOURIER), F32)),
        grid=(t // tm,),
        in_specs=[pl.BlockSpec((tm, d), lambda i: (i, 0)),
                  pl.BlockSpec((1, 1, d), lambda i: (i // per, 0, 0)),
                  pl.BlockSpec((1, 1, d), lambda i: (i // per, 0, 0)),
                  pl.BlockSpec((d, PROJ_W + D_FOURIER), lambda i: (0, 0))],
        out_specs=(pl.BlockSpec((tm, PROJ_W), lambda i: (i, 0)),
                   pl.BlockSpec((tm, D_FOURIER), lambda i: (i, 0))),
        compiler_params=_cparams(("arbitrary",)),
        name="inproj",
    )(x2d, shift, scale, w)


def _conv_kernel(y_ref, w_ref, b_ref, o_ref, pad_ref, *, n, grid2d, tiles):
    zeros = jnp.zeros((CONV_PAD, LANES), F32)
    pad_ref[0:CONV_PAD, :] = zeros
    pad_ref[CONV_PAD + n:CONV_PAD + n + CONV_PAD, :] = zeros
    rb = min(CONV_RB, n)
    col = lax.broadcasted_iota(jnp.int32, (rb, LANES), 0) % GRID_W
    not_first = col >= 1
    not_last = col <= GRID_W - 2
    row_taps = (0, 1, 2) if grid2d else (1,)
    for ct in range(tiles):
        lanes = slice(ct * LANES, (ct + 1) * LANES)
        pad_ref[CONV_PAD:CONV_PAD + n, :] = y_ref[0, :, lanes]
        k_scale = jnp.where(pl.program_id(1) * tiles + ct >= N_HEADS, D_HEAD ** -0.5, 1.0).astype(F32)
        bias = b_ref[:, lanes]
        for blk in range(n // rb):
            r0 = CONV_PAD + blk * rb
            acc = jnp.zeros((rb, LANES), F32)
            for dj in range(3):
                part = jnp.zeros((rb, LANES), F32)
                for di in row_taps:
                    off = (di - 1) * GRID_W + (dj - 1)
                    tap = w_ref[di * 3 + dj:di * 3 + dj + 1, lanes]
                    part = part + tap * pad_ref[r0 + off:r0 + off + rb, :]
                if grid2d and dj == 0:
                    part = jnp.where(not_first, part, 0.0)
                if grid2d and dj == 2:
                    part = jnp.where(not_last, part, 0.0)
                acc = acc + part
            val = _silu(acc + bias) * k_scale
            o_ref[0, blk * rb:(blk + 1) * rb, lanes] = val.astype(o_ref.dtype)


def _conv(y3d, conv_w9, conv_b, grid2d):
    b, n, _ = y3d.shape
    n_ct = 2 * D_MLSTM // LANES
    tiles = 1 if grid2d else n_ct
    w = tiles * LANES
    return pl.pallas_call(
        functools.partial(_conv_kernel, n=n, grid2d=grid2d, tiles=tiles),
        out_shape=jax.ShapeDtypeStruct((b, n, 2 * D_MLSTM), BF16),
        grid=(b, n_ct // tiles),
        in_specs=[pl.BlockSpec((1, n, w), lambda i, c: (i, 0, c)),
                  pl.BlockSpec((9, w), lambda i, c: (0, c)),
                  pl.BlockSpec((1, w), lambda i, c: (0, c))],
        out_specs=pl.BlockSpec((1, n, w), lambda i, c: (i, 0, c)),
        scratch_shapes=[pltpu.VMEM((n + 2 * CONV_PAD, LANES), F32)],
        compiler_params=_cparams(("arbitrary", "arbitrary")),
        name="conv2d" if grid2d else "conv1d",
    )(y3d, conv_w9, conv_b)


STATE_ROWS = 2 * D_HEAD
GATE_A, GATE_CM, GATE_B = 0, 1, 2
GATE_GROUP = 32


def _gates_kernel(g_ref, gb_ref, rv_ref, *, nc):
    u_i = lax.broadcasted_iota(jnp.int32, (CHUNK, CHUNK), 0)
    t_i = lax.broadcasted_iota(jnp.int32, (CHUNK, CHUNK), 1)
    prefix = (u_i <= t_i).astype(F32)
    suffix = (u_i >= t_i).astype(F32)
    grp = min(GATE_GROUP, nc)
    rows = grp * SUBLANES
    lane = lax.broadcasted_iota(jnp.int32, (rows, LANES), 1)
    is_fwd = lax.broadcasted_iota(jnp.int32, (rows, LANES), 0) % SUBLANES < N_HEADS
    h = N_HEADS

    def running_max(a):
        fwd, bwd = a, a
        sh = 1
        while sh < CHUNK:
            fwd = jnp.where(lane >= sh, jnp.maximum(fwd, pltpu.roll(fwd, sh, axis=1)), fwd)
            bwd = jnp.where(lane < CHUNK - sh, jnp.maximum(bwd, pltpu.roll(bwd, CHUNK - sh, axis=1)), bwd)
            sh *= 2
        return jnp.where(is_fwd, fwd, bwd)

    def body(i, carry):
        c0 = i * grp
        r0 = pl.multiple_of(c0 * CHUNK, CHUNK)
        pre = g_ref[0, pl.ds(r0, grp * CHUNK), :] + gb_ref[...]
        pre_t = jnp.concatenate(
            [pre[j * CHUNK:(j + 1) * CHUNK].T[:N_GATE_COLS, :] for j in range(grp)], axis=0)
        lf_t = -(jnp.maximum(-pre_t, 0.0) + jnp.log1p(jnp.exp(-jnp.abs(pre_t))))
        cum_f = jnp.dot(lf_t, prefix, precision=HIGHEST, preferred_element_type=F32)
        cum_b = jnp.dot(lf_t, suffix, precision=HIGHEST, preferred_element_type=F32)
        pick = lambda t, j, q: t[j * N_GATE_COLS + q * h:j * N_GATE_COLS + (q + 1) * h]
        b8 = jnp.concatenate([x for j in range(grp) for x in (pick(cum_f, j, 1), pick(cum_b, j, 3))], axis=0)
        li8 = jnp.concatenate([x for j in range(grp) for x in (pick(pre_t, j, 0), pick(pre_t, j, 2))], axis=0)
        a8 = li8 - b8
        as_chunks = lambda t: t.reshape(grp, SUBLANES, LANES)
        rv_ref[0, pl.ds(c0, grp), GATE_A] = as_chunks(a8)
        rv_ref[0, pl.ds(c0, grp), GATE_CM] = as_chunks(running_max(a8))
        rv_ref[0, pl.ds(c0, grp), GATE_B] = as_chunks(b8)
        return carry

    lax.fori_loop(0, nc // grp, body, 0)


def _gates(y3d, gate_b_row):
    b, n, _ = y3d.shape
    nc = n // CHUNK
    return pl.pallas_call(
        functools.partial(_gates_kernel, nc=nc),
        out_shape=jax.ShapeDtypeStruct((b, nc, 3, SUBLANES, LANES), F32),
        grid=(b,),
        in_specs=[pl.BlockSpec((1, n, LANES), lambda i: (i, 0, COL_G // LANES)),
                  pl.BlockSpec((1, LANES), lambda i: (0, 0))],
        out_specs=pl.BlockSpec((1, nc, 3, SUBLANES, LANES), lambda i: (i, 0, 0, 0, 0)),
        compiler_params=_cparams(("arbitrary",)),
        name="gates",
    )(y3d, gate_b_row)


def _mlstm_kernel(q_ref, k_ref, v_ref, o_ref, rv_ref, c0_ref, m0_ref, hg_ref,
                  out_ref, cfin_ref, mfin_ref, hf_ref, hb_ref, cst_ref, mst_ref, *, nc, heads):
    cst_ref[...] = c0_ref[0]
    mst_ref[...] = m0_ref[0]
    s_i = lax.broadcasted_iota(jnp.int32, (CHUNK, CHUNK), 0)
    t_i = lax.broadcasted_iota(jnp.int32, (CHUNK, CHUNK), 1)
    ones_row = (s_i[:STATE_ROWS - D_HEAD] == 0).astype(F32)

    def one_chunk(c, d, mask_t, last, h_ref, hh):
        r0 = pl.multiple_of(c * CHUNK, CHUNK)
        lanes = slice(hh * D_HEAD, (hh + 1) * D_HEAD)
        g = d * N_HEADS + pl.program_id(1) * heads + hh
        gate_row = lambda tbl: rv_ref[0, pl.ds(c, 1), tbl, pl.ds(g, 1), :].reshape(1, LANES)
        a_row, cm_row, b_row = gate_row(GATE_A), gate_row(GATE_CM), gate_row(GATE_B)
        q_t = q_ref[0, pl.ds(r0, CHUNK), lanes].T
        k = k_ref[0, pl.ds(r0, CHUNK), lanes]
        v_t_aug = jnp.concatenate([v_ref[0, pl.ds(r0, CHUNK), lanes].T, ones_row], axis=0)
        m_prev = mst_ref[hh, d, 0:1, :]
        m_row = jnp.maximum(m_prev, cm_row)
        a_bc = jnp.broadcast_to(a_row, (CHUNK, CHUNK)).T
        d_t = jnp.where(mask_t, jnp.exp(a_bc - m_row), 0.0)
        s_t = jnp.dot(k, q_t, preferred_element_type=F32) * d_t
        cs = cst_ref[hh, d]
        rhs = jnp.concatenate(
            [s_t.astype(BF16), (q_t.astype(F32) * jnp.exp(m_prev - m_row)).astype(BF16)], axis=0)
        lhs = jnp.concatenate([v_t_aug.astype(BF16), cs.astype(BF16)], axis=1)
        h_t = jnp.dot(lhs, rhs, preferred_element_type=F32)
        den = h_t[D_HEAD:D_HEAD + 1, :]
        floor = jnp.exp(-b_row - m_row)
        h_ref[hh, pl.ds(c, 1)] = (h_t[:D_HEAD] / jnp.maximum(jnp.abs(den), floor))[None]
        m_last = m_row[:, last:last + 1]
        wv_t = (v_t_aug * jnp.exp(a_row - m_last)).astype(BF16)
        cst_ref[hh, d] = jnp.exp(m_prev - m_last) * cs + jnp.dot(wv_t, k, preferred_element_type=F32)
        mst_ref[hh, d] = jnp.broadcast_to(b_row[:, last:last + 1] + m_last, (SUBLANES, LANES))

    def body(c, carry):
        for hh in range(heads):
            one_chunk(c, 0, s_i <= t_i, CHUNK - 1, hf_ref, hh)
            one_chunk(nc - 1 - c, 1, s_i >= t_i, 0, hb_ref, hh)
        return carry

    lax.fori_loop(0, nc, body, 0, unroll=min(16 // heads, nc))
    cfin_ref[0] = cst_ref[...]
    mfin_ref[0] = mst_ref[...]

    def readout(c, carry):
        r0 = pl.multiple_of(c * CHUNK, CHUNK)
        for hh in range(heads):
            lanes = slice(hh * D_HEAD, (hh + 1) * D_HEAD)
            h_t = hf_ref[hh, pl.ds(c, 1)][0] + hb_ref[hh, pl.ds(c, 1)][0]
            hc = h_t - jnp.mean(h_t, axis=0, keepdims=True)
            hn = (hc * lax.rsqrt(jnp.mean(hc * hc, axis=0, keepdims=True) + LN_EPS)).T
            gate = jax.nn.sigmoid(o_ref[0, pl.ds(r0, CHUNK), lanes])
            out_ref[0, pl.ds(r0, CHUNK), lanes] = (hn * hg_ref[:, lanes] * gate).astype(out_ref.dtype)
        return carry

    lax.fori_loop(0, nc, readout, 0, unroll=min(16, nc))


def _mlstm(qk, y3d, rowv, c0, m0, head_g):
    b, n, _ = qk.shape
    nc = n // CHUNK
    heads = N_HEADS if nc <= 4 else 1
    w = heads * D_HEAD
    v_blk = COL_V // w
    o_blk = COL_O // w
    tok = lambda off: pl.BlockSpec((1, n, w), lambda i, h: (i, 0, off + h))
    state_c = pl.BlockSpec((1, heads, 2, STATE_ROWS, D_HEAD), lambda i, h: (i, h, 0, 0, 0))
    state_m = pl.BlockSpec((1, heads, 2, SUBLANES, LANES), lambda i, h: (i, h, 0, 0, 0))
    return pl.pallas_call(
        functools.partial(_mlstm_kernel, nc=nc, heads=heads),
        out_shape=(jax.ShapeDtypeStruct((b, n, D_MLSTM), BF16),
                   jax.ShapeDtypeStruct((b, N_HEADS, 2, STATE_ROWS, D_HEAD), F32),
                   jax.ShapeDtypeStruct((b, N_HEADS, 2, SUBLANES, LANES), F32)),
        grid=(b, N_HEADS // heads),
        in_specs=[tok(0), tok(D_MLSTM // w), tok(v_blk), tok(o_blk),
                  pl.BlockSpec((1, nc, 3, SUBLANES, LANES), lambda i, h: (i, 0, 0, 0, 0)),
                  state_c, state_m,
                  pl.BlockSpec((1, w), lambda i, h: (0, h))],
        out_specs=(pl.BlockSpec((1, n, w), lambda i, h: (i, 0, h)), state_c, state_m),
        scratch_shapes=[pltpu.VMEM((heads, nc, D_HEAD, CHUNK), F32), pltpu.VMEM((heads, nc, D_HEAD, CHUNK), F32),
                        pltpu.VMEM((heads, 2, STATE_ROWS, D_HEAD), F32),
                        pltpu.VMEM((heads, 2, SUBLANES, LANES), F32)],
        compiler_params=_cparams(("arbitrary", "arbitrary")),
        name="mlstm",
    )(qk, qk, y3d, y3d, rowv, c0, m0, head_g)


def _dft_tables():
    r = DFT_R
    idx = np.arange(r)
    cg = np.cos(2 * np.pi * np.outer(idx, idx) / FOURIER_GROUP)
    sg = np.sin(2 * np.pi * np.outer(idx, idx) / FOURIER_GROUP)
    n_grp = D_FOURIER // FOURIER_GROUP
    bdc = np.kron(np.eye(n_grp), cg) / 8.0
    bds = np.kron(np.eye(n_grp), sg) / 8.0
    k1 = idx[:, None]
    a = idx[None, :]
    m1 = np.zeros((r, 2 * r, 2 * r))
    for b2 in range(r):
        th = 2 * np.pi * k1 * (r * a + b2) / (r * r)
        ec, es = np.cos(th), np.sin(th)
        m1[b2] = np.block([[ec, -es], [es, ec]]) / 8.0
    th2 = 2 * np.pi * np.outer(idx, idx) / r
    m2 = np.concatenate([np.cos(th2), -np.sin(th2)], axis=1) / 8.0
    as_bf16 = lambda t: jnp.asarray(t, dtype=F32).astype(BF16)
    return as_bf16(bdc), as_bf16(bds), as_bf16(m1), as_bf16(m2)


FOUR_BB = 8


N_SLABS = D_FOURIER // LANES


def _to_slabs(slab_ref, val):
    for s in range(N_SLABS):
        slab_ref[s] = val[:, s * LANES:(s + 1) * LANES]


def _strided_rows(slab_ref, start):
    return jnp.concatenate(
        [slab_ref[s, pl.ds(start, DFT_R, stride=FOUR_BB), :] for s in range(N_SLABS)], axis=1)


def _four_kernel(f_ref, bdc_ref, bds_ref, m1_ref, m2_ref, o_ref, z_ref, ps_ref, qs_ref, zs_ref, os_ref):
    r = DFT_R

    def stage1(blk, carry):
        b0 = pl.multiple_of(blk * FOUR_BB, FOUR_BB)
        fb = f_ref[0, :, pl.ds(b0, FOUR_BB), :].reshape(r * FOUR_BB, D_FOURIER).astype(BF16)
        _to_slabs(ps_ref, jnp.dot(fb, bdc_ref[...], preferred_element_type=F32))
        _to_slabs(qs_ref, jnp.dot(fb, bds_ref[...], preferred_element_type=F32))
        for j in range(FOUR_BB):
            x2 = jnp.concatenate([_strided_rows(ps_ref, j), _strided_rows(qs_ref, j)], axis=0).astype(BF16)
            zz = jnp.dot(m1_ref[b0 + j], x2, preferred_element_type=F32)
            z_ref[b0 + j] = _pack_pair(zz[:r], zz[r:])
        return carry

    lax.fori_loop(0, r // FOUR_BB, stage1, 0)

    def stage2(blk, carry):
        k0 = pl.multiple_of(blk * FOUR_BB, FOUR_BB)
        _to_slabs(zs_ref, z_ref[:, pl.ds(k0, FOUR_BB), :].reshape(r * FOUR_BB, D_FOURIER))
        for j in range(FOUR_BB):
            zc, zs = _unpack_pair_bf16(_strided_rows(zs_ref, j))
            o = jnp.dot(m2_ref[...], jnp.concatenate([zc, zs], axis=0), preferred_element_type=F32)
            for s in range(N_SLABS):
                os_ref[s, pl.ds(j, r, stride=FOUR_BB), :] = o[:, s * LANES:(s + 1) * LANES]
        for s in range(N_SLABS):
            o_ref[0, :, pl.ds(k0, FOUR_BB), s * LANES:(s + 1) * LANES] = os_ref[s].reshape(r, FOUR_BB, LANES)
        return carry

    lax.fori_loop(0, r // FOUR_BB, stage2, 0)


def _fourier(y4, tables):
    bdc, bds, m1, m2 = tables
    b = y4.shape[0]
    r = DFT_R
    slab = lambda dt: pltpu.VMEM((N_SLABS, r * FOUR_BB, LANES), dt)
    const = lambda *shape: pl.BlockSpec(shape, lambda i: (0,) * len(shape))
    return pl.pallas_call(
        _four_kernel,
        out_shape=jax.ShapeDtypeStruct((b, r, r, D_FOURIER), F32),
        grid=(b,),
        in_specs=[pl.BlockSpec((1, r, r, D_FOURIER), lambda i: (i, 0, 0, 0)),
                  const(D_FOURIER, D_FOURIER), const(D_FOURIER, D_FOURIER),
                  const(r, 2 * r, 2 * r), const(r, 2 * r)],
        out_specs=pl.BlockSpec((1, r, r, D_FOURIER), lambda i: (i, 0, 0, 0)),
        scratch_shapes=[pltpu.VMEM((r, r, D_FOURIER), jnp.int32), slab(F32), slab(F32), slab(jnp.int32), slab(F32)],
        compiler_params=_cparams(("arbitrary",), vmem=FOUR_VMEM_LIMIT),
        name="fourier",
    )(y4, bdc, bds, m1, m2)


def _outproj_kernel(x_ref, mr_ref, fo_ref, wm_ref, wf_ref, g1_ref, lg_ref, lb_ref,
                    sh_ref, sc_ref, wr_ref, x1_ref, xm_ref, s_ref):
    half = x_ref.shape[0] // 2
    for h in range(2):
        r = slice(h * half, (h + 1) * half)
        mix = (jnp.dot(mr_ref[r, :], wm_ref[...], preferred_element_type=F32)
               + jnp.dot(fo_ref[r, :].astype(BF16), wf_ref[...], preferred_element_type=F32))
        x1 = _layer_norm(DEEPNORM_ALPHA * x_ref[r, :] + g1_ref[0] * mix) * lg_ref[...] + lb_ref[...]
        x1_ref[r, :] = x1
        xm = (_layer_norm(x1) * (1.0 + sc_ref[0]) + sh_ref[0]).astype(BF16)
        xm_ref[r, :] = _pack_rows(xm)
        logits = jnp.dot(xm, wr_ref[...], preferred_element_type=F32)
        s_ref[r, :] = jax.nn.sigmoid(logits)


def _outproj(x2d, mread, four, w_m, w_f, gate1, ln_g, ln_b, shift2, scale2, w_router, n_per_batch):
    t, d = x2d.shape
    per = n_per_batch // TM
    tokd = lambda w: pl.BlockSpec((TM, w), lambda i: (i, 0))
    full = lambda r, c: pl.BlockSpec((r, c), lambda i: (0, 0))
    mod = pl.BlockSpec((1, 1, d), lambda i: (i // per, 0, 0))
    return pl.pallas_call(
        _outproj_kernel,
        out_shape=(jax.ShapeDtypeStruct((t, d), F32),
                   jax.ShapeDtypeStruct((t, PACK_W), jnp.int32),
                   jax.ShapeDtypeStruct((t, LANES), F32)),
        grid=(t // TM,),
        in_specs=[tokd(d), tokd(D_MLSTM), tokd(D_FOURIER), full(D_MLSTM, d), full(D_FOURIER, d),
                  mod, full(1, d), full(1, d), mod, mod, full(d, LANES)],
        out_specs=(tokd(d), tokd(PACK_W), tokd(LANES)),
        compiler_params=_cparams(("arbitrary",)),
        name="outproj",
    )(x2d, mread, four, w_m, w_f, gate1, ln_g, ln_b, shift2, scale2, w_router)


def _route_kernel(s_ref, b_ref, idx_ref, gw_ref, rank_ref, cnt_ref, carry_ref):
    @pl.when(pl.program_id(0) == 0)
    def _():
        carry_ref[...] = jnp.zeros_like(carry_ref)

    tm = s_ref.shape[0]
    reps = tm // LANES
    s_t = s_ref[...].T[:N_EXPERTS, :]
    e_i = lax.broadcasted_iota(jnp.int32, (N_EXPERTS, tm), 0).astype(F32)
    neg_inf = jnp.float32(-jnp.inf)
    sb = s_t + jnp.tile(b_ref[...], (1, reps))
    picks, sels = [], []
    taken = jnp.zeros((N_EXPERTS, tm), F32)
    for _ in range(TOP_K):
        best = jnp.max(sb, axis=0, keepdims=True)
        pick = jnp.min(jnp.where(sb == best, e_i, float(N_EXPERTS)), axis=0, keepdims=True)
        onehot = e_i == pick
        sels.append(jnp.sum(jnp.where(onehot, s_t, 0.0), axis=0, keepdims=True))
        picks.append(pick)
        sb = jnp.where(onehot, neg_inf, sb)
        taken = taken + onehot.astype(F32)
    u_i = lax.broadcasted_iota(jnp.int32, (tm, tm), 0)
    t_i = lax.broadcasted_iota(jnp.int32, (tm, tm), 1)
    before = (u_i < t_i).astype(BF16)
    taken_bf = taken.astype(BF16)
    rank_all = jnp.tile(carry_ref[...], (1, reps)) + jnp.dot(taken_bf, before, preferred_element_type=F32)
    carry_ref[...] = carry_ref[...] + jnp.dot(taken_bf, jnp.ones((tm, LANES), BF16),
                                              preferred_element_type=F32)
    cnt_ref[...] = carry_ref[...]
    total = sels[0]
    for v in sels[1:]:
        total = total + v
    ranks = [jnp.sum(jnp.where(e_i == picks[j], rank_all, 0.0), axis=0, keepdims=True) for j in range(TOP_K)]
    idx_ref[...] = jnp.concatenate(picks, axis=0).astype(jnp.int32)
    rank_ref[...] = jnp.concatenate(ranks, axis=0).astype(jnp.int32)
    gw_t = jnp.concatenate([ROUTED_SCALE * v / total for v in sels]
                           + [jnp.zeros((LANES - TOP_K, tm), F32)], axis=0)
    gw_ref[...] = gw_t.T[:, :TOP_K]


def _route(scores, bias_rep):
    t = scores.shape[0]
    tok = lambda w: pl.BlockSpec((TM_ROUTE, w), lambda i: (i, 0))
    slot_major = pl.BlockSpec((TOP_K, TM_ROUTE), lambda i: (0, i))
    per_expert = pl.BlockSpec((N_EXPERTS, LANES), lambda i: (0, 0))
    return pl.pallas_call(
        _route_kernel,
        out_shape=(jax.ShapeDtypeStruct((TOP_K, t), jnp.int32),
                   jax.ShapeDtypeStruct((t, TOP_K), F32),
                   jax.ShapeDtypeStruct((TOP_K, t), jnp.int32),
                   jax.ShapeDtypeStruct((N_EXPERTS, LANES), F32)),
        grid=(t // TM_ROUTE,),
        in_specs=[tok(LANES), per_expert],
        out_specs=(slot_major, tok(TOP_K), slot_major, per_expert),
        scratch_shapes=[pltpu.VMEM((N_EXPERTS, LANES), F32)],
        compiler_params=_cparams(("arbitrary",)),
        name="route",
    )(scores, bias_rep)


TN_DEST = 2048


def _dest_kernel(ps_ref, idx_ref, rank_ref, dest_ref):
    idx = idx_ref[...]
    start = jnp.zeros(idx.shape, jnp.int32)
    for e in range(N_EXPERTS):
        start = jnp.where(idx == e, ps_ref[e], start)
    dest_ref[...] = start + rank_ref[...]


def _dest(pad_start, idx_t, rank_t):
    t = idx_t.shape[1]
    blk = lambda: pl.BlockSpec((TOP_K, TN_DEST), lambda i, ps: (0, i))
    grid_spec = pltpu.PrefetchScalarGridSpec(
        num_scalar_prefetch=1, grid=(t // TN_DEST,), in_specs=[blk(), blk()], out_specs=blk())
    return pl.pallas_call(
        _dest_kernel,
        out_shape=jax.ShapeDtypeStruct((TOP_K, t), jnp.int32),
        grid_spec=grid_spec,
        compiler_params=_cparams(("arbitrary",)),
        name="dest",
    )(pad_start, idx_t, rank_t)


def _sc_worker_id(info):
    return lax.axis_index("s") * info.num_cores + lax.axis_index("c")


def _sc_dispatch(xm, dest3, n_rows):
    t, w = xm.shape
    info = plsc.get_sparse_core_info()
    n_workers = info.num_cores * info.num_subcores
    per_worker = t // SC_ROWS // n_workers
    mesh = plsc.VectorSubcoreMesh(core_axis_name="c", subcore_axis_name="s")

    @functools.partial(
        pl.kernel, mesh=mesh,
        out_type=jax.ShapeDtypeStruct((n_rows, w), xm.dtype),
        scratch_types=[pltpu.VMEM((TOP_K, SC_ROWS), jnp.int32), pltpu.VMEM((SC_ROWS, w), xm.dtype),
                       pltpu.SemaphoreType.DMA],
        name="sc_dispatch",
    )
    def run(xm_hbm, dest_hbm, xs_hbm, idx_v, rows_v, sem):
        first = _sc_worker_id(info) * per_worker

        @pl.loop(0, per_worker)
        def _(ci):
            chunk = first + ci
            pltpu.sync_copy(dest_hbm.at[chunk], idx_v)
            pltpu.sync_copy(xm_hbm.at[pl.ds(chunk * SC_ROWS, SC_ROWS)], rows_v)
            scatters = [pltpu.make_async_copy(rows_v, xs_hbm.at[idx_v.at[j]], sem) for j in range(TOP_K)]
            for s in scatters:
                s.start()
            for s in scatters:
                s.wait()

    return run(xm, dest3)


def _sc_gather(table, idx2):
    n_chunks, rows = idx2.shape
    w = table.shape[1]
    info = plsc.get_sparse_core_info()
    n_workers = info.num_cores * info.num_subcores
    per_worker = n_chunks // n_workers
    assert per_worker % 2 == 0
    mesh = plsc.VectorSubcoreMesh(core_axis_name="c", subcore_axis_name="s")

    @functools.partial(
        pl.kernel, mesh=mesh,
        out_type=jax.ShapeDtypeStruct((n_chunks * rows, w), table.dtype),
        scratch_types=[pltpu.VMEM((2, rows), jnp.int32), pltpu.VMEM((2, rows, w), table.dtype),
                       pltpu.SemaphoreType.DMA, pltpu.SemaphoreType.DMA],
        name="sc_gather",
    )
    def run(table_hbm, idx_hbm, out_hbm, idx_v, rows_v, sem0, sem1):
        first = _sc_worker_id(info) * per_worker
        sems = (sem0, sem1)

        def gather(b):
            return pltpu.make_async_copy(table_hbm.at[idx_v.at[b]], rows_v.at[b], sems[b])

        def start(chunk, b):
            pltpu.sync_copy(idx_hbm.at[chunk], idx_v.at[b])
            gather(b).start()

        def finish(chunk, b):
            gather(b).wait()
            pltpu.sync_copy(rows_v.at[b], out_hbm.at[pl.ds(chunk * rows, rows)])

        start(first, 0)

        @pl.loop(0, per_worker, step=2)
        def _(ci):
            chunk = first + ci
            start(chunk + 1, 1)
            finish(chunk, 0)

            @pl.when(ci + 2 < per_worker)
            def _():
                start(chunk + 2, 0)

            finish(chunk + 1, 1)

    return run(table, idx2)


def _expert_kernel(first_ref, count_ref, used_ref,
                   xs_hbm, wgu_ref, wd_ref, y_hbm, wgu_bf, wd_bf, xbuf, ybuf, in_sem, out_sem):
    e = pl.program_id(0)
    n_used = used_ref[0]
    first = first_ref[e]
    n_blocks = count_ref[e]

    def rows(g):
        return pl.ds(pl.multiple_of(g * ROW_BLK, ROW_BLK), ROW_BLK)

    def read(g):
        slot = g % EXP_IN_SLOTS
        return pltpu.make_async_copy(xs_hbm.at[rows(g), :], xbuf.at[slot], in_sem.at[slot])

    def write(g):
        slot = g % EXP_OUT_SLOTS
        return pltpu.make_async_copy(ybuf.at[slot], y_hbm.at[rows(g), :], out_sem.at[slot])

    @pl.when(e == 0)
    def _():
        for g0 in range(EXP_AHEAD):
            @pl.when(g0 < n_used)
            def _():
                read(g0).start()

    @pl.when(n_blocks > 0)
    def _():
        wgu_bf[...] = wgu_ref[0].astype(BF16)
        wd_bf[...] = wd_ref[0].astype(BF16)

    def acquire(g):
        @pl.when(g + EXP_AHEAD < n_used)
        def _():
            read(g + EXP_AHEAD).start()

        read(g).wait()

        @pl.when(g >= EXP_OUT_SLOTS)
        def _():
            write(g - EXP_OUT_SLOTS).wait()

    def compute(g):
        in_slot = g % EXP_IN_SLOTS
        out_slot = g % EXP_OUT_SLOTS
        for h in range(ROW_BLK // ROW_HALF):
            r0 = h * ROW_HALF
            x = _unpack_bf16(xbuf[in_slot, r0:r0 + ROW_HALF, :])
            gu = jnp.dot(x, wgu_bf[...], preferred_element_type=F32)
            act = _silu(gu[:, :D_EXPERT]) * gu[:, D_EXPERT:]
            yv = jnp.dot(act.astype(BF16), wd_bf[...], preferred_element_type=F32)
            words = pltpu.pack_elementwise([yv[:, :PACK_W], yv[:, PACK_W:]], packed_dtype=BF16)
            ybuf[out_slot, r0:r0 + ROW_HALF, :] = lax.bitcast_convert_type(words, jnp.int32)

    def pair(k, carry):
        g = first + 2 * k
        acquire(g)
        acquire(g + 1)
        compute(g)
        compute(g + 1)
        write(g).start()
        write(g + 1).start()
        return carry

    lax.fori_loop(0, n_blocks // 2, pair, 0)

    @pl.when(n_blocks % 2 == 1)
    def _():
        g = first + n_blocks - 1
        acquire(g)
        compute(g)
        write(g).start()

    @pl.when(e == pl.num_programs(0) - 1)
    def _():
        for back in range(EXP_OUT_SLOTS, 0, -1):
            @pl.when(n_used >= back)
            def _():
                write(n_used - back).wait()


def _experts(first_block, n_blocks, n_used, xs, w_gu, w_down):
    n_rows, w = xs.shape
    d = D_MODEL
    grid_spec = pltpu.PrefetchScalarGridSpec(
        num_scalar_prefetch=3,
        grid=(N_EXPERTS,),
        in_specs=[pl.BlockSpec(memory_space=pl.ANY),
                  pl.BlockSpec((1, d, 2 * D_EXPERT), lambda e, *_: (e, 0, 0)),
                  pl.BlockSpec((1, D_EXPERT, d), lambda e, *_: (e, 0, 0))],
        out_specs=pl.BlockSpec(memory_space=pl.ANY),
        scratch_shapes=[pltpu.VMEM((d, 2 * D_EXPERT), BF16), pltpu.VMEM((D_EXPERT, d), BF16),
                        pltpu.VMEM((EXP_IN_SLOTS, ROW_BLK, w), jnp.int32),
                        pltpu.VMEM((EXP_OUT_SLOTS, ROW_BLK, w), jnp.int32),
                        pltpu.SemaphoreType.DMA((EXP_IN_SLOTS,)),
                        pltpu.SemaphoreType.DMA((EXP_OUT_SLOTS,))],
    )
    return pl.pallas_call(
        _expert_kernel,
        out_shape=jax.ShapeDtypeStruct((n_rows, w), jnp.int32),
        grid_spec=grid_spec,
        compiler_params=_cparams(("arbitrary",)),
        name="experts",
    )(first_block, n_blocks, n_used, xs, w_gu, w_down)


TM_COMBINE = 512


def _combine_kernel(*refs):
    yg_refs = refs[:TOP_K]
    gw_ref, x1_ref, xm_ref, g2_ref, wsgu_ref, wsd_ref, lg_ref, lb_ref, out_ref = refs[TOP_K:]
    gu = jnp.dot(_unpack_bf16(xm_ref[...]), wsgu_ref[...], preferred_element_type=F32)
    act = _silu(gu[:, :D_EXPERT]) * gu[:, D_EXPERT:]
    acc = jnp.dot(act.astype(BF16), wsd_ref[...], preferred_element_type=F32)
    gw = gw_ref[...]
    for j in range(TOP_K):
        acc = acc + gw[:, j:j + 1] * _unpack_rows(yg_refs[j][...])
    z = DEEPNORM_ALPHA * x1_ref[...] + g2_ref[0] * acc
    out_ref[...] = _layer_norm(z) * lg_ref[...] + lb_ref[...]


def _combine(yg, gw, x1, xm, gate2, w_sgu, w_sd, ln_g, ln_b, n_per_batch):
    t, d = x1.shape
    tm = TM_COMBINE
    per = n_per_batch // tm
    steps = t // tm
    tok = lambda w: pl.BlockSpec((tm, w), lambda i: (i, 0))
    full = lambda r, c: pl.BlockSpec((r, c), lambda i: (0, 0))
    slot = lambda j: pl.BlockSpec((tm, PACK_W), lambda i: (j * steps + i, 0))
    return pl.pallas_call(
        _combine_kernel,
        out_shape=jax.ShapeDtypeStruct((t, d), F32),
        grid=(steps,),
        in_specs=[slot(j) for j in range(TOP_K)]
        + [tok(TOP_K), tok(d), tok(PACK_W), pl.BlockSpec((1, 1, d), lambda i: (i // per, 0, 0)),
           full(d, 2 * D_EXPERT), full(D_EXPERT, d), full(1, d), full(1, d)],
        out_specs=tok(d),
        compiler_params=_cparams(("arbitrary",)),
        name="combine",
    )(*([yg] * TOP_K), gw, x1, xm, gate2, w_sgu, w_sd, ln_g, ln_b)


def _mixer_heads(stream, shift, scale, w_in_k, conv_w9, conv_b, gate_b_row, head_g, c0, m0, grid2d):
    b, n, d = stream.shape
    y, f_in = _inproj(stream.reshape(b * n, d), shift, scale, w_in_k, n)
    y = y.reshape(b, n, PROJ_W)
    qk = _conv(y, conv_w9, conv_b, grid2d)
    mread, c_fin, m_fin = _mlstm(qk, y, _gates(y, gate_b_row), c0, m0, head_g)
    return f_in, mread, c_fin, m_fin


def _moe(xm, x1, scores, bias_row, gate2, w_gu, w_down, w_sgu, w_sd, ln_g, ln_b, n_per_batch):
    t = xm.shape[0]
    n_assign = t * TOP_K
    n_rows = n_assign + N_EXPERTS * ROW_BLK
    idx_t, gw, rank_t, counts = _route(scores, bias_row)
    sizes = counts[:, 0].astype(jnp.int32)
    padded = (sizes + ROW_BLK - 1) // ROW_BLK * ROW_BLK
    pad_end = jnp.cumsum(padded)
    pad_start = (pad_end - padded).astype(jnp.int32)
    dest_t = _dest(pad_start, idx_t, rank_t)
    dest3 = dest_t.reshape(TOP_K, t // SC_ROWS, SC_ROWS).transpose(1, 0, 2)

    xs = _sc_dispatch(xm, dest3, n_rows)
    yb = _experts(pad_start // ROW_BLK, (padded // ROW_BLK).astype(jnp.int32),
                  (pad_end[-1:] // ROW_BLK).astype(jnp.int32), xs, w_gu, w_down)
    yg = _sc_gather(yb, dest_t.reshape(n_assign // SC_GATHER_ROWS, SC_GATHER_ROWS))
    return _combine(yg, gw, x1, xm, gate2, w_sgu, w_sd, ln_g, ln_b, n_per_batch)


def kernel(x, c, ctx, c_ctx, w_ada, b_ada, w_in, conv_w, conv_b, gate_b, head_g, w_out, ln1_g, ln1_b,
           w_router, router_bias, w_expert_gu, w_expert_down, w_shared_gu, w_shared_down, ln2_g, ln2_b):
    bsz, n, d = x.shape
    depth = w_ada.shape[0]
    assert depth == 1
    t = bsz * n
    tables = _dft_tables()
    for l in range(depth):
        cvec = jnp.zeros((SUBLANES, d), F32).at[:bsz].set(c).at[bsz].set(c_ctx)
        mod = _ada(cvec, w_ada[l], b_ada[l][None, :])
        mods = mod.reshape(SUBLANES, 6, 1, d)
        mod_x = [mods[:bsz, j] for j in range(6)]
        mod_c = [mods[bsz:bsz + 1, j] for j in range(2)]

        wl = w_in[l]
        g0 = 4 * D_MLSTM
        w_in_k = jnp.concatenate(
            [wl[:, :g0 + N_GATE_COLS], jnp.zeros((d, LANES - N_GATE_COLS), F32),
             wl[:, g0 + N_GATE_COLS:]], axis=1).astype(BF16)
        conv_w9 = conv_w[l].reshape(9, 2 * D_MLSTM)
        conv_b_row = conv_b[l][None, :]
        gate_b_row = jnp.zeros((1, LANES), F32).at[0, :N_GATE_COLS].set(gate_b[l].reshape(-1))
        head_g_row = head_g[l][None, :]

        c0 = jnp.zeros((bsz, N_HEADS, 2, STATE_ROWS, D_HEAD), F32)
        m0 = jnp.zeros((bsz, N_HEADS, 2, SUBLANES, LANES), F32)
        _, _, c_ctx_fin, m_ctx_fin = _mixer_heads(ctx, mod_c[0], mod_c[1], w_in_k, conv_w9, conv_b_row,
                                                  gate_b_row, head_g_row, c0, m0, grid2d=False)
        f_in, mread, _, _ = _mixer_heads(x, mod_x[0], mod_x[1], w_in_k, conv_w9, conv_b_row,
                                      gate_b_row, head_g_row, c_ctx_fin, m_ctx_fin, grid2d=True)

        four = _fourier(f_in.reshape(bsz, DFT_R, DFT_R, D_FOURIER), tables).reshape(t, D_FOURIER)

        w_o = w_out[l].astype(BF16)
        w_r = jnp.zeros((d, LANES), F32).at[:, :N_EXPERTS].set(w_router[l]).astype(BF16)
        x2d = x.reshape(t, d)
        x1, xm, scores = _outproj(x2d, mread.reshape(t, D_MLSTM), four, w_o[:D_MLSTM], w_o[D_MLSTM:],
                                  mod_x[2], ln1_g[l][None, :], ln1_b[l][None, :], mod_x[3], mod_x[4],
                                  w_r, n)

        bias_rep = jnp.broadcast_to(router_bias[l][:, None], (N_EXPERTS, LANES))
        x = _moe(xm, x1, scores, bias_rep, mod_x[5], w_expert_gu[l], w_expert_down[l],
                 w_shared_gu[l].astype(BF16), w_shared_down[l].astype(BF16),
                 ln2_g[l][None, :], ln2_b[l][None, :], n).reshape(bsz, n, d)
    return x
```

```python
import functools

import numpy as np
import jax
import jax.numpy as jnp
from jax import lax
from jax.experimental import pallas as pl
from jax.experimental.pallas import tpu as pltpu
from jax.experimental.pallas import tpu_sc as plsc

F32 = jnp.float32
BF16 = jnp.bfloat16
HIGHEST = lax.Precision.HIGHEST

D_MODEL = 1024
GRID_W = 64
N_HEADS = 4
D_HEAD = 128
D_MLSTM = N_HEADS * D_HEAD
D_FOURIER = 512
FOURIER_GROUP = 64
N_GATE_COLS = 16
CHUNK = 128
N_EXPERTS = 64
TOP_K = 8
D_EXPERT = 256
ROUTED_SCALE = 2.5
DEEPNORM_ALPHA = 2.0 ** 0.25
LN_EPS = 1e-5

LANES = 128
SUBLANES = 8
VMEM_LIMIT = 48 * 1024 * 1024
FOUR_VMEM_LIMIT = 56 * 1024 * 1024

COL_QK = 0
COL_V = 2 * D_MLSTM
COL_O = 3 * D_MLSTM
COL_G = 4 * D_MLSTM
PROJ_W = COL_G + LANES

TN_ADA = 2048
TM = 1024
TM_ROUTE = 1024
ROW_BLK = 512
ROW_HALF = 256
EXP_AHEAD = 5
EXP_IN_SLOTS = 8
EXP_OUT_SLOTS = 4
PACK_W = D_MODEL // 2
SC_ROWS = 128
SC_GATHER_ROWS = 64
CONV_PAD = 72
CONV_RB = 256
DFT_R = 64


def _cparams(sem, vmem=VMEM_LIMIT):
    return pltpu.CompilerParams(dimension_semantics=sem, vmem_limit_bytes=vmem)


def _layer_norm(x):
    mu = jnp.mean(x, axis=-1, keepdims=True)
    xc = x - mu
    var = jnp.mean(xc * xc, axis=-1, keepdims=True)
    return xc * lax.rsqrt(var + LN_EPS)


def _silu(x):
    return x * jax.nn.sigmoid(x)


def _pack_pair(lo, hi):
    as_bits = lambda u: lax.bitcast_convert_type(u.astype(BF16).astype(F32), jnp.uint32)
    word = as_bits(hi) | lax.shift_right_logical(as_bits(lo), jnp.uint32(16))
    return lax.bitcast_convert_type(word, jnp.int32)


def _unpack_pair_bf16(w):
    lo = lax.bitcast_convert_type(w.astype(jnp.int16), BF16)
    hi = lax.bitcast_convert_type(lax.shift_right_logical(w, jnp.int32(16)).astype(jnp.int16), BF16)
    return lo, hi


def _pack_rows(v):
    return _pack_pair(v[:, :PACK_W], v[:, PACK_W:])


def _unpack_bf16(w):
    return jnp.concatenate(_unpack_pair_bf16(w), axis=-1)


def _unpack_rows(w):
    u = lax.bitcast_convert_type(w, jnp.uint32)
    lo = lax.bitcast_convert_type(lax.shift_left(u, jnp.uint32(16)), F32)
    hi = lax.bitcast_convert_type(u & jnp.uint32(0xFFFF0000), F32)
    return jnp.concatenate([lo, hi], axis=1)


def _ada_kernel(c_ref, w_ref, b_ref, o_ref):
    s = _silu(c_ref[...]).astype(BF16)
    o_ref[...] = jnp.dot(s, w_ref[...].astype(BF16), preferred_element_type=F32) + b_ref[...]


def _ada(cvec, w_ada, b_ada):
    rows, d = cvec.shape
    n_out = w_ada.shape[1]
    tn = TN_ADA
    return pl.pallas_call(
        _ada_kernel,
        out_shape=jax.ShapeDtypeStruct((rows, n_out), F32),
        grid=(n_out // tn,),
        in_specs=[pl.BlockSpec((rows, d), lambda j: (0, 0)),
                  pl.BlockSpec((d, tn), lambda j: (0, j)),
                  pl.BlockSpec((1, tn), lambda j: (0, j))],
        out_specs=pl.BlockSpec((rows, tn), lambda j: (0, j)),
        compiler_params=_cparams(("arbitrary",)),
        name="ada",
    )(cvec, w_ada, b_ada)


def _inproj_kernel(x_ref, sh_ref, sc_ref, w_ref, y_ref, f_ref):
    half = x_ref.shape[0] // 2
    for h in range(2):
        r = slice(h * half, (h + 1) * half)
        u = _layer_norm(x_ref[r, :]) * (1.0 + sc_ref[0]) + sh_ref[0]
        proj = jnp.dot(u.astype(BF16), w_ref[...], preferred_element_type=F32)
        y_ref[r, :] = proj[:, :PROJ_W]
        f_ref[r, :] = proj[:, PROJ_W:]


def _inproj(x2d, shift, scale, w, n_per_batch):
    t, d = x2d.shape
    if shift.shape[0] == 1:
        tm = min(TM, t)
        per = t // tm
    else:
        tm = min(TM, n_per_batch)
        per = n_per_batch // tm
    return pl.pallas_call(
        _inproj_kernel,
        out_shape=(jax.ShapeDtypeStruct((t, PROJ_W), F32), jax.ShapeDtypeStruct((t, D_FOURIER), F32)),
        grid=(t // tm,),
        in_specs=[pl.BlockSpec((tm, d), lambda i: (i, 0)),
                  pl.BlockSpec((1, 1, d), lambda i: (i // per, 0, 0)),
                  pl.BlockSpec((1, 1, d), lambda i: (i // per, 0, 0)),
                  pl.BlockSpec((d, PROJ_W + D_FOURIER), lambda i: (0, 0))],
        out_specs=(pl.BlockSpec((tm, PROJ_W), lambda i: (i, 0)),
                   pl.BlockSpec((tm, D_FOURIER), lambda i: (i, 0))),
        compiler_params=_cparams(("arbitrary",)),
        name="inproj",
    )(x2d, shift, scale, w)


def _conv_kernel(y_ref, w_ref, b_ref, o_ref, pad_ref, *, n, grid2d, tiles):
    zeros = jnp.zeros((CONV_PAD, LANES), F32)
    pad_ref[0:CONV_PAD, :] = zeros
    pad_ref[CONV_PAD + n:CONV_PAD + n + CONV_PAD, :] = zeros
    rb = min(CONV_RB, n)
    col = lax.broadcasted_iota(jnp.int32, (rb, LANES), 0) % GRID_W
    not_first = col >= 1
    not_last = col <= GRID_W - 2
    row_taps = (0, 1, 2) if grid2d else (1,)
    for ct in range(tiles):
        lanes = slice(ct * LANES, (ct + 1) * LANES)
        pad_ref[CONV_PAD:CONV_PAD + n, :] = y_ref[0, :, lanes]
        k_scale = jnp.where(pl.program_id(1) * tiles + ct >= N_HEADS, D_HEAD ** -0.5, 1.0).astype(F32)
        bias = b_ref[:, lanes]
        for blk in range(n // rb):
            r0 = CONV_PAD + blk * rb
            acc = jnp.zeros((rb, LANES), F32)
            for dj in range(3):
                part = jnp.zeros((rb, LANES), F32)
                for di in row_taps:
                    off = (di - 1) * GRID_W + (dj - 1)
                    tap = w_ref[di * 3 + dj:di * 3 + dj + 1, lanes]
                    part = part + tap * pad_ref[r0 + off:r0 + off + rb, :]
                if grid2d and dj == 0:
                    part = jnp.where(not_first, part, 0.0)
                if grid2d and dj == 2:
                    part = jnp.where(not_last, part, 0.0)
                acc = acc + part
            val = _silu(acc + bias) * k_scale
            o_ref[0, blk * rb:(blk + 1) * rb, lanes] = val.astype(o_ref.dtype)


def _conv(y3d, conv_w9, conv_b, grid2d):
    b, n, _ = y3d.shape
    n_ct = 2 * D_MLSTM // LANES
    tiles = 1 if grid2d else n_ct
    w = tiles * LANES
    return pl.pallas_call(
        functools.partial(_conv_kernel, n=n, grid2d=grid2d, tiles=tiles),
        out_shape=jax.ShapeDtypeStruct((b, n, 2 * D_MLSTM), BF16),
        grid=(b, n_ct // tiles),
        in_specs=[pl.BlockSpec((1, n, w), lambda i, c: (i, 0, c)),
                  pl.BlockSpec((9, w), lambda i, c: (0, c)),
                  pl.BlockSpec((1, w), lambda i, c: (0, c))],
        out_specs=pl.BlockSpec((1, n, w), lambda i, c: (i, 0, c)),
        scratch_shapes=[pltpu.VMEM((n + 2 * CONV_PAD, LANES), F32)],
        compiler_params=_cparams(("arbitrary", "arbitrary")),
        name="conv2d" if grid2d else "conv1d",
    )(y3d, conv_w9, conv_b)


STATE_ROWS = D_HEAD + 32
GATE_A, GATE_CM, GATE_B = 0, 1, 2
GATE_GROUP = 32


def _gates_kernel(g_ref, gb_ref, rv_ref, *, nc):
    u_i = lax.broadcasted_iota(jnp.int32, (CHUNK, CHUNK), 0)
    t_i = lax.broadcasted_iota(jnp.int32, (CHUNK, CHUNK), 1)
    prefix = (u_i <= t_i).astype(F32)
    suffix = (u_i >= t_i).astype(F32)
    grp = min(GATE_GROUP, nc)
    rows = grp * SUBLANES
    lane = lax.broadcasted_iota(jnp.int32, (rows, LANES), 1)
    is_fwd = lax.broadcasted_iota(jnp.int32, (rows, LANES), 0) % SUBLANES < N_HEADS
    h = N_HEADS

    def running_max(a):
        fwd, bwd = a, a
        sh = 1
        while sh < CHUNK:
            fwd = jnp.where(lane >= sh, jnp.maximum(fwd, pltpu.roll(fwd, sh, axis=1)), fwd)
            bwd = jnp.where(lane < CHUNK - sh, jnp.maximum(bwd, pltpu.roll(bwd, CHUNK - sh, axis=1)), bwd)
            sh *= 2
        return jnp.where(is_fwd, fwd, bwd)

    def body(i, carry):
        c0 = i * grp
        r0 = pl.multiple_of(c0 * CHUNK, CHUNK)
        pre = g_ref[0, pl.ds(r0, grp * CHUNK), :] + gb_ref[...]
        pre_t = jnp.concatenate(
            [pre[j * CHUNK:(j + 1) * CHUNK].T[:N_GATE_COLS, :] for j in range(grp)], axis=0)
        lf_t = -(jnp.maximum(-pre_t, 0.0) + jnp.log1p(jnp.exp(-jnp.abs(pre_t))))
        cum_f = jnp.dot(lf_t, prefix, precision=HIGHEST, preferred_element_type=F32)
        cum_b = jnp.dot(lf_t, suffix, precision=HIGHEST, preferred_element_type=F32)
        pick = lambda t, j, q: t[j * N_GATE_COLS + q * h:j * N_GATE_COLS + (q + 1) * h]
        b8 = jnp.concatenate([x for j in range(grp) for x in (pick(cum_f, j, 1), pick(cum_b, j, 3))], axis=0)
        li8 = jnp.concatenate([x for j in range(grp) for x in (pick(pre_t, j, 0), pick(pre_t, j, 2))], axis=0)
        a8 = li8 - b8
        as_chunks = lambda t: t.reshape(grp, SUBLANES, LANES)
        rv_ref[0, pl.ds(c0, grp), GATE_A] = as_chunks(a8)
        rv_ref[0, pl.ds(c0, grp), GATE_CM] = as_chunks(running_max(a8))
        rv_ref[0, pl.ds(c0, grp), GATE_B] = as_chunks(b8)
        return carry

    lax.fori_loop(0, nc // grp, body, 0)


def _gates(y3d, gate_b_row):
    b, n, _ = y3d.shape
    nc = n // CHUNK
    return pl.pallas_call(
        functools.partial(_gates_kernel, nc=nc),
        out_shape=jax.ShapeDtypeStruct((b, nc, 3, SUBLANES, LANES), F32),
        grid=(b,),
        in_specs=[pl.BlockSpec((1, n, LANES), lambda i: (i, 0, COL_G // LANES)),
                  pl.BlockSpec((1, LANES), lambda i: (0, 0))],
        out_specs=pl.BlockSpec((1, nc, 3, SUBLANES, LANES), lambda i: (i, 0, 0, 0, 0)),
        compiler_params=_cparams(("arbitrary",)),
        name="gates",
    )(y3d, gate_b_row)


def _mlstm_kernel(q_ref, k_ref, v_ref, o_ref, rv_ref, c0_ref, m0_ref, hg_ref,
                  out_ref, cfin_ref, mfin_ref, hf_ref, hb_ref, cst_ref, mst_ref, *, nc, heads):
    cst_ref[...] = c0_ref[0]
    mst_ref[...] = m0_ref[0]
    s_i = lax.broadcasted_iota(jnp.int32, (CHUNK, CHUNK), 0)
    t_i = lax.broadcasted_iota(jnp.int32, (CHUNK, CHUNK), 1)
    ones_row = (s_i[:STATE_ROWS - D_HEAD] == 0).astype(F32)

    def one_chunk(c, d, mask_t, last, h_ref, hh):
        r0 = pl.multiple_of(c * CHUNK, CHUNK)
        lanes = slice(hh * D_HEAD, (hh + 1) * D_HEAD)
        g = d * N_HEADS + pl.program_id(1) * heads + hh
        gate_row = lambda tbl: rv_ref[0, pl.ds(c, 1), tbl, pl.ds(g, 1), :].reshape(1, LANES)
        a_row, cm_row, b_row = gate_row(GATE_A), gate_row(GATE_CM), gate_row(GATE_B)
        q_t = q_ref[0, pl.ds(r0, CHUNK), lanes].T
        k = k_ref[0, pl.ds(r0, CHUNK), lanes]
        v_t_aug = jnp.concatenate([v_ref[0, pl.ds(r0, CHUNK), lanes].T, ones_row], axis=0)
        m_prev = mst_ref[hh, d, 0:1, :]
        m_row = jnp.maximum(m_prev, cm_row)
        a_bc = jnp.broadcast_to(a_row, (CHUNK, CHUNK)).T
        d_t = jnp.where(mask_t, jnp.exp(a_bc - m_row), 0.0)
        s_t = jnp.dot(k, q_t, preferred_element_type=F32) * d_t
        cs = cst_ref[hh, d]
        rhs = jnp.concatenate(
            [s_t.astype(BF16), (q_t.astype(F32) * jnp.exp(m_prev - m_row)).astype(BF16)], axis=0)
        lhs = jnp.concatenate([v_t_aug.astype(BF16), cs.astype(BF16)], axis=1)
        h_t = jnp.dot(lhs, rhs, preferred_element_type=F32)
        den = h_t[D_HEAD:D_HEAD + 1, :]
        floor = jnp.exp(-b_row - m_row)
        h_ref[hh, pl.ds(c, 1)] = (h_t[:D_HEAD] / jnp.maximum(jnp.abs(den), floor))[None]
        m_last = m_row[:, last:last + 1]
        wv_t = (v_t_aug * jnp.exp(a_row - m_last)).astype(BF16)
        cst_ref[hh, d] = jnp.exp(m_prev - m_last) * cs + jnp.dot(wv_t, k, preferred_element_type=F32)
        mst_ref[hh, d] = jnp.broadcast_to(b_row[:, last:last + 1] + m_last, (SUBLANES, LANES))

    def body(c, carry):
        for hh in range(heads):
            one_chunk(c, 0, s_i <= t_i, CHUNK - 1, hf_ref, hh)
            one_chunk(nc - 1 - c, 1, s_i >= t_i, 0, hb_ref, hh)
        return carry

    lax.fori_loop(0, nc, body, 0, unroll=min(16 // heads, nc))
    cfin_ref[0] = cst_ref[...]
    mfin_ref[0] = mst_ref[...]

    def readout(c, carry):
        r0 = pl.multiple_of(c * CHUNK, CHUNK)
        for hh in range(heads):
            lanes = slice(hh * D_HEAD, (hh + 1) * D_HEAD)
            h_t = hf_ref[hh, pl.ds(c, 1)][0] + hb_ref[hh, pl.ds(c, 1)][0]
            hc = h_t - jnp.mean(h_t, axis=0, keepdims=True)
            hn = (hc * lax.rsqrt(jnp.mean(hc * hc, axis=0, keepdims=True) + LN_EPS)).T
            gate = jax.nn.sigmoid(o_ref[0, pl.ds(r0, CHUNK), lanes])
            out_ref[0, pl.ds(r0, CHUNK), lanes] = (hn * hg_ref[:, lanes] * gate).astype(out_ref.dtype)
        return carry

    lax.fori_loop(0, nc, readout, 0, unroll=min(16, nc))


def _mlstm(qk, y3d, rowv, c0, m0, head_g):
    b, n, _ = qk.shape
    nc = n // CHUNK
    heads = N_HEADS if nc <= 4 else 1
    w = heads * D_HEAD
    v_blk = COL_V // w
    o_blk = COL_O // w
    tok = lambda off: pl.BlockSpec((1, n, w), lambda i, h: (i, 0, off + h))
    state_c = pl.BlockSpec((1, heads, 2, STATE_ROWS, D_HEAD), lambda i, h: (i, h, 0, 0, 0))
    state_m = pl.BlockSpec((1, heads, 2, SUBLANES, LANES), lambda i, h: (i, h, 0, 0, 0))
    return pl.pallas_call(
        functools.partial(_mlstm_kernel, nc=nc, heads=heads),
        out_shape=(jax.ShapeDtypeStruct((b, n, D_MLSTM), BF16),
                   jax.ShapeDtypeStruct((b, N_HEADS, 2, STATE_ROWS, D_HEAD), F32),
                   jax.ShapeDtypeStruct((b, N_HEADS, 2, SUBLANES, LANES), F32)),
        grid=(b, N_HEADS // heads),
        in_specs=[tok(0), tok(D_MLSTM // w), tok(v_blk), tok(o_blk),
                  pl.BlockSpec((1, nc, 3, SUBLANES, LANES), lambda i, h: (i, 0, 0, 0, 0)),
                  state_c, state_m,
                  pl.BlockSpec((1, w), lambda i, h: (0, h))],
        out_specs=(pl.BlockSpec((1, n, w), lambda i, h: (i, 0, h)), state_c, state_m),
        scratch_shapes=[pltpu.VMEM((heads, nc, D_HEAD, CHUNK), F32), pltpu.VMEM((heads, nc, D_HEAD, CHUNK), F32),
                        pltpu.VMEM((heads, 2, STATE_ROWS, D_HEAD), F32),
                        pltpu.VMEM((heads, 2, SUBLANES, LANES), F32)],
        compiler_params=_cparams(("arbitrary", "arbitrary")),
        name="mlstm",
    )(qk, qk, y3d, y3d, rowv, c0, m0, head_g)


def _dft_tables():
    r = DFT_R
    idx = np.arange(r)
    cg = np.cos(2 * np.pi * np.outer(idx, idx) / FOURIER_GROUP)
    sg = np.sin(2 * np.pi * np.outer(idx, idx) / FOURIER_GROUP)
    n_grp = D_FOURIER // FOURIER_GROUP
    bdc = np.kron(np.eye(n_grp), cg) / 8.0
    bds = np.kron(np.eye(n_grp), sg) / 8.0
    k1 = idx[:, None]
    a = idx[None, :]
    m1 = np.zeros((r, 2 * r, 2 * r))
    for b2 in range(r):
        th = 2 * np.pi * k1 * (r * a + b2) / (r * r)
        ec, es = np.cos(th), np.sin(th)
        m1[b2] = np.block([[ec, -es], [es, ec]]) / 8.0
    th2 = 2 * np.pi * np.outer(idx, idx) / r
    m2 = np.concatenate([np.cos(th2), -np.sin(th2)], axis=1) / 8.0
    as_bf16 = lambda t: jnp.asarray(t, dtype=F32).astype(BF16)
    return as_bf16(bdc), as_bf16(bds), as_bf16(m1), as_bf16(m2)


FOUR_BB = 8


N_SLABS = D_FOURIER // LANES


def _to_slabs(slab_ref, val):
    for s in range(N_SLABS):
        slab_ref[s] = val[:, s * LANES:(s + 1) * LANES]


def _strided_rows(slab_ref, start):
    return jnp.concatenate(
        [slab_ref[s, pl.ds(start, DFT_R, stride=FOUR_BB), :] for s in range(N_SLABS)], axis=1)


def _four_kernel(f_ref, bdc_ref, bds_ref, m1_ref, m2_ref, o_ref, z_ref, ps_ref, qs_ref, zs_ref, os_ref):
    r = DFT_R

    def stage1(blk, carry):
        b0 = pl.multiple_of(blk * FOUR_BB, FOUR_BB)
        fb = f_ref[0, :, pl.ds(b0, FOUR_BB), :].reshape(r * FOUR_BB, D_FOURIER).astype(BF16)
        _to_slabs(ps_ref, jnp.dot(fb, bdc_ref[...], preferred_element_type=F32))
        _to_slabs(qs_ref, jnp.dot(fb, bds_ref[...], preferred_element_type=F32))
        for j in range(FOUR_BB):
            x2 = jnp.concatenate([_strided_rows(ps_ref, j), _strided_rows(qs_ref, j)], axis=0).astype(BF16)
            zz = jnp.dot(m1_ref[b0 + j], x2, preferred_element_type=F32)
            z_ref[b0 + j] = _pack_pair(zz[:r], zz[r:])
        return carry

    lax.fori_loop(0, r // FOUR_BB, stage1, 0)

    def stage2(blk, carry):
        k0 = pl.multiple_of(blk * FOUR_BB, FOUR_BB)
        _to_slabs(zs_ref, z_ref[:, pl.ds(k0, FOUR_BB), :].reshape(r * FOUR_BB, D_FOURIER))
        for j in range(FOUR_BB):
            zc, zs = _unpack_pair_bf16(_strided_rows(zs_ref, j))
            o = jnp.dot(m2_ref[...], jnp.concatenate([zc, zs], axis=0), preferred_element_type=F32)
            for s in range(N_SLABS):
                os_ref[s, pl.ds(j, r, stride=FOUR_BB), :] = o[:, s * LANES:(s + 1) * LANES]
        for s in range(N_SLABS):
            o_ref[0, :, pl.ds(k0, FOUR_BB), s * LANES:(s + 1) * LANES] = os_ref[s].reshape(r, FOUR_BB, LANES)
        return carry

    lax.fori_loop(0, r // FOUR_BB, stage2, 0)


def _fourier(y4, tables):
    bdc, bds, m1, m2 = tables
    b = y4.shape[0]
    r = DFT_R
    slab = lambda dt: pltpu.VMEM((N_SLABS, r * FOUR_BB, LANES), dt)
    const = lambda *shape: pl.BlockSpec(shape, lambda i: (0,) * len(shape))
    return pl.pallas_call(
        _four_kernel,
        out_shape=jax.ShapeDtypeStruct((b, r, r, D_FOURIER), F32),
        grid=(b,),
        in_specs=[pl.BlockSpec((1, r, r, D_FOURIER), lambda i: (i, 0, 0, 0)),
                  const(D_FOURIER, D_FOURIER), const(D_FOURIER, D_FOURIER),
                  const(r, 2 * r, 2 * r), const(r, 2 * r)],
        out_specs=pl.BlockSpec((1, r, r, D_FOURIER), lambda i: (i, 0, 0, 0)),
        scratch_shapes=[pltpu.VMEM((r, r, D_FOURIER), jnp.int32), slab(F32), slab(F32), slab(jnp.int32), slab(F32)],
        compiler_params=_cparams(("arbitrary",), vmem=FOUR_VMEM_LIMIT),
        name="fourier",
    )(y4, bdc, bds, m1, m2)


def _outproj_kernel(x_ref, mr_ref, fo_ref, wm_ref, wf_ref, g1_ref, lg_ref, lb_ref,
                    sh_ref, sc_ref, wr_ref, x1_ref, xm_ref, s_ref):
    half = x_ref.shape[0] // 2
    for h in range(2):
        r = slice(h * half, (h + 1) * half)
        mix = (jnp.dot(mr_ref[r, :], wm_ref[...], preferred_element_type=F32)
               + jnp.dot(fo_ref[r, :].astype(BF16), wf_ref[...], preferred_element_type=F32))
        x1 = _layer_norm(DEEPNORM_ALPHA * x_ref[r, :] + g1_ref[0] * mix) * lg_ref[...] + lb_ref[...]
        x1_ref[r, :] = x1
        xm = (_layer_norm(x1) * (1.0 + sc_ref[0]) + sh_ref[0]).astype(BF16)
        xm_ref[r, :] = _pack_rows(xm)
        logits = jnp.dot(xm, wr_ref[...], preferred_element_type=F32)
        s_ref[r, :] = jax.nn.sigmoid(logits)


def _outproj(x2d, mread, four, w_m, w_f, gate1, ln_g, ln_b, shift2, scale2, w_router, n_per_batch):
    t, d = x2d.shape
    per = n_per_batch // TM
    tokd = lambda w: pl.BlockSpec((TM, w), lambda i: (i, 0))
    full = lambda r, c: pl.BlockSpec((r, c), lambda i: (0, 0))
    mod = pl.BlockSpec((1, 1, d), lambda i: (i // per, 0, 0))
    return pl.pallas_call(
        _outproj_kernel,
        out_shape=(jax.ShapeDtypeStruct((t, d), F32),
                   jax.ShapeDtypeStruct((t, PACK_W), jnp.int32),
                   jax.ShapeDtypeStruct((t, LANES), F32)),
        grid=(t // TM,),
        in_specs=[tokd(d), tokd(D_MLSTM), tokd(D_FOURIER), full(D_MLSTM, d), full(D_FOURIER, d),
                  mod, full(1, d), full(1, d), mod, mod, full(d, LANES)],
        out_specs=(tokd(d), tokd(PACK_W), tokd(LANES)),
        compiler_params=_cparams(("arbitrary",)),
        name="outproj",
    )(x2d, mread, four, w_m, w_f, gate1, ln_g, ln_b, shift2, scale2, w_router)


def _route_kernel(s_ref, b_ref, idx_ref, gw_ref, rank_ref, cnt_ref, carry_ref):
    @pl.when(pl.program_id(0) == 0)
    def _():
        carry_ref[...] = jnp.zeros_like(carry_ref)

    tm = s_ref.shape[0]
    reps = tm // LANES
    s_t = s_ref[...].T[:N_EXPERTS, :]
    e_i = lax.broadcasted_iota(jnp.int32, (N_EXPERTS, tm), 0).astype(F32)
    neg_inf = jnp.float32(-jnp.inf)
    sb = s_t + jnp.tile(b_ref[...], (1, reps))
    picks, sels = [], []
    taken = jnp.zeros((N_EXPERTS, tm), F32)
    for _ in range(TOP_K):
        best = jnp.max(sb, axis=0, keepdims=True)
        pick = jnp.min(jnp.where(sb == best, e_i, float(N_EXPERTS)), axis=0, keepdims=True)
        onehot = e_i == pick
        sels.append(jnp.sum(jnp.where(onehot, s_t, 0.0), axis=0, keepdims=True))
        picks.append(pick)
        sb = jnp.where(onehot, neg_inf, sb)
        taken = taken + onehot.astype(F32)
    u_i = lax.broadcasted_iota(jnp.int32, (tm, tm), 0)
    t_i = lax.broadcasted_iota(jnp.int32, (tm, tm), 1)
    before = (u_i < t_i).astype(BF16)
    taken_bf = taken.astype(BF16)
    rank_all = jnp.tile(carry_ref[...], (1, reps)) + jnp.dot(taken_bf, before, preferred_element_type=F32)
    carry_ref[...] = carry_ref[...] + jnp.dot(taken_bf, jnp.ones((tm, LANES), BF16),
                                              preferred_element_type=F32)
    cnt_ref[...] = carry_ref[...]
    total = sels[0]
    for v in sels[1:]:
        total = total + v
    ranks = [jnp.sum(jnp.where(e_i == picks[j], rank_all, 0.0), axis=0, keepdims=True) for j in range(TOP_K)]
    idx_ref[...] = jnp.concatenate(picks, axis=0).astype(jnp.int32)
    rank_ref[...] = jnp.concatenate(ranks, axis=0).astype(jnp.int32)
    gw_t = jnp.concatenate([ROUTED_SCALE * v / total for v in sels]
                           + [jnp.zeros((LANES - TOP_K, tm), F32)], axis=0)
    gw_ref[...] = gw_t.T[:, :TOP_K]


def _route(scores, bias_rep):
    t = scores.shape[0]
    tok = lambda w: pl.BlockSpec((TM_ROUTE, w), lambda i: (i, 0))
    slot_major = pl.BlockSpec((TOP_K, TM_ROUTE), lambda i: (0, i))
    per_expert = pl.BlockSpec((N_EXPERTS, LANES), lambda i: (0, 0))
    return pl.pallas_call(
        _route_kernel,
        out_shape=(jax.ShapeDtypeStruct((TOP_K, t), jnp.int32),
                   jax.ShapeDtypeStruct((t, TOP_K), F32),
                   jax.ShapeDtypeStruct((TOP_K, t), jnp.int32),
                   jax.ShapeDtypeStruct((N_EXPERTS, LANES), F32)),
        grid=(t // TM_ROUTE,),
        in_specs=[tok(LANES), per_expert],
        out_specs=(slot_major, tok(TOP_K), slot_major, per_expert),
        scratch_shapes=[pltpu.VMEM((N_EXPERTS, LANES), F32)],
        compiler_params=_cparams(("arbitrary",)),
        name="route",
    )(scores, bias_rep)


TN_DEST = 2048


def _dest_kernel(ps_ref, idx_ref, rank_ref, dest_ref):
    idx = idx_ref[...]
    start = jnp.zeros(idx.shape, jnp.int32)
    for e in range(N_EXPERTS):
        start = jnp.where(idx == e, ps_ref[e], start)
    dest_ref[...] = start + rank_ref[...]


def _dest(pad_start, idx_t, rank_t):
    t = idx_t.shape[1]
    blk = lambda: pl.BlockSpec((TOP_K, TN_DEST), lambda i, ps: (0, i))
    grid_spec = pltpu.PrefetchScalarGridSpec(
        num_scalar_prefetch=1, grid=(t // TN_DEST,), in_specs=[blk(), blk()], out_specs=blk())
    return pl.pallas_call(
        _dest_kernel,
        out_shape=jax.ShapeDtypeStruct((TOP_K, t), jnp.int32),
        grid_spec=grid_spec,
        compiler_params=_cparams(("arbitrary",)),
        name="dest",
    )(pad_start, idx_t, rank_t)


def _sc_worker_id(info):
    return lax.axis_index("s") * info.num_cores + lax.axis_index("c")


def _sc_dispatch(xm, dest3, n_rows):
    t, w = xm.shape
    info = plsc.get_sparse_core_info()
    n_workers = info.num_cores * info.num_subcores
    per_worker = t // SC_ROWS // n_workers
    mesh = plsc.VectorSubcoreMesh(core_axis_name="c", subcore_axis_name="s")

    @functools.partial(
        pl.kernel, mesh=mesh,
        out_type=jax.ShapeDtypeStruct((n_rows, w), xm.dtype),
        scratch_types=[pltpu.VMEM((TOP_K, SC_ROWS), jnp.int32), pltpu.VMEM((SC_ROWS, w), xm.dtype)],
        name="sc_dispatch",
    )
    def run(xm_hbm, dest_hbm, xs_hbm, idx_v, rows_v):
        first = _sc_worker_id(info) * per_worker

        @pl.loop(0, per_worker)
        def _(ci):
            chunk = first + ci
            pltpu.sync_copy(dest_hbm.at[chunk], idx_v)
            pltpu.sync_copy(xm_hbm.at[pl.ds(chunk * SC_ROWS, SC_ROWS)], rows_v)
            for j in range(TOP_K):
                pltpu.sync_copy(rows_v, xs_hbm.at[idx_v.at[j]])

    return run(xm, dest3)


def _sc_gather(table, idx2):
    n_chunks, rows = idx2.shape
    w = table.shape[1]
    info = plsc.get_sparse_core_info()
    n_workers = info.num_cores * info.num_subcores
    per_worker = n_chunks // n_workers
    assert per_worker % 2 == 0
    mesh = plsc.VectorSubcoreMesh(core_axis_name="c", subcore_axis_name="s")

    @functools.partial(
        pl.kernel, mesh=mesh,
        out_type=jax.ShapeDtypeStruct((n_chunks * rows, w), table.dtype),
        scratch_types=[pltpu.VMEM((2, rows), jnp.int32), pltpu.VMEM((2, rows, w), table.dtype),
                       pltpu.SemaphoreType.DMA, pltpu.SemaphoreType.DMA],
        name="sc_gather",
    )
    def run(table_hbm, idx_hbm, out_hbm, idx_v, rows_v, sem0, sem1):
        first = _sc_worker_id(info) * per_worker
        sems = (sem0, sem1)

        def gather(b):
            return pltpu.make_async_copy(table_hbm.at[idx_v.at[b]], rows_v.at[b], sems[b])

        def start(chunk, b):
            pltpu.sync_copy(idx_hbm.at[chunk], idx_v.at[b])
            gather(b).start()

        def finish(chunk, b):
            gather(b).wait()
            pltpu.sync_copy(rows_v.at[b], out_hbm.at[pl.ds(chunk * rows, rows)])

        start(first, 0)

        @pl.loop(0, per_worker, step=2)
        def _(ci):
            chunk = first + ci
            start(chunk + 1, 1)
            finish(chunk, 0)

            @pl.when(ci + 2 < per_worker)
            def _():
                start(chunk + 2, 0)

            finish(chunk + 1, 1)

    return run(table, idx2)


def _expert_kernel(first_ref, count_ref, used_ref,
                   xs_hbm, wgu_ref, wd_ref, y_hbm, wgu_bf, wd_bf, xbuf, ybuf, in_sem, out_sem):
    e = pl.program_id(0)
    n_used = used_ref[0]
    first = first_ref[e]
    n_blocks = count_ref[e]

    def rows(g):
        return pl.ds(pl.multiple_of(g * ROW_BLK, ROW_BLK), ROW_BLK)

    def read(g):
        slot = g % EXP_IN_SLOTS
        return pltpu.make_async_copy(xs_hbm.at[rows(g), :], xbuf.at[slot], in_sem.at[slot])

    def write(g):
        slot = g % EXP_OUT_SLOTS
        return pltpu.make_async_copy(ybuf.at[slot], y_hbm.at[rows(g), :], out_sem.at[slot])

    @pl.when(e == 0)
    def _():
        for g0 in range(EXP_AHEAD):
            @pl.when(g0 < n_used)
            def _():
                read(g0).start()

    @pl.when(n_blocks > 0)
    def _():
        wgu_bf[...] = wgu_ref[0].astype(BF16)
        wd_bf[...] = wd_ref[0].astype(BF16)

    def acquire(g):
        @pl.when(g + EXP_AHEAD < n_used)
        def _():
            read(g + EXP_AHEAD).start()

        read(g).wait()

        @pl.when(g >= EXP_OUT_SLOTS)
        def _():
            write(g - EXP_OUT_SLOTS).wait()

    def compute(g):
        in_slot = g % EXP_IN_SLOTS
        out_slot = g % EXP_OUT_SLOTS
        for h in range(ROW_BLK // ROW_HALF):
            r0 = h * ROW_HALF
            x = _unpack_bf16(xbuf[in_slot, r0:r0 + ROW_HALF, :])
            gu = jnp.dot(x, wgu_bf[...], preferred_element_type=F32)
            act = _silu(gu[:, :D_EXPERT]) * gu[:, D_EXPERT:]
            yv = jnp.dot(act.astype(BF16), wd_bf[...], preferred_element_type=F32)
            words = pltpu.pack_elementwise([yv[:, :PACK_W], yv[:, PACK_W:]], packed_dtype=BF16)
            ybuf[out_slot, r0:r0 + ROW_HALF, :] = lax.bitcast_convert_type(words, jnp.int32)

    def pair(k, carry):
        g = first + 2 * k
        acquire(g)
        acquire(g + 1)
        compute(g)
        compute(g + 1)
        write(g).start()
        write(g + 1).start()
        return carry

    lax.fori_loop(0, n_blocks // 2, pair, 0)

    @pl.when(n_blocks % 2 == 1)
    def _():
        g = first + n_blocks - 1
        acquire(g)
        compute(g)
        write(g).start()

    @pl.when(e == pl.num_programs(0) - 1)
    def _():
        for back in range(EXP_OUT_SLOTS, 0, -1):
            @pl.when(n_used >= back)
            def _():
                write(n_used - back).wait()


def _experts(first_block, n_blocks, n_used, xs, w_gu, w_down):
    n_rows, w = xs.shape
    d = D_MODEL
    grid_spec = pltpu.PrefetchScalarGridSpec(
        num_scalar_prefetch=3,
        grid=(N_EXPERTS,),
        in_specs=[pl.BlockSpec(memory_space=pl.ANY),
                  pl.BlockSpec((1, d, 2 * D_EXPERT), lambda e, *_: (e, 0, 0)),
                  pl.BlockSpec((1, D_EXPERT, d), lambda e, *_: (e, 0, 0))],
        out_specs=pl.BlockSpec(memory_space=pl.ANY),
        scratch_shapes=[pltpu.VMEM((d, 2 * D_EXPERT), BF16), pltpu.VMEM((D_EXPERT, d), BF16),
                        pltpu.VMEM((EXP_IN_SLOTS, ROW_BLK, w), jnp.int32),
                        pltpu.VMEM((EXP_OUT_SLOTS, ROW_BLK, w), jnp.int32),
                        pltpu.SemaphoreType.DMA((EXP_IN_SLOTS,)),
                        pltpu.SemaphoreType.DMA((EXP_OUT_SLOTS,))],
    )
    return pl.pallas_call(
        _expert_kernel,
        out_shape=jax.ShapeDtypeStruct((n_rows, w), jnp.int32),
        grid_spec=grid_spec,
        compiler_params=_cparams(("arbitrary",)),
        name="experts",
    )(first_block, n_blocks, n_used, xs, w_gu, w_down)


TM_COMBINE = 512


def _combine_kernel(*refs):
    yg_refs = refs[:TOP_K]
    gw_ref, x1_ref, xm_ref, g2_ref, wsgu_ref, wsd_ref, lg_ref, lb_ref, out_ref = refs[TOP_K:]
    gu = jnp.dot(_unpack_bf16(xm_ref[...]), wsgu_ref[...], preferred_element_type=F32)
    act = _silu(gu[:, :D_EXPERT]) * gu[:, D_EXPERT:]
    acc = jnp.dot(act.astype(BF16), wsd_ref[...], preferred_element_type=F32)
    gw = gw_ref[...]
    for j in range(TOP_K):
        acc = acc + gw[:, j:j + 1] * _unpack_rows(yg_refs[j][...])
    z = DEEPNORM_ALPHA * x1_ref[...] + g2_ref[0] * acc
    out_ref[...] = _layer_norm(z) * lg_ref[...] + lb_ref[...]


def _combine(yg, gw, x1, xm, gate2, w_sgu, w_sd, ln_g, ln_b, n_per_batch):
    t, d = x1.shape
    tm = TM_COMBINE
    per = n_per_batch // tm
    steps = t // tm
    tok = lambda w: pl.BlockSpec((tm, w), lambda i: (i, 0))
    full = lambda r, c: pl.BlockSpec((r, c), lambda i: (0, 0))
    slot = lambda j: pl.BlockSpec((tm, PACK_W), lambda i: (j * steps + i, 0))
    return pl.pallas_call(
        _combine_kernel,
        out_shape=jax.ShapeDtypeStruct((t, d), F32),
        grid=(steps,),
        in_specs=[slot(j) for j in range(TOP_K)]
        + [tok(TOP_K), tok(d), tok(PACK_W), pl.BlockSpec((1, 1, d), lambda i: (i // per, 0, 0)),
           full(d, 2 * D_EXPERT), full(D_EXPERT, d), full(1, d), full(1, d)],
        out_specs=tok(d),
        compiler_params=_cparams(("arbitrary",)),
        name="combine",
    )(*([yg] * TOP_K), gw, x1, xm, gate2, w_sgu, w_sd, ln_g, ln_b)


def _mixer_heads(stream, shift, scale, w_in_k, conv_w9, conv_b, gate_b_row, head_g, c0, m0, grid2d):
    b, n, d = stream.shape
    y, f_in = _inproj(stream.reshape(b * n, d), shift, scale, w_in_k, n)
    y = y.reshape(b, n, PROJ_W)
    qk = _conv(y, conv_w9, conv_b, grid2d)
    mread, c_fin, m_fin = _mlstm(qk, y, _gates(y, gate_b_row), c0, m0, head_g)
    return f_in, mread, c_fin, m_fin


def _moe(xm, x1, scores, bias_row, gate2, w_gu, w_down, w_sgu, w_sd, ln_g, ln_b, n_per_batch):
    t = xm.shape[0]
    n_assign = t * TOP_K
    n_rows = n_assign + N_EXPERTS * ROW_BLK
    idx_t, gw, rank_t, counts = _route(scores, bias_row)
    sizes = counts[:, 0].astype(jnp.int32)
    padded = (sizes + ROW_BLK - 1) // ROW_BLK * ROW_BLK
    pad_end = jnp.cumsum(padded)
    pad_start = (pad_end - padded).astype(jnp.int32)
    dest_t = _dest(pad_start, idx_t, rank_t)
    dest3 = dest_t.reshape(TOP_K, t // SC_ROWS, SC_ROWS).transpose(1, 0, 2)

    xs = _sc_dispatch(xm, dest3, n_rows)
    yb = _experts(pad_start // ROW_BLK, (padded // ROW_BLK).astype(jnp.int32),
                  (pad_end[-1:] // ROW_BLK).astype(jnp.int32), xs, w_gu, w_down)
    yg = _sc_gather(yb, dest_t.reshape(n_assign // SC_GATHER_ROWS, SC_GATHER_ROWS))
    return _combine(yg, gw, x1, xm, gate2, w_sgu, w_sd, ln_g, ln_b, n_per_batch)


def kernel(x, c, ctx, c_ctx, w_ada, b_ada, w_in, conv_w, conv_b, gate_b, head_g, w_out, ln1_g, ln1_b,
           w_router, router_bias, w_expert_gu, w_expert_down, w_shared_gu, w_shared_down, ln2_g, ln2_b):
    bsz, n, d = x.shape
    depth = w_ada.shape[0]
    assert depth == 1
    t = bsz * n
    tables = _dft_tables()
    for l in range(depth):
        cvec = jnp.zeros((SUBLANES, d), F32).at[:bsz].set(c).at[bsz].set(c_ctx)
        mod = _ada(cvec, w_ada[l], b_ada[l][None, :])
        mods = mod.reshape(SUBLANES, 6, 1, d)
        mod_x = [mods[:bsz, j] for j in range(6)]
        mod_c = [mods[bsz:bsz + 1, j] for j in range(2)]

        wl = w_in[l]
        g0 = 4 * D_MLSTM
        w_in_k = jnp.concatenate(
            [wl[:, :g0 + N_GATE_COLS], jnp.zeros((d, LANES - N_GATE_COLS), F32),
             wl[:, g0 + N_GATE_COLS:]], axis=1).astype(BF16)
        conv_w9 = conv_w[l].reshape(9, 2 * D_MLSTM)
        conv_b_row = conv_b[l][None, :]
        gate_b_row = jnp.zeros((1, LANES), F32).at[0, :N_GATE_COLS].set(gate_b[l].reshape(-1))
        head_g_row = head_g[l][None, :]

        c0 = jnp.zeros((bsz, N_HEADS, 2, STATE_ROWS, D_HEAD), F32)
        m0 = jnp.zeros((bsz, N_HEADS, 2, SUBLANES, LANES), F32)
        _, _, c_ctx_fin, m_ctx_fin = _mixer_heads(ctx, mod_c[0], mod_c[1], w_in_k, conv_w9, conv_b_row,
                                                  gate_b_row, head_g_row, c0, m0, grid2d=False)
        f_in, mread, _, _ = _mixer_heads(x, mod_x[0], mod_x[1], w_in_k, conv_w9, conv_b_row,
                                      gate_b_row, head_g_row, c_ctx_fin, m_ctx_fin, grid2d=True)

        four = _fourier(f_in.reshape(bsz, DFT_R, DFT_R, D_FOURIER), tables).reshape(t, D_FOURIER)

        w_o = w_out[l].astype(BF16)
        w_r = jnp.zeros((d, LANES), F32).at[:, :N_EXPERTS].set(w_router[l]).astype(BF16)
        x2d = x.reshape(t, d)
        x1, xm, scores = _outproj(x2d, mread.reshape(t, D_MLSTM), four, w_o[:D_MLSTM], w_o[D_MLSTM:],
                                  mod_x[2], ln1_g[l][None, :], ln1_b[l][None, :], mod_x[3], mod_x[4],
                                  w_r, n)

        bias_rep = jnp.broadcast_to(router_bias[l][:, None], (N_EXPERTS, LANES))
        x = _moe(xm, x1, scores, bias_rep, mod_x[5], w_expert_gu[l], w_expert_down[l],
                 w_shared_gu[l].astype(BF16), w_shared_down[l].astype(BF16),
                 ln2_g[l][None, :], ln2_b[l][None, :], n).reshape(bsz, n, d)
    return x
```

```python
import functools

import numpy as np
import jax
import jax.numpy as jnp
from jax import lax
from jax.experimental import pallas as pl
from jax.experimental.pallas import tpu as pltpu
from jax.experimental.pallas import tpu_sc as plsc

F32 = jnp.float32
BF16 = jnp.bfloat16
HIGHEST = lax.Precision.HIGHEST

D_MODEL = 1024
GRID_W = 64
N_HEADS = 4
D_HEAD = 128
D_MLSTM = N_HEADS * D_HEAD
D_FOURIER = 512
FOURIER_GROUP = 64
N_GATE_COLS = 16
CHUNK = 128
N_EXPERTS = 64
TOP_K = 8
D_EXPERT = 256
ROUTED_SCALE = 2.5
DEEPNORM_ALPHA = 2.0 ** 0.25
LN_EPS = 1e-5

LANES = 128
SUBLANES = 8
VMEM_LIMIT = 48 * 1024 * 1024
FOUR_VMEM_LIMIT = 56 * 1024 * 1024

COL_QK = 0
COL_V = 2 * D_MLSTM
COL_O = 3 * D_MLSTM
COL_G = 4 * D_MLSTM
PROJ_W = COL_G + LANES

TN_ADA = 2048
TM = 1024
TM_ROUTE = 1024
ROW_BLK = 256
EXP_GROUP = 8
EXP_AHEAD = 10
EXP_IN_SLOTS = 32
EXP_OUT_SLOTS = 2 * EXP_GROUP
PACK_W = D_MODEL // 2
SC_ROWS = 128
SC_GATHER_ROWS = 64
CONV_PAD = 72
CONV_RB = 256
DFT_R = 64


def _cparams(sem, vmem=VMEM_LIMIT):
    return pltpu.CompilerParams(dimension_semantics=sem, vmem_limit_bytes=vmem)


def _layer_norm(x):
    mu = jnp.mean(x, axis=-1, keepdims=True)
    xc = x - mu
    var = jnp.mean(xc * xc, axis=-1, keepdims=True)
    return xc * lax.rsqrt(var + LN_EPS)


def _silu(x):
    return x * jax.nn.sigmoid(x)


def _pack_pair(lo, hi):
    as_bits = lambda u: lax.bitcast_convert_type(u.astype(BF16).astype(F32), jnp.uint32)
    word = as_bits(hi) | lax.shift_right_logical(as_bits(lo), jnp.uint32(16))
    return lax.bitcast_convert_type(word, jnp.int32)


def _unpack_pair_bf16(w):
    lo = lax.bitcast_convert_type(w.astype(jnp.int16), BF16)
    hi = lax.bitcast_convert_type(lax.shift_right_logical(w, jnp.int32(16)).astype(jnp.int16), BF16)
    return lo, hi


def _pack_rows(v):
    return _pack_pair(v[:, :PACK_W], v[:, PACK_W:])


def _unpack_bf16(w):
    return jnp.concatenate(_unpack_pair_bf16(w), axis=-1)


def _unpack_rows(w):
    u = lax.bitcast_convert_type(w, jnp.uint32)
    lo = lax.bitcast_convert_type(lax.shift_left(u, jnp.uint32(16)), F32)
    hi = lax.bitcast_convert_type(u & jnp.uint32(0xFFFF0000), F32)
    return jnp.concatenate([lo, hi], axis=1)


def _ada_kernel(c_ref, w_ref, b_ref, o_ref):
    s = _silu(c_ref[...]).astype(BF16)
    o_ref[...] = jnp.dot(s, w_ref[...].astype(BF16), preferred_element_type=F32) + b_ref[...]


def _ada(cvec, w_ada, b_ada):
    rows, d = cvec.shape
    n_out = w_ada.shape[1]
    tn = TN_ADA
    return pl.pallas_call(
        _ada_kernel,
        out_shape=jax.ShapeDtypeStruct((rows, n_out), F32),
        grid=(n_out // tn,),
        in_specs=[pl.BlockSpec((rows, d), lambda j: (0, 0)),
                  pl.BlockSpec((d, tn), lambda j: (0, j)),
                  pl.BlockSpec((1, tn), lambda j: (0, j))],
        out_specs=pl.BlockSpec((rows, tn), lambda j: (0, j)),
        compiler_params=_cparams(("arbitrary",)),
        name="ada",
    )(cvec, w_ada, b_ada)


def _inproj_kernel(x_ref, sh_ref, sc_ref, w_ref, y_ref, f_ref):
    half = x_ref.shape[0] // 2
    for h in range(2):
        r = slice(h * half, (h + 1) * half)
        u = _layer_norm(x_ref[r, :]) * (1.0 + sc_ref[0]) + sh_ref[0]
        proj = jnp.dot(u.astype(BF16), w_ref[...], preferred_element_type=F32)
        y_ref[r, :] = proj[:, :PROJ_W]
        f_ref[r, :] = proj[:, PROJ_W:]


def _inproj(x2d, shift, scale, w, n_per_batch):
    t, d = x2d.shape
    if shift.shape[0] == 1:
        tm = min(TM, t)
        per = t // tm
    else:
        tm = min(TM, n_per_batch)
        per = n_per_batch // tm
    return pl.pallas_call(
        _inproj_kernel,
        out_shape=(jax.ShapeDtypeStruct((t, PROJ_W), F32), jax.ShapeDtypeStruct((t, D_FOURIER), F32)),
        grid=(t // tm,),
        in_specs=[pl.BlockSpec((tm, d), lambda i: (i, 0)),
                  pl.BlockSpec((1, 1, d), lambda i: (i // per, 0, 0)),
                  pl.BlockSpec((1, 1, d), lambda i: (i // per, 0, 0)),
                  pl.BlockSpec((d, PROJ_W + D_FOURIER), lambda i: (0, 0))],
        out_specs=(pl.BlockSpec((tm, PROJ_W), lambda i: (i, 0)),
                   pl.BlockSpec((tm, D_FOURIER), lambda i: (i, 0))),
        compiler_params=_cparams(("arbitrary",)),
        name="inproj",
    )(x2d, shift, scale, w)


def _conv_kernel(y_ref, w_ref, b_ref, o_ref, pad_ref, *, n, grid2d, tiles):
    zeros = jnp.zeros((CONV_PAD, LANES), F32)
    pad_ref[0:CONV_PAD, :] = zeros
    pad_ref[CONV_PAD + n:CONV_PAD + n + CONV_PAD, :] = zeros
    rb = min(CONV_RB, n)
    col = lax.broadcasted_iota(jnp.int32, (rb, LANES), 0) % GRID_W
    not_first = col >= 1
    not_last = col <= GRID_W - 2
    row_taps = (0, 1, 2) if grid2d else (1,)
    for ct in range(tiles):
        lanes = slice(ct * LANES, (ct + 1) * LANES)
        pad_ref[CONV_PAD:CONV_PAD + n, :] = y_ref[0, :, lanes]
        k_scale = jnp.where(pl.program_id(1) * tiles + ct >= N_HEADS, D_HEAD ** -0.5, 1.0).astype(F32)
        bias = b_ref[:, lanes]
        for blk in range(n // rb):
            r0 = CONV_PAD + blk * rb
            acc = jnp.zeros((rb, LANES), F32)
            for dj in range(3):
                part = jnp.zeros((rb, LANES), F32)
                for di in row_taps:
                    off = (di - 1) * GRID_W + (dj - 1)
                    tap = w_ref[di * 3 + dj:di * 3 + dj + 1, lanes]
                    part = part + tap * pad_ref[r0 + off:r0 + off + rb, :]
                if grid2d and dj == 0:
                    part = jnp.where(not_first, part, 0.0)
                if grid2d and dj == 2:
                    part = jnp.where(not_last, part, 0.0)
                acc = acc + part
            val = _silu(acc + bias) * k_scale
            o_ref[0, blk * rb:(blk + 1) * rb, lanes] = val.astype(o_ref.dtype)


def _conv(y3d, conv_w9, conv_b, grid2d):
    b, n, _ = y3d.shape
    n_ct = 2 * D_MLSTM // LANES
    tiles = 1 if grid2d else n_ct
    w = tiles * LANES
    return pl.pallas_call(
        functools.partial(_conv_kernel, n=n, grid2d=grid2d, tiles=tiles),
        out_shape=jax.ShapeDtypeStruct((b, n, 2 * D_MLSTM), BF16),
        grid=(b, n_ct // tiles),
        in_specs=[pl.BlockSpec((1, n, w), lambda i, c: (i, 0, c)),
                  pl.BlockSpec((9, w), lambda i, c: (0, c)),
                  pl.BlockSpec((1, w), lambda i, c: (0, c))],
        out_specs=pl.BlockSpec((1, n, w), lambda i, c: (i, 0, c)),
        scratch_shapes=[pltpu.VMEM((n + 2 * CONV_PAD, LANES), F32)],
        compiler_params=_cparams(("arbitrary", "arbitrary")),
        name="conv2d" if grid2d else "conv1d",
    )(y3d, conv_w9, conv_b)


STATE_ROWS = D_HEAD + 32
GATE_A, GATE_CM, GATE_B = 0, 1, 2
GATE_GROUP = 32


def _gates_kernel(g_ref, gb_ref, rv_ref, *, nc):
    u_i = lax.broadcasted_iota(jnp.int32, (CHUNK, CHUNK), 0)
    t_i = lax.broadcasted_iota(jnp.int32, (CHUNK, CHUNK), 1)
    prefix = (u_i <= t_i).astype(F32)
    suffix = (u_i >= t_i).astype(F32)
    grp = min(GATE_GROUP, nc)
    rows = grp * SUBLANES
    lane = lax.broadcasted_iota(jnp.int32, (rows, LANES), 1)
    is_fwd = lax.broadcasted_iota(jnp.int32, (rows, LANES), 0) % SUBLANES < N_HEADS
    h = N_HEADS

    def running_max(a):
        fwd, bwd = a, a
        sh = 1
        while sh < CHUNK:
            fwd = jnp.where(lane >= sh, jnp.maximum(fwd, pltpu.roll(fwd, sh, axis=1)), fwd)
            bwd = jnp.where(lane < CHUNK - sh, jnp.maximum(bwd, pltpu.roll(bwd, CHUNK - sh, axis=1)), bwd)
            sh *= 2
        return jnp.where(is_fwd, fwd, bwd)

    def body(i, carry):
        c0 = i * grp
        r0 = pl.multiple_of(c0 * CHUNK, CHUNK)
        pre = g_ref[0, pl.ds(r0, grp * CHUNK), :] + gb_ref[...]
        pre_t = jnp.concatenate(
            [pre[j * CHUNK:(j + 1) * CHUNK].T[:N_GATE_COLS, :] for j in range(grp)], axis=0)
        lf_t = -(jnp.maximum(-pre_t, 0.0) + jnp.log1p(jnp.exp(-jnp.abs(pre_t))))
        cum_f = jnp.dot(lf_t, prefix, precision=HIGHEST, preferred_element_type=F32)
        cum_b = jnp.dot(lf_t, suffix, precision=HIGHEST, preferred_element_type=F32)
        pick = lambda t, j, q: t[j * N_GATE_COLS + q * h:j * N_GATE_COLS + (q + 1) * h]
        b8 = jnp.concatenate([x for j in range(grp) for x in (pick(cum_f, j, 1), pick(cum_b, j, 3))], axis=0)
        li8 = jnp.concatenate([x for j in range(grp) for x in (pick(pre_t, j, 0), pick(pre_t, j, 2))], axis=0)
        a8 = li8 - b8
        as_chunks = lambda t: t.reshape(grp, SUBLANES, LANES)
        rv_ref[0, pl.ds(c0, grp), GATE_A] = as_chunks(a8)
        rv_ref[0, pl.ds(c0, grp), GATE_CM] = as_chunks(running_max(a8))
        rv_ref[0, pl.ds(c0, grp), GATE_B] = as_chunks(b8)
        return carry

    lax.fori_loop(0, nc // grp, body, 0)


def _gates(y3d, gate_b_row):
    b, n, _ = y3d.shape
    nc = n // CHUNK
    return pl.pallas_call(
        functools.partial(_gates_kernel, nc=nc),
        out_shape=jax.ShapeDtypeStruct((b, nc, 3, SUBLANES, LANES), F32),
        grid=(b,),
        in_specs=[pl.BlockSpec((1, n, LANES), lambda i: (i, 0, COL_G // LANES)),
                  pl.BlockSpec((1, LANES), lambda i: (0, 0))],
        out_specs=pl.BlockSpec((1, nc, 3, SUBLANES, LANES), lambda i: (i, 0, 0, 0, 0)),
        compiler_params=_cparams(("arbitrary",)),
        name="gates",
    )(y3d, gate_b_row)


def _mlstm_kernel(q_ref, k_ref, v_ref, o_ref, rv_ref, c0_ref, m0_ref, hg_ref,
                  out_ref, cfin_ref, mfin_ref, hf_ref, hb_ref, cst_ref, mst_ref, *, nc, heads):
    cst_ref[...] = c0_ref[0]
    mst_ref[...] = m0_ref[0]
    s_i = lax.broadcasted_iota(jnp.int32, (CHUNK, CHUNK), 0)
    t_i = lax.broadcasted_iota(jnp.int32, (CHUNK, CHUNK), 1)
    ones_row = (s_i[:STATE_ROWS - D_HEAD] == 0).astype(F32)

    def one_chunk(c, d, mask_t, last, h_ref, hh):
        r0 = pl.multiple_of(c * CHUNK, CHUNK)
        lanes = slice(hh * D_HEAD, (hh + 1) * D_HEAD)
        g = d * N_HEADS + pl.program_id(1) * heads + hh
        gate_row = lambda tbl: rv_ref[0, pl.ds(c, 1), tbl, pl.ds(g, 1), :].reshape(1, LANES)
        a_row, cm_row, b_row = gate_row(GATE_A), gate_row(GATE_CM), gate_row(GATE_B)
        q_t = q_ref[0, pl.ds(r0, CHUNK), lanes].T
        k = k_ref[0, pl.ds(r0, CHUNK), lanes]
        v_t_aug = jnp.concatenate([v_ref[0, pl.ds(r0, CHUNK), lanes].T, ones_row], axis=0)
        m_prev = mst_ref[hh, d, 0:1, :]
        m_row = jnp.maximum(m_prev, cm_row)
        a_bc = jnp.broadcast_to(a_row, (CHUNK, CHUNK)).T
        d_t = jnp.where(mask_t, jnp.exp(a_bc - m_row), 0.0)
        s_t = jnp.dot(k, q_t, preferred_element_type=F32) * d_t
        cs = cst_ref[hh, d]
        rhs = jnp.concatenate(
            [s_t.astype(BF16), (q_t.astype(F32) * jnp.exp(m_prev - m_row)).astype(BF16)], axis=0)
        lhs = jnp.concatenate([v_t_aug.astype(BF16), cs.astype(BF16)], axis=1)
        h_t = jnp.dot(lhs, rhs, preferred_element_type=F32)
        den = h_t[D_HEAD:D_HEAD + 1, :]
        floor = jnp.exp(-b_row - m_row)
        h_ref[hh, pl.ds(c, 1)] = (h_t[:D_HEAD] / jnp.maximum(jnp.abs(den), floor))[None]
        m_last = m_row[:, last:last + 1]
        wv_t = (v_t_aug * jnp.exp(a_row - m_last)).astype(BF16)
        cst_ref[hh, d] = jnp.exp(m_prev - m_last) * cs + jnp.dot(wv_t, k, preferred_element_type=F32)
        mst_ref[hh, d] = jnp.broadcast_to(b_row[:, last:last + 1] + m_last, (SUBLANES, LANES))

    def body(c, carry):
        for hh in range(heads):
            one_chunk(c, 0, s_i <= t_i, CHUNK - 1, hf_ref, hh)
            one_chunk(nc - 1 - c, 1, s_i >= t_i, 0, hb_ref, hh)
        return carry

    lax.fori_loop(0, nc, body, 0, unroll=min(16 // heads, nc))
    cfin_ref[0] = cst_ref[...]
    mfin_ref[0] = mst_ref[...]

    def readout(c, carry):
        r0 = pl.multiple_of(c * CHUNK, CHUNK)
        for hh in range(heads):
            lanes = slice(hh * D_HEAD, (hh + 1) * D_HEAD)
            h_t = hf_ref[hh, pl.ds(c, 1)][0] + hb_ref[hh, pl.ds(c, 1)][0]
            hc = h_t - jnp.mean(h_t, axis=0, keepdims=True)
            hn = (hc * lax.rsqrt(jnp.mean(hc * hc, axis=0, keepdims=True) + LN_EPS)).T
            gate = jax.nn.sigmoid(o_ref[0, pl.ds(r0, CHUNK), lanes])
            out_ref[0, pl.ds(r0, CHUNK), lanes] = (hn * hg_ref[:, lanes] * gate).astype(out_ref.dtype)
        return carry

    lax.fori_loop(0, nc, readout, 0, unroll=min(16, nc))


def _mlstm(qk, y3d, rowv, c0, m0, head_g):
    b, n, _ = qk.shape
    nc = n // CHUNK
    heads = N_HEADS if nc <= 4 else 1
    w = heads * D_HEAD
    v_blk = COL_V // w
    o_blk = COL_O // w
    tok = lambda off: pl.BlockSpec((1, n, w), lambda i, h: (i, 0, off + h))
    state_c = pl.BlockSpec((1, heads, 2, STATE_ROWS, D_HEAD), lambda i, h: (i, h, 0, 0, 0))
    state_m = pl.BlockSpec((1, heads, 2, SUBLANES, LANES), lambda i, h: (i, h, 0, 0, 0))
    return pl.pallas_call(
        functools.partial(_mlstm_kernel, nc=nc, heads=heads),
        out_shape=(jax.ShapeDtypeStruct((b, n, D_MLSTM), BF16),
                   jax.ShapeDtypeStruct((b, N_HEADS, 2, STATE_ROWS, D_HEAD), F32),
                   jax.ShapeDtypeStruct((b, N_HEADS, 2, SUBLANES, LANES), F32)),
        grid=(b, N_HEADS // heads),
        in_specs=[tok(0), tok(D_MLSTM // w), tok(v_blk), tok(o_blk),
                  pl.BlockSpec((1, nc, 3, SUBLANES, LANES), lambda i, h: (i, 0, 0, 0, 0)),
                  state_c, state_m,
                  pl.BlockSpec((1, w), lambda i, h: (0, h))],
        out_specs=(pl.BlockSpec((1, n, w), lambda i, h: (i, 0, h)), state_c, state_m),
        scratch_shapes=[pltpu.VMEM((heads, nc, D_HEAD, CHUNK), F32), pltpu.VMEM((heads, nc, D_HEAD, CHUNK), F32),
                        pltpu.VMEM((heads, 2, STATE_ROWS, D_HEAD), F32),
                        pltpu.VMEM((heads, 2, SUBLANES, LANES), F32)],
        compiler_params=_cparams(("arbitrary", "arbitrary")),
        name="mlstm",
    )(qk, qk, y3d, y3d, rowv, c0, m0, head_g)


def _dft_tables():
    r = DFT_R
    idx = np.arange(r)
    cg = np.cos(2 * np.pi * np.outer(idx, idx) / FOURIER_GROUP)
    sg = np.sin(2 * np.pi * np.outer(idx, idx) / FOURIER_GROUP)
    n_grp = D_FOURIER // FOURIER_GROUP
    bdc = np.kron(np.eye(n_grp), cg) / 8.0
    bds = np.kron(np.eye(n_grp), sg) / 8.0
    k1 = idx[:, None]
    a = idx[None, :]
    m1 = np.zeros((r, 2 * r, 2 * r))
    for b2 in range(r):
        th = 2 * np.pi * k1 * (r * a + b2) / (r * r)
        ec, es = np.cos(th), np.sin(th)
        m1[b2] = np.block([[ec, -es], [es, ec]]) / 8.0
    th2 = 2 * np.pi * np.outer(idx, idx) / r
    m2 = np.concatenate([np.cos(th2), -np.sin(th2)], axis=1) / 8.0
    as_bf16 = lambda t: jnp.asarray(t, dtype=F32).astype(BF16)
    return as_bf16(bdc), as_bf16(bds), as_bf16(m1), as_bf16(m2)


FOUR_BB = 8


N_SLABS = D_FOURIER // LANES


def _to_slabs(slab_ref, val):
    for s in range(N_SLABS):
        slab_ref[s] = val[:, s * LANES:(s + 1) * LANES]


def _strided_rows(slab_ref, start):
    return jnp.concatenate(
        [slab_ref[s, pl.ds(start, DFT_R, stride=FOUR_BB), :] for s in range(N_SLABS)], axis=1)


def _four_kernel(f_ref, bdc_ref, bds_ref, m1_ref, m2_ref, o_ref, z_ref, ps_ref, qs_ref, zs_ref, os_ref):
    r = DFT_R

    def stage1(blk, carry):
        b0 = pl.multiple_of(blk * FOUR_BB, FOUR_BB)
        fb = f_ref[0, :, pl.ds(b0, FOUR_BB), :].reshape(r * FOUR_BB, D_FOURIER).astype(BF16)
        _to_slabs(ps_ref, jnp.dot(fb, bdc_ref[...], preferred_element_type=F32))
        _to_slabs(qs_ref, jnp.dot(fb, bds_ref[...], preferred_element_type=F32))
        for j in range(FOUR_BB):
            x2 = jnp.concatenate([_strided_rows(ps_ref, j), _strided_rows(qs_ref, j)], axis=0).astype(BF16)
            zz = jnp.dot(m1_ref[b0 + j], x2, preferred_element_type=F32)
            z_ref[b0 + j] = _pack_pair(zz[:r], zz[r:])
        return carry

    lax.fori_loop(0, r // FOUR_BB, stage1, 0)

    def stage2(blk, carry):
        k0 = pl.multiple_of(blk * FOUR_BB, FOUR_BB)
        _to_slabs(zs_ref, z_ref[:, pl.ds(k0, FOUR_BB), :].reshape(r * FOUR_BB, D_FOURIER))
        for j in range(FOUR_BB):
            zc, zs = _unpack_pair_bf16(_strided_rows(zs_ref, j))
            o = jnp.dot(m2_ref[...], jnp.concatenate([zc, zs], axis=0), preferred_element_type=F32)
            for s in range(N_SLABS):
                os_ref[s, pl.ds(j, r, stride=FOUR_BB), :] = o[:, s * LANES:(s + 1) * LANES]
        for s in range(N_SLABS):
            o_ref[0, :, pl.ds(k0, FOUR_BB), s * LANES:(s + 1) * LANES] = os_ref[s].reshape(r, FOUR_BB, LANES)
        return carry

    lax.fori_loop(0, r // FOUR_BB, stage2, 0)


def _fourier(y4, tables):
    bdc, bds, m1, m2 = tables
    b = y4.shape[0]
    r = DFT_R
    slab = lambda dt: pltpu.VMEM((N_SLABS, r * FOUR_BB, LANES), dt)
    const = lambda *shape: pl.BlockSpec(shape, lambda i: (0,) * len(shape))
    return pl.pallas_call(
        _four_kernel,
        out_shape=jax.ShapeDtypeStruct((b, r, r, D_FOURIER), F32),
        grid=(b,),
        in_specs=[pl.BlockSpec((1, r, r, D_FOURIER), lambda i: (i, 0, 0, 0)),
                  const(D_FOURIER, D_FOURIER), const(D_FOURIER, D_FOURIER),
                  const(r, 2 * r, 2 * r), const(r, 2 * r)],
        out_specs=pl.BlockSpec((1, r, r, D_FOURIER), lambda i: (i, 0, 0, 0)),
        scratch_shapes=[pltpu.VMEM((r, r, D_FOURIER), jnp.int32), slab(F32), slab(F32), slab(jnp.int32), slab(F32)],
        compiler_params=_cparams(("arbitrary",), vmem=FOUR_VMEM_LIMIT),
        name="fourier",
    )(y4, bdc, bds, m1, m2)


def _outproj_kernel(x_ref, mr_ref, fo_ref, wm_ref, wf_ref, g1_ref, lg_ref, lb_ref,
                    sh_ref, sc_ref, wr_ref, x1_ref, xm_ref, s_ref):
    half = x_ref.shape[0] // 2
    for h in range(2):
        r = slice(h * half, (h + 1) * half)
        mix = (jnp.dot(mr_ref[r, :], wm_ref[...], preferred_element_type=F32)
               + jnp.dot(fo_ref[r, :].astype(BF16), wf_ref[...], preferred_element_type=F32))
        x1 = _layer_norm(DEEPNORM_ALPHA * x_ref[r, :] + g1_ref[0] * mix) * lg_ref[...] + lb_ref[...]
        x1_ref[r, :] = x1
        xm = (_layer_norm(x1) * (1.0 + sc_ref[0]) + sh_ref[0]).astype(BF16)
        xm_ref[r, :] = _pack_rows(xm)
        logits = jnp.dot(xm, wr_ref[...], preferred_element_type=F32)
        s_ref[r, :] = jax.nn.sigmoid(logits)


def _outproj(x2d, mread, four, w_m, w_f, gate1, ln_g, ln_b, shift2, scale2, w_router, n_per_batch):
    t, d = x2d.shape
    per = n_per_batch // TM
    tokd = lambda w: pl.BlockSpec((TM, w), lambda i: (i, 0))
    full = lambda r, c: pl.BlockSpec((r, c), lambda i: (0, 0))
    mod = pl.BlockSpec((1, 1, d), lambda i: (i // per, 0, 0))
    return pl.pallas_call(
        _outproj_kernel,
        out_shape=(jax.ShapeDtypeStruct((t, d), F32),
                   jax.ShapeDtypeStruct((t, PACK_W), jnp.int32),
                   jax.ShapeDtypeStruct((t, LANES), F32)),
        grid=(t // TM,),
        in_specs=[tokd(d), tokd(D_MLSTM), tokd(D_FOURIER), full(D_MLSTM, d), full(D_FOURIER, d),
                  mod, full(1, d), full(1, d), mod, mod, full(d, LANES)],
        out_specs=(tokd(d), tokd(PACK_W), tokd(LANES)),
        compiler_params=_cparams(("arbitrary",)),
        name="outproj",
    )(x2d, mread, four, w_m, w_f, gate1, ln_g, ln_b, shift2, scale2, w_router)


def _route_kernel(s_ref, b_ref, idx_ref, gw_ref, rank_ref, cnt_ref, carry_ref):
    @pl.when(pl.program_id(0) == 0)
    def _():
        carry_ref[...] = jnp.zeros_like(carry_ref)

    tm = s_ref.shape[0]
    reps = tm // LANES
    s_t = s_ref[...].T[:N_EXPERTS, :]
    e_i = lax.broadcasted_iota(jnp.int32, (N_EXPERTS, tm), 0).astype(F32)
    neg_inf = jnp.float32(-jnp.inf)
    sb = s_t + jnp.tile(b_ref[...], (1, reps))
    picks, sels = [], []
    taken = jnp.zeros((N_EXPERTS, tm), F32)
    for _ in range(TOP_K):
        best = jnp.max(sb, axis=0, keepdims=True)
        pick = jnp.min(jnp.where(sb == best, e_i, float(N_EXPERTS)), axis=0, keepdims=True)
        onehot = e_i == pick
        sels.append(jnp.sum(jnp.where(onehot, s_t, 0.0), axis=0, keepdims=True))
        picks.append(pick)
        sb = jnp.where(onehot, neg_inf, sb)
        taken = taken + onehot.astype(F32)
    u_i = lax.broadcasted_iota(jnp.int32, (tm, tm), 0)
    t_i = lax.broadcasted_iota(jnp.int32, (tm, tm), 1)
    before = (u_i < t_i).astype(BF16)
    taken_bf = taken.astype(BF16)
    rank_all = jnp.tile(carry_ref[...], (1, reps)) + jnp.dot(taken_bf, before, preferred_element_type=F32)
    carry_ref[...] = carry_ref[...] + jnp.dot(taken_bf, jnp.ones((tm, LANES), BF16),
                                              preferred_element_type=F32)
    cnt_ref[...] = carry_ref[...]
    total = sels[0]
    for v in sels[1:]:
        total = total + v
    ranks = [jnp.sum(jnp.where(e_i == picks[j], rank_all, 0.0), axis=0, keepdims=True) for j in range(TOP_K)]
    idx_ref[...] = jnp.concatenate(picks, axis=0).astype(jnp.int32)
    rank_ref[...] = jnp.concatenate(ranks, axis=0).astype(jnp.int32)
    gw_t = jnp.concatenate([ROUTED_SCALE * v / total for v in sels]
                           + [jnp.zeros((LANES - TOP_K, tm), F32)], axis=0)
    gw_ref[...] = gw_t.T[:, :TOP_K]


def _route(scores, bias_rep):
    t = scores.shape[0]
    tok = lambda w: pl.BlockSpec((TM_ROUTE, w), lambda i: (i, 0))
    slot_major = pl.BlockSpec((TOP_K, TM_ROUTE), lambda i: (0, i))
    per_expert = pl.BlockSpec((N_EXPERTS, LANES), lambda i: (0, 0))
    return pl.pallas_call(
        _route_kernel,
        out_shape=(jax.ShapeDtypeStruct((TOP_K, t), jnp.int32),
                   jax.ShapeDtypeStruct((t, TOP_K), F32),
                   jax.ShapeDtypeStruct((TOP_K, t), jnp.int32),
                   jax.ShapeDtypeStruct((N_EXPERTS, LANES), F32)),
        grid=(t // TM_ROUTE,),
        in_specs=[tok(LANES), per_expert],
        out_specs=(slot_major, tok(TOP_K), slot_major, per_expert),
        scratch_shapes=[pltpu.VMEM((N_EXPERTS, LANES), F32)],
        compiler_params=_cparams(("arbitrary",)),
        name="route",
    )(scores, bias_rep)


TN_DEST = 2048


def _dest_kernel(ps_ref, idx_ref, rank_ref, dest_ref):
    idx = idx_ref[...]
    start = jnp.zeros(idx.shape, jnp.int32)
    for e in range(N_EXPERTS):
        start = jnp.where(idx == e, ps_ref[e], start)
    dest_ref[...] = start + rank_ref[...]


def _dest(pad_start, idx_t, rank_t):
    t = idx_t.shape[1]
    blk = lambda: pl.BlockSpec((TOP_K, TN_DEST), lambda i, ps: (0, i))
    grid_spec = pltpu.PrefetchScalarGridSpec(
        num_scalar_prefetch=1, grid=(t // TN_DEST,), in_specs=[blk(), blk()], out_specs=blk())
    return pl.pallas_call(
        _dest_kernel,
        out_shape=jax.ShapeDtypeStruct((TOP_K, t), jnp.int32),
        grid_spec=grid_spec,
        compiler_params=_cparams(("arbitrary",)),
        name="dest",
    )(pad_start, idx_t, rank_t)


def _sc_worker_id(info):
    return lax.axis_index("s") * info.num_cores + lax.axis_index("c")


def _sc_dispatch(xm, dest3, n_rows):
    t, w = xm.shape
    info = plsc.get_sparse_core_info()
    n_workers = info.num_cores * info.num_subcores
    per_worker = t // SC_ROWS // n_workers
    mesh = plsc.VectorSubcoreMesh(core_axis_name="c", subcore_axis_name="s")

    @functools.partial(
        pl.kernel, mesh=mesh,
        out_type=jax.ShapeDtypeStruct((n_rows, w), xm.dtype),
        scratch_types=[pltpu.VMEM((TOP_K, SC_ROWS), jnp.int32), pltpu.VMEM((SC_ROWS, w), xm.dtype)],
        name="sc_dispatch",
    )
    def run(xm_hbm, dest_hbm, xs_hbm, idx_v, rows_v):
        first = _sc_worker_id(info) * per_worker

        @pl.loop(0, per_worker)
        def _(ci):
            chunk = first + ci
            pltpu.sync_copy(dest_hbm.at[chunk], idx_v)
            pltpu.sync_copy(xm_hbm.at[pl.ds(chunk * SC_ROWS, SC_ROWS)], rows_v)
            for j in range(TOP_K):
                pltpu.sync_copy(rows_v, xs_hbm.at[idx_v.at[j]])

    return run(xm, dest3)


def _sc_gather(table, idx2):
    n_chunks, rows = idx2.shape
    w = table.shape[1]
    info = plsc.get_sparse_core_info()
    n_workers = info.num_cores * info.num_subcores
    per_worker = n_chunks // n_workers
    assert per_worker % 2 == 0
    mesh = plsc.VectorSubcoreMesh(core_axis_name="c", subcore_axis_name="s")

    @functools.partial(
        pl.kernel, mesh=mesh,
        out_type=jax.ShapeDtypeStruct((n_chunks * rows, w), table.dtype),
        scratch_types=[pltpu.VMEM((2, rows), jnp.int32), pltpu.VMEM((2, rows, w), table.dtype),
                       pltpu.SemaphoreType.DMA, pltpu.SemaphoreType.DMA],
        name="sc_gather",
    )
    def run(table_hbm, idx_hbm, out_hbm, idx_v, rows_v, sem0, sem1):
        first = _sc_worker_id(info) * per_worker
        sems = (sem0, sem1)

        def gather(b):
            return pltpu.make_async_copy(table_hbm.at[idx_v.at[b]], rows_v.at[b], sems[b])

        def start(chunk, b):
            pltpu.sync_copy(idx_hbm.at[chunk], idx_v.at[b])
            gather(b).start()

        def finish(chunk, b):
            gather(b).wait()
            pltpu.sync_copy(rows_v.at[b], out_hbm.at[pl.ds(chunk * rows, rows)])

        start(first, 0)

        @pl.loop(0, per_worker, step=2)
        def _(ci):
            chunk = first + ci
            start(chunk + 1, 1)
            finish(chunk, 0)

            @pl.when(ci + 2 < per_worker)
            def _():
                start(chunk + 2, 0)

            finish(chunk + 1, 1)

    return run(table, idx2)


def _expert_kernel(first_ref, count_ref, used_ref,
                   xs_hbm, wgu_ref, wd_ref, y_hbm, wgu_bf, wd_bf, xbuf, ybuf, in_sem, out_sem):
    e = pl.program_id(0)
    n_used = used_ref[0]
    first = first_ref[e]
    n_blocks = count_ref[e]

    def rows(g):
        return pl.ds(pl.multiple_of(g * ROW_BLK, ROW_BLK), ROW_BLK)

    def read(g):
        slot = g % EXP_IN_SLOTS
        return pltpu.make_async_copy(xs_hbm.at[rows(g), :], xbuf.at[slot], in_sem.at[slot])

    def write(g):
        slot = g % EXP_OUT_SLOTS
        return pltpu.make_async_copy(ybuf.at[slot], y_hbm.at[rows(g), :], out_sem.at[slot])

    @pl.when(e == 0)
    def _():
        for g0 in range(EXP_AHEAD):
            @pl.when(g0 < n_used)
            def _():
                read(g0).start()

    @pl.when(n_blocks > 0)
    def _():
        wgu_bf[...] = wgu_ref[0].astype(BF16)
        wd_bf[...] = wd_ref[0].astype(BF16)

    def acquire(g):
        @pl.when(g + EXP_AHEAD < n_used)
        def _():
            read(g + EXP_AHEAD).start()

        read(g).wait()

        @pl.when(g >= EXP_OUT_SLOTS)
        def _():
            write(g - EXP_OUT_SLOTS).wait()

    def gate_up(g):
        return jnp.dot(_unpack_bf16(xbuf[g % EXP_IN_SLOTS]), wgu_bf[...], preferred_element_type=F32)

    def down(g, gu):
        act = _silu(gu[:, :D_EXPERT]) * gu[:, D_EXPERT:]
        yv = jnp.dot(act.astype(BF16), wd_bf[...], preferred_element_type=F32)
        words = pltpu.pack_elementwise([yv[:, :PACK_W], yv[:, PACK_W:]], packed_dtype=BF16)
        ybuf[g % EXP_OUT_SLOTS] = lax.bitcast_convert_type(words, jnp.int32)

    def handle(g, n):
        for i in range(n):
            acquire(g + i)
        gu = gate_up(g)
        for i in range(n):
            gu_next = gate_up(g + i + 1) if i + 1 < n else None
            down(g + i, gu)
            gu = gu_next
        for i in range(n):
            write(g + i).start()

    def group(k, carry):
        handle(first + EXP_GROUP * k, EXP_GROUP)
        return carry

    lax.fori_loop(0, n_blocks // EXP_GROUP, group, 0)

    n = EXP_GROUP // 2
    while n >= 1:
        @pl.when(n_blocks % (2 * n) >= n)
        def _(n=n):
            handle(first + n_blocks - n_blocks % (2 * n), n)
        n //= 2

    @pl.when(e == pl.num_programs(0) - 1)
    def _():
        for back in range(EXP_OUT_SLOTS, 0, -1):
            @pl.when(n_used >= back)
            def _():
                write(n_used - back).wait()


def _experts(first_block, n_blocks, n_used, xs, w_gu, w_down):
    n_rows, w = xs.shape
    d = D_MODEL
    grid_spec = pltpu.PrefetchScalarGridSpec(
        num_scalar_prefetch=3,
        grid=(N_EXPERTS,),
        in_specs=[pl.BlockSpec(memory_space=pl.ANY),
                  pl.BlockSpec((1, d, 2 * D_EXPERT), lambda e, *_: (e, 0, 0)),
                  pl.BlockSpec((1, D_EXPERT, d), lambda e, *_: (e, 0, 0))],
        out_specs=pl.BlockSpec(memory_space=pl.ANY),
        scratch_shapes=[pltpu.VMEM((d, 2 * D_EXPERT), BF16), pltpu.VMEM((D_EXPERT, d), BF16),
                        pltpu.VMEM((EXP_IN_SLOTS, ROW_BLK, w), jnp.int32),
                        pltpu.VMEM((EXP_OUT_SLOTS, ROW_BLK, w), jnp.int32),
                        pltpu.SemaphoreType.DMA((EXP_IN_SLOTS,)),
                        pltpu.SemaphoreType.DMA((EXP_OUT_SLOTS,))],
    )
    return pl.pallas_call(
        _expert_kernel,
        out_shape=jax.ShapeDtypeStruct((n_rows, w), jnp.int32),
        grid_spec=grid_spec,
        compiler_params=_cparams(("arbitrary",)),
        name="experts",
    )(first_block, n_blocks, n_used, xs, w_gu, w_down)


TM_COMBINE = 512


def _combine_kernel(*refs):
    yg_refs = refs[:TOP_K]
    gw_ref, x1_ref, xm_ref, g2_ref, wsgu_ref, wsd_ref, lg_ref, lb_ref, out_ref = refs[TOP_K:]
    gu = jnp.dot(_unpack_bf16(xm_ref[...]), wsgu_ref[...], preferred_element_type=F32)
    act = _silu(gu[:, :D_EXPERT]) * gu[:, D_EXPERT:]
    acc = jnp.dot(act.astype(BF16), wsd_ref[...], preferred_element_type=F32)
    gw = gw_ref[...]
    for j in range(TOP_K):
        acc = acc + gw[:, j:j + 1] * _unpack_rows(yg_refs[j][...])
    z = DEEPNORM_ALPHA * x1_ref[...] + g2_ref[0] * acc
    out_ref[...] = _layer_norm(z) * lg_ref[...] + lb_ref[...]


def _combine(yg, gw, x1, xm, gate2, w_sgu, w_sd, ln_g, ln_b, n_per_batch):
    t, d = x1.shape
    tm = TM_COMBINE
    per = n_per_batch // tm
    steps = t // tm
    tok = lambda w: pl.BlockSpec((tm, w), lambda i: (i, 0))
    full = lambda r, c: pl.BlockSpec((r, c), lambda i: (0, 0))
    slot = lambda j: pl.BlockSpec((tm, PACK_W), lambda i: (j * steps + i, 0))
    return pl.pallas_call(
        _combine_kernel,
        out_shape=jax.ShapeDtypeStruct((t, d), F32),
        grid=(steps,),
        in_specs=[slot(j) for j in range(TOP_K)]
        + [tok(TOP_K), tok(d), tok(PACK_W), pl.BlockSpec((1, 1, d), lambda i: (i // per, 0, 0)),
           full(d, 2 * D_EXPERT), full(D_EXPERT, d), full(1, d), full(1, d)],
        out_specs=tok(d),
        compiler_params=_cparams(("arbitrary",)),
        name="combine",
    )(*([yg] * TOP_K), gw, x1, xm, gate2, w_sgu, w_sd, ln_g, ln_b)


def _mixer_heads(stream, shift, scale, w_in_k, conv_w9, conv_b, gate_b_row, head_g, c0, m0, grid2d):
    b, n, d = stream.shape
    y, f_in = _inproj(stream.reshape(b * n, d), shift, scale, w_in_k, n)
    y = y.reshape(b, n, PROJ_W)
    qk = _conv(y, conv_w9, conv_b, grid2d)
    mread, c_fin, m_fin = _mlstm(qk, y, _gates(y, gate_b_row), c0, m0, head_g)
    return f_in, mread, c_fin, m_fin


def _moe(xm, x1, scores, bias_row, gate2, w_gu, w_down, w_sgu, w_sd, ln_g, ln_b, n_per_batch):
    t = xm.shape[0]
    n_assign = t * TOP_K
    n_rows = n_assign + N_EXPERTS * ROW_BLK
    idx_t, gw, rank_t, counts = _route(scores, bias_row)
    sizes = counts[:, 0].astype(jnp.int32)
    padded = (sizes + ROW_BLK - 1) // ROW_BLK * ROW_BLK
    pad_end = jnp.cumsum(padded)
    pad_start = (pad_end - padded).astype(jnp.int32)
    dest_t = _dest(pad_start, idx_t, rank_t)
    dest3 = dest_t.reshape(TOP_K, t // SC_ROWS, SC_ROWS).transpose(1, 0, 2)

    xs = _sc_dispatch(xm, dest3, n_rows)
    yb = _experts(pad_start // ROW_BLK, (padded // ROW_BLK).astype(jnp.int32),
                  (pad_end[-1:] // ROW_BLK).astype(jnp.int32), xs, w_gu, w_down)
    yg = _sc_gather(yb, dest_t.reshape(n_assign // SC_GATHER_ROWS, SC_GATHER_ROWS))
    return _combine(yg, gw, x1, xm, gate2, w_sgu, w_sd, ln_g, ln_b, n_per_batch)


def kernel(x, c, ctx, c_ctx, w_ada, b_ada, w_in, conv_w, conv_b, gate_b, head_g, w_out, ln1_g, ln1_b,
           w_router, router_bias, w_expert_gu, w_expert_down, w_shared_gu, w_shared_down, ln2_g, ln2_b):
    bsz, n, d = x.shape
    depth = w_ada.shape[0]
    assert depth == 1
    t = bsz * n
    tables = _dft_tables()
    for l in range(depth):
        cvec = jnp.zeros((SUBLANES, d), F32).at[:bsz].set(c).at[bsz].set(c_ctx)
        mod = _ada(cvec, w_ada[l], b_ada[l][None, :])
        mods = mod.reshape(SUBLANES, 6, 1, d)
        mod_x = [mods[:bsz, j] for j in range(6)]
        mod_c = [mods[bsz:bsz + 1, j] for j in range(2)]

        wl = w_in[l]
        g0 = 4 * D_MLSTM
        w_in_k = jnp.concatenate(
            [wl[:, :g0 + N_GATE_COLS], jnp.zeros((d, LANES - N_GATE_COLS), F32),
             wl[:, g0 + N_GATE_COLS:]], axis=1).astype(BF16)
        conv_w9 = conv_w[l].reshape(9, 2 * D_MLSTM)
        conv_b_row = conv_b[l][None, :]
        gate_b_row = jnp.zeros((1, LANES), F32).at[0, :N_GATE_COLS].set(gate_b[l].reshape(-1))
        head_g_row = head_g[l][None, :]

        c0 = jnp.zeros((bsz, N_HEADS, 2, STATE_ROWS, D_HEAD), F32)
        m0 = jnp.zeros((bsz, N_HEADS, 2, SUBLANES, LANES), F32)
        _, _, c_ctx_fin, m_ctx_fin = _mixer_heads(ctx, mod_c[0], mod_c[1], w_in_k, conv_w9, conv_b_row,
                                                  gate_b_row, head_g_row, c0, m0, grid2d=False)
        f_in, mread, _, _ = _mixer_heads(x, mod_x[0], mod_x[1], w_in_k, conv_w9, conv_b_row,
                                      gate_b_row, head_g_row, c_ctx_fin, m_ctx_fin, grid2d=True)

        four = _fourier(f_in.reshape(bsz, DFT_R, DFT_R, D_FOURIER), tables).reshape(t, D_FOURIER)

        w_o = w_out[l].astype(BF16)
        w_r = jnp.zeros((d, LANES), F32).at[:, :N_EXPERTS].set(w_router[l]).astype(BF16)
        x2d = x.reshape(t, d)
        x1, xm, scores = _outproj(x2d, mread.reshape(t, D_MLSTM), four, w_o[:D_MLSTM], w_o[D_MLSTM:],
                                  mod_x[2], ln1_g[l][None, :], ln1_b[l][None, :], mod_x[3], mod_x[4],
                                  w_r, n)

        bias_rep = jnp.broadcast_to(router_bias[l][:, None], (N_EXPERTS, LANES))
        x = _moe(xm, x1, scores, bias_rep, mod_x[5], w_expert_gu[l], w_expert_down[l],
                 w_shared_gu[l].astype(BF16), w_shared_down[l].astype(BF16),
                 ln2_g[l][None, :], ln2_b[l][None, :], n).reshape(bsz, n, d)
    return x
```

```python
import functools

import numpy as np
import jax
import jax.numpy as jnp
from jax import lax
from jax.experimental import pallas as pl
from jax.experimental.pallas import tpu as pltpu
from jax.experimental.pallas import tpu_sc as plsc

F32 = jnp.float32
BF16 = jnp.bfloat16
HIGHEST = lax.Precision.HIGHEST

D_MODEL = 1024
GRID_W = 64
N_HEADS = 4
D_HEAD = 128
D_MLSTM = N_HEADS * D_HEAD
D_FOURIER = 512
FOURIER_GROUP = 64
N_GATE_COLS = 16
CHUNK = 128
N_EXPERTS = 64
TOP_K = 8
D_EXPERT = 256
ROUTED_SCALE = 2.5
DEEPNORM_ALPHA = 2.0 ** 0.25
LN_EPS = 1e-5

LANES = 128
SUBLANES = 8
VMEM_LIMIT = 48 * 1024 * 1024
FOUR_VMEM_LIMIT = 56 * 1024 * 1024

COL_QK = 0
COL_V = 2 * D_MLSTM
COL_O = 3 * D_MLSTM
COL_G = 4 * D_MLSTM
PROJ_W = COL_G + LANES

TN_ADA = 2048
TM = 1024
TM_ROUTE = 1024
ROW_BLK = 256
EXP_GROUP = 8
EXP_AHEAD = 10
EXP_IN_SLOTS = 32
EXP_OUT_SLOTS = 2 * EXP_GROUP
PACK_W = D_MODEL // 2
SC_ROWS = 128
SC_GATHER_ROWS = 64
CONV_PAD = 72
CONV_RB = 256
DFT_R = 64


def _cparams(sem, vmem=VMEM_LIMIT):
    return pltpu.CompilerParams(dimension_semantics=sem, vmem_limit_bytes=vmem)


def _layer_norm(x):
    mu = jnp.mean(x, axis=-1, keepdims=True)
    xc = x - mu
    var = jnp.mean(xc * xc, axis=-1, keepdims=True)
    return xc * lax.rsqrt(var + LN_EPS)


def _silu(x):
    return x * jax.nn.sigmoid(x)


def _pack_pair(lo, hi):
    as_bits = lambda u: lax.bitcast_convert_type(u.astype(BF16).astype(F32), jnp.uint32)
    word = as_bits(hi) | lax.shift_right_logical(as_bits(lo), jnp.uint32(16))
    return lax.bitcast_convert_type(word, jnp.int32)


def _unpack_pair_bf16(w):
    lo = lax.bitcast_convert_type(w.astype(jnp.int16), BF16)
    hi = lax.bitcast_convert_type(lax.shift_right_logical(w, jnp.int32(16)).astype(jnp.int16), BF16)
    return lo, hi


def _pack_rows(v):
    return _pack_pair(v[:, :PACK_W], v[:, PACK_W:])


def _unpack_bf16(w):
    return jnp.concatenate(_unpack_pair_bf16(w), axis=-1)


def _unpack_rows(w):
    u = lax.bitcast_convert_type(w, jnp.uint32)
    lo = lax.bitcast_convert_type(lax.shift_left(u, jnp.uint32(16)), F32)
    hi = lax.bitcast_convert_type(u & jnp.uint32(0xFFFF0000), F32)
    return jnp.concatenate([lo, hi], axis=1)


def _ada_kernel(c_ref, w_ref, b_ref, o_ref):
    s = _silu(c_ref[...]).astype(BF16)
    o_ref[...] = jnp.dot(s, w_ref[...].astype(BF16), preferred_element_type=F32) + b_ref[...]


def _ada(cvec, w_ada, b_ada):
    rows, d = cvec.shape
    n_out = w_ada.shape[1]
    tn = TN_ADA
    return pl.pallas_call(
        _ada_kernel,
        out_shape=jax.ShapeDtypeStruct((rows, n_out), F32),
        grid=(n_out // tn,),
        in_specs=[pl.BlockSpec((rows, d), lambda j: (0, 0)),
                  pl.BlockSpec((d, tn), lambda j: (0, j)),
                  pl.BlockSpec((1, tn), lambda j: (0, j))],
        out_specs=pl.BlockSpec((rows, tn), lambda j: (0, j)),
        compiler_params=_cparams(("arbitrary",)),
        name="ada",
    )(cvec, w_ada, b_ada)


def _inproj_kernel(x_ref, sh_ref, sc_ref, w_ref, y_ref, f_ref):
    half = x_ref.shape[0] // 2
    for h in range(2):
        r = slice(h * half, (h + 1) * half)
        u = _layer_norm(x_ref[r, :]) * (1.0 + sc_ref[0]) + sh_ref[0]
        proj = jnp.dot(u.astype(BF16), w_ref[...], preferred_element_type=F32)
        y_ref[r, :] = proj[:, :PROJ_W]
        f_ref[r, :] = proj[:, PROJ_W:]


def _inproj(x2d, shift, scale, w, n_per_batch):
    t, d = x2d.shape
    if shift.shape[0] == 1:
        tm = min(TM, t)
        per = t // tm
    else:
        tm = min(TM, n_per_batch)
        per = n_per_batch // tm
    return pl.pallas_call(
        _inproj_kernel,
        out_shape=(jax.ShapeDtypeStruct((t, PROJ_W), F32), jax.ShapeDtypeStruct((t, D_FOURIER), F32)),
        grid=(t // tm,),
        in_specs=[pl.BlockSpec((tm, d), lambda i: (i, 0)),
                  pl.BlockSpec((1, 1, d), lambda i: (i // per, 0, 0)),
                  pl.BlockSpec((1, 1, d), lambda i: (i // per, 0, 0)),
                  pl.BlockSpec((d, PROJ_W + D_FOURIER), lambda i: (0, 0))],
        out_specs=(pl.BlockSpec((tm, PROJ_W), lambda i: (i, 0)),
                   pl.BlockSpec((tm, D_FOURIER), lambda i: (i, 0))),
        compiler_params=_cparams(("arbitrary",)),
        name="inproj",
    )(x2d, shift, scale, w)


def _conv_kernel(y_ref, w_ref, b_ref, o_ref, pad_ref, *, n, grid2d, tiles):
    zeros = jnp.zeros((CONV_PAD, LANES), F32)
    pad_ref[0:CONV_PAD, :] = zeros
    pad_ref[CONV_PAD + n:CONV_PAD + n + CONV_PAD, :] = zeros
    rb = min(CONV_RB, n)
    col = lax.broadcasted_iota(jnp.int32, (rb, LANES), 0) % GRID_W
    not_first = col >= 1
    not_last = col <= GRID_W - 2
    row_taps = (0, 1, 2) if grid2d else (1,)
    for ct in range(tiles):
        lanes = slice(ct * LANES, (ct + 1) * LANES)
        pad_ref[CONV_PAD:CONV_PAD + n, :] = y_ref[0, :, lanes]
        k_scale = jnp.where(pl.program_id(1) * tiles + ct >= N_HEADS, D_HEAD ** -0.5, 1.0).astype(F32)
        bias = b_ref[:, lanes]
        for blk in range(n // rb):
            r0 = CONV_PAD + blk * rb
            acc = jnp.zeros((rb, LANES), F32)
            for dj in range(3):
                part = jnp.zeros((rb, LANES), F32)
                for di in row_taps:
                    off = (di - 1) * GRID_W + (dj - 1)
                    tap = w_ref[di * 3 + dj:di * 3 + dj + 1, lanes]
                    part = part + tap * pad_ref[r0 + off:r0 + off + rb, :]
                if grid2d and dj == 0:
                    part = jnp.where(not_first, part, 0.0)
                if grid2d and dj == 2:
                    part = jnp.where(not_last, part, 0.0)
                acc = acc + part
            val = _silu(acc + bias) * k_scale
            o_ref[0, blk * rb:(blk + 1) * rb, lanes] = val.astype(o_ref.dtype)


def _conv(y3d, conv_w9, conv_b, grid2d):
    b, n, _ = y3d.shape
    n_ct = 2 * D_MLSTM // LANES
    tiles = 1 if grid2d else n_ct
    w = tiles * LANES
    return pl.pallas_call(
        functools.partial(_conv_kernel, n=n, grid2d=grid2d, tiles=tiles),
        out_shape=jax.ShapeDtypeStruct((b, n, 2 * D_MLSTM), BF16),
        grid=(b, n_ct // tiles),
        in_specs=[pl.BlockSpec((1, n, w), lambda i, c: (i, 0, c)),
                  pl.BlockSpec((9, w), lambda i, c: (0, c)),
                  pl.BlockSpec((1, w), lambda i, c: (0, c))],
        out_specs=pl.BlockSpec((1, n, w), lambda i, c: (i, 0, c)),
        scratch_shapes=[pltpu.VMEM((n + 2 * CONV_PAD, LANES), F32)],
        compiler_params=_cparams(("arbitrary", "arbitrary")),
        name="conv2d" if grid2d else "conv1d",
    )(y3d, conv_w9, conv_b)


STATE_ROWS = D_HEAD + 32
GATE_A, GATE_CM, GATE_B = 0, 1, 2
GATE_GROUP = 32


def _gates_kernel(g_ref, gb_ref, rv_ref, *, nc):
    u_i = lax.broadcasted_iota(jnp.int32, (CHUNK, CHUNK), 0)
    t_i = lax.broadcasted_iota(jnp.int32, (CHUNK, CHUNK), 1)
    prefix = (u_i <= t_i).astype(F32)
    suffix = (u_i >= t_i).astype(F32)
    grp = min(GATE_GROUP, nc)
    rows = grp * SUBLANES
    lane = lax.broadcasted_iota(jnp.int32, (rows, LANES), 1)
    is_fwd = lax.broadcasted_iota(jnp.int32, (rows, LANES), 0) % SUBLANES < N_HEADS
    h = N_HEADS

    def running_max(a):
        fwd, bwd = a, a
        sh = 1
        while sh < CHUNK:
            fwd = jnp.where(lane >= sh, jnp.maximum(fwd, pltpu.roll(fwd, sh, axis=1)), fwd)
            bwd = jnp.where(lane < CHUNK - sh, jnp.maximum(bwd, pltpu.roll(bwd, CHUNK - sh, axis=1)), bwd)
            sh *= 2
        return jnp.where(is_fwd, fwd, bwd)

    def body(i, carry):
        c0 = i * grp
        r0 = pl.multiple_of(c0 * CHUNK, CHUNK)
        pre = g_ref[0, pl.ds(r0, grp * CHUNK), :] + gb_ref[...]
        pre_t = jnp.concatenate(
            [pre[j * CHUNK:(j + 1) * CHUNK].T[:N_GATE_COLS, :] for j in range(grp)], axis=0)
        lf_t = -(jnp.maximum(-pre_t, 0.0) + jnp.log1p(jnp.exp(-jnp.abs(pre_t))))
        cum_f = jnp.dot(lf_t, prefix, precision=HIGHEST, preferred_element_type=F32)
        cum_b = jnp.dot(lf_t, suffix, precision=HIGHEST, preferred_element_type=F32)
        pick = lambda t, j, q: t[j * N_GATE_COLS + q * h:j * N_GATE_COLS + (q + 1) * h]
        b8 = jnp.concatenate([x for j in range(grp) for x in (pick(cum_f, j, 1), pick(cum_b, j, 3))], axis=0)
        li8 = jnp.concatenate([x for j in range(grp) for x in (pick(pre_t, j, 0), pick(pre_t, j, 2))], axis=0)
        a8 = li8 - b8
        as_chunks = lambda t: t.reshape(grp, SUBLANES, LANES)
        rv_ref[0, pl.ds(c0, grp), GATE_A] = as_chunks(a8)
        rv_ref[0, pl.ds(c0, grp), GATE_CM] = as_chunks(running_max(a8))
        rv_ref[0, pl.ds(c0, grp), GATE_B] = as_chunks(b8)
        return carry

    lax.fori_loop(0, nc // grp, body, 0)


def _gates(y3d, gate_b_row):
    b, n, _ = y3d.shape
    nc = n // CHUNK
    return pl.pallas_call(
        functools.partial(_gates_kernel, nc=nc),
        out_shape=jax.ShapeDtypeStruct((b, nc, 3, SUBLANES, LANES), F32),
        grid=(b,),
        in_specs=[pl.BlockSpec((1, n, LANES), lambda i: (i, 0, COL_G // LANES)),
                  pl.BlockSpec((1, LANES), lambda i: (0, 0))],
        out_specs=pl.BlockSpec((1, nc, 3, SUBLANES, LANES), lambda i: (i, 0, 0, 0, 0)),
        compiler_params=_cparams(("arbitrary",)),
        name="gates",
    )(y3d, gate_b_row)


def _mlstm_kernel(q_ref, k_ref, v_ref, o_ref, rv_ref, c0_ref, m0_ref, hg_ref,
                  out_ref, cfin_ref, mfin_ref, hf_ref, hb_ref, cst_ref, mst_ref, *, nc, heads):
    cst_ref[...] = c0_ref[0]
    mst_ref[...] = m0_ref[0]
    s_i = lax.broadcasted_iota(jnp.int32, (CHUNK, CHUNK), 0)
    t_i = lax.broadcasted_iota(jnp.int32, (CHUNK, CHUNK), 1)
    ones_row = (s_i[:STATE_ROWS - D_HEAD] == 0).astype(F32)

    def one_chunk(c, d, mask_t, last, h_ref, hh):
        r0 = pl.multiple_of(c * CHUNK, CHUNK)
        lanes = slice(hh * D_HEAD, (hh + 1) * D_HEAD)
        g = d * N_HEADS + pl.program_id(1) * heads + hh
        gate_row = lambda tbl: rv_ref[0, pl.ds(c, 1), tbl, pl.ds(g, 1), :].reshape(1, LANES)
        a_row, cm_row, b_row = gate_row(GATE_A), gate_row(GATE_CM), gate_row(GATE_B)
        q_t = q_ref[0, pl.ds(r0, CHUNK), lanes].T
        k = k_ref[0, pl.ds(r0, CHUNK), lanes]
        v_t_aug = jnp.concatenate([v_ref[0, pl.ds(r0, CHUNK), lanes].T, ones_row], axis=0)
        m_prev = mst_ref[hh, d, 0:1, :]
        m_row = jnp.maximum(m_prev, cm_row)
        a_bc = jnp.broadcast_to(a_row, (CHUNK, CHUNK)).T
        d_t = jnp.where(mask_t, jnp.exp(a_bc - m_row), 0.0)
        s_t = jnp.dot(k, q_t, preferred_element_type=F32) * d_t
        cs = cst_ref[hh, d]
        rhs = jnp.concatenate(
            [s_t.astype(BF16), (q_t.astype(F32) * jnp.exp(m_prev - m_row)).astype(BF16)], axis=0)
        lhs = jnp.concatenate([v_t_aug.astype(BF16), cs.astype(BF16)], axis=1)
        h_t = jnp.dot(lhs, rhs, preferred_element_type=F32)
        den = h_t[D_HEAD:D_HEAD + 1, :]
        floor = jnp.exp(-b_row - m_row)
        h_ref[hh, pl.ds(c, 1)] = (h_t[:D_HEAD] / jnp.maximum(jnp.abs(den), floor))[None]
        m_last = m_row[:, last:last + 1]
        wv_t = (v_t_aug * jnp.exp(a_row - m_last)).astype(BF16)
        cst_ref[hh, d] = jnp.exp(m_prev - m_last) * cs + jnp.dot(wv_t, k, preferred_element_type=F32)
        mst_ref[hh, d] = jnp.broadcast_to(b_row[:, last:last + 1] + m_last, (SUBLANES, LANES))

    def body(c, carry):
        for hh in range(heads):
            one_chunk(c, 0, s_i <= t_i, CHUNK - 1, hf_ref, hh)
            one_chunk(nc - 1 - c, 1, s_i >= t_i, 0, hb_ref, hh)
        return carry

    lax.fori_loop(0, nc, body, 0, unroll=min(16 // heads, nc))
    cfin_ref[0] = cst_ref[...]
    mfin_ref[0] = mst_ref[...]

    def readout(c, carry):
        r0 = pl.multiple_of(c * CHUNK, CHUNK)
        for hh in range(heads):
            lanes = slice(hh * D_HEAD, (hh + 1) * D_HEAD)
            h_t = hf_ref[hh, pl.ds(c, 1)][0] + hb_ref[hh, pl.ds(c, 1)][0]
            hc = h_t - jnp.mean(h_t, axis=0, keepdims=True)
            hn = (hc * lax.rsqrt(jnp.mean(hc * hc, axis=0, keepdims=True) + LN_EPS)).T
            gate = jax.nn.sigmoid(o_ref[0, pl.ds(r0, CHUNK), lanes])
            out_ref[0, pl.ds(r0, CHUNK), lanes] = (hn * hg_ref[:, lanes] * gate).astype(out_ref.dtype)
        return carry

    lax.fori_loop(0, nc, readout, 0, unroll=min(16, nc))


def _mlstm(qk, y3d, rowv, c0, m0, head_g):
    b, n, _ = qk.shape
    nc = n // CHUNK
    heads = N_HEADS if nc <= 4 else 1
    w = heads * D_HEAD
    v_blk = COL_V // w
    o_blk = COL_O // w
    tok = lambda off: pl.BlockSpec((1, n, w), lambda i, h: (i, 0, off + h))
    state_c = pl.BlockSpec((1, heads, 2, STATE_ROWS, D_HEAD), lambda i, h: (i, h, 0, 0, 0))
    state_m = pl.BlockSpec((1, heads, 2, SUBLANES, LANES), lambda i, h: (i, h, 0, 0, 0))
    return pl.pallas_call(
        functools.partial(_mlstm_kernel, nc=nc, heads=heads),
        out_shape=(jax.ShapeDtypeStruct((b, n, D_MLSTM), BF16),
                   jax.ShapeDtypeStruct((b, N_HEADS, 2, STATE_ROWS, D_HEAD), F32),
                   jax.ShapeDtypeStruct((b, N_HEADS, 2, SUBLANES, LANES), F32)),
        grid=(b, N_HEADS // heads),
        in_specs=[tok(0), tok(D_MLSTM // w), tok(v_blk), tok(o_blk),
                  pl.BlockSpec((1, nc, 3, SUBLANES, LANES), lambda i, h: (i, 0, 0, 0, 0)),
                  state_c, state_m,
                  pl.BlockSpec((1, w), lambda i, h: (0, h))],
        out_specs=(pl.BlockSpec((1, n, w), lambda i, h: (i, 0, h)), state_c, state_m),
        scratch_shapes=[pltpu.VMEM((heads, nc, D_HEAD, CHUNK), F32), pltpu.VMEM((heads, nc, D_HEAD, CHUNK), F32),
                        pltpu.VMEM((heads, 2, STATE_ROWS, D_HEAD), F32),
                        pltpu.VMEM((heads, 2, SUBLANES, LANES), F32)],
        compiler_params=_cparams(("arbitrary", "arbitrary")),
        name="mlstm",
    )(qk, qk, y3d, y3d, rowv, c0, m0, head_g)


def _dft_tables():
    r = DFT_R
    idx = np.arange(r)
    cg = np.cos(2 * np.pi * np.outer(idx, idx) / FOURIER_GROUP)
    sg = np.sin(2 * np.pi * np.outer(idx, idx) / FOURIER_GROUP)
    n_grp = D_FOURIER // FOURIER_GROUP
    bdc = np.kron(np.eye(n_grp), cg) / 8.0
    bds = np.kron(np.eye(n_grp), sg) / 8.0
    k1 = idx[:, None]
    a = idx[None, :]
    m1 = np.zeros((r, 2 * r, 2 * r))
    for b2 in range(r):
        th = 2 * np.pi * k1 * (r * a + b2) / (r * r)
        ec, es = np.cos(th), np.sin(th)
        m1[b2] = np.block([[ec, -es], [es, ec]]) / 8.0
    th2 = 2 * np.pi * np.outer(idx, idx) / r
    m2 = np.concatenate([np.cos(th2), -np.sin(th2)], axis=1) / 8.0
    as_bf16 = lambda t: jnp.asarray(t, dtype=F32).astype(BF16)
    return as_bf16(bdc), as_bf16(bds), as_bf16(m1), as_bf16(m2)


FOUR_BB = 8


N_SLABS = D_FOURIER // LANES


def _to_slabs(slab_ref, val):
    for s in range(N_SLABS):
        slab_ref[s] = val[:, s * LANES:(s + 1) * LANES]


def _strided_rows(slab_ref, start):
    return jnp.concatenate(
        [slab_ref[s, pl.ds(start, DFT_R, stride=FOUR_BB), :] for s in range(N_SLABS)], axis=1)


def _four_kernel(f_ref, bdc_ref, bds_ref, m1_ref, m2_ref, o_ref, z_ref, ps_ref, qs_ref, zs_ref, os_ref):
    r = DFT_R

    def stage1(blk, carry):
        b0 = pl.multiple_of(blk * FOUR_BB, FOUR_BB)
        fb = f_ref[0, :, pl.ds(b0, FOUR_BB), :].reshape(r * FOUR_BB, D_FOURIER).astype(BF16)
        _to_slabs(ps_ref, jnp.dot(fb, bdc_ref[...], preferred_element_type=F32))
        _to_slabs(qs_ref, jnp.dot(fb, bds_ref[...], preferred_element_type=F32))
        for j in range(FOUR_BB):
            x2 = jnp.concatenate([_strided_rows(ps_ref, j), _strided_rows(qs_ref, j)], axis=0).astype(BF16)
            zz = jnp.dot(m1_ref[b0 + j], x2, preferred_element_type=F32)
            z_ref[b0 + j] = _pack_pair(zz[:r], zz[r:])
        return carry

    lax.fori_loop(0, r // FOUR_BB, stage1, 0)

    def stage2(blk, carry):
        k0 = pl.multiple_of(blk * FOUR_BB, FOUR_BB)
        _to_slabs(zs_ref, z_ref[:, pl.ds(k0, FOUR_BB), :].reshape(r * FOUR_BB, D_FOURIER))
        for j in range(FOUR_BB):
            zc, zs = _unpack_pair_bf16(_strided_rows(zs_ref, j))
            o = jnp.dot(m2_ref[...], jnp.concatenate([zc, zs], axis=0), preferred_element_type=F32)
            for s in range(N_SLABS):
                os_ref[s, pl.ds(j, r, stride=FOUR_BB), :] = o[:, s * LANES:(s + 1) * LANES]
        for s in range(N_SLABS):
            o_ref[0, :, pl.ds(k0, FOUR_BB), s * LANES:(s + 1) * LANES] = os_ref[s].reshape(r, FOUR_BB, LANES)
        return carry

    lax.fori_loop(0, r // FOUR_BB, stage2, 0)


def _fourier(y4, tables):
    bdc, bds, m1, m2 = tables
    b = y4.shape[0]
    r = DFT_R
    slab = lambda dt: pltpu.VMEM((N_SLABS, r * FOUR_BB, LANES), dt)
    const = lambda *shape: pl.BlockSpec(shape, lambda i: (0,) * len(shape))
    return pl.pallas_call(
        _four_kernel,
        out_shape=jax.ShapeDtypeStruct((b, r, r, D_FOURIER), F32),
        grid=(b,),
        in_specs=[pl.BlockSpec((1, r, r, D_FOURIER), lambda i: (i, 0, 0, 0)),
                  const(D_FOURIER, D_FOURIER), const(D_FOURIER, D_FOURIER),
                  const(r, 2 * r, 2 * r), const(r, 2 * r)],
        out_specs=pl.BlockSpec((1, r, r, D_FOURIER), lambda i: (i, 0, 0, 0)),
        scratch_shapes=[pltpu.VMEM((r, r, D_FOURIER), jnp.int32), slab(F32), slab(F32), slab(jnp.int32), slab(F32)],
        compiler_params=_cparams(("arbitrary",), vmem=FOUR_VMEM_LIMIT),
        name="fourier",
    )(y4, bdc, bds, m1, m2)


OUTPROJ_PARTS = 4


def _outproj_kernel(x_ref, mr_ref, fo_ref, wm_ref, wf_ref, g1_ref, lg_ref, lb_ref,
                    sh_ref, sc_ref, wr_ref, x1_ref, xm_ref, s_ref):
    part = x_ref.shape[0] // OUTPROJ_PARTS
    rows = lambda h: slice(h * part, (h + 1) * part)
    mixer = lambda r: (jnp.dot(mr_ref[r, :], wm_ref[...], preferred_element_type=F32)
                       + jnp.dot(fo_ref[r, :].astype(BF16), wf_ref[...], preferred_element_type=F32))
    mix_next = mixer(rows(0))
    for h in range(OUTPROJ_PARTS):
        r = rows(h)
        mix = mix_next
        if h + 1 < OUTPROJ_PARTS:
            mix_next = mixer(rows(h + 1))
        x1 = _layer_norm(DEEPNORM_ALPHA * x_ref[r, :] + g1_ref[0] * mix) * lg_ref[...] + lb_ref[...]
        x1_ref[r, :] = x1
        xm = (_layer_norm(x1) * (1.0 + sc_ref[0]) + sh_ref[0]).astype(BF16)
        xm_ref[r, :] = _pack_rows(xm)
        logits = jnp.dot(xm, wr_ref[...], preferred_element_type=F32)
        s_ref[r, :] = jax.nn.sigmoid(logits)


def _outproj(x2d, mread, four, w_m, w_f, gate1, ln_g, ln_b, shift2, scale2, w_router, n_per_batch):
    t, d = x2d.shape
    per = n_per_batch // TM
    tokd = lambda w: pl.BlockSpec((TM, w), lambda i: (i, 0))
    full = lambda r, c: pl.BlockSpec((r, c), lambda i: (0, 0))
    mod = pl.BlockSpec((1, 1, d), lambda i: (i // per, 0, 0))
    return pl.pallas_call(
        _outproj_kernel,
        out_shape=(jax.ShapeDtypeStruct((t, d), F32),
                   jax.ShapeDtypeStruct((t, PACK_W), jnp.int32),
                   jax.ShapeDtypeStruct((t, LANES), F32)),
        grid=(t // TM,),
        in_specs=[tokd(d), tokd(D_MLSTM), tokd(D_FOURIER), full(D_MLSTM, d), full(D_FOURIER, d),
                  mod, full(1, d), full(1, d), mod, mod, full(d, LANES)],
        out_specs=(tokd(d), tokd(PACK_W), tokd(LANES)),
        compiler_params=_cparams(("arbitrary",)),
        name="outproj",
    )(x2d, mread, four, w_m, w_f, gate1, ln_g, ln_b, shift2, scale2, w_router)


def _route_kernel(s_ref, b_ref, idx_ref, gw_ref, rank_ref, cnt_ref, carry_ref):
    @pl.when(pl.program_id(0) == 0)
    def _():
        carry_ref[...] = jnp.zeros_like(carry_ref)

    tm = s_ref.shape[0]
    reps = tm // LANES
    s_t = s_ref[...].T[:N_EXPERTS, :]
    e_i = lax.broadcasted_iota(jnp.int32, (N_EXPERTS, tm), 0).astype(F32)
    neg_inf = jnp.float32(-jnp.inf)
    sb = s_t + jnp.tile(b_ref[...], (1, reps))
    picks, sels = [], []
    taken = jnp.zeros((N_EXPERTS, tm), F32)
    for _ in range(TOP_K):
        best = jnp.max(sb, axis=0, keepdims=True)
        pick = jnp.min(jnp.where(sb == best, e_i, float(N_EXPERTS)), axis=0, keepdims=True)
        onehot = e_i == pick
        sels.append(jnp.sum(jnp.where(onehot, s_t, 0.0), axis=0, keepdims=True))
        picks.append(pick)
        sb = jnp.where(onehot, neg_inf, sb)
        taken = taken + onehot.astype(F32)
    u_i = lax.broadcasted_iota(jnp.int32, (tm, tm), 0)
    t_i = lax.broadcasted_iota(jnp.int32, (tm, tm), 1)
    before = (u_i < t_i).astype(BF16)
    taken_bf = taken.astype(BF16)
    rank_all = jnp.tile(carry_ref[...], (1, reps)) + jnp.dot(taken_bf, before, preferred_element_type=F32)
    carry_ref[...] = carry_ref[...] + jnp.dot(taken_bf, jnp.ones((tm, LANES), BF16),
                                              preferred_element_type=F32)
    cnt_ref[...] = carry_ref[...]
    total = sels[0]
    for v in sels[1:]:
        total = total + v
    ranks = [jnp.sum(jnp.where(e_i == picks[j], rank_all, 0.0), axis=0, keepdims=True) for j in range(TOP_K)]
    idx_ref[...] = jnp.concatenate(picks, axis=0).astype(jnp.int32)
    rank_ref[...] = jnp.concatenate(ranks, axis=0).astype(jnp.int32)
    gw_t = jnp.concatenate([ROUTED_SCALE * v / total for v in sels]
                           + [jnp.zeros((LANES - TOP_K, tm), F32)], axis=0)
    gw_ref[...] = gw_t.T[:, :TOP_K]


def _route(scores, bias_rep):
    t = scores.shape[0]
    tok = lambda w: pl.BlockSpec((TM_ROUTE, w), lambda i: (i, 0))
    slot_major = pl.BlockSpec((TOP_K, TM_ROUTE), lambda i: (0, i))
    per_expert = pl.BlockSpec((N_EXPERTS, LANES), lambda i: (0, 0))
    return pl.pallas_call(
        _route_kernel,
        out_shape=(jax.ShapeDtypeStruct((TOP_K, t), jnp.int32),
                   jax.ShapeDtypeStruct((t, TOP_K), F32),
                   jax.ShapeDtypeStruct((TOP_K, t), jnp.int32),
                   jax.ShapeDtypeStruct((N_EXPERTS, LANES), F32)),
        grid=(t // TM_ROUTE,),
        in_specs=[tok(LANES), per_expert],
        out_specs=(slot_major, tok(TOP_K), slot_major, per_expert),
        scratch_shapes=[pltpu.VMEM((N_EXPERTS, LANES), F32)],
        compiler_params=_cparams(("arbitrary",)),
        name="route",
    )(scores, bias_rep)


TN_DEST = 2048


def _dest_kernel(ps_ref, idx_ref, rank_ref, dest_ref):
    idx = idx_ref[...]
    start = jnp.zeros(idx.shape, jnp.int32)
    for e in range(N_EXPERTS):
        start = jnp.where(idx == e, ps_ref[e], start)
    dest_ref[...] = start + rank_ref[...]


def _dest(pad_start, idx_t, rank_t):
    t = idx_t.shape[1]
    blk = lambda: pl.BlockSpec((TOP_K, TN_DEST), lambda i, ps: (0, i))
    grid_spec = pltpu.PrefetchScalarGridSpec(
        num_scalar_prefetch=1, grid=(t // TN_DEST,), in_specs=[blk(), blk()], out_specs=blk())
    return pl.pallas_call(
        _dest_kernel,
        out_shape=jax.ShapeDtypeStruct((TOP_K, t), jnp.int32),
        grid_spec=grid_spec,
        compiler_params=_cparams(("arbitrary",)),
        name="dest",
    )(pad_start, idx_t, rank_t)


def _sc_worker_id(info):
    return lax.axis_index("s") * info.num_cores + lax.axis_index("c")


def _sc_dispatch(xm, dest3, n_rows):
    t, w = xm.shape
    info = plsc.get_sparse_core_info()
    n_workers = info.num_cores * info.num_subcores
    per_worker = t // SC_ROWS // n_workers
    mesh = plsc.VectorSubcoreMesh(core_axis_name="c", subcore_axis_name="s")

    @functools.partial(
        pl.kernel, mesh=mesh,
        out_type=jax.ShapeDtypeStruct((n_rows, w), xm.dtype),
        scratch_types=[pltpu.VMEM((TOP_K, SC_ROWS), jnp.int32), pltpu.VMEM((SC_ROWS, w), xm.dtype)],
        name="sc_dispatch",
    )
    def run(xm_hbm, dest_hbm, xs_hbm, idx_v, rows_v):
        first = _sc_worker_id(info) * per_worker

        @pl.loop(0, per_worker)
        def _(ci):
            chunk = first + ci
            pltpu.sync_copy(dest_hbm.at[chunk], idx_v)
            pltpu.sync_copy(xm_hbm.at[pl.ds(chunk * SC_ROWS, SC_ROWS)], rows_v)
            for j in range(TOP_K):
                pltpu.sync_copy(rows_v, xs_hbm.at[idx_v.at[j]])

    return run(xm, dest3)


def _sc_gather(table, idx2):
    n_chunks, rows = idx2.shape
    w = table.shape[1]
    info = plsc.get_sparse_core_info()
    n_workers = info.num_cores * info.num_subcores
    per_worker = n_chunks // n_workers
    assert per_worker % 2 == 0
    mesh = plsc.VectorSubcoreMesh(core_axis_name="c", subcore_axis_name="s")

    @functools.partial(
        pl.kernel, mesh=mesh,
        out_type=jax.ShapeDtypeStruct((n_chunks * rows, w), table.dtype),
        scratch_types=[pltpu.VMEM((2, rows), jnp.int32), pltpu.VMEM((2, rows, w), table.dtype),
                       pltpu.SemaphoreType.DMA, pltpu.SemaphoreType.DMA],
        name="sc_gather",
    )
    def run(table_hbm, idx_hbm, out_hbm, idx_v, rows_v, sem0, sem1):
        first = _sc_worker_id(info) * per_worker
        sems = (sem0, sem1)

        def gather(b):
            return pltpu.make_async_copy(table_hbm.at[idx_v.at[b]], rows_v.at[b], sems[b])

        def start(chunk, b):
            pltpu.sync_copy(idx_hbm.at[chunk], idx_v.at[b])
            gather(b).start()

        def finish(chunk, b):
            gather(b).wait()
            pltpu.sync_copy(rows_v.at[b], out_hbm.at[pl.ds(chunk * rows, rows)])

        start(first, 0)

        @pl.loop(0, per_worker, step=2)
        def _(ci):
            chunk = first + ci
            start(chunk + 1, 1)
            finish(chunk, 0)

            @pl.when(ci + 2 < per_worker)
            def _():
                start(chunk + 2, 0)

            finish(chunk + 1, 1)

    return run(table, idx2)


def _expert_kernel(first_ref, count_ref, used_ref,
                   xs_hbm, wgu_ref, wd_ref, y_hbm, wgu_bf, wd_bf, xbuf, ybuf, in_sem, out_sem):
    e = pl.program_id(0)
    n_used = used_ref[0]
    first = first_ref[e]
    n_blocks = count_ref[e]

    def rows(g):
        return pl.ds(pl.multiple_of(g * ROW_BLK, ROW_BLK), ROW_BLK)

    def read(g):
        slot = g % EXP_IN_SLOTS
        return pltpu.make_async_copy(xs_hbm.at[rows(g), :], xbuf.at[slot], in_sem.at[slot])

    def write(g):
        slot = g % EXP_OUT_SLOTS
        return pltpu.make_async_copy(ybuf.at[slot], y_hbm.at[rows(g), :], out_sem.at[slot])

    @pl.when(e == 0)
    def _():
        for g0 in range(EXP_AHEAD):
            @pl.when(g0 < n_used)
            def _():
                read(g0).start()

    @pl.when(n_blocks > 0)
    def _():
        wgu_bf[...] = wgu_ref[0].astype(BF16)
        wd_bf[...] = wd_ref[0].astype(BF16)

    def acquire(g):
        @pl.when(g + EXP_AHEAD < n_used)
        def _():
            read(g + EXP_AHEAD).start()

        read(g).wait()

        @pl.when(g >= EXP_OUT_SLOTS)
        def _():
            write(g - EXP_OUT_SLOTS).wait()

    def gate_up(g):
        return jnp.dot(_unpack_bf16(xbuf[g % EXP_IN_SLOTS]), wgu_bf[...], preferred_element_type=F32)

    def down(g, gu):
        act = _silu(gu[:, :D_EXPERT]) * gu[:, D_EXPERT:]
        yv = jnp.dot(act.astype(BF16), wd_bf[...], preferred_element_type=F32)
        words = pltpu.pack_elementwise([yv[:, :PACK_W], yv[:, PACK_W:]], packed_dtype=BF16)
        ybuf[g % EXP_OUT_SLOTS] = lax.bitcast_convert_type(words, jnp.int32)

    def handle(g, n):
        for i in range(n):
            acquire(g + i)
        gu = gate_up(g)
        for i in range(n):
            gu_next = gate_up(g + i + 1) if i + 1 < n else None
            down(g + i, gu)
            gu = gu_next
        for i in range(n):
            write(g + i).start()

    def group(k, carry):
        handle(first + EXP_GROUP * k, EXP_GROUP)
        return carry

    lax.fori_loop(0, n_blocks // EXP_GROUP, group, 0)

    n = EXP_GROUP // 2
    while n >= 1:
        @pl.when(n_blocks % (2 * n) >= n)
        def _(n=n):
            handle(first + n_blocks - n_blocks % (2 * n), n)
        n //= 2

    @pl.when(e == pl.num_programs(0) - 1)
    def _():
        for back in range(EXP_OUT_SLOTS, 0, -1):
            @pl.when(n_used >= back)
            def _():
                write(n_used - back).wait()


def _experts(first_block, n_blocks, n_used, xs, w_gu, w_down):
    n_rows, w = xs.shape
    d = D_MODEL
    grid_spec = pltpu.PrefetchScalarGridSpec(
        num_scalar_prefetch=3,
        grid=(N_EXPERTS,),
        in_specs=[pl.BlockSpec(memory_space=pl.ANY),
                  pl.BlockSpec((1, d, 2 * D_EXPERT), lambda e, *_: (e, 0, 0)),
                  pl.BlockSpec((1, D_EXPERT, d), lambda e, *_: (e, 0, 0))],
        out_specs=pl.BlockSpec(memory_space=pl.ANY),
        scratch_shapes=[pltpu.VMEM((d, 2 * D_EXPERT), BF16), pltpu.VMEM((D_EXPERT, d), BF16),
                        pltpu.VMEM((EXP_IN_SLOTS, ROW_BLK, w), jnp.int32),
                        pltpu.VMEM((EXP_OUT_SLOTS, ROW_BLK, w), jnp.int32),
                        pltpu.SemaphoreType.DMA((EXP_IN_SLOTS,)),
                        pltpu.SemaphoreType.DMA((EXP_OUT_SLOTS,))],
    )
    return pl.pallas_call(
        _expert_kernel,
        out_shape=jax.ShapeDtypeStruct((n_rows, w), jnp.int32),
        grid_spec=grid_spec,
        compiler_params=_cparams(("arbitrary",)),
        name="experts",
    )(first_block, n_blocks, n_used, xs, w_gu, w_down)


TM_COMBINE = 512


def _combine_kernel(*refs):
    yg_refs = refs[:TOP_K]
    gw_ref, x1_ref, xm_ref, g2_ref, wsgu_ref, wsd_ref, lg_ref, lb_ref, out_ref = refs[TOP_K:]
    gu = jnp.dot(_unpack_bf16(xm_ref[...]), wsgu_ref[...], preferred_element_type=F32)
    act = _silu(gu[:, :D_EXPERT]) * gu[:, D_EXPERT:]
    acc = jnp.dot(act.astype(BF16), wsd_ref[...], preferred_element_type=F32)
    gw = gw_ref[...]
    for j in range(TOP_K):
        acc = acc + gw[:, j:j + 1] * _unpack_rows(yg_refs[j][...])
    z = DEEPNORM_ALPHA * x1_ref[...] + g2_ref[0] * acc
    out_ref[...] = _layer_norm(z) * lg_ref[...] + lb_ref[...]


def _combine(yg, gw, x1, xm, gate2, w_sgu, w_sd, ln_g, ln_b, n_per_batch):
    t, d = x1.shape
    tm = TM_COMBINE
    per = n_per_batch // tm
    steps = t // tm
    tok = lambda w: pl.BlockSpec((tm, w), lambda i: (i, 0))
    full = lambda r, c: pl.BlockSpec((r, c), lambda i: (0, 0))
    slot = lambda j: pl.BlockSpec((tm, PACK_W), lambda i: (j * steps + i, 0))
    return pl.pallas_call(
        _combine_kernel,
        out_shape=jax.ShapeDtypeStruct((t, d), F32),
        grid=(steps,),
        in_specs=[slot(j) for j in range(TOP_K)]
        + [tok(TOP_K), tok(d), tok(PACK_W), pl.BlockSpec((1, 1, d), lambda i: (i // per, 0, 0)),
           full(d, 2 * D_EXPERT), full(D_EXPERT, d), full(1, d), full(1, d)],
        out_specs=tok(d),
        compiler_params=_cparams(("arbitrary",)),
        name="combine",
    )(*([yg] * TOP_K), gw, x1, xm, gate2, w_sgu, w_sd, ln_g, ln_b)


def _mixer_heads(stream, shift, scale, w_in_k, conv_w9, conv_b, gate_b_row, head_g, c0, m0, grid2d):
    b, n, d = stream.shape
    y, f_in = _inproj(stream.reshape(b * n, d), shift, scale, w_in_k, n)
    y = y.reshape(b, n, PROJ_W)
    qk = _conv(y, conv_w9, conv_b, grid2d)
    mread, c_fin, m_fin = _mlstm(qk, y, _gates(y, gate_b_row), c0, m0, head_g)
    return f_in, mread, c_fin, m_fin


def _moe(xm, x1, scores, bias_row, gate2, w_gu, w_down, w_sgu, w_sd, ln_g, ln_b, n_per_batch):
    t = xm.shape[0]
    n_assign = t * TOP_K
    n_rows = n_assign + N_EXPERTS * ROW_BLK
    idx_t, gw, rank_t, counts = _route(scores, bias_row)
    sizes = counts[:, 0].astype(jnp.int32)
    padded = (sizes + ROW_BLK - 1) // ROW_BLK * ROW_BLK
    pad_end = jnp.cumsum(padded)
    pad_start = (pad_end - padded).astype(jnp.int32)
    dest_t = _dest(pad_start, idx_t, rank_t)
    dest3 = dest_t.reshape(TOP_K, t // SC_ROWS, SC_ROWS).transpose(1, 0, 2)

    xs = _sc_dispatch(xm, dest3, n_rows)
    yb = _experts(pad_start // ROW_BLK, (padded // ROW_BLK).astype(jnp.int32),
                  (pad_end[-1:] // ROW_BLK).astype(jnp.int32), xs, w_gu, w_down)
    yg = _sc_gather(yb, dest_t.reshape(n_assign // SC_GATHER_ROWS, SC_GATHER_ROWS))
    return _combine(yg, gw, x1, xm, gate2, w_sgu, w_sd, ln_g, ln_b, n_per_batch)


def kernel(x, c, ctx, c_ctx, w_ada, b_ada, w_in, conv_w, conv_b, gate_b, head_g, w_out, ln1_g, ln1_b,
           w_router, router_bias, w_expert_gu, w_expert_down, w_shared_gu, w_shared_down, ln2_g, ln2_b):
    bsz, n, d = x.shape
    depth = w_ada.shape[0]
    assert depth == 1
    t = bsz * n
    tables = _dft_tables()
    for l in range(depth):
        cvec = jnp.zeros((SUBLANES, d), F32).at[:bsz].set(c).at[bsz].set(c_ctx)
        mod = _ada(cvec, w_ada[l], b_ada[l][None, :])
        mods = mod.reshape(SUBLANES, 6, 1, d)
        mod_x = [mods[:bsz, j] for j in range(6)]
        mod_c = [mods[bsz:bsz + 1, j] for j in range(2)]

        wl = w_in[l]
        g0 = 4 * D_MLSTM
        w_in_k = jnp.concatenate(
            [wl[:, :g0 + N_GATE_COLS], jnp.zeros((d, LANES - N_GATE_COLS), F32),
             wl[:, g0 + N_GATE_COLS:]], axis=1).astype(BF16)
        conv_w9 = conv_w[l].reshape(9, 2 * D_MLSTM)
        conv_b_row = conv_b[l][None, :]
        gate_b_row = jnp.zeros((1, LANES), F32).at[0, :N_GATE_COLS].set(gate_b[l].reshape(-1))
        head_g_row = head_g[l][None, :]

        c0 = jnp.zeros((bsz, N_HEADS, 2, STATE_ROWS, D_HEAD), F32)
        m0 = jnp.zeros((bsz, N_HEADS, 2, SUBLANES, LANES), F32)
        _, _, c_ctx_fin, m_ctx_fin = _mixer_heads(ctx, mod_c[0], mod_c[1], w_in_k, conv_w9, conv_b_row,
                                                  gate_b_row, head_g_row, c0, m0, grid2d=False)
        f_in, mread, _, _ = _mixer_heads(x, mod_x[0], mod_x[1], w_in_k, conv_w9, conv_b_row,
                                      gate_b_row, head_g_row, c_ctx_fin, m_ctx_fin, grid2d=True)

        four = _fourier(f_in.reshape(bsz, DFT_R, DFT_R, D_FOURIER), tables).reshape(t, D_FOURIER)

        w_o = w_out[l].astype(BF16)
        w_r = jnp.zeros((d, LANES), F32).at[:, :N_EXPERTS].set(w_router[l]).astype(BF16)
        x2d = x.reshape(t, d)
        x1, xm, scores = _outproj(x2d, mread.reshape(t, D_MLSTM), four, w_o[:D_MLSTM], w_o[D_MLSTM:],
                                  mod_x[2], ln1_g[l][None, :], ln1_b[l][None, :], mod_x[3], mod_x[4],
                                  w_r, n)

        bias_rep = jnp.broadcast_to(router_bias[l][:, None], (N_EXPERTS, LANES))
        x = _moe(xm, x1, scores, bias_rep, mod_x[5], w_expert_gu[l], w_expert_down[l],
                 w_shared_gu[l].astype(BF16), w_shared_down[l].astype(BF16),
                 ln2_g[l][None, :], ln2_b[l][None, :], n).reshape(bsz, n, d)
    return x
```

```python
import functools

import numpy as np
import jax
import jax.numpy as jnp
from jax import lax
from jax.experimental import pallas as pl
from jax.experimental.pallas import tpu as pltpu
from jax.experimental.pallas import tpu_sc as plsc

F32 = jnp.float32
BF16 = jnp.bfloat16
HIGHEST = lax.Precision.HIGHEST

D_MODEL = 1024
GRID_W = 64
N_HEADS = 4
D_HEAD = 128
D_MLSTM = N_HEADS * D_HEAD
D_FOURIER = 512
FOURIER_GROUP = 64
N_GATE_COLS = 16
CHUNK = 128
N_EXPERTS = 64
TOP_K = 8
D_EXPERT = 256
ROUTED_SCALE = 2.5
DEEPNORM_ALPHA = 2.0 ** 0.25
LN_EPS = 1e-5

LANES = 128
SUBLANES = 8
VMEM_LIMIT = 48 * 1024 * 1024
FOUR_VMEM_LIMIT = 56 * 1024 * 1024

COL_QK = 0
COL_V = 2 * D_MLSTM
COL_O = 3 * D_MLSTM
COL_G = 4 * D_MLSTM
PROJ_W = COL_G + LANES

TN_ADA = 2048
TM = 1024
TM_ROUTE = 1024
ROW_BLK = 256
EXP_GROUP = 8
EXP_AHEAD = 10
EXP_IN_SLOTS = 32
EXP_OUT_SLOTS = 2 * EXP_GROUP
PACK_W = D_MODEL // 2
SC_ROWS = 128
SC_GATHER_ROWS = 64
CONV_PAD = 72
CONV_RB = 256
DFT_R = 64


def _cparams(sem, vmem=VMEM_LIMIT):
    return pltpu.CompilerParams(dimension_semantics=sem, vmem_limit_bytes=vmem)


def _layer_norm(x):
    mu = jnp.mean(x, axis=-1, keepdims=True)
    xc = x - mu
    var = jnp.mean(xc * xc, axis=-1, keepdims=True)
    return xc * lax.rsqrt(var + LN_EPS)


def _silu(x):
    return x * jax.nn.sigmoid(x)


def _pack_pair(lo, hi):
    as_bits = lambda u: lax.bitcast_convert_type(u.astype(BF16).astype(F32), jnp.uint32)
    word = as_bits(hi) | lax.shift_right_logical(as_bits(lo), jnp.uint32(16))
    return lax.bitcast_convert_type(word, jnp.int32)


def _unpack_pair_bf16(w):
    lo = lax.bitcast_convert_type(w.astype(jnp.int16), BF16)
    hi = lax.bitcast_convert_type(lax.shift_right_logical(w, jnp.int32(16)).astype(jnp.int16), BF16)
    return lo, hi


def _pack_rows(v):
    return _pack_pair(v[:, :PACK_W], v[:, PACK_W:])


def _unpack_bf16(w):
    return jnp.concatenate(_unpack_pair_bf16(w), axis=-1)


def _unpack_rows(w):
    u = lax.bitcast_convert_type(w, jnp.uint32)
    lo = lax.bitcast_convert_type(lax.shift_left(u, jnp.uint32(16)), F32)
    hi = lax.bitcast_convert_type(u & jnp.uint32(0xFFFF0000), F32)
    return jnp.concatenate([lo, hi], axis=1)


def _ada_kernel(c_ref, w_ref, b_ref, o_ref):
    s = _silu(c_ref[...]).astype(BF16)
    o_ref[...] = jnp.dot(s, w_ref[...].astype(BF16), preferred_element_type=F32) + b_ref[...]


def _ada(cvec, w_ada, b_ada):
    rows, d = cvec.shape
    n_out = w_ada.shape[1]
    tn = TN_ADA
    return pl.pallas_call(
        _ada_kernel,
        out_shape=jax.ShapeDtypeStruct((rows, n_out), F32),
        grid=(n_out // tn,),
        in_specs=[pl.BlockSpec((rows, d), lambda j: (0, 0)),
                  pl.BlockSpec((d, tn), lambda j: (0, j)),
                  pl.BlockSpec((1, tn), lambda j: (0, j))],
        out_specs=pl.BlockSpec((rows, tn), lambda j: (0, j)),
        compiler_params=_cparams(("arbitrary",)),
        name="ada",
    )(cvec, w_ada, b_ada)


def _inproj_kernel(x_ref, sh_ref, sc_ref, w_ref, y_ref, f_ref):
    half = x_ref.shape[0] // 2
    for h in range(2):
        r = slice(h * half, (h + 1) * half)
        u = _layer_norm(x_ref[r, :]) * (1.0 + sc_ref[0]) + sh_ref[0]
        proj = jnp.dot(u.astype(BF16), w_ref[...], preferred_element_type=F32)
        y_ref[r, :] = proj[:, :PROJ_W]
        f_ref[r, :] = proj[:, PROJ_W:].astype(BF16)


def _inproj(x2d, shift, scale, w, n_per_batch):
    t, d = x2d.shape
    if shift.shape[0] == 1:
        tm = min(TM, t)
        per = t // tm
    else:
        tm = min(TM, n_per_batch)
        per = n_per_batch // tm
    return pl.pallas_call(
        _inproj_kernel,
        out_shape=(jax.ShapeDtypeStruct((t, PROJ_W), F32), jax.ShapeDtypeStruct((t, D_FOURIER), BF16)),
        grid=(t // tm,),
        in_specs=[pl.BlockSpec((tm, d), lambda i: (i, 0)),
                  pl.BlockSpec((1, 1, d), lambda i: (i // per, 0, 0)),
                  pl.BlockSpec((1, 1, d), lambda i: (i // per, 0, 0)),
                  pl.BlockSpec((d, PROJ_W + D_FOURIER), lambda i: (0, 0))],
        out_specs=(pl.BlockSpec((tm, PROJ_W), lambda i: (i, 0)),
                   pl.BlockSpec((tm, D_FOURIER), lambda i: (i, 0))),
        compiler_params=_cparams(("arbitrary",)),
        name="inproj",
    )(x2d, shift, scale, w)


def _conv_kernel(y_ref, w_ref, b_ref, o_ref, pad_ref, *, n, grid2d, tiles):
    zeros = jnp.zeros((CONV_PAD, LANES), F32)
    pad_ref[0:CONV_PAD, :] = zeros
    pad_ref[CONV_PAD + n:CONV_PAD + n + CONV_PAD, :] = zeros
    rb = min(CONV_RB, n)
    col = lax.broadcasted_iota(jnp.int32, (rb, LANES), 0) % GRID_W
    not_first = col >= 1
    not_last = col <= GRID_W - 2
    row_taps = (0, 1, 2) if grid2d else (1,)
    for ct in range(tiles):
        lanes = slice(ct * LANES, (ct + 1) * LANES)
        pad_ref[CONV_PAD:CONV_PAD + n, :] = y_ref[0, :, lanes]
        k_scale = jnp.where(pl.program_id(1) * tiles + ct >= N_HEADS, D_HEAD ** -0.5, 1.0).astype(F32)
        bias = b_ref[:, lanes]
        for blk in range(n // rb):
            r0 = CONV_PAD + blk * rb
            acc = jnp.zeros((rb, LANES), F32)
            for dj in range(3):
                part = jnp.zeros((rb, LANES), F32)
                for di in row_taps:
                    off = (di - 1) * GRID_W + (dj - 1)
                    tap = w_ref[di * 3 + dj:di * 3 + dj + 1, lanes]
                    part = part + tap * pad_ref[r0 + off:r0 + off + rb, :]
                if grid2d and dj == 0:
                    part = jnp.where(not_first, part, 0.0)
                if grid2d and dj == 2:
                    part = jnp.where(not_last, part, 0.0)
                acc = acc + part
            val = _silu(acc + bias) * k_scale
            o_ref[0, blk * rb:(blk + 1) * rb, lanes] = val.astype(o_ref.dtype)


def _conv(y3d, conv_w9, conv_b, grid2d):
    b, n, _ = y3d.shape
    n_ct = 2 * D_MLSTM // LANES
    tiles = 1 if grid2d else n_ct
    w = tiles * LANES
    return pl.pallas_call(
        functools.partial(_conv_kernel, n=n, grid2d=grid2d, tiles=tiles),
        out_shape=jax.ShapeDtypeStruct((b, n, 2 * D_MLSTM), BF16),
        grid=(b, n_ct // tiles),
        in_specs=[pl.BlockSpec((1, n, w), lambda i, c: (i, 0, c)),
                  pl.BlockSpec((9, w), lambda i, c: (0, c)),
                  pl.BlockSpec((1, w), lambda i, c: (0, c))],
        out_specs=pl.BlockSpec((1, n, w), lambda i, c: (i, 0, c)),
        scratch_shapes=[pltpu.VMEM((n + 2 * CONV_PAD, LANES), F32)],
        compiler_params=_cparams(("arbitrary", "arbitrary")),
        name="conv2d" if grid2d else "conv1d",
    )(y3d, conv_w9, conv_b)


STATE_ROWS = D_HEAD + 32
GATE_A, GATE_CM, GATE_B = 0, 1, 2
GATE_GROUP = 32


def _gates_kernel(g_ref, gb_ref, rv_ref, *, nc):
    u_i = lax.broadcasted_iota(jnp.int32, (CHUNK, CHUNK), 0)
    t_i = lax.broadcasted_iota(jnp.int32, (CHUNK, CHUNK), 1)
    prefix = (u_i <= t_i).astype(F32)
    suffix = (u_i >= t_i).astype(F32)
    grp = min(GATE_GROUP, nc)
    rows = grp * SUBLANES
    lane = lax.broadcasted_iota(jnp.int32, (rows, LANES), 1)
    is_fwd = lax.broadcasted_iota(jnp.int32, (rows, LANES), 0) % SUBLANES < N_HEADS
    h = N_HEADS

    def running_max(a):
        fwd, bwd = a, a
        sh = 1
        while sh < CHUNK:
            fwd = jnp.where(lane >= sh, jnp.maximum(fwd, pltpu.roll(fwd, sh, axis=1)), fwd)
            bwd = jnp.where(lane < CHUNK - sh, jnp.maximum(bwd, pltpu.roll(bwd, CHUNK - sh, axis=1)), bwd)
            sh *= 2
        return jnp.where(is_fwd, fwd, bwd)

    def body(i, carry):
        c0 = i * grp
        r0 = pl.multiple_of(c0 * CHUNK, CHUNK)
        pre = g_ref[0, pl.ds(r0, grp * CHUNK), :] + gb_ref[...]
        pre_t = jnp.concatenate(
            [pre[j * CHUNK:(j + 1) * CHUNK].T[:N_GATE_COLS, :] for j in range(grp)], axis=0)
        lf_t = -(jnp.maximum(-pre_t, 0.0) + jnp.log1p(jnp.exp(-jnp.abs(pre_t))))
        cum_f = jnp.dot(lf_t, prefix, precision=HIGHEST, preferred_element_type=F32)
        cum_b = jnp.dot(lf_t, suffix, precision=HIGHEST, preferred_element_type=F32)
        pick = lambda t, j, q: t[j * N_GATE_COLS + q * h:j * N_GATE_COLS + (q + 1) * h]
        b8 = jnp.concatenate([x for j in range(grp) for x in (pick(cum_f, j, 1), pick(cum_b, j, 3))], axis=0)
        li8 = jnp.concatenate([x for j in range(grp) for x in (pick(pre_t, j, 0), pick(pre_t, j, 2))], axis=0)
        a8 = li8 - b8
        as_chunks = lambda t: t.reshape(grp, SUBLANES, LANES)
        rv_ref[0, pl.ds(c0, grp), GATE_A] = as_chunks(a8)
        rv_ref[0, pl.ds(c0, grp), GATE_CM] = as_chunks(running_max(a8))
        rv_ref[0, pl.ds(c0, grp), GATE_B] = as_chunks(b8)
        return carry

    lax.fori_loop(0, nc // grp, body, 0)


def _gates(y3d, gate_b_row):
    b, n, _ = y3d.shape
    nc = n // CHUNK
    return pl.pallas_call(
        functools.partial(_gates_kernel, nc=nc),
        out_shape=jax.ShapeDtypeStruct((b, nc, 3, SUBLANES, LANES), F32),
        grid=(b,),
        in_specs=[pl.BlockSpec((1, n, LANES), lambda i: (i, 0, COL_G // LANES)),
                  pl.BlockSpec((1, LANES), lambda i: (0, 0))],
        out_specs=pl.BlockSpec((1, nc, 3, SUBLANES, LANES), lambda i: (i, 0, 0, 0, 0)),
        compiler_params=_cparams(("arbitrary",)),
        name="gates",
    )(y3d, gate_b_row)


def _mlstm_kernel(q_ref, k_ref, v_ref, o_ref, rv_ref, c0_ref, m0_ref, hg_ref,
                  out_ref, cfin_ref, mfin_ref, hf_ref, hb_ref, cst_ref, mst_ref, *, nc, heads):
    cst_ref[...] = c0_ref[0]
    mst_ref[...] = m0_ref[0]
    s_i = lax.broadcasted_iota(jnp.int32, (CHUNK, CHUNK), 0)
    t_i = lax.broadcasted_iota(jnp.int32, (CHUNK, CHUNK), 1)
    ones_row = (s_i[:STATE_ROWS - D_HEAD] == 0).astype(F32)

    def one_chunk(c, d, mask_t, last, h_ref, hh):
        r0 = pl.multiple_of(c * CHUNK, CHUNK)
        lanes = slice(hh * D_HEAD, (hh + 1) * D_HEAD)
        g = d * N_HEADS + pl.program_id(1) * heads + hh
        gate_row = lambda tbl: rv_ref[0, pl.ds(c, 1), tbl, pl.ds(g, 1), :].reshape(1, LANES)
        a_row, cm_row, b_row = gate_row(GATE_A), gate_row(GATE_CM), gate_row(GATE_B)
        q_t = q_ref[0, pl.ds(r0, CHUNK), lanes].T
        k = k_ref[0, pl.ds(r0, CHUNK), lanes]
        v_t_aug = jnp.concatenate([v_ref[0, pl.ds(r0, CHUNK), lanes].T, ones_row], axis=0)
        m_prev = mst_ref[hh, d, 0:1, :]
        m_row = jnp.maximum(m_prev, cm_row)
        a_bc = jnp.broadcast_to(a_row, (CHUNK, CHUNK)).T
        d_t = jnp.where(mask_t, jnp.exp(a_bc - m_row), 0.0)
        s_t = jnp.dot(k, q_t, preferred_element_type=F32) * d_t
        cs = cst_ref[hh, d]
        rhs = jnp.concatenate(
            [s_t.astype(BF16), (q_t.astype(F32) * jnp.exp(m_prev - m_row)).astype(BF16)], axis=0)
        lhs = jnp.concatenate([v_t_aug.astype(BF16), cs.astype(BF16)], axis=1)
        h_t = jnp.dot(lhs, rhs, preferred_element_type=F32)
        den = h_t[D_HEAD:D_HEAD + 1, :]
        floor = jnp.exp(-b_row - m_row)
        h_ref[hh, pl.ds(c, 1)] = (h_t[:D_HEAD] / jnp.maximum(jnp.abs(den), floor))[None]
        m_last = m_row[:, last:last + 1]
        wv_t = (v_t_aug * jnp.exp(a_row - m_last)).astype(BF16)
        cst_ref[hh, d] = jnp.exp(m_prev - m_last) * cs + jnp.dot(wv_t, k, preferred_element_type=F32)
        mst_ref[hh, d] = jnp.broadcast_to(b_row[:, last:last + 1] + m_last, (SUBLANES, LANES))

    def body(c, carry):
        for hh in range(heads):
            one_chunk(c, 0, s_i <= t_i, CHUNK - 1, hf_ref, hh)
            one_chunk(nc - 1 - c, 1, s_i >= t_i, 0, hb_ref, hh)
        return carry

    lax.fori_loop(0, nc, body, 0, unroll=min(16 // heads, nc))
    cfin_ref[0] = cst_ref[...]
    mfin_ref[0] = mst_ref[...]

    def readout(c, carry):
        r0 = pl.multiple_of(c * CHUNK, CHUNK)
        for hh in range(heads):
            lanes = slice(hh * D_HEAD, (hh + 1) * D_HEAD)
            h_t = hf_ref[hh, pl.ds(c, 1)][0] + hb_ref[hh, pl.ds(c, 1)][0]
            hc = h_t - jnp.mean(h_t, axis=0, keepdims=True)
            hn = (hc * lax.rsqrt(jnp.mean(hc * hc, axis=0, keepdims=True) + LN_EPS)).T
            gate = jax.nn.sigmoid(o_ref[0, pl.ds(r0, CHUNK), lanes])
            out_ref[0, pl.ds(r0, CHUNK), lanes] = (hn * hg_ref[:, lanes] * gate).astype(out_ref.dtype)
        return carry

    lax.fori_loop(0, nc, readout, 0, unroll=min(16, nc))


def _mlstm(qk, y3d, rowv, c0, m0, head_g):
    b, n, _ = qk.shape
    nc = n // CHUNK
    heads = N_HEADS if nc <= 4 else 1
    w = heads * D_HEAD
    v_blk = COL_V // w
    o_blk = COL_O // w
    tok = lambda off: pl.BlockSpec((1, n, w), lambda i, h: (i, 0, off + h))
    state_c = pl.BlockSpec((1, heads, 2, STATE_ROWS, D_HEAD), lambda i, h: (i, h, 0, 0, 0))
    state_m = pl.BlockSpec((1, heads, 2, SUBLANES, LANES), lambda i, h: (i, h, 0, 0, 0))
    return pl.pallas_call(
        functools.partial(_mlstm_kernel, nc=nc, heads=heads),
        out_shape=(jax.ShapeDtypeStruct((b, n, D_MLSTM), BF16),
                   jax.ShapeDtypeStruct((b, N_HEADS, 2, STATE_ROWS, D_HEAD), F32),
                   jax.ShapeDtypeStruct((b, N_HEADS, 2, SUBLANES, LANES), F32)),
        grid=(b, N_HEADS // heads),
        in_specs=[tok(0), tok(D_MLSTM // w), tok(v_blk), tok(o_blk),
                  pl.BlockSpec((1, nc, 3, SUBLANES, LANES), lambda i, h: (i, 0, 0, 0, 0)),
                  state_c, state_m,
                  pl.BlockSpec((1, w), lambda i, h: (0, h))],
        out_specs=(pl.BlockSpec((1, n, w), lambda i, h: (i, 0, h)), state_c, state_m),
        scratch_shapes=[pltpu.VMEM((heads, nc, D_HEAD, CHUNK), F32), pltpu.VMEM((heads, nc, D_HEAD, CHUNK), F32),
                        pltpu.VMEM((heads, 2, STATE_ROWS, D_HEAD), F32),
                        pltpu.VMEM((heads, 2, SUBLANES, LANES), F32)],
        compiler_params=_cparams(("arbitrary", "arbitrary")),
        name="mlstm",
    )(qk, qk, y3d, y3d, rowv, c0, m0, head_g)


def _dft_tables():
    r = DFT_R
    idx = np.arange(r)
    cg = np.cos(2 * np.pi * np.outer(idx, idx) / FOURIER_GROUP)
    sg = np.sin(2 * np.pi * np.outer(idx, idx) / FOURIER_GROUP)
    n_grp = D_FOURIER // FOURIER_GROUP
    bdc = np.kron(np.eye(n_grp), cg) / 8.0
    bds = np.kron(np.eye(n_grp), sg) / 8.0
    k1 = idx[:, None]
    a = idx[None, :]
    m1 = np.zeros((r, 2 * r, 2 * r))
    for b2 in range(r):
        th = 2 * np.pi * k1 * (r * a + b2) / (r * r)
        ec, es = np.cos(th), np.sin(th)
        m1[b2] = np.block([[ec, -es], [es, ec]]) / 8.0
    th2 = 2 * np.pi * np.outer(idx, idx) / r
    m2 = np.concatenate([np.cos(th2), -np.sin(th2)], axis=1) / 8.0
    as_bf16 = lambda t: jnp.asarray(t, dtype=F32).astype(BF16)
    return as_bf16(bdc), as_bf16(bds), as_bf16(m1), as_bf16(m2)


FOUR_BB = 8
FOUR_IN_BB = 16


N_SLABS = D_FOURIER // LANES


def _to_slabs(slab_ref, val):
    for s in range(N_SLABS):
        slab_ref[s] = val[:, s * LANES:(s + 1) * LANES]


def _strided_rows(slab_ref, start, stride=FOUR_BB):
    return jnp.concatenate(
        [slab_ref[s, pl.ds(start, DFT_R, stride=stride), :] for s in range(N_SLABS)], axis=1)


def _four_kernel(f_ref, bdc_ref, bds_ref, m1_ref, m2_ref, o_ref, z_ref, ps_ref, qs_ref, zs_ref, os_ref):
    r = DFT_R

    def stage1(blk, carry):
        b0 = pl.multiple_of(blk * FOUR_IN_BB, FOUR_IN_BB)
        fb = f_ref[0, :, pl.ds(b0, FOUR_IN_BB), :].reshape(r * FOUR_IN_BB, D_FOURIER)
        _to_slabs(ps_ref, jnp.dot(fb, bdc_ref[...], preferred_element_type=F32))
        _to_slabs(qs_ref, jnp.dot(fb, bds_ref[...], preferred_element_type=F32))
        for j in range(FOUR_IN_BB):
            x2 = jnp.concatenate([_strided_rows(ps_ref, j, FOUR_IN_BB),
                                  _strided_rows(qs_ref, j, FOUR_IN_BB)], axis=0).astype(BF16)
            zz = jnp.dot(m1_ref[b0 + j], x2, preferred_element_type=F32)
            z_ref[b0 + j] = _pack_pair(zz[:r], zz[r:])
        return carry

    lax.fori_loop(0, r // FOUR_IN_BB, stage1, 0)

    def stage2(blk, carry):
        k0 = pl.multiple_of(blk * FOUR_BB, FOUR_BB)
        _to_slabs(zs_ref, z_ref[:, pl.ds(k0, FOUR_BB), :].reshape(r * FOUR_BB, D_FOURIER))
        for j in range(FOUR_BB):
            zc, zs = _unpack_pair_bf16(_strided_rows(zs_ref, j))
            o = jnp.dot(m2_ref[...], jnp.concatenate([zc, zs], axis=0), preferred_element_type=F32)
            for s in range(N_SLABS):
                os_ref[s, pl.ds(j, r, stride=FOUR_BB), :] = o[:, s * LANES:(s + 1) * LANES]
        for s in range(N_SLABS):
            o_ref[0, :, pl.ds(k0, FOUR_BB), s * LANES:(s + 1) * LANES] = os_ref[s].reshape(r, FOUR_BB, LANES)
        return carry

    lax.fori_loop(0, r // FOUR_BB, stage2, 0)


def _fourier(y4, tables):
    bdc, bds, m1, m2 = tables
    b = y4.shape[0]
    r = DFT_R
    slab = lambda dt, bb=FOUR_BB: pltpu.VMEM((N_SLABS, r * bb, LANES), dt)
    const = lambda *shape: pl.BlockSpec(shape, lambda i: (0,) * len(shape))
    return pl.pallas_call(
        _four_kernel,
        out_shape=jax.ShapeDtypeStruct((b, r, r, D_FOURIER), F32),
        grid=(b,),
        in_specs=[pl.BlockSpec((1, r, r, D_FOURIER), lambda i: (i, 0, 0, 0)),
                  const(D_FOURIER, D_FOURIER), const(D_FOURIER, D_FOURIER),
                  const(r, 2 * r, 2 * r), const(r, 2 * r)],
        out_specs=pl.BlockSpec((1, r, r, D_FOURIER), lambda i: (i, 0, 0, 0)),
        scratch_shapes=[pltpu.VMEM((r, r, D_FOURIER), jnp.int32), slab(F32, FOUR_IN_BB), slab(F32, FOUR_IN_BB),
                        slab(jnp.int32), slab(F32)],
        compiler_params=_cparams(("arbitrary",), vmem=FOUR_VMEM_LIMIT),
        name="fourier",
    )(y4, bdc, bds, m1, m2)


OUTPROJ_PARTS = 4


def _outproj_kernel(x_ref, mr_ref, fo_ref, wm_ref, wf_ref, g1_ref, lg_ref, lb_ref,
                    sh_ref, sc_ref, wr_ref, x1_ref, xm_ref, s_ref):
    part = x_ref.shape[0] // OUTPROJ_PARTS
    rows = lambda h: slice(h * part, (h + 1) * part)
    mixer = lambda r: (jnp.dot(mr_ref[r, :], wm_ref[...], preferred_element_type=F32)
                       + jnp.dot(fo_ref[r, :].astype(BF16), wf_ref[...], preferred_element_type=F32))
    mix_next = mixer(rows(0))
    for h in range(OUTPROJ_PARTS):
        r = rows(h)
        mix = mix_next
        if h + 1 < OUTPROJ_PARTS:
            mix_next = mixer(rows(h + 1))
        x1 = _layer_norm(DEEPNORM_ALPHA * x_ref[r, :] + g1_ref[0] * mix) * lg_ref[...] + lb_ref[...]
        x1_ref[r, :] = x1
        xm = (_layer_norm(x1) * (1.0 + sc_ref[0]) + sh_ref[0]).astype(BF16)
        xm_ref[r, :] = _pack_rows(xm)
        logits = jnp.dot(xm, wr_ref[...], preferred_element_type=F32)
        s_ref[r, :] = jax.nn.sigmoid(logits)


def _outproj(x2d, mread, four, w_m, w_f, gate1, ln_g, ln_b, shift2, scale2, w_router, n_per_batch):
    t, d = x2d.shape
    per = n_per_batch // TM
    tokd = lambda w: pl.BlockSpec((TM, w), lambda i: (i, 0))
    full = lambda r, c: pl.BlockSpec((r, c), lambda i: (0, 0))
    mod = pl.BlockSpec((1, 1, d), lambda i: (i // per, 0, 0))
    return pl.pallas_call(
        _outproj_kernel,
        out_shape=(jax.ShapeDtypeStruct((t, d), F32),
                   jax.ShapeDtypeStruct((t, PACK_W), jnp.int32),
                   jax.ShapeDtypeStruct((t, LANES), F32)),
        grid=(t // TM,),
        in_specs=[tokd(d), tokd(D_MLSTM), tokd(D_FOURIER), full(D_MLSTM, d), full(D_FOURIER, d),
                  mod, full(1, d), full(1, d), mod, mod, full(d, LANES)],
        out_specs=(tokd(d), tokd(PACK_W), tokd(LANES)),
        compiler_params=_cparams(("arbitrary",)),
        name="outproj",
    )(x2d, mread, four, w_m, w_f, gate1, ln_g, ln_b, shift2, scale2, w_router)


def _route_kernel(s_ref, b_ref, idx_ref, gw_ref, rank_ref, cnt_ref, carry_ref):
    @pl.when(pl.program_id(0) == 0)
    def _():
        carry_ref[...] = jnp.zeros_like(carry_ref)

    tm = s_ref.shape[0]
    reps = tm // LANES
    s_t = s_ref[...].T[:N_EXPERTS, :]
    e_i = lax.broadcasted_iota(jnp.int32, (N_EXPERTS, tm), 0).astype(F32)
    neg_inf = jnp.float32(-jnp.inf)
    sb = s_t + jnp.tile(b_ref[...], (1, reps))
    picks, sels = [], []
    taken = jnp.zeros((N_EXPERTS, tm), F32)
    for _ in range(TOP_K):
        best = jnp.max(sb, axis=0, keepdims=True)
        pick = jnp.min(jnp.where(sb == best, e_i, float(N_EXPERTS)), axis=0, keepdims=True)
        onehot = e_i == pick
        sels.append(jnp.sum(jnp.where(onehot, s_t, 0.0), axis=0, keepdims=True))
        picks.append(pick)
        sb = jnp.where(onehot, neg_inf, sb)
        taken = taken + onehot.astype(F32)
    u_i = lax.broadcasted_iota(jnp.int32, (tm, tm), 0)
    t_i = lax.broadcasted_iota(jnp.int32, (tm, tm), 1)
    before = (u_i < t_i).astype(BF16)
    taken_bf = taken.astype(BF16)
    rank_all = jnp.tile(carry_ref[...], (1, reps)) + jnp.dot(taken_bf, before, preferred_element_type=F32)
    carry_ref[...] = carry_ref[...] + jnp.dot(taken_bf, jnp.ones((tm, LANES), BF16),
                                              preferred_element_type=F32)
    cnt_ref[...] = carry_ref[...]
    total = sels[0]
    for v in sels[1:]:
        total = total + v
    ranks = [jnp.sum(jnp.where(e_i == picks[j], rank_all, 0.0), axis=0, keepdims=True) for j in range(TOP_K)]
    idx_ref[...] = jnp.concatenate(picks, axis=0).astype(jnp.int32)
    rank_ref[...] = jnp.concatenate(ranks, axis=0).astype(jnp.int32)
    gw_t = jnp.concatenate([ROUTED_SCALE * v / total for v in sels]
                           + [jnp.zeros((LANES - TOP_K, tm), F32)], axis=0)
    gw_ref[...] = gw_t.T[:, :TOP_K]


def _route(scores, bias_rep):
    t = scores.shape[0]
    tok = lambda w: pl.BlockSpec((TM_ROUTE, w), lambda i: (i, 0))
    slot_major = pl.BlockSpec((TOP_K, TM_ROUTE), lambda i: (0, i))
    per_expert = pl.BlockSpec((N_EXPERTS, LANES), lambda i: (0, 0))
    return pl.pallas_call(
        _route_kernel,
        out_shape=(jax.ShapeDtypeStruct((TOP_K, t), jnp.int32),
                   jax.ShapeDtypeStruct((t, TOP_K), F32),
                   jax.ShapeDtypeStruct((TOP_K, t), jnp.int32),
                   jax.ShapeDtypeStruct((N_EXPERTS, LANES), F32)),
        grid=(t // TM_ROUTE,),
        in_specs=[tok(LANES), per_expert],
        out_specs=(slot_major, tok(TOP_K), slot_major, per_expert),
        scratch_shapes=[pltpu.VMEM((N_EXPERTS, LANES), F32)],
        compiler_params=_cparams(("arbitrary",)),
        name="route",
    )(scores, bias_rep)


TN_DEST = 2048


def _dest_kernel(ps_ref, idx_ref, rank_ref, dest_ref):
    idx = idx_ref[...]
    start = jnp.zeros(idx.shape, jnp.int32)
    for e in range(N_EXPERTS):
        start = jnp.where(idx == e, ps_ref[e], start)
    dest_ref[...] = start + rank_ref[...]


def _dest(pad_start, idx_t, rank_t):
    t = idx_t.shape[1]
    blk = lambda: pl.BlockSpec((TOP_K, TN_DEST), lambda i, ps: (0, i))
    grid_spec = pltpu.PrefetchScalarGridSpec(
        num_scalar_prefetch=1, grid=(t // TN_DEST,), in_specs=[blk(), blk()], out_specs=blk())
    return pl.pallas_call(
        _dest_kernel,
        out_shape=jax.ShapeDtypeStruct((TOP_K, t), jnp.int32),
        grid_spec=grid_spec,
        compiler_params=_cparams(("arbitrary",)),
        name="dest",
    )(pad_start, idx_t, rank_t)


def _sc_worker_id(info):
    return lax.axis_index("s") * info.num_cores + lax.axis_index("c")


def _sc_dispatch(xm, dest3, n_rows):
    t, w = xm.shape
    info = plsc.get_sparse_core_info()
    n_workers = info.num_cores * info.num_subcores
    per_worker = t // SC_ROWS // n_workers
    mesh = plsc.VectorSubcoreMesh(core_axis_name="c", subcore_axis_name="s")

    @functools.partial(
        pl.kernel, mesh=mesh,
        out_type=jax.ShapeDtypeStruct((n_rows, w), xm.dtype),
        scratch_types=[pltpu.VMEM((TOP_K, SC_ROWS), jnp.int32), pltpu.VMEM((SC_ROWS, w), xm.dtype)],
        name="sc_dispatch",
    )
    def run(xm_hbm, dest_hbm, xs_hbm, idx_v, rows_v):
        first = _sc_worker_id(info) * per_worker

        @pl.loop(0, per_worker)
        def _(ci):
            chunk = first + ci
            pltpu.sync_copy(dest_hbm.at[chunk], idx_v)
            pltpu.sync_copy(xm_hbm.at[pl.ds(chunk * SC_ROWS, SC_ROWS)], rows_v)
            for j in range(TOP_K):
                pltpu.sync_copy(rows_v, xs_hbm.at[idx_v.at[j]])

    return run(xm, dest3)


def _sc_gather(table, idx2):
    n_chunks, rows = idx2.shape
    w = table.shape[1]
    info = plsc.get_sparse_core_info()
    n_workers = info.num_cores * info.num_subcores
    per_worker = n_chunks // n_workers
    assert per_worker % 2 == 0
    mesh = plsc.VectorSubcoreMesh(core_axis_name="c", subcore_axis_name="s")

    @functools.partial(
        pl.kernel, mesh=mesh,
        out_type=jax.ShapeDtypeStruct((n_chunks * rows, w), table.dtype),
        scratch_types=[pltpu.VMEM((2, rows), jnp.int32), pltpu.VMEM((2, rows, w), table.dtype),
                       pltpu.SemaphoreType.DMA, pltpu.SemaphoreType.DMA],
        name="sc_gather",
    )
    def run(table_hbm, idx_hbm, out_hbm, idx_v, rows_v, sem0, sem1):
        first = _sc_worker_id(info) * per_worker
        sems = (sem0, sem1)

        def gather(b):
            return pltpu.make_async_copy(table_hbm.at[idx_v.at[b]], rows_v.at[b], sems[b])

        def start(chunk, b):
            pltpu.sync_copy(idx_hbm.at[chunk], idx_v.at[b])
            gather(b).start()

        def finish(chunk, b):
            gather(b).wait()
            pltpu.sync_copy(rows_v.at[b], out_hbm.at[pl.ds(chunk * rows, rows)])

        start(first, 0)

        @pl.loop(0, per_worker, step=2)
        def _(ci):
            chunk = first + ci
            start(chunk + 1, 1)
            finish(chunk, 0)

            @pl.when(ci + 2 < per_worker)
            def _():
                start(chunk + 2, 0)

            finish(chunk + 1, 1)

    return run(table, idx2)


def _expert_kernel(first_ref, count_ref, used_ref,
                   xs_hbm, wgu_ref, wd_ref, y_hbm, wgu_bf, wd_bf, xbuf, ybuf, in_sem, out_sem):
    e = pl.program_id(0)
    n_used = used_ref[0]
    first = first_ref[e]
    n_blocks = count_ref[e]

    def rows(g):
        return pl.ds(pl.multiple_of(g * ROW_BLK, ROW_BLK), ROW_BLK)

    def read(g):
        slot = g % EXP_IN_SLOTS
        return pltpu.make_async_copy(xs_hbm.at[rows(g), :], xbuf.at[slot], in_sem.at[slot])

    def write(g):
        slot = g % EXP_OUT_SLOTS
        return pltpu.make_async_copy(ybuf.at[slot], y_hbm.at[rows(g), :], out_sem.at[slot])

    @pl.when(e == 0)
    def _():
        for g0 in range(EXP_AHEAD):
            @pl.when(g0 < n_used)
            def _():
                read(g0).start()

    @pl.when(n_blocks > 0)
    def _():
        wgu_bf[...] = wgu_ref[0].astype(BF16)
        wd_bf[...] = wd_ref[0].astype(BF16)

    def acquire(g):
        @pl.when(g + EXP_AHEAD < n_used)
        def _():
            read(g + EXP_AHEAD).start()

        read(g).wait()

        @pl.when(g >= EXP_OUT_SLOTS)
        def _():
            write(g - EXP_OUT_SLOTS).wait()

    def gate_up(g):
        return jnp.dot(_unpack_bf16(xbuf[g % EXP_IN_SLOTS]), wgu_bf[...], preferred_element_type=F32)

    def down(g, gu):
        act = _silu(gu[:, :D_EXPERT]) * gu[:, D_EXPERT:]
        yv = jnp.dot(act.astype(BF16), wd_bf[...], preferred_element_type=F32)
        words = pltpu.pack_elementwise([yv[:, :PACK_W], yv[:, PACK_W:]], packed_dtype=BF16)
        ybuf[g % EXP_OUT_SLOTS] = lax.bitcast_convert_type(words, jnp.int32)

    def handle(g, n):
        for i in range(n):
            acquire(g + i)
        gu = gate_up(g)
        for i in range(n):
            gu_next = gate_up(g + i + 1) if i + 1 < n else None
            down(g + i, gu)
            gu = gu_next
        for i in range(n):
            write(g + i).start()

    def group(k, carry):
        handle(first + EXP_GROUP * k, EXP_GROUP)
        return carry

    lax.fori_loop(0, n_blocks // EXP_GROUP, group, 0)

    n = EXP_GROUP // 2
    while n >= 1:
        @pl.when(n_blocks % (2 * n) >= n)
        def _(n=n):
            handle(first + n_blocks - n_blocks % (2 * n), n)
        n //= 2

    @pl.when(e == pl.num_programs(0) - 1)
    def _():
        for back in range(EXP_OUT_SLOTS, 0, -1):
            @pl.when(n_used >= back)
            def _():
                write(n_used - back).wait()


def _experts(first_block, n_blocks, n_used, xs, w_gu, w_down):
    n_rows, w = xs.shape
    d = D_MODEL
    grid_spec = pltpu.PrefetchScalarGridSpec(
        num_scalar_prefetch=3,
        grid=(N_EXPERTS,),
        in_specs=[pl.BlockSpec(memory_space=pl.ANY),
                  pl.BlockSpec((1, d, 2 * D_EXPERT), lambda e, *_: (e, 0, 0)),
                  pl.BlockSpec((1, D_EXPERT, d), lambda e, *_: (e, 0, 0))],
        out_specs=pl.BlockSpec(memory_space=pl.ANY),
        scratch_shapes=[pltpu.VMEM((d, 2 * D_EXPERT), BF16), pltpu.VMEM((D_EXPERT, d), BF16),
                        pltpu.VMEM((EXP_IN_SLOTS, ROW_BLK, w), jnp.int32),
                        pltpu.VMEM((EXP_OUT_SLOTS, ROW_BLK, w), jnp.int32),
                        pltpu.SemaphoreType.DMA((EXP_IN_SLOTS,)),
                        pltpu.SemaphoreType.DMA((EXP_OUT_SLOTS,))],
    )
    return pl.pallas_call(
        _expert_kernel,
        out_shape=jax.ShapeDtypeStruct((n_rows, w), jnp.int32),
        grid_spec=grid_spec,
        compiler_params=_cparams(("arbitrary",)),
        name="experts",
    )(first_block, n_blocks, n_used, xs, w_gu, w_down)


TM_COMBINE = 512


def _combine_kernel(*refs):
    yg_refs = refs[:TOP_K]
    gw_ref, x1_ref, xm_ref, g2_ref, wsgu_ref, wsd_ref, lg_ref, lb_ref, out_ref = refs[TOP_K:]
    gu = jnp.dot(_unpack_bf16(xm_ref[...]), wsgu_ref[...], preferred_element_type=F32)
    act = _silu(gu[:, :D_EXPERT]) * gu[:, D_EXPERT:]
    acc = jnp.dot(act.astype(BF16), wsd_ref[...], preferred_element_type=F32)
    gw = gw_ref[...]
    for j in range(TOP_K):
        acc = acc + gw[:, j:j + 1] * _unpack_rows(yg_refs[j][...])
    z = DEEPNORM_ALPHA * x1_ref[...] + g2_ref[0] * acc
    out_ref[...] = _layer_norm(z) * lg_ref[...] + lb_ref[...]


def _combine(yg, gw, x1, xm, gate2, w_sgu, w_sd, ln_g, ln_b, n_per_batch):
    t, d = x1.shape
    tm = TM_COMBINE
    per = n_per_batch // tm
    steps = t // tm
    tok = lambda w: pl.BlockSpec((tm, w), lambda i: (i, 0))
    full = lambda r, c: pl.BlockSpec((r, c), lambda i: (0, 0))
    slot = lambda j: pl.BlockSpec((tm, PACK_W), lambda i: (j * steps + i, 0))
    return pl.pallas_call(
        _combine_kernel,
        out_shape=jax.ShapeDtypeStruct((t, d), F32),
        grid=(steps,),
        in_specs=[slot(j) for j in range(TOP_K)]
        + [tok(TOP_K), tok(d), tok(PACK_W), pl.BlockSpec((1, 1, d), lambda i: (i // per, 0, 0)),
           full(d, 2 * D_EXPERT), full(D_EXPERT, d), full(1, d), full(1, d)],
        out_specs=tok(d),
        compiler_params=_cparams(("arbitrary",)),
        name="combine",
    )(*([yg] * TOP_K), gw, x1, xm, gate2, w_sgu, w_sd, ln_g, ln_b)


def _mixer_heads(stream, shift, scale, w_in_k, conv_w9, conv_b, gate_b_row, head_g, c0, m0, grid2d):
    b, n, d = stream.shape
    y, f_in = _inproj(stream.reshape(b * n, d), shift, scale, w_in_k, n)
    y = y.reshape(b, n, PROJ_W)
    qk = _conv(y, conv_w9, conv_b, grid2d)
    mread, c_fin, m_fin = _mlstm(qk, y, _gates(y, gate_b_row), c0, m0, head_g)
    return f_in, mread, c_fin, m_fin


def _moe(xm, x1, scores, bias_row, gate2, w_gu, w_down, w_sgu, w_sd, ln_g, ln_b, n_per_batch):
    t = xm.shape[0]
    n_assign = t * TOP_K
    n_rows = n_assign + N_EXPERTS * ROW_BLK
    idx_t, gw, rank_t, counts = _route(scores, bias_row)
    sizes = counts[:, 0].astype(jnp.int32)
    padded = (sizes + ROW_BLK - 1) // ROW_BLK * ROW_BLK
    pad_end = jnp.cumsum(padded)
    pad_start = (pad_end - padded).astype(jnp.int32)
    dest_t = _dest(pad_start, idx_t, rank_t)
    dest3 = dest_t.reshape(TOP_K, t // SC_ROWS, SC_ROWS).transpose(1, 0, 2)

    xs = _sc_dispatch(xm, dest3, n_rows)
    yb = _experts(pad_start // ROW_BLK, (padded // ROW_BLK).astype(jnp.int32),
                  (pad_end[-1:] // ROW_BLK).astype(jnp.int32), xs, w_gu, w_down)
    yg = _sc_gather(yb, dest_t.reshape(n_assign // SC_GATHER_ROWS, SC_GATHER_ROWS))
    return _combine(yg, gw, x1, xm, gate2, w_sgu, w_sd, ln_g, ln_b, n_per_batch)


def kernel(x, c, ctx, c_ctx, w_ada, b_ada, w_in, conv_w, conv_b, gate_b, head_g, w_out, ln1_g, ln1_b,
           w_router, router_bias, w_expert_gu, w_expert_down, w_shared_gu, w_shared_down, ln2_g, ln2_b):
    bsz, n, d = x.shape
    depth = w_ada.shape[0]
    assert depth == 1
    t = bsz * n
    tables = _dft_tables()
    for l in range(depth):
        cvec = jnp.zeros((SUBLANES, d), F32).at[:bsz].set(c).at[bsz].set(c_ctx)
        mod = _ada(cvec, w_ada[l], b_ada[l][None, :])
        mods = mod.reshape(SUBLANES, 6, 1, d)
        mod_x = [mods[:bsz, j] for j in range(6)]
        mod_c = [mods[bsz:bsz + 1, j] for j in range(2)]

        wl = w_in[l]
        g0 = 4 * D_MLSTM
        w_in_k = jnp.concatenate(
            [wl[:, :g0 + N_GATE_COLS], jnp.zeros((d, LANES - N_GATE_COLS), F32),
             wl[:, g0 + N_GATE_COLS:]], axis=1).astype(BF16)
        conv_w9 = conv_w[l].reshape(9, 2 * D_MLSTM)
        conv_b_row = conv_b[l][None, :]
        gate_b_row = jnp.zeros((1, LANES), F32).at[0, :N_GATE_COLS].set(gate_b[l].reshape(-1))
        head_g_row = head_g[l][None, :]

        c0 = jnp.zeros((bsz, N_HEADS, 2, STATE_ROWS, D_HEAD), F32)
        m0 = jnp.zeros((bsz, N_HEADS, 2, SUBLANES, LANES), F32)
        _, _, c_ctx_fin, m_ctx_fin = _mixer_heads(ctx, mod_c[0], mod_c[1], w_in_k, conv_w9, conv_b_row,
                                                  gate_b_row, head_g_row, c0, m0, grid2d=False)
        f_in, mread, _, _ = _mixer_heads(x, mod_x[0], mod_x[1], w_in_k, conv_w9, conv_b_row,
                                      gate_b_row, head_g_row, c_ctx_fin, m_ctx_fin, grid2d=True)

        four = _fourier(f_in.reshape(bsz, DFT_R, DFT_R, D_FOURIER), tables).reshape(t, D_FOURIER)

        w_o = w_out[l].astype(BF16)
        w_r = jnp.zeros((d, LANES), F32).at[:, :N_EXPERTS].set(w_router[l]).astype(BF16)
        x2d = x.reshape(t, d)
        x1, xm, scores = _outproj(x2d, mread.reshape(t, D_MLSTM), four, w_o[:D_MLSTM], w_o[D_MLSTM:],
                                  mod_x[2], ln1_g[l][None, :], ln1_b[l][None, :], mod_x[3], mod_x[4],
                                  w_r, n)

        bias_rep = jnp.broadcast_to(router_bias[l][:, None], (N_EXPERTS, LANES))
        x = _moe(xm, x1, scores, bias_rep, mod_x[5], w_expert_gu[l], w_expert_down[l],
                 w_shared_gu[l].astype(BF16), w_shared_down[l].astype(BF16),
                 ln2_g[l][None, :], ln2_b[l][None, :], n).reshape(bsz, n, d)
    return x
```

```python
import functools

import numpy as np
import jax
import jax.numpy as jnp
from jax import lax
from jax.experimental import pallas as pl
from jax.experimental.pallas import tpu as pltpu
from jax.experimental.pallas import tpu_sc as plsc

F32 = jnp.float32
BF16 = jnp.bfloat16
HIGHEST = lax.Precision.HIGHEST

D_MODEL = 1024
GRID_W = 64
N_HEADS = 4
D_HEAD = 128
D_MLSTM = N_HEADS * D_HEAD
D_FOURIER = 512
FOURIER_GROUP = 64
N_GATE_COLS = 16
CHUNK = 128
N_EXPERTS = 64
TOP_K = 8
D_EXPERT = 256
ROUTED_SCALE = 2.5
DEEPNORM_ALPHA = 2.0 ** 0.25
LN_EPS = 1e-5

LANES = 128
SUBLANES = 8
VMEM_LIMIT = 48 * 1024 * 1024
FOUR_VMEM_LIMIT = 56 * 1024 * 1024

COL_QK = 0
COL_V = 2 * D_MLSTM
COL_O = 3 * D_MLSTM
COL_G = 4 * D_MLSTM
PROJ_W = COL_G + LANES

TN_ADA = 2048
TM = 1024
TM_ROUTE = 1024
ROW_BLK = 256
EXP_GROUP = 8
EXP_AHEAD = 10
EXP_IN_SLOTS = 32
EXP_OUT_SLOTS = 2 * EXP_GROUP
PACK_W = D_MODEL // 2
SC_ROWS = 128
SC_GATHER_ROWS = 64
CONV_PAD = 72
CONV_RB = 256
DFT_R = 64


def _cparams(sem, vmem=VMEM_LIMIT):
    return pltpu.CompilerParams(dimension_semantics=sem, vmem_limit_bytes=vmem)


def _layer_norm(x):
    mu = jnp.mean(x, axis=-1, keepdims=True)
    xc = x - mu
    var = jnp.mean(xc * xc, axis=-1, keepdims=True)
    return xc * lax.rsqrt(var + LN_EPS)


def _silu(x):
    return x * jax.nn.sigmoid(x)


def _pack_pair(lo, hi):
    as_bits = lambda u: lax.bitcast_convert_type(u.astype(BF16).astype(F32), jnp.uint32)
    word = as_bits(hi) | lax.shift_right_logical(as_bits(lo), jnp.uint32(16))
    return lax.bitcast_convert_type(word, jnp.int32)


def _unpack_pair_bf16(w):
    lo = lax.bitcast_convert_type(w.astype(jnp.int16), BF16)
    hi = lax.bitcast_convert_type(lax.shift_right_logical(w, jnp.int32(16)).astype(jnp.int16), BF16)
    return lo, hi


def _pack_rows(v):
    return _pack_pair(v[:, :PACK_W], v[:, PACK_W:])


def _unpack_bf16(w):
    return jnp.concatenate(_unpack_pair_bf16(w), axis=-1)


def _unpack_rows(w):
    u = lax.bitcast_convert_type(w, jnp.uint32)
    lo = lax.bitcast_convert_type(lax.shift_left(u, jnp.uint32(16)), F32)
    hi = lax.bitcast_convert_type(u & jnp.uint32(0xFFFF0000), F32)
    return jnp.concatenate([lo, hi], axis=1)


def _ada_kernel(c_ref, w_ref, b_ref, o_ref):
    s = _silu(c_ref[...]).astype(BF16)
    o_ref[...] = jnp.dot(s, w_ref[...].astype(BF16), preferred_element_type=F32) + b_ref[...]


def _ada(cvec, w_ada, b_ada):
    rows, d = cvec.shape
    n_out = w_ada.shape[1]
    tn = TN_ADA
    return pl.pallas_call(
        _ada_kernel,
        out_shape=jax.ShapeDtypeStruct((rows, n_out), F32),
        grid=(n_out // tn,),
        in_specs=[pl.BlockSpec((rows, d), lambda j: (0, 0)),
                  pl.BlockSpec((d, tn), lambda j: (0, j)),
                  pl.BlockSpec((1, tn), lambda j: (0, j))],
        out_specs=pl.BlockSpec((rows, tn), lambda j: (0, j)),
        compiler_params=_cparams(("arbitrary",)),
        name="ada",
    )(cvec, w_ada, b_ada)


def _inproj_kernel(x_ref, sh_ref, sc_ref, w_ref, y_ref, f_ref):
    half = x_ref.shape[0] // 2
    for h in range(2):
        r = slice(h * half, (h + 1) * half)
        u = _layer_norm(x_ref[r, :]) * (1.0 + sc_ref[0]) + sh_ref[0]
        proj = jnp.dot(u.astype(BF16), w_ref[...], preferred_element_type=F32)
        y_ref[r, :] = proj[:, :PROJ_W]
        f_ref[r, :] = proj[:, PROJ_W:]


def _inproj(x2d, shift, scale, w, n_per_batch):
    t, d = x2d.shape
    if shift.shape[0] == 1:
        tm = min(TM, t)
        per = t // tm
    else:
        tm = min(TM, n_per_batch)
        per = n_per_batch // tm
    return pl.pallas_call(
        _inproj_kernel,
        out_shape=(jax.ShapeDtypeStruct((t, PROJ_W), F32), jax.ShapeDtypeStruct((t, D_FOURIER), F32)),
        grid=(t // tm,),
        in_specs=[pl.BlockSpec((tm, d), lambda i: (i, 0)),
                  pl.BlockSpec((1, 1, d), lambda i: (i // per, 0, 0)),
                  pl.BlockSpec((1, 1, d), lambda i: (i // per, 0, 0)),
                  pl.BlockSpec((d, PROJ_W + D_FOURIER), lambda i: (0, 0))],
        out_specs=(pl.BlockSpec((tm, PROJ_W), lambda i: (i, 0)),
                   pl.BlockSpec((tm, D_FOURIER), lambda i: (i, 0))),
        compiler_params=_cparams(("arbitrary",)),
        name="inproj",
    )(x2d, shift, scale, w)


def _conv_kernel(y_ref, w_ref, b_ref, o_ref, pad_ref, *, n, grid2d, tiles):
    zeros = jnp.zeros((CONV_PAD, LANES), F32)
    pad_ref[0:CONV_PAD, :] = zeros
    pad_ref[CONV_PAD + n:CONV_PAD + n + CONV_PAD, :] = zeros
    rb = min(CONV_RB, n)
    col = lax.broadcasted_iota(jnp.int32, (rb, LANES), 0) % GRID_W
    not_first = col >= 1
    not_last = col <= GRID_W - 2
    row_taps = (0, 1, 2) if grid2d else (1,)
    for ct in range(tiles):
        lanes = slice(ct * LANES, (ct + 1) * LANES)
        pad_ref[CONV_PAD:CONV_PAD + n, :] = y_ref[0, :, lanes]
        k_scale = jnp.where(pl.program_id(1) * tiles + ct >= N_HEADS, D_HEAD ** -0.5, 1.0).astype(F32)
        bias = b_ref[:, lanes]
        for blk in range(n // rb):
            r0 = CONV_PAD + blk * rb
            acc = jnp.zeros((rb, LANES), F32)
            for dj in range(3):
                part = jnp.zeros((rb, LANES), F32)
                for di in row_taps:
                    off = (di - 1) * GRID_W + (dj - 1)
                    tap = w_ref[di * 3 + dj:di * 3 + dj + 1, lanes]
                    part = part + tap * pad_ref[r0 + off:r0 + off + rb, :]
                if grid2d and dj == 0:
                    part = jnp.where(not_first, part, 0.0)
                if grid2d and dj == 2:
                    part = jnp.where(not_last, part, 0.0)
                acc = acc + part
            val = _silu(acc + bias) * k_scale
            o_ref[0, blk * rb:(blk + 1) * rb, lanes] = val.astype(o_ref.dtype)


def _conv(y3d, conv_w9, conv_b, grid2d):
    b, n, _ = y3d.shape
    n_ct = 2 * D_MLSTM // LANES
    tiles = 1 if grid2d else n_ct
    w = tiles * LANES
    return pl.pallas_call(
        functools.partial(_conv_kernel, n=n, grid2d=grid2d, tiles=tiles),
        out_shape=jax.ShapeDtypeStruct((b, n, 2 * D_MLSTM), BF16),
        grid=(b, n_ct // tiles),
        in_specs=[pl.BlockSpec((1, n, w), lambda i, c: (i, 0, c)),
                  pl.BlockSpec((9, w), lambda i, c: (0, c)),
                  pl.BlockSpec((1, w), lambda i, c: (0, c))],
        out_specs=pl.BlockSpec((1, n, w), lambda i, c: (i, 0, c)),
        scratch_shapes=[pltpu.VMEM((n + 2 * CONV_PAD, LANES), F32)],
        compiler_params=_cparams(("arbitrary", "arbitrary")),
        name="conv2d" if grid2d else "conv1d",
    )(y3d, conv_w9, conv_b)


STATE_ROWS = D_HEAD + 32
GATE_A, GATE_CM, GATE_B = 0, 1, 2
GATE_GROUP = 32


def _gates_kernel(g_ref, gb_ref, rv_ref, *, nc):
    u_i = lax.broadcasted_iota(jnp.int32, (CHUNK, CHUNK), 0)
    t_i = lax.broadcasted_iota(jnp.int32, (CHUNK, CHUNK), 1)
    prefix = (u_i <= t_i).astype(F32)
    suffix = (u_i >= t_i).astype(F32)
    grp = min(GATE_GROUP, nc)
    rows = grp * SUBLANES
    lane = lax.broadcasted_iota(jnp.int32, (rows, LANES), 1)
    is_fwd = lax.broadcasted_iota(jnp.int32, (rows, LANES), 0) % SUBLANES < N_HEADS
    h = N_HEADS

    def running_max(a):
        fwd, bwd = a, a
        sh = 1
        while sh < CHUNK:
            fwd = jnp.where(lane >= sh, jnp.maximum(fwd, pltpu.roll(fwd, sh, axis=1)), fwd)
            bwd = jnp.where(lane < CHUNK - sh, jnp.maximum(bwd, pltpu.roll(bwd, CHUNK - sh, axis=1)), bwd)
            sh *= 2
        return jnp.where(is_fwd, fwd, bwd)

    def body(i, carry):
        c0 = i * grp
        r0 = pl.multiple_of(c0 * CHUNK, CHUNK)
        pre = g_ref[0, pl.ds(r0, grp * CHUNK), :] + gb_ref[...]
        pre_t = jnp.concatenate(
            [pre[j * CHUNK:(j + 1) * CHUNK].T[:N_GATE_COLS, :] for j in range(grp)], axis=0)
        lf_t = -(jnp.maximum(-pre_t, 0.0) + jnp.log1p(jnp.exp(-jnp.abs(pre_t))))
        cum_f = jnp.dot(lf_t, prefix, precision=HIGHEST, preferred_element_type=F32)
        cum_b = jnp.dot(lf_t, suffix, precision=HIGHEST, preferred_element_type=F32)
        pick = lambda t, j, q: t[j * N_GATE_COLS + q * h:j * N_GATE_COLS + (q + 1) * h]
        b8 = jnp.concatenate([x for j in range(grp) for x in (pick(cum_f, j, 1), pick(cum_b, j, 3))], axis=0)
        li8 = jnp.concatenate([x for j in range(grp) for x in (pick(pre_t, j, 0), pick(pre_t, j, 2))], axis=0)
        a8 = li8 - b8
        as_chunks = lambda t: t.reshape(grp, SUBLANES, LANES)
        rv_ref[0, pl.ds(c0, grp), GATE_A] = as_chunks(a8)
        rv_ref[0, pl.ds(c0, grp), GATE_CM] = as_chunks(running_max(a8))
        rv_ref[0, pl.ds(c0, grp), GATE_B] = as_chunks(b8)
        return carry

    lax.fori_loop(0, nc // grp, body, 0)


def _gates(y3d, gate_b_row):
    b, n, _ = y3d.shape
    nc = n // CHUNK
    return pl.pallas_call(
        functools.partial(_gates_kernel, nc=nc),
        out_shape=jax.ShapeDtypeStruct((b, nc, 3, SUBLANES, LANES), F32),
        grid=(b,),
        in_specs=[pl.BlockSpec((1, n, LANES), lambda i: (i, 0, COL_G // LANES)),
                  pl.BlockSpec((1, LANES), lambda i: (0, 0))],
        out_specs=pl.BlockSpec((1, nc, 3, SUBLANES, LANES), lambda i: (i, 0, 0, 0, 0)),
        compiler_params=_cparams(("arbitrary",)),
        name="gates",
    )(y3d, gate_b_row)


def _mlstm_kernel(q_ref, k_ref, v_ref, o_ref, rv_ref, c0_ref, m0_ref, hg_ref,
                  out_ref, cfin_ref, mfin_ref, hf_ref, hb_ref, cst_ref, mst_ref, *, nc, heads):
    cst_ref[...] = c0_ref[0]
    mst_ref[...] = m0_ref[0]
    s_i = lax.broadcasted_iota(jnp.int32, (CHUNK, CHUNK), 0)
    t_i = lax.broadcasted_iota(jnp.int32, (CHUNK, CHUNK), 1)
    ones_row = (s_i[:STATE_ROWS - D_HEAD] == 0).astype(F32)

    def one_chunk(c, d, mask_t, last, h_ref, hh):
        r0 = pl.multiple_of(c * CHUNK, CHUNK)
        lanes = slice(hh * D_HEAD, (hh + 1) * D_HEAD)
        g = d * N_HEADS + pl.program_id(1) * heads + hh
        gate_row = lambda tbl: rv_ref[0, pl.ds(c, 1), tbl, pl.ds(g, 1), :].reshape(1, LANES)
        a_row, cm_row, b_row = gate_row(GATE_A), gate_row(GATE_CM), gate_row(GATE_B)
        q_t = q_ref[0, pl.ds(r0, CHUNK), lanes].T
        k = k_ref[0, pl.ds(r0, CHUNK), lanes]
        v_t_aug = jnp.concatenate([v_ref[0, pl.ds(r0, CHUNK), lanes].T, ones_row], axis=0)
        m_prev = mst_ref[hh, d, 0:1, :]
        m_row = jnp.maximum(m_prev, cm_row)
        a_bc = jnp.broadcast_to(a_row, (CHUNK, CHUNK)).T
        d_t = jnp.where(mask_t, jnp.exp(a_bc - m_row), 0.0)
        s_t = jnp.dot(k, q_t, preferred_element_type=F32) * d_t
        cs = cst_ref[hh, d]
        rhs = jnp.concatenate(
            [s_t.astype(BF16), (q_t.astype(F32) * jnp.exp(m_prev - m_row)).astype(BF16)], axis=0)
        lhs = jnp.concatenate([v_t_aug.astype(BF16), cs.astype(BF16)], axis=1)
        h_t = jnp.dot(lhs, rhs, preferred_element_type=F32)
        den = h_t[D_HEAD:D_HEAD + 1, :]
        floor = jnp.exp(-b_row - m_row)
        h_ref[hh, pl.ds(c, 1)] = (h_t[:D_HEAD] / jnp.maximum(jnp.abs(den), floor))[None]
        m_last = m_row[:, last:last + 1]
        wv_t = (v_t_aug * jnp.exp(a_row - m_last)).astype(BF16)
        cst_ref[hh, d] = jnp.exp(m_prev - m_last) * cs + jnp.dot(wv_t, k, preferred_element_type=F32)
        mst_ref[hh, d] = jnp.broadcast_to(b_row[:, last:last + 1] + m_last, (SUBLANES, LANES))

    def body(c, carry):
        for hh in range(heads):
            one_chunk(c, 0, s_i <= t_i, CHUNK - 1, hf_ref, hh)
            one_chunk(nc - 1 - c, 1, s_i >= t_i, 0, hb_ref, hh)
        return carry

    lax.fori_loop(0, nc, body, 0, unroll=min(16 // heads, nc))
    cfin_ref[0] = cst_ref[...]
    mfin_ref[0] = mst_ref[...]

    def readout(c, carry):
        r0 = pl.multiple_of(c * CHUNK, CHUNK)
        for hh in range(heads):
            lanes = slice(hh * D_HEAD, (hh + 1) * D_HEAD)
            h_t = hf_ref[hh, pl.ds(c, 1)][0] + hb_ref[hh, pl.ds(c, 1)][0]
            hc = h_t - jnp.mean(h_t, axis=0, keepdims=True)
            hn = (hc * lax.rsqrt(jnp.mean(hc * hc, axis=0, keepdims=True) + LN_EPS)).T
            gate = jax.nn.sigmoid(o_ref[0, pl.ds(r0, CHUNK), lanes])
            out_ref[0, pl.ds(r0, CHUNK), lanes] = (hn * hg_ref[:, lanes] * gate).astype(out_ref.dtype)
        return carry

    lax.fori_loop(0, nc, readout, 0, unroll=min(16, nc))


def _mlstm(qk, y3d, rowv, c0, m0, head_g):
    b, n, _ = qk.shape
    nc = n // CHUNK
    heads = N_HEADS if nc <= 4 else 1
    w = heads * D_HEAD
    v_blk = COL_V // w
    o_blk = COL_O // w
    tok = lambda off: pl.BlockSpec((1, n, w), lambda i, h: (i, 0, off + h))
    state_c = pl.BlockSpec((1, heads, 2, STATE_ROWS, D_HEAD), lambda i, h: (i, h, 0, 0, 0))
    state_m = pl.BlockSpec((1, heads, 2, SUBLANES, LANES), lambda i, h: (i, h, 0, 0, 0))
    return pl.pallas_call(
        functools.partial(_mlstm_kernel, nc=nc, heads=heads),
        out_shape=(jax.ShapeDtypeStruct((b, n, D_MLSTM), BF16),
                   jax.ShapeDtypeStruct((b, N_HEADS, 2, STATE_ROWS, D_HEAD), F32),
                   jax.ShapeDtypeStruct((b, N_HEADS, 2, SUBLANES, LANES), F32)),
        grid=(b, N_HEADS // heads),
        in_specs=[tok(0), tok(D_MLSTM // w), tok(v_blk), tok(o_blk),
                  pl.BlockSpec((1, nc, 3, SUBLANES, LANES), lambda i, h: (i, 0, 0, 0, 0)),
                  state_c, state_m,
                  pl.BlockSpec((1, w), lambda i, h: (0, h))],
        out_specs=(pl.BlockSpec((1, n, w), lambda i, h: (i, 0, h)), state_c, state_m),
        scratch_shapes=[pltpu.VMEM((heads, nc, D_HEAD, CHUNK), F32), pltpu.VMEM((heads, nc, D_HEAD, CHUNK), F32),
                        pltpu.VMEM((heads, 2, STATE_ROWS, D_HEAD), F32),
                        pltpu.VMEM((heads, 2, SUBLANES, LANES), F32)],
        compiler_params=_cparams(("arbitrary", "arbitrary")),
        name="mlstm",
    )(qk, qk, y3d, y3d, rowv, c0, m0, head_g)


def _dft_tables():
    r = DFT_R
    idx = np.arange(r)
    cg = np.cos(2 * np.pi * np.outer(idx, idx) / FOURIER_GROUP)
    sg = np.sin(2 * np.pi * np.outer(idx, idx) / FOURIER_GROUP)
    n_grp = D_FOURIER // FOURIER_GROUP
    bdc = np.kron(np.eye(n_grp), cg) / 8.0
    bds = np.kron(np.eye(n_grp), sg) / 8.0
    k1 = idx[:, None]
    a = idx[None, :]
    m1 = np.zeros((r, 2 * r, 2 * r))
    for b2 in range(r):
        th = 2 * np.pi * k1 * (r * a + b2) / (r * r)
        ec, es = np.cos(th), np.sin(th)
        m1[b2] = np.block([[ec, -es], [es, ec]]) / 8.0
    th2 = 2 * np.pi * np.outer(idx, idx) / r
    m2 = np.concatenate([np.cos(th2), -np.sin(th2)], axis=1) / 8.0
    as_bf16 = lambda t: jnp.asarray(t, dtype=F32).astype(BF16)
    return as_bf16(bdc), as_bf16(bds), as_bf16(m1), as_bf16(m2)


FOUR_BB = 8


N_SLABS = D_FOURIER // LANES


def _to_slabs(slab_ref, val):
    for s in range(N_SLABS):
        slab_ref[s] = val[:, s * LANES:(s + 1) * LANES]


def _strided_rows(slab_ref, start):
    return jnp.concatenate(
        [slab_ref[s, pl.ds(start, DFT_R, stride=FOUR_BB), :] for s in range(N_SLABS)], axis=1)


def _four_kernel(f_ref, bdc_ref, bds_ref, m1_ref, m2_ref, o_ref, z_ref, ps_ref, qs_ref, zs_ref, os_ref):
    r = DFT_R

    def stage1(blk, carry):
        b0 = pl.multiple_of(blk * FOUR_BB, FOUR_BB)
        fb = f_ref[0, :, pl.ds(b0, FOUR_BB), :].reshape(r * FOUR_BB, D_FOURIER).astype(BF16)
        _to_slabs(ps_ref, jnp.dot(fb, bdc_ref[...], preferred_element_type=F32))
        _to_slabs(qs_ref, jnp.dot(fb, bds_ref[...], preferred_element_type=F32))
        for j in range(FOUR_BB):
            x2 = jnp.concatenate([_strided_rows(ps_ref, j), _strided_rows(qs_ref, j)], axis=0).astype(BF16)
            zz = jnp.dot(m1_ref[b0 + j], x2, preferred_element_type=F32)
            z_ref[b0 + j] = _pack_pair(zz[:r], zz[r:])
        return carry

    lax.fori_loop(0, r // FOUR_BB, stage1, 0)

    def stage2(blk, carry):
        k0 = pl.multiple_of(blk * FOUR_BB, FOUR_BB)
        _to_slabs(zs_ref, z_ref[:, pl.ds(k0, FOUR_BB), :].reshape(r * FOUR_BB, D_FOURIER))
        for j in range(FOUR_BB):
            zc, zs = _unpack_pair_bf16(_strided_rows(zs_ref, j))
            o = jnp.dot(m2_ref[...], jnp.concatenate([zc, zs], axis=0), preferred_element_type=F32)
            for s in range(N_SLABS):
                os_ref[s, pl.ds(j, r, stride=FOUR_BB), :] = o[:, s * LANES:(s + 1) * LANES]
        for s in range(N_SLABS):
            o_ref[0, :, pl.ds(k0, FOUR_BB), s * LANES:(s + 1) * LANES] = os_ref[s].reshape(r, FOUR_BB, LANES)
        return carry

    lax.fori_loop(0, r // FOUR_BB, stage2, 0)


def _fourier(y4, tables):
    bdc, bds, m1, m2 = tables
    b = y4.shape[0]
    r = DFT_R
    slab = lambda dt: pltpu.VMEM((N_SLABS, r * FOUR_BB, LANES), dt)
    const = lambda *shape: pl.BlockSpec(shape, lambda i: (0,) * len(shape))
    return pl.pallas_call(
        _four_kernel,
        out_shape=jax.ShapeDtypeStruct((b, r, r, D_FOURIER), F32),
        grid=(b,),
        in_specs=[pl.BlockSpec((1, r, r, D_FOURIER), lambda i: (i, 0, 0, 0)),
                  const(D_FOURIER, D_FOURIER), const(D_FOURIER, D_FOURIER),
                  const(r, 2 * r, 2 * r), const(r, 2 * r)],
        out_specs=pl.BlockSpec((1, r, r, D_FOURIER), lambda i: (i, 0, 0, 0)),
        scratch_shapes=[pltpu.VMEM((r, r, D_FOURIER), jnp.int32), slab(F32), slab(F32), slab(jnp.int32), slab(F32)],
        compiler_params=_cparams(("arbitrary",), vmem=FOUR_VMEM_LIMIT),
        name="fourier",
    )(y4, bdc, bds, m1, m2)


OUTPROJ_PARTS = 2


def _outproj_kernel(x_ref, mr_ref, fo_ref, wm_ref, wf_ref, g1_ref, lg_ref, lb_ref,
                    sh_ref, sc_ref, wr_ref, g2_ref, wsgu_ref, wsd_ref, res_ref, xm_ref, s_ref):
    part = x_ref.shape[0] // OUTPROJ_PARTS
    rows = lambda h: slice(h * part, (h + 1) * part)

    def mixer(h):
        r = rows(h)
        return (jnp.dot(mr_ref[r, :], wm_ref[...], preferred_element_type=F32)
                + jnp.dot(fo_ref[r, :].astype(BF16), wf_ref[...], preferred_element_type=F32))

    def norms(h, mix):
        r = rows(h)
        x1 = _layer_norm(DEEPNORM_ALPHA * x_ref[r, :] + g1_ref[0] * mix) * lg_ref[...] + lb_ref[...]
        xm = (_layer_norm(x1) * (1.0 + sc_ref[0]) + sh_ref[0]).astype(BF16)
        xm_ref[r, :] = _pack_rows(xm)
        s_ref[r, :] = jax.nn.sigmoid(jnp.dot(xm, wr_ref[...], preferred_element_type=F32))
        return x1, jnp.dot(xm, wsgu_ref[...], preferred_element_type=F32)

    def residual(h, x1, gu):
        act = _silu(gu[:, :D_EXPERT]) * gu[:, D_EXPERT:]
        shared = jnp.dot(act.astype(BF16), wsd_ref[...], preferred_element_type=F32)
        res_ref[rows(h), :] = DEEPNORM_ALPHA * x1 + g2_ref[0] * shared

    n = OUTPROJ_PARTS
    mix = {0: mixer(0)}
    mid = {}
    for h in range(n + 1):
        if h + 1 < n:
            mix[h + 1] = mixer(h + 1)
        if h < n:
            mid[h] = norms(h, mix.pop(h))
        if h >= 1:
            residual(h - 1, *mid.pop(h - 1))


def _outproj(x2d, mread, four, w_m, w_f, gate1, ln_g, ln_b, shift2, scale2, w_router, gate2, w_sgu, w_sd,
             n_per_batch):
    t, d = x2d.shape
    per = n_per_batch // TM
    tokd = lambda w: pl.BlockSpec((TM, w), lambda i: (i, 0))
    full = lambda r, c: pl.BlockSpec((r, c), lambda i: (0, 0))
    mod = pl.BlockSpec((1, 1, d), lambda i: (i // per, 0, 0))
    return pl.pallas_call(
        _outproj_kernel,
        out_shape=(jax.ShapeDtypeStruct((t, d), F32),
                   jax.ShapeDtypeStruct((t, PACK_W), jnp.int32),
                   jax.ShapeDtypeStruct((t, LANES), F32)),
        grid=(t // TM,),
        in_specs=[tokd(d), tokd(D_MLSTM), tokd(D_FOURIER), full(D_MLSTM, d), full(D_FOURIER, d),
                  mod, full(1, d), full(1, d), mod, mod, full(d, LANES),
                  mod, full(d, 2 * D_EXPERT), full(D_EXPERT, d)],
        out_specs=(tokd(d), tokd(PACK_W), tokd(LANES)),
        compiler_params=_cparams(("arbitrary",)),
        name="outproj",
    )(x2d, mread, four, w_m, w_f, gate1, ln_g, ln_b, shift2, scale2, w_router, gate2, w_sgu, w_sd)


def _route_kernel(s_ref, b_ref, idx_ref, gw_ref, rank_ref, cnt_ref, carry_ref):
    @pl.when(pl.program_id(0) == 0)
    def _():
        carry_ref[...] = jnp.zeros_like(carry_ref)

    tm = s_ref.shape[0]
    reps = tm // LANES
    s_t = s_ref[...].T[:N_EXPERTS, :]
    e_i = lax.broadcasted_iota(jnp.int32, (N_EXPERTS, tm), 0).astype(F32)
    neg_inf = jnp.float32(-jnp.inf)
    sb = s_t + jnp.tile(b_ref[...], (1, reps))
    picks, sels = [], []
    taken = jnp.zeros((N_EXPERTS, tm), F32)
    for _ in range(TOP_K):
        best = jnp.max(sb, axis=0, keepdims=True)
        pick = jnp.min(jnp.where(sb == best, e_i, float(N_EXPERTS)), axis=0, keepdims=True)
        onehot = e_i == pick
        sels.append(jnp.sum(jnp.where(onehot, s_t, 0.0), axis=0, keepdims=True))
        picks.append(pick)
        sb = jnp.where(onehot, neg_inf, sb)
        taken = taken + onehot.astype(F32)
    u_i = lax.broadcasted_iota(jnp.int32, (tm, tm), 0)
    t_i = lax.broadcasted_iota(jnp.int32, (tm, tm), 1)
    before = (u_i < t_i).astype(BF16)
    taken_bf = taken.astype(BF16)
    rank_all = jnp.tile(carry_ref[...], (1, reps)) + jnp.dot(taken_bf, before, preferred_element_type=F32)
    carry_ref[...] = carry_ref[...] + jnp.dot(taken_bf, jnp.ones((tm, LANES), BF16),
                                              preferred_element_type=F32)
    cnt_ref[...] = carry_ref[...]
    total = sels[0]
    for v in sels[1:]:
        total = total + v
    ranks = [jnp.sum(jnp.where(e_i == picks[j], rank_all, 0.0), axis=0, keepdims=True) for j in range(TOP_K)]
    idx_ref[...] = jnp.concatenate(picks, axis=0).astype(jnp.int32)
    rank_ref[...] = jnp.concatenate(ranks, axis=0).astype(jnp.int32)
    gw_t = jnp.concatenate([ROUTED_SCALE * v / total for v in sels]
                           + [jnp.zeros((LANES - TOP_K, tm), F32)], axis=0)
    gw_ref[...] = gw_t.T[:, :TOP_K]


def _route(scores, bias_rep):
    t = scores.shape[0]
    tok = lambda w: pl.BlockSpec((TM_ROUTE, w), lambda i: (i, 0))
    slot_major = pl.BlockSpec((TOP_K, TM_ROUTE), lambda i: (0, i))
    per_expert = pl.BlockSpec((N_EXPERTS, LANES), lambda i: (0, 0))
    return pl.pallas_call(
        _route_kernel,
        out_shape=(jax.ShapeDtypeStruct((TOP_K, t), jnp.int32),
                   jax.ShapeDtypeStruct((t, TOP_K), F32),
                   jax.ShapeDtypeStruct((TOP_K, t), jnp.int32),
                   jax.ShapeDtypeStruct((N_EXPERTS, LANES), F32)),
        grid=(t // TM_ROUTE,),
        in_specs=[tok(LANES), per_expert],
        out_specs=(slot_major, tok(TOP_K), slot_major, per_expert),
        scratch_shapes=[pltpu.VMEM((N_EXPERTS, LANES), F32)],
        compiler_params=_cparams(("arbitrary",)),
        name="route",
    )(scores, bias_rep)


TN_DEST = 2048


def _dest_kernel(ps_ref, idx_ref, rank_ref, dest_ref):
    idx = idx_ref[...]
    start = jnp.zeros(idx.shape, jnp.int32)
    for e in range(N_EXPERTS):
        start = jnp.where(idx == e, ps_ref[e], start)
    dest_ref[...] = start + rank_ref[...]


def _dest(pad_start, idx_t, rank_t):
    t = idx_t.shape[1]
    blk = lambda: pl.BlockSpec((TOP_K, TN_DEST), lambda i, ps: (0, i))
    grid_spec = pltpu.PrefetchScalarGridSpec(
        num_scalar_prefetch=1, grid=(t // TN_DEST,), in_specs=[blk(), blk()], out_specs=blk())
    return pl.pallas_call(
        _dest_kernel,
        out_shape=jax.ShapeDtypeStruct((TOP_K, t), jnp.int32),
        grid_spec=grid_spec,
        compiler_params=_cparams(("arbitrary",)),
        name="dest",
    )(pad_start, idx_t, rank_t)


def _sc_worker_id(info):
    return lax.axis_index("s") * info.num_cores + lax.axis_index("c")


def _sc_dispatch(xm, dest3, n_rows):
    t, w = xm.shape
    info = plsc.get_sparse_core_info()
    n_workers = info.num_cores * info.num_subcores
    per_worker = t // SC_ROWS // n_workers
    mesh = plsc.VectorSubcoreMesh(core_axis_name="c", subcore_axis_name="s")

    @functools.partial(
        pl.kernel, mesh=mesh,
        out_type=jax.ShapeDtypeStruct((n_rows, w), xm.dtype),
        scratch_types=[pltpu.VMEM((TOP_K, SC_ROWS), jnp.int32), pltpu.VMEM((SC_ROWS, w), xm.dtype)],
        name="sc_dispatch",
    )
    def run(xm_hbm, dest_hbm, xs_hbm, idx_v, rows_v):
        first = _sc_worker_id(info) * per_worker

        @pl.loop(0, per_worker)
        def _(ci):
            chunk = first + ci
            pltpu.sync_copy(dest_hbm.at[chunk], idx_v)
            pltpu.sync_copy(xm_hbm.at[pl.ds(chunk * SC_ROWS, SC_ROWS)], rows_v)
            for j in range(TOP_K):
                pltpu.sync_copy(rows_v, xs_hbm.at[idx_v.at[j]])

    return run(xm, dest3)


def _sc_gather(table, idx2):
    n_chunks, rows = idx2.shape
    w = table.shape[1]
    info = plsc.get_sparse_core_info()
    n_workers = info.num_cores * info.num_subcores
    per_worker = n_chunks // n_workers
    assert per_worker % 2 == 0
    mesh = plsc.VectorSubcoreMesh(core_axis_name="c", subcore_axis_name="s")

    @functools.partial(
        pl.kernel, mesh=mesh,
        out_type=jax.ShapeDtypeStruct((n_chunks * rows, w), table.dtype),
        scratch_types=[pltpu.VMEM((2, rows), jnp.int32), pltpu.VMEM((2, rows, w), table.dtype),
                       pltpu.SemaphoreType.DMA, pltpu.SemaphoreType.DMA],
        name="sc_gather",
    )
    def run(table_hbm, idx_hbm, out_hbm, idx_v, rows_v, sem0, sem1):
        first = _sc_worker_id(info) * per_worker
        sems = (sem0, sem1)

        def gather(b):
            return pltpu.make_async_copy(table_hbm.at[idx_v.at[b]], rows_v.at[b], sems[b])

        def start(chunk, b):
            pltpu.sync_copy(idx_hbm.at[chunk], idx_v.at[b])
            gather(b).start()

        def finish(chunk, b):
            gather(b).wait()
            pltpu.sync_copy(rows_v.at[b], out_hbm.at[pl.ds(chunk * rows, rows)])

        start(first, 0)

        @pl.loop(0, per_worker, step=2)
        def _(ci):
            chunk = first + ci
            start(chunk + 1, 1)
            finish(chunk, 0)

            @pl.when(ci + 2 < per_worker)
            def _():
                start(chunk + 2, 0)

            finish(chunk + 1, 1)

    return run(table, idx2)


def _expert_kernel(first_ref, count_ref, used_ref,
                   xs_hbm, wgu_ref, wd_ref, y_hbm, wgu_bf, wd_bf, xbuf, ybuf, in_sem, out_sem):
    e = pl.program_id(0)
    n_used = used_ref[0]
    first = first_ref[e]
    n_blocks = count_ref[e]

    def rows(g):
        return pl.ds(pl.multiple_of(g * ROW_BLK, ROW_BLK), ROW_BLK)

    def read(g):
        slot = g % EXP_IN_SLOTS
        return pltpu.make_async_copy(xs_hbm.at[rows(g), :], xbuf.at[slot], in_sem.at[slot])

    def write(g):
        slot = g % EXP_OUT_SLOTS
        return pltpu.make_async_copy(ybuf.at[slot], y_hbm.at[rows(g), :], out_sem.at[slot])

    @pl.when(e == 0)
    def _():
        for g0 in range(EXP_AHEAD):
            @pl.when(g0 < n_used)
            def _():
                read(g0).start()

    @pl.when(n_blocks > 0)
    def _():
        wgu_bf[...] = wgu_ref[0].astype(BF16)
        wd_bf[...] = wd_ref[0].astype(BF16)

    def acquire(g):
        @pl.when(g + EXP_AHEAD < n_used)
        def _():
            read(g + EXP_AHEAD).start()

        read(g).wait()

        @pl.when(g >= EXP_OUT_SLOTS)
        def _():
            write(g - EXP_OUT_SLOTS).wait()

    def gate_up(g):
        return jnp.dot(_unpack_bf16(xbuf[g % EXP_IN_SLOTS]), wgu_bf[...], preferred_element_type=F32)

    def down(g, gu):
        act = _silu(gu[:, :D_EXPERT]) * gu[:, D_EXPERT:]
        yv = jnp.dot(act.astype(BF16), wd_bf[...], preferred_element_type=F32)
        words = pltpu.pack_elementwise([yv[:, :PACK_W], yv[:, PACK_W:]], packed_dtype=BF16)
        ybuf[g % EXP_OUT_SLOTS] = lax.bitcast_convert_type(words, jnp.int32)

    def handle(g, n):
        for i in range(n):
            acquire(g + i)
        gu = gate_up(g)
        for i in range(n):
            gu_next = gate_up(g + i + 1) if i + 1 < n else None
            down(g + i, gu)
            gu = gu_next
        for i in range(n):
            write(g + i).start()

    def group(k, carry):
        handle(first + EXP_GROUP * k, EXP_GROUP)
        return carry

    lax.fori_loop(0, n_blocks // EXP_GROUP, group, 0)

    n = EXP_GROUP // 2
    while n >= 1:
        @pl.when(n_blocks % (2 * n) >= n)
        def _(n=n):
            handle(first + n_blocks - n_blocks % (2 * n), n)
        n //= 2

    @pl.when(e == pl.num_programs(0) - 1)
    def _():
        for back in range(EXP_OUT_SLOTS, 0, -1):
            @pl.when(n_used >= back)
            def _():
                write(n_used - back).wait()


def _experts(first_block, n_blocks, n_used, xs, w_gu, w_down):
    n_rows, w = xs.shape
    d = D_MODEL
    grid_spec = pltpu.PrefetchScalarGridSpec(
        num_scalar_prefetch=3,
        grid=(N_EXPERTS,),
        in_specs=[pl.BlockSpec(memory_space=pl.ANY),
                  pl.BlockSpec((1, d, 2 * D_EXPERT), lambda e, *_: (e, 0, 0)),
                  pl.BlockSpec((1, D_EXPERT, d), lambda e, *_: (e, 0, 0))],
        out_specs=pl.BlockSpec(memory_space=pl.ANY),
        scratch_shapes=[pltpu.VMEM((d, 2 * D_EXPERT), BF16), pltpu.VMEM((D_EXPERT, d), BF16),
                        pltpu.VMEM((EXP_IN_SLOTS, ROW_BLK, w), jnp.int32),
                        pltpu.VMEM((EXP_OUT_SLOTS, ROW_BLK, w), jnp.int32),
                        pltpu.SemaphoreType.DMA((EXP_IN_SLOTS,)),
                        pltpu.SemaphoreType.DMA((EXP_OUT_SLOTS,))],
    )
    return pl.pallas_call(
        _expert_kernel,
        out_shape=jax.ShapeDtypeStruct((n_rows, w), jnp.int32),
        grid_spec=grid_spec,
        compiler_params=_cparams(("arbitrary",)),
        name="experts",
    )(first_block, n_blocks, n_used, xs, w_gu, w_down)


TM_COMBINE = 512


def _combine_kernel(*refs):
    yg_refs = refs[:TOP_K]
    gw_ref, res_ref, g2_ref, lg_ref, lb_ref, out_ref = refs[TOP_K:]
    gw = gw_ref[...]
    acc = gw[:, 0:1] * _unpack_rows(yg_refs[0][...])
    for j in range(1, TOP_K):
        acc = acc + gw[:, j:j + 1] * _unpack_rows(yg_refs[j][...])
    z = res_ref[...] + g2_ref[0] * acc
    out_ref[...] = _layer_norm(z) * lg_ref[...] + lb_ref[...]


def _combine(yg, gw, res, gate2, ln_g, ln_b, n_per_batch):
    t, d = res.shape
    tm = TM_COMBINE
    per = n_per_batch // tm
    steps = t // tm
    tok = lambda w: pl.BlockSpec((tm, w), lambda i: (i, 0))
    full = lambda r, c: pl.BlockSpec((r, c), lambda i: (0, 0))
    slot = lambda j: pl.BlockSpec((tm, PACK_W), lambda i: (j * steps + i, 0))
    return pl.pallas_call(
        _combine_kernel,
        out_shape=jax.ShapeDtypeStruct((t, d), F32),
        grid=(steps,),
        in_specs=[slot(j) for j in range(TOP_K)]
        + [tok(TOP_K), tok(d), pl.BlockSpec((1, 1, d), lambda i: (i // per, 0, 0)), full(1, d), full(1, d)],
        out_specs=tok(d),
        compiler_params=_cparams(("arbitrary",)),
        name="combine",
    )(*([yg] * TOP_K), gw, res, gate2, ln_g, ln_b)


def _mixer_heads(stream, shift, scale, w_in_k, conv_w9, conv_b, gate_b_row, head_g, c0, m0, grid2d):
    b, n, d = stream.shape
    y, f_in = _inproj(stream.reshape(b * n, d), shift, scale, w_in_k, n)
    y = y.reshape(b, n, PROJ_W)
    qk = _conv(y, conv_w9, conv_b, grid2d)
    mread, c_fin, m_fin = _mlstm(qk, y, _gates(y, gate_b_row), c0, m0, head_g)
    return f_in, mread, c_fin, m_fin


def _moe(xm, res, scores, bias_row, gate2, w_gu, w_down, ln_g, ln_b, n_per_batch):
    t = xm.shape[0]
    n_assign = t * TOP_K
    n_rows = n_assign + N_EXPERTS * ROW_BLK
    idx_t, gw, rank_t, counts = _route(scores, bias_row)
    sizes = counts[:, 0].astype(jnp.int32)
    padded = (sizes + ROW_BLK - 1) // ROW_BLK * ROW_BLK
    pad_end = jnp.cumsum(padded)
    pad_start = (pad_end - padded).astype(jnp.int32)
    dest_t = _dest(pad_start, idx_t, rank_t)
    dest3 = dest_t.reshape(TOP_K, t // SC_ROWS, SC_ROWS).transpose(1, 0, 2)

    xs = _sc_dispatch(xm, dest3, n_rows)
    yb = _experts(pad_start // ROW_BLK, (padded // ROW_BLK).astype(jnp.int32),
                  (pad_end[-1:] // ROW_BLK).astype(jnp.int32), xs, w_gu, w_down)
    yg = _sc_gather(yb, dest_t.reshape(n_assign // SC_GATHER_ROWS, SC_GATHER_ROWS))
    return _combine(yg, gw, res, gate2, ln_g, ln_b, n_per_batch)


def kernel(x, c, ctx, c_ctx, w_ada, b_ada, w_in, conv_w, conv_b, gate_b, head_g, w_out, ln1_g, ln1_b,
           w_router, router_bias, w_expert_gu, w_expert_down, w_shared_gu, w_shared_down, ln2_g, ln2_b):
    bsz, n, d = x.shape
    depth = w_ada.shape[0]
    assert depth == 1
    t = bsz * n
    tables = _dft_tables()
    for l in range(depth):
        cvec = jnp.zeros((SUBLANES, d), F32).at[:bsz].set(c).at[bsz].set(c_ctx)
        mod = _ada(cvec, w_ada[l], b_ada[l][None, :])
        mods = mod.reshape(SUBLANES, 6, 1, d)
        mod_x = [mods[:bsz, j] for j in range(6)]
        mod_c = [mods[bsz:bsz + 1, j] for j in range(2)]

        wl = w_in[l]
        g0 = 4 * D_MLSTM
        w_in_k = jnp.concatenate(
            [wl[:, :g0 + N_GATE_COLS], jnp.zeros((d, LANES - N_GATE_COLS), F32),
             wl[:, g0 + N_GATE_COLS:]], axis=1).astype(BF16)
        conv_w9 = conv_w[l].reshape(9, 2 * D_MLSTM)
        conv_b_row = conv_b[l][None, :]
        gate_b_row = jnp.zeros((1, LANES), F32).at[0, :N_GATE_COLS].set(gate_b[l].reshape(-1))
        head_g_row = head_g[l][None, :]

        c0 = jnp.zeros((bsz, N_HEADS, 2, STATE_ROWS, D_HEAD), F32)
        m0 = jnp.zeros((bsz, N_HEADS, 2, SUBLANES, LANES), F32)
        _, _, c_ctx_fin, m_ctx_fin = _mixer_heads(ctx, mod_c[0], mod_c[1], w_in_k, conv_w9, conv_b_row,
                                                  gate_b_row, head_g_row, c0, m0, grid2d=False)
        f_in, mread, _, _ = _mixer_heads(x, mod_x[0], mod_x[1], w_in_k, conv_w9, conv_b_row,
                                      gate_b_row, head_g_row, c_ctx_fin, m_ctx_fin, grid2d=True)

        four = _fourier(f_in.reshape(bsz, DFT_R, DFT_R, D_FOURIER), tables).reshape(t, D_FOURIER)

        w_o = w_out[l].astype(BF16)
        w_r = jnp.zeros((d, LANES), F32).at[:, :N_EXPERTS].set(w_router[l]).astype(BF16)
        x2d = x.reshape(t, d)
        res, xm, scores = _outproj(x2d, mread.reshape(t, D_MLSTM), four, w_o[:D_MLSTM], w_o[D_MLSTM:],
                                   mod_x[2], ln1_g[l][None, :], ln1_b[l][None, :], mod_x[3], mod_x[4],
                                   w_r, mod_x[5], w_shared_gu[l].astype(BF16), w_shared_down[l].astype(BF16), n)

        bias_rep = jnp.broadcast_to(router_bias[l][:, None], (N_EXPERTS, LANES))
        x = _moe(xm, res, scores, bias_rep, mod_x[5], w_expert_gu[l], w_expert_down[l],
                 ln2_g[l][None, :], ln2_b[l][None, :], n).reshape(bsz, n, d)
    return x
```

```python
import functools

import numpy as np
import jax
import jax.numpy as jnp
from jax import lax
from jax.experimental import pallas as pl
from jax.experimental.pallas import tpu as pltpu
from jax.experimental.pallas import tpu_sc as plsc

F32 = jnp.float32
BF16 = jnp.bfloat16
HIGHEST = lax.Precision.HIGHEST

D_MODEL = 1024
GRID_W = 64
N_HEADS = 4
D_HEAD = 128
D_MLSTM = N_HEADS * D_HEAD
D_FOURIER = 512
FOURIER_GROUP = 64
N_GATE_COLS = 16
CHUNK = 128
N_EXPERTS = 64
TOP_K = 8
D_EXPERT = 256
ROUTED_SCALE = 2.5
DEEPNORM_ALPHA = 2.0 ** 0.25
LN_EPS = 1e-5

LANES = 128
SUBLANES = 8
VMEM_LIMIT = 48 * 1024 * 1024
FOUR_VMEM_LIMIT = 56 * 1024 * 1024

COL_QK = 0
COL_V = 2 * D_MLSTM
COL_O = 3 * D_MLSTM
COL_G = 4 * D_MLSTM
PROJ_W = COL_G + LANES

TN_ADA = 2048
TM = 1024
TM_ROUTE = 1024
ROW_BLK = 256
EXP_GROUP = 8
EXP_AHEAD = 10
EXP_IN_SLOTS = 32
EXP_OUT_SLOTS = 2 * EXP_GROUP
PACK_W = D_MODEL // 2
SC_ROWS = 128
SC_GATHER_ROWS = 64
CONV_PAD = 72
CONV_RB = 256
DFT_R = 64


def _cparams(sem, vmem=VMEM_LIMIT):
    return pltpu.CompilerParams(dimension_semantics=sem, vmem_limit_bytes=vmem)


def _layer_norm(x):
    mu = jnp.mean(x, axis=-1, keepdims=True)
    xc = x - mu
    var = jnp.mean(xc * xc, axis=-1, keepdims=True)
    return xc * lax.rsqrt(var + LN_EPS)


def _silu(x):
    return x * jax.nn.sigmoid(x)


def _pack_pair(lo, hi):
    as_bits = lambda u: lax.bitcast_convert_type(u.astype(BF16).astype(F32), jnp.uint32)
    word = as_bits(hi) | lax.shift_right_logical(as_bits(lo), jnp.uint32(16))
    return lax.bitcast_convert_type(word, jnp.int32)


def _unpack_pair_bf16(w):
    lo = lax.bitcast_convert_type(w.astype(jnp.int16), BF16)
    hi = lax.bitcast_convert_type(lax.shift_right_logical(w, jnp.int32(16)).astype(jnp.int16), BF16)
    return lo, hi


def _pack_rows(v):
    return _pack_pair(v[:, :PACK_W], v[:, PACK_W:])


def _unpack_bf16(w):
    return jnp.concatenate(_unpack_pair_bf16(w), axis=-1)


def _unpack_rows(w):
    u = lax.bitcast_convert_type(w, jnp.uint32)
    lo = lax.bitcast_convert_type(lax.shift_left(u, jnp.uint32(16)), F32)
    hi = lax.bitcast_convert_type(u & jnp.uint32(0xFFFF0000), F32)
    return jnp.concatenate([lo, hi], axis=1)


def _ada_kernel(c_ref, w_ref, b_ref, o_ref):
    s = _silu(c_ref[...]).astype(BF16)
    o_ref[...] = jnp.dot(s, w_ref[...].astype(BF16), preferred_element_type=F32) + b_ref[...]


def _ada(cvec, w_ada, b_ada):
    rows, d = cvec.shape
    n_out = w_ada.shape[1]
    tn = TN_ADA
    return pl.pallas_call(
        _ada_kernel,
        out_shape=jax.ShapeDtypeStruct((rows, n_out), F32),
        grid=(n_out // tn,),
        in_specs=[pl.BlockSpec((rows, d), lambda j: (0, 0)),
                  pl.BlockSpec((d, tn), lambda j: (0, j)),
                  pl.BlockSpec((1, tn), lambda j: (0, j))],
        out_specs=pl.BlockSpec((rows, tn), lambda j: (0, j)),
        compiler_params=_cparams(("arbitrary",)),
        name="ada",
    )(cvec, w_ada, b_ada)


W_IN_ROWS = 256


def _w_in_kernel(w_ref, o_ref):
    g1 = COL_G + N_GATE_COLS
    o_ref[:, :COL_G] = w_ref[0, :, :COL_G].astype(BF16)
    gates = jnp.concatenate(
        [w_ref[0, :, COL_G:g1], jnp.zeros((W_IN_ROWS, LANES - N_GATE_COLS), F32)], axis=1)
    o_ref[:, COL_G:PROJ_W] = gates.astype(BF16)
    o_ref[:, PROJ_W:] = w_ref[0, :, g1:].astype(BF16)


def _w_in_layout(w_in, layer):
    _, d, cols = w_in.shape
    return pl.pallas_call(
        _w_in_kernel,
        out_shape=jax.ShapeDtypeStruct((d, PROJ_W + D_FOURIER), BF16),
        grid=(d // W_IN_ROWS,),
        in_specs=[pl.BlockSpec((1, W_IN_ROWS, cols), lambda i: (layer, i, 0))],
        out_specs=pl.BlockSpec((W_IN_ROWS, PROJ_W + D_FOURIER), lambda i: (i, 0)),
        compiler_params=_cparams(("arbitrary",)),
        name="w_in_layout",
    )(w_in)


def _inproj_kernel(x_ref, sh_ref, sc_ref, w_ref, y_ref, f_ref):
    half = x_ref.shape[0] // 2
    for h in range(2):
        r = slice(h * half, (h + 1) * half)
        u = _layer_norm(x_ref[r, :]) * (1.0 + sc_ref[0]) + sh_ref[0]
        proj = jnp.dot(u.astype(BF16), w_ref[...], preferred_element_type=F32)
        y_ref[r, :] = proj[:, :PROJ_W]
        f_ref[r, :] = proj[:, PROJ_W:]


def _inproj(x2d, shift, scale, w, n_per_batch):
    t, d = x2d.shape
    if shift.shape[0] == 1:
        tm = min(TM, t)
        per = t // tm
    else:
        tm = min(TM, n_per_batch)
        per = n_per_batch // tm
    return pl.pallas_call(
        _inproj_kernel,
        out_shape=(jax.ShapeDtypeStruct((t, PROJ_W), F32), jax.ShapeDtypeStruct((t, D_FOURIER), F32)),
        grid=(t // tm,),
        in_specs=[pl.BlockSpec((tm, d), lambda i: (i, 0)),
                  pl.BlockSpec((1, 1, d), lambda i: (i // per, 0, 0)),
                  pl.BlockSpec((1, 1, d), lambda i: (i // per, 0, 0)),
                  pl.BlockSpec((d, PROJ_W + D_FOURIER), lambda i: (0, 0))],
        out_specs=(pl.BlockSpec((tm, PROJ_W), lambda i: (i, 0)),
                   pl.BlockSpec((tm, D_FOURIER), lambda i: (i, 0))),
        compiler_params=_cparams(("arbitrary",)),
        name="inproj",
    )(x2d, shift, scale, w)


def _conv_kernel(y_ref, w_ref, b_ref, o_ref, pad_ref, *, n, grid2d, tiles):
    zeros = jnp.zeros((CONV_PAD, LANES), F32)
    pad_ref[0:CONV_PAD, :] = zeros
    pad_ref[CONV_PAD + n:CONV_PAD + n + CONV_PAD, :] = zeros
    rb = min(CONV_RB, n)
    col = lax.broadcasted_iota(jnp.int32, (rb, LANES), 0) % GRID_W
    not_first = col >= 1
    not_last = col <= GRID_W - 2
    row_taps = (0, 1, 2) if grid2d else (1,)
    for ct in range(tiles):
        lanes = slice(ct * LANES, (ct + 1) * LANES)
        pad_ref[CONV_PAD:CONV_PAD + n, :] = y_ref[0, :, lanes]
        k_scale = jnp.where(pl.program_id(1) * tiles + ct >= N_HEADS, D_HEAD ** -0.5, 1.0).astype(F32)
        bias = b_ref[:, lanes]
        for blk in range(n // rb):
            r0 = CONV_PAD + blk * rb
            acc = jnp.zeros((rb, LANES), F32)
            for dj in range(3):
                part = jnp.zeros((rb, LANES), F32)
                for di in row_taps:
                    off = (di - 1) * GRID_W + (dj - 1)
                    tap = w_ref[di * 3 + dj:di * 3 + dj + 1, lanes]
                    part = part + tap * pad_ref[r0 + off:r0 + off + rb, :]
                if grid2d and dj == 0:
                    part = jnp.where(not_first, part, 0.0)
                if grid2d and dj == 2:
                    part = jnp.where(not_last, part, 0.0)
                acc = acc + part
            val = _silu(acc + bias) * k_scale
            o_ref[0, blk * rb:(blk + 1) * rb, lanes] = val.astype(o_ref.dtype)


def _conv(y3d, conv_w9, conv_b, grid2d):
    b, n, _ = y3d.shape
    n_ct = 2 * D_MLSTM // LANES
    tiles = 1 if grid2d else n_ct
    w = tiles * LANES
    return pl.pallas_call(
        functools.partial(_conv_kernel, n=n, grid2d=grid2d, tiles=tiles),
        out_shape=jax.ShapeDtypeStruct((b, n, 2 * D_MLSTM), BF16),
        grid=(b, n_ct // tiles),
        in_specs=[pl.BlockSpec((1, n, w), lambda i, c: (i, 0, c)),
                  pl.BlockSpec((9, w), lambda i, c: (0, c)),
                  pl.BlockSpec((1, w), lambda i, c: (0, c))],
        out_specs=pl.BlockSpec((1, n, w), lambda i, c: (i, 0, c)),
        scratch_shapes=[pltpu.VMEM((n + 2 * CONV_PAD, LANES), F32)],
        compiler_params=_cparams(("arbitrary", "arbitrary")),
        name="conv2d" if grid2d else "conv1d",
    )(y3d, conv_w9, conv_b)


STATE_ROWS = D_HEAD + 32
GATE_A, GATE_CM, GATE_B = 0, 1, 2
GATE_GROUP = 32


def _gates_kernel(g_ref, gb_ref, rv_ref, *, nc):
    u_i = lax.broadcasted_iota(jnp.int32, (CHUNK, CHUNK), 0)
    t_i = lax.broadcasted_iota(jnp.int32, (CHUNK, CHUNK), 1)
    prefix = (u_i <= t_i).astype(F32)
    suffix = (u_i >= t_i).astype(F32)
    grp = min(GATE_GROUP, nc)
    rows = grp * SUBLANES
    lane = lax.broadcasted_iota(jnp.int32, (rows, LANES), 1)
    is_fwd = lax.broadcasted_iota(jnp.int32, (rows, LANES), 0) % SUBLANES < N_HEADS
    h = N_HEADS

    def running_max(a):
        fwd, bwd = a, a
        sh = 1
        while sh < CHUNK:
            fwd = jnp.where(lane >= sh, jnp.maximum(fwd, pltpu.roll(fwd, sh, axis=1)), fwd)
            bwd = jnp.where(lane < CHUNK - sh, jnp.maximum(bwd, pltpu.roll(bwd, CHUNK - sh, axis=1)), bwd)
            sh *= 2
        return jnp.where(is_fwd, fwd, bwd)

    def body(i, carry):
        c0 = i * grp
        r0 = pl.multiple_of(c0 * CHUNK, CHUNK)
        pre = g_ref[0, pl.ds(r0, grp * CHUNK), :] + gb_ref[...]
        pre_t = jnp.concatenate(
            [pre[j * CHUNK:(j + 1) * CHUNK].T[:N_GATE_COLS, :] for j in range(grp)], axis=0)
        lf_t = -(jnp.maximum(-pre_t, 0.0) + jnp.log1p(jnp.exp(-jnp.abs(pre_t))))
        cum_f = jnp.dot(lf_t, prefix, precision=HIGHEST, preferred_element_type=F32)
        cum_b = jnp.dot(lf_t, suffix, precision=HIGHEST, preferred_element_type=F32)
        pick = lambda t, j, q: t[j * N_GATE_COLS + q * h:j * N_GATE_COLS + (q + 1) * h]
        b8 = jnp.concatenate([x for j in range(grp) for x in (pick(cum_f, j, 1), pick(cum_b, j, 3))], axis=0)
        li8 = jnp.concatenate([x for j in range(grp) for x in (pick(pre_t, j, 0), pick(pre_t, j, 2))], axis=0)
        a8 = li8 - b8
        as_chunks = lambda t: t.reshape(grp, SUBLANES, LANES)
        rv_ref[0, pl.ds(c0, grp), GATE_A] = as_chunks(a8)
        rv_ref[0, pl.ds(c0, grp), GATE_CM] = as_chunks(running_max(a8))
        rv_ref[0, pl.ds(c0, grp), GATE_B] = as_chunks(b8)
        return carry

    lax.fori_loop(0, nc // grp, body, 0)


def _gates(y3d, gate_b_row):
    b, n, _ = y3d.shape
    nc = n // CHUNK
    return pl.pallas_call(
        functools.partial(_gates_kernel, nc=nc),
        out_shape=jax.ShapeDtypeStruct((b, nc, 3, SUBLANES, LANES), F32),
        grid=(b,),
        in_specs=[pl.BlockSpec((1, n, LANES), lambda i: (i, 0, COL_G // LANES)),
                  pl.BlockSpec((1, LANES), lambda i: (0, 0))],
        out_specs=pl.BlockSpec((1, nc, 3, SUBLANES, LANES), lambda i: (i, 0, 0, 0, 0)),
        compiler_params=_cparams(("arbitrary",)),
        name="gates",
    )(y3d, gate_b_row)


def _mlstm_kernel(q_ref, k_ref, v_ref, o_ref, rv_ref, c0_ref, m0_ref, hg_ref,
                  out_ref, cfin_ref, mfin_ref, hf_ref, hb_ref, cst_ref, mst_ref, *, nc, heads):
    cst_ref[...] = c0_ref[0]
    mst_ref[...] = m0_ref[0]
    s_i = lax.broadcasted_iota(jnp.int32, (CHUNK, CHUNK), 0)
    t_i = lax.broadcasted_iota(jnp.int32, (CHUNK, CHUNK), 1)
    ones_row = (s_i[:STATE_ROWS - D_HEAD] == 0).astype(F32)

    def one_chunk(c, d, mask_t, last, h_ref, hh):
        r0 = pl.multiple_of(c * CHUNK, CHUNK)
        lanes = slice(hh * D_HEAD, (hh + 1) * D_HEAD)
        g = d * N_HEADS + pl.program_id(1) * heads + hh
        gate_row = lambda tbl: rv_ref[0, pl.ds(c, 1), tbl, pl.ds(g, 1), :].reshape(1, LANES)
        a_row, cm_row, b_row = gate_row(GATE_A), gate_row(GATE_CM), gate_row(GATE_B)
        q_t = q_ref[0, pl.ds(r0, CHUNK), lanes].T
        k = k_ref[0, pl.ds(r0, CHUNK), lanes]
        v_t_aug = jnp.concatenate([v_ref[0, pl.ds(r0, CHUNK), lanes].T, ones_row], axis=0)
        m_prev = mst_ref[hh, d, 0:1, :]
        m_row = jnp.maximum(m_prev, cm_row)
        a_bc = jnp.broadcast_to(a_row, (CHUNK, CHUNK)).T
        d_t = jnp.where(mask_t, jnp.exp(a_bc - m_row), 0.0)
        s_t = jnp.dot(k, q_t, preferred_element_type=F32) * d_t
        cs = cst_ref[hh, d]
        rhs = jnp.concatenate(
            [s_t.astype(BF16), (q_t.astype(F32) * jnp.exp(m_prev - m_row)).astype(BF16)], axis=0)
        lhs = jnp.concatenate([v_t_aug.astype(BF16), cs.astype(BF16)], axis=1)
        h_t = jnp.dot(lhs, rhs, preferred_element_type=F32)
        den = h_t[D_HEAD:D_HEAD + 1, :]
        floor = jnp.exp(-b_row - m_row)
        h_ref[hh, pl.ds(c, 1)] = (h_t[:D_HEAD] / jnp.maximum(jnp.abs(den), floor))[None]
        m_last = m_row[:, last:last + 1]
        wv_t = (v_t_aug * jnp.exp(a_row - m_last)).astype(BF16)
        cst_ref[hh, d] = jnp.exp(m_prev - m_last) * cs + jnp.dot(wv_t, k, preferred_element_type=F32)
        mst_ref[hh, d] = jnp.broadcast_to(b_row[:, last:last + 1] + m_last, (SUBLANES, LANES))

    def body(c, carry):
        for hh in range(heads):
            one_chunk(c, 0, s_i <= t_i, CHUNK - 1, hf_ref, hh)
            one_chunk(nc - 1 - c, 1, s_i >= t_i, 0, hb_ref, hh)
        return carry

    lax.fori_loop(0, nc, body, 0, unroll=min(16 // heads, nc))
    cfin_ref[0] = cst_ref[...]
    mfin_ref[0] = mst_ref[...]

    def readout(c, carry):
        r0 = pl.multiple_of(c * CHUNK, CHUNK)
        for hh in range(heads):
            lanes = slice(hh * D_HEAD, (hh + 1) * D_HEAD)
            h_t = hf_ref[hh, pl.ds(c, 1)][0] + hb_ref[hh, pl.ds(c, 1)][0]
            hc = h_t - jnp.mean(h_t, axis=0, keepdims=True)
            hn = (hc * lax.rsqrt(jnp.mean(hc * hc, axis=0, keepdims=True) + LN_EPS)).T
            gate = jax.nn.sigmoid(o_ref[0, pl.ds(r0, CHUNK), lanes])
            out_ref[0, pl.ds(r0, CHUNK), lanes] = (hn * hg_ref[:, lanes] * gate).astype(out_ref.dtype)
        return carry

    lax.fori_loop(0, nc, readout, 0, unroll=min(16, nc))


def _mlstm(qk, y3d, rowv, c0, m0, head_g):
    b, n, _ = qk.shape
    nc = n // CHUNK
    heads = N_HEADS if nc <= 4 else 1
    w = heads * D_HEAD
    v_blk = COL_V // w
    o_blk = COL_O // w
    tok = lambda off: pl.BlockSpec((1, n, w), lambda i, h: (i, 0, off + h))
    state_c = pl.BlockSpec((1, heads, 2, STATE_ROWS, D_HEAD), lambda i, h: (i, h, 0, 0, 0))
    state_m = pl.BlockSpec((1, heads, 2, SUBLANES, LANES), lambda i, h: (i, h, 0, 0, 0))
    return pl.pallas_call(
        functools.partial(_mlstm_kernel, nc=nc, heads=heads),
        out_shape=(jax.ShapeDtypeStruct((b, n, D_MLSTM), BF16),
                   jax.ShapeDtypeStruct((b, N_HEADS, 2, STATE_ROWS, D_HEAD), F32),
                   jax.ShapeDtypeStruct((b, N_HEADS, 2, SUBLANES, LANES), F32)),
        grid=(b, N_HEADS // heads),
        in_specs=[tok(0), tok(D_MLSTM // w), tok(v_blk), tok(o_blk),
                  pl.BlockSpec((1, nc, 3, SUBLANES, LANES), lambda i, h: (i, 0, 0, 0, 0)),
                  state_c, state_m,
                  pl.BlockSpec((1, w), lambda i, h: (0, h))],
        out_specs=(pl.BlockSpec((1, n, w), lambda i, h: (i, 0, h)), state_c, state_m),
        scratch_shapes=[pltpu.VMEM((heads, nc, D_HEAD, CHUNK), F32), pltpu.VMEM((heads, nc, D_HEAD, CHUNK), F32),
                        pltpu.VMEM((heads, 2, STATE_ROWS, D_HEAD), F32),
                        pltpu.VMEM((heads, 2, SUBLANES, LANES), F32)],
        compiler_params=_cparams(("arbitrary", "arbitrary")),
        name="mlstm",
    )(qk, qk, y3d, y3d, rowv, c0, m0, head_g)


def _dft_tables():
    r = DFT_R
    idx = np.arange(r)
    cg = np.cos(2 * np.pi * np.outer(idx, idx) / FOURIER_GROUP)
    sg = np.sin(2 * np.pi * np.outer(idx, idx) / FOURIER_GROUP)
    n_grp = D_FOURIER // FOURIER_GROUP
    bdc = np.kron(np.eye(n_grp), cg) / 8.0
    bds = np.kron(np.eye(n_grp), sg) / 8.0
    k1 = idx[:, None]
    a = idx[None, :]
    m1 = np.zeros((r, 2 * r, 2 * r))
    for b2 in range(r):
        th = 2 * np.pi * k1 * (r * a + b2) / (r * r)
        ec, es = np.cos(th), np.sin(th)
        m1[b2] = np.block([[ec, -es], [es, ec]]) / 8.0
    th2 = 2 * np.pi * np.outer(idx, idx) / r
    m2 = np.concatenate([np.cos(th2), -np.sin(th2)], axis=1) / 8.0
    as_bf16 = lambda t: jnp.asarray(t, dtype=F32).astype(BF16)
    return as_bf16(bdc), as_bf16(bds), as_bf16(m1), as_bf16(m2)


FOUR_BB = 8


N_SLABS = D_FOURIER // LANES


def _to_slabs(slab_ref, val):
    for s in range(N_SLABS):
        slab_ref[s] = val[:, s * LANES:(s + 1) * LANES]


def _strided_rows(slab_ref, start):
    return jnp.concatenate(
        [slab_ref[s, pl.ds(start, DFT_R, stride=FOUR_BB), :] for s in range(N_SLABS)], axis=1)


def _four_kernel(f_ref, bdc_ref, bds_ref, m1_ref, m2_ref, o_ref, z_ref, ps_ref, qs_ref, zs_ref, os_ref):
    r = DFT_R

    def stage1(blk, carry):
        b0 = pl.multiple_of(blk * FOUR_BB, FOUR_BB)
        fb = f_ref[0, :, pl.ds(b0, FOUR_BB), :].reshape(r * FOUR_BB, D_FOURIER).astype(BF16)
        _to_slabs(ps_ref, jnp.dot(fb, bdc_ref[...], preferred_element_type=F32))
        _to_slabs(qs_ref, jnp.dot(fb, bds_ref[...], preferred_element_type=F32))
        for j in range(FOUR_BB):
            x2 = jnp.concatenate([_strided_rows(ps_ref, j), _strided_rows(qs_ref, j)], axis=0).astype(BF16)
            zz = jnp.dot(m1_ref[b0 + j], x2, preferred_element_type=F32)
            z_ref[b0 + j] = _pack_pair(zz[:r], zz[r:])
        return carry

    lax.fori_loop(0, r // FOUR_BB, stage1, 0)

    def stage2(blk, carry):
        k0 = pl.multiple_of(blk * FOUR_BB, FOUR_BB)
        _to_slabs(zs_ref, z_ref[:, pl.ds(k0, FOUR_BB), :].reshape(r * FOUR_BB, D_FOURIER))
        for j in range(FOUR_BB):
            zc, zs = _unpack_pair_bf16(_strided_rows(zs_ref, j))
            o = jnp.dot(m2_ref[...], jnp.concatenate([zc, zs], axis=0), preferred_element_type=F32)
            for s in range(N_SLABS):
                os_ref[s, pl.ds(j, r, stride=FOUR_BB), :] = o[:, s * LANES:(s + 1) * LANES]
        for s in range(N_SLABS):
            o_ref[0, :, pl.ds(k0, FOUR_BB), s * LANES:(s + 1) * LANES] = os_ref[s].reshape(r, FOUR_BB, LANES)
        return carry

    lax.fori_loop(0, r // FOUR_BB, stage2, 0)


def _fourier(y4, tables):
    bdc, bds, m1, m2 = tables
    b = y4.shape[0]
    r = DFT_R
    slab = lambda dt: pltpu.VMEM((N_SLABS, r * FOUR_BB, LANES), dt)
    const = lambda *shape: pl.BlockSpec(shape, lambda i: (0,) * len(shape))
    return pl.pallas_call(
        _four_kernel,
        out_shape=jax.ShapeDtypeStruct((b, r, r, D_FOURIER), F32),
        grid=(b,),
        in_specs=[pl.BlockSpec((1, r, r, D_FOURIER), lambda i: (i, 0, 0, 0)),
                  const(D_FOURIER, D_FOURIER), const(D_FOURIER, D_FOURIER),
                  const(r, 2 * r, 2 * r), const(r, 2 * r)],
        out_specs=pl.BlockSpec((1, r, r, D_FOURIER), lambda i: (i, 0, 0, 0)),
        scratch_shapes=[pltpu.VMEM((r, r, D_FOURIER), jnp.int32), slab(F32), slab(F32), slab(jnp.int32), slab(F32)],
        compiler_params=_cparams(("arbitrary",), vmem=FOUR_VMEM_LIMIT),
        name="fourier",
    )(y4, bdc, bds, m1, m2)


OUTPROJ_PARTS = 4


def _outproj_kernel(x_ref, mr_ref, fo_ref, wm_ref, wf_ref, g1_ref, lg_ref, lb_ref,
                    sh_ref, sc_ref, wr_ref, x1_ref, xm_ref, s_ref):
    part = x_ref.shape[0] // OUTPROJ_PARTS
    rows = lambda h: slice(h * part, (h + 1) * part)
    mixer = lambda r: (jnp.dot(mr_ref[r, :], wm_ref[...], preferred_element_type=F32)
                       + jnp.dot(fo_ref[r, :].astype(BF16), wf_ref[...], preferred_element_type=F32))
    mix_next = mixer(rows(0))
    for h in range(OUTPROJ_PARTS):
        r = rows(h)
        mix = mix_next
        if h + 1 < OUTPROJ_PARTS:
            mix_next = mixer(rows(h + 1))
        x1 = _layer_norm(DEEPNORM_ALPHA * x_ref[r, :] + g1_ref[0] * mix) * lg_ref[...] + lb_ref[...]
        x1_ref[r, :] = x1
        xm = (_layer_norm(x1) * (1.0 + sc_ref[0]) + sh_ref[0]).astype(BF16)
        xm_ref[r, :] = _pack_rows(xm)
        logits = jnp.dot(xm, wr_ref[...], preferred_element_type=F32)
        s_ref[r, :] = jax.nn.sigmoid(logits)


def _outproj(x2d, mread, four, w_m, w_f, gate1, ln_g, ln_b, shift2, scale2, w_router, n_per_batch):
    t, d = x2d.shape
    per = n_per_batch // TM
    tokd = lambda w: pl.BlockSpec((TM, w), lambda i: (i, 0))
    full = lambda r, c: pl.BlockSpec((r, c), lambda i: (0, 0))
    mod = pl.BlockSpec((1, 1, d), lambda i: (i // per, 0, 0))
    return pl.pallas_call(
        _outproj_kernel,
        out_shape=(jax.ShapeDtypeStruct((t, d), F32),
                   jax.ShapeDtypeStruct((t, PACK_W), jnp.int32),
                   jax.ShapeDtypeStruct((t, LANES), F32)),
        grid=(t // TM,),
        in_specs=[tokd(d), tokd(D_MLSTM), tokd(D_FOURIER), full(D_MLSTM, d), full(D_FOURIER, d),
                  mod, full(1, d), full(1, d), mod, mod, full(d, LANES)],
        out_specs=(tokd(d), tokd(PACK_W), tokd(LANES)),
        compiler_params=_cparams(("arbitrary",)),
        name="outproj",
    )(x2d, mread, four, w_m, w_f, gate1, ln_g, ln_b, shift2, scale2, w_router)


def _route_kernel(s_ref, b_ref, idx_ref, gw_ref, rank_ref, cnt_ref, carry_ref):
    @pl.when(pl.program_id(0) == 0)
    def _():
        carry_ref[...] = jnp.zeros_like(carry_ref)

    tm = s_ref.shape[0]
    reps = tm // LANES
    s_t = s_ref[...].T[:N_EXPERTS, :]
    e_i = lax.broadcasted_iota(jnp.int32, (N_EXPERTS, tm), 0).astype(F32)
    neg_inf = jnp.float32(-jnp.inf)
    sb = s_t + jnp.tile(b_ref[...], (1, reps))
    picks, sels = [], []
    taken = jnp.zeros((N_EXPERTS, tm), F32)
    for _ in range(TOP_K):
        best = jnp.max(sb, axis=0, keepdims=True)
        pick = jnp.min(jnp.where(sb == best, e_i, float(N_EXPERTS)), axis=0, keepdims=True)
        onehot = e_i == pick
        sels.append(jnp.sum(jnp.where(onehot, s_t, 0.0), axis=0, keepdims=True))
        picks.append(pick)
        sb = jnp.where(onehot, neg_inf, sb)
        taken = taken + onehot.astype(F32)
    u_i = lax.broadcasted_iota(jnp.int32, (tm, tm), 0)
    t_i = lax.broadcasted_iota(jnp.int32, (tm, tm), 1)
    before = (u_i < t_i).astype(BF16)
    taken_bf = taken.astype(BF16)
    rank_all = jnp.tile(carry_ref[...], (1, reps)) + jnp.dot(taken_bf, before, preferred_element_type=F32)
    carry_ref[...] = carry_ref[...] + jnp.dot(taken_bf, jnp.ones((tm, LANES), BF16),
                                              preferred_element_type=F32)
    cnt_ref[...] = carry_ref[...]
    total = sels[0]
    for v in sels[1:]:
        total = total + v
    ranks = [jnp.sum(jnp.where(e_i == picks[j], rank_all, 0.0), axis=0, keepdims=True) for j in range(TOP_K)]
    idx_ref[...] = jnp.concatenate(picks, axis=0).astype(jnp.int32)
    rank_ref[...] = jnp.concatenate(ranks, axis=0).astype(jnp.int32)
    gw_t = jnp.concatenate([ROUTED_SCALE * v / total for v in sels]
                           + [jnp.zeros((LANES - TOP_K, tm), F32)], axis=0)
    gw_ref[...] = gw_t.T[:, :TOP_K]


def _route(scores, bias_rep):
    t = scores.shape[0]
    tok = lambda w: pl.BlockSpec((TM_ROUTE, w), lambda i: (i, 0))
    slot_major = pl.BlockSpec((TOP_K, TM_ROUTE), lambda i: (0, i))
    per_expert = pl.BlockSpec((N_EXPERTS, LANES), lambda i: (0, 0))
    return pl.pallas_call(
        _route_kernel,
        out_shape=(jax.ShapeDtypeStruct((TOP_K, t), jnp.int32),
                   jax.ShapeDtypeStruct((t, TOP_K), F32),
                   jax.ShapeDtypeStruct((TOP_K, t), jnp.int32),
                   jax.ShapeDtypeStruct((N_EXPERTS, LANES), F32)),
        grid=(t // TM_ROUTE,),
        in_specs=[tok(LANES), per_expert],
        out_specs=(slot_major, tok(TOP_K), slot_major, per_expert),
        scratch_shapes=[pltpu.VMEM((N_EXPERTS, LANES), F32)],
        compiler_params=_cparams(("arbitrary",)),
        name="route",
    )(scores, bias_rep)


TN_DEST = 2048


def _dest_kernel(ps_ref, idx_ref, rank_ref, dest_ref):
    idx = idx_ref[...]
    start = jnp.zeros(idx.shape, jnp.int32)
    for e in range(N_EXPERTS):
        start = jnp.where(idx == e, ps_ref[e], start)
    dest_ref[...] = start + rank_ref[...]


def _dest(pad_start, idx_t, rank_t):
    t = idx_t.shape[1]
    blk = lambda: pl.BlockSpec((TOP_K, TN_DEST), lambda i, ps: (0, i))
    grid_spec = pltpu.PrefetchScalarGridSpec(
        num_scalar_prefetch=1, grid=(t // TN_DEST,), in_specs=[blk(), blk()], out_specs=blk())
    return pl.pallas_call(
        _dest_kernel,
        out_shape=jax.ShapeDtypeStruct((TOP_K, t), jnp.int32),
        grid_spec=grid_spec,
        compiler_params=_cparams(("arbitrary",)),
        name="dest",
    )(pad_start, idx_t, rank_t)


def _sc_worker_id(info):
    return lax.axis_index("s") * info.num_cores + lax.axis_index("c")


def _sc_dispatch(xm, dest3, n_rows):
    t, w = xm.shape
    info = plsc.get_sparse_core_info()
    n_workers = info.num_cores * info.num_subcores
    per_worker = t // SC_ROWS // n_workers
    mesh = plsc.VectorSubcoreMesh(core_axis_name="c", subcore_axis_name="s")

    @functools.partial(
        pl.kernel, mesh=mesh,
        out_type=jax.ShapeDtypeStruct((n_rows, w), xm.dtype),
        scratch_types=[pltpu.VMEM((TOP_K, SC_ROWS), jnp.int32), pltpu.VMEM((SC_ROWS, w), xm.dtype)],
        name="sc_dispatch",
    )
    def run(xm_hbm, dest_hbm, xs_hbm, idx_v, rows_v):
        first = _sc_worker_id(info) * per_worker

        @pl.loop(0, per_worker)
        def _(ci):
            chunk = first + ci
            pltpu.sync_copy(dest_hbm.at[chunk], idx_v)
            pltpu.sync_copy(xm_hbm.at[pl.ds(chunk * SC_ROWS, SC_ROWS)], rows_v)
            for j in range(TOP_K):
                pltpu.sync_copy(rows_v, xs_hbm.at[idx_v.at[j]])

    return run(xm, dest3)


def _sc_gather(table, idx2):
    n_chunks, rows = idx2.shape
    w = table.shape[1]
    info = plsc.get_sparse_core_info()
    n_workers = info.num_cores * info.num_subcores
    per_worker = n_chunks // n_workers
    assert per_worker % 2 == 0
    mesh = plsc.VectorSubcoreMesh(core_axis_name="c", subcore_axis_name="s")

    @functools.partial(
        pl.kernel, mesh=mesh,
        out_type=jax.ShapeDtypeStruct((n_chunks * rows, w), table.dtype),
        scratch_types=[pltpu.VMEM((2, rows), jnp.int32), pltpu.VMEM((2, rows, w), table.dtype),
                       pltpu.SemaphoreType.DMA, pltpu.SemaphoreType.DMA],
        name="sc_gather",
    )
    def run(table_hbm, idx_hbm, out_hbm, idx_v, rows_v, sem0, sem1):
        first = _sc_worker_id(info) * per_worker
        sems = (sem0, sem1)

        def gather(b):
            return pltpu.make_async_copy(table_hbm.at[idx_v.at[b]], rows_v.at[b], sems[b])

        def start(chunk, b):
            pltpu.sync_copy(idx_hbm.at[chunk], idx_v.at[b])
            gather(b).start()

        def finish(chunk, b):
            gather(b).wait()
            pltpu.sync_copy(rows_v.at[b], out_hbm.at[pl.ds(chunk * rows, rows)])

        start(first, 0)

        @pl.loop(0, per_worker, step=2)
        def _(ci):
            chunk = first + ci
            start(chunk + 1, 1)
            finish(chunk, 0)

            @pl.when(ci + 2 < per_worker)
            def _():
                start(chunk + 2, 0)

            finish(chunk + 1, 1)

    return run(table, idx2)


def _expert_kernel(first_ref, count_ref, used_ref,
                   xs_hbm, wgu_ref, wd_ref, y_hbm, wgu_bf, wd_bf, xbuf, ybuf, in_sem, out_sem):
    e = pl.program_id(0)
    n_used = used_ref[0]
    first = first_ref[e]
    n_blocks = count_ref[e]

    def rows(g):
        return pl.ds(pl.multiple_of(g * ROW_BLK, ROW_BLK), ROW_BLK)

    def read(g):
        slot = g % EXP_IN_SLOTS
        return pltpu.make_async_copy(xs_hbm.at[rows(g), :], xbuf.at[slot], in_sem.at[slot])

    def write(g):
        slot = g % EXP_OUT_SLOTS
        return pltpu.make_async_copy(ybuf.at[slot], y_hbm.at[rows(g), :], out_sem.at[slot])

    @pl.when(e == 0)
    def _():
        for g0 in range(EXP_AHEAD):
            @pl.when(g0 < n_used)
            def _():
                read(g0).start()

    @pl.when(n_blocks > 0)
    def _():
        wgu_bf[...] = wgu_ref[0].astype(BF16)
        wd_bf[...] = wd_ref[0].astype(BF16)

    def acquire(g):
        @pl.when(g + EXP_AHEAD < n_used)
        def _():
            read(g + EXP_AHEAD).start()

        read(g).wait()

        @pl.when(g >= EXP_OUT_SLOTS)
        def _():
            write(g - EXP_OUT_SLOTS).wait()

    def gate_up(g):
        return jnp.dot(_unpack_bf16(xbuf[g % EXP_IN_SLOTS]), wgu_bf[...], preferred_element_type=F32)

    def down(g, gu):
        act = _silu(gu[:, :D_EXPERT]) * gu[:, D_EXPERT:]
        yv = jnp.dot(act.astype(BF16), wd_bf[...], preferred_element_type=F32)
        words = pltpu.pack_elementwise([yv[:, :PACK_W], yv[:, PACK_W:]], packed_dtype=BF16)
        ybuf[g % EXP_OUT_SLOTS] = lax.bitcast_convert_type(words, jnp.int32)

    def handle(g, n):
        for i in range(n):
            acquire(g + i)
        gu = gate_up(g)
        for i in range(n):
            gu_next = gate_up(g + i + 1) if i + 1 < n else None
            down(g + i, gu)
            gu = gu_next
        for i in range(n):
            write(g + i).start()

    def group(k, carry):
        handle(first + EXP_GROUP * k, EXP_GROUP)
        return carry

    lax.fori_loop(0, n_blocks // EXP_GROUP, group, 0)

    n = EXP_GROUP // 2
    while n >= 1:
        @pl.when(n_blocks % (2 * n) >= n)
        def _(n=n):
            handle(first + n_blocks - n_blocks % (2 * n), n)
        n //= 2

    @pl.when(e == pl.num_programs(0) - 1)
    def _():
        for back in range(EXP_OUT_SLOTS, 0, -1):
            @pl.when(n_used >= back)
            def _():
                write(n_used - back).wait()


def _experts(first_block, n_blocks, n_used, xs, w_gu, w_down):
    n_rows, w = xs.shape
    d = D_MODEL
    grid_spec = pltpu.PrefetchScalarGridSpec(
        num_scalar_prefetch=3,
        grid=(N_EXPERTS,),
        in_specs=[pl.BlockSpec(memory_space=pl.ANY),
                  pl.BlockSpec((1, d, 2 * D_EXPERT), lambda e, *_: (e, 0, 0)),
                  pl.BlockSpec((1, D_EXPERT, d), lambda e, *_: (e, 0, 0))],
        out_specs=pl.BlockSpec(memory_space=pl.ANY),
        scratch_shapes=[pltpu.VMEM((d, 2 * D_EXPERT), BF16), pltpu.VMEM((D_EXPERT, d), BF16),
                        pltpu.VMEM((EXP_IN_SLOTS, ROW_BLK, w), jnp.int32),
                        pltpu.VMEM((EXP_OUT_SLOTS, ROW_BLK, w), jnp.int32),
                        pltpu.SemaphoreType.DMA((EXP_IN_SLOTS,)),
                        pltpu.SemaphoreType.DMA((EXP_OUT_SLOTS,))],
    )
    return pl.pallas_call(
        _expert_kernel,
        out_shape=jax.ShapeDtypeStruct((n_rows, w), jnp.int32),
        grid_spec=grid_spec,
        compiler_params=_cparams(("arbitrary",)),
        name="experts",
    )(first_block, n_blocks, n_used, xs, w_gu, w_down)


TM_COMBINE = 512


def _combine_kernel(*refs):
    yg_refs = refs[:TOP_K]
    gw_ref, x1_ref, xm_ref, g2_ref, wsgu_ref, wsd_ref, lg_ref, lb_ref, out_ref = refs[TOP_K:]
    gu = jnp.dot(_unpack_bf16(xm_ref[...]), wsgu_ref[...], preferred_element_type=F32)
    act = _silu(gu[:, :D_EXPERT]) * gu[:, D_EXPERT:]
    acc = jnp.dot(act.astype(BF16), wsd_ref[...], preferred_element_type=F32)
    gw = gw_ref[...]
    for j in range(TOP_K):
        acc = acc + gw[:, j:j + 1] * _unpack_rows(yg_refs[j][...])
    z = DEEPNORM_ALPHA * x1_ref[...] + g2_ref[0] * acc
    out_ref[...] = _layer_norm(z) * lg_ref[...] + lb_ref[...]


def _combine(yg, gw, x1, xm, gate2, w_sgu, w_sd, ln_g, ln_b, n_per_batch):
    t, d = x1.shape
    tm = TM_COMBINE
    per = n_per_batch // tm
    steps = t // tm
    tok = lambda w: pl.BlockSpec((tm, w), lambda i: (i, 0))
    full = lambda r, c: pl.BlockSpec((r, c), lambda i: (0, 0))
    slot = lambda j: pl.BlockSpec((tm, PACK_W), lambda i: (j * steps + i, 0))
    return pl.pallas_call(
        _combine_kernel,
        out_shape=jax.ShapeDtypeStruct((t, d), F32),
        grid=(steps,),
        in_specs=[slot(j) for j in range(TOP_K)]
        + [tok(TOP_K), tok(d), tok(PACK_W), pl.BlockSpec((1, 1, d), lambda i: (i // per, 0, 0)),
           full(d, 2 * D_EXPERT), full(D_EXPERT, d), full(1, d), full(1, d)],
        out_specs=tok(d),
        compiler_params=_cparams(("arbitrary",)),
        name="combine",
    )(*([yg] * TOP_K), gw, x1, xm, gate2, w_sgu, w_sd, ln_g, ln_b)


def _mixer_heads(stream, shift, scale, w_in_k, conv_w9, conv_b, gate_b_row, head_g, c0, m0, grid2d):
    b, n, d = stream.shape
    y, f_in = _inproj(stream.reshape(b * n, d), shift, scale, w_in_k, n)
    y = y.reshape(b, n, PROJ_W)
    qk = _conv(y, conv_w9, conv_b, grid2d)
    mread, c_fin, m_fin = _mlstm(qk, y, _gates(y, gate_b_row), c0, m0, head_g)
    return f_in, mread, c_fin, m_fin


def _moe(xm, x1, scores, bias_row, gate2, w_gu, w_down, w_sgu, w_sd, ln_g, ln_b, n_per_batch):
    t = xm.shape[0]
    n_assign = t * TOP_K
    n_rows = n_assign + N_EXPERTS * ROW_BLK
    idx_t, gw, rank_t, counts = _route(scores, bias_row)
    sizes = counts[:, 0].astype(jnp.int32)
    padded = (sizes + ROW_BLK - 1) // ROW_BLK * ROW_BLK
    pad_end = jnp.cumsum(padded)
    pad_start = (pad_end - padded).astype(jnp.int32)
    dest_t = _dest(pad_start, idx_t, rank_t)
    dest3 = dest_t.reshape(TOP_K, t // SC_ROWS, SC_ROWS).transpose(1, 0, 2)

    xs = _sc_dispatch(xm, dest3, n_rows)
    yb = _experts(pad_start // ROW_BLK, (padded // ROW_BLK).astype(jnp.int32),
                  (pad_end[-1:] // ROW_BLK).astype(jnp.int32), xs, w_gu, w_down)
    yg = _sc_gather(yb, dest_t.reshape(n_assign // SC_GATHER_ROWS, SC_GATHER_ROWS))
    return _combine(yg, gw, x1, xm, gate2, w_sgu, w_sd, ln_g, ln_b, n_per_batch)


def kernel(x, c, ctx, c_ctx, w_ada, b_ada, w_in, conv_w, conv_b, gate_b, head_g, w_out, ln1_g, ln1_b,
           w_router, router_bias, w_expert_gu, w_expert_down, w_shared_gu, w_shared_down, ln2_g, ln2_b):
    bsz, n, d = x.shape
    depth = w_ada.shape[0]
    assert depth == 1
    t = bsz * n
    tables = _dft_tables()
    for l in range(depth):
        cvec = jnp.zeros((SUBLANES, d), F32).at[:bsz].set(c).at[bsz].set(c_ctx)
        mod = _ada(cvec, w_ada[l], b_ada[l][None, :])
        mods = mod.reshape(SUBLANES, 6, 1, d)
        mod_x = [mods[:bsz, j] for j in range(6)]
        mod_c = [mods[bsz:bsz + 1, j] for j in range(2)]

        w_in_k = _w_in_layout(w_in, l)
        conv_w9 = conv_w[l].reshape(9, 2 * D_MLSTM)
        conv_b_row = conv_b[l][None, :]
        gate_b_row = jnp.zeros((1, LANES), F32).at[0, :N_GATE_COLS].set(gate_b[l].reshape(-1))
        head_g_row = head_g[l][None, :]

        c0 = jnp.zeros((bsz, N_HEADS, 2, STATE_ROWS, D_HEAD), F32)
        m0 = jnp.zeros((bsz, N_HEADS, 2, SUBLANES, LANES), F32)
        _, _, c_ctx_fin, m_ctx_fin = _mixer_heads(ctx, mod_c[0], mod_c[1], w_in_k, conv_w9, conv_b_row,
                                                  gate_b_row, head_g_row, c0, m0, grid2d=False)
        f_in, mread, _, _ = _mixer_heads(x, mod_x[0], mod_x[1], w_in_k, conv_w9, conv_b_row,
                                      gate_b_row, head_g_row, c_ctx_fin, m_ctx_fin, grid2d=True)

        four = _fourier(f_in.reshape(bsz, DFT_R, DFT_R, D_FOURIER), tables).reshape(t, D_FOURIER)

        w_o = w_out[l].astype(BF16)
        w_r = jnp.zeros((d, LANES), F32).at[:, :N_EXPERTS].set(w_router[l]).astype(BF16)
        x2d = x.reshape(t, d)
        x1, xm, scores = _outproj(x2d, mread.reshape(t, D_MLSTM), four, w_o[:D_MLSTM], w_o[D_MLSTM:],
                                  mod_x[2], ln1_g[l][None, :], ln1_b[l][None, :], mod_x[3], mod_x[4],
                                  w_r, n)

        bias_rep = jnp.broadcast_to(router_bias[l][:, None], (N_EXPERTS, LANES))
        x = _moe(xm, x1, scores, bias_rep, mod_x[5], w_expert_gu[l], w_expert_down[l],
                 w_shared_gu[l].astype(BF16), w_shared_down[l].astype(BF16),
                 ln2_g[l][None, :], ln2_b[l][None, :], n).reshape(bsz, n, d)
    return x
```

```python
import functools

import numpy as np
import jax
import jax.numpy as jnp
from jax import lax
from jax.experimental import pallas as pl
from jax.experimental.pallas import tpu as pltpu
from jax.experimental.pallas import tpu_sc as plsc

F32 = jnp.float32
BF16 = jnp.bfloat16
HIGHEST = lax.Precision.HIGHEST

D_MODEL = 1024
GRID_W = 64
N_HEADS = 4
D_HEAD = 128
D_MLSTM = N_HEADS * D_HEAD
D_FOURIER = 512
FOURIER_GROUP = 64
N_GATE_COLS = 16
CHUNK = 128
N_EXPERTS = 64
TOP_K = 8
D_EXPERT = 256
ROUTED_SCALE = 2.5
DEEPNORM_ALPHA = 2.0 ** 0.25
LN_EPS = 1e-5

LANES = 128
SUBLANES = 8
VMEM_LIMIT = 48 * 1024 * 1024
FOUR_VMEM_LIMIT = 56 * 1024 * 1024

COL_QK = 0
COL_V = 2 * D_MLSTM
COL_O = 3 * D_MLSTM
COL_G = 4 * D_MLSTM
PROJ_W = COL_G + LANES

TN_ADA = 2048
TM = 1024
TM_ROUTE = 1024
ROW_BLK = 256
EXP_GROUP = 8
EXP_AHEAD = 20
EXP_IN_SLOTS = 32
EXP_OUT_SLOTS = 2 * EXP_GROUP
PACK_W = D_MODEL // 2
SC_ROWS = 128
SC_GATHER_ROWS = 64
CONV_PAD = 72
CONV_RB = 256
DFT_R = 64


def _cparams(sem, vmem=VMEM_LIMIT):
    return pltpu.CompilerParams(dimension_semantics=sem, vmem_limit_bytes=vmem)


def _layer_norm(x):
    mu = jnp.mean(x, axis=-1, keepdims=True)
    xc = x - mu
    var = jnp.mean(xc * xc, axis=-1, keepdims=True)
    return xc * lax.rsqrt(var + LN_EPS)


def _silu(x):
    return x * jax.nn.sigmoid(x)


def _pack_pair(lo, hi):
    as_bits = lambda u: lax.bitcast_convert_type(u.astype(BF16).astype(F32), jnp.uint32)
    word = as_bits(hi) | lax.shift_right_logical(as_bits(lo), jnp.uint32(16))
    return lax.bitcast_convert_type(word, jnp.int32)


def _unpack_pair_bf16(w):
    lo = lax.bitcast_convert_type(w.astype(jnp.int16), BF16)
    hi = lax.bitcast_convert_type(lax.shift_right_logical(w, jnp.int32(16)).astype(jnp.int16), BF16)
    return lo, hi


def _pack_rows(v):
    return _pack_pair(v[:, :PACK_W], v[:, PACK_W:])


def _unpack_bf16(w):
    return jnp.concatenate(_unpack_pair_bf16(w), axis=-1)


def _unpack_rows(w):
    u = lax.bitcast_convert_type(w, jnp.uint32)
    lo = lax.bitcast_convert_type(lax.shift_left(u, jnp.uint32(16)), F32)
    hi = lax.bitcast_convert_type(u & jnp.uint32(0xFFFF0000), F32)
    return jnp.concatenate([lo, hi], axis=1)


def _ada_kernel(c_ref, w_ref, b_ref, o_ref):
    s = _silu(c_ref[...]).astype(BF16)
    o_ref[...] = jnp.dot(s, w_ref[...].astype(BF16), preferred_element_type=F32) + b_ref[...]


def _ada(cvec, w_ada, b_ada):
    rows, d = cvec.shape
    n_out = w_ada.shape[1]
    tn = TN_ADA
    return pl.pallas_call(
        _ada_kernel,
        out_shape=jax.ShapeDtypeStruct((rows, n_out), F32),
        grid=(n_out // tn,),
        in_specs=[pl.BlockSpec((rows, d), lambda j: (0, 0)),
                  pl.BlockSpec((d, tn), lambda j: (0, j)),
                  pl.BlockSpec((1, tn), lambda j: (0, j))],
        out_specs=pl.BlockSpec((rows, tn), lambda j: (0, j)),
        compiler_params=_cparams(("arbitrary",)),
        name="ada",
    )(cvec, w_ada, b_ada)


def _inproj_kernel(x_ref, sh_ref, sc_ref, w_ref, y_ref, f_ref):
    half = x_ref.shape[0] // 2
    for h in range(2):
        r = slice(h * half, (h + 1) * half)
        u = _layer_norm(x_ref[r, :]) * (1.0 + sc_ref[0]) + sh_ref[0]
        proj = jnp.dot(u.astype(BF16), w_ref[...], preferred_element_type=F32)
        y_ref[r, :] = proj[:, :PROJ_W]
        f_ref[r, :] = proj[:, PROJ_W:]


def _inproj(x2d, shift, scale, w, n_per_batch):
    t, d = x2d.shape
    if shift.shape[0] == 1:
        tm = min(TM, t)
        per = t // tm
    else:
        tm = min(TM, n_per_batch)
        per = n_per_batch // tm
    return pl.pallas_call(
        _inproj_kernel,
        out_shape=(jax.ShapeDtypeStruct((t, PROJ_W), F32), jax.ShapeDtypeStruct((t, D_FOURIER), F32)),
        grid=(t // tm,),
        in_specs=[pl.BlockSpec((tm, d), lambda i: (i, 0)),
                  pl.BlockSpec((1, 1, d), lambda i: (i // per, 0, 0)),
                  pl.BlockSpec((1, 1, d), lambda i: (i // per, 0, 0)),
                  pl.BlockSpec((d, PROJ_W + D_FOURIER), lambda i: (0, 0))],
        out_specs=(pl.BlockSpec((tm, PROJ_W), lambda i: (i, 0)),
                   pl.BlockSpec((tm, D_FOURIER), lambda i: (i, 0))),
        compiler_params=_cparams(("arbitrary",)),
        name="inproj",
    )(x2d, shift, scale, w)


def _conv_kernel(y_ref, w_ref, b_ref, o_ref, pad_ref, *, n, grid2d, tiles):
    zeros = jnp.zeros((CONV_PAD, LANES), F32)
    pad_ref[0:CONV_PAD, :] = zeros
    pad_ref[CONV_PAD + n:CONV_PAD + n + CONV_PAD, :] = zeros
    rb = min(CONV_RB, n)
    col = lax.broadcasted_iota(jnp.int32, (rb, LANES), 0) % GRID_W
    not_first = col >= 1
    not_last = col <= GRID_W - 2
    row_taps = (0, 1, 2) if grid2d else (1,)
    for ct in range(tiles):
        lanes = slice(ct * LANES, (ct + 1) * LANES)
        pad_ref[CONV_PAD:CONV_PAD + n, :] = y_ref[0, :, lanes]
        k_scale = jnp.where(pl.program_id(1) * tiles + ct >= N_HEADS, D_HEAD ** -0.5, 1.0).astype(F32)
        bias = b_ref[:, lanes]
        for blk in range(n // rb):
            r0 = CONV_PAD + blk * rb
            acc = jnp.zeros((rb, LANES), F32)
            for dj in range(3):
                part = jnp.zeros((rb, LANES), F32)
                for di in row_taps:
                    off = (di - 1) * GRID_W + (dj - 1)
                    tap = w_ref[di * 3 + dj:di * 3 + dj + 1, lanes]
                    part = part + tap * pad_ref[r0 + off:r0 + off + rb, :]
                if grid2d and dj == 0:
                    part = jnp.where(not_first, part, 0.0)
                if grid2d and dj == 2:
                    part = jnp.where(not_last, part, 0.0)
                acc = acc + part
            val = _silu(acc + bias) * k_scale
            o_ref[0, blk * rb:(blk + 1) * rb, lanes] = val.astype(o_ref.dtype)


def _conv(y3d, conv_w9, conv_b, grid2d):
    b, n, _ = y3d.shape
    n_ct = 2 * D_MLSTM // LANES
    tiles = 1 if grid2d else n_ct
    w = tiles * LANES
    return pl.pallas_call(
        functools.partial(_conv_kernel, n=n, grid2d=grid2d, tiles=tiles),
        out_shape=jax.ShapeDtypeStruct((b, n, 2 * D_MLSTM), BF16),
        grid=(b, n_ct // tiles),
        in_specs=[pl.BlockSpec((1, n, w), lambda i, c: (i, 0, c)),
                  pl.BlockSpec((9, w), lambda i, c: (0, c)),
                  pl.BlockSpec((1, w), lambda i, c: (0, c))],
        out_specs=pl.BlockSpec((1, n, w), lambda i, c: (i, 0, c)),
        scratch_shapes=[pltpu.VMEM((n + 2 * CONV_PAD, LANES), F32)],
        compiler_params=_cparams(("arbitrary", "arbitrary")),
        name="conv2d" if grid2d else "conv1d",
    )(y3d, conv_w9, conv_b)


STATE_ROWS = D_HEAD + 32
GATE_A, GATE_CM, GATE_B = 0, 1, 2
GATE_GROUP = 32


def _gates_kernel(g_ref, gb_ref, rv_ref, *, nc):
    u_i = lax.broadcasted_iota(jnp.int32, (CHUNK, CHUNK), 0)
    t_i = lax.broadcasted_iota(jnp.int32, (CHUNK, CHUNK), 1)
    prefix = (u_i <= t_i).astype(F32)
    suffix = (u_i >= t_i).astype(F32)
    grp = min(GATE_GROUP, nc)
    rows = grp * SUBLANES
    lane = lax.broadcasted_iota(jnp.int32, (rows, LANES), 1)
    is_fwd = lax.broadcasted_iota(jnp.int32, (rows, LANES), 0) % SUBLANES < N_HEADS
    h = N_HEADS

    def running_max(a):
        fwd, bwd = a, a
        sh = 1
        while sh < CHUNK:
            fwd = jnp.where(lane >= sh, jnp.maximum(fwd, pltpu.roll(fwd, sh, axis=1)), fwd)
            bwd = jnp.where(lane < CHUNK - sh, jnp.maximum(bwd, pltpu.roll(bwd, CHUNK - sh, axis=1)), bwd)
            sh *= 2
        return jnp.where(is_fwd, fwd, bwd)

    def body(i, carry):
        c0 = i * grp
        r0 = pl.multiple_of(c0 * CHUNK, CHUNK)
        pre = g_ref[0, pl.ds(r0, grp * CHUNK), :] + gb_ref[...]
        pre_t = jnp.concatenate(
            [pre[j * CHUNK:(j + 1) * CHUNK].T[:N_GATE_COLS, :] for j in range(grp)], axis=0)
        lf_t = -(jnp.maximum(-pre_t, 0.0) + jnp.log1p(jnp.exp(-jnp.abs(pre_t))))
        cum_f = jnp.dot(lf_t, prefix, precision=HIGHEST, preferred_element_type=F32)
        cum_b = jnp.dot(lf_t, suffix, precision=HIGHEST, preferred_element_type=F32)
        pick = lambda t, j, q: t[j * N_GATE_COLS + q * h:j * N_GATE_COLS + (q + 1) * h]
        b8 = jnp.concatenate([x for j in range(grp) for x in (pick(cum_f, j, 1), pick(cum_b, j, 3))], axis=0)
        li8 = jnp.concatenate([x for j in range(grp) for x in (pick(pre_t, j, 0), pick(pre_t, j, 2))], axis=0)
        a8 = li8 - b8
        as_chunks = lambda t: t.reshape(grp, SUBLANES, LANES)
        rv_ref[0, pl.ds(c0, grp), GATE_A] = as_chunks(a8)
        rv_ref[0, pl.ds(c0, grp), GATE_CM] = as_chunks(running_max(a8))
        rv_ref[0, pl.ds(c0, grp), GATE_B] = as_chunks(b8)
        return carry

    lax.fori_loop(0, nc // grp, body, 0)


def _gates(y3d, gate_b_row):
    b, n, _ = y3d.shape
    nc = n // CHUNK
    return pl.pallas_call(
        functools.partial(_gates_kernel, nc=nc),
        out_shape=jax.ShapeDtypeStruct((b, nc, 3, SUBLANES, LANES), F32),
        grid=(b,),
        in_specs=[pl.BlockSpec((1, n, LANES), lambda i: (i, 0, COL_G // LANES)),
                  pl.BlockSpec((1, LANES), lambda i: (0, 0))],
        out_specs=pl.BlockSpec((1, nc, 3, SUBLANES, LANES), lambda i: (i, 0, 0, 0, 0)),
        compiler_params=_cparams(("arbitrary",)),
        name="gates",
    )(y3d, gate_b_row)


def _mlstm_kernel(q_ref, k_ref, v_ref, o_ref, rv_ref, c0_ref, m0_ref, hg_ref,
                  out_ref, cfin_ref, mfin_ref, hf_ref, hb_ref, cst_ref, mst_ref, *, nc, heads):
    cst_ref[...] = c0_ref[0]
    mst_ref[...] = m0_ref[0]
    s_i = lax.broadcasted_iota(jnp.int32, (CHUNK, CHUNK), 0)
    t_i = lax.broadcasted_iota(jnp.int32, (CHUNK, CHUNK), 1)
    ones_row = (s_i[:STATE_ROWS - D_HEAD] == 0).astype(F32)

    def one_chunk(c, d, mask_t, last, h_ref, hh):
        r0 = pl.multiple_of(c * CHUNK, CHUNK)
        lanes = slice(hh * D_HEAD, (hh + 1) * D_HEAD)
        g = d * N_HEADS + pl.program_id(1) * heads + hh
        gate_row = lambda tbl: rv_ref[0, pl.ds(c, 1), tbl, pl.ds(g, 1), :].reshape(1, LANES)
        a_row, cm_row, b_row = gate_row(GATE_A), gate_row(GATE_CM), gate_row(GATE_B)
        q_t = q_ref[0, pl.ds(r0, CHUNK), lanes].T
        k = k_ref[0, pl.ds(r0, CHUNK), lanes]
        v_t_aug = jnp.concatenate([v_ref[0, pl.ds(r0, CHUNK), lanes].T, ones_row], axis=0)
        m_prev = mst_ref[hh, d, 0:1, :]
        m_row = jnp.maximum(m_prev, cm_row)
        a_bc = jnp.broadcast_to(a_row, (CHUNK, CHUNK)).T
        d_t = jnp.where(mask_t, jnp.exp(a_bc - m_row), 0.0)
        s_t = jnp.dot(k, q_t, preferred_element_type=F32) * d_t
        cs = cst_ref[hh, d]
        rhs = jnp.concatenate(
            [s_t.astype(BF16), (q_t.astype(F32) * jnp.exp(m_prev - m_row)).astype(BF16)], axis=0)
        lhs = jnp.concatenate([v_t_aug.astype(BF16), cs.astype(BF16)], axis=1)
        h_t = jnp.dot(lhs, rhs, preferred_element_type=F32)
        den = h_t[D_HEAD:D_HEAD + 1, :]
        floor = jnp.exp(-b_row - m_row)
        h_ref[hh, pl.ds(c, 1)] = (h_t[:D_HEAD] / jnp.maximum(jnp.abs(den), floor))[None]
        m_last = m_row[:, last:last + 1]
        wv_t = (v_t_aug * jnp.exp(a_row - m_last)).astype(BF16)
        cst_ref[hh, d] = jnp.exp(m_prev - m_last) * cs + jnp.dot(wv_t, k, preferred_element_type=F32)
        mst_ref[hh, d] = jnp.broadcast_to(b_row[:, last:last + 1] + m_last, (SUBLANES, LANES))

    def body(c, carry):
        for hh in range(heads):
            one_chunk(c, 0, s_i <= t_i, CHUNK - 1, hf_ref, hh)
            one_chunk(nc - 1 - c, 1, s_i >= t_i, 0, hb_ref, hh)
        return carry

    lax.fori_loop(0, nc, body, 0, unroll=min(16 // heads, nc))
    cfin_ref[0] = cst_ref[...]
    mfin_ref[0] = mst_ref[...]

    def readout(c, carry):
        r0 = pl.multiple_of(c * CHUNK, CHUNK)
        for hh in range(heads):
            lanes = slice(hh * D_HEAD, (hh + 1) * D_HEAD)
            h_t = hf_ref[hh, pl.ds(c, 1)][0] + hb_ref[hh, pl.ds(c, 1)][0]
            hc = h_t - jnp.mean(h_t, axis=0, keepdims=True)
            hn = (hc * lax.rsqrt(jnp.mean(hc * hc, axis=0, keepdims=True) + LN_EPS)).T
            gate = jax.nn.sigmoid(o_ref[0, pl.ds(r0, CHUNK), lanes])
            out_ref[0, pl.ds(r0, CHUNK), lanes] = (hn * hg_ref[:, lanes] * gate).astype(out_ref.dtype)
        return carry

    lax.fori_loop(0, nc, readout, 0, unroll=min(16, nc))


def _mlstm(qk, y3d, rowv, c0, m0, head_g):
    b, n, _ = qk.shape
    nc = n // CHUNK
    heads = N_HEADS if nc <= 4 else 1
    w = heads * D_HEAD
    v_blk = COL_V // w
    o_blk = COL_O // w
    tok = lambda off: pl.BlockSpec((1, n, w), lambda i, h: (i, 0, off + h))
    state_c = pl.BlockSpec((1, heads, 2, STATE_ROWS, D_HEAD), lambda i, h: (i, h, 0, 0, 0))
    state_m = pl.BlockSpec((1, heads, 2, SUBLANES, LANES), lambda i, h: (i, h, 0, 0, 0))
    return pl.pallas_call(
        functools.partial(_mlstm_kernel, nc=nc, heads=heads),
        out_shape=(jax.ShapeDtypeStruct((b, n, D_MLSTM), BF16),
                   jax.ShapeDtypeStruct((b, N_HEADS, 2, STATE_ROWS, D_HEAD), F32),
                   jax.ShapeDtypeStruct((b, N_HEADS, 2, SUBLANES, LANES), F32)),
        grid=(b, N_HEADS // heads),
        in_specs=[tok(0), tok(D_MLSTM // w), tok(v_blk), tok(o_blk),
                  pl.BlockSpec((1, nc, 3, SUBLANES, LANES), lambda i, h: (i, 0, 0, 0, 0)),
                  state_c, state_m,
                  pl.BlockSpec((1, w), lambda i, h: (0, h))],
        out_specs=(pl.BlockSpec((1, n, w), lambda i, h: (i, 0, h)), state_c, state_m),
        scratch_shapes=[pltpu.VMEM((heads, nc, D_HEAD, CHUNK), F32), pltpu.VMEM((heads, nc, D_HEAD, CHUNK), F32),
                        pltpu.VMEM((heads, 2, STATE_ROWS, D_HEAD), F32),
                        pltpu.VMEM((heads, 2, SUBLANES, LANES), F32)],
        compiler_params=_cparams(("arbitrary", "arbitrary")),
        name="mlstm",
    )(qk, qk, y3d, y3d, rowv, c0, m0, head_g)


def _dft_tables():
    r = DFT_R
    idx = np.arange(r)
    cg = np.cos(2 * np.pi * np.outer(idx, idx) / FOURIER_GROUP)
    sg = np.sin(2 * np.pi * np.outer(idx, idx) / FOURIER_GROUP)
    n_grp = D_FOURIER // FOURIER_GROUP
    bdc = np.kron(np.eye(n_grp), cg) / 8.0
    bds = np.kron(np.eye(n_grp), sg) / 8.0
    k1 = idx[:, None]
    a = idx[None, :]
    m1 = np.zeros((r, 2 * r, 2 * r))
    for b2 in range(r):
        th = 2 * np.pi * k1 * (r * a + b2) / (r * r)
        ec, es = np.cos(th), np.sin(th)
        m1[b2] = np.block([[ec, -es], [es, ec]]) / 8.0
    th2 = 2 * np.pi * np.outer(idx, idx) / r
    m2 = np.concatenate([np.cos(th2), -np.sin(th2)], axis=1) / 8.0
    as_bf16 = lambda t: jnp.asarray(t, dtype=F32).astype(BF16)
    return as_bf16(bdc), as_bf16(bds), as_bf16(m1), as_bf16(m2)


FOUR_BB = 8


N_SLABS = D_FOURIER // LANES


def _to_slabs(slab_ref, val):
    for s in range(N_SLABS):
        slab_ref[s] = val[:, s * LANES:(s + 1) * LANES]


def _strided_rows(slab_ref, start):
    return jnp.concatenate(
        [slab_ref[s, pl.ds(start, DFT_R, stride=FOUR_BB), :] for s in range(N_SLABS)], axis=1)


def _four_kernel(f_ref, bdc_ref, bds_ref, m1_ref, m2_ref, o_ref, z_ref, ps_ref, qs_ref, zs_ref, os_ref):
    r = DFT_R

    def stage1(blk, carry):
        b0 = pl.multiple_of(blk * FOUR_BB, FOUR_BB)
        fb = f_ref[0, :, pl.ds(b0, FOUR_BB), :].reshape(r * FOUR_BB, D_FOURIER).astype(BF16)
        _to_slabs(ps_ref, jnp.dot(fb, bdc_ref[...], preferred_element_type=F32))
        _to_slabs(qs_ref, jnp.dot(fb, bds_ref[...], preferred_element_type=F32))
        for j in range(FOUR_BB):
            x2 = jnp.concatenate([_strided_rows(ps_ref, j), _strided_rows(qs_ref, j)], axis=0).astype(BF16)
            zz = jnp.dot(m1_ref[b0 + j], x2, preferred_element_type=F32)
            z_ref[b0 + j] = _pack_pair(zz[:r], zz[r:])
        return carry

    lax.fori_loop(0, r // FOUR_BB, stage1, 0)

    def stage2(blk, carry):
        k0 = pl.multiple_of(blk * FOUR_BB, FOUR_BB)
        _to_slabs(zs_ref, z_ref[:, pl.ds(k0, FOUR_BB), :].reshape(r * FOUR_BB, D_FOURIER))
        for j in range(FOUR_BB):
            zc, zs = _unpack_pair_bf16(_strided_rows(zs_ref, j))
            o = jnp.dot(m2_ref[...], jnp.concatenate([zc, zs], axis=0), preferred_element_type=F32)
            for s in range(N_SLABS):
                os_ref[s, pl.ds(j, r, stride=FOUR_BB), :] = o[:, s * LANES:(s + 1) * LANES]
        for s in range(N_SLABS):
            o_ref[0, :, pl.ds(k0, FOUR_BB), s * LANES:(s + 1) * LANES] = os_ref[s].reshape(r, FOUR_BB, LANES)
        return carry

    lax.fori_loop(0, r // FOUR_BB, stage2, 0)


def _fourier(y4, tables):
    bdc, bds, m1, m2 = tables
    b = y4.shape[0]
    r = DFT_R
    slab = lambda dt: pltpu.VMEM((N_SLABS, r * FOUR_BB, LANES), dt)
    const = lambda *shape: pl.BlockSpec(shape, lambda i: (0,) * len(shape))
    return pl.pallas_call(
        _four_kernel,
        out_shape=jax.ShapeDtypeStruct((b, r, r, D_FOURIER), F32),
        grid=(b,),
        in_specs=[pl.BlockSpec((1, r, r, D_FOURIER), lambda i: (i, 0, 0, 0)),
                  const(D_FOURIER, D_FOURIER), const(D_FOURIER, D_FOURIER),
                  const(r, 2 * r, 2 * r), const(r, 2 * r)],
        out_specs=pl.BlockSpec((1, r, r, D_FOURIER), lambda i: (i, 0, 0, 0)),
        scratch_shapes=[pltpu.VMEM((r, r, D_FOURIER), jnp.int32), slab(F32), slab(F32), slab(jnp.int32), slab(F32)],
        compiler_params=_cparams(("arbitrary",), vmem=FOUR_VMEM_LIMIT),
        name="fourier",
    )(y4, bdc, bds, m1, m2)


OUTPROJ_PARTS = 4


def _outproj_kernel(x_ref, mr_ref, fo_ref, wm_ref, wf_ref, g1_ref, lg_ref, lb_ref,
                    sh_ref, sc_ref, wr_ref, x1_ref, xm_ref, s_ref):
    part = x_ref.shape[0] // OUTPROJ_PARTS
    rows = lambda h: slice(h * part, (h + 1) * part)
    mixer = lambda r: (jnp.dot(mr_ref[r, :], wm_ref[...], preferred_element_type=F32)
                       + jnp.dot(fo_ref[r, :].astype(BF16), wf_ref[...], preferred_element_type=F32))
    mix_next = mixer(rows(0))
    for h in range(OUTPROJ_PARTS):
        r = rows(h)
        mix = mix_next
        if h + 1 < OUTPROJ_PARTS:
            mix_next = mixer(rows(h + 1))
        x1 = _layer_norm(DEEPNORM_ALPHA * x_ref[r, :] + g1_ref[0] * mix) * lg_ref[...] + lb_ref[...]
        x1_ref[r, :] = x1
        xm = (_layer_norm(x1) * (1.0 + sc_ref[0]) + sh_ref[0]).astype(BF16)
        xm_ref[r, :] = _pack_rows(xm)
        logits = jnp.dot(xm, wr_ref[...], preferred_element_type=F32)
        s_ref[r, :] = jax.nn.sigmoid(logits)


def _outproj(x2d, mread, four, w_m, w_f, gate1, ln_g, ln_b, shift2, scale2, w_router, n_per_batch):
    t, d = x2d.shape
    per = n_per_batch // TM
    tokd = lambda w: pl.BlockSpec((TM, w), lambda i: (i, 0))
    full = lambda r, c: pl.BlockSpec((r, c), lambda i: (0, 0))
    mod = pl.BlockSpec((1, 1, d), lambda i: (i // per, 0, 0))
    return pl.pallas_call(
        _outproj_kernel,
        out_shape=(jax.ShapeDtypeStruct((t, d), F32),
                   jax.ShapeDtypeStruct((t, PACK_W), jnp.int32),
                   jax.ShapeDtypeStruct((t, LANES), F32)),
        grid=(t // TM,),
        in_specs=[tokd(d), tokd(D_MLSTM), tokd(D_FOURIER), full(D_MLSTM, d), full(D_FOURIER, d),
                  mod, full(1, d), full(1, d), mod, mod, full(d, LANES)],
        out_specs=(tokd(d), tokd(PACK_W), tokd(LANES)),
        compiler_params=_cparams(("arbitrary",)),
        name="outproj",
    )(x2d, mread, four, w_m, w_f, gate1, ln_g, ln_b, shift2, scale2, w_router)


def _route_kernel(s_ref, b_ref, idx_ref, gw_ref, rank_ref, cnt_ref, carry_ref):
    @pl.when(pl.program_id(0) == 0)
    def _():
        carry_ref[...] = jnp.zeros_like(carry_ref)

    tm = s_ref.shape[0]
    reps = tm // LANES
    s_t = s_ref[...].T[:N_EXPERTS, :]
    e_i = lax.broadcasted_iota(jnp.int32, (N_EXPERTS, tm), 0).astype(F32)
    neg_inf = jnp.float32(-jnp.inf)
    sb = s_t + jnp.tile(b_ref[...], (1, reps))
    picks, sels = [], []
    taken = jnp.zeros((N_EXPERTS, tm), F32)
    for _ in range(TOP_K):
        best = jnp.max(sb, axis=0, keepdims=True)
        pick = jnp.min(jnp.where(sb == best, e_i, float(N_EXPERTS)), axis=0, keepdims=True)
        onehot = e_i == pick
        sels.append(jnp.sum(jnp.where(onehot, s_t, 0.0), axis=0, keepdims=True))
        picks.append(pick)
        sb = jnp.where(onehot, neg_inf, sb)
        taken = taken + onehot.astype(F32)
    u_i = lax.broadcasted_iota(jnp.int32, (tm, tm), 0)
    t_i = lax.broadcasted_iota(jnp.int32, (tm, tm), 1)
    before = (u_i < t_i).astype(BF16)
    taken_bf = taken.astype(BF16)
    rank_all = jnp.tile(carry_ref[...], (1, reps)) + jnp.dot(taken_bf, before, preferred_element_type=F32)
    carry_ref[...] = carry_ref[...] + jnp.dot(taken_bf, jnp.ones((tm, LANES), BF16),
                                              preferred_element_type=F32)
    cnt_ref[...] = carry_ref[...]
    total = sels[0]
    for v in sels[1:]:
        total = total + v
    ranks = [jnp.sum(jnp.where(e_i == picks[j], rank_all, 0.0), axis=0, keepdims=True) for j in range(TOP_K)]
    idx_ref[...] = jnp.concatenate(picks, axis=0).astype(jnp.int32)
    rank_ref[...] = jnp.concatenate(ranks, axis=0).astype(jnp.int32)
    gw_t = jnp.concatenate([ROUTED_SCALE * v / total for v in sels]
                           + [jnp.zeros((LANES - TOP_K, tm), F32)], axis=0)
    gw_ref[...] = gw_t.T[:, :TOP_K]


def _route(scores, bias_rep):
    t = scores.shape[0]
    tok = lambda w: pl.BlockSpec((TM_ROUTE, w), lambda i: (i, 0))
    slot_major = pl.BlockSpec((TOP_K, TM_ROUTE), lambda i: (0, i))
    per_expert = pl.BlockSpec((N_EXPERTS, LANES), lambda i: (0, 0))
    return pl.pallas_call(
        _route_kernel,
        out_shape=(jax.ShapeDtypeStruct((TOP_K, t), jnp.int32),
                   jax.ShapeDtypeStruct((t, TOP_K), F32),
                   jax.ShapeDtypeStruct((TOP_K, t), jnp.int32),
                   jax.ShapeDtypeStruct((N_EXPERTS, LANES), F32)),
        grid=(t // TM_ROUTE,),
        in_specs=[tok(LANES), per_expert],
        out_specs=(slot_major, tok(TOP_K), slot_major, per_expert),
        scratch_shapes=[pltpu.VMEM((N_EXPERTS, LANES), F32)],
        compiler_params=_cparams(("arbitrary",)),
        name="route",
    )(scores, bias_rep)


TN_DEST = 2048


def _dest_kernel(ps_ref, idx_ref, rank_ref, dest_ref):
    idx = idx_ref[...]
    start = jnp.zeros(idx.shape, jnp.int32)
    for e in range(N_EXPERTS):
        start = jnp.where(idx == e, ps_ref[e], start)
    dest_ref[...] = start + rank_ref[...]


def _dest(pad_start, idx_t, rank_t):
    t = idx_t.shape[1]
    blk = lambda: pl.BlockSpec((TOP_K, TN_DEST), lambda i, ps: (0, i))
    grid_spec = pltpu.PrefetchScalarGridSpec(
        num_scalar_prefetch=1, grid=(t // TN_DEST,), in_specs=[blk(), blk()], out_specs=blk())
    return pl.pallas_call(
        _dest_kernel,
        out_shape=jax.ShapeDtypeStruct((TOP_K, t), jnp.int32),
        grid_spec=grid_spec,
        compiler_params=_cparams(("arbitrary",)),
        name="dest",
    )(pad_start, idx_t, rank_t)


def _sc_worker_id(info):
    return lax.axis_index("s") * info.num_cores + lax.axis_index("c")


def _sc_dispatch(xm, dest3, n_rows):
    t, w = xm.shape
    info = plsc.get_sparse_core_info()
    n_workers = info.num_cores * info.num_subcores
    per_worker = t // SC_ROWS // n_workers
    mesh = plsc.VectorSubcoreMesh(core_axis_name="c", subcore_axis_name="s")

    @functools.partial(
        pl.kernel, mesh=mesh,
        out_type=jax.ShapeDtypeStruct((n_rows, w), xm.dtype),
        scratch_types=[pltpu.VMEM((TOP_K, SC_ROWS), jnp.int32), pltpu.VMEM((SC_ROWS, w), xm.dtype)],
        name="sc_dispatch",
    )
    def run(xm_hbm, dest_hbm, xs_hbm, idx_v, rows_v):
        first = _sc_worker_id(info) * per_worker

        @pl.loop(0, per_worker)
        def _(ci):
            chunk = first + ci
            pltpu.sync_copy(dest_hbm.at[chunk], idx_v)
            pltpu.sync_copy(xm_hbm.at[pl.ds(chunk * SC_ROWS, SC_ROWS)], rows_v)
            for j in range(TOP_K):
                pltpu.sync_copy(rows_v, xs_hbm.at[idx_v.at[j]])

    return run(xm, dest3)


def _sc_gather(table, idx2):
    n_chunks, rows = idx2.shape
    w = table.shape[1]
    info = plsc.get_sparse_core_info()
    n_workers = info.num_cores * info.num_subcores
    per_worker = n_chunks // n_workers
    assert per_worker % 2 == 0
    mesh = plsc.VectorSubcoreMesh(core_axis_name="c", subcore_axis_name="s")

    @functools.partial(
        pl.kernel, mesh=mesh,
        out_type=jax.ShapeDtypeStruct((n_chunks * rows, w), table.dtype),
        scratch_types=[pltpu.VMEM((2, rows), jnp.int32), pltpu.VMEM((2, rows, w), table.dtype),
                       pltpu.SemaphoreType.DMA, pltpu.SemaphoreType.DMA],
        name="sc_gather",
    )
    def run(table_hbm, idx_hbm, out_hbm, idx_v, rows_v, sem0, sem1):
        first = _sc_worker_id(info) * per_worker
        sems = (sem0, sem1)

        def gather(b):
            return pltpu.make_async_copy(table_hbm.at[idx_v.at[b]], rows_v.at[b], sems[b])

        def start(chunk, b):
            pltpu.sync_copy(idx_hbm.at[chunk], idx_v.at[b])
            gather(b).start()

        def finish(chunk, b):
            gather(b).wait()
            pltpu.sync_copy(rows_v.at[b], out_hbm.at[pl.ds(chunk * rows, rows)])

        start(first, 0)

        @pl.loop(0, per_worker, step=2)
        def _(ci):
            chunk = first + ci
            start(chunk + 1, 1)
            finish(chunk, 0)

            @pl.when(ci + 2 < per_worker)
            def _():
                start(chunk + 2, 0)

            finish(chunk + 1, 1)

    return run(table, idx2)


def _expert_kernel(first_ref, count_ref, used_ref,
                   xs_hbm, wgu_ref, wd_ref, y_hbm, wgu_bf, wd_bf, xbuf, ybuf, in_sem, out_sem):
    e = pl.program_id(0)
    n_used = used_ref[0]
    first = first_ref[e]
    n_blocks = count_ref[e]

    def rows(g):
        return pl.ds(pl.multiple_of(g * ROW_BLK, ROW_BLK), ROW_BLK)

    def read(g):
        slot = g % EXP_IN_SLOTS
        return pltpu.make_async_copy(xs_hbm.at[rows(g), :], xbuf.at[slot], in_sem.at[slot])

    def write(g):
        slot = g % EXP_OUT_SLOTS
        return pltpu.make_async_copy(ybuf.at[slot], y_hbm.at[rows(g), :], out_sem.at[slot])

    @pl.when(e == 0)
    def _():
        for g0 in range(EXP_AHEAD):
            @pl.when(g0 < n_used)
            def _():
                read(g0).start()

    @pl.when(n_blocks > 0)
    def _():
        wgu_bf[...] = wgu_ref[0].astype(BF16)
        wd_bf[...] = wd_ref[0].astype(BF16)

    def acquire(g):
        @pl.when(g + EXP_AHEAD < n_used)
        def _():
            read(g + EXP_AHEAD).start()

        read(g).wait()

        @pl.when(g >= EXP_OUT_SLOTS)
        def _():
            write(g - EXP_OUT_SLOTS).wait()

    def gate_up(g):
        return jnp.dot(_unpack_bf16(xbuf[g % EXP_IN_SLOTS]), wgu_bf[...], preferred_element_type=F32)

    def down(g, gu):
        act = _silu(gu[:, :D_EXPERT]) * gu[:, D_EXPERT:]
        yv = jnp.dot(act.astype(BF16), wd_bf[...], preferred_element_type=F32)
        words = pltpu.pack_elementwise([yv[:, :PACK_W], yv[:, PACK_W:]], packed_dtype=BF16)
        ybuf[g % EXP_OUT_SLOTS] = lax.bitcast_convert_type(words, jnp.int32)

    def handle(g, n):
        for i in range(n):
            acquire(g + i)
        gu = gate_up(g)
        for i in range(n):
            gu_next = gate_up(g + i + 1) if i + 1 < n else None
            down(g + i, gu)
            gu = gu_next
        for i in range(n):
            write(g + i).start()

    def group(k, carry):
        handle(first + EXP_GROUP * k, EXP_GROUP)
        return carry

    lax.fori_loop(0, n_blocks // EXP_GROUP, group, 0)

    n = EXP_GROUP // 2
    while n >= 1:
        @pl.when(n_blocks % (2 * n) >= n)
        def _(n=n):
            handle(first + n_blocks - n_blocks % (2 * n), n)
        n //= 2

    @pl.when(e == pl.num_programs(0) - 1)
    def _():
        for back in range(EXP_OUT_SLOTS, 0, -1):
            @pl.when(n_used >= back)
            def _():
                write(n_used - back).wait()


def _experts(first_block, n_blocks, n_used, xs, w_gu, w_down):
    n_rows, w = xs.shape
    d = D_MODEL
    grid_spec = pltpu.PrefetchScalarGridSpec(
        num_scalar_prefetch=3,
        grid=(N_EXPERTS,),
        in_specs=[pl.BlockSpec(memory_space=pl.ANY),
                  pl.BlockSpec((1, d, 2 * D_EXPERT), lambda e, *_: (e, 0, 0)),
                  pl.BlockSpec((1, D_EXPERT, d), lambda e, *_: (e, 0, 0))],
        out_specs=pl.BlockSpec(memory_space=pl.ANY),
        scratch_shapes=[pltpu.VMEM((d, 2 * D_EXPERT), BF16), pltpu.VMEM((D_EXPERT, d), BF16),
                        pltpu.VMEM((EXP_IN_SLOTS, ROW_BLK, w), jnp.int32),
                        pltpu.VMEM((EXP_OUT_SLOTS, ROW_BLK, w), jnp.int32),
                        pltpu.SemaphoreType.DMA((EXP_IN_SLOTS,)),
                        pltpu.SemaphoreType.DMA((EXP_OUT_SLOTS,))],
    )
    return pl.pallas_call(
        _expert_kernel,
        out_shape=jax.ShapeDtypeStruct((n_rows, w), jnp.int32),
        grid_spec=grid_spec,
        compiler_params=_cparams(("arbitrary",)),
        name="experts",
    )(first_block, n_blocks, n_used, xs, w_gu, w_down)


TM_COMBINE = 512


def _combine_kernel(*refs):
    yg_refs = refs[:TOP_K]
    gw_ref, x1_ref, xm_ref, g2_ref, wsgu_ref, wsd_ref, lg_ref, lb_ref, out_ref = refs[TOP_K:]
    gu = jnp.dot(_unpack_bf16(xm_ref[...]), wsgu_ref[...], preferred_element_type=F32)
    act = _silu(gu[:, :D_EXPERT]) * gu[:, D_EXPERT:]
    acc = jnp.dot(act.astype(BF16), wsd_ref[...], preferred_element_type=F32)
    gw = gw_ref[...]
    for j in range(TOP_K):
        acc = acc + gw[:, j:j + 1] * _unpack_rows(yg_refs[j][...])
    z = DEEPNORM_ALPHA * x1_ref[...] + g2_ref[0] * acc
    out_ref[...] = _layer_norm(z) * lg_ref[...] + lb_ref[...]


def _combine(yg, gw, x1, xm, gate2, w_sgu, w_sd, ln_g, ln_b, n_per_batch):
    t, d = x1.shape
    tm = TM_COMBINE
    per = n_per_batch // tm
    steps = t // tm
    tok = lambda w: pl.BlockSpec((tm, w), lambda i: (i, 0))
    full = lambda r, c: pl.BlockSpec((r, c), lambda i: (0, 0))
    slot = lambda j: pl.BlockSpec((tm, PACK_W), lambda i: (j * steps + i, 0))
    return pl.pallas_call(
        _combine_kernel,
        out_shape=jax.ShapeDtypeStruct((t, d), F32),
        grid=(steps,),
        in_specs=[slot(j) for j in range(TOP_K)]
        + [tok(TOP_K), tok(d), tok(PACK_W), pl.BlockSpec((1, 1, d), lambda i: (i // per, 0, 0)),
           full(d, 2 * D_EXPERT), full(D_EXPERT, d), full(1, d), full(1, d)],
        out_specs=tok(d),
        compiler_params=_cparams(("arbitrary",)),
        name="combine",
    )(*([yg] * TOP_K), gw, x1, xm, gate2, w_sgu, w_sd, ln_g, ln_b)


def _mixer_heads(stream, shift, scale, w_in_k, conv_w9, conv_b, gate_b_row, head_g, c0, m0, grid2d):
    b, n, d = stream.shape
    y, f_in = _inproj(stream.reshape(b * n, d), shift, scale, w_in_k, n)
    y = y.reshape(b, n, PROJ_W)
    qk = _conv(y, conv_w9, conv_b, grid2d)
    mread, c_fin, m_fin = _mlstm(qk, y, _gates(y, gate_b_row), c0, m0, head_g)
    return f_in, mread, c_fin, m_fin


def _moe(xm, x1, scores, bias_row, gate2, w_gu, w_down, w_sgu, w_sd, ln_g, ln_b, n_per_batch):
    t = xm.shape[0]
    n_assign = t * TOP_K
    n_rows = n_assign + N_EXPERTS * ROW_BLK
    idx_t, gw, rank_t, counts = _route(scores, bias_row)
    sizes = counts[:, 0].astype(jnp.int32)
    padded = (sizes + ROW_BLK - 1) // ROW_BLK * ROW_BLK
    pad_end = jnp.cumsum(padded)
    pad_start = (pad_end - padded).astype(jnp.int32)
    dest_t = _dest(pad_start, idx_t, rank_t)
    dest3 = dest_t.reshape(TOP_K, t // SC_ROWS, SC_ROWS).transpose(1, 0, 2)

    xs = _sc_dispatch(xm, dest3, n_rows)
    yb = _experts(pad_start // ROW_BLK, (padded // ROW_BLK).astype(jnp.int32),
                  (pad_end[-1:] // ROW_BLK).astype(jnp.int32), xs, w_gu, w_down)
    yg = _sc_gather(yb, dest_t.reshape(n_assign // SC_GATHER_ROWS, SC_GATHER_ROWS))
    return _combine(yg, gw, x1, xm, gate2, w_sgu, w_sd, ln_g, ln_b, n_per_batch)


def kernel(x, c, ctx, c_ctx, w_ada, b_ada, w_in, conv_w, conv_b, gate_b, head_g, w_out, ln1_g, ln1_b,
           w_router, router_bias, w_expert_gu, w_expert_down, w_shared_gu, w_shared_down, ln2_g, ln2_b):
    bsz, n, d = x.shape
    depth = w_ada.shape[0]
    assert depth == 1
    t = bsz * n
    tables = _dft_tables()
    for l in range(depth):
        cvec = jnp.zeros((SUBLANES, d), F32).at[:bsz].set(c).at[bsz].set(c_ctx)
        mod = _ada(cvec, w_ada[l], b_ada[l][None, :])
        mods = mod.reshape(SUBLANES, 6, 1, d)
        mod_x = [mods[:bsz, j] for j in range(6)]
        mod_c = [mods[bsz:bsz + 1, j] for j in range(2)]

        wl = w_in[l]
        g0 = 4 * D_MLSTM
        w_in_k = jnp.concatenate(
            [wl[:, :g0 + N_GATE_COLS], jnp.zeros((d, LANES - N_GATE_COLS), F32),
             wl[:, g0 + N_GATE_COLS:]], axis=1).astype(BF16)
        conv_w9 = conv_w[l].reshape(9, 2 * D_MLSTM)
        conv_b_row = conv_b[l][None, :]
        gate_b_row = jnp.zeros((1, LANES), F32).at[0, :N_GATE_COLS].set(gate_b[l].reshape(-1))
        head_g_row = head_g[l][None, :]

        c0 = jnp.zeros((bsz, N_HEADS, 2, STATE_ROWS, D_HEAD), F32)
        m0 = jnp.zeros((bsz, N_HEADS, 2, SUBLANES, LANES), F32)
        _, _, c_ctx_fin, m_ctx_fin = _mixer_heads(ctx, mod_c[0], mod_c[1], w_in_k, conv_w9, conv_b_row,
                                                  gate_b_row, head_g_row, c0, m0, grid2d=False)
        f_in, mread, _, _ = _mixer_heads(x, mod_x[0], mod_x[1], w_in_k, conv_w9, conv_b_row,
                                      gate_b_row, head_g_row, c_ctx_fin, m_ctx_fin, grid2d=True)

        four = _fourier(f_in.reshape(bsz, DFT_R, DFT_R, D_FOURIER), tables).reshape(t, D_FOURIER)

        w_o = w_out[l].astype(BF16)
        w_r = jnp.zeros((d, LANES), F32).at[:, :N_EXPERTS].set(w_router[l]).astype(BF16)
        x2d = x.reshape(t, d)
        x1, xm, scores = _outproj(x2d, mread.reshape(t, D_MLSTM), four, w_o[:D_MLSTM], w_o[D_MLSTM:],
                                  mod_x[2], ln1_g[l][None, :], ln1_b[l][None, :], mod_x[3], mod_x[4],
                                  w_r, n)

        bias_rep = jnp.broadcast_to(router_bias[l][:, None], (N_EXPERTS, LANES))
        x = _moe(xm, x1, scores, bias_rep, mod_x[5], w_expert_gu[l], w_expert_down[l],
                 w_shared_gu[l].astype(BF16), w_shared_down[l].astype(BF16),
                 ln2_g[l][None, :], ln2_b[l][None, :], n).reshape(bsz, n, d)
    return x
```

```python
import functools

import numpy as np
import jax
import jax.numpy as jnp
from jax import lax
from jax.experimental import pallas as pl
from jax.experimental.pallas import tpu as pltpu
from jax.experimental.pallas import tpu_sc as plsc

F32 = jnp.float32
BF16 = jnp.bfloat16
HIGHEST = lax.Precision.HIGHEST

D_MODEL = 1024
GRID_W = 64
N_HEADS = 4
D_HEAD = 128
D_MLSTM = N_HEADS * D_HEAD
D_FOURIER = 512
FOURIER_GROUP = 64
N_GATE_COLS = 16
CHUNK = 128
N_EXPERTS = 64
TOP_K = 8
D_EXPERT = 256
ROUTED_SCALE = 2.5
DEEPNORM_ALPHA = 2.0 ** 0.25
LN_EPS = 1e-5

LANES = 128
SUBLANES = 8
VMEM_LIMIT = 48 * 1024 * 1024
FOUR_VMEM_LIMIT = 56 * 1024 * 1024

COL_QK = 0
COL_V = 2 * D_MLSTM
COL_O = 3 * D_MLSTM
COL_G = 4 * D_MLSTM
PROJ_W = COL_G + LANES

TN_ADA = 2048
TM = 1024
TM_ROUTE = 1024
ROW_BLK = 256
EXP_GROUP = 8
EXP_AHEAD = 10
EXP_IN_SLOTS = 32
EXP_OUT_SLOTS = 2 * EXP_GROUP
EXP_READ_PRIORITY = 1
PACK_W = D_MODEL // 2
SC_ROWS = 128
SC_GATHER_ROWS = 64
CONV_PAD = 72
CONV_RB = 256
DFT_R = 64


def _cparams(sem, vmem=VMEM_LIMIT):
    return pltpu.CompilerParams(dimension_semantics=sem, vmem_limit_bytes=vmem)


def _layer_norm(x):
    mu = jnp.mean(x, axis=-1, keepdims=True)
    xc = x - mu
    var = jnp.mean(xc * xc, axis=-1, keepdims=True)
    return xc * lax.rsqrt(var + LN_EPS)


def _silu(x):
    return x * jax.nn.sigmoid(x)


def _pack_pair(lo, hi):
    as_bits = lambda u: lax.bitcast_convert_type(u.astype(BF16).astype(F32), jnp.uint32)
    word = as_bits(hi) | lax.shift_right_logical(as_bits(lo), jnp.uint32(16))
    return lax.bitcast_convert_type(word, jnp.int32)


def _unpack_pair_bf16(w):
    lo = lax.bitcast_convert_type(w.astype(jnp.int16), BF16)
    hi = lax.bitcast_convert_type(lax.shift_right_logical(w, jnp.int32(16)).astype(jnp.int16), BF16)
    return lo, hi


def _pack_rows(v):
    return _pack_pair(v[:, :PACK_W], v[:, PACK_W:])


def _unpack_bf16(w):
    return jnp.concatenate(_unpack_pair_bf16(w), axis=-1)


def _unpack_rows(w):
    u = lax.bitcast_convert_type(w, jnp.uint32)
    lo = lax.bitcast_convert_type(lax.shift_left(u, jnp.uint32(16)), F32)
    hi = lax.bitcast_convert_type(u & jnp.uint32(0xFFFF0000), F32)
    return jnp.concatenate([lo, hi], axis=1)


def _ada_kernel(c_ref, w_ref, b_ref, o_ref):
    s = _silu(c_ref[...]).astype(BF16)
    o_ref[...] = jnp.dot(s, w_ref[...].astype(BF16), preferred_element_type=F32) + b_ref[...]


def _ada(cvec, w_ada, b_ada):
    rows, d = cvec.shape
    n_out = w_ada.shape[1]
    tn = TN_ADA
    return pl.pallas_call(
        _ada_kernel,
        out_shape=jax.ShapeDtypeStruct((rows, n_out), F32),
        grid=(n_out // tn,),
        in_specs=[pl.BlockSpec((rows, d), lambda j: (0, 0)),
                  pl.BlockSpec((d, tn), lambda j: (0, j)),
                  pl.BlockSpec((1, tn), lambda j: (0, j))],
        out_specs=pl.BlockSpec((rows, tn), lambda j: (0, j)),
        compiler_params=_cparams(("arbitrary",)),
        name="ada",
    )(cvec, w_ada, b_ada)


def _inproj_kernel(x_ref, sh_ref, sc_ref, w_ref, y_ref, f_ref):
    half = x_ref.shape[0] // 2
    for h in range(2):
        r = slice(h * half, (h + 1) * half)
        u = _layer_norm(x_ref[r, :]) * (1.0 + sc_ref[0]) + sh_ref[0]
        proj = jnp.dot(u.astype(BF16), w_ref[...], preferred_element_type=F32)
        y_ref[r, :] = proj[:, :PROJ_W]
        f_ref[r, :] = proj[:, PROJ_W:]


def _inproj(x2d, shift, scale, w, n_per_batch):
    t, d = x2d.shape
    if shift.shape[0] == 1:
        tm = min(TM, t)
        per = t // tm
    else:
        tm = min(TM, n_per_batch)
        per = n_per_batch // tm
    return pl.pallas_call(
        _inproj_kernel,
        out_shape=(jax.ShapeDtypeStruct((t, PROJ_W), F32), jax.ShapeDtypeStruct((t, D_FOURIER), F32)),
        grid=(t // tm,),
        in_specs=[pl.BlockSpec((tm, d), lambda i: (i, 0)),
                  pl.BlockSpec((1, 1, d), lambda i: (i // per, 0, 0)),
                  pl.BlockSpec((1, 1, d), lambda i: (i // per, 0, 0)),
                  pl.BlockSpec((d, PROJ_W + D_FOURIER), lambda i: (0, 0))],
        out_specs=(pl.BlockSpec((tm, PROJ_W), lambda i: (i, 0)),
                   pl.BlockSpec((tm, D_FOURIER), lambda i: (i, 0))),
        compiler_params=_cparams(("arbitrary",)),
        name="inproj",
    )(x2d, shift, scale, w)


def _conv_kernel(y_ref, w_ref, b_ref, o_ref, pad_ref, *, n, grid2d, tiles):
    zeros = jnp.zeros((CONV_PAD, LANES), F32)
    pad_ref[0:CONV_PAD, :] = zeros
    pad_ref[CONV_PAD + n:CONV_PAD + n + CONV_PAD, :] = zeros
    rb = min(CONV_RB, n)
    col = lax.broadcasted_iota(jnp.int32, (rb, LANES), 0) % GRID_W
    not_first = col >= 1
    not_last = col <= GRID_W - 2
    row_taps = (0, 1, 2) if grid2d else (1,)
    for ct in range(tiles):
        lanes = slice(ct * LANES, (ct + 1) * LANES)
        pad_ref[CONV_PAD:CONV_PAD + n, :] = y_ref[0, :, lanes]
        k_scale = jnp.where(pl.program_id(1) * tiles + ct >= N_HEADS, D_HEAD ** -0.5, 1.0).astype(F32)
        bias = b_ref[:, lanes]
        for blk in range(n // rb):
            r0 = CONV_PAD + blk * rb
            acc = jnp.zeros((rb, LANES), F32)
            for dj in range(3):
                part = jnp.zeros((rb, LANES), F32)
                for di in row_taps:
                    off = (di - 1) * GRID_W + (dj - 1)
                    tap = w_ref[di * 3 + dj:di * 3 + dj + 1, lanes]
                    part = part + tap * pad_ref[r0 + off:r0 + off + rb, :]
                if grid2d and dj == 0:
                    part = jnp.where(not_first, part, 0.0)
                if grid2d and dj == 2:
                    part = jnp.where(not_last, part, 0.0)
                acc = acc + part
            val = _silu(acc + bias) * k_scale
            o_ref[0, blk * rb:(blk + 1) * rb, lanes] = val.astype(o_ref.dtype)


def _conv(y3d, conv_w9, conv_b, grid2d):
    b, n, _ = y3d.shape
    n_ct = 2 * D_MLSTM // LANES
    tiles = 1 if grid2d else n_ct
    w = tiles * LANES
    return pl.pallas_call(
        functools.partial(_conv_kernel, n=n, grid2d=grid2d, tiles=tiles),
        out_shape=jax.ShapeDtypeStruct((b, n, 2 * D_MLSTM), BF16),
        grid=(b, n_ct // tiles),
        in_specs=[pl.BlockSpec((1, n, w), lambda i, c: (i, 0, c)),
                  pl.BlockSpec((9, w), lambda i, c: (0, c)),
                  pl.BlockSpec((1, w), lambda i, c: (0, c))],
        out_specs=pl.BlockSpec((1, n, w), lambda i, c: (i, 0, c)),
        scratch_shapes=[pltpu.VMEM((n + 2 * CONV_PAD, LANES), F32)],
        compiler_params=_cparams(("arbitrary", "arbitrary")),
        name="conv2d" if grid2d else "conv1d",
    )(y3d, conv_w9, conv_b)


STATE_ROWS = D_HEAD + 32
GATE_A, GATE_CM, GATE_B = 0, 1, 2
GATE_GROUP = 32


def _gates_kernel(g_ref, gb_ref, rv_ref, *, nc):
    u_i = lax.broadcasted_iota(jnp.int32, (CHUNK, CHUNK), 0)
    t_i = lax.broadcasted_iota(jnp.int32, (CHUNK, CHUNK), 1)
    prefix = (u_i <= t_i).astype(F32)
    suffix = (u_i >= t_i).astype(F32)
    grp = min(GATE_GROUP, nc)
    rows = grp * SUBLANES
    lane = lax.broadcasted_iota(jnp.int32, (rows, LANES), 1)
    is_fwd = lax.broadcasted_iota(jnp.int32, (rows, LANES), 0) % SUBLANES < N_HEADS
    h = N_HEADS

    def running_max(a):
        fwd, bwd = a, a
        sh = 1
        while sh < CHUNK:
            fwd = jnp.where(lane >= sh, jnp.maximum(fwd, pltpu.roll(fwd, sh, axis=1)), fwd)
            bwd = jnp.where(lane < CHUNK - sh, jnp.maximum(bwd, pltpu.roll(bwd, CHUNK - sh, axis=1)), bwd)
            sh *= 2
        return jnp.where(is_fwd, fwd, bwd)

    def body(i, carry):
        c0 = i * grp
        r0 = pl.multiple_of(c0 * CHUNK, CHUNK)
        pre = g_ref[0, pl.ds(r0, grp * CHUNK), :] + gb_ref[...]
        pre_t = jnp.concatenate(
            [pre[j * CHUNK:(j + 1) * CHUNK].T[:N_GATE_COLS, :] for j in range(grp)], axis=0)
        lf_t = -(jnp.maximum(-pre_t, 0.0) + jnp.log1p(jnp.exp(-jnp.abs(pre_t))))
        cum_f = jnp.dot(lf_t, prefix, precision=HIGHEST, preferred_element_type=F32)
        cum_b = jnp.dot(lf_t, suffix, precision=HIGHEST, preferred_element_type=F32)
        pick = lambda t, j, q: t[j * N_GATE_COLS + q * h:j * N_GATE_COLS + (q + 1) * h]
        b8 = jnp.concatenate([x for j in range(grp) for x in (pick(cum_f, j, 1), pick(cum_b, j, 3))], axis=0)
        li8 = jnp.concatenate([x for j in range(grp) for x in (pick(pre_t, j, 0), pick(pre_t, j, 2))], axis=0)
        a8 = li8 - b8
        as_chunks = lambda t: t.reshape(grp, SUBLANES, LANES)
        rv_ref[0, pl.ds(c0, grp), GATE_A] = as_chunks(a8)
        rv_ref[0, pl.ds(c0, grp), GATE_CM] = as_chunks(running_max(a8))
        rv_ref[0, pl.ds(c0, grp), GATE_B] = as_chunks(b8)
        return carry

    lax.fori_loop(0, nc // grp, body, 0)


def _gates(y3d, gate_b_row):
    b, n, _ = y3d.shape
    nc = n // CHUNK
    return pl.pallas_call(
        functools.partial(_gates_kernel, nc=nc),
        out_shape=jax.ShapeDtypeStruct((b, nc, 3, SUBLANES, LANES), F32),
        grid=(b,),
        in_specs=[pl.BlockSpec((1, n, LANES), lambda i: (i, 0, COL_G // LANES)),
                  pl.BlockSpec((1, LANES), lambda i: (0, 0))],
        out_specs=pl.BlockSpec((1, nc, 3, SUBLANES, LANES), lambda i: (i, 0, 0, 0, 0)),
        compiler_params=_cparams(("arbitrary",)),
        name="gates",
    )(y3d, gate_b_row)


def _mlstm_kernel(q_ref, k_ref, v_ref, o_ref, rv_ref, c0_ref, m0_ref, hg_ref,
                  out_ref, cfin_ref, mfin_ref, hf_ref, hb_ref, cst_ref, mst_ref, *, nc, heads):
    cst_ref[...] = c0_ref[0]
    mst_ref[...] = m0_ref[0]
    s_i = lax.broadcasted_iota(jnp.int32, (CHUNK, CHUNK), 0)
    t_i = lax.broadcasted_iota(jnp.int32, (CHUNK, CHUNK), 1)
    ones_row = (s_i[:STATE_ROWS - D_HEAD] == 0).astype(F32)

    def one_chunk(c, d, mask_t, last, h_ref, hh):
        r0 = pl.multiple_of(c * CHUNK, CHUNK)
        lanes = slice(hh * D_HEAD, (hh + 1) * D_HEAD)
        g = d * N_HEADS + pl.program_id(1) * heads + hh
        gate_row = lambda tbl: rv_ref[0, pl.ds(c, 1), tbl, pl.ds(g, 1), :].reshape(1, LANES)
        a_row, cm_row, b_row = gate_row(GATE_A), gate_row(GATE_CM), gate_row(GATE_B)
        q_t = q_ref[0, pl.ds(r0, CHUNK), lanes].T
        k = k_ref[0, pl.ds(r0, CHUNK), lanes]
        v_t_aug = jnp.concatenate([v_ref[0, pl.ds(r0, CHUNK), lanes].T, ones_row], axis=0)
        m_prev = mst_ref[hh, d, 0:1, :]
        m_row = jnp.maximum(m_prev, cm_row)
        a_bc = jnp.broadcast_to(a_row, (CHUNK, CHUNK)).T
        d_t = jnp.where(mask_t, jnp.exp(a_bc - m_row), 0.0)
        s_t = jnp.dot(k, q_t, preferred_element_type=F32) * d_t
        cs = cst_ref[hh, d]
        rhs = jnp.concatenate(
            [s_t.astype(BF16), (q_t.astype(F32) * jnp.exp(m_prev - m_row)).astype(BF16)], axis=0)
        lhs = jnp.concatenate([v_t_aug.astype(BF16), cs.astype(BF16)], axis=1)
        h_t = jnp.dot(lhs, rhs, preferred_element_type=F32)
        den = h_t[D_HEAD:D_HEAD + 1, :]
        floor = jnp.exp(-b_row - m_row)
        h_ref[hh, pl.ds(c, 1)] = (h_t[:D_HEAD] / jnp.maximum(jnp.abs(den), floor))[None]
        m_last = m_row[:, last:last + 1]
        wv_t = (v_t_aug * jnp.exp(a_row - m_last)).astype(BF16)
        cst_ref[hh, d] = jnp.exp(m_prev - m_last) * cs + jnp.dot(wv_t, k, preferred_element_type=F32)
        mst_ref[hh, d] = jnp.broadcast_to(b_row[:, last:last + 1] + m_last, (SUBLANES, LANES))

    def body(c, carry):
        for hh in range(heads):
            one_chunk(c, 0, s_i <= t_i, CHUNK - 1, hf_ref, hh)
            one_chunk(nc - 1 - c, 1, s_i >= t_i, 0, hb_ref, hh)
        return carry

    lax.fori_loop(0, nc, body, 0, unroll=min(16 // heads, nc))
    cfin_ref[0] = cst_ref[...]
    mfin_ref[0] = mst_ref[...]

    def readout(c, carry):
        r0 = pl.multiple_of(c * CHUNK, CHUNK)
        for hh in range(heads):
            lanes = slice(hh * D_HEAD, (hh + 1) * D_HEAD)
            h_t = hf_ref[hh, pl.ds(c, 1)][0] + hb_ref[hh, pl.ds(c, 1)][0]
            hc = h_t - jnp.mean(h_t, axis=0, keepdims=True)
            hn = (hc * lax.rsqrt(jnp.mean(hc * hc, axis=0, keepdims=True) + LN_EPS)).T
            gate = jax.nn.sigmoid(o_ref[0, pl.ds(r0, CHUNK), lanes])
            out_ref[0, pl.ds(r0, CHUNK), lanes] = (hn * hg_ref[:, lanes] * gate).astype(out_ref.dtype)
        return carry

    lax.fori_loop(0, nc, readout, 0, unroll=min(16, nc))


def _mlstm(qk, y3d, rowv, c0, m0, head_g):
    b, n, _ = qk.shape
    nc = n // CHUNK
    heads = N_HEADS if nc <= 4 else 1
    w = heads * D_HEAD
    v_blk = COL_V // w
    o_blk = COL_O // w
    tok = lambda off: pl.BlockSpec((1, n, w), lambda i, h: (i, 0, off + h))
    state_c = pl.BlockSpec((1, heads, 2, STATE_ROWS, D_HEAD), lambda i, h: (i, h, 0, 0, 0))
    state_m = pl.BlockSpec((1, heads, 2, SUBLANES, LANES), lambda i, h: (i, h, 0, 0, 0))
    return pl.pallas_call(
        functools.partial(_mlstm_kernel, nc=nc, heads=heads),
        out_shape=(jax.ShapeDtypeStruct((b, n, D_MLSTM), BF16),
                   jax.ShapeDtypeStruct((b, N_HEADS, 2, STATE_ROWS, D_HEAD), F32),
                   jax.ShapeDtypeStruct((b, N_HEADS, 2, SUBLANES, LANES), F32)),
        grid=(b, N_HEADS // heads),
        in_specs=[tok(0), tok(D_MLSTM // w), tok(v_blk), tok(o_blk),
                  pl.BlockSpec((1, nc, 3, SUBLANES, LANES), lambda i, h: (i, 0, 0, 0, 0)),
                  state_c, state_m,
                  pl.BlockSpec((1, w), lambda i, h: (0, h))],
        out_specs=(pl.BlockSpec((1, n, w), lambda i, h: (i, 0, h)), state_c, state_m),
        scratch_shapes=[pltpu.VMEM((heads, nc, D_HEAD, CHUNK), F32), pltpu.VMEM((heads, nc, D_HEAD, CHUNK), F32),
                        pltpu.VMEM((heads, 2, STATE_ROWS, D_HEAD), F32),
                        pltpu.VMEM((heads, 2, SUBLANES, LANES), F32)],
        compiler_params=_cparams(("arbitrary", "arbitrary")),
        name="mlstm",
    )(qk, qk, y3d, y3d, rowv, c0, m0, head_g)


def _dft_tables():
    r = DFT_R
    idx = np.arange(r)
    cg = np.cos(2 * np.pi * np.outer(idx, idx) / FOURIER_GROUP)
    sg = np.sin(2 * np.pi * np.outer(idx, idx) / FOURIER_GROUP)
    n_grp = D_FOURIER // FOURIER_GROUP
    bdc = np.kron(np.eye(n_grp), cg) / 8.0
    bds = np.kron(np.eye(n_grp), sg) / 8.0
    k1 = idx[:, None]
    a = idx[None, :]
    m1 = np.zeros((r, 2 * r, 2 * r))
    for b2 in range(r):
        th = 2 * np.pi * k1 * (r * a + b2) / (r * r)
        ec, es = np.cos(th), np.sin(th)
        m1[b2] = np.block([[ec, -es], [es, ec]]) / 8.0
    th2 = 2 * np.pi * np.outer(idx, idx) / r
    m2 = np.concatenate([np.cos(th2), -np.sin(th2)], axis=1) / 8.0
    as_bf16 = lambda t: jnp.asarray(t, dtype=F32).astype(BF16)
    return as_bf16(bdc), as_bf16(bds), as_bf16(m1), as_bf16(m2)


FOUR_BB = 8


N_SLABS = D_FOURIER // LANES


def _to_slabs(slab_ref, val):
    for s in range(N_SLABS):
        slab_ref[s] = val[:, s * LANES:(s + 1) * LANES]


def _strided_rows(slab_ref, start):
    return jnp.concatenate(
        [slab_ref[s, pl.ds(start, DFT_R, stride=FOUR_BB), :] for s in range(N_SLABS)], axis=1)


def _four_kernel(f_ref, bdc_ref, bds_ref, m1_ref, m2_ref, o_ref, z_ref, ps_ref, qs_ref, zs_ref, os_ref):
    r = DFT_R

    def stage1(blk, carry):
        b0 = pl.multiple_of(blk * FOUR_BB, FOUR_BB)
        fb = f_ref[0, :, pl.ds(b0, FOUR_BB), :].reshape(r * FOUR_BB, D_FOURIER).astype(BF16)
        _to_slabs(ps_ref, jnp.dot(fb, bdc_ref[...], preferred_element_type=F32))
        _to_slabs(qs_ref, jnp.dot(fb, bds_ref[...], preferred_element_type=F32))
        for j in range(FOUR_BB):
            x2 = jnp.concatenate([_strided_rows(ps_ref, j), _strided_rows(qs_ref, j)], axis=0).astype(BF16)
            zz = jnp.dot(m1_ref[b0 + j], x2, preferred_element_type=F32)
            z_ref[b0 + j] = _pack_pair(zz[:r], zz[r:])
        return carry

    lax.fori_loop(0, r // FOUR_BB, stage1, 0)

    def stage2(blk, carry):
        k0 = pl.multiple_of(blk * FOUR_BB, FOUR_BB)
        _to_slabs(zs_ref, z_ref[:, pl.ds(k0, FOUR_BB), :].reshape(r * FOUR_BB, D_FOURIER))
        for j in range(FOUR_BB):
            zc, zs = _unpack_pair_bf16(_strided_rows(zs_ref, j))
            o = jnp.dot(m2_ref[...], jnp.concatenate([zc, zs], axis=0), preferred_element_type=F32)
            for s in range(N_SLABS):
                os_ref[s, pl.ds(j, r, stride=FOUR_BB), :] = o[:, s * LANES:(s + 1) * LANES]
        for s in range(N_SLABS):
            o_ref[0, :, pl.ds(k0, FOUR_BB), s * LANES:(s + 1) * LANES] = os_ref[s].reshape(r, FOUR_BB, LANES)
        return carry

    lax.fori_loop(0, r // FOUR_BB, stage2, 0)


def _fourier(y4, tables):
    bdc, bds, m1, m2 = tables
    b = y4.shape[0]
    r = DFT_R
    slab = lambda dt: pltpu.VMEM((N_SLABS, r * FOUR_BB, LANES), dt)
    const = lambda *shape: pl.BlockSpec(shape, lambda i: (0,) * len(shape))
    return pl.pallas_call(
        _four_kernel,
        out_shape=jax.ShapeDtypeStruct((b, r, r, D_FOURIER), F32),
        grid=(b,),
        in_specs=[pl.BlockSpec((1, r, r, D_FOURIER), lambda i: (i, 0, 0, 0)),
                  const(D_FOURIER, D_FOURIER), const(D_FOURIER, D_FOURIER),
                  const(r, 2 * r, 2 * r), const(r, 2 * r)],
        out_specs=pl.BlockSpec((1, r, r, D_FOURIER), lambda i: (i, 0, 0, 0)),
        scratch_shapes=[pltpu.VMEM((r, r, D_FOURIER), jnp.int32), slab(F32), slab(F32), slab(jnp.int32), slab(F32)],
        compiler_params=_cparams(("arbitrary",), vmem=FOUR_VMEM_LIMIT),
        name="fourier",
    )(y4, bdc, bds, m1, m2)


OUTPROJ_PARTS = 4


def _outproj_kernel(x_ref, mr_ref, fo_ref, wm_ref, wf_ref, g1_ref, lg_ref, lb_ref,
                    sh_ref, sc_ref, wr_ref, x1_ref, xm_ref, s_ref):
    part = x_ref.shape[0] // OUTPROJ_PARTS
    rows = lambda h: slice(h * part, (h + 1) * part)
    mixer = lambda r: (jnp.dot(mr_ref[r, :], wm_ref[...], preferred_element_type=F32)
                       + jnp.dot(fo_ref[r, :].astype(BF16), wf_ref[...], preferred_element_type=F32))
    mix_next = mixer(rows(0))
    for h in range(OUTPROJ_PARTS):
        r = rows(h)
        mix = mix_next
        if h + 1 < OUTPROJ_PARTS:
            mix_next = mixer(rows(h + 1))
        x1 = _layer_norm(DEEPNORM_ALPHA * x_ref[r, :] + g1_ref[0] * mix) * lg_ref[...] + lb_ref[...]
        x1_ref[r, :] = x1
        xm = (_layer_norm(x1) * (1.0 + sc_ref[0]) + sh_ref[0]).astype(BF16)
        xm_ref[r, :] = _pack_rows(xm)
        logits = jnp.dot(xm, wr_ref[...], preferred_element_type=F32)
        s_ref[r, :] = jax.nn.sigmoid(logits)


def _outproj(x2d, mread, four, w_m, w_f, gate1, ln_g, ln_b, shift2, scale2, w_router, n_per_batch):
    t, d = x2d.shape
    per = n_per_batch // TM
    tokd = lambda w: pl.BlockSpec((TM, w), lambda i: (i, 0))
    full = lambda r, c: pl.BlockSpec((r, c), lambda i: (0, 0))
    mod = pl.BlockSpec((1, 1, d), lambda i: (i // per, 0, 0))
    return pl.pallas_call(
        _outproj_kernel,
        out_shape=(jax.ShapeDtypeStruct((t, d), F32),
                   jax.ShapeDtypeStruct((t, PACK_W), jnp.int32),
                   jax.ShapeDtypeStruct((t, LANES), F32)),
        grid=(t // TM,),
        in_specs=[tokd(d), tokd(D_MLSTM), tokd(D_FOURIER), full(D_MLSTM, d), full(D_FOURIER, d),
                  mod, full(1, d), full(1, d), mod, mod, full(d, LANES)],
        out_specs=(tokd(d), tokd(PACK_W), tokd(LANES)),
        compiler_params=_cparams(("arbitrary",)),
        name="outproj",
    )(x2d, mread, four, w_m, w_f, gate1, ln_g, ln_b, shift2, scale2, w_router)


def _route_kernel(s_ref, b_ref, idx_ref, gw_ref, rank_ref, cnt_ref, carry_ref):
    @pl.when(pl.program_id(0) == 0)
    def _():
        carry_ref[...] = jnp.zeros_like(carry_ref)

    tm = s_ref.shape[0]
    reps = tm // LANES
    s_t = s_ref[...].T[:N_EXPERTS, :]
    e_i = lax.broadcasted_iota(jnp.int32, (N_EXPERTS, tm), 0).astype(F32)
    neg_inf = jnp.float32(-jnp.inf)
    sb = s_t + jnp.tile(b_ref[...], (1, reps))
    picks, sels = [], []
    taken = jnp.zeros((N_EXPERTS, tm), F32)
    for _ in range(TOP_K):
        best = jnp.max(sb, axis=0, keepdims=True)
        pick = jnp.min(jnp.where(sb == best, e_i, float(N_EXPERTS)), axis=0, keepdims=True)
        onehot = e_i == pick
        sels.append(jnp.sum(jnp.where(onehot, s_t, 0.0), axis=0, keepdims=True))
        picks.append(pick)
        sb = jnp.where(onehot, neg_inf, sb)
        taken = taken + onehot.astype(F32)
    u_i = lax.broadcasted_iota(jnp.int32, (tm, tm), 0)
    t_i = lax.broadcasted_iota(jnp.int32, (tm, tm), 1)
    before = (u_i < t_i).astype(BF16)
    taken_bf = taken.astype(BF16)
    rank_all = jnp.tile(carry_ref[...], (1, reps)) + jnp.dot(taken_bf, before, preferred_element_type=F32)
    carry_ref[...] = carry_ref[...] + jnp.dot(taken_bf, jnp.ones((tm, LANES), BF16),
                                              preferred_element_type=F32)
    cnt_ref[...] = carry_ref[...]
    total = sels[0]
    for v in sels[1:]:
        total = total + v
    ranks = [jnp.sum(jnp.where(e_i == picks[j], rank_all, 0.0), axis=0, keepdims=True) for j in range(TOP_K)]
    idx_ref[...] = jnp.concatenate(picks, axis=0).astype(jnp.int32)
    rank_ref[...] = jnp.concatenate(ranks, axis=0).astype(jnp.int32)
    gw_t = jnp.concatenate([ROUTED_SCALE * v / total for v in sels]
                           + [jnp.zeros((LANES - TOP_K, tm), F32)], axis=0)
    gw_ref[...] = gw_t.T[:, :TOP_K]


def _route(scores, bias_rep):
    t = scores.shape[0]
    tok = lambda w: pl.BlockSpec((TM_ROUTE, w), lambda i: (i, 0))
    slot_major = pl.BlockSpec((TOP_K, TM_ROUTE), lambda i: (0, i))
    per_expert = pl.BlockSpec((N_EXPERTS, LANES), lambda i: (0, 0))
    return pl.pallas_call(
        _route_kernel,
        out_shape=(jax.ShapeDtypeStruct((TOP_K, t), jnp.int32),
                   jax.ShapeDtypeStruct((t, TOP_K), F32),
                   jax.ShapeDtypeStruct((TOP_K, t), jnp.int32),
                   jax.ShapeDtypeStruct((N_EXPERTS, LANES), F32)),
        grid=(t // TM_ROUTE,),
        in_specs=[tok(LANES), per_expert],
        out_specs=(slot_major, tok(TOP_K), slot_major, per_expert),
        scratch_shapes=[pltpu.VMEM((N_EXPERTS, LANES), F32)],
        compiler_params=_cparams(("arbitrary",)),
        name="route",
    )(scores, bias_rep)


TN_DEST = 2048


def _dest_kernel(ps_ref, idx_ref, rank_ref, dest_ref):
    idx = idx_ref[...]
    start = jnp.zeros(idx.shape, jnp.int32)
    for e in range(N_EXPERTS):
        start = jnp.where(idx == e, ps_ref[e], start)
    dest_ref[...] = start + rank_ref[...]


def _dest(pad_start, idx_t, rank_t):
    t = idx_t.shape[1]
    blk = lambda: pl.BlockSpec((TOP_K, TN_DEST), lambda i, ps: (0, i))
    grid_spec = pltpu.PrefetchScalarGridSpec(
        num_scalar_prefetch=1, grid=(t // TN_DEST,), in_specs=[blk(), blk()], out_specs=blk())
    return pl.pallas_call(
        _dest_kernel,
        out_shape=jax.ShapeDtypeStruct((TOP_K, t), jnp.int32),
        grid_spec=grid_spec,
        compiler_params=_cparams(("arbitrary",)),
        name="dest",
    )(pad_start, idx_t, rank_t)


def _sc_worker_id(info):
    return lax.axis_index("s") * info.num_cores + lax.axis_index("c")


def _sc_dispatch(xm, dest3, n_rows):
    t, w = xm.shape
    info = plsc.get_sparse_core_info()
    n_workers = info.num_cores * info.num_subcores
    per_worker = t // SC_ROWS // n_workers
    mesh = plsc.VectorSubcoreMesh(core_axis_name="c", subcore_axis_name="s")

    @functools.partial(
        pl.kernel, mesh=mesh,
        out_type=jax.ShapeDtypeStruct((n_rows, w), xm.dtype),
        scratch_types=[pltpu.VMEM((TOP_K, SC_ROWS), jnp.int32), pltpu.VMEM((SC_ROWS, w), xm.dtype)],
        name="sc_dispatch",
    )
    def run(xm_hbm, dest_hbm, xs_hbm, idx_v, rows_v):
        first = _sc_worker_id(info) * per_worker

        @pl.loop(0, per_worker)
        def _(ci):
            chunk = first + ci
            pltpu.sync_copy(dest_hbm.at[chunk], idx_v)
            pltpu.sync_copy(xm_hbm.at[pl.ds(chunk * SC_ROWS, SC_ROWS)], rows_v)
            for j in range(TOP_K):
                pltpu.sync_copy(rows_v, xs_hbm.at[idx_v.at[j]])

    return run(xm, dest3)


def _sc_gather(table, idx2):
    n_chunks, rows = idx2.shape
    w = table.shape[1]
    info = plsc.get_sparse_core_info()
    n_workers = info.num_cores * info.num_subcores
    per_worker = n_chunks // n_workers
    assert per_worker % 2 == 0
    mesh = plsc.VectorSubcoreMesh(core_axis_name="c", subcore_axis_name="s")

    @functools.partial(
        pl.kernel, mesh=mesh,
        out_type=jax.ShapeDtypeStruct((n_chunks * rows, w), table.dtype),
        scratch_types=[pltpu.VMEM((2, rows), jnp.int32), pltpu.VMEM((2, rows, w), table.dtype),
                       pltpu.SemaphoreType.DMA, pltpu.SemaphoreType.DMA],
        name="sc_gather",
    )
    def run(table_hbm, idx_hbm, out_hbm, idx_v, rows_v, sem0, sem1):
        first = _sc_worker_id(info) * per_worker
        sems = (sem0, sem1)

        def gather(b):
            return pltpu.make_async_copy(table_hbm.at[idx_v.at[b]], rows_v.at[b], sems[b])

        def start(chunk, b):
            pltpu.sync_copy(idx_hbm.at[chunk], idx_v.at[b])
            gather(b).start()

        def finish(chunk, b):
            gather(b).wait()
            pltpu.sync_copy(rows_v.at[b], out_hbm.at[pl.ds(chunk * rows, rows)])

        start(first, 0)

        @pl.loop(0, per_worker, step=2)
        def _(ci):
            chunk = first + ci
            start(chunk + 1, 1)
            finish(chunk, 0)

            @pl.when(ci + 2 < per_worker)
            def _():
                start(chunk + 2, 0)

            finish(chunk + 1, 1)

    return run(table, idx2)


def _expert_kernel(first_ref, count_ref, used_ref,
                   xs_hbm, wgu_ref, wd_ref, y_hbm, wgu_bf, wd_bf, xbuf, ybuf, in_sem, out_sem):
    e = pl.program_id(0)
    n_used = used_ref[0]
    first = first_ref[e]
    n_blocks = count_ref[e]

    def rows(g):
        return pl.ds(pl.multiple_of(g * ROW_BLK, ROW_BLK), ROW_BLK)

    def read(g):
        slot = g % EXP_IN_SLOTS
        return pltpu.make_async_copy(xs_hbm.at[rows(g), :], xbuf.at[slot], in_sem.at[slot])

    def write(g):
        slot = g % EXP_OUT_SLOTS
        return pltpu.make_async_copy(ybuf.at[slot], y_hbm.at[rows(g), :], out_sem.at[slot])

    @pl.when(e == 0)
    def _():
        for g0 in range(EXP_AHEAD):
            @pl.when(g0 < n_used)
            def _():
                read(g0).start(priority=EXP_READ_PRIORITY)

    @pl.when(n_blocks > 0)
    def _():
        wgu_bf[...] = wgu_ref[0].astype(BF16)
        wd_bf[...] = wd_ref[0].astype(BF16)

    def acquire(g):
        @pl.when(g + EXP_AHEAD < n_used)
        def _():
            read(g + EXP_AHEAD).start(priority=EXP_READ_PRIORITY)

        read(g).wait()

        @pl.when(g >= EXP_OUT_SLOTS)
        def _():
            write(g - EXP_OUT_SLOTS).wait()

    def gate_up(g):
        return jnp.dot(_unpack_bf16(xbuf[g % EXP_IN_SLOTS]), wgu_bf[...], preferred_element_type=F32)

    def down(g, gu):
        act = _silu(gu[:, :D_EXPERT]) * gu[:, D_EXPERT:]
        yv = jnp.dot(act.astype(BF16), wd_bf[...], preferred_element_type=F32)
        words = pltpu.pack_elementwise([yv[:, :PACK_W], yv[:, PACK_W:]], packed_dtype=BF16)
        ybuf[g % EXP_OUT_SLOTS] = lax.bitcast_convert_type(words, jnp.int32)

    def handle(g, n):
        for i in range(n):
            acquire(g + i)
        gu = gate_up(g)
        for i in range(n):
            gu_next = gate_up(g + i + 1) if i + 1 < n else None
            down(g + i, gu)
            gu = gu_next
        for i in range(n):
            write(g + i).start()

    def group(k, carry):
        handle(first + EXP_GROUP * k, EXP_GROUP)
        return carry

    lax.fori_loop(0, n_blocks // EXP_GROUP, group, 0)

    n = EXP_GROUP // 2
    while n >= 1:
        @pl.when(n_blocks % (2 * n) >= n)
        def _(n=n):
            handle(first + n_blocks - n_blocks % (2 * n), n)
        n //= 2

    @pl.when(e == pl.num_programs(0) - 1)
    def _():
        for back in range(EXP_OUT_SLOTS, 0, -1):
            @pl.when(n_used >= back)
            def _():
                write(n_used - back).wait()


def _experts(first_block, n_blocks, n_used, xs, w_gu, w_down):
    n_rows, w = xs.shape
    d = D_MODEL
    grid_spec = pltpu.PrefetchScalarGridSpec(
        num_scalar_prefetch=3,
        grid=(N_EXPERTS,),
        in_specs=[pl.BlockSpec(memory_space=pl.ANY),
                  pl.BlockSpec((1, d, 2 * D_EXPERT), lambda e, *_: (e, 0, 0)),
                  pl.BlockSpec((1, D_EXPERT, d), lambda e, *_: (e, 0, 0))],
        out_specs=pl.BlockSpec(memory_space=pl.ANY),
        scratch_shapes=[pltpu.VMEM((d, 2 * D_EXPERT), BF16), pltpu.VMEM((D_EXPERT, d), BF16),
                        pltpu.VMEM((EXP_IN_SLOTS, ROW_BLK, w), jnp.int32),
                        pltpu.VMEM((EXP_OUT_SLOTS, ROW_BLK, w), jnp.int32),
                        pltpu.SemaphoreType.DMA((EXP_IN_SLOTS,)),
                        pltpu.SemaphoreType.DMA((EXP_OUT_SLOTS,))],
    )
    return pl.pallas_call(
        _expert_kernel,
        out_shape=jax.ShapeDtypeStruct((n_rows, w), jnp.int32),
        grid_spec=grid_spec,
        compiler_params=_cparams(("arbitrary",)),
        name="experts",
    )(first_block, n_blocks, n_used, xs, w_gu, w_down)


TM_COMBINE = 512


def _combine_kernel(*refs):
    yg_refs = refs[:TOP_K]
    gw_ref, x1_ref, xm_ref, g2_ref, wsgu_ref, wsd_ref, lg_ref, lb_ref, out_ref = refs[TOP_K:]
    gu = jnp.dot(_unpack_bf16(xm_ref[...]), wsgu_ref[...], preferred_element_type=F32)
    act = _silu(gu[:, :D_EXPERT]) * gu[:, D_EXPERT:]
    acc = jnp.dot(act.astype(BF16), wsd_ref[...], preferred_element_type=F32)
    gw = gw_ref[...]
    for j in range(TOP_K):
        acc = acc + gw[:, j:j + 1] * _unpack_rows(yg_refs[j][...])
    z = DEEPNORM_ALPHA * x1_ref[...] + g2_ref[0] * acc
    out_ref[...] = _layer_norm(z) * lg_ref[...] + lb_ref[...]


def _combine(yg, gw, x1, xm, gate2, w_sgu, w_sd, ln_g, ln_b, n_per_batch):
    t, d = x1.shape
    tm = TM_COMBINE
    per = n_per_batch // tm
    steps = t // tm
    tok = lambda w: pl.BlockSpec((tm, w), lambda i: (i, 0))
    full = lambda r, c: pl.BlockSpec((r, c), lambda i: (0, 0))
    slot = lambda j: pl.BlockSpec((tm, PACK_W), lambda i: (j * steps + i, 0))
    return pl.pallas_call(
        _combine_kernel,
        out_shape=jax.ShapeDtypeStruct((t, d), F32),
        grid=(steps,),
        in_specs=[slot(j) for j in range(TOP_K)]
        + [tok(TOP_K), tok(d), tok(PACK_W), pl.BlockSpec((1, 1, d), lambda i: (i // per, 0, 0)),
           full(d, 2 * D_EXPERT), full(D_EXPERT, d), full(1, d), full(1, d)],
        out_specs=tok(d),
        compiler_params=_cparams(("arbitrary",)),
        name="combine",
    )(*([yg] * TOP_K), gw, x1, xm, gate2, w_sgu, w_sd, ln_g, ln_b)


def _mixer_heads(stream, shift, scale, w_in_k, conv_w9, conv_b, gate_b_row, head_g, c0, m0, grid2d):
    b, n, d = stream.shape
    y, f_in = _inproj(stream.reshape(b * n, d), shift, scale, w_in_k, n)
    y = y.reshape(b, n, PROJ_W)
    qk = _conv(y, conv_w9, conv_b, grid2d)
    mread, c_fin, m_fin = _mlstm(qk, y, _gates(y, gate_b_row), c0, m0, head_g)
    return f_in, mread, c_fin, m_fin


def _moe(xm, x1, scores, bias_row, gate2, w_gu, w_down, w_sgu, w_sd, ln_g, ln_b, n_per_batch):
    t = xm.shape[0]
    n_assign = t * TOP_K
    n_rows = n_assign + N_EXPERTS * ROW_BLK
    idx_t, gw, rank_t, counts = _route(scores, bias_row)
    sizes = counts[:, 0].astype(jnp.int32)
    padded = (sizes + ROW_BLK - 1) // ROW_BLK * ROW_BLK
    pad_end = jnp.cumsum(padded)
    pad_start = (pad_end - padded).astype(jnp.int32)
    dest_t = _dest(pad_start, idx_t, rank_t)
    dest3 = dest_t.reshape(TOP_K, t // SC_ROWS, SC_ROWS).transpose(1, 0, 2)

    xs = _sc_dispatch(xm, dest3, n_rows)
    yb = _experts(pad_start // ROW_BLK, (padded // ROW_BLK).astype(jnp.int32),
                  (pad_end[-1:] // ROW_BLK).astype(jnp.int32), xs, w_gu, w_down)
    yg = _sc_gather(yb, dest_t.reshape(n_assign // SC_GATHER_ROWS, SC_GATHER_ROWS))
    return _combine(yg, gw, x1, xm, gate2, w_sgu, w_sd, ln_g, ln_b, n_per_batch)


def kernel(x, c, ctx, c_ctx, w_ada, b_ada, w_in, conv_w, conv_b, gate_b, head_g, w_out, ln1_g, ln1_b,
           w_router, router_bias, w_expert_gu, w_expert_down, w_shared_gu, w_shared_down, ln2_g, ln2_b):
    bsz, n, d = x.shape
    depth = w_ada.shape[0]
    assert depth == 1
    t = bsz * n
    tables = _dft_tables()
    for l in range(depth):
        cvec = jnp.zeros((SUBLANES, d), F32).at[:bsz].set(c).at[bsz].set(c_ctx)
        mod = _ada(cvec, w_ada[l], b_ada[l][None, :])
        mods = mod.reshape(SUBLANES, 6, 1, d)
        mod_x = [mods[:bsz, j] for j in range(6)]
        mod_c = [mods[bsz:bsz + 1, j] for j in range(2)]

        wl = w_in[l]
        g0 = 4 * D_MLSTM
        w_in_k = jnp.concatenate(
            [wl[:, :g0 + N_GATE_COLS], jnp.zeros((d, LANES - N_GATE_COLS), F32),
             wl[:, g0 + N_GATE_COLS:]], axis=1).astype(BF16)
        conv_w9 = conv_w[l].reshape(9, 2 * D_MLSTM)
        conv_b_row = conv_b[l][None, :]
        gate_b_row = jnp.zeros((1, LANES), F32).at[0, :N_GATE_COLS].set(gate_b[l].reshape(-1))
        head_g_row = head_g[l][None, :]

        c0 = jnp.zeros((bsz, N_HEADS, 2, STATE_ROWS, D_HEAD), F32)
        m0 = jnp.zeros((bsz, N_HEADS, 2, SUBLANES, LANES), F32)
        _, _, c_ctx_fin, m_ctx_fin = _mixer_heads(ctx, mod_c[0], mod_c[1], w_in_k, conv_w9, conv_b_row,
                                                  gate_b_row, head_g_row, c0, m0, grid2d=False)
        f_in, mread, _, _ = _mixer_heads(x, mod_x[0], mod_x[1], w_in_k, conv_w9, conv_b_row,
                                      gate_b_row, head_g_row, c_ctx_fin, m_ctx_fin, grid2d=True)

        four = _fourier(f_in.reshape(bsz, DFT_R, DFT_R, D_FOURIER), tables).reshape(t, D_FOURIER)

        w_o = w_out[l].astype(BF16)
        w_r = jnp.zeros((d, LANES), F32).at[:, :N_EXPERTS].set(w_router[l]).astype(BF16)
        x2d = x.reshape(t, d)
        x1, xm, scores = _outproj(x2d, mread.reshape(t, D_MLSTM), four, w_o[:D_MLSTM], w_o[D_MLSTM:],
                                  mod_x[2], ln1_g[l][None, :], ln1_b[l][None, :], mod_x[3], mod_x[4],
                                  w_r, n)

        bias_rep = jnp.broadcast_to(router_bias[l][:, None], (N_EXPERTS, LANES))
        x = _moe(xm, x1, scores, bias_rep, mod_x[5], w_expert_gu[l], w_expert_down[l],
                 w_shared_gu[l].astype(BF16), w_shared_down[l].astype(BF16),
                 ln2_g[l][None, :], ln2_b[l][None, :], n).reshape(bsz, n, d)
    return x
```

```python
import functools

import numpy as np
import jax
import jax.numpy as jnp
from jax import lax
from jax.experimental import pallas as pl
from jax.experimental.pallas import tpu as pltpu
from jax.experimental.pallas import tpu_sc as plsc

F32 = jnp.float32
BF16 = jnp.bfloat16
HIGHEST = lax.Precision.HIGHEST

D_MODEL = 1024
GRID_W = 64
N_HEADS = 4
D_HEAD = 128
D_MLSTM = N_HEADS * D_HEAD
D_FOURIER = 512
FOURIER_GROUP = 64
N_GATE_COLS = 16
CHUNK = 128
N_EXPERTS = 64
TOP_K = 8
D_EXPERT = 256
ROUTED_SCALE = 2.5
DEEPNORM_ALPHA = 2.0 ** 0.25
LN_EPS = 1e-5

LANES = 128
SUBLANES = 8
VMEM_LIMIT = 48 * 1024 * 1024
FOUR_VMEM_LIMIT = 56 * 1024 * 1024

COL_QK = 0
COL_V = 2 * D_MLSTM
COL_O = 3 * D_MLSTM
COL_G = 4 * D_MLSTM
PROJ_W = COL_G + LANES

TN_ADA = 2048
TM = 1024
TM_ROUTE = 1024
ROW_BLK = 256
EXP_GROUP = 8
EXP_AHEAD = 10
EXP_IN_SLOTS = 32
EXP_OUT_SLOTS = 2 * EXP_GROUP
PACK_W = D_MODEL // 2
SC_ROWS = 128
SC_GATHER_ROWS = 64
CONV_PAD = 72
CONV_RB = 256
DFT_R = 64


def _cparams(sem, vmem=VMEM_LIMIT):
    return pltpu.CompilerParams(dimension_semantics=sem, vmem_limit_bytes=vmem)


def _layer_norm(x):
    mu = jnp.mean(x, axis=-1, keepdims=True)
    xc = x - mu
    var = jnp.mean(xc * xc, axis=-1, keepdims=True)
    return xc * lax.rsqrt(var + LN_EPS)


def _silu(x):
    return x * jax.nn.sigmoid(x)


def _pack_pair(lo, hi):
    as_bits = lambda u: lax.bitcast_convert_type(u.astype(BF16).astype(F32), jnp.uint32)
    word = as_bits(hi) | lax.shift_right_logical(as_bits(lo), jnp.uint32(16))
    return lax.bitcast_convert_type(word, jnp.int32)


def _unpack_pair_bf16(w):
    lo = lax.bitcast_convert_type(w.astype(jnp.int16), BF16)
    hi = lax.bitcast_convert_type(lax.shift_right_logical(w, jnp.int32(16)).astype(jnp.int16), BF16)
    return lo, hi


def _pack_rows(v):
    return _pack_pair(v[:, :PACK_W], v[:, PACK_W:])


def _unpack_bf16(w):
    return jnp.concatenate(_unpack_pair_bf16(w), axis=-1)


def _unpack_rows(w):
    u = lax.bitcast_convert_type(w, jnp.uint32)
    lo = lax.bitcast_convert_type(lax.shift_left(u, jnp.uint32(16)), F32)
    hi = lax.bitcast_convert_type(u & jnp.uint32(0xFFFF0000), F32)
    return jnp.concatenate([lo, hi], axis=1)


def _ada_kernel(c_ref, w_ref, b_ref, o_ref):
    s = _silu(c_ref[...]).astype(BF16)
    o_ref[...] = jnp.dot(s, w_ref[...].astype(BF16), preferred_element_type=F32) + b_ref[...]


def _ada(cvec, w_ada, b_ada):
    rows, d = cvec.shape
    n_out = w_ada.shape[1]
    tn = TN_ADA
    return pl.pallas_call(
        _ada_kernel,
        out_shape=jax.ShapeDtypeStruct((rows, n_out), F32),
        grid=(n_out // tn,),
        in_specs=[pl.BlockSpec((rows, d), lambda j: (0, 0)),
                  pl.BlockSpec((d, tn), lambda j: (0, j)),
                  pl.BlockSpec((1, tn), lambda j: (0, j))],
        out_specs=pl.BlockSpec((rows, tn), lambda j: (0, j)),
        compiler_params=_cparams(("arbitrary",)),
        name="ada",
    )(cvec, w_ada, b_ada)


INPROJ_SLOTS = 3


def _inproj_kernel(x_hbm, sh_ref, sc_ref, w_ref, y_ref, f_ref, xbuf, sem):
    i = pl.program_id(0)
    n = pl.num_programs(0)
    tm = xbuf.shape[1]

    def fetch(s):
        slot = s % INPROJ_SLOTS
        rows = pl.ds(pl.multiple_of(s * tm, tm), tm)
        return pltpu.make_async_copy(x_hbm.at[rows, :], xbuf.at[slot], sem.at[slot])

    @pl.when(i == 0)
    def _():
        for s in range(INPROJ_SLOTS - 1):
            @pl.when(s < n)
            def _():
                fetch(s).start()

    @pl.when(i + INPROJ_SLOTS - 1 < n)
    def _():
        fetch(i + INPROJ_SLOTS - 1).start()

    fetch(i).wait()
    x_ref = xbuf.at[i % INPROJ_SLOTS]
    half = tm // 2
    for h in range(2):
        r = slice(h * half, (h + 1) * half)
        u = _layer_norm(x_ref[r, :]) * (1.0 + sc_ref[0]) + sh_ref[0]
        proj = jnp.dot(u.astype(BF16), w_ref[...], preferred_element_type=F32)
        y_ref[r, :] = proj[:, :PROJ_W]
        f_ref[r, :] = proj[:, PROJ_W:]


def _inproj(x2d, shift, scale, w, n_per_batch):
    t, d = x2d.shape
    if shift.shape[0] == 1:
        tm = min(TM, t)
        per = t // tm
    else:
        tm = min(TM, n_per_batch)
        per = n_per_batch // tm
    return pl.pallas_call(
        _inproj_kernel,
        out_shape=(jax.ShapeDtypeStruct((t, PROJ_W), F32), jax.ShapeDtypeStruct((t, D_FOURIER), F32)),
        grid=(t // tm,),
        in_specs=[pl.BlockSpec(memory_space=pl.ANY),
                  pl.BlockSpec((1, 1, d), lambda i: (i // per, 0, 0)),
                  pl.BlockSpec((1, 1, d), lambda i: (i // per, 0, 0)),
                  pl.BlockSpec((d, PROJ_W + D_FOURIER), lambda i: (0, 0))],
        out_specs=(pl.BlockSpec((tm, PROJ_W), lambda i: (i, 0)),
                   pl.BlockSpec((tm, D_FOURIER), lambda i: (i, 0))),
        scratch_shapes=[pltpu.VMEM((INPROJ_SLOTS, tm, d), F32), pltpu.SemaphoreType.DMA((INPROJ_SLOTS,))],
        compiler_params=_cparams(("arbitrary",)),
        name="inproj",
    )(x2d, shift, scale, w)


def _conv_kernel(y_ref, w_ref, b_ref, o_ref, pad_ref, *, n, grid2d, tiles):
    zeros = jnp.zeros((CONV_PAD, LANES), F32)
    pad_ref[0:CONV_PAD, :] = zeros
    pad_ref[CONV_PAD + n:CONV_PAD + n + CONV_PAD, :] = zeros
    rb = min(CONV_RB, n)
    col = lax.broadcasted_iota(jnp.int32, (rb, LANES), 0) % GRID_W
    not_first = col >= 1
    not_last = col <= GRID_W - 2
    row_taps = (0, 1, 2) if grid2d else (1,)
    for ct in range(tiles):
        lanes = slice(ct * LANES, (ct + 1) * LANES)
        pad_ref[CONV_PAD:CONV_PAD + n, :] = y_ref[0, :, lanes]
        k_scale = jnp.where(pl.program_id(1) * tiles + ct >= N_HEADS, D_HEAD ** -0.5, 1.0).astype(F32)
        bias = b_ref[:, lanes]
        for blk in range(n // rb):
            r0 = CONV_PAD + blk * rb
            acc = jnp.zeros((rb, LANES), F32)
            for dj in range(3):
                part = jnp.zeros((rb, LANES), F32)
                for di in row_taps:
                    off = (di - 1) * GRID_W + (dj - 1)
                    tap = w_ref[di * 3 + dj:di * 3 + dj + 1, lanes]
                    part = part + tap * pad_ref[r0 + off:r0 + off + rb, :]
                if grid2d and dj == 0:
                    part = jnp.where(not_first, part, 0.0)
                if grid2d and dj == 2:
                    part = jnp.where(not_last, part, 0.0)
                acc = acc + part
            val = _silu(acc + bias) * k_scale
            o_ref[0, blk * rb:(blk + 1) * rb, lanes] = val.astype(o_ref.dtype)


def _conv(y3d, conv_w9, conv_b, grid2d):
    b, n, _ = y3d.shape
    n_ct = 2 * D_MLSTM // LANES
    tiles = 1 if grid2d else n_ct
    w = tiles * LANES
    return pl.pallas_call(
        functools.partial(_conv_kernel, n=n, grid2d=grid2d, tiles=tiles),
        out_shape=jax.ShapeDtypeStruct((b, n, 2 * D_MLSTM), BF16),
        grid=(b, n_ct // tiles),
        in_specs=[pl.BlockSpec((1, n, w), lambda i, c: (i, 0, c)),
                  pl.BlockSpec((9, w), lambda i, c: (0, c)),
                  pl.BlockSpec((1, w), lambda i, c: (0, c))],
        out_specs=pl.BlockSpec((1, n, w), lambda i, c: (i, 0, c)),
        scratch_shapes=[pltpu.VMEM((n + 2 * CONV_PAD, LANES), F32)],
        compiler_params=_cparams(("arbitrary", "arbitrary")),
        name="conv2d" if grid2d else "conv1d",
    )(y3d, conv_w9, conv_b)


STATE_ROWS = D_HEAD + 32
GATE_A, GATE_CM, GATE_B = 0, 1, 2
GATE_GROUP = 32


def _gates_kernel(g_ref, gb_ref, rv_ref, *, nc):
    u_i = lax.broadcasted_iota(jnp.int32, (CHUNK, CHUNK), 0)
    t_i = lax.broadcasted_iota(jnp.int32, (CHUNK, CHUNK), 1)
    prefix = (u_i <= t_i).astype(F32)
    suffix = (u_i >= t_i).astype(F32)
    grp = min(GATE_GROUP, nc)
    rows = grp * SUBLANES
    lane = lax.broadcasted_iota(jnp.int32, (rows, LANES), 1)
    is_fwd = lax.broadcasted_iota(jnp.int32, (rows, LANES), 0) % SUBLANES < N_HEADS
    h = N_HEADS

    def running_max(a):
        fwd, bwd = a, a
        sh = 1
        while sh < CHUNK:
            fwd = jnp.where(lane >= sh, jnp.maximum(fwd, pltpu.roll(fwd, sh, axis=1)), fwd)
            bwd = jnp.where(lane < CHUNK - sh, jnp.maximum(bwd, pltpu.roll(bwd, CHUNK - sh, axis=1)), bwd)
            sh *= 2
        return jnp.where(is_fwd, fwd, bwd)

    def body(i, carry):
        c0 = i * grp
        r0 = pl.multiple_of(c0 * CHUNK, CHUNK)
        pre = g_ref[0, pl.ds(r0, grp * CHUNK), :] + gb_ref[...]
        pre_t = jnp.concatenate(
            [pre[j * CHUNK:(j + 1) * CHUNK].T[:N_GATE_COLS, :] for j in range(grp)], axis=0)
        lf_t = -(jnp.maximum(-pre_t, 0.0) + jnp.log1p(jnp.exp(-jnp.abs(pre_t))))
        cum_f = jnp.dot(lf_t, prefix, precision=HIGHEST, preferred_element_type=F32)
        cum_b = jnp.dot(lf_t, suffix, precision=HIGHEST, preferred_element_type=F32)
        pick = lambda t, j, q: t[j * N_GATE_COLS + q * h:j * N_GATE_COLS + (q + 1) * h]
        b8 = jnp.concatenate([x for j in range(grp) for x in (pick(cum_f, j, 1), pick(cum_b, j, 3))], axis=0)
        li8 = jnp.concatenate([x for j in range(grp) for x in (pick(pre_t, j, 0), pick(pre_t, j, 2))], axis=0)
        a8 = li8 - b8
        as_chunks = lambda t: t.reshape(grp, SUBLANES, LANES)
        rv_ref[0, pl.ds(c0, grp), GATE_A] = as_chunks(a8)
        rv_ref[0, pl.ds(c0, grp), GATE_CM] = as_chunks(running_max(a8))
        rv_ref[0, pl.ds(c0, grp), GATE_B] = as_chunks(b8)
        return carry

    lax.fori_loop(0, nc // grp, body, 0)


def _gates(y3d, gate_b_row):
    b, n, _ = y3d.shape
    nc = n // CHUNK
    return pl.pallas_call(
        functools.partial(_gates_kernel, nc=nc),
        out_shape=jax.ShapeDtypeStruct((b, nc, 3, SUBLANES, LANES), F32),
        grid=(b,),
        in_specs=[pl.BlockSpec((1, n, LANES), lambda i: (i, 0, COL_G // LANES)),
                  pl.BlockSpec((1, LANES), lambda i: (0, 0))],
        out_specs=pl.BlockSpec((1, nc, 3, SUBLANES, LANES), lambda i: (i, 0, 0, 0, 0)),
        compiler_params=_cparams(("arbitrary",)),
        name="gates",
    )(y3d, gate_b_row)


def _mlstm_kernel(q_ref, k_ref, v_ref, o_ref, rv_ref, c0_ref, m0_ref, hg_ref,
                  out_ref, cfin_ref, mfin_ref, hf_ref, hb_ref, cst_ref, mst_ref, *, nc, heads):
    cst_ref[...] = c0_ref[0]
    mst_ref[...] = m0_ref[0]
    s_i = lax.broadcasted_iota(jnp.int32, (CHUNK, CHUNK), 0)
    t_i = lax.broadcasted_iota(jnp.int32, (CHUNK, CHUNK), 1)
    ones_row = (s_i[:STATE_ROWS - D_HEAD] == 0).astype(F32)

    def one_chunk(c, d, mask_t, last, h_ref, hh):
        r0 = pl.multiple_of(c * CHUNK, CHUNK)
        lanes = slice(hh * D_HEAD, (hh + 1) * D_HEAD)
        g = d * N_HEADS + pl.program_id(1) * heads + hh
        gate_row = lambda tbl: rv_ref[0, pl.ds(c, 1), tbl, pl.ds(g, 1), :].reshape(1, LANES)
        a_row, cm_row, b_row = gate_row(GATE_A), gate_row(GATE_CM), gate_row(GATE_B)
        q_t = q_ref[0, pl.ds(r0, CHUNK), lanes].T
        k = k_ref[0, pl.ds(r0, CHUNK), lanes]
        v_t_aug = jnp.concatenate([v_ref[0, pl.ds(r0, CHUNK), lanes].T, ones_row], axis=0)
        m_prev = mst_ref[hh, d, 0:1, :]
        m_row = jnp.maximum(m_prev, cm_row)
        a_bc = jnp.broadcast_to(a_row, (CHUNK, CHUNK)).T
        d_t = jnp.where(mask_t, jnp.exp(a_bc - m_row), 0.0)
        s_t = jnp.dot(k, q_t, preferred_element_type=F32) * d_t
        cs = cst_ref[hh, d]
        rhs = jnp.concatenate(
            [s_t.astype(BF16), (q_t.astype(F32) * jnp.exp(m_prev - m_row)).astype(BF16)], axis=0)
        lhs = jnp.concatenate([v_t_aug.astype(BF16), cs.astype(BF16)], axis=1)
        h_t = jnp.dot(lhs, rhs, preferred_element_type=F32)
        den = h_t[D_HEAD:D_HEAD + 1, :]
        floor = jnp.exp(-b_row - m_row)
        h_ref[hh, pl.ds(c, 1)] = (h_t[:D_HEAD] / jnp.maximum(jnp.abs(den), floor))[None]
        m_last = m_row[:, last:last + 1]
        wv_t = (v_t_aug * jnp.exp(a_row - m_last)).astype(BF16)
        cst_ref[hh, d] = jnp.exp(m_prev - m_last) * cs + jnp.dot(wv_t, k, preferred_element_type=F32)
        mst_ref[hh, d] = jnp.broadcast_to(b_row[:, last:last + 1] + m_last, (SUBLANES, LANES))

    def body(c, carry):
        for hh in range(heads):
            one_chunk(c, 0, s_i <= t_i, CHUNK - 1, hf_ref, hh)
            one_chunk(nc - 1 - c, 1, s_i >= t_i, 0, hb_ref, hh)
        return carry

    lax.fori_loop(0, nc, body, 0, unroll=min(16 // heads, nc))
    cfin_ref[0] = cst_ref[...]
    mfin_ref[0] = mst_ref[...]

    def readout(c, carry):
        r0 = pl.multiple_of(c * CHUNK, CHUNK)
        for hh in range(heads):
            lanes = slice(hh * D_HEAD, (hh + 1) * D_HEAD)
            h_t = hf_ref[hh, pl.ds(c, 1)][0] + hb_ref[hh, pl.ds(c, 1)][0]
            hc = h_t - jnp.mean(h_t, axis=0, keepdims=True)
            hn = (hc * lax.rsqrt(jnp.mean(hc * hc, axis=0, keepdims=True) + LN_EPS)).T
            gate = jax.nn.sigmoid(o_ref[0, pl.ds(r0, CHUNK), lanes])
            out_ref[0, pl.ds(r0, CHUNK), lanes] = (hn * hg_ref[:, lanes] * gate).astype(out_ref.dtype)
        return carry

    lax.fori_loop(0, nc, readout, 0, unroll=min(16, nc))


def _mlstm(qk, y3d, rowv, c0, m0, head_g):
    b, n, _ = qk.shape
    nc = n // CHUNK
    heads = N_HEADS if nc <= 4 else 1
    w = heads * D_HEAD
    v_blk = COL_V // w
    o_blk = COL_O // w
    tok = lambda off: pl.BlockSpec((1, n, w), lambda i, h: (i, 0, off + h))
    state_c = pl.BlockSpec((1, heads, 2, STATE_ROWS, D_HEAD), lambda i, h: (i, h, 0, 0, 0))
    state_m = pl.BlockSpec((1, heads, 2, SUBLANES, LANES), lambda i, h: (i, h, 0, 0, 0))
    return pl.pallas_call(
        functools.partial(_mlstm_kernel, nc=nc, heads=heads),
        out_shape=(jax.ShapeDtypeStruct((b, n, D_MLSTM), BF16),
                   jax.ShapeDtypeStruct((b, N_HEADS, 2, STATE_ROWS, D_HEAD), F32),
                   jax.ShapeDtypeStruct((b, N_HEADS, 2, SUBLANES, LANES), F32)),
        grid=(b, N_HEADS // heads),
        in_specs=[tok(0), tok(D_MLSTM // w), tok(v_blk), tok(o_blk),
                  pl.BlockSpec((1, nc, 3, SUBLANES, LANES), lambda i, h: (i, 0, 0, 0, 0)),
                  state_c, state_m,
                  pl.BlockSpec((1, w), lambda i, h: (0, h))],
        out_specs=(pl.BlockSpec((1, n, w), lambda i, h: (i, 0, h)), state_c, state_m),
        scratch_shapes=[pltpu.VMEM((heads, nc, D_HEAD, CHUNK), F32), pltpu.VMEM((heads, nc, D_HEAD, CHUNK), F32),
                        pltpu.VMEM((heads, 2, STATE_ROWS, D_HEAD), F32),
                        pltpu.VMEM((heads, 2, SUBLANES, LANES), F32)],
        compiler_params=_cparams(("arbitrary", "arbitrary")),
        name="mlstm",
    )(qk, qk, y3d, y3d, rowv, c0, m0, head_g)


def _dft_tables():
    r = DFT_R
    idx = np.arange(r)
    cg = np.cos(2 * np.pi * np.outer(idx, idx) / FOURIER_GROUP)
    sg = np.sin(2 * np.pi * np.outer(idx, idx) / FOURIER_GROUP)
    n_grp = D_FOURIER // FOURIER_GROUP
    bdc = np.kron(np.eye(n_grp), cg) / 8.0
    bds = np.kron(np.eye(n_grp), sg) / 8.0
    k1 = idx[:, None]
    a = idx[None, :]
    m1 = np.zeros((r, 2 * r, 2 * r))
    for b2 in range(r):
        th = 2 * np.pi * k1 * (r * a + b2) / (r * r)
        ec, es = np.cos(th), np.sin(th)
        m1[b2] = np.block([[ec, -es], [es, ec]]) / 8.0
    th2 = 2 * np.pi * np.outer(idx, idx) / r
    m2 = np.concatenate([np.cos(th2), -np.sin(th2)], axis=1) / 8.0
    as_bf16 = lambda t: jnp.asarray(t, dtype=F32).astype(BF16)
    return as_bf16(bdc), as_bf16(bds), as_bf16(m1), as_bf16(m2)


FOUR_BB = 8


N_SLABS = D_FOURIER // LANES


def _to_slabs(slab_ref, val):
    for s in range(N_SLABS):
        slab_ref[s] = val[:, s * LANES:(s + 1) * LANES]


def _strided_rows(slab_ref, start):
    return jnp.concatenate(
        [slab_ref[s, pl.ds(start, DFT_R, stride=FOUR_BB), :] for s in range(N_SLABS)], axis=1)


def _four_kernel(f_ref, bdc_ref, bds_ref, m1_ref, m2_ref, o_ref, z_ref, ps_ref, qs_ref, zs_ref, os_ref):
    r = DFT_R

    def stage1(blk, carry):
        b0 = pl.multiple_of(blk * FOUR_BB, FOUR_BB)
        fb = f_ref[0, :, pl.ds(b0, FOUR_BB), :].reshape(r * FOUR_BB, D_FOURIER).astype(BF16)
        _to_slabs(ps_ref, jnp.dot(fb, bdc_ref[...], preferred_element_type=F32))
        _to_slabs(qs_ref, jnp.dot(fb, bds_ref[...], preferred_element_type=F32))
        for j in range(FOUR_BB):
            x2 = jnp.concatenate([_strided_rows(ps_ref, j), _strided_rows(qs_ref, j)], axis=0).astype(BF16)
            zz = jnp.dot(m1_ref[b0 + j], x2, preferred_element_type=F32)
            z_ref[b0 + j] = _pack_pair(zz[:r], zz[r:])
        return carry

    lax.fori_loop(0, r // FOUR_BB, stage1, 0)

    def stage2(blk, carry):
        k0 = pl.multiple_of(blk * FOUR_BB, FOUR_BB)
        _to_slabs(zs_ref, z_ref[:, pl.ds(k0, FOUR_BB), :].reshape(r * FOUR_BB, D_FOURIER))
        for j in range(FOUR_BB):
            zc, zs = _unpack_pair_bf16(_strided_rows(zs_ref, j))
            o = jnp.dot(m2_ref[...], jnp.concatenate([zc, zs], axis=0), preferred_element_type=F32)
            for s in range(N_SLABS):
                os_ref[s, pl.ds(j, r, stride=FOUR_BB), :] = o[:, s * LANES:(s + 1) * LANES]
        for s in range(N_SLABS):
            o_ref[0, :, pl.ds(k0, FOUR_BB), s * LANES:(s + 1) * LANES] = os_ref[s].reshape(r, FOUR_BB, LANES)
        return carry

    lax.fori_loop(0, r // FOUR_BB, stage2, 0)


def _fourier(y4, tables):
    bdc, bds, m1, m2 = tables
    b = y4.shape[0]
    r = DFT_R
    slab = lambda dt: pltpu.VMEM((N_SLABS, r * FOUR_BB, LANES), dt)
    const = lambda *shape: pl.BlockSpec(shape, lambda i: (0,) * len(shape))
    return pl.pallas_call(
        _four_kernel,
        out_shape=jax.ShapeDtypeStruct((b, r, r, D_FOURIER), F32),
        grid=(b,),
        in_specs=[pl.BlockSpec((1, r, r, D_FOURIER), lambda i: (i, 0, 0, 0)),
                  const(D_FOURIER, D_FOURIER), const(D_FOURIER, D_FOURIER),
                  const(r, 2 * r, 2 * r), const(r, 2 * r)],
        out_specs=pl.BlockSpec((1, r, r, D_FOURIER), lambda i: (i, 0, 0, 0)),
        scratch_shapes=[pltpu.VMEM((r, r, D_FOURIER), jnp.int32), slab(F32), slab(F32), slab(jnp.int32), slab(F32)],
        compiler_params=_cparams(("arbitrary",), vmem=FOUR_VMEM_LIMIT),
        name="fourier",
    )(y4, bdc, bds, m1, m2)


OUTPROJ_PARTS = 4


def _outproj_kernel(x_ref, mr_ref, fo_ref, wm_ref, wf_ref, g1_ref, lg_ref, lb_ref,
                    sh_ref, sc_ref, wr_ref, x1_ref, xm_ref, s_ref):
    part = x_ref.shape[0] // OUTPROJ_PARTS
    rows = lambda h: slice(h * part, (h + 1) * part)
    mixer = lambda r: (jnp.dot(mr_ref[r, :], wm_ref[...], preferred_element_type=F32)
                       + jnp.dot(fo_ref[r, :].astype(BF16), wf_ref[...], preferred_element_type=F32))
    mix_next = mixer(rows(0))
    for h in range(OUTPROJ_PARTS):
        r = rows(h)
        mix = mix_next
        if h + 1 < OUTPROJ_PARTS:
            mix_next = mixer(rows(h + 1))
        x1 = _layer_norm(DEEPNORM_ALPHA * x_ref[r, :] + g1_ref[0] * mix) * lg_ref[...] + lb_ref[...]
        x1_ref[r, :] = x1
        xm = (_layer_norm(x1) * (1.0 + sc_ref[0]) + sh_ref[0]).astype(BF16)
        xm_ref[r, :] = _pack_rows(xm)
        logits = jnp.dot(xm, wr_ref[...], preferred_element_type=F32)
        s_ref[r, :] = jax.nn.sigmoid(logits)


def _outproj(x2d, mread, four, w_m, w_f, gate1, ln_g, ln_b, shift2, scale2, w_router, n_per_batch):
    t, d = x2d.shape
    per = n_per_batch // TM
    tokd = lambda w: pl.BlockSpec((TM, w), lambda i: (i, 0))
    full = lambda r, c: pl.BlockSpec((r, c), lambda i: (0, 0))
    mod = pl.BlockSpec((1, 1, d), lambda i: (i // per, 0, 0))
    return pl.pallas_call(
        _outproj_kernel,
        out_shape=(jax.ShapeDtypeStruct((t, d), F32),
                   jax.ShapeDtypeStruct((t, PACK_W), jnp.int32),
                   jax.ShapeDtypeStruct((t, LANES), F32)),
        grid=(t // TM,),
        in_specs=[tokd(d), tokd(D_MLSTM), tokd(D_FOURIER), full(D_MLSTM, d), full(D_FOURIER, d),
                  mod, full(1, d), full(1, d), mod, mod, full(d, LANES)],
        out_specs=(tokd(d), tokd(PACK_W), tokd(LANES)),
        compiler_params=_cparams(("arbitrary",)),
        name="outproj",
    )(x2d, mread, four, w_m, w_f, gate1, ln_g, ln_b, shift2, scale2, w_router)


def _route_kernel(s_ref, b_ref, idx_ref, gw_ref, rank_ref, cnt_ref, carry_ref):
    @pl.when(pl.program_id(0) == 0)
    def _():
        carry_ref[...] = jnp.zeros_like(carry_ref)

    tm = s_ref.shape[0]
    reps = tm // LANES
    s_t = s_ref[...].T[:N_EXPERTS, :]
    e_i = lax.broadcasted_iota(jnp.int32, (N_EXPERTS, tm), 0).astype(F32)
    neg_inf = jnp.float32(-jnp.inf)
    sb = s_t + jnp.tile(b_ref[...], (1, reps))
    picks, sels = [], []
    taken = jnp.zeros((N_EXPERTS, tm), F32)
    for _ in range(TOP_K):
        best = jnp.max(sb, axis=0, keepdims=True)
        pick = jnp.min(jnp.where(sb == best, e_i, float(N_EXPERTS)), axis=0, keepdims=True)
        onehot = e_i == pick
        sels.append(jnp.sum(jnp.where(onehot, s_t, 0.0), axis=0, keepdims=True))
        picks.append(pick)
        sb = jnp.where(onehot, neg_inf, sb)
        taken = taken + onehot.astype(F32)
    u_i = lax.broadcasted_iota(jnp.int32, (tm, tm), 0)
    t_i = lax.broadcasted_iota(jnp.int32, (tm, tm), 1)
    before = (u_i < t_i).astype(BF16)
    taken_bf = taken.astype(BF16)
    rank_all = jnp.tile(carry_ref[...], (1, reps)) + jnp.dot(taken_bf, before, preferred_element_type=F32)
    carry_ref[...] = carry_ref[...] + jnp.dot(taken_bf, jnp.ones((tm, LANES), BF16),
                                              preferred_element_type=F32)
    cnt_ref[...] = carry_ref[...]
    total = sels[0]
    for v in sels[1:]:
        total = total + v
    ranks = [jnp.sum(jnp.where(e_i == picks[j], rank_all, 0.0), axis=0, keepdims=True) for j in range(TOP_K)]
    idx_ref[...] = jnp.concatenate(picks, axis=0).astype(jnp.int32)
    rank_ref[...] = jnp.concatenate(ranks, axis=0).astype(jnp.int32)
    gw_t = jnp.concatenate([ROUTED_SCALE * v / total for v in sels]
                           + [jnp.zeros((LANES - TOP_K, tm), F32)], axis=0)
    gw_ref[...] = gw_t.T[:, :TOP_K]


def _route(scores, bias_rep):
    t = scores.shape[0]
    tok = lambda w: pl.BlockSpec((TM_ROUTE, w), lambda i: (i, 0))
    slot_major = pl.BlockSpec((TOP_K, TM_ROUTE), lambda i: (0, i))
    per_expert = pl.BlockSpec((N_EXPERTS, LANES), lambda i: (0, 0))
    return pl.pallas_call(
        _route_kernel,
        out_shape=(jax.ShapeDtypeStruct((TOP_K, t), jnp.int32),
                   jax.ShapeDtypeStruct((t, TOP_K), F32),
                   jax.ShapeDtypeStruct((TOP_K, t), jnp.int32),
                   jax.ShapeDtypeStruct((N_EXPERTS, LANES), F32)),
        grid=(t // TM_ROUTE,),
        in_specs=[tok(LANES), per_expert],
        out_specs=(slot_major, tok(TOP_K), slot_major, per_expert),
        scratch_shapes=[pltpu.VMEM((N_EXPERTS, LANES), F32)],
        compiler_params=_cparams(("arbitrary",)),
        name="route",
    )(scores, bias_rep)


TN_DEST = 2048


def _dest_kernel(ps_ref, idx_ref, rank_ref, dest_ref):
    idx = idx_ref[...]
    start = jnp.zeros(idx.shape, jnp.int32)
    for e in range(N_EXPERTS):
        start = jnp.where(idx == e, ps_ref[e], start)
    dest_ref[...] = start + rank_ref[...]


def _dest(pad_start, idx_t, rank_t):
    t = idx_t.shape[1]
    blk = lambda: pl.BlockSpec((TOP_K, TN_DEST), lambda i, ps: (0, i))
    grid_spec = pltpu.PrefetchScalarGridSpec(
        num_scalar_prefetch=1, grid=(t // TN_DEST,), in_specs=[blk(), blk()], out_specs=blk())
    return pl.pallas_call(
        _dest_kernel,
        out_shape=jax.ShapeDtypeStruct((TOP_K, t), jnp.int32),
        grid_spec=grid_spec,
        compiler_params=_cparams(("arbitrary",)),
        name="dest",
    )(pad_start, idx_t, rank_t)


def _sc_worker_id(info):
    return lax.axis_index("s") * info.num_cores + lax.axis_index("c")


def _sc_dispatch(xm, dest3, n_rows):
    t, w = xm.shape
    info = plsc.get_sparse_core_info()
    n_workers = info.num_cores * info.num_subcores
    per_worker = t // SC_ROWS // n_workers
    mesh = plsc.VectorSubcoreMesh(core_axis_name="c", subcore_axis_name="s")

    @functools.partial(
        pl.kernel, mesh=mesh,
        out_type=jax.ShapeDtypeStruct((n_rows, w), xm.dtype),
        scratch_types=[pltpu.VMEM((TOP_K, SC_ROWS), jnp.int32), pltpu.VMEM((SC_ROWS, w), xm.dtype)],
        name="sc_dispatch",
    )
    def run(xm_hbm, dest_hbm, xs_hbm, idx_v, rows_v):
        first = _sc_worker_id(info) * per_worker

        @pl.loop(0, per_worker)
        def _(ci):
            chunk = first + ci
            pltpu.sync_copy(dest_hbm.at[chunk], idx_v)
            pltpu.sync_copy(xm_hbm.at[pl.ds(chunk * SC_ROWS, SC_ROWS)], rows_v)
            for j in range(TOP_K):
                pltpu.sync_copy(rows_v, xs_hbm.at[idx_v.at[j]])

    return run(xm, dest3)


def _sc_gather(table, idx2):
    n_chunks, rows = idx2.shape
    w = table.shape[1]
    info = plsc.get_sparse_core_info()
    n_workers = info.num_cores * info.num_subcores
    per_worker = n_chunks // n_workers
    assert per_worker % 2 == 0
    mesh = plsc.VectorSubcoreMesh(core_axis_name="c", subcore_axis_name="s")

    @functools.partial(
        pl.kernel, mesh=mesh,
        out_type=jax.ShapeDtypeStruct((n_chunks * rows, w), table.dtype),
        scratch_types=[pltpu.VMEM((2, rows), jnp.int32), pltpu.VMEM((2, rows, w), table.dtype),
                       pltpu.SemaphoreType.DMA, pltpu.SemaphoreType.DMA],
        name="sc_gather",
    )
    def run(table_hbm, idx_hbm, out_hbm, idx_v, rows_v, sem0, sem1):
        first = _sc_worker_id(info) * per_worker
        sems = (sem0, sem1)

        def gather(b):
            return pltpu.make_async_copy(table_hbm.at[idx_v.at[b]], rows_v.at[b], sems[b])

        def start(chunk, b):
            pltpu.sync_copy(idx_hbm.at[chunk], idx_v.at[b])
            gather(b).start()

        def finish(chunk, b):
            gather(b).wait()
            pltpu.sync_copy(rows_v.at[b], out_hbm.at[pl.ds(chunk * rows, rows)])

        start(first, 0)

        @pl.loop(0, per_worker, step=2)
        def _(ci):
            chunk = first + ci
            start(chunk + 1, 1)
            finish(chunk, 0)

            @pl.when(ci + 2 < per_worker)
            def _():
                start(chunk + 2, 0)

            finish(chunk + 1, 1)

    return run(table, idx2)


def _expert_kernel(first_ref, count_ref, used_ref,
                   xs_hbm, wgu_ref, wd_ref, y_hbm, wgu_bf, wd_bf, xbuf, ybuf, in_sem, out_sem):
    e = pl.program_id(0)
    n_used = used_ref[0]
    first = first_ref[e]
    n_blocks = count_ref[e]

    def rows(g):
        return pl.ds(pl.multiple_of(g * ROW_BLK, ROW_BLK), ROW_BLK)

    def read(g):
        slot = g % EXP_IN_SLOTS
        return pltpu.make_async_copy(xs_hbm.at[rows(g), :], xbuf.at[slot], in_sem.at[slot])

    def write(g):
        slot = g % EXP_OUT_SLOTS
        return pltpu.make_async_copy(ybuf.at[slot], y_hbm.at[rows(g), :], out_sem.at[slot])

    @pl.when(e == 0)
    def _():
        for g0 in range(EXP_AHEAD):
            @pl.when(g0 < n_used)
            def _():
                read(g0).start()

    @pl.when(n_blocks > 0)
    def _():
        wgu_bf[...] = wgu_ref[0].astype(BF16)
        wd_bf[...] = wd_ref[0].astype(BF16)

    def acquire(g):
        @pl.when(g + EXP_AHEAD < n_used)
        def _():
            read(g + EXP_AHEAD).start()

        read(g).wait()

        @pl.when(g >= EXP_OUT_SLOTS)
        def _():
            write(g - EXP_OUT_SLOTS).wait()

    def gate_up(g):
        return jnp.dot(_unpack_bf16(xbuf[g % EXP_IN_SLOTS]), wgu_bf[...], preferred_element_type=F32)

    def down(g, gu):
        act = _silu(gu[:, :D_EXPERT]) * gu[:, D_EXPERT:]
        yv = jnp.dot(act.astype(BF16), wd_bf[...], preferred_element_type=F32)
        words = pltpu.pack_elementwise([yv[:, :PACK_W], yv[:, PACK_W:]], packed_dtype=BF16)
        ybuf[g % EXP_OUT_SLOTS] = lax.bitcast_convert_type(words, jnp.int32)

    def handle(g, n):
        for i in range(n):
            acquire(g + i)
        gu = gate_up(g)
        for i in range(n):
            gu_next = gate_up(g + i + 1) if i + 1 < n else None
            down(g + i, gu)
            gu = gu_next
        for i in range(n):
            write(g + i).start()

    def group(k, carry):
        handle(first + EXP_GROUP * k, EXP_GROUP)
        return carry

    lax.fori_loop(0, n_blocks // EXP_GROUP, group, 0)

    n = EXP_GROUP // 2
    while n >= 1:
        @pl.when(n_blocks % (2 * n) >= n)
        def _(n=n):
            handle(first + n_blocks - n_blocks % (2 * n), n)
        n //= 2

    @pl.when(e == pl.num_programs(0) - 1)
    def _():
        for back in range(EXP_OUT_SLOTS, 0, -1):
            @pl.when(n_used >= back)
            def _():
                write(n_used - back).wait()


def _experts(first_block, n_blocks, n_used, xs, w_gu, w_down):
    n_rows, w = xs.shape
    d = D_MODEL
    grid_spec = pltpu.PrefetchScalarGridSpec(
        num_scalar_prefetch=3,
        grid=(N_EXPERTS,),
        in_specs=[pl.BlockSpec(memory_space=pl.ANY),
                  pl.BlockSpec((1, d, 2 * D_EXPERT), lambda e, *_: (e, 0, 0)),
                  pl.BlockSpec((1, D_EXPERT, d), lambda e, *_: (e, 0, 0))],
        out_specs=pl.BlockSpec(memory_space=pl.ANY),
        scratch_shapes=[pltpu.VMEM((d, 2 * D_EXPERT), BF16), pltpu.VMEM((D_EXPERT, d), BF16),
                        pltpu.VMEM((EXP_IN_SLOTS, ROW_BLK, w), jnp.int32),
                        pltpu.VMEM((EXP_OUT_SLOTS, ROW_BLK, w), jnp.int32),
                        pltpu.SemaphoreType.DMA((EXP_IN_SLOTS,)),
                        pltpu.SemaphoreType.DMA((EXP_OUT_SLOTS,))],
    )
    return pl.pallas_call(
        _expert_kernel,
        out_shape=jax.ShapeDtypeStruct((n_rows, w), jnp.int32),
        grid_spec=grid_spec,
        compiler_params=_cparams(("arbitrary",)),
        name="experts",
    )(first_block, n_blocks, n_used, xs, w_gu, w_down)


TM_COMBINE = 512


def _combine_kernel(*refs):
    yg_refs = refs[:TOP_K]
    gw_ref, x1_ref, xm_ref, g2_ref, wsgu_ref, wsd_ref, lg_ref, lb_ref, out_ref = refs[TOP_K:]
    gu = jnp.dot(_unpack_bf16(xm_ref[...]), wsgu_ref[...], preferred_element_type=F32)
    act = _silu(gu[:, :D_EXPERT]) * gu[:, D_EXPERT:]
    acc = jnp.dot(act.astype(BF16), wsd_ref[...], preferred_element_type=F32)
    gw = gw_ref[...]
    for j in range(TOP_K):
        acc = acc + gw[:, j:j + 1] * _unpack_rows(yg_refs[j][...])
    z = DEEPNORM_ALPHA * x1_ref[...] + g2_ref[0] * acc
    out_ref[...] = _layer_norm(z) * lg_ref[...] + lb_ref[...]


def _combine(yg, gw, x1, xm, gate2, w_sgu, w_sd, ln_g, ln_b, n_per_batch):
    t, d = x1.shape
    tm = TM_COMBINE
    per = n_per_batch // tm
    steps = t // tm
    tok = lambda w: pl.BlockSpec((tm, w), lambda i: (i, 0))
    full = lambda r, c: pl.BlockSpec((r, c), lambda i: (0, 0))
    slot = lambda j: pl.BlockSpec((tm, PACK_W), lambda i: (j * steps + i, 0))
    return pl.pallas_call(
        _combine_kernel,
        out_shape=jax.ShapeDtypeStruct((t, d), F32),
        grid=(steps,),
        in_specs=[slot(j) for j in range(TOP_K)]
        + [tok(TOP_K), tok(d), tok(PACK_W), pl.BlockSpec((1, 1, d), lambda i: (i // per, 0, 0)),
           full(d, 2 * D_EXPERT), full(D_EXPERT, d), full(1, d), full(1, d)],
        out_specs=tok(d),
        compiler_params=_cparams(("arbitrary",)),
        name="combine",
    )(*([yg] * TOP_K), gw, x1, xm, gate2, w_sgu, w_sd, ln_g, ln_b)


def _mixer_heads(stream, shift, scale, w_in_k, conv_w9, conv_b, gate_b_row, head_g, c0, m0, grid2d):
    b, n, d = stream.shape
    y, f_in = _inproj(stream.reshape(b * n, d), shift, scale, w_in_k, n)
    y = y.reshape(b, n, PROJ_W)
    qk = _conv(y, conv_w9, conv_b, grid2d)
    mread, c_fin, m_fin = _mlstm(qk, y, _gates(y, gate_b_row), c0, m0, head_g)
    return f_in, mread, c_fin, m_fin


def _moe(xm, x1, scores, bias_row, gate2, w_gu, w_down, w_sgu, w_sd, ln_g, ln_b, n_per_batch):
    t = xm.shape[0]
    n_assign = t * TOP_K
    n_rows = n_assign + N_EXPERTS * ROW_BLK
    idx_t, gw, rank_t, counts = _route(scores, bias_row)
    sizes = counts[:, 0].astype(jnp.int32)
    padded = (sizes + ROW_BLK - 1) // ROW_BLK * ROW_BLK
    pad_end = jnp.cumsum(padded)
    pad_start = (pad_end - padded).astype(jnp.int32)
    dest_t = _dest(pad_start, idx_t, rank_t)
    dest3 = dest_t.reshape(TOP_K, t // SC_ROWS, SC_ROWS).transpose(1, 0, 2)

    xs = _sc_dispatch(xm, dest3, n_rows)
    yb = _experts(pad_start // ROW_BLK, (padded // ROW_BLK).astype(jnp.int32),
                  (pad_end[-1:] // ROW_BLK).astype(jnp.int32), xs, w_gu, w_down)
    yg = _sc_gather(yb, dest_t.reshape(n_assign // SC_GATHER_ROWS, SC_GATHER_ROWS))
    return _combine(yg, gw, x1, xm, gate2, w_sgu, w_sd, ln_g, ln_b, n_per_batch)


def kernel(x, c, ctx, c_ctx, w_ada, b_ada, w_in, conv_w, conv_b, gate_b, head_g, w_out, ln1_g, ln1_b,
           w_router, router_bias, w_expert_gu, w_expert_down, w_shared_gu, w_shared_down, ln2_g, ln2_b):
    bsz, n, d = x.shape
    depth = w_ada.shape[0]
    assert depth == 1
    t = bsz * n
    tables = _dft_tables()
    for l in range(depth):
        cvec = jnp.zeros((SUBLANES, d), F32).at[:bsz].set(c).at[bsz].set(c_ctx)
        mod = _ada(cvec, w_ada[l], b_ada[l][None, :])
        mods = mod.reshape(SUBLANES, 6, 1, d)
        mod_x = [mods[:bsz, j] for j in range(6)]
        mod_c = [mods[bsz:bsz + 1, j] for j in range(2)]

        wl = w_in[l]
        g0 = 4 * D_MLSTM
        w_in_k = jnp.concatenate(
            [wl[:, :g0 + N_GATE_COLS], jnp.zeros((d, LANES - N_GATE_COLS), F32),
             wl[:, g0 + N_GATE_COLS:]], axis=1).astype(BF16)
        conv_w9 = conv_w[l].reshape(9, 2 * D_MLSTM)
        conv_b_row = conv_b[l][None, :]
        gate_b_row = jnp.zeros((1, LANES), F32).at[0, :N_GATE_COLS].set(gate_b[l].reshape(-1))
        head_g_row = head_g[l][None, :]

        c0 = jnp.zeros((bsz, N_HEADS, 2, STATE_ROWS, D_HEAD), F32)
        m0 = jnp.zeros((bsz, N_HEADS, 2, SUBLANES, LANES), F32)
        _, _, c_ctx_fin, m_ctx_fin = _mixer_heads(ctx, mod_c[0], mod_c[1], w_in_k, conv_w9, conv_b_row,
                                                  gate_b_row, head_g_row, c0, m0, grid2d=False)
        f_in, mread, _, _ = _mixer_heads(x, mod_x[0], mod_x[1], w_in_k, conv_w9, conv_b_row,
                                      gate_b_row, head_g_row, c_ctx_fin, m_ctx_fin, grid2d=True)

        four = _fourier(f_in.reshape(bsz, DFT_R, DFT_R, D_FOURIER), tables).reshape(t, D_FOURIER)

        w_o = w_out[l].astype(BF16)
        w_r = jnp.zeros((d, LANES), F32).at[:, :N_EXPERTS].set(w_router[l]).astype(BF16)
        x2d = x.reshape(t, d)
        x1, xm, scores = _outproj(x2d, mread.reshape(t, D_MLSTM), four, w_o[:D_MLSTM], w_o[D_MLSTM:],
                                  mod_x[2], ln1_g[l][None, :], ln1_b[l][None, :], mod_x[3], mod_x[4],
                                  w_r, n)

        bias_rep = jnp.broadcast_to(router_bias[l][:, None], (N_EXPERTS, LANES))
        x = _moe(xm, x1, scores, bias_rep, mod_x[5], w_expert_gu[l], w_expert_down[l],
                 w_shared_gu[l].astype(BF16), w_shared_down[l].astype(BF16),
                 ln2_g[l][None, :], ln2_b[l][None, :], n).reshape(bsz, n, d)
    return x
```

```python
import functools

import numpy as np
import jax
import jax.numpy as jnp
from jax import lax
from jax.experimental import pallas as pl
from jax.experimental.pallas import tpu as pltpu
from jax.experimental.pallas import tpu_sc as plsc

F32 = jnp.float32
BF16 = jnp.bfloat16
HIGHEST = lax.Precision.HIGHEST

D_MODEL = 1024
GRID_W = 64
N_HEADS = 4
D_HEAD = 128
D_MLSTM = N_HEADS * D_HEAD
D_FOURIER = 512
FOURIER_GROUP = 64
N_GATE_COLS = 16
CHUNK = 128
N_EXPERTS = 64
TOP_K = 8
D_EXPERT = 256
ROUTED_SCALE = 2.5
DEEPNORM_ALPHA = 2.0 ** 0.25
LN_EPS = 1e-5

LANES = 128
SUBLANES = 8
VMEM_LIMIT = 48 * 1024 * 1024
FOUR_VMEM_LIMIT = 56 * 1024 * 1024

COL_QK = 0
COL_V = 2 * D_MLSTM
COL_O = 3 * D_MLSTM
COL_G = 4 * D_MLSTM
PROJ_W = COL_G + LANES

TN_ADA = 2048
TM = 1024
TM_ROUTE = 1024
ROW_BLK = 256
EXP_GROUP = 8
EXP_AHEAD = 10
EXP_IN_SLOTS = 32
EXP_OUT_SLOTS = 2 * EXP_GROUP
PACK_W = D_MODEL // 2
SC_ROWS = 128
SC_GATHER_ROWS = 64
CONV_PAD = 72
CONV_RB = 256
DFT_R = 64


def _cparams(sem, vmem=VMEM_LIMIT):
    return pltpu.CompilerParams(dimension_semantics=sem, vmem_limit_bytes=vmem)


def _layer_norm(x):
    mu = jnp.mean(x, axis=-1, keepdims=True)
    xc = x - mu
    var = jnp.mean(xc * xc, axis=-1, keepdims=True)
    return xc * lax.rsqrt(var + LN_EPS)


def _silu(x):
    return x * jax.nn.sigmoid(x)


def _pack_pair(lo, hi):
    as_bits = lambda u: lax.bitcast_convert_type(u.astype(BF16).astype(F32), jnp.uint32)
    word = as_bits(hi) | lax.shift_right_logical(as_bits(lo), jnp.uint32(16))
    return lax.bitcast_convert_type(word, jnp.int32)


def _unpack_pair_bf16(w):
    lo = lax.bitcast_convert_type(w.astype(jnp.int16), BF16)
    hi = lax.bitcast_convert_type(lax.shift_right_logical(w, jnp.int32(16)).astype(jnp.int16), BF16)
    return lo, hi


def _pack_rows(v):
    return _pack_pair(v[:, :PACK_W], v[:, PACK_W:])


def _unpack_bf16(w):
    return jnp.concatenate(_unpack_pair_bf16(w), axis=-1)


def _unpack_rows(w):
    u = lax.bitcast_convert_type(w, jnp.uint32)
    lo = lax.bitcast_convert_type(lax.shift_left(u, jnp.uint32(16)), F32)
    hi = lax.bitcast_convert_type(u & jnp.uint32(0xFFFF0000), F32)
    return jnp.concatenate([lo, hi], axis=1)


def _ada_kernel(c_ref, w_ref, b_ref, o_ref):
    s = _silu(c_ref[...]).astype(BF16)
    o_ref[...] = jnp.dot(s, w_ref[...].astype(BF16), preferred_element_type=F32) + b_ref[...]


def _ada(cvec, w_ada, b_ada):
    rows, d = cvec.shape
    n_out = w_ada.shape[1]
    tn = TN_ADA
    return pl.pallas_call(
        _ada_kernel,
        out_shape=jax.ShapeDtypeStruct((rows, n_out), F32),
        grid=(n_out // tn,),
        in_specs=[pl.BlockSpec((rows, d), lambda j: (0, 0)),
                  pl.BlockSpec((d, tn), lambda j: (0, j)),
                  pl.BlockSpec((1, tn), lambda j: (0, j))],
        out_specs=pl.BlockSpec((rows, tn), lambda j: (0, j)),
        compiler_params=_cparams(("arbitrary",)),
        name="ada",
    )(cvec, w_ada, b_ada)


def _inproj_kernel(x_ref, sh_ref, sc_ref, w_ref, y_ref, f_ref):
    half = x_ref.shape[0] // 2
    for h in range(2):
        r = slice(h * half, (h + 1) * half)
        u = _layer_norm(x_ref[r, :]) * (1.0 + sc_ref[0]) + sh_ref[0]
        proj = jnp.dot(u.astype(BF16), w_ref[...], preferred_element_type=F32)
        y_ref[r, :] = proj[:, :PROJ_W]
        f_ref[r, :] = proj[:, PROJ_W:]


def _inproj(x2d, shift, scale, w, n_per_batch):
    t, d = x2d.shape
    if shift.shape[0] == 1:
        tm = min(TM, t)
        per = t // tm
    else:
        tm = min(TM, n_per_batch)
        per = n_per_batch // tm
    return pl.pallas_call(
        _inproj_kernel,
        out_shape=(jax.ShapeDtypeStruct((t, PROJ_W), F32), jax.ShapeDtypeStruct((t, D_FOURIER), F32)),
        grid=(t // tm,),
        in_specs=[pl.BlockSpec((tm, d), lambda i: (i, 0)),
                  pl.BlockSpec((1, 1, d), lambda i: (i // per, 0, 0)),
                  pl.BlockSpec((1, 1, d), lambda i: (i // per, 0, 0)),
                  pl.BlockSpec((d, PROJ_W + D_FOURIER), lambda i: (0, 0))],
        out_specs=(pl.BlockSpec((tm, PROJ_W), lambda i: (i, 0)),
                   pl.BlockSpec((tm, D_FOURIER), lambda i: (i, 0))),
        compiler_params=_cparams(("arbitrary",)),
        name="inproj",
    )(x2d, shift, scale, w)


def _conv_kernel(y_ref, w_ref, b_ref, o_ref, pad_ref, *, n, grid2d, tiles):
    zeros = jnp.zeros((CONV_PAD, LANES), F32)
    pad_ref[0:CONV_PAD, :] = zeros
    pad_ref[CONV_PAD + n:CONV_PAD + n + CONV_PAD, :] = zeros
    rb = min(CONV_RB, n)
    col = lax.broadcasted_iota(jnp.int32, (rb, LANES), 0) % GRID_W
    not_first = col >= 1
    not_last = col <= GRID_W - 2
    row_taps = (0, 1, 2) if grid2d else (1,)
    for ct in range(tiles):
        lanes = slice(ct * LANES, (ct + 1) * LANES)
        pad_ref[CONV_PAD:CONV_PAD + n, :] = y_ref[0, :, lanes]
        k_scale = jnp.where(pl.program_id(1) * tiles + ct >= N_HEADS, D_HEAD ** -0.5, 1.0).astype(F32)
        bias = b_ref[:, lanes]
        for blk in range(n // rb):
            r0 = CONV_PAD + blk * rb
            acc = jnp.zeros((rb, LANES), F32)
            for dj in range(3):
                part = jnp.zeros((rb, LANES), F32)
                for di in row_taps:
                    off = (di - 1) * GRID_W + (dj - 1)
                    tap = w_ref[di * 3 + dj:di * 3 + dj + 1, lanes]
                    part = part + tap * pad_ref[r0 + off:r0 + off + rb, :]
                if grid2d and dj == 0:
                    part = jnp.where(not_first, part, 0.0)
                if grid2d and dj == 2:
                    part = jnp.where(not_last, part, 0.0)
                acc = acc + part
            val = _silu(acc + bias) * k_scale
            o_ref[0, blk * rb:(blk + 1) * rb, lanes] = val.astype(o_ref.dtype)


def _conv(y3d, conv_w9, conv_b, grid2d):
    b, n, _ = y3d.shape
    n_ct = 2 * D_MLSTM // LANES
    tiles = 1 if grid2d else n_ct
    w = tiles * LANES
    return pl.pallas_call(
        functools.partial(_conv_kernel, n=n, grid2d=grid2d, tiles=tiles),
        out_shape=jax.ShapeDtypeStruct((b, n, 2 * D_MLSTM), BF16),
        grid=(b, n_ct // tiles),
        in_specs=[pl.BlockSpec((1, n, w), lambda i, c: (i, 0, c)),
                  pl.BlockSpec((9, w), lambda i, c: (0, c)),
                  pl.BlockSpec((1, w), lambda i, c: (0, c))],
        out_specs=pl.BlockSpec((1, n, w), lambda i, c: (i, 0, c)),
        scratch_shapes=[pltpu.VMEM((n + 2 * CONV_PAD, LANES), F32)],
        compiler_params=_cparams(("arbitrary", "arbitrary")),
        name="conv2d" if grid2d else "conv1d",
    )(y3d, conv_w9, conv_b)


STATE_ROWS = D_HEAD + 32
GATE_A, GATE_CM, GATE_B = 0, 1, 2
GATE_GROUP = 32


def _gates_kernel(g_ref, gb_ref, rv_ref, *, nc):
    u_i = lax.broadcasted_iota(jnp.int32, (CHUNK, CHUNK), 0)
    t_i = lax.broadcasted_iota(jnp.int32, (CHUNK, CHUNK), 1)
    prefix = (u_i <= t_i).astype(F32)
    suffix = (u_i >= t_i).astype(F32)
    grp = min(GATE_GROUP, nc)
    rows = grp * SUBLANES
    lane = lax.broadcasted_iota(jnp.int32, (rows, LANES), 1)
    is_fwd = lax.broadcasted_iota(jnp.int32, (rows, LANES), 0) % SUBLANES < N_HEADS
    h = N_HEADS

    def running_max(a):
        fwd, bwd = a, a
        sh = 1
        while sh < CHUNK:
            fwd = jnp.where(lane >= sh, jnp.maximum(fwd, pltpu.roll(fwd, sh, axis=1)), fwd)
            bwd = jnp.where(lane < CHUNK - sh, jnp.maximum(bwd, pltpu.roll(bwd, CHUNK - sh, axis=1)), bwd)
            sh *= 2
        return jnp.where(is_fwd, fwd, bwd)

    def body(i, carry):
        c0 = i * grp
        r0 = pl.multiple_of(c0 * CHUNK, CHUNK)
        pre = g_ref[0, pl.ds(r0, grp * CHUNK), :] + gb_ref[...]
        pre_t = jnp.concatenate(
            [pre[j * CHUNK:(j + 1) * CHUNK].T[:N_GATE_COLS, :] for j in range(grp)], axis=0)
        lf_t = -(jnp.maximum(-pre_t, 0.0) + jnp.log1p(jnp.exp(-jnp.abs(pre_t))))
        cum_f = jnp.dot(lf_t, prefix, precision=HIGHEST, preferred_element_type=F32)
        cum_b = jnp.dot(lf_t, suffix, precision=HIGHEST, preferred_element_type=F32)
        pick = lambda t, j, q: t[j * N_GATE_COLS + q * h:j * N_GATE_COLS + (q + 1) * h]
        b8 = jnp.concatenate([x for j in range(grp) for x in (pick(cum_f, j, 1), pick(cum_b, j, 3))], axis=0)
        li8 = jnp.concatenate([x for j in range(grp) for x in (pick(pre_t, j, 0), pick(pre_t, j, 2))], axis=0)
        a8 = li8 - b8
        as_chunks = lambda t: t.reshape(grp, SUBLANES, LANES)
        rv_ref[0, pl.ds(c0, grp), GATE_A] = as_chunks(a8)
        rv_ref[0, pl.ds(c0, grp), GATE_CM] = as_chunks(running_max(a8))
        rv_ref[0, pl.ds(c0, grp), GATE_B] = as_chunks(b8)
        return carry

    lax.fori_loop(0, nc // grp, body, 0)


def _gates(y3d, gate_b_row):
    b, n, _ = y3d.shape
    nc = n // CHUNK
    return pl.pallas_call(
        functools.partial(_gates_kernel, nc=nc),
        out_shape=jax.ShapeDtypeStruct((b, nc, 3, SUBLANES, LANES), F32),
        grid=(b,),
        in_specs=[pl.BlockSpec((1, n, LANES), lambda i: (i, 0, COL_G // LANES)),
                  pl.BlockSpec((1, LANES), lambda i: (0, 0))],
        out_specs=pl.BlockSpec((1, nc, 3, SUBLANES, LANES), lambda i: (i, 0, 0, 0, 0)),
        compiler_params=_cparams(("arbitrary",)),
        name="gates",
    )(y3d, gate_b_row)


def _mlstm_kernel(q_ref, k_ref, v_ref, o_ref, rv_ref, c0_ref, m0_ref, hg_ref,
                  out_ref, cfin_ref, mfin_ref, hf_ref, hb_ref, cst_ref, mst_ref, *, nc, heads):
    cst_ref[...] = c0_ref[0]
    mst_ref[...] = m0_ref[0]
    s_i = lax.broadcasted_iota(jnp.int32, (CHUNK, CHUNK), 0)
    t_i = lax.broadcasted_iota(jnp.int32, (CHUNK, CHUNK), 1)
    ones_row = (s_i[:STATE_ROWS - D_HEAD] == 0).astype(F32)

    def one_chunk(c, d, mask_t, last, h_ref, hh):
        r0 = pl.multiple_of(c * CHUNK, CHUNK)
        lanes = slice(hh * D_HEAD, (hh + 1) * D_HEAD)
        g = d * N_HEADS + pl.program_id(1) * heads + hh
        gate_row = lambda tbl: rv_ref[0, pl.ds(c, 1), tbl, pl.ds(g, 1), :].reshape(1, LANES)
        a_row, cm_row, b_row = gate_row(GATE_A), gate_row(GATE_CM), gate_row(GATE_B)
        q_t = q_ref[0, pl.ds(r0, CHUNK), lanes].T
        k = k_ref[0, pl.ds(r0, CHUNK), lanes]
        v_t_aug = jnp.concatenate([v_ref[0, pl.ds(r0, CHUNK), lanes].T, ones_row], axis=0)
        m_prev = mst_ref[hh, d, 0:1, :]
        m_row = jnp.maximum(m_prev, cm_row)
        a_bc = jnp.broadcast_to(a_row, (CHUNK, CHUNK)).T
        d_t = jnp.where(mask_t, jnp.exp(a_bc - m_row), 0.0)
        s_t = jnp.dot(k, q_t, preferred_element_type=F32) * d_t
        cs = cst_ref[hh, d]
        rhs = jnp.concatenate(
            [s_t.astype(BF16), (q_t.astype(F32) * jnp.exp(m_prev - m_row)).astype(BF16)], axis=0)
        lhs = jnp.concatenate([v_t_aug.astype(BF16), cs.astype(BF16)], axis=1)
        h_t = jnp.dot(lhs, rhs, preferred_element_type=F32)
        den = h_t[D_HEAD:D_HEAD + 1, :]
        floor = jnp.exp(-b_row - m_row)
        h_ref[hh, pl.ds(c, 1)] = (h_t[:D_HEAD] / jnp.maximum(jnp.abs(den), floor))[None]
        m_last = m_row[:, last:last + 1]
        wv_t = (v_t_aug * jnp.exp(a_row - m_last)).astype(BF16)
        cst_ref[hh, d] = jnp.exp(m_prev - m_last) * cs + jnp.dot(wv_t, k, preferred_element_type=F32)
        mst_ref[hh, d] = jnp.broadcast_to(b_row[:, last:last + 1] + m_last, (SUBLANES, LANES))

    def body(c, carry):
        for hh in range(heads):
            one_chunk(c, 0, s_i <= t_i, CHUNK - 1, hf_ref, hh)
            one_chunk(nc - 1 - c, 1, s_i >= t_i, 0, hb_ref, hh)
        return carry

    lax.fori_loop(0, nc, body, 0, unroll=min(16 // heads, nc))
    cfin_ref[0] = cst_ref[...]
    mfin_ref[0] = mst_ref[...]

    def readout(c, carry):
        r0 = pl.multiple_of(c * CHUNK, CHUNK)
        for hh in range(heads):
            lanes = slice(hh * D_HEAD, (hh + 1) * D_HEAD)
            h_t = hf_ref[hh, pl.ds(c, 1)][0] + hb_ref[hh, pl.ds(c, 1)][0]
            hc = h_t - jnp.mean(h_t, axis=0, keepdims=True)
            hn = (hc * lax.rsqrt(jnp.mean(hc * hc, axis=0, keepdims=True) + LN_EPS)).T
            gate = jax.nn.sigmoid(o_ref[0, pl.ds(r0, CHUNK), lanes])
            out_ref[0, pl.ds(r0, CHUNK), lanes] = (hn * hg_ref[:, lanes] * gate).astype(out_ref.dtype)
        return carry

    lax.fori_loop(0, nc, readout, 0, unroll=min(16, nc))


def _mlstm(qk, y3d, rowv, c0, m0, head_g):
    b, n, _ = qk.shape
    nc = n // CHUNK
    heads = N_HEADS if nc <= 4 else 1
    w = heads * D_HEAD
    v_blk = COL_V // w
    o_blk = COL_O // w
    tok = lambda off: pl.BlockSpec((1, n, w), lambda i, h: (i, 0, off + h))
    state_c = pl.BlockSpec((1, heads, 2, STATE_ROWS, D_HEAD), lambda i, h: (i, h, 0, 0, 0))
    state_m = pl.BlockSpec((1, heads, 2, SUBLANES, LANES), lambda i, h: (i, h, 0, 0, 0))
    return pl.pallas_call(
        functools.partial(_mlstm_kernel, nc=nc, heads=heads),
        out_shape=(jax.ShapeDtypeStruct((b, n, D_MLSTM), BF16),
                   jax.ShapeDtypeStruct((b, N_HEADS, 2, STATE_ROWS, D_HEAD), F32),
                   jax.ShapeDtypeStruct((b, N_HEADS, 2, SUBLANES, LANES), F32)),
        grid=(b, N_HEADS // heads),
        in_specs=[tok(0), tok(D_MLSTM // w), tok(v_blk), tok(o_blk),
                  pl.BlockSpec((1, nc, 3, SUBLANES, LANES), lambda i, h: (i, 0, 0, 0, 0)),
                  state_c, state_m,
                  pl.BlockSpec((1, w), lambda i, h: (0, h))],
        out_specs=(pl.BlockSpec((1, n, w), lambda i, h: (i, 0, h)), state_c, state_m),
        scratch_shapes=[pltpu.VMEM((heads, nc, D_HEAD, CHUNK), F32), pltpu.VMEM((heads, nc, D_HEAD, CHUNK), F32),
                        pltpu.VMEM((heads, 2, STATE_ROWS, D_HEAD), F32),
                        pltpu.VMEM((heads, 2, SUBLANES, LANES), F32)],
        compiler_params=_cparams(("arbitrary", "arbitrary")),
        name="mlstm",
    )(qk, qk, y3d, y3d, rowv, c0, m0, head_g)


def _dft_tables():
    r = DFT_R
    idx = np.arange(r)
    cg = np.cos(2 * np.pi * np.outer(idx, idx) / FOURIER_GROUP)
    sg = np.sin(2 * np.pi * np.outer(idx, idx) / FOURIER_GROUP)
    n_grp = D_FOURIER // FOURIER_GROUP
    bdc = np.kron(np.eye(n_grp), cg) / 8.0
    bds = np.kron(np.eye(n_grp), sg) / 8.0
    k1 = idx[:, None]
    a = idx[None, :]
    m1 = np.zeros((r, 2 * r, 2 * r))
    for b2 in range(r):
        th = 2 * np.pi * k1 * (r * a + b2) / (r * r)
        ec, es = np.cos(th), np.sin(th)
        m1[b2] = np.block([[ec, -es], [es, ec]]) / 8.0
    th2 = 2 * np.pi * np.outer(idx, idx) / r
    m2 = np.concatenate([np.cos(th2), -np.sin(th2)], axis=1) / 8.0
    as_bf16 = lambda t: jnp.asarray(t, dtype=F32).astype(BF16)
    return as_bf16(bdc), as_bf16(bds), as_bf16(m1), as_bf16(m2)


FOUR_BB = 8


N_SLABS = D_FOURIER // LANES


def _to_slabs(slab_ref, val):
    for s in range(N_SLABS):
        slab_ref[s] = val[:, s * LANES:(s + 1) * LANES]


def _strided_rows(slab_ref, start):
    return jnp.concatenate(
        [slab_ref[s, pl.ds(start, DFT_R, stride=FOUR_BB), :] for s in range(N_SLABS)], axis=1)


def _four_kernel(f_ref, bdc_ref, bds_ref, m1_ref, m2_ref, o_ref, z_ref, ps_ref, qs_ref, zs_ref, os_ref):
    r = DFT_R

    def stage1(blk, carry):
        b0 = pl.multiple_of(blk * FOUR_BB, FOUR_BB)
        fb = f_ref[0, :, pl.ds(b0, FOUR_BB), :].reshape(r * FOUR_BB, D_FOURIER).astype(BF16)
        _to_slabs(ps_ref, jnp.dot(fb, bdc_ref[...], preferred_element_type=F32))
        _to_slabs(qs_ref, jnp.dot(fb, bds_ref[...], preferred_element_type=F32))
        for j in range(FOUR_BB):
            x2 = jnp.concatenate([_strided_rows(ps_ref, j), _strided_rows(qs_ref, j)], axis=0).astype(BF16)
            zz = jnp.dot(m1_ref[b0 + j], x2, preferred_element_type=F32)
            z_ref[b0 + j] = _pack_pair(zz[:r], zz[r:])
        return carry

    lax.fori_loop(0, r // FOUR_BB, stage1, 0)

    def stage2(blk, carry):
        k0 = pl.multiple_of(blk * FOUR_BB, FOUR_BB)
        _to_slabs(zs_ref, z_ref[:, pl.ds(k0, FOUR_BB), :].reshape(r * FOUR_BB, D_FOURIER))
        for j in range(FOUR_BB):
            zc, zs = _unpack_pair_bf16(_strided_rows(zs_ref, j))
            o = jnp.dot(m2_ref[...], jnp.concatenate([zc, zs], axis=0), preferred_element_type=F32)
            for s in range(N_SLABS):
                os_ref[s, pl.ds(j, r, stride=FOUR_BB), :] = o[:, s * LANES:(s + 1) * LANES]
        for s in range(N_SLABS):
            o_ref[0, :, pl.ds(k0, FOUR_BB), s * LANES:(s + 1) * LANES] = os_ref[s].reshape(r, FOUR_BB, LANES)
        return carry

    lax.fori_loop(0, r // FOUR_BB, stage2, 0)


def _fourier(y4, tables):
    bdc, bds, m1, m2 = tables
    b = y4.shape[0]
    r = DFT_R
    slab = lambda dt: pltpu.VMEM((N_SLABS, r * FOUR_BB, LANES), dt)
    const = lambda *shape: pl.BlockSpec(shape, lambda i: (0,) * len(shape))
    return pl.pallas_call(
        _four_kernel,
        out_shape=jax.ShapeDtypeStruct((b, r, r, D_FOURIER), F32),
        grid=(b,),
        in_specs=[pl.BlockSpec((1, r, r, D_FOURIER), lambda i: (i, 0, 0, 0)),
                  const(D_FOURIER, D_FOURIER), const(D_FOURIER, D_FOURIER),
                  const(r, 2 * r, 2 * r), const(r, 2 * r)],
        out_specs=pl.BlockSpec((1, r, r, D_FOURIER), lambda i: (i, 0, 0, 0)),
        scratch_shapes=[pltpu.VMEM((r, r, D_FOURIER), jnp.int32), slab(F32), slab(F32), slab(jnp.int32), slab(F32)],
        compiler_params=_cparams(("arbitrary",), vmem=FOUR_VMEM_LIMIT),
        name="fourier",
    )(y4, bdc, bds, m1, m2)


OUTPROJ_PARTS = 4


def _outproj_kernel(x_ref, mr_ref, fo_ref, wm_ref, wf_ref, g1_ref, lg_ref, lb_ref,
                    sh_ref, sc_ref, wr_ref, x1_ref, xm_ref, s_ref):
    part = x_ref.shape[0] // OUTPROJ_PARTS
    rows = lambda h: slice(h * part, (h + 1) * part)
    mixer = lambda r: (jnp.dot(mr_ref[r, :], wm_ref[...], preferred_element_type=F32)
                       + jnp.dot(fo_ref[r, :].astype(BF16), wf_ref[...], preferred_element_type=F32))
    mix_next = mixer(rows(0))
    for h in range(OUTPROJ_PARTS):
        r = rows(h)
        mix = mix_next
        if h + 1 < OUTPROJ_PARTS:
            mix_next = mixer(rows(h + 1))
        x1 = _layer_norm(DEEPNORM_ALPHA * x_ref[r, :] + g1_ref[0] * mix) * lg_ref[...] + lb_ref[...]
        x1_ref[r, :] = x1
        xm = (_layer_norm(x1) * (1.0 + sc_ref[0]) + sh_ref[0]).astype(BF16)
        xm_ref[r, :] = _pack_rows(xm)
        logits = jnp.dot(xm, wr_ref[...], preferred_element_type=F32)
        s_ref[r, :] = jax.nn.sigmoid(logits)


def _outproj(x2d, mread, four, w_m, w_f, gate1, ln_g, ln_b, shift2, scale2, w_router, n_per_batch):
    t, d = x2d.shape
    per = n_per_batch // TM
    tokd = lambda w: pl.BlockSpec((TM, w), lambda i: (i, 0))
    full = lambda r, c: pl.BlockSpec((r, c), lambda i: (0, 0))
    mod = pl.BlockSpec((1, 1, d), lambda i: (i // per, 0, 0))
    return pl.pallas_call(
        _outproj_kernel,
        out_shape=(jax.ShapeDtypeStruct((t, d), F32),
                   jax.ShapeDtypeStruct((t, PACK_W), jnp.int32),
                   jax.ShapeDtypeStruct((t, LANES), F32)),
        grid=(t // TM,),
        in_specs=[tokd(d), tokd(D_MLSTM), tokd(D_FOURIER), full(D_MLSTM, d), full(D_FOURIER, d),
                  mod, full(1, d), full(1, d), mod, mod, full(d, LANES)],
        out_specs=(tokd(d), tokd(PACK_W), tokd(LANES)),
        compiler_params=_cparams(("arbitrary",)),
        name="outproj",
    )(x2d, mread, four, w_m, w_f, gate1, ln_g, ln_b, shift2, scale2, w_router)


def _route_kernel(s_ref, b_ref, idx_ref, gw_ref, rank_ref, cnt_ref, carry_ref):
    @pl.when(pl.program_id(0) == 0)
    def _():
        carry_ref[...] = jnp.zeros_like(carry_ref)

    tm = s_ref.shape[0]
    reps = tm // LANES
    s_t = s_ref[...].T[:N_EXPERTS, :]
    e_i = lax.broadcasted_iota(jnp.int32, (N_EXPERTS, tm), 0).astype(F32)
    neg_inf = jnp.float32(-jnp.inf)
    sb = s_t + jnp.tile(b_ref[...], (1, reps))
    picks, sels = [], []
    taken = jnp.zeros((N_EXPERTS, tm), F32)
    for _ in range(TOP_K):
        best = jnp.max(sb, axis=0, keepdims=True)
        pick = jnp.min(jnp.where(sb == best, e_i, float(N_EXPERTS)), axis=0, keepdims=True)
        onehot = e_i == pick
        sels.append(jnp.sum(jnp.where(onehot, s_t, 0.0), axis=0, keepdims=True))
        picks.append(pick)
        sb = jnp.where(onehot, neg_inf, sb)
        taken = taken + onehot.astype(F32)
    u_i = lax.broadcasted_iota(jnp.int32, (tm, tm), 0)
    t_i = lax.broadcasted_iota(jnp.int32, (tm, tm), 1)
    before = (u_i < t_i).astype(BF16)
    taken_bf = taken.astype(BF16)
    rank_all = jnp.tile(carry_ref[...], (1, reps)) + jnp.dot(taken_bf, before, preferred_element_type=F32)
    carry_ref[...] = carry_ref[...] + jnp.dot(taken_bf, jnp.ones((tm, LANES), BF16),
                                              preferred_element_type=F32)
    cnt_ref[...] = carry_ref[...]
    total = sels[0]
    for v in sels[1:]:
        total = total + v
    ranks = [jnp.sum(jnp.where(e_i == picks[j], rank_all, 0.0), axis=0, keepdims=True) for j in range(TOP_K)]
    idx_ref[...] = jnp.concatenate(picks, axis=0).astype(jnp.int32)
    rank_ref[...] = jnp.concatenate(ranks, axis=0).astype(jnp.int32)
    gw_t = jnp.concatenate([ROUTED_SCALE * v / total for v in sels]
                           + [jnp.zeros((LANES - TOP_K, tm), F32)], axis=0)
    gw_ref[...] = gw_t.T[:, :TOP_K]


def _route(scores, bias_rep):
    t = scores.shape[0]
    tok = lambda w: pl.BlockSpec((TM_ROUTE, w), lambda i: (i, 0))
    slot_major = pl.BlockSpec((TOP_K, TM_ROUTE), lambda i: (0, i))
    per_expert = pl.BlockSpec((N_EXPERTS, LANES), lambda i: (0, 0))
    return pl.pallas_call(
        _route_kernel,
        out_shape=(jax.ShapeDtypeStruct((TOP_K, t), jnp.int32),
                   jax.ShapeDtypeStruct((t, TOP_K), F32),
                   jax.ShapeDtypeStruct((TOP_K, t), jnp.int32),
                   jax.ShapeDtypeStruct((N_EXPERTS, LANES), F32)),
        grid=(t // TM_ROUTE,),
        in_specs=[tok(LANES), per_expert],
        out_specs=(slot_major, tok(TOP_K), slot_major, per_expert),
        scratch_shapes=[pltpu.VMEM((N_EXPERTS, LANES), F32)],
        compiler_params=_cparams(("arbitrary",)),
        name="route",
    )(scores, bias_rep)


TN_DEST = 2048


def _dest_kernel(ps_ref, idx_ref, rank_ref, dest_ref):
    idx = idx_ref[...]
    start = jnp.zeros(idx.shape, jnp.int32)
    for e in range(N_EXPERTS):
        start = jnp.where(idx == e, ps_ref[e], start)
    dest_ref[...] = start + rank_ref[...]


def _dest(pad_start, idx_t, rank_t):
    t = idx_t.shape[1]
    blk = lambda: pl.BlockSpec((TOP_K, TN_DEST), lambda i, ps: (0, i))
    grid_spec = pltpu.PrefetchScalarGridSpec(
        num_scalar_prefetch=1, grid=(t // TN_DEST,), in_specs=[blk(), blk()], out_specs=blk())
    return pl.pallas_call(
        _dest_kernel,
        out_shape=jax.ShapeDtypeStruct((TOP_K, t), jnp.int32),
        grid_spec=grid_spec,
        compiler_params=_cparams(("arbitrary",)),
        name="dest",
    )(pad_start, idx_t, rank_t)


def _sc_worker_id(info):
    return lax.axis_index("s") * info.num_cores + lax.axis_index("c")


def _sc_dispatch(xm, dest3, n_rows):
    t, w = xm.shape
    info = plsc.get_sparse_core_info()
    n_workers = info.num_cores * info.num_subcores
    per_worker = t // SC_ROWS // n_workers
    mesh = plsc.VectorSubcoreMesh(core_axis_name="c", subcore_axis_name="s")

    @functools.partial(
        pl.kernel, mesh=mesh,
        out_type=jax.ShapeDtypeStruct((n_rows, w), xm.dtype),
        scratch_types=[pltpu.VMEM((TOP_K, SC_ROWS), jnp.int32), pltpu.VMEM((SC_ROWS, w), xm.dtype)],
        name="sc_dispatch",
    )
    def run(xm_hbm, dest_hbm, xs_hbm, idx_v, rows_v):
        first = _sc_worker_id(info) * per_worker

        @pl.loop(0, per_worker)
        def _(ci):
            chunk = first + ci
            pltpu.sync_copy(dest_hbm.at[chunk], idx_v)
            pltpu.sync_copy(xm_hbm.at[pl.ds(chunk * SC_ROWS, SC_ROWS)], rows_v)
            for j in range(TOP_K):
                pltpu.sync_copy(rows_v, xs_hbm.at[idx_v.at[j]])

    return run(xm, dest3)


def _sc_gather(table, idx2):
    n_chunks, rows = idx2.shape
    w = table.shape[1]
    info = plsc.get_sparse_core_info()
    n_workers = info.num_cores * info.num_subcores
    per_worker = n_chunks // n_workers
    assert per_worker % 2 == 0
    mesh = plsc.VectorSubcoreMesh(core_axis_name="c", subcore_axis_name="s")

    @functools.partial(
        pl.kernel, mesh=mesh,
        out_type=jax.ShapeDtypeStruct((n_chunks * rows, w), table.dtype),
        scratch_types=[pltpu.VMEM((2, rows), jnp.int32), pltpu.VMEM((2, rows, w), table.dtype),
                       pltpu.SemaphoreType.DMA, pltpu.SemaphoreType.DMA],
        name="sc_gather",
    )
    def run(table_hbm, idx_hbm, out_hbm, idx_v, rows_v, sem0, sem1):
        first = _sc_worker_id(info) * per_worker
        sems = (sem0, sem1)

        def gather(b):
            return pltpu.make_async_copy(table_hbm.at[idx_v.at[b]], rows_v.at[b], sems[b])

        def start(chunk, b):
            pltpu.sync_copy(idx_hbm.at[chunk], idx_v.at[b])
            gather(b).start()

        def finish(chunk, b):
            gather(b).wait()
            pltpu.sync_copy(rows_v.at[b], out_hbm.at[pl.ds(chunk * rows, rows)])

        start(first, 0)

        @pl.loop(0, per_worker, step=2)
        def _(ci):
            chunk = first + ci
            start(chunk + 1, 1)
            finish(chunk, 0)

            @pl.when(ci + 2 < per_worker)
            def _():
                start(chunk + 2, 0)

            finish(chunk + 1, 1)

    return run(table, idx2)


def _expert_kernel(first_ref, count_ref, used_ref,
                   xs_hbm, wgu_ref, wd_ref, y_hbm, wgu_bf, wd_bf, xbuf, ybuf, in_sem, out_sem):
    e = pl.program_id(0)
    n_used = used_ref[0]
    first = first_ref[e]
    n_blocks = count_ref[e]

    def rows(g):
        return pl.ds(pl.multiple_of(g * ROW_BLK, ROW_BLK), ROW_BLK)

    def read(g):
        slot = g % EXP_IN_SLOTS
        return pltpu.make_async_copy(xs_hbm.at[rows(g), :], xbuf.at[slot], in_sem.at[slot])

    def write(g):
        slot = g % EXP_OUT_SLOTS
        return pltpu.make_async_copy(ybuf.at[slot], y_hbm.at[rows(g), :], out_sem.at[slot])

    @pl.when(e == 0)
    def _():
        for g0 in range(EXP_AHEAD):
            @pl.when(g0 < n_used)
            def _():
                read(g0).start()

    @pl.when(n_blocks > 0)
    def _():
        wgu_bf[...] = wgu_ref[0].astype(BF16)
        wd_bf[...] = wd_ref[0].astype(BF16)

    def acquire(g):
        @pl.when(g + EXP_AHEAD < n_used)
        def _():
            read(g + EXP_AHEAD).start()

        read(g).wait()

        @pl.when(g >= EXP_OUT_SLOTS)
        def _():
            write(g - EXP_OUT_SLOTS).wait()

    def gate_up(g):
        return jnp.dot(_unpack_bf16(xbuf[g % EXP_IN_SLOTS]), wgu_bf[...], preferred_element_type=F32)

    def down(g, gu):
        act = _silu(gu[:, :D_EXPERT]) * gu[:, D_EXPERT:]
        yv = jnp.dot(act.astype(BF16), wd_bf[...], preferred_element_type=F32)
        words = pltpu.pack_elementwise([yv[:, :PACK_W], yv[:, PACK_W:]], packed_dtype=BF16)
        ybuf[g % EXP_OUT_SLOTS] = lax.bitcast_convert_type(words, jnp.int32)

    def handle(g, n):
        for i in range(n):
            acquire(g + i)
        gu = gate_up(g)
        for i in range(n):
            gu_next = gate_up(g + i + 1) if i + 1 < n else None
            down(g + i, gu)
            write(g + i).start()
            gu = gu_next

    def group(k, carry):
        handle(first + EXP_GROUP * k, EXP_GROUP)
        return carry

    lax.fori_loop(0, n_blocks // EXP_GROUP, group, 0)

    n = EXP_GROUP // 2
    while n >= 1:
        @pl.when(n_blocks % (2 * n) >= n)
        def _(n=n):
            handle(first + n_blocks - n_blocks % (2 * n), n)
        n //= 2

    @pl.when(e == pl.num_programs(0) - 1)
    def _():
        for back in range(EXP_OUT_SLOTS, 0, -1):
            @pl.when(n_used >= back)
            def _():
                write(n_used - back).wait()


def _experts(first_block, n_blocks, n_used, xs, w_gu, w_down):
    n_rows, w = xs.shape
    d = D_MODEL
    grid_spec = pltpu.PrefetchScalarGridSpec(
        num_scalar_prefetch=3,
        grid=(N_EXPERTS,),
        in_specs=[pl.BlockSpec(memory_space=pl.ANY),
                  pl.BlockSpec((1, d, 2 * D_EXPERT), lambda e, *_: (e, 0, 0)),
                  pl.BlockSpec((1, D_EXPERT, d), lambda e, *_: (e, 0, 0))],
        out_specs=pl.BlockSpec(memory_space=pl.ANY),
        scratch_shapes=[pltpu.VMEM((d, 2 * D_EXPERT), BF16), pltpu.VMEM((D_EXPERT, d), BF16),
                        pltpu.VMEM((EXP_IN_SLOTS, ROW_BLK, w), jnp.int32),
                        pltpu.VMEM((EXP_OUT_SLOTS, ROW_BLK, w), jnp.int32),
                        pltpu.SemaphoreType.DMA((EXP_IN_SLOTS,)),
                        pltpu.SemaphoreType.DMA((EXP_OUT_SLOTS,))],
    )
    return pl.pallas_call(
        _expert_kernel,
        out_shape=jax.ShapeDtypeStruct((n_rows, w), jnp.int32),
        grid_spec=grid_spec,
        compiler_params=_cparams(("arbitrary",)),
        name="experts",
    )(first_block, n_blocks, n_used, xs, w_gu, w_down)


TM_COMBINE = 512


def _combine_kernel(*refs):
    yg_refs = refs[:TOP_K]
    gw_ref, x1_ref, xm_ref, g2_ref, wsgu_ref, wsd_ref, lg_ref, lb_ref, out_ref = refs[TOP_K:]
    gu = jnp.dot(_unpack_bf16(xm_ref[...]), wsgu_ref[...], preferred_element_type=F32)
    act = _silu(gu[:, :D_EXPERT]) * gu[:, D_EXPERT:]
    acc = jnp.dot(act.astype(BF16), wsd_ref[...], preferred_element_type=F32)
    gw = gw_ref[...]
    for j in range(TOP_K):
        acc = acc + gw[:, j:j + 1] * _unpack_rows(yg_refs[j][...])
    z = DEEPNORM_ALPHA * x1_ref[...] + g2_ref[0] * acc
    out_ref[...] = _layer_norm(z) * lg_ref[...] + lb_ref[...]


def _combine(yg, gw, x1, xm, gate2, w_sgu, w_sd, ln_g, ln_b, n_per_batch):
    t, d = x1.shape
    tm = TM_COMBINE
    per = n_per_batch // tm
    steps = t // tm
    tok = lambda w: pl.BlockSpec((tm, w), lambda i: (i, 0))
    full = lambda r, c: pl.BlockSpec((r, c), lambda i: (0, 0))
    slot = lambda j: pl.BlockSpec((tm, PACK_W), lambda i: (j * steps + i, 0))
    return pl.pallas_call(
        _combine_kernel,
        out_shape=jax.ShapeDtypeStruct((t, d), F32),
        grid=(steps,),
        in_specs=[slot(j) for j in range(TOP_K)]
        + [tok(TOP_K), tok(d), tok(PACK_W), pl.BlockSpec((1, 1, d), lambda i: (i // per, 0, 0)),
           full(d, 2 * D_EXPERT), full(D_EXPERT, d), full(1, d), full(1, d)],
        out_specs=tok(d),
        compiler_params=_cparams(("arbitrary",)),
        name="combine",
    )(*([yg] * TOP_K), gw, x1, xm, gate2, w_sgu, w_sd, ln_g, ln_b)


def _mixer_heads(stream, shift, scale, w_in_k, conv_w9, conv_b, gate_b_row, head_g, c0, m0, grid2d):
    b, n, d = stream.shape
    y, f_in = _inproj(stream.reshape(b * n, d), shift, scale, w_in_k, n)
    y = y.reshape(b, n, PROJ_W)
    qk = _conv(y, conv_w9, conv_b, grid2d)
    mread, c_fin, m_fin = _mlstm(qk, y, _gates(y, gate_b_row), c0, m0, head_g)
    return f_in, mread, c_fin, m_fin


def _moe(xm, x1, scores, bias_row, gate2, w_gu, w_down, w_sgu, w_sd, ln_g, ln_b, n_per_batch):
    t = xm.shape[0]
    n_assign = t * TOP_K
    n_rows = n_assign + N_EXPERTS * ROW_BLK
    idx_t, gw, rank_t, counts = _route(scores, bias_row)
    sizes = counts[:, 0].astype(jnp.int32)
    padded = (sizes + ROW_BLK - 1) // ROW_BLK * ROW_BLK
    pad_end = jnp.cumsum(padded)
    pad_start = (pad_end - padded).astype(jnp.int32)
    dest_t = _dest(pad_start, idx_t, rank_t)
    dest3 = dest_t.reshape(TOP_K, t // SC_ROWS, SC_ROWS).transpose(1, 0, 2)

    xs = _sc_dispatch(xm, dest3, n_rows)
    yb = _experts(pad_start // ROW_BLK, (padded // ROW_BLK).astype(jnp.int32),
                  (pad_end[-1:] // ROW_BLK).astype(jnp.int32), xs, w_gu, w_down)
    yg = _sc_gather(yb, dest_t.reshape(n_assign // SC_GATHER_ROWS, SC_GATHER_ROWS))
    return _combine(yg, gw, x1, xm, gate2, w_sgu, w_sd, ln_g, ln_b, n_per_batch)


def kernel(x, c, ctx, c_ctx, w_ada, b_ada, w_in, conv_w, conv_b, gate_b, head_g, w_out, ln1_g, ln1_b,
           w_router, router_bias, w_expert_gu, w_expert_down, w_shared_gu, w_shared_down, ln2_g, ln2_b):
    bsz, n, d = x.shape
    depth = w_ada.shape[0]
    assert depth == 1
    t = bsz * n
    tables = _dft_tables()
    for l in range(depth):
        cvec = jnp.zeros((SUBLANES, d), F32).at[:bsz].set(c).at[bsz].set(c_ctx)
        mod = _ada(cvec, w_ada[l], b_ada[l][None, :])
        mods = mod.reshape(SUBLANES, 6, 1, d)
        mod_x = [mods[:bsz, j] for j in range(6)]
        mod_c = [mods[bsz:bsz + 1, j] for j in range(2)]

        wl = w_in[l]
        g0 = 4 * D_MLSTM
        w_in_k = jnp.concatenate(
            [wl[:, :g0 + N_GATE_COLS], jnp.zeros((d, LANES - N_GATE_COLS), F32),
             wl[:, g0 + N_GATE_COLS:]], axis=1).astype(BF16)
        conv_w9 = conv_w[l].reshape(9, 2 * D_MLSTM)
        conv_b_row = conv_b[l][None, :]
        gate_b_row = jnp.zeros((1, LANES), F32).at[0, :N_GATE_COLS].set(gate_b[l].reshape(-1))
        head_g_row = head_g[l][None, :]

        c0 = jnp.zeros((bsz, N_HEADS, 2, STATE_ROWS, D_HEAD), F32)
        m0 = jnp.zeros((bsz, N_HEADS, 2, SUBLANES, LANES), F32)
        _, _, c_ctx_fin, m_ctx_fin = _mixer_heads(ctx, mod_c[0], mod_c[1], w_in_k, conv_w9, conv_b_row,
                                                  gate_b_row, head_g_row, c0, m0, grid2d=False)
        f_in, mread, _, _ = _mixer_heads(x, mod_x[0], mod_x[1], w_in_k, conv_w9, conv_b_row,
                                      gate_b_row, head_g_row, c_ctx_fin, m_ctx_fin, grid2d=True)

        four = _fourier(f_in.reshape(bsz, DFT_R, DFT_R, D_FOURIER), tables).reshape(t, D_FOURIER)

        w_o = w_out[l].astype(BF16)
        w_r = jnp.zeros((d, LANES), F32).at[:, :N_EXPERTS].set(w_router[l]).astype(BF16)
        x2d = x.reshape(t, d)
        x1, xm, scores = _outproj(x2d, mread.reshape(t, D_MLSTM), four, w_o[:D_MLSTM], w_o[D_MLSTM:],
                                  mod_x[2], ln1_g[l][None, :], ln1_b[l][None, :], mod_x[3], mod_x[4],
                                  w_r, n)

        bias_rep = jnp.broadcast_to(router_bias[l][:, None], (N_EXPERTS, LANES))
        x = _moe(xm, x1, scores, bias_rep, mod_x[5], w_expert_gu[l], w_expert_down[l],
                 w_shared_gu[l].astype(BF16), w_shared_down[l].astype(BF16),
                 ln2_g[l][None, :], ln2_b[l][None, :], n).reshape(bsz, n, d)
    return x
```
